```python
import math
import jax, jax.numpy as jnp
from jax import lax
import numpy as np

D_MODEL = 1024
BATCH = 8
SEQ = 4096
DEPTH = 2

CHUNK = 64
N_HEADS = 8
N_KV_HEADS = 2
HEAD_DIM = 64
Q_PER_KV = N_HEADS // N_KV_HEADS
WINDOW = 128
WIN_CHUNKS = WINDOW // CHUNK
ATT_W = N_HEADS * HEAD_DIM
KV_W = N_KV_HEADS * HEAD_DIM
SSM_W = 512
SSM_GROUP = 16
SSM_GROUPS = SSM_W // SSM_GROUP
SSM_STATE = 64
POOL_W = 512
POOL_WINDOWS = (2, 4, 8, 16)
POOL_GROUPS = len(POOL_WINDOWS)
POOL_GW = POOL_W // POOL_GROUPS
N_BRANCH = 3
SPLIT_SIZES = (ATT_W, KV_W, KV_W, SSM_W, POOL_W, ATT_W, SSM_W, POOL_W, N_BRANCH * D_MODEL)
IN_W = sum(SPLIT_SIZES)
EPS = 1e-6
NEG_INF = -1e30

kernel_name = "hybrid_gated_swa_s5_pool_adaln"


def rmsnorm(x, g):
    xf = x.astype(jnp.float32)
    y = xf * lax.rsqrt(jnp.mean(xf * xf, axis=-1, keepdims=True) + EPS)
    return (y * g.astype(jnp.float32)).astype(x.dtype)


def alibi_slopes(n):
    return jnp.asarray([2.0 ** (-8.0 * (h + 1) / n) for h in range(n)], dtype=jnp.float32)


def window_attention(q, k, v, sinks):
    b, l = q.shape[:2]
    nc = l // CHUNK
    pad = WIN_CHUNKS * CHUNK
    nk = (WIN_CHUNKS + 1) * CHUNK
    kp = jnp.pad(k, ((0, 0), (pad, 0), (0, 0), (0, 0))).reshape(b, nc + WIN_CHUNKS, CHUNK, N_KV_HEADS, HEAD_DIM)
    vp = jnp.pad(v, ((0, 0), (pad, 0), (0, 0), (0, 0))).reshape(b, nc + WIN_CHUNKS, CHUNK, N_KV_HEADS, HEAD_DIM)
    kb = jnp.concatenate([kp[:, j:j + nc] for j in range(WIN_CHUNKS + 1)], axis=2)
    vb = jnp.concatenate([vp[:, j:j + nc] for j in range(WIN_CHUNKS + 1)], axis=2)
    qb = q.reshape(b, nc, CHUNK, N_KV_HEADS, Q_PER_KV, HEAD_DIM)
    s = jnp.einsum('bcqkgd,bcskd->bckgqs', qb, kb).astype(jnp.float32) * (1.0 / math.sqrt(HEAD_DIM))
    qi = jnp.arange(CHUNK)[:, None]
    kj = jnp.arange(nk)[None, :]
    dist = jnp.abs(qi + pad - kj).astype(jnp.float32)
    slopes = alibi_slopes(N_HEADS).reshape(N_KV_HEADS, Q_PER_KV)
    s = s - slopes[:, :, None, None] * dist[None, None]
    valid = (jnp.arange(nc)[:, None] * CHUNK + jnp.arange(nk)[None, :]) >= pad
    s = jnp.where(valid[None, :, None, None, None, :], s, NEG_INF)
    sink = jnp.broadcast_to(sinks.astype(jnp.float32).reshape(N_KV_HEADS, Q_PER_KV)[None, None, :, :, None, None],
                            s.shape[:-1] + (1,))
    p = jax.nn.softmax(jnp.concatenate([s, sink], axis=-1), axis=-1)[..., :-1]
    o = jnp.einsum('bckgqs,bcskd->bcqkgd', p.astype(v.dtype), vb)
    return o.reshape(b, l, ATT_W)


def s5_layer(u, a_re, a_im, log_dt, b_re, b_im, c_re, c_im, d_skip, w_glu, b_glu):
    b, l = u.shape[:2]
    uf = u.astype(jnp.float32)
    lam = lax.complex(a_re.astype(jnp.float32), a_im.astype(jnp.float32))
    dt = jnp.exp(log_dt.astype(jnp.float32))[:, None]
    lam_bar = jnp.exp(lam * dt)
    bmat = lax.complex(b_re.astype(jnp.float32), b_im.astype(jnp.float32))
    b_bar = ((lam_bar - 1.0) / lam)[..., None] * bmat
    ug = uf.reshape(b, l, SSM_GROUPS, SSM_GROUP).astype(jnp.complex64)
    bu = jnp.einsum('gpc,blgc->blgp', b_bar, ug)
    a = jnp.broadcast_to(lam_bar, bu.shape)

    def combine(e1, e2):
        a1, x1 = e1
        a2, x2 = e2
        return a1 * a2, a2 * x1 + x2

    _, states = lax.associative_scan(combine, (a, bu), axis=1)
    cmat = lax.complex(c_re.astype(jnp.float32), c_im.astype(jnp.float32))
    y = jnp.real(jnp.einsum('gcp,blgp->blgc', cmat, states)).reshape(b, l, SSM_W)
    y = y + d_skip.astype(jnp.float32) * uf
    y = jax.nn.gelu(y)
    y = y * jax.nn.sigmoid(y @ w_glu.astype(jnp.float32) + b_glu.astype(jnp.float32))
    return y.astype(u.dtype)


def multiscale_pool(u, w_pool, pool_scale):
    b, l = u.shape[:2]
    uf = u.astype(jnp.float32).reshape(b, l, POOL_GROUPS, POOL_GW)
    cs = jnp.concatenate([jnp.zeros((b, 1, POOL_GROUPS, POOL_GW), jnp.float32), jnp.cumsum(uf, axis=1)], axis=1)
    t = jnp.arange(l)
    pooled = []
    for gi, w in enumerate(POOL_WINDOWS):
        csp = jnp.pad(cs[:, :, gi], ((0, 0), (w - 1, 0), (0, 0)))
        ssum = csp[:, w:w + l] - csp[:, :l]
        cnt = jnp.minimum(t + 1, w).astype(jnp.float32)[None, :, None]
        pooled.append(ssum / cnt - uf[:, :, gi])
    pooled = jnp.stack(pooled, axis=2)
    y = jnp.einsum('blgi,gio->blgo', pooled, w_pool.astype(jnp.float32)).reshape(b, l, POOL_W)
    return (y * pool_scale.astype(jnp.float32)).astype(u.dtype)


def _fwd_setup_inputs(seed: int = 0) -> dict:
    key = jax.random.key(seed)
    ks = jax.random.split(key, 32)
    f32 = jnp.float32
    nrm = lambda k, shape, s: jax.random.normal(k, shape, f32) * s
    D = D_MODEL
    n_idx = jnp.arange(SSM_STATE, dtype=f32)
    a_re = -0.5 * (1.0 + 0.02 * jax.random.normal(ks[6], (DEPTH, SSM_GROUPS, SSM_STATE), f32))
    a_im = math.pi * n_idx[None, None, :] + 0.02 * jax.random.normal(ks[7], (DEPTH, SSM_GROUPS, SSM_STATE), f32)
    log_dt = jax.random.uniform(ks[8], (DEPTH, SSM_GROUPS), f32, math.log(1e-3), math.log(1e-1))
    return {
        "x": nrm(ks[0], (BATCH, SEQ, D), 1.0),
        "c": nrm(ks[1], (BATCH, D), 1.0),
        "norm_g": 1.0 + nrm(ks[2], (DEPTH, D), 0.02),
        "w_ada": nrm(ks[3], (DEPTH, D, 3 * D), 0.5 * D ** -0.5),
        "b_ada": nrm(ks[4], (DEPTH, 3 * D), 0.02),
        "w_in": nrm(ks[5], (DEPTH, D, IN_W), D ** -0.5),
        "attn_sinks": nrm(ks[9], (DEPTH, N_HEADS), 0.5),
        "ssm_a_re": a_re,
        "ssm_a_im": a_im,
        "ssm_log_dt": log_dt,
        "ssm_b_re": nrm(ks[10], (DEPTH, SSM_GROUPS, SSM_STATE, SSM_GROUP), (2 * SSM_GROUP) ** -0.5),
        "ssm_b_im": nrm(ks[11], (DEPTH, SSM_GROUPS, SSM_STATE, SSM_GROUP), (2 * SSM_GROUP) ** -0.5),
        "ssm_c_re": nrm(ks[12], (DEPTH, SSM_GROUPS, SSM_GROUP, SSM_STATE), (2 * SSM_STATE) ** -0.5),
        "ssm_c_im": nrm(ks[13], (DEPTH, SSM_GROUPS, SSM_GROUP, SSM_STATE), (2 * SSM_STATE) ** -0.5),
        "ssm_d": nrm(ks[14], (DEPTH, SSM_W), 1.0),
        "w_glu": nrm(ks[15], (DEPTH, SSM_W, SSM_W), SSM_W ** -0.5),
        "b_glu": nrm(ks[16], (DEPTH, SSM_W), 0.02),
        "w_pool": nrm(ks[17], (DEPTH, POOL_GROUPS, POOL_GW, POOL_GW), POOL_GW ** -0.5),
        "pool_scale": 1.0 + nrm(ks[18], (DEPTH, POOL_W), 0.1),
        "w_br_att": nrm(ks[19], (DEPTH, ATT_W, D), ATT_W ** -0.5),
        "w_br_ssm": nrm(ks[20], (DEPTH, SSM_W, D), SSM_W ** -0.5),
        "w_br_pool": nrm(ks[21], (DEPTH, POOL_W, D), POOL_W ** -0.5),
        "w_out": nrm(ks[22], (DEPTH, D, D), D ** -0.5),
        "final_g": 1.0 + nrm(ks[23], (D,), 0.02),
    }


def _fwd_reference(x, c, norm_g, w_ada, b_ada, w_in, attn_sinks, ssm_a_re, ssm_a_im, ssm_log_dt,
              ssm_b_re, ssm_b_im, ssm_c_re, ssm_c_im, ssm_d, w_glu, b_glu, w_pool, pool_scale,
              w_br_att, w_br_ssm, w_br_pool, w_out, final_g):
    b, l, _ = x.shape
    split_idx = [int(v) for v in np.cumsum(SPLIT_SIZES)[:-1]]
    c_act = jax.nn.silu(c)
    for li in range(DEPTH):
        mod = c_act @ w_ada[li] + b_ada[li]
        shift, scale, gate = jnp.split(mod, 3, axis=-1)
        h = rmsnorm(x, norm_g[li]) * (1.0 + scale[:, None, :]) + shift[:, None, :]
        proj = h @ w_in[li]
        q, k, v, u_ssm, u_pool, z_att, z_ssm, z_pool, g_logits = jnp.split(proj, split_idx, axis=-1)
        y_att = window_attention(q.reshape(b, l, N_HEADS, HEAD_DIM),
                                 k.reshape(b, l, N_KV_HEADS, HEAD_DIM),
                                 v.reshape(b, l, N_KV_HEADS, HEAD_DIM), attn_sinks[li]) * jax.nn.silu(z_att)
        y_ssm = s5_layer(u_ssm, ssm_a_re[li], ssm_a_im[li], ssm_log_dt[li], ssm_b_re[li], ssm_b_im[li],
                         ssm_c_re[li], ssm_c_im[li], ssm_d[li], w_glu[li], b_glu[li]) * jax.nn.silu(z_ssm)
        y_pool = multiscale_pool(u_pool, w_pool[li], pool_scale[li]) * jax.nn.silu(z_pool)
        gates = jax.nn.sigmoid(g_logits).reshape(b, l, N_BRANCH, D_MODEL)
        merged = (gates[:, :, 0] * (y_att @ w_br_att[li])
                  + gates[:, :, 1] * (y_ssm @ w_br_ssm[li])
                  + gates[:, :, 2] * (y_pool @ w_br_pool[li]))
        x = x + gate[:, None, :] * (merged @ w_out[li])
    return rmsnorm(x, final_g)


import jax as _jax
import jax.numpy as _jnp

TWIN_FORMAT = 'train_step'
FWD_PARAMS = ['x', 'c', 'norm_g', 'w_ada', 'b_ada', 'w_in', 'attn_sinks', 'ssm_a_re', 'ssm_a_im', 'ssm_log_dt', 'ssm_b_re', 'ssm_b_im', 'ssm_c_re', 'ssm_c_im', 'ssm_d', 'w_glu', 'b_glu', 'w_pool', 'pool_scale', 'w_br_att', 'w_br_ssm', 'w_br_pool', 'w_out', 'final_g']
TWIN_WEIGHTS = ['norm_g', 'w_ada', 'b_ada', 'w_in', 'attn_sinks', 'ssm_a_re', 'ssm_a_im', 'ssm_log_dt', 'ssm_b_re', 'ssm_b_im', 'ssm_c_re', 'ssm_c_im', 'ssm_d', 'w_glu', 'b_glu', 'w_pool', 'pool_scale', 'w_br_att', 'w_br_ssm', 'w_br_pool', 'w_out', 'final_g']
TWIN_DIFF_INPUT = 'x'
TWIN_INPUTS = ['x', 'c', 'norm_g', 'w_ada', 'b_ada', 'w_in', 'attn_sinks', 'ssm_a_re', 'ssm_a_im', 'ssm_log_dt', 'ssm_b_re', 'ssm_b_im', 'ssm_c_re', 'ssm_c_im', 'ssm_d', 'w_glu', 'b_glu', 'w_pool', 'pool_scale', 'w_br_att', 'w_br_ssm', 'w_br_pool', 'w_out', 'final_g', 'loss_target', 'm_norm_g', 'm_w_ada', 'm_b_ada', 'm_w_in', 'm_attn_sinks', 'm_ssm_a_re', 'm_ssm_a_im', 'm_ssm_log_dt', 'm_ssm_b_re', 'm_ssm_b_im', 'm_ssm_c_re', 'm_ssm_c_im', 'm_ssm_d', 'm_w_glu', 'm_b_glu', 'm_w_pool', 'm_pool_scale', 'm_w_br_att', 'm_w_br_ssm', 'm_w_br_pool', 'm_w_out', 'm_final_g', 'v_norm_g', 'v_w_ada', 'v_b_ada', 'v_w_in', 'v_attn_sinks', 'v_ssm_a_re', 'v_ssm_a_im', 'v_ssm_log_dt', 'v_ssm_b_re', 'v_ssm_b_im', 'v_ssm_c_re', 'v_ssm_c_im', 'v_ssm_d', 'v_w_glu', 'v_b_glu', 'v_w_pool', 'v_pool_scale', 'v_w_br_att', 'v_w_br_ssm', 'v_w_br_pool', 'v_w_out', 'v_final_g']
TWIN_OUTPUTS = ['loss', 'grad_x', 'grad_norm_g', 'grad_w_ada', 'grad_b_ada', 'grad_w_in', 'grad_attn_sinks', 'grad_ssm_a_re', 'grad_ssm_a_im', 'grad_ssm_log_dt', 'grad_ssm_b_re', 'grad_ssm_b_im', 'grad_ssm_c_re', 'grad_ssm_c_im', 'grad_ssm_d', 'grad_w_glu', 'grad_b_glu', 'grad_w_pool', 'grad_pool_scale', 'grad_w_br_att', 'grad_w_br_ssm', 'grad_w_br_pool', 'grad_w_out', 'grad_final_g', 'delta_norm_g', 'delta_w_ada', 'delta_b_ada', 'delta_w_in', 'delta_attn_sinks', 'delta_ssm_a_re', 'delta_ssm_a_im', 'delta_ssm_log_dt', 'delta_ssm_b_re', 'delta_ssm_b_im', 'delta_ssm_c_re', 'delta_ssm_c_im', 'delta_ssm_d', 'delta_w_glu', 'delta_b_glu', 'delta_w_pool', 'delta_pool_scale', 'delta_w_br_att', 'delta_w_br_ssm', 'delta_w_br_pool', 'delta_w_out', 'delta_final_g', 'new_m_norm_g', 'new_m_w_ada', 'new_m_b_ada', 'new_m_w_in', 'new_m_attn_sinks', 'new_m_ssm_a_re', 'new_m_ssm_a_im', 'new_m_ssm_log_dt', 'new_m_ssm_b_re', 'new_m_ssm_b_im', 'new_m_ssm_c_re', 'new_m_ssm_c_im', 'new_m_ssm_d', 'new_m_w_glu', 'new_m_b_glu', 'new_m_w_pool', 'new_m_pool_scale', 'new_m_w_br_att', 'new_m_w_br_ssm', 'new_m_w_br_pool', 'new_m_w_out', 'new_m_final_g', 'new_v_norm_g', 'new_v_w_ada', 'new_v_b_ada', 'new_v_w_in', 'new_v_attn_sinks', 'new_v_ssm_a_re', 'new_v_ssm_a_im', 'new_v_ssm_log_dt', 'new_v_ssm_b_re', 'new_v_ssm_b_im', 'new_v_ssm_c_re', 'new_v_ssm_c_im', 'new_v_ssm_d', 'new_v_w_glu', 'new_v_b_glu', 'new_v_w_pool', 'new_v_pool_scale', 'new_v_w_br_att', 'new_v_w_br_ssm', 'new_v_w_br_pool', 'new_v_w_out', 'new_v_final_g']
TWIN_LEAF_KINDS = {'loss': 'loss', 'grad_x': 'grad_x', 'grad_norm_g': 'grad_w', 'grad_w_ada': 'grad_w', 'grad_b_ada': 'grad_w', 'grad_w_in': 'grad_w', 'grad_attn_sinks': 'grad_w', 'grad_ssm_a_re': 'grad_w', 'grad_ssm_a_im': 'grad_w', 'grad_ssm_log_dt': 'grad_w', 'grad_ssm_b_re': 'grad_w', 'grad_ssm_b_im': 'grad_w', 'grad_ssm_c_re': 'grad_w', 'grad_ssm_c_im': 'grad_w', 'grad_ssm_d': 'grad_w', 'grad_w_glu': 'grad_w', 'grad_b_glu': 'grad_w', 'grad_w_pool': 'grad_w', 'grad_pool_scale': 'grad_w', 'grad_w_br_att': 'grad_w', 'grad_w_br_ssm': 'grad_w', 'grad_w_br_pool': 'grad_w', 'grad_w_out': 'grad_w', 'grad_final_g': 'grad_w', 'delta_norm_g': 'delta_w', 'delta_w_ada': 'delta_w', 'delta_b_ada': 'delta_w', 'delta_w_in': 'delta_w', 'delta_attn_sinks': 'delta_w', 'delta_ssm_a_re': 'delta_w', 'delta_ssm_a_im': 'delta_w', 'delta_ssm_log_dt': 'delta_w', 'delta_ssm_b_re': 'delta_w', 'delta_ssm_b_im': 'delta_w', 'delta_ssm_c_re': 'delta_w', 'delta_ssm_c_im': 'delta_w', 'delta_ssm_d': 'delta_w', 'delta_w_glu': 'delta_w', 'delta_b_glu': 'delta_w', 'delta_w_pool': 'delta_w', 'delta_pool_scale': 'delta_w', 'delta_w_br_att': 'delta_w', 'delta_w_br_ssm': 'delta_w', 'delta_w_br_pool': 'delta_w', 'delta_w_out': 'delta_w', 'delta_final_g': 'delta_w', 'new_m_norm_g': 'new_m', 'new_m_w_ada': 'new_m', 'new_m_b_ada': 'new_m', 'new_m_w_in': 'new_m', 'new_m_attn_sinks': 'new_m', 'new_m_ssm_a_re': 'new_m', 'new_m_ssm_a_im': 'new_m', 'new_m_ssm_log_dt': 'new_m', 'new_m_ssm_b_re': 'new_m', 'new_m_ssm_b_im': 'new_m', 'new_m_ssm_c_re': 'new_m', 'new_m_ssm_c_im': 'new_m', 'new_m_ssm_d': 'new_m', 'new_m_w_glu': 'new_m', 'new_m_b_glu': 'new_m', 'new_m_w_pool': 'new_m', 'new_m_pool_scale': 'new_m', 'new_m_w_br_att': 'new_m', 'new_m_w_br_ssm': 'new_m', 'new_m_w_br_pool': 'new_m', 'new_m_w_out': 'new_m', 'new_m_final_g': 'new_m', 'new_v_norm_g': 'new_v', 'new_v_w_ada': 'new_v', 'new_v_b_ada': 'new_v', 'new_v_w_in': 'new_v', 'new_v_attn_sinks': 'new_v', 'new_v_ssm_a_re': 'new_v', 'new_v_ssm_a_im': 'new_v', 'new_v_ssm_log_dt': 'new_v', 'new_v_ssm_b_re': 'new_v', 'new_v_ssm_b_im': 'new_v', 'new_v_ssm_c_re': 'new_v', 'new_v_ssm_c_im': 'new_v', 'new_v_ssm_d': 'new_v', 'new_v_w_glu': 'new_v', 'new_v_b_glu': 'new_v', 'new_v_w_pool': 'new_v', 'new_v_pool_scale': 'new_v', 'new_v_w_br_att': 'new_v', 'new_v_w_br_ssm': 'new_v', 'new_v_w_br_pool': 'new_v', 'new_v_w_out': 'new_v', 'new_v_final_g': 'new_v'}


def _forward(args):
    return _fwd_reference(*[args[k] for k in FWD_PARAMS])


def _output_shape():
    out = _jax.eval_shape(lambda: _forward(_fwd_setup_inputs(0)))
    return out.shape, out.dtype

N_MICROBATCH = 1
ADAM_LR = 0.001
ADAM_B1 = 0.9
ADAM_B2 = 0.999
ADAM_EPS = 1e-08
ADAM_WD = 0.01
ADAM_STEP = 10
PER_EXAMPLE_BATCH_AXIS = {'x': 0, 'c': 0, 'loss_target': 0}
SHARED_INPUTS = []
_WEIGHT_DTYPES = {'norm_g': _jnp.float32, 'w_ada': _jnp.float32, 'b_ada': _jnp.float32, 'w_in': _jnp.float32, 'attn_sinks': _jnp.float32, 'ssm_a_re': _jnp.float32, 'ssm_a_im': _jnp.float32, 'ssm_log_dt': _jnp.float32, 'ssm_b_re': _jnp.float32, 'ssm_b_im': _jnp.float32, 'ssm_c_re': _jnp.float32, 'ssm_c_im': _jnp.float32, 'ssm_d': _jnp.float32, 'w_glu': _jnp.float32, 'b_glu': _jnp.float32, 'w_pool': _jnp.float32, 'pool_scale': _jnp.float32, 'w_br_att': _jnp.float32, 'w_br_ssm': _jnp.float32, 'w_br_pool': _jnp.float32, 'w_out': _jnp.float32, 'final_g': _jnp.float32}
MOMENT_SCALE = {'norm_g': 3.116798e-02, 'w_ada': 2.914338e-02, 'b_ada': 4.682250e-02, 'w_in': 1.310339e-02, 'attn_sinks': 7.877113e-03, 'ssm_a_re': 8.670722e-04, 'ssm_a_im': 6.652508e-04, 'ssm_log_dt': 3.191981e-01, 'ssm_b_re': 4.102963e-04, 'ssm_b_im': 4.841431e-04, 'ssm_c_re': 9.260887e-04, 'ssm_c_im': 9.090737e-04, 'ssm_d': 1.198376e-02, 'w_glu': 3.360757e-03, 'b_glu': 4.543526e-03, 'w_pool': 2.560319e-02, 'pool_scale': 2.598587e-02, 'w_br_att': 8.092981e-03, 'w_br_ssm': 7.641834e-03, 'w_br_pool': 1.809617e-02, 'w_out': 2.113189e-02, 'final_g': 3.195163e+01}


def _to_microbatches(a, axis):
    t = _jnp.moveaxis(a, axis, 0)
    t = t.reshape((N_MICROBATCH, t.shape[0] // N_MICROBATCH) + t.shape[1:])
    return _jnp.moveaxis(t, 1, axis + 1)


def setup_inputs(seed: int = 0) -> dict:
    inp = _fwd_setup_inputs(seed)
    key = _jax.random.fold_in(_jax.random.key(seed), 7919)
    shape, _ = _output_shape()
    out = dict(inp)
    out["loss_target"] = _jax.random.normal(_jax.random.fold_in(key, 0), shape, _jnp.float32)
    for i, name in enumerate(TWIN_WEIGHTS):
        w = inp[name].astype(_jnp.float32)
        if MOMENT_SCALE is None:
            s = _jnp.sqrt(_jnp.mean(_jnp.square(w)) + 1e-30)
        else:
            s = MOMENT_SCALE[name]
        km, kv = _jax.random.split(_jax.random.fold_in(key, i + 1))
        out[name] = w
        out["m_" + name] = s * _jax.random.normal(km, w.shape, _jnp.float32)
        out["v_" + name] = (s * s) * _jax.random.uniform(kv, w.shape, _jnp.float32, 0.5, 1.5)
    if N_MICROBATCH > 1:
        for name, axis in PER_EXAMPLE_BATCH_AXIS.items():
            out[name] = _to_microbatches(out[name], axis)
    return {'x': out['x'], 'c': out['c'], 'norm_g': out['norm_g'], 'w_ada': out['w_ada'], 'b_ada': out['b_ada'], 'w_in': out['w_in'], 'attn_sinks': out['attn_sinks'], 'ssm_a_re': out['ssm_a_re'], 'ssm_a_im': out['ssm_a_im'], 'ssm_log_dt': out['ssm_log_dt'], 'ssm_b_re': out['ssm_b_re'], 'ssm_b_im': out['ssm_b_im'], 'ssm_c_re': out['ssm_c_re'], 'ssm_c_im': out['ssm_c_im'], 'ssm_d': out['ssm_d'], 'w_glu': out['w_glu'], 'b_glu': out['b_glu'], 'w_pool': out['w_pool'], 'pool_scale': out['pool_scale'], 'w_br_att': out['w_br_att'], 'w_br_ssm': out['w_br_ssm'], 'w_br_pool': out['w_br_pool'], 'w_out': out['w_out'], 'final_g': out['final_g'], 'loss_target': out['loss_target'], 'm_norm_g': out['m_norm_g'], 'm_w_ada': out['m_w_ada'], 'm_b_ada': out['m_b_ada'], 'm_w_in': out['m_w_in'], 'm_attn_sinks': out['m_attn_sinks'], 'm_ssm_a_re': out['m_ssm_a_re'], 'm_ssm_a_im': out['m_ssm_a_im'], 'm_ssm_log_dt': out['m_ssm_log_dt'], 'm_ssm_b_re': out['m_ssm_b_re'], 'm_ssm_b_im': out['m_ssm_b_im'], 'm_ssm_c_re': out['m_ssm_c_re'], 'm_ssm_c_im': out['m_ssm_c_im'], 'm_ssm_d': out['m_ssm_d'], 'm_w_glu': out['m_w_glu'], 'm_b_glu': out['m_b_glu'], 'm_w_pool': out['m_w_pool'], 'm_pool_scale': out['m_pool_scale'], 'm_w_br_att': out['m_w_br_att'], 'm_w_br_ssm': out['m_w_br_ssm'], 'm_w_br_pool': out['m_w_br_pool'], 'm_w_out': out['m_w_out'], 'm_final_g': out['m_final_g'], 'v_norm_g': out['v_norm_g'], 'v_w_ada': out['v_w_ada'], 'v_b_ada': out['v_b_ada'], 'v_w_in': out['v_w_in'], 'v_attn_sinks': out['v_attn_sinks'], 'v_ssm_a_re': out['v_ssm_a_re'], 'v_ssm_a_im': out['v_ssm_a_im'], 'v_ssm_log_dt': out['v_ssm_log_dt'], 'v_ssm_b_re': out['v_ssm_b_re'], 'v_ssm_b_im': out['v_ssm_b_im'], 'v_ssm_c_re': out['v_ssm_c_re'], 'v_ssm_c_im': out['v_ssm_c_im'], 'v_ssm_d': out['v_ssm_d'], 'v_w_glu': out['v_w_glu'], 'v_b_glu': out['v_b_glu'], 'v_w_pool': out['v_w_pool'], 'v_pool_scale': out['v_pool_scale'], 'v_w_br_att': out['v_w_br_att'], 'v_w_br_ssm': out['v_w_br_ssm'], 'v_w_br_pool': out['v_w_br_pool'], 'v_w_out': out['v_w_out'], 'v_final_g': out['v_final_g']}


def _loss(weights, diff, rest, loss_target):
    with _jax.named_scope("forward"):
        args = {**rest, TWIN_DIFF_INPUT: diff, **{k: w.astype(_WEIGHT_DTYPES[k]) for k, w in weights.items()}}
        y = _forward(args)
    with _jax.named_scope("loss_head"):
        err = _jnp.square(y.astype(_jnp.float32) - loss_target)
        return 0.5 * _jnp.sum(_jnp.mean(err, axis=-1)) if err.ndim else 0.5 * err


def _adamw(w, g, m, v):
    m = ADAM_B1 * m + (1.0 - ADAM_B1) * g
    v = ADAM_B2 * v + (1.0 - ADAM_B2) * _jnp.square(g)
    m_hat = m / (1.0 - ADAM_B1 ** ADAM_STEP)
    v_hat = v / (1.0 - ADAM_B2 ** ADAM_STEP)
    delta = -ADAM_LR * (m_hat / (_jnp.sqrt(v_hat) + ADAM_EPS) + ADAM_WD * w)
    return delta, m, v


def reference(x, c, norm_g, w_ada, b_ada, w_in, attn_sinks, ssm_a_re, ssm_a_im, ssm_log_dt, ssm_b_re, ssm_b_im, ssm_c_re, ssm_c_im, ssm_d, w_glu, b_glu, w_pool, pool_scale, w_br_att, w_br_ssm, w_br_pool, w_out, final_g, loss_target, m_norm_g, m_w_ada, m_b_ada, m_w_in, m_attn_sinks, m_ssm_a_re, m_ssm_a_im, m_ssm_log_dt, m_ssm_b_re, m_ssm_b_im, m_ssm_c_re, m_ssm_c_im, m_ssm_d, m_w_glu, m_b_glu, m_w_pool, m_pool_scale, m_w_br_att, m_w_br_ssm, m_w_br_pool, m_w_out, m_final_g, v_norm_g, v_w_ada, v_b_ada, v_w_in, v_attn_sinks, v_ssm_a_re, v_ssm_a_im, v_ssm_log_dt, v_ssm_b_re, v_ssm_b_im, v_ssm_c_re, v_ssm_c_im, v_ssm_d, v_w_glu, v_b_glu, v_w_pool, v_pool_scale, v_w_br_att, v_w_br_ssm, v_w_br_pool, v_w_out, v_final_g):
    given = dict(x=x, c=c, norm_g=norm_g, w_ada=w_ada, b_ada=b_ada, w_in=w_in, attn_sinks=attn_sinks, ssm_a_re=ssm_a_re, ssm_a_im=ssm_a_im, ssm_log_dt=ssm_log_dt, ssm_b_re=ssm_b_re, ssm_b_im=ssm_b_im, ssm_c_re=ssm_c_re, ssm_c_im=ssm_c_im, ssm_d=ssm_d, w_glu=w_glu, b_glu=b_glu, w_pool=w_pool, pool_scale=pool_scale, w_br_att=w_br_att, w_br_ssm=w_br_ssm, w_br_pool=w_br_pool, w_out=w_out, final_g=final_g, loss_target=loss_target, m_norm_g=m_norm_g, m_w_ada=m_w_ada, m_b_ada=m_b_ada, m_w_in=m_w_in, m_attn_sinks=m_attn_sinks, m_ssm_a_re=m_ssm_a_re, m_ssm_a_im=m_ssm_a_im, m_ssm_log_dt=m_ssm_log_dt, m_ssm_b_re=m_ssm_b_re, m_ssm_b_im=m_ssm_b_im, m_ssm_c_re=m_ssm_c_re, m_ssm_c_im=m_ssm_c_im, m_ssm_d=m_ssm_d, m_w_glu=m_w_glu, m_b_glu=m_b_glu, m_w_pool=m_w_pool, m_pool_scale=m_pool_scale, m_w_br_att=m_w_br_att, m_w_br_ssm=m_w_br_ssm, m_w_br_pool=m_w_br_pool, m_w_out=m_w_out, m_final_g=m_final_g, v_norm_g=v_norm_g, v_w_ada=v_w_ada, v_b_ada=v_b_ada, v_w_in=v_w_in, v_attn_sinks=v_attn_sinks, v_ssm_a_re=v_ssm_a_re, v_ssm_a_im=v_ssm_a_im, v_ssm_log_dt=v_ssm_log_dt, v_ssm_b_re=v_ssm_b_re, v_ssm_b_im=v_ssm_b_im, v_ssm_c_re=v_ssm_c_re, v_ssm_c_im=v_ssm_c_im, v_ssm_d=v_ssm_d, v_w_glu=v_w_glu, v_b_glu=v_b_glu, v_w_pool=v_w_pool, v_pool_scale=v_pool_scale, v_w_br_att=v_w_br_att, v_w_br_ssm=v_w_br_ssm, v_w_br_pool=v_w_br_pool, v_w_out=v_w_out, v_final_g=v_final_g)
    weights = {n: given[n] for n in TWIN_WEIGHTS}
    shared = {n: given[n] for n in SHARED_INPUTS}
    per_example = {n: given[n] for n in ['x', 'c']}
    grad_fn = _jax.value_and_grad(_loss, argnums=(0, 1))

    def one_microbatch(ex, loss_target):
        ex = dict(ex)
        diff = ex.pop(TWIN_DIFF_INPUT)
        return grad_fn(weights, diff, {**shared, **ex}, loss_target)

    if N_MICROBATCH == 1:
        loss, (grad_w, grad_x) = one_microbatch(per_example, given["loss_target"])
    else:
        def body(carry, xs):
            loss_sum, grad_sum = carry
            l_k, (gw_k, gx_k) = one_microbatch(xs[0], xs[1])
            with _jax.named_scope("update"):
                return (loss_sum + l_k, _jax.tree.map(_jnp.add, grad_sum, gw_k)), gx_k

        init = (_jnp.zeros((), _jnp.float32), _jax.tree.map(_jnp.zeros_like, weights))
        (loss, grad_w), grad_x = _jax.lax.scan(body, init, (per_example, given["loss_target"]))
    with _jax.named_scope("update"):
        delta_w, new_m, new_v = {}, {}, {}
        for n in TWIN_WEIGHTS:
            delta_w[n], new_m[n], new_v[n] = _adamw(weights[n], grad_w[n], given["m_" + n], given["v_" + n])
    return (loss, grad_x, *[grad_w[n] for n in TWIN_WEIGHTS], *[delta_w[n] for n in TWIN_WEIGHTS],
            *[new_m[n] for n in TWIN_WEIGHTS], *[new_v[n] for n in TWIN_WEIGHTS])
```

```python
import functools
import math

import jax
import jax.numpy as jnp
from jax import lax
from jax.experimental import pallas as pl
from jax.experimental.pallas import tpu as pltpu

F32 = jnp.float32
BF16 = jnp.bfloat16

D_MODEL = 1024
CHUNK = 64
N_HEADS = 8
N_KV_HEADS = 2
HEAD_DIM = 64
Q_PER_KV = N_HEADS // N_KV_HEADS
HALO = 128
ATT_W = 512
KV_W = 128
SSM_W = 512
SSM_GROUP = 16
SSM_GROUPS = 32
SSM_STATE = 64
POOL_W = 512
POOL_GW = 128
IN_W = 6400
EPS = 1e-6
NEG_INF = -1e30
ADAM_LR = 0.001
ADAM_B1 = 0.9
ADAM_B2 = 0.999
ADAM_EPS = 1e-08
ADAM_WD = 0.01
ADAM_STEP = 10

OFF_GL, OFF_Q, OFF_ZA, OFF_US, OFF_ZS, OFF_UP, OFF_ZP, OFF_KV = 0, 3072, 3584, 4096, 4608, 5120, 5632, 6144
_PERM_PIECES = ((3328, 3072), (0, 512), (1792, 512), (768, 512), (2304, 512), (1280, 512), (2816, 512), (512, 256))

LANES = 128
SUBLANES = 8
N_SBLK = SSM_GROUPS * SSM_STATE // LANES
VMEM_LIMIT = 48 * 1024 * 1024

N_CHIPS = 4
N_DEV = 8

WEIGHTS = ['norm_g', 'w_ada', 'b_ada', 'w_in', 'attn_sinks', 'ssm_a_re', 'ssm_a_im', 'ssm_log_dt', 'ssm_b_re',
           'ssm_b_im', 'ssm_c_re', 'ssm_c_im', 'ssm_d', 'w_glu', 'b_glu', 'w_pool', 'pool_scale', 'w_br_att',
           'w_br_ssm', 'w_br_pool', 'w_out', 'final_g']
SMALL = ['norm_g', 'b_ada', 'attn_sinks', 'ssm_a_re', 'ssm_a_im', 'ssm_log_dt', 'ssm_b_re', 'ssm_b_im', 'ssm_c_re',
         'ssm_c_im', 'ssm_d', 'b_glu', 'w_pool', 'pool_scale', 'final_g']
BIG = (('w_in', (1024, 1600)), ('w_br_att', (512, 256)), ('w_br_ssm', (512, 256)), ('w_br_pool', (512, 256)),
       ('w_out', (256, 1024)), ('w_glu', (128, 512)))
PACK_COLS = 1024


def _params(sem=None):
    return pltpu.CompilerParams(dimension_semantics=sem, vmem_limit_bytes=VMEM_LIMIT)


def _row(v):
    return v.reshape(1, -1)


def _full(shape):
    nd = len(shape)
    return pl.BlockSpec(shape, lambda *_: (0,) * nd)


def _sigmoid(v):
    return 1.0 / (1.0 + jnp.exp(-v))


def _silu_and_grad(z):
    s = _sigmoid(z)
    return z * s, s * (1.0 + z * (1.0 - s))


_GELU_K = math.sqrt(2.0 / math.pi)


def _gelu_and_grad(v):
    inner = _GELU_K * (v + 0.044715 * v * v * v)
    th = jnp.tanh(inner)
    val = 0.5 * v * (1.0 + th)
    grad = 0.5 * (1.0 + th) + 0.5 * v * (1.0 - th * th) * _GELU_K * (1.0 + 3 * 0.044715 * v * v)
    return val, grad


_NN = (((1,), (0,)), ((), ()))
_NT = (((1,), (1,)), ((), ()))
_TN = (((0,), (0,)), ((), ()))


def _dot(a, b, dims=_NN):
    return lax.dot_general(a.astype(BF16), b.astype(BF16), dims, preferred_element_type=F32)


def _matmul(a, b, mode, *, tm, tn, tk, name, out_dtype=F32):
    if mode == 'nn':
        (m, k), (_, n) = a.shape, b.shape
    elif mode == 'nt':
        (m, k), (n, _) = a.shape, b.shape
    else:
        (k, m), (_, n) = a.shape, b.shape
    tm, tn, tk = min(tm, m), min(tn, n), min(tk, k)
    assert m % tm == 0 and n % tn == 0 and k % tk == 0, (name, a.shape, b.shape)
    nk = k // tk
    dims = {'nn': _NN, 'nt': _NT, 'tn': _TN}[mode]

    def body(a_ref, b_ref, o_ref, acc_ref):
        kk = pl.program_id(2)

        @pl.when(kk == 0)
        def _():
            acc_ref[...] = jnp.zeros_like(acc_ref)

        acc_ref[...] += _dot(a_ref[...], b_ref[...], dims)

        @pl.when(kk == nk - 1)
        def _():
            o_ref[...] = acc_ref[...].astype(out_dtype)

    a_spec = pl.BlockSpec((tk, tm), lambda i, j, kk: (kk, i)) if mode == 'tn' else pl.BlockSpec((tm, tk), lambda i, j, kk: (i, kk))
    b_spec = pl.BlockSpec((tn, tk), lambda i, j, kk: (j, kk)) if mode == 'nt' else pl.BlockSpec((tk, tn), lambda i, j, kk: (kk, j))
    return pl.pallas_call(
        body, name=name, grid=(m // tm, n // tn, nk), in_specs=[a_spec, b_spec],
        out_specs=pl.BlockSpec((tm, tn), lambda i, j, kk: (i, j)),
        out_shape=jax.ShapeDtypeStruct((m, n), out_dtype), scratch_shapes=[pltpu.VMEM((tm, tn), F32)],
        compiler_params=_params(("parallel", "parallel", "arbitrary")))(a, b)


ROWS_NORM = 512


def _norm_fwd(x, g, mod):
    l, d = x.shape
    tr = min(ROWS_NORM, l)

    def body(x_ref, g_ref, mod_ref, h_ref):
        xv = x_ref[...]
        r = lax.rsqrt(jnp.mean(xv * xv, axis=-1, keepdims=True) + EPS)
        shift, scale = mod_ref[:, 0:d], mod_ref[:, d:2 * d]
        h_ref[...] = ((xv * r * g_ref[...]) * (1.0 + scale) + shift).astype(BF16)

    return pl.pallas_call(
        body, name="norm_fwd", grid=(l // tr,),
        in_specs=[pl.BlockSpec((tr, d), lambda i: (i, 0)), _full((1, d)), _full((1, 3 * d))],
        out_specs=pl.BlockSpec((tr, d), lambda i: (i, 0)), out_shape=jax.ShapeDtypeStruct((l, d), BF16),
        compiler_params=_params(("parallel",)))(x, g, mod)


def _norm_bwd(x, dh, dxo, g, mod):
    l, d = x.shape
    tr = min(ROWS_NORM, l)

    def body(x_ref, dh_ref, dxo_ref, g_ref, mod_ref, dx_ref, st_ref):
        @pl.when(pl.program_id(0) == 0)
        def _():
            st_ref[...] = jnp.zeros_like(st_ref)

        xv, dhv = x_ref[...], dh_ref[...]
        r = lax.rsqrt(jnp.mean(xv * xv, axis=-1, keepdims=True) + EPS)
        xn = xv * r
        gv = g_ref[...]
        sc1 = 1.0 + mod_ref[:, d:2 * d]
        dxn = dhv * gv * sc1
        dx_ref[...] = dxo_ref[...] + r * (dxn - xn * jnp.mean(dxn * xn, axis=-1, keepdims=True))
        st_ref[0:1, :] += jnp.sum(dhv, axis=0, keepdims=True)
        st_ref[1:2, :] += jnp.sum(dhv * xn * gv, axis=0, keepdims=True)
        st_ref[2:3, :] += jnp.sum(dhv * xn * sc1, axis=0, keepdims=True)

    blk = pl.BlockSpec((tr, d), lambda i: (i, 0))
    return pl.pallas_call(
        body, name="norm_bwd", grid=(l // tr,),
        in_specs=[blk, blk, blk, _full((1, d)), _full((1, 3 * d))],
        out_specs=[blk, _full((SUBLANES, d))],
        out_shape=[jax.ShapeDtypeStruct((l, d), F32), jax.ShapeDtypeStruct((SUBLANES, d), F32)],
        compiler_params=_params(("arbitrary",)))(x, dh, dxo, g, mod)


def _loss_head(x, g, target):
    l, d = x.shape
    tr = min(ROWS_NORM, l)

    def body(x_ref, g_ref, t_ref, dx_ref, st_ref):
        @pl.when(pl.program_id(0) == 0)
        def _():
            st_ref[...] = jnp.zeros_like(st_ref)

        xv = x_ref[...]
        r = lax.rsqrt(jnp.mean(xv * xv, axis=-1, keepdims=True) + EPS)
        xn = xv * r
        gv = g_ref[...]
        err = xn * gv - t_ref[...]
        part = 0.5 * jnp.sum(jnp.mean(err * err, axis=-1, keepdims=True), axis=0, keepdims=True)
        dy = err * (1.0 / d)
        dxn = dy * gv
        dx_ref[...] = r * (dxn - xn * jnp.mean(dxn * xn, axis=-1, keepdims=True))
        st_ref[0:1, :] += jnp.sum(dy * xn, axis=0, keepdims=True)
        st_ref[1:2, :] += jnp.broadcast_to(part, (1, d))

    blk = pl.BlockSpec((tr, d), lambda i: (i, 0))
    return pl.pallas_call(
        body, name="loss_head", grid=(l // tr,), in_specs=[blk, _full((1, d)), blk],
        out_specs=[blk, _full((SUBLANES, d))],
        out_shape=[jax.ShapeDtypeStruct((l, d), F32), jax.ShapeDtypeStruct((SUBLANES, d), F32)],
        compiler_params=_params(("arbitrary",)))(x, g, target)


ROWS_ATT = 256
_SLOPES = tuple(2.0 ** (-8.0 * (h + 1) / N_HEADS) for h in range(N_HEADS))


def _att_mask(i, t):
    r = lax.broadcasted_iota(jnp.int32, (t, t + HALO), 0)
    j = lax.broadcasted_iota(jnp.int32, (t, t + HALO), 1)
    dist = jnp.abs(r + HALO - j).astype(F32)
    rc, jc = r // CHUNK, j // CHUNK
    allowed = (jc >= rc) & (jc <= rc + 2) & ((j >= HALO) | (i > 0))
    return dist, allowed


def _att_probs(qh, k, dist, allowed, slope, sink):
    s = _dot(qh, k, _NT) * (1.0 / math.sqrt(HEAD_DIM)) - slope * dist
    s = jnp.where(allowed, s, NEG_INF)
    m = jnp.maximum(jnp.max(s, axis=1, keepdims=True), sink)
    e = jnp.exp(s - m)
    es = jnp.exp(sink - m)
    den = jnp.sum(e, axis=1, keepdims=True) + es
    return e / den, es / den


def _att_specs(t):
    q_spec = pl.BlockSpec((t, ATT_W), lambda i: (i, OFF_Q // ATT_W))
    kv_spec = pl.BlockSpec((t, 2 * KV_W), lambda i: (i, OFF_KV // (2 * KV_W)))
    halo_spec = pl.BlockSpec((HALO, 2 * KV_W), lambda i: (jnp.maximum(i * (t // HALO) - 1, 0), OFF_KV // (2 * KV_W)))
    return q_spec, kv_spec, halo_spec


def _attention_fwd(proj, sinks):
    l = proj.shape[0]
    t = min(ROWS_ATT, l)

    def body(sink_ref, q_ref, kv_ref, halo_ref, o_ref):
        dist, allowed = _att_mask(pl.program_id(0), t)
        kv = jnp.concatenate([halo_ref[...], kv_ref[...]], axis=0)
        for h in range(N_HEADS):
            kh = h // Q_PER_KV
            k = kv[:, kh * HEAD_DIM:(kh + 1) * HEAD_DIM]
            v = kv[:, KV_W + kh * HEAD_DIM:KV_W + (kh + 1) * HEAD_DIM]
            p, _ = _att_probs(q_ref[:, h * HEAD_DIM:(h + 1) * HEAD_DIM], k, dist, allowed, _SLOPES[h], sink_ref[h])
            o_ref[:, h * HEAD_DIM:(h + 1) * HEAD_DIM] = _dot(p, v)

    q_spec, kv_spec, halo_spec = _att_specs(t)
    return pl.pallas_call(
        body, name="attention_fwd", grid=(l // t,),
        in_specs=[pl.BlockSpec(memory_space=pltpu.SMEM), q_spec, kv_spec, halo_spec],
        out_specs=pl.BlockSpec((t, ATT_W), lambda i: (i, 0)), out_shape=jax.ShapeDtypeStruct((l, ATT_W), F32),
        compiler_params=_params(("parallel",)))(sinks, proj, proj, proj)


def _attention_bwd(proj, sinks, do):
    l = proj.shape[0]
    t = min(ROWS_ATT, l)

    def body(sink_ref, q_ref, kv_ref, halo_ref, do_ref, dq_ref, dkv_ref, dsink_ref):
        i = pl.program_id(0)

        @pl.when(i == 0)
        def _():
            dkv_ref[...] = jnp.zeros_like(dkv_ref)
            dsink_ref[...] = jnp.zeros_like(dsink_ref)

        dist, allowed = _att_mask(i, t)
        kv = jnp.concatenate([halo_ref[...], kv_ref[...]], axis=0)
        rows = pl.ds(pl.multiple_of(i * t, t), t + HALO)
        for kh in range(N_KV_HEADS):
            k = kv[:, kh * HEAD_DIM:(kh + 1) * HEAD_DIM]
            v = kv[:, KV_W + kh * HEAD_DIM:KV_W + (kh + 1) * HEAD_DIM]
            dk = jnp.zeros((t + HALO, HEAD_DIM), F32)
            dv = jnp.zeros((t + HALO, HEAD_DIM), F32)
            for h in range(kh * Q_PER_KV, (kh + 1) * Q_PER_KV):
                qh = q_ref[:, h * HEAD_DIM:(h + 1) * HEAD_DIM]
                doh = do_ref[:, h * HEAD_DIM:(h + 1) * HEAD_DIM]
                p, ps = _att_probs(qh, k, dist, allowed, _SLOPES[h], sink_ref[h])
                dp = _dot(doh, v, _NT)
                delta = jnp.sum(p * dp, axis=1, keepdims=True)
                ds = p * (dp - delta) * (1.0 / math.sqrt(HEAD_DIM))
                dsink_ref[h:h + 1, :] += jnp.broadcast_to(-jnp.sum(ps * delta, axis=0, keepdims=True), (1, LANES))
                dq_ref[:, h * HEAD_DIM:(h + 1) * HEAD_DIM] = _dot(ds, k)
                dk = dk + _dot(ds, qh, _TN)
                dv = dv + _dot(p, doh, _TN)
            dkv_ref[rows, kh * HEAD_DIM:(kh + 1) * HEAD_DIM] += dk
            dkv_ref[rows, KV_W + kh * HEAD_DIM:KV_W + (kh + 1) * HEAD_DIM] += dv

    q_spec, kv_spec, halo_spec = _att_specs(t)
    blk = pl.BlockSpec((t, ATT_W), lambda i: (i, 0))
    return pl.pallas_call(
        body, name="attention_bwd", grid=(l // t,),
        in_specs=[pl.BlockSpec(memory_space=pltpu.SMEM), q_spec, kv_spec, halo_spec, blk],
        out_specs=[blk, _full((HALO + l, 2 * KV_W)), _full((N_HEADS, LANES))],
        out_shape=[jax.ShapeDtypeStruct((l, ATT_W), F32), jax.ShapeDtypeStruct((HALO + l, 2 * KV_W), F32),
                   jax.ShapeDtypeStruct((N_HEADS, LANES), F32)],
        compiler_params=_params(("arbitrary",)))(sinks, proj, proj, proj, do)


ROWS_SSM = 512


def _ssm_discretize(a_re, a_im, log_dt, b_re, b_im):
    lam = lax.complex(a_re, a_im)
    dt = jnp.exp(log_dt)[:, None]
    lam_bar = jnp.exp(lam * dt)
    b_bar = ((lam_bar - 1.0) / lam)[..., None] * lax.complex(b_re, b_im)
    return lam, dt, lam_bar, b_bar


def _ssm_block_diag(m):
    e = m.reshape(N_SBLK, 2, SSM_GROUP, SSM_STATE)
    e = e[:, :, :, None, :] * jnp.eye(2, dtype=m.dtype)[None, :, None, :, None]
    e = e.reshape(N_SBLK, 2 * SSM_GROUP, LANES)
    oh = jax.nn.one_hot(jnp.arange(N_SBLK) % 4, 4, dtype=m.dtype)
    return (oh[:, :, None, None] * e[:, None]).reshape(N_SBLK, LANES, LANES)


def _ssm_tables(a_re, a_im, log_dt, b_re, b_im, c_re, c_im):
    _, _, lam_bar, b_bar = _ssm_discretize(a_re, a_im, log_dt, b_re, b_im)
    lam_blk = jnp.stack([jnp.real(lam_bar).reshape(N_SBLK, LANES), jnp.imag(lam_bar).reshape(N_SBLK, LANES)], axis=1)
    bt = jnp.transpose(b_bar, (0, 2, 1))
    bb = jnp.concatenate([_ssm_block_diag(jnp.real(bt)), _ssm_block_diag(jnp.imag(bt))], axis=2)
    cb = jnp.concatenate([jnp.transpose(_ssm_block_diag(c_re), (0, 2, 1)),
                          jnp.transpose(_ssm_block_diag(-c_im), (0, 2, 1))], axis=1)
    return lam_blk, bb, cb


def _ssm_powers(a_re, a_im, log_dt):
    lam = lax.complex(a_re, a_im)
    dt = jnp.exp(log_dt)[:, None]
    k = jnp.arange(1, SUBLANES + 1, dtype=F32)
    pw = jnp.exp((lam * dt)[None] * k[:, None, None]).reshape(SUBLANES, N_SBLK, LANES)
    pw = jnp.transpose(pw, (1, 0, 2))
    rev = pw[:, ::-1]
    return jnp.concatenate([jnp.real(pw), jnp.imag(pw), jnp.real(rev), jnp.imag(rev)], axis=1)


def _scan_consts(pw_ref, reverse):
    row = lax.broadcasted_iota(jnp.int32, (SUBLANES, LANES), 0)
    sign = -1.0 if reverse else 1.0

    def power(k):
        return (jnp.broadcast_to(pw_ref[0, k - 1:k, :], (SUBLANES, LANES)),
                sign * jnp.broadcast_to(pw_ref[0, SUBLANES + k - 1:SUBLANES + k, :], (SUBLANES, LANES)))

    steps = []
    for d in (1, 2, 4):
        pr, pi = power(d)
        keep = (row < SUBLANES - d) if reverse else (row >= d)
        steps.append((d, jnp.where(keep, pr, 0.0), jnp.where(keep, pi, 0.0)))
    base = 2 * SUBLANES if reverse else 0
    return steps, pw_ref[0, base:base + SUBLANES, :], sign * pw_ref[0, base + SUBLANES:base + 2 * SUBLANES, :]


def _scan_tile(xr, xi, steps, reverse):
    for d, ar, ai in steps:
        shift = SUBLANES - d if reverse else d
        rr, ri = pltpu.roll(xr, shift, 0), pltpu.roll(xi, shift, 0)
        xr, xi = xr + (ar * rr - ai * ri), xi + (ar * ri + ai * rr)
    return xr, xi


def _bcast_row(v, r):
    return jnp.broadcast_to(v[r:r + 1, :], (SUBLANES, LANES))


def _scan_forward(s_scr, pw_ref, n_tiles):
    steps, pr, pi = _scan_consts(pw_ref, False)

    def tile(t, carry):
        cr, ci = carry
        rows = pl.ds(pl.multiple_of(t * SUBLANES, SUBLANES), SUBLANES)
        xr, xi = _scan_tile(s_scr[rows, 0:LANES], s_scr[rows, LANES:2 * LANES], steps, False)
        sr, si = xr + (pr * cr - pi * ci), xi + (pr * ci + pi * cr)
        s_scr[rows, 0:LANES] = sr
        s_scr[rows, LANES:2 * LANES] = si
        return _bcast_row(sr, SUBLANES - 1), _bcast_row(si, SUBLANES - 1)

    zero = jnp.zeros((SUBLANES, LANES), F32)
    lax.fori_loop(0, n_tiles, tile, (zero, zero))


def _row_chunks(l):
    rc = min(ROWS_SSM, l)
    return rc, l // rc


def _ssm_fwd(proj, bb, cb, pw):
    l = proj.shape[0]
    rc, n_chunks = _row_chunks(l)

    def body(u_ref, bb_ref, cb_ref, pw_ref, y_ref, s_scr):
        j = pl.program_id(0)

        def fill(ci, _):
            rows = pl.ds(pl.multiple_of(ci * rc, rc), rc)
            s_scr[rows, :] = _dot(u_ref[rows, :], bb_ref[0])
            return 0

        lax.fori_loop(0, n_chunks, fill, 0)
        _scan_forward(s_scr, pw_ref, l // SUBLANES)

        @pl.when(j % 4 == 0)
        def _():
            y_ref[...] = jnp.zeros_like(y_ref)

        def emit(ci, _):
            rows = pl.ds(pl.multiple_of(ci * rc, rc), rc)
            y_ref[rows, :] += _dot(s_scr[rows, :], cb_ref[0])
            return 0

        lax.fori_loop(0, n_chunks, emit, 0)

    return pl.pallas_call(
        body, name="ssm_fwd", grid=(N_SBLK,),
        in_specs=[pl.BlockSpec((l, LANES), lambda j: (0, OFF_US // LANES + j // 4)),
                  pl.BlockSpec((1, LANES, 2 * LANES), lambda j: (j, 0, 0)),
                  pl.BlockSpec((1, 2 * LANES, LANES), lambda j: (j, 0, 0)),
                  pl.BlockSpec((1, 4 * SUBLANES, LANES), lambda j: (j, 0, 0))],
        out_specs=pl.BlockSpec((l, LANES), lambda j: (0, j // 4)),
        out_shape=jax.ShapeDtypeStruct((l, SSM_W), F32), scratch_shapes=[pltpu.VMEM((l, 2 * LANES), F32)],
        compiler_params=_params(("arbitrary",)))(proj, bb, cb, pw)


def _ssm_bwd(proj, dy, bb, cb, pw):
    l = proj.shape[0]
    rc, n_chunks = _row_chunks(l)
    n_tiles = l // SUBLANES

    def body(u_ref, dy_ref, bb_ref, cb_ref, pw_ref, du_ref, dbb_ref, dcb_ref, dlam_ref, s_scr, a_scr):
        j = pl.program_id(0)

        def fill(ci, _):
            rows = pl.ds(pl.multiple_of(ci * rc, rc), rc)
            s_scr[rows, :] = _dot(u_ref[rows, :], bb_ref[0])
            return 0

        lax.fori_loop(0, n_chunks, fill, 0)
        _scan_forward(s_scr, pw_ref, n_tiles)

        dcb_ref[...] = jnp.zeros_like(dcb_ref)

        def through_c(ci, _):
            rows = pl.ds(pl.multiple_of(ci * rc, rc), rc)
            dyv = dy_ref[rows, :]
            dcb_ref[0] += _dot(s_scr[rows, :], dyv, _TN)
            a_scr[rows, :] = _dot(dyv, cb_ref[0], _NT)
            return 0

        lax.fori_loop(0, n_chunks, through_c, 0)

        steps, pr, pi = _scan_consts(pw_ref, True)
        row = lax.broadcasted_iota(jnp.int32, (SUBLANES, LANES), 0)

        def tile(k, carry):
            cr, ci, accr, acci = carry
            t = n_tiles - 1 - k
            rows = pl.ds(pl.multiple_of(t * SUBLANES, SUBLANES), SUBLANES)
            prev = pl.ds(pl.multiple_of(jnp.maximum(t - 1, 0) * SUBLANES, SUBLANES), SUBLANES)
            xr, xi = _scan_tile(a_scr[rows, 0:LANES], a_scr[rows, LANES:2 * LANES], steps, True)
            gr, gi = xr + (pr * cr - pi * ci), xi + (pr * ci + pi * cr)
            a_scr[rows, 0:LANES] = gr
            a_scr[rows, LANES:2 * LANES] = gi
            first = row == 0
            edge = jnp.where(t > 0, 1.0, 0.0)
            spr = jnp.where(first, edge * pltpu.roll(s_scr[prev, 0:LANES], 1, 0), pltpu.roll(s_scr[rows, 0:LANES], 1, 0))
            spi = jnp.where(first, edge * pltpu.roll(s_scr[prev, LANES:2 * LANES], 1, 0),
                            pltpu.roll(s_scr[rows, LANES:2 * LANES], 1, 0))
            return (_bcast_row(gr, 0), _bcast_row(gi, 0), accr + (gr * spr + gi * spi), acci + (gi * spr - gr * spi))

        zero = jnp.zeros((SUBLANES, LANES), F32)
        _, _, accr, acci = lax.fori_loop(0, n_tiles, tile, (zero, zero, zero, zero))
        dlr = jnp.broadcast_to(jnp.sum(accr, axis=0, keepdims=True), (SUBLANES, LANES))
        dli = jnp.broadcast_to(jnp.sum(acci, axis=0, keepdims=True), (SUBLANES, LANES))
        dlam_ref[0] = jnp.where(row == 0, dlr, jnp.where(row == 1, dli, 0.0))

        dbb_ref[...] = jnp.zeros_like(dbb_ref)

        @pl.when(j % 4 == 0)
        def _():
            du_ref[...] = jnp.zeros_like(du_ref)

        def through_b(ci, _):
            rows = pl.ds(pl.multiple_of(ci * rc, rc), rc)
            av = a_scr[rows, :]
            dbb_ref[0] += _dot(u_ref[rows, :], av, _TN)
            du_ref[rows, :] += _dot(av, bb_ref[0], _NT)
            return 0

        lax.fori_loop(0, n_chunks, through_b, 0)

    return pl.pallas_call(
        body, name="ssm_bwd", grid=(N_SBLK,),
        in_specs=[pl.BlockSpec((l, LANES), lambda j: (0, OFF_US // LANES + j // 4)),
                  pl.BlockSpec((l, LANES), lambda j: (0, j // 4)),
                  pl.BlockSpec((1, LANES, 2 * LANES), lambda j: (j, 0, 0)),
                  pl.BlockSpec((1, 2 * LANES, LANES), lambda j: (j, 0, 0)),
                  pl.BlockSpec((1, 4 * SUBLANES, LANES), lambda j: (j, 0, 0))],
        out_specs=[pl.BlockSpec((l, LANES), lambda j: (0, j // 4)),
                   pl.BlockSpec((1, LANES, 2 * LANES), lambda j: (j, 0, 0)),
                   pl.BlockSpec((1, 2 * LANES, LANES), lambda j: (j, 0, 0)),
                   pl.BlockSpec((1, SUBLANES, LANES), lambda j: (j, 0, 0))],
        out_shape=[jax.ShapeDtypeStruct((l, SSM_W), F32), jax.ShapeDtypeStruct((N_SBLK, LANES, 2 * LANES), F32),
                   jax.ShapeDtypeStruct((N_SBLK, 2 * LANES, LANES), F32), jax.ShapeDtypeStruct((N_SBLK, SUBLANES, LANES), F32)],
        scratch_shapes=[pltpu.VMEM((l, 2 * LANES), F32), pltpu.VMEM((l, 2 * LANES), F32)],
        compiler_params=_params(("arbitrary",)))(proj, dy, bb, cb, pw)


def _pool_windows(v, t, l, ahead):
    def shifted(a, d):
        if ahead:
            return jnp.where(t < l - d, pltpu.roll(a, l - d, 0), 0.0)
        return jnp.where(t >= d, pltpu.roll(a, d, 0), 0.0)

    w2 = v + shifted(v, 1)
    w4 = w2 + shifted(w2, 2)
    w8 = w4 + shifted(w4, 4)
    w16 = w8 + shifted(w8, 8)
    return w2, w4, w8, w16


def _pool_select(g, ws):
    return jnp.where(g == 0, ws[0], jnp.where(g == 1, ws[1], jnp.where(g == 2, ws[2], ws[3])))


def _pool_count(g, t):
    return jnp.minimum(t + 1, jnp.left_shift(2, g)).astype(F32)


def _pool_specs(l):
    return [pl.BlockSpec((l, POOL_GW), lambda g: (0, OFF_UP // POOL_GW + g)),
            pl.BlockSpec((1, POOL_GW, POOL_GW), lambda g: (g, 0, 0)),
            pl.BlockSpec((1, POOL_GW), lambda g: (0, g))]


def _pool_fwd(proj, w_pool, scale):
    l = proj.shape[0]

    def body(u_ref, w_ref, sc_ref, y_ref):
        g = pl.program_id(0)
        t = lax.broadcasted_iota(jnp.int32, (l, 1), 0)
        u = u_ref[...]
        pooled = _pool_select(g, _pool_windows(u, t, l, False)) / _pool_count(g, t) - u
        y_ref[...] = _dot(pooled, w_ref[0]) * sc_ref[...]

    return pl.pallas_call(
        body, name="pool_fwd", grid=(4,), in_specs=_pool_specs(l),
        out_specs=pl.BlockSpec((l, POOL_GW), lambda g: (0, g)), out_shape=jax.ShapeDtypeStruct((l, POOL_W), F32),
        compiler_params=_params(("parallel",)))(proj, w_pool, scale)


def _pool_bwd(proj, w_pool, scale, dy):
    l = proj.shape[0]

    def body(u_ref, w_ref, sc_ref, dy_ref, du_ref, dw_ref, dsc_ref):
        g = pl.program_id(0)
        t = lax.broadcasted_iota(jnp.int32, (l, 1), 0)
        u = u_ref[...]
        cnt = _pool_count(g, t)
        pooled = _pool_select(g, _pool_windows(u, t, l, False)) / cnt - u
        dyv = dy_ref[...]
        dsc_ref[...] = jnp.sum(dyv * _dot(pooled, w_ref[0]), axis=0, keepdims=True)
        dyl = dyv * sc_ref[...]
        dw_ref[0] = _dot(pooled, dyl, _TN)
        dpooled = _dot(dyl, w_ref[0], _NT)
        du_ref[...] = _pool_select(g, _pool_windows(dpooled / cnt, t, l, True)) - dpooled

    return pl.pallas_call(
        body, name="pool_bwd", grid=(4,), in_specs=_pool_specs(l) + [pl.BlockSpec((l, POOL_GW), lambda g: (0, g))],
        out_specs=[pl.BlockSpec((l, POOL_GW), lambda g: (0, g)), pl.BlockSpec((1, POOL_GW, POOL_GW), lambda g: (g, 0, 0)),
                   pl.BlockSpec((1, POOL_GW), lambda g: (0, g))],
        out_shape=[jax.ShapeDtypeStruct((l, POOL_W), F32), jax.ShapeDtypeStruct((4, POOL_GW, POOL_GW), F32),
                   jax.ShapeDtypeStruct((1, POOL_W), F32)],
        compiler_params=_params(("parallel",)))(proj, w_pool, scale, dy)


ROWS_MERGE = 128


def _merge_inputs(tr):
    def col(off, w):
        return pl.BlockSpec((tr, w), lambda i: (i, off // w))

    def act(w):
        return pl.BlockSpec((tr, w), lambda i: (i, 0))

    d = D_MODEL
    return ([col(OFF_GL, 3 * d), col(OFF_ZA, ATT_W), col(OFF_US, SSM_W), col(OFF_ZS, SSM_W), col(OFF_ZP, POOL_W),
             act(ATT_W), act(SSM_W), act(POOL_W)]
            + [_full((1, SSM_W)), _full((SSM_W, SSM_W)), _full((1, SSM_W)), _full((ATT_W, d)), _full((SSM_W, d)),
               _full((POOL_W, d)), _full((d, d)), _full((1, 3 * d))])


def _merge_forward_math(gl_ref, za_ref, us_ref, zs_ref, zp_ref, o_ref, yl_ref, yp_ref, d_ref, wg_ref, bg_ref,
                        wba_ref, wbs_ref, wbp_ref):
    d = D_MODEL
    r = {}
    r['sa'], r['dsa'] = _silu_and_grad(za_ref[...])
    r['ss'], r['dss'] = _silu_and_grad(zs_ref[...])
    r['sp'], r['dsp'] = _silu_and_grad(zp_ref[...])
    r['y_att'] = o_ref[...] * r['sa']
    y1 = yl_ref[...] + d_ref[...] * us_ref[...]
    r['y2'], r['dgelu'] = _gelu_and_grad(y1)
    r['sg'] = _sigmoid(_dot(r['y2'], wg_ref[...]) + bg_ref[...])
    r['y3'] = r['y2'] * r['sg']
    r['y_ssm'] = r['y3'] * r['ss']
    r['y_pool'] = yp_ref[...] * r['sp']
    r['g'] = [_sigmoid(gl_ref[:, b * d:(b + 1) * d]) for b in range(3)]
    r['p'] = [_dot(r['y_att'], wba_ref[...]), _dot(r['y_ssm'], wbs_ref[...]), _dot(r['y_pool'], wbp_ref[...])]
    r['merged'] = r['g'][0] * r['p'][0] + r['g'][1] * r['p'][1] + r['g'][2] * r['p'][2]
    return r


def _merge_fwd(proj, o, y_lin, y_pool, x, ssm_d, w_glu, b_glu, wba, wbs, wbp, w_out, mod):
    l, d = x.shape
    tr = min(ROWS_MERGE, l)

    def body(gl_ref, za_ref, us_ref, zs_ref, zp_ref, o_ref, yl_ref, yp_ref, d_ref, wg_ref, bg_ref, wba_ref, wbs_ref,
             wbp_ref, wo_ref, mod_ref, x_ref, xn_ref):
        r = _merge_forward_math(gl_ref, za_ref, us_ref, zs_ref, zp_ref, o_ref, yl_ref, yp_ref, d_ref, wg_ref, bg_ref,
                                wba_ref, wbs_ref, wbp_ref)
        xn_ref[...] = x_ref[...] + mod_ref[:, 2 * d:3 * d] * _dot(r['merged'], wo_ref[...])

    blk = pl.BlockSpec((tr, d), lambda i: (i, 0))
    return pl.pallas_call(
        body, name="merge_fwd", grid=(l // tr,), in_specs=_merge_inputs(tr) + [blk], out_specs=blk,
        out_shape=jax.ShapeDtypeStruct((l, d), F32), compiler_params=_params(("parallel",)))(
            proj, proj, proj, proj, proj, o, y_lin, y_pool, ssm_d, w_glu, b_glu, wba, wbs, wbp, w_out, mod, x)


def _merge_bwd(proj, o, y_lin, y_pool, dxo, ssm_d, w_glu, b_glu, wba, wbs, wbp, w_out, mod):
    l, d = dxo.shape
    tr = min(ROWS_MERGE, l)

    def body(gl_ref, za_ref, us_ref, zs_ref, zp_ref, o_ref, yl_ref, yp_ref, d_ref, wg_ref, bg_ref, wba_ref, wbs_ref,
             wbp_ref, wo_ref, mod_ref, dxo_ref,
             do_ref, dza_ref, dyl_ref, dus_ref, dzs_ref, dyp_ref, dzp_ref, dgl_ref,
             ya_ref, ys_ref, ypl_ref, y2_ref, dt_ref, dpa_ref, dps_ref, dpp_ref, mg_ref, dout_ref, st_ref):
        @pl.when(pl.program_id(0) == 0)
        def _():
            st_ref[...] = jnp.zeros_like(st_ref)

        r = _merge_forward_math(gl_ref, za_ref, us_ref, zs_ref, zp_ref, o_ref, yl_ref, yp_ref, d_ref, wg_ref, bg_ref,
                                wba_ref, wbs_ref, wbp_ref)
        dxov = dxo_ref[...]
        out = _dot(r['merged'], wo_ref[...])
        st_ref[0:1, :] += jnp.sum(dxov * out, axis=0, keepdims=True)
        dout = dxov * mod_ref[:, 2 * d:3 * d]
        dmerged = _dot(dout, wo_ref[...], _NT)
        dys = []
        for b, (w_ref, dp_ref) in enumerate(((wba_ref, dpa_ref), (wbs_ref, dps_ref), (wbp_ref, dpp_ref))):
            gb = r['g'][b]
            dgl_ref[:, b * d:(b + 1) * d] = dmerged * r['p'][b] * gb * (1.0 - gb)
            dp = dmerged * gb
            dp_ref[...] = dp.astype(BF16)
            dys.append(_dot(dp, w_ref[...], _NT))
        mg_ref[...] = r['merged'].astype(BF16)
        dout_ref[...] = dout.astype(BF16)
        ya_ref[...] = r['y_att'].astype(BF16)
        ys_ref[...] = r['y_ssm'].astype(BF16)
        ypl_ref[...] = r['y_pool'].astype(BF16)
        y2_ref[...] = r['y2'].astype(BF16)
        do_ref[...] = dys[0] * r['sa']
        dza_ref[...] = dys[0] * o_ref[...] * r['dsa']
        dy3 = dys[1] * r['ss']
        dzs_ref[...] = dys[1] * r['y3'] * r['dss']
        dt = dy3 * r['y2'] * r['sg'] * (1.0 - r['sg'])
        dt_ref[...] = dt.astype(BF16)
        dy1 = (dy3 * r['sg'] + _dot(dt, wg_ref[...], _NT)) * r['dgelu']
        dyl_ref[...] = dy1
        dus_ref[...] = dy1 * d_ref[...]
        st_ref[1:2, 0:SSM_W] += jnp.sum(dt, axis=0, keepdims=True)
        st_ref[1:2, SSM_W:2 * SSM_W] += jnp.sum(dy1 * us_ref[...], axis=0, keepdims=True)
        dyp_ref[...] = dys[2] * r['sp']
        dzp_ref[...] = dys[2] * yp_ref[...] * r['dsp']

    blk = pl.BlockSpec((tr, d), lambda i: (i, 0))
    half = pl.BlockSpec((tr, ATT_W), lambda i: (i, 0))
    wide = pl.BlockSpec((tr, 3 * d), lambda i: (i, 0))
    sds = jax.ShapeDtypeStruct
    return pl.pallas_call(
        body, name="merge_bwd", grid=(l // tr,), in_specs=_merge_inputs(tr) + [blk],
        out_specs=[half] * 7 + [wide] + [half] * 5 + [blk] * 5 + [_full((SUBLANES, d))],
        out_shape=[sds((l, ATT_W), F32)] * 7 + [sds((l, 3 * d), F32)] + [sds((l, ATT_W), BF16)] * 5
        + [sds((l, d), BF16)] * 5 + [sds((SUBLANES, d), F32)],
        compiler_params=_params(("arbitrary",)))(
            proj, proj, proj, proj, proj, o, y_lin, y_pool, ssm_d, w_glu, b_glu, wba, wbs, wbp, w_out, mod, dxo)


def _ada_fwd(c_all, w_ada, b_shard):
    depth, d, n = w_ada.shape

    def body(c_ref, w_ref, b_ref, o_ref):
        act, _ = _silu_and_grad(c_ref[...])
        o_ref[0] = _dot(act, w_ref[0]) + b_ref[0]

    return pl.pallas_call(
        body, name="ada_fwd", grid=(depth,),
        in_specs=[_full((N_DEV, d)), pl.BlockSpec((1, d, n), lambda i: (i, 0, 0)), pl.BlockSpec((1, 1, n), lambda i: (i, 0, 0))],
        out_specs=pl.BlockSpec((1, N_DEV, n), lambda i: (i, 0, 0)), out_shape=jax.ShapeDtypeStruct((depth, N_DEV, n), F32),
        compiler_params=_params(("parallel",)))(c_all, w_ada, b_shard)


def _ada_bwd(c_all_t, dmod):
    d = c_all_t.shape[0]
    depth, _, n = dmod.shape

    def body(c_ref, dm_ref, o_ref):
        act, _ = _silu_and_grad(c_ref[...])
        acc = act[:, 0:1] * dm_ref[0, 0:1, :]
        for b in range(1, N_DEV):
            acc = acc + act[:, b:b + 1] * dm_ref[0, b:b + 1, :]
        o_ref[0] = acc

    return pl.pallas_call(
        body, name="ada_bwd", grid=(depth,),
        in_specs=[_full((d, N_DEV)), pl.BlockSpec((1, N_DEV, n), lambda i: (i, 0, 0))],
        out_specs=pl.BlockSpec((1, d, n), lambda i: (i, 0, 0)), out_shape=jax.ShapeDtypeStruct((depth, d, n), F32),
        compiler_params=_params(("parallel",)))(c_all_t, dmod)


ROWS_FLAT = 256


def _ordered_sum(stack, name):
    n, r, c = stack.shape
    tr = math.gcd(ROWS_FLAT, r)

    def body(s_ref, o_ref):
        acc = s_ref[0].astype(F32)
        for k in range(1, n):
            acc = acc + s_ref[k].astype(F32)
        o_ref[...] = acc

    return pl.pallas_call(
        body, name=name, grid=(r // tr,), in_specs=[pl.BlockSpec((n, tr, c), lambda i: (0, i, 0))],
        out_specs=pl.BlockSpec((tr, c), lambda i: (i, 0)), out_shape=jax.ShapeDtypeStruct((r, c), F32),
        compiler_params=_params(("parallel",)))(stack)


def _adamw(w, g, m, v, name):
    r, c = w.shape
    tr = math.gcd(ROWS_FLAT, r)

    def body(w_ref, g_ref, m_ref, v_ref, d_ref, nm_ref, nv_ref):
        gv = g_ref[...]
        mv = ADAM_B1 * m_ref[...] + (1.0 - ADAM_B1) * gv
        vv = ADAM_B2 * v_ref[...] + (1.0 - ADAM_B2) * (gv * gv)
        m_hat = mv / (1.0 - ADAM_B1 ** ADAM_STEP)
        v_hat = vv / (1.0 - ADAM_B2 ** ADAM_STEP)
        d_ref[...] = -ADAM_LR * (m_hat / (jnp.sqrt(v_hat) + ADAM_EPS) + ADAM_WD * w_ref[...])
        nm_ref[...] = mv
        nv_ref[...] = vv

    blk = pl.BlockSpec((tr, c), lambda i: (i, 0))
    return pl.pallas_call(
        body, name=name, grid=(r // tr,), in_specs=[blk] * 4, out_specs=[blk] * 3,
        out_shape=[jax.ShapeDtypeStruct((r, c), F32)] * 3, compiler_params=_params(("parallel",)))(w, g, m, v)


_GROUP_MASKS = {
    'xy': ((1, 0, 0), (0, 1, 0), (1, 1, 0)),
    'c': ((0, 0, 1),),
    'xyc': ((0, 0, 1), (0, 1, 0), (0, 1, 1), (1, 0, 0), (1, 0, 1), (1, 1, 0), (1, 1, 1)),
}


def _group_rank(group, pos):
    x, y, c = pos
    return {'xy': 2 * x + y, 'c': c, 'xyc': 4 * x + 2 * y + c}[group]


def _exchange(src, group, scatter, name):
    masks = _GROUP_MASKS[group]
    n = len(masks) + 1
    shape = src.shape[1:] if scatter else src.shape
    assert (not scatter) or src.shape[0] == n

    def body(src_ref, out_ref, send_sems, recv_sems, local_sem):
        me = (lax.axis_index("x"), lax.axis_index("y"), lax.axis_index("c"))
        my_rank = _group_rank(group, me)

        def mine_for(rank):
            return src_ref.at[rank] if scatter else src_ref

        local = pltpu.make_async_copy(mine_for(my_rank), out_ref.at[my_rank], local_sem)
        local.start()
        sends = []
        for k, mask in enumerate(masks):
            peer = tuple(1 - p if f else p for p, f in zip(me, mask))
            peer_rank = _group_rank(group, peer)
            send = pltpu.make_async_remote_copy(
                src_ref=mine_for(peer_rank), dst_ref=out_ref.at[my_rank], send_sem=send_sems.at[k],
                recv_sem=recv_sems.at[k], device_id=peer, device_id_type=pl.DeviceIdType.MESH)
            send.start()
            sends.append((send, peer, peer_rank))
        for k, (send, peer, peer_rank) in enumerate(sends):
            pltpu.make_async_remote_copy(
                src_ref=mine_for(peer_rank), dst_ref=out_ref.at[peer_rank], send_sem=send_sems.at[k],
                recv_sem=recv_sems.at[k], device_id=peer, device_id_type=pl.DeviceIdType.MESH).wait_recv()
        for send, _, _ in sends:
            send.wait_send()
        local.wait()

    return pl.pallas_call(
        body, name=name, in_specs=[pl.BlockSpec(memory_space=pl.ANY)], out_specs=pl.BlockSpec(memory_space=pl.ANY),
        out_shape=jax.ShapeDtypeStruct((n,) + tuple(shape), src.dtype),
        scratch_shapes=[pltpu.SemaphoreType.DMA((n - 1,)), pltpu.SemaphoreType.DMA((n - 1,)), pltpu.SemaphoreType.DMA(())],
    )(src)


def _pack_flat(pieces, cols, dtype):
    flat = jnp.concatenate([p.reshape(-1).astype(dtype) for p in pieces])
    pad = (-flat.shape[0]) % (SUBLANES * 2 * cols)
    return jnp.pad(flat, (0, pad)).reshape(-1, cols)


def _unpack_flat(buf, shapes):
    flat = buf.reshape(-1)
    out, off = [], 0
    for s in shapes:
        n = math.prod(s)
        out.append(flat[off:off + n].reshape(s))
        off += n
    return out


def _permute_cols(w):
    return jnp.concatenate([w[..., s:s + n] for s, n in _PERM_PIECES], axis=-1)


def _unpermute_cols(w):
    off, pieces = 0, []
    for s, n in _PERM_PIECES:
        pieces.append((s, w[..., off:off + n]))
        off += n
    return jnp.concatenate([p for _, p in sorted(pieces, key=lambda sp: sp[0])], axis=-1)


def _big_shard_shapes(depth):
    return [s for _ in range(depth) for _, s in BIG]


def _gathered_to_full(name, shards):
    if name in ('w_out', 'w_glu'):
        return shards.reshape(-1, shards.shape[-1])
    return jnp.transpose(shards, (1, 0, 2)).reshape(shards.shape[1], -1)


def _full_to_shards(name, full):
    if name in ('w_out', 'w_glu'):
        return full.reshape(N_CHIPS, -1, full.shape[-1])
    return jnp.transpose(full.reshape(full.shape[0], N_CHIPS, -1), (1, 0, 2))


def _layer_tables(p, li):
    raw = tuple(p[k][li] for k in ('ssm_a_re', 'ssm_a_im', 'ssm_log_dt', 'ssm_b_re', 'ssm_b_im', 'ssm_c_re', 'ssm_c_im'))
    (lam_blk, bb, cb), vjp = jax.vjp(_ssm_tables, *raw)
    del lam_blk
    return bb.astype(BF16), cb.astype(BF16), _ssm_powers(*raw[:3]), vjp


def _layer_fwd(x, p, full, li, mod):
    h = _norm_fwd(x, _row(p['norm_g'][li]), mod)
    proj = _matmul(h, full['w_in'], 'nn', tm=512, tn=1280, tk=1024, name="proj")
    bb, cb, pw, tables_vjp = _layer_tables(p, li)
    o = _attention_fwd(proj, p['attn_sinks'][li])
    y_lin = _ssm_fwd(proj, bb, cb, pw)
    y_pool = _pool_fwd(proj, p['w_pool'][li].astype(BF16), _row(p['pool_scale'][li]))
    x_new = _merge_fwd(proj, o, y_lin, y_pool, x, _row(p['ssm_d'][li]), full['w_glu'], _row(p['b_glu'][li]),
                       full['w_br_att'], full['w_br_ssm'], full['w_br_pool'], full['w_out'], mod)
    saved = dict(x=x, h=h, proj=proj, o=o, y_lin=y_lin, y_pool=y_pool, bb=bb, cb=cb, pw=pw, tables_vjp=tables_vjp)
    return x_new, saved


def _layer_bwd(dxo, s, p, full, li, mod):
    l = dxo.shape[0]
    proj = s['proj']
    (do, dza, dyl, dus_skip, dzs, dyp, dzp, dgl, ya, ys, ypl, y2, dt, dpa, dps, dpp, mg, dout, st) = _merge_bwd(
        proj, s['o'], s['y_lin'], s['y_pool'], dxo, _row(p['ssm_d'][li]), full['w_glu'], _row(p['b_glu'][li]),
        full['w_br_att'], full['w_br_ssm'], full['w_br_pool'], full['w_out'], mod)
    g = {}
    g['w_br_att'] = _matmul(ya, dpa, 'tn', tm=512, tn=1024, tk=512, name="grad_w_br")
    g['w_br_ssm'] = _matmul(ys, dps, 'tn', tm=512, tn=1024, tk=512, name="grad_w_br")
    g['w_br_pool'] = _matmul(ypl, dpp, 'tn', tm=512, tn=1024, tk=512, name="grad_w_br")
    g['w_out'] = _matmul(mg, dout, 'tn', tm=512, tn=1024, tk=512, name="grad_w_out")
    g['w_glu'] = _matmul(y2, dt, 'tn', tm=512, tn=512, tk=512, name="grad_w_glu")
    g['b_glu'] = st[1, 0:SSM_W]
    g['ssm_d'] = st[1, SSM_W:2 * SSM_W]
    dgate = st[0]

    dq, dkv, dsink = _attention_bwd(proj, p['attn_sinks'][li], do)
    g['attn_sinks'] = dsink[:, 0]
    dus_scan, dbb, dcb, dlam = _ssm_bwd(proj, dyl, s['bb'], s['cb'], s['pw'])
    raw = s['tables_vjp']((dlam[:, 0:2, :], dbb, dcb))
    for k, v in zip(('ssm_a_re', 'ssm_a_im', 'ssm_log_dt', 'ssm_b_re', 'ssm_b_im', 'ssm_c_re', 'ssm_c_im'), raw):
        g[k] = v
    dup, dwp, dps_scale = _pool_bwd(proj, p['w_pool'][li].astype(BF16), _row(p['pool_scale'][li]), dyp)
    g['w_pool'] = dwp
    g['pool_scale'] = dps_scale[0]

    dproj = jnp.concatenate([dgl, dq, dza, dus_skip + dus_scan, dzs, dup, dzp, dkv[HALO:]], axis=1)
    g['w_in'] = _unpermute_cols(_matmul(s['h'], dproj, 'tn', tm=512, tn=1280, tk=512, name="grad_w_in"))
    dh = _matmul(dproj, full['w_in'], 'nt', tm=512, tn=1024, tk=1280, name="grad_h")
    dx, nst = _norm_bwd(s['x'], dh, dxo, _row(p['norm_g'][li]), mod)
    g['norm_g'] = nst[2]
    dmod = jnp.concatenate([nst[0], nst[1], dgate])
    del l
    return dx, g, dmod


def _gather_big(p, depth):
    pieces = [p[name][li] for li in range(depth) for name, _ in BIG]
    pack = _pack_flat(pieces, PACK_COLS, BF16)
    gathered = _exchange(pack, 'xy', False, "gather_weights")
    per_chip = [_unpack_flat(gathered[r], _big_shard_shapes(depth)) for r in range(N_CHIPS)]
    fulls = []
    for li in range(depth):
        full = {}
        for bi, (name, _) in enumerate(BIG):
            shards = jnp.stack([per_chip[r][li * len(BIG) + bi] for r in range(N_CHIPS)])
            full[name] = _gathered_to_full(name, shards)
        full['w_in'] = _permute_cols(full['w_in'])
        fulls.append(full)
    return fulls


def _reduce_big(grads, depth):
    by_dest = [[] for _ in range(N_CHIPS)]
    for li in range(depth):
        for name, _ in BIG:
            shards = _full_to_shards(name, grads[li][name])
            for r in range(N_CHIPS):
                by_dest[r].append(shards[r])
    pack = jnp.stack([_pack_flat(pieces, PACK_COLS, F32) for pieces in by_dest])
    from_chips = _exchange(pack, 'xy', True, "scatter_weight_grads")
    partial = _ordered_sum(from_chips, "sum_over_chips")
    both = _exchange(partial, 'c', False, "swap_core_partials")
    total = _ordered_sum(both, "sum_over_cores")
    flat = _unpack_flat(total, _big_shard_shapes(depth))
    return [{name: flat[li * len(BIG) + bi] for bi, (name, _) in enumerate(BIG)} for li in range(depth)]


def _step(p, m, v, x, c, target):
    depth = p['norm_g'].shape[0]
    d = D_MODEL
    ix, iy, ic = lax.axis_index("x"), lax.axis_index("y"), lax.axis_index("c")
    chip = 2 * ix + iy
    dev = 4 * ix + 2 * iy + ic
    x0 = x[0]

    c_pad = jnp.pad(c, ((0, SUBLANES - 1), (0, 0)))
    c_all = _exchange(c_pad, 'xyc', False, "gather_c")[:, 0, :]
    n_ada = p['w_ada'].shape[-1]
    b_shard = lax.dynamic_slice_in_dim(p['b_ada'], chip * n_ada, n_ada, axis=1)[:, None, :]
    mod_shard = _ada_fwd(c_all, p['w_ada'].astype(BF16), b_shard)
    mod_all = _exchange(mod_shard.reshape(depth * N_DEV, n_ada), 'xy', False, "gather_mod")
    mod_all = jnp.transpose(mod_all.reshape(N_CHIPS, depth, N_DEV, n_ada), (1, 2, 0, 3)).reshape(depth, N_DEV, 3 * d)
    mods = lax.dynamic_index_in_dim(mod_all, dev, axis=1, keepdims=True)

    fulls = _gather_big(p, depth)

    saved = []
    xs = x0
    for li in range(depth):
        xs, s = _layer_fwd(xs, p, fulls[li], li, mods[li])
        saved.append(s)
    dx, hst = _loss_head(xs, _row(p['final_g']), target[0])
    loss = lax.psum(hst[1, 0], ("x", "y", "c"))

    grads = [None] * depth
    dmods = [None] * depth
    for li in reversed(range(depth)):
        dx, grads[li], dmods[li] = _layer_bwd(dx, saved[li], p, fulls[li], li, mods[li])

    dmod_pad = jnp.pad(jnp.stack(dmods), ((0, SUBLANES - depth), (0, 0)))
    dmod_all = _exchange(dmod_pad, 'xyc', False, "gather_dmod")[:, :depth, :]
    dmod_cols = lax.dynamic_slice_in_dim(jnp.transpose(dmod_all, (1, 0, 2)), chip * n_ada, n_ada, axis=2)
    g_w_ada = _ada_bwd(jnp.transpose(c_all), dmod_cols)

    local_small = {k: jnp.stack([grads[li][k] for li in range(depth)]) for k in SMALL if k not in ('final_g', 'b_ada')}
    local_small['final_g'] = hst[0]
    local_small['b_ada'] = jnp.stack(dmods)
    small_pack = _pack_flat([local_small[k] for k in SMALL], PACK_COLS, F32)
    small_sum = _ordered_sum(_exchange(small_pack, 'xyc', False, "gather_small_grads"), "sum_small_grads")
    g_small = dict(zip(SMALL, _unpack_flat(small_sum, [p[k].shape for k in SMALL])))

    g_big = _reduce_big(grads, depth)

    grad = dict(g_small)
    grad['w_ada'] = g_w_ada
    for name, _ in BIG:
        grad[name] = jnp.stack([g_big[li][name] for li in range(depth)])

    delta, new_m, new_v = {}, {}, {}
    packs = [_pack_flat([t[k] for k in SMALL], PACK_COLS, F32) for t in (p, grad, m, v)]
    outs = _adamw(*packs, name="adamw_small")
    for res, o in zip((delta, new_m, new_v), outs):
        res.update(zip(SMALL, _unpack_flat(o, [p[k].shape for k in SMALL])))
    for name in ['w_ada'] + [n for n, _ in BIG]:
        shape = p[name].shape
        two_d = (-1, shape[-1])
        outs = _adamw(p[name].reshape(two_d), grad[name].reshape(two_d), m[name].reshape(two_d), v[name].reshape(two_d),
                      name="adamw_" + name)
        delta[name], new_m[name], new_v[name] = (o.reshape(shape) for o in outs)

    return (loss, dx[None], *[grad[k] for k in WEIGHTS], *[delta[k] for k in WEIGHTS],
            *[new_m[k] for k in WEIGHTS], *[new_v[k] for k in WEIGHTS])


def kernel(x, c, norm_g, w_ada, b_ada, w_in, attn_sinks, ssm_a_re, ssm_a_im, ssm_log_dt, ssm_b_re, ssm_b_im, ssm_c_re, ssm_c_im, ssm_d, w_glu, b_glu, w_pool, pool_scale, w_br_att, w_br_ssm, w_br_pool, w_out, final_g, loss_target, m_norm_g, m_w_ada, m_b_ada, m_w_in, m_attn_sinks, m_ssm_a_re, m_ssm_a_im, m_ssm_log_dt, m_ssm_b_re, m_ssm_b_im, m_ssm_c_re, m_ssm_c_im, m_ssm_d, m_w_glu, m_b_glu, m_w_pool, m_pool_scale, m_w_br_att, m_w_br_ssm, m_w_br_pool, m_w_out, m_final_g, v_norm_g, v_w_ada, v_b_ada, v_w_in, v_attn_sinks, v_ssm_a_re, v_ssm_a_im, v_ssm_log_dt, v_ssm_b_re, v_ssm_b_im, v_ssm_c_re, v_ssm_c_im, v_ssm_d, v_w_glu, v_b_glu, v_w_pool, v_pool_scale, v_w_br_att, v_w_br_ssm, v_w_br_pool, v_w_out, v_final_g):
    p = dict(zip(WEIGHTS, (norm_g, w_ada, b_ada, w_in, attn_sinks, ssm_a_re, ssm_a_im, ssm_log_dt, ssm_b_re, ssm_b_im,
                           ssm_c_re, ssm_c_im, ssm_d, w_glu, b_glu, w_pool, pool_scale, w_br_att, w_br_ssm, w_br_pool,
                           w_out, final_g)))
    m = dict(zip(WEIGHTS, (m_norm_g, m_w_ada, m_b_ada, m_w_in, m_attn_sinks, m_ssm_a_re, m_ssm_a_im, m_ssm_log_dt,
                           m_ssm_b_re, m_ssm_b_im, m_ssm_c_re, m_ssm_c_im, m_ssm_d, m_w_glu, m_b_glu, m_w_pool,
                           m_pool_scale, m_w_br_att, m_w_br_ssm, m_w_br_pool, m_w_out, m_final_g)))
    v = dict(zip(WEIGHTS, (v_norm_g, v_w_ada, v_b_ada, v_w_in, v_attn_sinks, v_ssm_a_re, v_ssm_a_im, v_ssm_log_dt,
                           v_ssm_b_re, v_ssm_b_im, v_ssm_c_re, v_ssm_c_im, v_ssm_d, v_w_glu, v_b_glu, v_w_pool,
                           v_pool_scale, v_w_br_att, v_w_br_ssm, v_w_br_pool, v_w_out, v_final_g)))
    return _step(p, m, v, x, c, loss_target)
```

```python
import functools
import math

import jax
import jax.numpy as jnp
from jax import lax
from jax.experimental import pallas as pl
from jax.experimental.pallas import tpu as pltpu

F32 = jnp.float32
BF16 = jnp.bfloat16

D_MODEL = 1024
CHUNK = 64
N_HEADS = 8
N_KV_HEADS = 2
HEAD_DIM = 64
Q_PER_KV = N_HEADS // N_KV_HEADS
HALO = 128
ATT_W = 512
KV_W = 128
SSM_W = 512
SSM_GROUP = 16
SSM_GROUPS = 32
SSM_STATE = 64
POOL_W = 512
POOL_GW = 128
IN_W = 6400
EPS = 1e-6
NEG_INF = -1e30
ADAM_LR = 0.001
ADAM_B1 = 0.9
ADAM_B2 = 0.999
ADAM_EPS = 1e-08
ADAM_WD = 0.01
ADAM_STEP = 10

OFF_GL, OFF_Q, OFF_ZA, OFF_US, OFF_ZS, OFF_UP, OFF_ZP, OFF_KV = 0, 3072, 3584, 4096, 4608, 5120, 5632, 6144
_PERM_PIECES = ((3328, 3072), (0, 512), (1792, 512), (768, 512), (2304, 512), (1280, 512), (2816, 512), (512, 256))

LANES = 128
SUBLANES = 8
N_SBLK = SSM_GROUPS * SSM_STATE // LANES
VMEM_LIMIT = 48 * 1024 * 1024

N_CHIPS = 4
N_DEV = 8

WEIGHTS = ['norm_g', 'w_ada', 'b_ada', 'w_in', 'attn_sinks', 'ssm_a_re', 'ssm_a_im', 'ssm_log_dt', 'ssm_b_re',
           'ssm_b_im', 'ssm_c_re', 'ssm_c_im', 'ssm_d', 'w_glu', 'b_glu', 'w_pool', 'pool_scale', 'w_br_att',
           'w_br_ssm', 'w_br_pool', 'w_out', 'final_g']
SMALL = ['norm_g', 'b_ada', 'attn_sinks', 'ssm_a_re', 'ssm_a_im', 'ssm_log_dt', 'ssm_b_re', 'ssm_b_im', 'ssm_c_re',
         'ssm_c_im', 'ssm_d', 'b_glu', 'w_pool', 'pool_scale', 'final_g']
BIG = (('w_in', (1024, 1600)), ('w_br_att', (512, 256)), ('w_br_ssm', (512, 256)), ('w_br_pool', (512, 256)),
       ('w_out', (256, 1024)), ('w_glu', (128, 512)))
PACK_COLS = 1024


def _params(sem=None):
    return pltpu.CompilerParams(dimension_semantics=sem, vmem_limit_bytes=VMEM_LIMIT)


def _row(v):
    return v.reshape(1, -1)


def _full(shape):
    nd = len(shape)
    return pl.BlockSpec(shape, lambda *_: (0,) * nd)


def _sigmoid(v):
    return 1.0 / (1.0 + jnp.exp(-v))


def _silu_and_grad(z):
    s = _sigmoid(z)
    return z * s, s * (1.0 + z * (1.0 - s))


_GELU_K = math.sqrt(2.0 / math.pi)


def _gelu_and_grad(v):
    inner = _GELU_K * (v + 0.044715 * v * v * v)
    th = jnp.tanh(inner)
    val = 0.5 * v * (1.0 + th)
    grad = 0.5 * (1.0 + th) + 0.5 * v * (1.0 - th * th) * _GELU_K * (1.0 + 3 * 0.044715 * v * v)
    return val, grad


_NN = (((1,), (0,)), ((), ()))
_NT = (((1,), (1,)), ((), ()))
_TN = (((0,), (0,)), ((), ()))


def _dot(a, b, dims=_NN):
    return lax.dot_general(a.astype(BF16), b.astype(BF16), dims, preferred_element_type=F32)


def _matmul(a, b, mode, *, tm, tn, tk, name, out_dtype=F32):
    if mode == 'nn':
        (m, k), (_, n) = a.shape, b.shape
    elif mode == 'nt':
        (m, k), (n, _) = a.shape, b.shape
    else:
        (k, m), (_, n) = a.shape, b.shape
    tm, tn, tk = min(tm, m), min(tn, n), min(tk, k)
    assert m % tm == 0 and n % tn == 0 and k % tk == 0, (name, a.shape, b.shape)
    nk = k // tk
    dims = {'nn': _NN, 'nt': _NT, 'tn': _TN}[mode]

    def body(a_ref, b_ref, o_ref, acc_ref):
        kk = pl.program_id(2)

        @pl.when(kk == 0)
        def _():
            acc_ref[...] = jnp.zeros_like(acc_ref)

        acc_ref[...] += _dot(a_ref[...], b_ref[...], dims)

        @pl.when(kk == nk - 1)
        def _():
            o_ref[...] = acc_ref[...].astype(out_dtype)

    a_spec = pl.BlockSpec((tk, tm), lambda i, j, kk: (kk, i)) if mode == 'tn' else pl.BlockSpec((tm, tk), lambda i, j, kk: (i, kk))
    b_spec = pl.BlockSpec((tn, tk), lambda i, j, kk: (j, kk)) if mode == 'nt' else pl.BlockSpec((tk, tn), lambda i, j, kk: (kk, j))
    return pl.pallas_call(
        body, name=name, grid=(m // tm, n // tn, nk), in_specs=[a_spec, b_spec],
        out_specs=pl.BlockSpec((tm, tn), lambda i, j, kk: (i, j)),
        out_shape=jax.ShapeDtypeStruct((m, n), out_dtype), scratch_shapes=[pltpu.VMEM((tm, tn), F32)],
        compiler_params=_params(("parallel", "parallel", "arbitrary")))(a, b)


ROWS_NORM = 512


def _norm_fwd(x, g, mod):
    l, d = x.shape
    tr = min(ROWS_NORM, l)

    def body(x_ref, g_ref, mod_ref, h_ref):
        xv = x_ref[...]
        r = lax.rsqrt(jnp.mean(xv * xv, axis=-1, keepdims=True) + EPS)
        shift, scale = mod_ref[:, 0:d], mod_ref[:, d:2 * d]
        h_ref[...] = ((xv * r * g_ref[...]) * (1.0 + scale) + shift).astype(BF16)

    return pl.pallas_call(
        body, name="norm_fwd", grid=(l // tr,),
        in_specs=[pl.BlockSpec((tr, d), lambda i: (i, 0)), _full((1, d)), _full((1, 3 * d))],
        out_specs=pl.BlockSpec((tr, d), lambda i: (i, 0)), out_shape=jax.ShapeDtypeStruct((l, d), BF16),
        compiler_params=_params(("parallel",)))(x, g, mod)


def _norm_bwd(x, dh, dxo, g, mod):
    l, d = x.shape
    tr = min(ROWS_NORM, l)

    def body(x_ref, dh_ref, dxo_ref, g_ref, mod_ref, dx_ref, st_ref):
        @pl.when(pl.program_id(0) == 0)
        def _():
            st_ref[...] = jnp.zeros_like(st_ref)

        xv, dhv = x_ref[...], dh_ref[...]
        r = lax.rsqrt(jnp.mean(xv * xv, axis=-1, keepdims=True) + EPS)
        xn = xv * r
        gv = g_ref[...]
        sc1 = 1.0 + mod_ref[:, d:2 * d]
        dxn = dhv * gv * sc1
        dx_ref[...] = dxo_ref[...] + r * (dxn - xn * jnp.mean(dxn * xn, axis=-1, keepdims=True))
        st_ref[0:1, :] += jnp.sum(dhv, axis=0, keepdims=True)
        st_ref[1:2, :] += jnp.sum(dhv * xn * gv, axis=0, keepdims=True)
        st_ref[2:3, :] += jnp.sum(dhv * xn * sc1, axis=0, keepdims=True)

    blk = pl.BlockSpec((tr, d), lambda i: (i, 0))
    return pl.pallas_call(
        body, name="norm_bwd", grid=(l // tr,),
        in_specs=[blk, blk, blk, _full((1, d)), _full((1, 3 * d))],
        out_specs=[blk, _full((SUBLANES, d))],
        out_shape=[jax.ShapeDtypeStruct((l, d), F32), jax.ShapeDtypeStruct((SUBLANES, d), F32)],
        compiler_params=_params(("arbitrary",)))(x, dh, dxo, g, mod)


def _loss_head(x, g, target):
    l, d = x.shape
    tr = min(ROWS_NORM, l)

    def body(x_ref, g_ref, t_ref, dx_ref, st_ref):
        @pl.when(pl.program_id(0) == 0)
        def _():
            st_ref[...] = jnp.zeros_like(st_ref)

        xv = x_ref[...]
        r = lax.rsqrt(jnp.mean(xv * xv, axis=-1, keepdims=True) + EPS)
        xn = xv * r
        gv = g_ref[...]
        err = xn * gv - t_ref[...]
        part = 0.5 * jnp.sum(jnp.mean(err * err, axis=-1, keepdims=True), axis=0, keepdims=True)
        dy = err * (1.0 / d)
        dxn = dy * gv
        dx_ref[...] = r * (dxn - xn * jnp.mean(dxn * xn, axis=-1, keepdims=True))
        st_ref[0:1, :] += jnp.sum(dy * xn, axis=0, keepdims=True)
        st_ref[1:2, :] += jnp.broadcast_to(part, (1, d))

    blk = pl.BlockSpec((tr, d), lambda i: (i, 0))
    return pl.pallas_call(
        body, name="loss_head", grid=(l // tr,), in_specs=[blk, _full((1, d)), blk],
        out_specs=[blk, _full((SUBLANES, d))],
        out_shape=[jax.ShapeDtypeStruct((l, d), F32), jax.ShapeDtypeStruct((SUBLANES, d), F32)],
        compiler_params=_params(("arbitrary",)))(x, g, target)


ROWS_ATT = 256
_SLOPES = tuple(2.0 ** (-8.0 * (h + 1) / N_HEADS) for h in range(N_HEADS))


def _att_mask(i, t):
    r = lax.broadcasted_iota(jnp.int32, (t, t + HALO), 0)
    j = lax.broadcasted_iota(jnp.int32, (t, t + HALO), 1)
    dist = jnp.abs(r + HALO - j).astype(F32)
    rc, jc = r // CHUNK, j // CHUNK
    allowed = (jc >= rc) & (jc <= rc + 2) & ((j >= HALO) | (i > 0))
    return dist, allowed


def _att_probs(qh, k, dist, allowed, slope, sink):
    s = _dot(qh, k, _NT) * (1.0 / math.sqrt(HEAD_DIM)) - slope * dist
    s = jnp.where(allowed, s, NEG_INF)
    m = jnp.maximum(jnp.max(s, axis=1, keepdims=True), sink)
    e = jnp.exp(s - m)
    es = jnp.exp(sink - m)
    den = jnp.sum(e, axis=1, keepdims=True) + es
    return e / den, es / den


def _att_specs(t):
    q_spec = pl.BlockSpec((t, ATT_W), lambda i: (i, OFF_Q // ATT_W))
    kv_spec = pl.BlockSpec((t, 2 * KV_W), lambda i: (i, OFF_KV // (2 * KV_W)))
    halo_spec = pl.BlockSpec((HALO, 2 * KV_W), lambda i: (jnp.maximum(i * (t // HALO) - 1, 0), OFF_KV // (2 * KV_W)))
    return q_spec, kv_spec, halo_spec


def _attention_fwd(proj, sinks):
    l = proj.shape[0]
    t = min(ROWS_ATT, l)

    def body(sink_ref, q_ref, kv_ref, halo_ref, o_ref):
        dist, allowed = _att_mask(pl.program_id(0), t)
        kv = jnp.concatenate([halo_ref[...], kv_ref[...]], axis=0)
        for h in range(N_HEADS):
            kh = h // Q_PER_KV
            k = kv[:, kh * HEAD_DIM:(kh + 1) * HEAD_DIM]
            v = kv[:, KV_W + kh * HEAD_DIM:KV_W + (kh + 1) * HEAD_DIM]
            p, _ = _att_probs(q_ref[:, h * HEAD_DIM:(h + 1) * HEAD_DIM], k, dist, allowed, _SLOPES[h], sink_ref[h])
            o_ref[:, h * HEAD_DIM:(h + 1) * HEAD_DIM] = _dot(p, v)

    q_spec, kv_spec, halo_spec = _att_specs(t)
    return pl.pallas_call(
        body, name="attention_fwd", grid=(l // t,),
        in_specs=[pl.BlockSpec(memory_space=pltpu.SMEM), q_spec, kv_spec, halo_spec],
        out_specs=pl.BlockSpec((t, ATT_W), lambda i: (i, 0)), out_shape=jax.ShapeDtypeStruct((l, ATT_W), F32),
        compiler_params=_params(("parallel",)))(sinks, proj, proj, proj)


def _attention_bwd(proj, sinks, do):
    l = proj.shape[0]
    t = min(ROWS_ATT, l)

    def body(sink_ref, q_ref, kv_ref, halo_ref, do_ref, dq_ref, dkv_ref, dsink_ref):
        i = pl.program_id(0)

        @pl.when(i == 0)
        def _():
            dkv_ref[...] = jnp.zeros_like(dkv_ref)
            dsink_ref[...] = jnp.zeros_like(dsink_ref)

        dist, allowed = _att_mask(i, t)
        kv = jnp.concatenate([halo_ref[...], kv_ref[...]], axis=0)
        rows = pl.ds(pl.multiple_of(i * t, t), t + HALO)
        for kh in range(N_KV_HEADS):
            k = kv[:, kh * HEAD_DIM:(kh + 1) * HEAD_DIM]
            v = kv[:, KV_W + kh * HEAD_DIM:KV_W + (kh + 1) * HEAD_DIM]
            dk = jnp.zeros((t + HALO, HEAD_DIM), F32)
            dv = jnp.zeros((t + HALO, HEAD_DIM), F32)
            for h in range(kh * Q_PER_KV, (kh + 1) * Q_PER_KV):
                qh = q_ref[:, h * HEAD_DIM:(h + 1) * HEAD_DIM]
                doh = do_ref[:, h * HEAD_DIM:(h + 1) * HEAD_DIM]
                p, ps = _att_probs(qh, k, dist, allowed, _SLOPES[h], sink_ref[h])
                dp = _dot(doh, v, _NT)
                delta = jnp.sum(p * dp, axis=1, keepdims=True)
                ds = p * (dp - delta) * (1.0 / math.sqrt(HEAD_DIM))
                dsink_ref[h:h + 1, :] += jnp.broadcast_to(-jnp.sum(ps * delta, axis=0, keepdims=True), (1, LANES))
                dq_ref[:, h * HEAD_DIM:(h + 1) * HEAD_DIM] = _dot(ds, k)
                dk = dk + _dot(ds, qh, _TN)
                dv = dv + _dot(p, doh, _TN)
            dkv_ref[rows, kh * HEAD_DIM:(kh + 1) * HEAD_DIM] += dk
            dkv_ref[rows, KV_W + kh * HEAD_DIM:KV_W + (kh + 1) * HEAD_DIM] += dv

    q_spec, kv_spec, halo_spec = _att_specs(t)
    blk = pl.BlockSpec((t, ATT_W), lambda i: (i, 0))
    return pl.pallas_call(
        body, name="attention_bwd", grid=(l // t,),
        in_specs=[pl.BlockSpec(memory_space=pltpu.SMEM), q_spec, kv_spec, halo_spec, blk],
        out_specs=[blk, _full((HALO + l, 2 * KV_W)), _full((N_HEADS, LANES))],
        out_shape=[jax.ShapeDtypeStruct((l, ATT_W), F32), jax.ShapeDtypeStruct((HALO + l, 2 * KV_W), F32),
                   jax.ShapeDtypeStruct((N_HEADS, LANES), F32)],
        compiler_params=_params(("arbitrary",)))(sinks, proj, proj, proj, do)


ROWS_SSM = 512
SCAN_UNROLL = 4


def _ssm_discretize(a_re, a_im, log_dt, b_re, b_im):
    lam = lax.complex(a_re, a_im)
    dt = jnp.exp(log_dt)[:, None]
    lam_bar = jnp.exp(lam * dt)
    b_bar = ((lam_bar - 1.0) / lam)[..., None] * lax.complex(b_re, b_im)
    return lam, dt, lam_bar, b_bar


def _ssm_block_diag(m):
    e = m.reshape(N_SBLK, 2, SSM_GROUP, SSM_STATE)
    e = e[:, :, :, None, :] * jnp.eye(2, dtype=m.dtype)[None, :, None, :, None]
    e = e.reshape(N_SBLK, 2 * SSM_GROUP, LANES)
    oh = jax.nn.one_hot(jnp.arange(N_SBLK) % 4, 4, dtype=m.dtype)
    return (oh[:, :, None, None] * e[:, None]).reshape(N_SBLK, LANES, LANES)


def _ssm_tables(a_re, a_im, log_dt, b_re, b_im, c_re, c_im):
    _, _, lam_bar, b_bar = _ssm_discretize(a_re, a_im, log_dt, b_re, b_im)
    lam_blk = jnp.stack([jnp.real(lam_bar).reshape(N_SBLK, LANES), jnp.imag(lam_bar).reshape(N_SBLK, LANES)], axis=1)
    bt = jnp.transpose(b_bar, (0, 2, 1))
    bb = jnp.concatenate([_ssm_block_diag(jnp.real(bt)), _ssm_block_diag(jnp.imag(bt))], axis=2)
    cb = jnp.concatenate([jnp.transpose(_ssm_block_diag(c_re), (0, 2, 1)),
                          jnp.transpose(_ssm_block_diag(-c_im), (0, 2, 1))], axis=1)
    return lam_blk, bb, cb


def _ssm_powers(a_re, a_im, log_dt):
    lam = lax.complex(a_re, a_im)
    dt = jnp.exp(log_dt)[:, None]
    k = jnp.arange(1, SUBLANES + 1, dtype=F32)
    pw = jnp.exp((lam * dt)[None] * k[:, None, None]).reshape(SUBLANES, N_SBLK, LANES)
    pw = jnp.transpose(pw, (1, 0, 2))
    rev = pw[:, ::-1]
    return jnp.concatenate([jnp.real(pw), jnp.imag(pw), jnp.real(rev), jnp.imag(rev)], axis=1)


def _scan_consts(pw_ref, reverse):
    row = lax.broadcasted_iota(jnp.int32, (SUBLANES, LANES), 0)
    sign = -1.0 if reverse else 1.0

    def power(k):
        return (jnp.broadcast_to(pw_ref[0, k - 1:k, :], (SUBLANES, LANES)),
                sign * jnp.broadcast_to(pw_ref[0, SUBLANES + k - 1:SUBLANES + k, :], (SUBLANES, LANES)))

    steps = []
    for d in (1, 2, 4):
        pr, pi = power(d)
        keep = (row < SUBLANES - d) if reverse else (row >= d)
        steps.append((d, jnp.where(keep, pr, 0.0), jnp.where(keep, pi, 0.0)))
    base = 2 * SUBLANES if reverse else 0
    return (steps, pw_ref[0, base:base + SUBLANES, :], sign * pw_ref[0, base + SUBLANES:base + 2 * SUBLANES, :],
            power(SUBLANES))


def _scan_tile(xr, xi, steps, reverse):
    for d, ar, ai in steps:
        shift = SUBLANES - d if reverse else d
        rr, ri = pltpu.roll(xr, shift, 0), pltpu.roll(xi, shift, 0)
        xr, xi = xr + (ar * rr - ai * ri), xi + (ar * ri + ai * rr)
    return xr, xi


def _bcast_row(v, r):
    return jnp.broadcast_to(v[r:r + 1, :], (SUBLANES, LANES))


def _scan_forward(s_scr, pw_ref, n_tiles):
    steps, pr, pi, (l8r, l8i) = _scan_consts(pw_ref, False)

    def tile(t, carry):
        cr, ci = carry
        rows = pl.ds(pl.multiple_of(t * SUBLANES, SUBLANES), SUBLANES)
        xr, xi = _scan_tile(s_scr[rows, 0:LANES], s_scr[rows, LANES:2 * LANES], steps, False)
        s_scr[rows, 0:LANES] = xr + (pr * cr - pi * ci)
        s_scr[rows, LANES:2 * LANES] = xi + (pr * ci + pi * cr)
        return ((l8r * cr - l8i * ci) + _bcast_row(xr, SUBLANES - 1), (l8r * ci + l8i * cr) + _bcast_row(xi, SUBLANES - 1))

    zero = jnp.zeros((SUBLANES, LANES), F32)
    lax.fori_loop(0, n_tiles, tile, (zero, zero), unroll=SCAN_UNROLL)


def _row_chunks(l):
    rc = min(ROWS_SSM, l)
    return rc, l // rc


def _ssm_fwd(proj, bb, cb, pw):
    l = proj.shape[0]
    rc, n_chunks = _row_chunks(l)

    def body(u_ref, bb_ref, cb_ref, pw_ref, y_ref, s_scr):
        j = pl.program_id(0)

        def fill(ci, _):
            rows = pl.ds(pl.multiple_of(ci * rc, rc), rc)
            s_scr[rows, :] = _dot(u_ref[rows, :], bb_ref[0])
            return 0

        lax.fori_loop(0, n_chunks, fill, 0)
        _scan_forward(s_scr, pw_ref, l // SUBLANES)

        @pl.when(j % 4 == 0)
        def _():
            y_ref[...] = jnp.zeros_like(y_ref)

        def emit(ci, _):
            rows = pl.ds(pl.multiple_of(ci * rc, rc), rc)
            y_ref[rows, :] += _dot(s_scr[rows, :], cb_ref[0])
            return 0

        lax.fori_loop(0, n_chunks, emit, 0)

    return pl.pallas_call(
        body, name="ssm_fwd", grid=(N_SBLK,),
        in_specs=[pl.BlockSpec((l, LANES), lambda j: (0, OFF_US // LANES + j // 4)),
                  pl.BlockSpec((1, LANES, 2 * LANES), lambda j: (j, 0, 0)),
                  pl.BlockSpec((1, 2 * LANES, LANES), lambda j: (j, 0, 0)),
                  pl.BlockSpec((1, 4 * SUBLANES, LANES), lambda j: (j, 0, 0))],
        out_specs=pl.BlockSpec((l, LANES), lambda j: (0, j // 4)),
        out_shape=jax.ShapeDtypeStruct((l, SSM_W), F32), scratch_shapes=[pltpu.VMEM((l, 2 * LANES), F32)],
        compiler_params=_params(("arbitrary",)))(proj, bb, cb, pw)


def _ssm_bwd(proj, dy, bb, cb, pw):
    l = proj.shape[0]
    rc, n_chunks = _row_chunks(l)
    n_tiles = l // SUBLANES

    def body(u_ref, dy_ref, bb_ref, cb_ref, pw_ref, du_ref, dbb_ref, dcb_ref, dlam_ref, s_scr, a_scr):
        j = pl.program_id(0)

        def fill(ci, _):
            rows = pl.ds(pl.multiple_of(ci * rc, rc), rc)
            s_scr[rows, :] = _dot(u_ref[rows, :], bb_ref[0])
            return 0

        lax.fori_loop(0, n_chunks, fill, 0)
        _scan_forward(s_scr, pw_ref, n_tiles)

        dcb_ref[...] = jnp.zeros_like(dcb_ref)

        def through_c(ci, _):
            rows = pl.ds(pl.multiple_of(ci * rc, rc), rc)
            dyv = dy_ref[rows, :]
            dcb_ref[0] += _dot(s_scr[rows, :], dyv, _TN)
            a_scr[rows, :] = _dot(dyv, cb_ref[0], _NT)
            return 0

        lax.fori_loop(0, n_chunks, through_c, 0)

        steps, pr, pi, (l8r, l8i) = _scan_consts(pw_ref, True)
        row = lax.broadcasted_iota(jnp.int32, (SUBLANES, LANES), 0)

        def tile(k, carry):
            cr, ci, accr, acci = carry
            t = n_tiles - 1 - k
            rows = pl.ds(pl.multiple_of(t * SUBLANES, SUBLANES), SUBLANES)
            prev = pl.ds(pl.multiple_of(jnp.maximum(t - 1, 0) * SUBLANES, SUBLANES), SUBLANES)
            xr, xi = _scan_tile(a_scr[rows, 0:LANES], a_scr[rows, LANES:2 * LANES], steps, True)
            gr, gi = xr + (pr * cr - pi * ci), xi + (pr * ci + pi * cr)
            a_scr[rows, 0:LANES] = gr
            a_scr[rows, LANES:2 * LANES] = gi
            first = row == 0
            edge = jnp.where(t > 0, 1.0, 0.0)
            spr = jnp.where(first, edge * pltpu.roll(s_scr[prev, 0:LANES], 1, 0), pltpu.roll(s_scr[rows, 0:LANES], 1, 0))
            spi = jnp.where(first, edge * pltpu.roll(s_scr[prev, LANES:2 * LANES], 1, 0),
                            pltpu.roll(s_scr[rows, LANES:2 * LANES], 1, 0))
            return ((l8r * cr - l8i * ci) + _bcast_row(xr, 0), (l8r * ci + l8i * cr) + _bcast_row(xi, 0),
                    accr + (gr * spr + gi * spi), acci + (gi * spr - gr * spi))

        zero = jnp.zeros((SUBLANES, LANES), F32)
        _, _, accr, acci = lax.fori_loop(0, n_tiles, tile, (zero, zero, zero, zero), unroll=SCAN_UNROLL // 2)
        dlr = jnp.broadcast_to(jnp.sum(accr, axis=0, keepdims=True), (SUBLANES, LANES))
        dli = jnp.broadcast_to(jnp.sum(acci, axis=0, keepdims=True), (SUBLANES, LANES))
        dlam_ref[0] = jnp.where(row == 0, dlr, jnp.where(row == 1, dli, 0.0))

        dbb_ref[...] = jnp.zeros_like(dbb_ref)

        @pl.when(j % 4 == 0)
        def _():
            du_ref[...] = jnp.zeros_like(du_ref)

        def through_b(ci, _):
            rows = pl.ds(pl.multiple_of(ci * rc, rc), rc)
            av = a_scr[rows, :]
            dbb_ref[0] += _dot(u_ref[rows, :], av, _TN)
            du_ref[rows, :] += _dot(av, bb_ref[0], _NT)
            return 0

        lax.fori_loop(0, n_chunks, through_b, 0)

    return pl.pallas_call(
        body, name="ssm_bwd", grid=(N_SBLK,),
        in_specs=[pl.BlockSpec((l, LANES), lambda j: (0, OFF_US // LANES + j // 4)),
                  pl.BlockSpec((l, LANES), lambda j: (0, j // 4)),
                  pl.BlockSpec((1, LANES, 2 * LANES), lambda j: (j, 0, 0)),
                  pl.BlockSpec((1, 2 * LANES, LANES), lambda j: (j, 0, 0)),
                  pl.BlockSpec((1, 4 * SUBLANES, LANES), lambda j: (j, 0, 0))],
        out_specs=[pl.BlockSpec((l, LANES), lambda j: (0, j // 4)),
                   pl.BlockSpec((1, LANES, 2 * LANES), lambda j: (j, 0, 0)),
                   pl.BlockSpec((1, 2 * LANES, LANES), lambda j: (j, 0, 0)),
                   pl.BlockSpec((1, SUBLANES, LANES), lambda j: (j, 0, 0))],
        out_shape=[jax.ShapeDtypeStruct((l, SSM_W), F32), jax.ShapeDtypeStruct((N_SBLK, LANES, 2 * LANES), F32),
                   jax.ShapeDtypeStruct((N_SBLK, 2 * LANES, LANES), F32), jax.ShapeDtypeStruct((N_SBLK, SUBLANES, LANES), F32)],
        scratch_shapes=[pltpu.VMEM((l, 2 * LANES), F32), pltpu.VMEM((l, 2 * LANES), F32)],
        compiler_params=_params(("arbitrary",)))(proj, dy, bb, cb, pw)


def _pool_windows(v, t, l, ahead):
    def shifted(a, d):
        if ahead:
            return jnp.where(t < l - d, pltpu.roll(a, l - d, 0), 0.0)
        return jnp.where(t >= d, pltpu.roll(a, d, 0), 0.0)

    w2 = v + shifted(v, 1)
    w4 = w2 + shifted(w2, 2)
    w8 = w4 + shifted(w4, 4)
    w16 = w8 + shifted(w8, 8)
    return w2, w4, w8, w16


def _pool_select(g, ws):
    return jnp.where(g == 0, ws[0], jnp.where(g == 1, ws[1], jnp.where(g == 2, ws[2], ws[3])))


def _pool_count(g, t):
    return jnp.minimum(t + 1, jnp.left_shift(2, g)).astype(F32)


def _pool_specs(l):
    return [pl.BlockSpec((l, POOL_GW), lambda g: (0, OFF_UP // POOL_GW + g)),
            pl.BlockSpec((1, POOL_GW, POOL_GW), lambda g: (g, 0, 0)),
            pl.BlockSpec((1, POOL_GW), lambda g: (0, g))]


def _pool_fwd(proj, w_pool, scale):
    l = proj.shape[0]

    def body(u_ref, w_ref, sc_ref, y_ref):
        g = pl.program_id(0)
        t = lax.broadcasted_iota(jnp.int32, (l, 1), 0)
        u = u_ref[...]
        pooled = _pool_select(g, _pool_windows(u, t, l, False)) / _pool_count(g, t) - u
        y_ref[...] = _dot(pooled, w_ref[0]) * sc_ref[...]

    return pl.pallas_call(
        body, name="pool_fwd", grid=(4,), in_specs=_pool_specs(l),
        out_specs=pl.BlockSpec((l, POOL_GW), lambda g: (0, g)), out_shape=jax.ShapeDtypeStruct((l, POOL_W), F32),
        compiler_params=_params(("parallel",)))(proj, w_pool, scale)


def _pool_bwd(proj, w_pool, scale, dy):
    l = proj.shape[0]

    def body(u_ref, w_ref, sc_ref, dy_ref, du_ref, dw_ref, dsc_ref):
        g = pl.program_id(0)
        t = lax.broadcasted_iota(jnp.int32, (l, 1), 0)
        u = u_ref[...]
        cnt = _pool_count(g, t)
        pooled = _pool_select(g, _pool_windows(u, t, l, False)) / cnt - u
        dyv = dy_ref[...]
        dsc_ref[...] = jnp.sum(dyv * _dot(pooled, w_ref[0]), axis=0, keepdims=True)
        dyl = dyv * sc_ref[...]
        dw_ref[0] = _dot(pooled, dyl, _TN)
        dpooled = _dot(dyl, w_ref[0], _NT)
        du_ref[...] = _pool_select(g, _pool_windows(dpooled / cnt, t, l, True)) - dpooled

    return pl.pallas_call(
        body, name="pool_bwd", grid=(4,), in_specs=_pool_specs(l) + [pl.BlockSpec((l, POOL_GW), lambda g: (0, g))],
        out_specs=[pl.BlockSpec((l, POOL_GW), lambda g: (0, g)), pl.BlockSpec((1, POOL_GW, POOL_GW), lambda g: (g, 0, 0)),
                   pl.BlockSpec((1, POOL_GW), lambda g: (0, g))],
        out_shape=[jax.ShapeDtypeStruct((l, POOL_W), F32), jax.ShapeDtypeStruct((4, POOL_GW, POOL_GW), F32),
                   jax.ShapeDtypeStruct((1, POOL_W), F32)],
        compiler_params=_params(("parallel",)))(proj, w_pool, scale, dy)


ROWS_MERGE = 128


def _merge_inputs(tr):
    def col(off, w):
        return pl.BlockSpec((tr, w), lambda i: (i, off // w))

    def act(w):
        return pl.BlockSpec((tr, w), lambda i: (i, 0))

    d = D_MODEL
    return ([col(OFF_GL, 3 * d), col(OFF_ZA, ATT_W), col(OFF_US, SSM_W), col(OFF_ZS, SSM_W), col(OFF_ZP, POOL_W),
             act(ATT_W), act(SSM_W), act(POOL_W)]
            + [_full((1, SSM_W)), _full((SSM_W, SSM_W)), _full((1, SSM_W)), _full((ATT_W, d)), _full((SSM_W, d)),
               _full((POOL_W, d)), _full((d, d)), _full((1, 3 * d))])


def _merge_forward_math(gl_ref, za_ref, us_ref, zs_ref, zp_ref, o_ref, yl_ref, yp_ref, d_ref, wg_ref, bg_ref,
                        wba_ref, wbs_ref, wbp_ref):
    d = D_MODEL
    r = {}
    r['sa'], r['dsa'] = _silu_and_grad(za_ref[...])
    r['ss'], r['dss'] = _silu_and_grad(zs_ref[...])
    r['sp'], r['dsp'] = _silu_and_grad(zp_ref[...])
    r['y_att'] = o_ref[...] * r['sa']
    y1 = yl_ref[...] + d_ref[...] * us_ref[...]
    r['y2'], r['dgelu'] = _gelu_and_grad(y1)
    r['sg'] = _sigmoid(_dot(r['y2'], wg_ref[...]) + bg_ref[...])
    r['y3'] = r['y2'] * r['sg']
    r['y_ssm'] = r['y3'] * r['ss']
    r['y_pool'] = yp_ref[...] * r['sp']
    r['g'] = [_sigmoid(gl_ref[:, b * d:(b + 1) * d]) for b in range(3)]
    r['p'] = [_dot(r['y_att'], wba_ref[...]), _dot(r['y_ssm'], wbs_ref[...]), _dot(r['y_pool'], wbp_ref[...])]
    r['merged'] = r['g'][0] * r['p'][0] + r['g'][1] * r['p'][1] + r['g'][2] * r['p'][2]
    return r


def _merge_fwd(proj, o, y_lin, y_pool, x, ssm_d, w_glu, b_glu, wba, wbs, wbp, w_out, mod):
    l, d = x.shape
    tr = min(ROWS_MERGE, l)

    def body(gl_ref, za_ref, us_ref, zs_ref, zp_ref, o_ref, yl_ref, yp_ref, d_ref, wg_ref, bg_ref, wba_ref, wbs_ref,
             wbp_ref, wo_ref, mod_ref, x_ref, xn_ref):
        r = _merge_forward_math(gl_ref, za_ref, us_ref, zs_ref, zp_ref, o_ref, yl_ref, yp_ref, d_ref, wg_ref, bg_ref,
                                wba_ref, wbs_ref, wbp_ref)
        xn_ref[...] = x_ref[...] + mod_ref[:, 2 * d:3 * d] * _dot(r['merged'], wo_ref[...])

    blk = pl.BlockSpec((tr, d), lambda i: (i, 0))
    return pl.pallas_call(
        body, name="merge_fwd", grid=(l // tr,), in_specs=_merge_inputs(tr) + [blk], out_specs=blk,
        out_shape=jax.ShapeDtypeStruct((l, d), F32), compiler_params=_params(("parallel",)))(
            proj, proj, proj, proj, proj, o, y_lin, y_pool, ssm_d, w_glu, b_glu, wba, wbs, wbp, w_out, mod, x)


def _merge_bwd(proj, o, y_lin, y_pool, dxo, ssm_d, w_glu, b_glu, wba, wbs, wbp, w_out, mod):
    l, d = dxo.shape
    tr = min(ROWS_MERGE, l)

    def body(gl_ref, za_ref, us_ref, zs_ref, zp_ref, o_ref, yl_ref, yp_ref, d_ref, wg_ref, bg_ref, wba_ref, wbs_ref,
             wbp_ref, wo_ref, mod_ref, dxo_ref,
             do_ref, dza_ref, dyl_ref, dus_ref, dzs_ref, dyp_ref, dzp_ref, dgl_ref,
             ya_ref, ys_ref, ypl_ref, y2_ref, dt_ref, dpa_ref, dps_ref, dpp_ref, mg_ref, dout_ref, st_ref):
        @pl.when(pl.program_id(0) == 0)
        def _():
            st_ref[...] = jnp.zeros_like(st_ref)

        r = _merge_forward_math(gl_ref, za_ref, us_ref, zs_ref, zp_ref, o_ref, yl_ref, yp_ref, d_ref, wg_ref, bg_ref,
                                wba_ref, wbs_ref, wbp_ref)
        dxov = dxo_ref[...]
        out = _dot(r['merged'], wo_ref[...])
        st_ref[0:1, :] += jnp.sum(dxov * out, axis=0, keepdims=True)
        dout = dxov * mod_ref[:, 2 * d:3 * d]
        dmerged = _dot(dout, wo_ref[...], _NT)
        dys = []
        for b, (w_ref, dp_ref) in enumerate(((wba_ref, dpa_ref), (wbs_ref, dps_ref), (wbp_ref, dpp_ref))):
            gb = r['g'][b]
            dgl_ref[:, b * d:(b + 1) * d] = dmerged * r['p'][b] * gb * (1.0 - gb)
            dp = dmerged * gb
            dp_ref[...] = dp.astype(BF16)
            dys.append(_dot(dp, w_ref[...], _NT))
        mg_ref[...] = r['merged'].astype(BF16)
        dout_ref[...] = dout.astype(BF16)
        ya_ref[...] = r['y_att'].astype(BF16)
        ys_ref[...] = r['y_ssm'].astype(BF16)
        ypl_ref[...] = r['y_pool'].astype(BF16)
        y2_ref[...] = r['y2'].astype(BF16)
        do_ref[...] = dys[0] * r['sa']
        dza_ref[...] = dys[0] * o_ref[...] * r['dsa']
        dy3 = dys[1] * r['ss']
        dzs_ref[...] = dys[1] * r['y3'] * r['dss']
        dt = dy3 * r['y2'] * r['sg'] * (1.0 - r['sg'])
        dt_ref[...] = dt.astype(BF16)
        dy1 = (dy3 * r['sg'] + _dot(dt, wg_ref[...], _NT)) * r['dgelu']
        dyl_ref[...] = dy1
        dus_ref[...] = dy1 * d_ref[...]
        st_ref[1:2, 0:SSM_W] += jnp.sum(dt, axis=0, keepdims=True)
        st_ref[1:2, SSM_W:2 * SSM_W] += jnp.sum(dy1 * us_ref[...], axis=0, keepdims=True)
        dyp_ref[...] = dys[2] * r['sp']
        dzp_ref[...] = dys[2] * yp_ref[...] * r['dsp']

    blk = pl.BlockSpec((tr, d), lambda i: (i, 0))
    half = pl.BlockSpec((tr, ATT_W), lambda i: (i, 0))
    wide = pl.BlockSpec((tr, 3 * d), lambda i: (i, 0))
    sds = jax.ShapeDtypeStruct
    return pl.pallas_call(
        body, name="merge_bwd", grid=(l // tr,), in_specs=_merge_inputs(tr) + [blk],
        out_specs=[half] * 7 + [wide] + [half] * 5 + [blk] * 5 + [_full((SUBLANES, d))],
        out_shape=[sds((l, ATT_W), F32)] * 7 + [sds((l, 3 * d), F32)] + [sds((l, ATT_W), BF16)] * 5
        + [sds((l, d), BF16)] * 5 + [sds((SUBLANES, d), F32)],
        compiler_params=_params(("arbitrary",)))(
            proj, proj, proj, proj, proj, o, y_lin, y_pool, ssm_d, w_glu, b_glu, wba, wbs, wbp, w_out, mod, dxo)


def _ada_fwd(c_all, w_ada, b_shard):
    depth, d, n = w_ada.shape

    def body(c_ref, w_ref, b_ref, o_ref):
        act, _ = _silu_and_grad(c_ref[...])
        o_ref[0] = _dot(act, w_ref[0]) + b_ref[0]

    return pl.pallas_call(
        body, name="ada_fwd", grid=(depth,),
        in_specs=[_full((N_DEV, d)), pl.BlockSpec((1, d, n), lambda i: (i, 0, 0)), pl.BlockSpec((1, 1, n), lambda i: (i, 0, 0))],
        out_specs=pl.BlockSpec((1, N_DEV, n), lambda i: (i, 0, 0)), out_shape=jax.ShapeDtypeStruct((depth, N_DEV, n), F32),
        compiler_params=_params(("parallel",)))(c_all, w_ada, b_shard)


def _ada_bwd(c_all_t, dmod):
    d = c_all_t.shape[0]
    depth, _, n = dmod.shape

    def body(c_ref, dm_ref, o_ref):
        act, _ = _silu_and_grad(c_ref[...])
        acc = act[:, 0:1] * dm_ref[0, 0:1, :]
        for b in range(1, N_DEV):
            acc = acc + act[:, b:b + 1] * dm_ref[0, b:b + 1, :]
        o_ref[0] = acc

    return pl.pallas_call(
        body, name="ada_bwd", grid=(depth,),
        in_specs=[_full((d, N_DEV)), pl.BlockSpec((1, N_DEV, n), lambda i: (i, 0, 0))],
        out_specs=pl.BlockSpec((1, d, n), lambda i: (i, 0, 0)), out_shape=jax.ShapeDtypeStruct((depth, d, n), F32),
        compiler_params=_params(("parallel",)))(c_all_t, dmod)


ROWS_FLAT = 256


def _ordered_sum(stack, name):
    n, r, c = stack.shape
    tr = math.gcd(ROWS_FLAT, r)

    def body(s_ref, o_ref):
        acc = s_ref[0].astype(F32)
        for k in range(1, n):
            acc = acc + s_ref[k].astype(F32)
        o_ref[...] = acc

    return pl.pallas_call(
        body, name=name, grid=(r // tr,), in_specs=[pl.BlockSpec((n, tr, c), lambda i: (0, i, 0))],
        out_specs=pl.BlockSpec((tr, c), lambda i: (i, 0)), out_shape=jax.ShapeDtypeStruct((r, c), F32),
        compiler_params=_params(("parallel",)))(stack)


def _add2(a, b, out_dtype, name):
    shape = a.shape
    a, b = a.reshape(-1, shape[-1]), b.reshape(-1, shape[-1])
    r, c = a.shape
    tr = math.gcd(ROWS_FLAT, r)

    def body(a_ref, b_ref, o_ref):
        o_ref[...] = (a_ref[...] + b_ref[...]).astype(out_dtype)

    blk = pl.BlockSpec((tr, c), lambda i: (i, 0))
    return pl.pallas_call(
        body, name=name, grid=(r // tr,), in_specs=[blk, blk], out_specs=blk,
        out_shape=jax.ShapeDtypeStruct((r, c), out_dtype), compiler_params=_params(("parallel",)))(a, b).reshape(shape)


def _adamw(w, g, m, v, name):
    r, c = w.shape
    tr = math.gcd(ROWS_FLAT, r)

    def body(w_ref, g_ref, m_ref, v_ref, d_ref, nm_ref, nv_ref):
        gv = g_ref[...]
        mv = ADAM_B1 * m_ref[...] + (1.0 - ADAM_B1) * gv
        vv = ADAM_B2 * v_ref[...] + (1.0 - ADAM_B2) * (gv * gv)
        m_hat = mv / (1.0 - ADAM_B1 ** ADAM_STEP)
        v_hat = vv / (1.0 - ADAM_B2 ** ADAM_STEP)
        d_ref[...] = -ADAM_LR * (m_hat / (jnp.sqrt(v_hat) + ADAM_EPS) + ADAM_WD * w_ref[...])
        nm_ref[...] = mv
        nv_ref[...] = vv

    blk = pl.BlockSpec((tr, c), lambda i: (i, 0))
    return pl.pallas_call(
        body, name=name, grid=(r // tr,), in_specs=[blk] * 4, out_specs=[blk] * 3,
        out_shape=[jax.ShapeDtypeStruct((r, c), F32)] * 3, compiler_params=_params(("parallel",)))(w, g, m, v)


_GROUP_MASKS = {
    'xy': ((1, 0, 0), (0, 1, 0), (1, 1, 0)),
    'c': ((0, 0, 1),),
    'xyc': ((0, 0, 1), (0, 1, 0), (0, 1, 1), (1, 0, 0), (1, 0, 1), (1, 1, 0), (1, 1, 1)),
}


def _group_rank(group, pos):
    x, y, c = pos
    return {'xy': 2 * x + y, 'c': c, 'xyc': 4 * x + 2 * y + c}[group]


def _exchange(src, group, scatter, name):
    masks = _GROUP_MASKS[group]
    n = len(masks) + 1
    shape = src.shape[1:] if scatter else src.shape
    assert (not scatter) or src.shape[0] == n

    def body(src_ref, out_ref, send_sems, recv_sems, local_sem):
        me = (lax.axis_index("x"), lax.axis_index("y"), lax.axis_index("c"))
        my_rank = _group_rank(group, me)

        def mine_for(rank):
            return src_ref.at[rank] if scatter else src_ref

        local = pltpu.make_async_copy(mine_for(my_rank), out_ref.at[my_rank], local_sem)
        local.start()
        sends = []
        for k, mask in enumerate(masks):
            peer = tuple(1 - p if f else p for p, f in zip(me, mask))
            peer_rank = _group_rank(group, peer)
            send = pltpu.make_async_remote_copy(
                src_ref=mine_for(peer_rank), dst_ref=out_ref.at[my_rank], send_sem=send_sems.at[k],
                recv_sem=recv_sems.at[k], device_id=peer, device_id_type=pl.DeviceIdType.MESH)
            send.start()
            sends.append((send, peer, peer_rank))
        for k, (send, peer, peer_rank) in enumerate(sends):
            pltpu.make_async_remote_copy(
                src_ref=mine_for(peer_rank), dst_ref=out_ref.at[peer_rank], send_sem=send_sems.at[k],
                recv_sem=recv_sems.at[k], device_id=peer, device_id_type=pl.DeviceIdType.MESH).wait_recv()
        for send, _, _ in sends:
            send.wait_send()
        local.wait()

    return pl.pallas_call(
        body, name=name, in_specs=[pl.BlockSpec(memory_space=pl.ANY)], out_specs=pl.BlockSpec(memory_space=pl.ANY),
        out_shape=jax.ShapeDtypeStruct((n,) + tuple(shape), src.dtype),
        scratch_shapes=[pltpu.SemaphoreType.DMA((n - 1,)), pltpu.SemaphoreType.DMA((n - 1,)), pltpu.SemaphoreType.DMA(())],
    )(src)


CHUNK_BYTES = 1 << 20
MIN_CHUNK_ROWS = 64


def _row_parts(rows, cols, itemsize):
    n = 1
    while rows % (2 * n) == 0 and rows // (2 * n) >= MIN_CHUNK_ROWS and rows * cols * itemsize > n * CHUNK_BYTES:
        n *= 2
    return n


def _remote(src, dst, send_sem, recv_sem, peer):
    return pltpu.make_async_remote_copy(src_ref=src, dst_ref=dst, send_sem=send_sem, recv_sem=recv_sem,
                                        device_id=peer, device_id_type=pl.DeviceIdType.MESH)


def _start_rows(src, dst, send_sem, recv_sem, peer):
    rows, cols = src.shape
    n = _row_parts(rows, cols, jnp.dtype(src.dtype).itemsize)
    pr = rows // n
    for i in range(n):
        _remote(src.at[pl.ds(i * pr, pr), :], dst.at[pl.ds(i * pr, pr), :], send_sem, recv_sem, peer).start()


def _mesh_place():
    x, y, c = lax.axis_index("x"), lax.axis_index("y"), lax.axis_index("c")
    other_chips = ((1 - x, y), (x, 1 - y), (1 - x, 1 - y))
    return x, y, c, 2 * x + y, (x, y, 1 - c), other_chips


def _comm_call(body, name, ins, out_shapes, sem_counts):
    any_spec = pl.BlockSpec(memory_space=pl.ANY)
    return pl.pallas_call(
        body, name=name, in_specs=[any_spec] * len(ins), out_specs=[any_spec] * len(out_shapes), out_shape=out_shapes,
        scratch_shapes=[pltpu.SemaphoreType.DMA((n,)) for n in sem_counts])(*ins)


def _gather_weights(shards):
    nt = len(shards)

    def body(*refs):
        ins, outs = refs[:nt], refs[nt:2 * nt]
        send_a, recv_a, send_b, recv_b, local_sems = refs[2 * nt:]
        _, _, c, chip, sibling, other_chips = _mesh_place()
        local = []
        for t in range(nt):
            r2 = ins[t].shape[1] // 2
            for h in range(2):
                local.append(pltpu.make_async_copy(ins[t].at[:, pl.ds(h * r2, r2), :], outs[t].at[h, chip],
                                                   local_sems.at[2 * t + h]))
                local[-1].start()
        for t in range(nt):
            nl, r2 = ins[t].shape[0], ins[t].shape[1] // 2
            for j, (px, py) in enumerate(other_chips):
                for li in range(nl):
                    _remote(ins[t].at[li, pl.ds(pl.multiple_of(c * r2, MIN_CHUNK_ROWS), r2), :], outs[t].at[c, chip, li],
                            send_a.at[3 * t + j], recv_a.at[3 * t + j], (px, py, c)).start()
        for t in range(nt):
            for j, (px, py) in enumerate(other_chips):
                landed = outs[t].at[c, 2 * px + py]
                _remote(landed, landed, send_a.at[3 * t + j], recv_a.at[3 * t + j], (px, py, c)).wait_recv()
                for li in range(ins[t].shape[0]):
                    _start_rows(landed.at[li], landed.at[li], send_b.at[3 * t + j], recv_b.at[3 * t + j], sibling)
        for t in range(nt):
            for j, (px, py) in enumerate(other_chips):
                theirs = outs[t].at[1 - c, 2 * px + py]
                _remote(theirs, theirs, send_b.at[3 * t + j], recv_b.at[3 * t + j], sibling).wait_recv()
        for t in range(nt):
            for j, (px, py) in enumerate(other_chips):
                sent = outs[t].at[c, 2 * px + py]
                _remote(sent, sent, send_a.at[3 * t + j], recv_a.at[3 * t + j], (px, py, c)).wait_send()
                _remote(sent, sent, send_b.at[3 * t + j], recv_b.at[3 * t + j], sibling).wait_send()
        for cp in local:
            cp.wait()

    out_shapes = [jax.ShapeDtypeStruct((2, N_CHIPS, s.shape[0], s.shape[1] // 2, s.shape[2]), s.dtype) for s in shards]
    return _comm_call(body, "gather_weights", shards, out_shapes, (3 * nt, 3 * nt, 3 * nt, 3 * nt, 2 * nt))


def _swap_halves(gs):
    nt = len(gs)

    def body(*refs):
        ins, outs = refs[:nt], refs[nt:2 * nt]
        send_sems, recv_sems = refs[2 * nt:]
        _, _, c, _, sibling, _ = _mesh_place()
        for t in range(nt):
            _, nd, nl, _, _ = ins[t].shape
            for q in range(nd):
                for li in range(nl):
                    _start_rows(ins[t].at[1 - c, q, li], outs[t].at[q, li], send_sems.at[t], recv_sems.at[t], sibling)
        for t in range(nt):
            _remote(outs[t], outs[t], send_sems.at[t], recv_sems.at[t], sibling).wait_recv()
        for t in range(nt):
            _remote(outs[t], outs[t], send_sems.at[t], recv_sems.at[t], sibling).wait_send()

    out_shapes = [jax.ShapeDtypeStruct(g.shape[1:], g.dtype) for g in gs]
    return _comm_call(body, "swap_halves", gs, out_shapes, (nt, nt))


def _scatter_chips(hs):
    nt = len(hs)

    def body(*refs):
        ins, outs = refs[:nt], refs[nt:2 * nt]
        send_sems, recv_sems, local_sems = refs[2 * nt:]
        _, _, c, chip, _, other_chips = _mesh_place()

        def slab(t, q):
            return ins[t].at[q if ins[t].shape[0] == N_CHIPS else 0]

        local = []
        for t in range(nt):
            local.append(pltpu.make_async_copy(slab(t, chip), outs[t].at[chip], local_sems.at[t]))
            local[-1].start()
        for t in range(nt):
            for j, (px, py) in enumerate(other_chips):
                for li in range(ins[t].shape[1]):
                    _remote(slab(t, 2 * px + py).at[li], outs[t].at[chip, li], send_sems.at[3 * t + j],
                            recv_sems.at[3 * t + j], (px, py, c)).start()
        for t in range(nt):
            for j, (px, py) in enumerate(other_chips):
                landed = outs[t].at[2 * px + py]
                _remote(landed, landed, send_sems.at[3 * t + j], recv_sems.at[3 * t + j], (px, py, c)).wait_recv()
        for t in range(nt):
            for j, (px, py) in enumerate(other_chips):
                sent = outs[t].at[2 * px + py]
                _remote(sent, sent, send_sems.at[3 * t + j], recv_sems.at[3 * t + j], (px, py, c)).wait_send()
        for cp in local:
            cp.wait()

    out_shapes = [jax.ShapeDtypeStruct((N_CHIPS,) + h.shape[1:], h.dtype) for h in hs]
    return _comm_call(body, "scatter_chips", hs, out_shapes, (3 * nt, 3 * nt, nt))


def _gather_cores(halves):
    nt = len(halves)

    def body(*refs):
        ins, outs = refs[:nt], refs[nt:2 * nt]
        send_sems, recv_sems, local_sems = refs[2 * nt:]
        _, _, c, _, sibling, _ = _mesh_place()
        local = []
        for t in range(nt):
            local.append(pltpu.make_async_copy(ins[t], outs[t].at[c], local_sems.at[t]))
            local[-1].start()
        for t in range(nt):
            for li in range(ins[t].shape[0]):
                _start_rows(ins[t].at[li], outs[t].at[c, li], send_sems.at[t], recv_sems.at[t], sibling)
        for t in range(nt):
            theirs = outs[t].at[1 - c]
            _remote(theirs, theirs, send_sems.at[t], recv_sems.at[t], sibling).wait_recv()
        for t in range(nt):
            _remote(ins[t], ins[t], send_sems.at[t], recv_sems.at[t], sibling).wait_send()
        for cp in local:
            cp.wait()

    out_shapes = [jax.ShapeDtypeStruct((2,) + h.shape, h.dtype) for h in halves]
    return _comm_call(body, "gather_cores", halves, out_shapes, (nt, nt, nt))


def _pack_flat(pieces, cols, dtype):
    flat = jnp.concatenate([p.reshape(-1).astype(dtype) for p in pieces])
    pad = (-flat.shape[0]) % (SUBLANES * 2 * cols)
    return jnp.pad(flat, (0, pad)).reshape(-1, cols)


def _unpack_flat(buf, shapes):
    flat = buf.reshape(-1)
    out, off = [], 0
    for s in shapes:
        n = math.prod(s)
        out.append(flat[off:off + n].reshape(s))
        off += n
    return out


def _permute_cols(w):
    return jnp.concatenate([w[..., s:s + n] for s, n in _PERM_PIECES], axis=-1)


def _unpermute_cols(w):
    off, pieces = 0, []
    for s, n in _PERM_PIECES:
        pieces.append((s, w[..., off:off + n]))
        off += n
    return jnp.concatenate([p for _, p in sorted(pieces, key=lambda sp: sp[0])], axis=-1)


ROW_SHARDED = ('w_out', 'w_glu')


def _assemble_full(name, g):
    nh, nc, nl, r2, cols = g.shape
    if name in ROW_SHARDED:
        return jnp.transpose(g, (2, 1, 0, 3, 4)).reshape(nl, nc * nh * r2, cols)
    return jnp.transpose(g, (2, 0, 3, 1, 4)).reshape(nl, nh * r2, nc * cols)


def _split_by_dest(name, full):
    nl, rows, cols = full.shape
    if name in ROW_SHARDED:
        return jnp.transpose(full.reshape(nl, N_CHIPS, 2, rows // (2 * N_CHIPS), cols), (2, 1, 0, 3, 4))
    return jnp.transpose(full.reshape(nl, 2, rows // 2, N_CHIPS, cols // N_CHIPS), (1, 3, 0, 2, 4))


def _layer_tables(p, li):
    raw = tuple(p[k][li] for k in ('ssm_a_re', 'ssm_a_im', 'ssm_log_dt', 'ssm_b_re', 'ssm_b_im', 'ssm_c_re', 'ssm_c_im'))
    (lam_blk, bb, cb), vjp = jax.vjp(_ssm_tables, *raw)
    del lam_blk
    return bb.astype(BF16), cb.astype(BF16), _ssm_powers(*raw[:3]), vjp


def _layer_fwd(x, p, full, li, mod):
    h = _norm_fwd(x, _row(p['norm_g'][li]), mod)
    proj = _matmul(h, full['w_in'], 'nn', tm=512, tn=1280, tk=1024, name="proj")
    bb, cb, pw, tables_vjp = _layer_tables(p, li)
    o = _attention_fwd(proj, p['attn_sinks'][li])
    y_lin = _ssm_fwd(proj, bb, cb, pw)
    y_pool = _pool_fwd(proj, p['w_pool'][li].astype(BF16), _row(p['pool_scale'][li]))
    x_new = _merge_fwd(proj, o, y_lin, y_pool, x, _row(p['ssm_d'][li]), full['w_glu'], _row(p['b_glu'][li]),
                       full['w_br_att'], full['w_br_ssm'], full['w_br_pool'], full['w_out'], mod)
    saved = dict(x=x, h=h, proj=proj, o=o, y_lin=y_lin, y_pool=y_pool, bb=bb, cb=cb, pw=pw, tables_vjp=tables_vjp)
    return x_new, saved


def _layer_bwd(dxo, s, p, full, li, mod):
    l = dxo.shape[0]
    proj = s['proj']
    (do, dza, dyl, dus_skip, dzs, dyp, dzp, dgl, ya, ys, ypl, y2, dt, dpa, dps, dpp, mg, dout, st) = _merge_bwd(
        proj, s['o'], s['y_lin'], s['y_pool'], dxo, _row(p['ssm_d'][li]), full['w_glu'], _row(p['b_glu'][li]),
        full['w_br_att'], full['w_br_ssm'], full['w_br_pool'], full['w_out'], mod)
    g = {}
    g['w_br_att'] = _matmul(ya, dpa, 'tn', tm=512, tn=1024, tk=512, name="grad_w_br")
    g['w_br_ssm'] = _matmul(ys, dps, 'tn', tm=512, tn=1024, tk=512, name="grad_w_br")
    g['w_br_pool'] = _matmul(ypl, dpp, 'tn', tm=512, tn=1024, tk=512, name="grad_w_br")
    g['w_out'] = _matmul(mg, dout, 'tn', tm=512, tn=1024, tk=512, name="grad_w_out")
    g['w_glu'] = _matmul(y2, dt, 'tn', tm=512, tn=512, tk=512, name="grad_w_glu")
    g['b_glu'] = st[1, 0:SSM_W]
    g['ssm_d'] = st[1, SSM_W:2 * SSM_W]
    dgate = st[0]

    dq, dkv, dsink = _attention_bwd(proj, p['attn_sinks'][li], do)
    g['attn_sinks'] = dsink[:, 0]
    dus_scan, dbb, dcb, dlam = _ssm_bwd(proj, dyl, s['bb'], s['cb'], s['pw'])
    raw = s['tables_vjp']((dlam[:, 0:2, :], dbb, dcb))
    for k, v in zip(('ssm_a_re', 'ssm_a_im', 'ssm_log_dt', 'ssm_b_re', 'ssm_b_im', 'ssm_c_re', 'ssm_c_im'), raw):
        g[k] = v
    dup, dwp, dps_scale = _pool_bwd(proj, p['w_pool'][li].astype(BF16), _row(p['pool_scale'][li]), dyp)
    g['w_pool'] = dwp
    g['pool_scale'] = dps_scale[0]

    dproj = jnp.concatenate([dgl, dq, dza, dus_skip + dus_scan, dzs, dup, dzp, dkv[HALO:]], axis=1)
    g['w_in'] = _unpermute_cols(_matmul(s['h'], dproj, 'tn', tm=512, tn=1280, tk=512, name="grad_w_in"))
    dh = _matmul(dproj, full['w_in'], 'nt', tm=512, tn=1024, tk=1280, name="grad_h")
    dx, nst = _norm_bwd(s['x'], dh, dxo, _row(p['norm_g'][li]), mod)
    g['norm_g'] = nst[2]
    dmod = jnp.concatenate([nst[0], nst[1], dgate])
    del l
    return dx, g, dmod


def _gather_big(p, depth):
    gathered = _gather_weights([p[name].astype(BF16) for name, _ in BIG])
    fulls = []
    for li in range(depth):
        full = {name: _assemble_full(name, g[:, :, li:li + 1])[0] for (name, _), g in zip(BIG, gathered)}
        full['w_in'] = _permute_cols(full['w_in'])
        fulls.append(full)
    return fulls


def _reduce_grads(grads, small_pack, core, depth):
    names = [name for name, _ in BIG]
    mine = [_split_by_dest(name, jnp.stack([grads[li][name] for li in range(depth)])) for name in names]
    mine.append(small_pack.reshape(2, 1, 1, small_pack.shape[0] // 2, small_pack.shape[1]))
    theirs = _swap_halves(mine)
    wire = [BF16] * len(names) + [F32]
    pair_sums = [_add2(lax.dynamic_index_in_dim(g, core, 0, keepdims=False), a, dt, "sum_core_pair")
                 for g, a, dt in zip(mine, theirs, wire)]
    by_chip = _scatter_chips(pair_sums)
    halves = [_ordered_sum(b.reshape(N_CHIPS, -1, b.shape[-1]), "sum_over_chips").reshape(b.shape[1:]) for b in by_chip]
    both = _gather_cores(halves)
    out = {name: jnp.transpose(o, (1, 0, 2, 3)).reshape(o.shape[1], 2 * o.shape[2], o.shape[3])
           for name, o in zip(names, both[:-1])}
    return out, both[-1].reshape(small_pack.shape)


def _step(p, m, v, x, c, target):
    depth = p['norm_g'].shape[0]
    d = D_MODEL
    ix, iy, ic = lax.axis_index("x"), lax.axis_index("y"), lax.axis_index("c")
    chip = 2 * ix + iy
    dev = 4 * ix + 2 * iy + ic
    x0 = x[0]

    c_pad = jnp.pad(c, ((0, SUBLANES - 1), (0, 0)))
    c_all = _exchange(c_pad, 'xyc', False, "gather_c")[:, 0, :]
    n_ada = p['w_ada'].shape[-1]
    b_shard = lax.dynamic_slice_in_dim(p['b_ada'], chip * n_ada, n_ada, axis=1)[:, None, :]
    mod_shard = _ada_fwd(c_all, p['w_ada'].astype(BF16), b_shard)
    mod_all = _exchange(mod_shard.reshape(depth * N_DEV, n_ada), 'xy', False, "gather_mod")
    mod_all = jnp.transpose(mod_all.reshape(N_CHIPS, depth, N_DEV, n_ada), (1, 2, 0, 3)).reshape(depth, N_DEV, 3 * d)
    mods = lax.dynamic_index_in_dim(mod_all, dev, axis=1, keepdims=True)

    fulls = _gather_big(p, depth)

    saved = []
    xs = x0
    for li in range(depth):
        xs, s = _layer_fwd(xs, p, fulls[li], li, mods[li])
        saved.append(s)
    dx, hst = _loss_head(xs, _row(p['final_g']), target[0])
    loss = lax.psum(hst[1, 0], ("x", "y", "c"))

    grads = [None] * depth
    dmods = [None] * depth
    for li in reversed(range(depth)):
        dx, grads[li], dmods[li] = _layer_bwd(dx, saved[li], p, fulls[li], li, mods[li])

    dmod_pad = jnp.pad(jnp.stack(dmods), ((0, SUBLANES - depth), (0, 0)))
    dmod_all = _exchange(dmod_pad, 'xyc', False, "gather_dmod")[:, :depth, :]
    dmod_cols = lax.dynamic_slice_in_dim(jnp.transpose(dmod_all, (1, 0, 2)), chip * n_ada, n_ada, axis=2)
    g_w_ada = _ada_bwd(jnp.transpose(c_all), dmod_cols)

    local_small = {k: jnp.stack([grads[li][k] for li in range(depth)]) for k in SMALL if k not in ('final_g', 'b_ada')}
    local_small['final_g'] = hst[0]
    local_small['b_ada'] = jnp.stack(dmods)
    small_pack = _pack_flat([local_small[k] for k in SMALL], PACK_COLS, F32)
    g_big, small_sum = _reduce_grads(grads, small_pack, ic, depth)

    grad = dict(zip(SMALL, _unpack_flat(small_sum, [p[k].shape for k in SMALL])))
    grad['w_ada'] = g_w_ada
    grad.update(g_big)

    delta, new_m, new_v = {}, {}, {}
    packs = [_pack_flat([t[k] for k in SMALL], PACK_COLS, F32) for t in (p, grad, m, v)]
    outs = _adamw(*packs, name="adamw_small")
    for res, o in zip((delta, new_m, new_v), outs):
        res.update(zip(SMALL, _unpack_flat(o, [p[k].shape for k in SMALL])))
    for name in ['w_ada'] + [n for n, _ in BIG]:
        shape = p[name].shape
        two_d = (-1, shape[-1])
        outs = _adamw(p[name].reshape(two_d), grad[name].reshape(two_d), m[name].reshape(two_d), v[name].reshape(two_d),
                      name="adamw_" + name)
        delta[name], new_m[name], new_v[name] = (o.reshape(shape) for o in outs)

    return (loss, dx[None], *[grad[k] for k in WEIGHTS], *[delta[k] for k in WEIGHTS],
            *[new_m[k] for k in WEIGHTS], *[new_v[k] for k in WEIGHTS])


def kernel(x, c, norm_g, w_ada, b_ada, w_in, attn_sinks, ssm_a_re, ssm_a_im, ssm_log_dt, ssm_b_re, ssm_b_im, ssm_c_re, ssm_c_im, ssm_d, w_glu, b_glu, w_pool, pool_scale, w_br_att, w_br_ssm, w_br_pool, w_out, final_g, loss_target, m_norm_g, m_w_ada, m_b_ada, m_w_in, m_attn_sinks, m_ssm_a_re, m_ssm_a_im, m_ssm_log_dt, m_ssm_b_re, m_ssm_b_im, m_ssm_c_re, m_ssm_c_im, m_ssm_d, m_w_glu, m_b_glu, m_w_pool, m_pool_scale, m_w_br_att, m_w_br_ssm, m_w_br_pool, m_w_out, m_final_g, v_norm_g, v_w_ada, v_b_ada, v_w_in, v_attn_sinks, v_ssm_a_re, v_ssm_a_im, v_ssm_log_dt, v_ssm_b_re, v_ssm_b_im, v_ssm_c_re, v_ssm_c_im, v_ssm_d, v_w_glu, v_b_glu, v_w_pool, v_pool_scale, v_w_br_att, v_w_br_ssm, v_w_br_pool, v_w_out, v_final_g):
    p = dict(zip(WEIGHTS, (norm_g, w_ada, b_ada, w_in, attn_sinks, ssm_a_re, ssm_a_im, ssm_log_dt, ssm_b_re, ssm_b_im,
                           ssm_c_re, ssm_c_im, ssm_d, w_glu, b_glu, w_pool, pool_scale, w_br_att, w_br_ssm, w_br_pool,
                           w_out, final_g)))
    m = dict(zip(WEIGHTS, (m_norm_g, m_w_ada, m_b_ada, m_w_in, m_attn_sinks, m_ssm_a_re, m_ssm_a_im, m_ssm_log_dt,
                           m_ssm_b_re, m_ssm_b_im, m_ssm_c_re, m_ssm_c_im, m_ssm_d, m_w_glu, m_b_glu, m_w_pool,
                           m_pool_scale, m_w_br_att, m_w_br_ssm, m_w_br_pool, m_w_out, m_final_g)))
    v = dict(zip(WEIGHTS, (v_norm_g, v_w_ada, v_b_ada, v_w_in, v_attn_sinks, v_ssm_a_re, v_ssm_a_im, v_ssm_log_dt,
                           v_ssm_b_re, v_ssm_b_im, v_ssm_c_re, v_ssm_c_im, v_ssm_d, v_w_glu, v_b_glu, v_w_pool,
                           v_pool_scale, v_w_br_att, v_w_br_ssm, v_w_br_pool, v_w_out, v_final_g)))
    return _step(p, m, v, x, c, loss_target)
```

```python
import functools
import math

import jax
import jax.numpy as jnp
from jax import lax
from jax.experimental import pallas as pl
from jax.experimental.pallas import tpu as pltpu

F32 = jnp.float32
BF16 = jnp.bfloat16

D_MODEL = 1024
CHUNK = 64
N_HEADS = 8
N_KV_HEADS = 2
HEAD_DIM = 64
Q_PER_KV = N_HEADS // N_KV_HEADS
HALO = 128
ATT_W = 512
KV_W = 128
SSM_W = 512
SSM_GROUP = 16
SSM_GROUPS = 32
SSM_STATE = 64
POOL_W = 512
POOL_GW = 128
IN_W = 6400
EPS = 1e-6
NEG_INF = -1e30
ADAM_LR = 0.001
ADAM_B1 = 0.9
ADAM_B2 = 0.999
ADAM_EPS = 1e-08
ADAM_WD = 0.01
ADAM_STEP = 10

OFF_GL, OFF_Q, OFF_ZA, OFF_US, OFF_ZS, OFF_UP, OFF_ZP, OFF_KV = 0, 3072, 3584, 4096, 4608, 5120, 5632, 6144
_PERM_PIECES = ((3328, 3072), (0, 512), (1792, 512), (768, 512), (2304, 512), (1280, 512), (2816, 512), (512, 256))

LANES = 128
SUBLANES = 8
N_SBLK = SSM_GROUPS * SSM_STATE // LANES
VMEM_LIMIT = 48 * 1024 * 1024

N_CHIPS = 4
N_DEV = 8

WEIGHTS = ['norm_g', 'w_ada', 'b_ada', 'w_in', 'attn_sinks', 'ssm_a_re', 'ssm_a_im', 'ssm_log_dt', 'ssm_b_re',
           'ssm_b_im', 'ssm_c_re', 'ssm_c_im', 'ssm_d', 'w_glu', 'b_glu', 'w_pool', 'pool_scale', 'w_br_att',
           'w_br_ssm', 'w_br_pool', 'w_out', 'final_g']
SMALL = ['norm_g', 'b_ada', 'attn_sinks', 'ssm_a_re', 'ssm_a_im', 'ssm_log_dt', 'ssm_b_re', 'ssm_b_im', 'ssm_c_re',
         'ssm_c_im', 'ssm_d', 'b_glu', 'w_pool', 'pool_scale', 'final_g']
BIG = (('w_in', (1024, 1600)), ('w_br_att', (512, 256)), ('w_br_ssm', (512, 256)), ('w_br_pool', (512, 256)),
       ('w_out', (256, 1024)), ('w_glu', (128, 512)))
PACK_COLS = 1024


def _params(sem=None):
    return pltpu.CompilerParams(dimension_semantics=sem, vmem_limit_bytes=VMEM_LIMIT)


def _row(v):
    return v.reshape(1, -1)


def _full(shape):
    nd = len(shape)
    return pl.BlockSpec(shape, lambda *_: (0,) * nd)


def _sigmoid(v):
    return 1.0 / (1.0 + jnp.exp(-v))


def _silu_and_grad(z):
    s = _sigmoid(z)
    return z * s, s * (1.0 + z * (1.0 - s))


_GELU_K = math.sqrt(2.0 / math.pi)


def _gelu_and_grad(v):
    inner = _GELU_K * (v + 0.044715 * v * v * v)
    th = jnp.tanh(inner)
    val = 0.5 * v * (1.0 + th)
    grad = 0.5 * (1.0 + th) + 0.5 * v * (1.0 - th * th) * _GELU_K * (1.0 + 3 * 0.044715 * v * v)
    return val, grad


_NN = (((1,), (0,)), ((), ()))
_NT = (((1,), (1,)), ((), ()))
_TN = (((0,), (0,)), ((), ()))


def _dot(a, b, dims=_NN):
    return lax.dot_general(a.astype(BF16), b.astype(BF16), dims, preferred_element_type=F32)


def _matmul(a, b, mode, *, tm, tn, tk, name, out_dtype=F32):
    if mode == 'nn':
        (m, k), (_, n) = a.shape, b.shape
    elif mode == 'nt':
        (m, k), (n, _) = a.shape, b.shape
    else:
        (k, m), (_, n) = a.shape, b.shape
    tm, tn, tk = min(tm, m), min(tn, n), min(tk, k)
    assert m % tm == 0 and n % tn == 0 and k % tk == 0, (name, a.shape, b.shape)
    nk = k // tk
    dims = {'nn': _NN, 'nt': _NT, 'tn': _TN}[mode]

    def body(a_ref, b_ref, o_ref, acc_ref):
        kk = pl.program_id(2)

        @pl.when(kk == 0)
        def _():
            acc_ref[...] = jnp.zeros_like(acc_ref)

        acc_ref[...] += _dot(a_ref[...], b_ref[...], dims)

        @pl.when(kk == nk - 1)
        def _():
            o_ref[...] = acc_ref[...].astype(out_dtype)

    a_spec = pl.BlockSpec((tk, tm), lambda i, j, kk: (kk, i)) if mode == 'tn' else pl.BlockSpec((tm, tk), lambda i, j, kk: (i, kk))
    b_spec = pl.BlockSpec((tn, tk), lambda i, j, kk: (j, kk)) if mode == 'nt' else pl.BlockSpec((tk, tn), lambda i, j, kk: (kk, j))
    return pl.pallas_call(
        body, name=name, grid=(m // tm, n // tn, nk), in_specs=[a_spec, b_spec],
        out_specs=pl.BlockSpec((tm, tn), lambda i, j, kk: (i, j)),
        out_shape=jax.ShapeDtypeStruct((m, n), out_dtype), scratch_shapes=[pltpu.VMEM((tm, tn), F32)],
        compiler_params=_params(("parallel", "parallel", "arbitrary")))(a, b)


ROWS_NORM = 512


def _norm_fwd(x, g, mod):
    l, d = x.shape
    tr = min(ROWS_NORM, l)

    def body(x_ref, g_ref, mod_ref, h_ref):
        xv = x_ref[...]
        r = lax.rsqrt(jnp.mean(xv * xv, axis=-1, keepdims=True) + EPS)
        shift, scale = mod_ref[:, 0:d], mod_ref[:, d:2 * d]
        h_ref[...] = ((xv * r * g_ref[...]) * (1.0 + scale) + shift).astype(BF16)

    return pl.pallas_call(
        body, name="norm_fwd", grid=(l // tr,),
        in_specs=[pl.BlockSpec((tr, d), lambda i: (i, 0)), _full((1, d)), _full((1, 3 * d))],
        out_specs=pl.BlockSpec((tr, d), lambda i: (i, 0)), out_shape=jax.ShapeDtypeStruct((l, d), BF16),
        compiler_params=_params(("parallel",)))(x, g, mod)


def _norm_bwd(x, dh, dxo, g, mod):
    l, d = x.shape
    tr = min(ROWS_NORM, l)

    def body(x_ref, dh_ref, dxo_ref, g_ref, mod_ref, dx_ref, st_ref):
        @pl.when(pl.program_id(0) == 0)
        def _():
            st_ref[...] = jnp.zeros_like(st_ref)

        xv, dhv = x_ref[...], dh_ref[...]
        r = lax.rsqrt(jnp.mean(xv * xv, axis=-1, keepdims=True) + EPS)
        xn = xv * r
        gv = g_ref[...]
        sc1 = 1.0 + mod_ref[:, d:2 * d]
        dxn = dhv * gv * sc1
        dx_ref[...] = dxo_ref[...] + r * (dxn - xn * jnp.mean(dxn * xn, axis=-1, keepdims=True))
        st_ref[0:1, :] += jnp.sum(dhv, axis=0, keepdims=True)
        st_ref[1:2, :] += jnp.sum(dhv * xn * gv, axis=0, keepdims=True)
        st_ref[2:3, :] += jnp.sum(dhv * xn * sc1, axis=0, keepdims=True)

    blk = pl.BlockSpec((tr, d), lambda i: (i, 0))
    return pl.pallas_call(
        body, name="norm_bwd", grid=(l // tr,),
        in_specs=[blk, blk, blk, _full((1, d)), _full((1, 3 * d))],
        out_specs=[blk, _full((SUBLANES, d))],
        out_shape=[jax.ShapeDtypeStruct((l, d), F32), jax.ShapeDtypeStruct((SUBLANES, d), F32)],
        compiler_params=_params(("arbitrary",)))(x, dh, dxo, g, mod)


def _loss_head(x, g, target):
    l, d = x.shape
    tr = min(ROWS_NORM, l)

    def body(x_ref, g_ref, t_ref, dx_ref, st_ref):
        @pl.when(pl.program_id(0) == 0)
        def _():
            st_ref[...] = jnp.zeros_like(st_ref)

        xv = x_ref[...]
        r = lax.rsqrt(jnp.mean(xv * xv, axis=-1, keepdims=True) + EPS)
        xn = xv * r
        gv = g_ref[...]
        err = xn * gv - t_ref[...]
        part = 0.5 * jnp.sum(jnp.mean(err * err, axis=-1, keepdims=True), axis=0, keepdims=True)
        dy = err * (1.0 / d)
        dxn = dy * gv
        dx_ref[...] = r * (dxn - xn * jnp.mean(dxn * xn, axis=-1, keepdims=True))
        st_ref[0:1, :] += jnp.sum(dy * xn, axis=0, keepdims=True)
        st_ref[1:2, :] += jnp.broadcast_to(part, (1, d))

    blk = pl.BlockSpec((tr, d), lambda i: (i, 0))
    return pl.pallas_call(
        body, name="loss_head", grid=(l // tr,), in_specs=[blk, _full((1, d)), blk],
        out_specs=[blk, _full((SUBLANES, d))],
        out_shape=[jax.ShapeDtypeStruct((l, d), F32), jax.ShapeDtypeStruct((SUBLANES, d), F32)],
        compiler_params=_params(("arbitrary",)))(x, g, target)


ROWS_ATT = 256
_SLOPES = tuple(2.0 ** (-8.0 * (h + 1) / N_HEADS) for h in range(N_HEADS))


def _att_mask(i, t):
    r = lax.broadcasted_iota(jnp.int32, (t, t + HALO), 0)
    j = lax.broadcasted_iota(jnp.int32, (t, t + HALO), 1)
    dist = jnp.abs(r + HALO - j).astype(F32)
    rc, jc = r // CHUNK, j // CHUNK
    allowed = (jc >= rc) & (jc <= rc + 2) & ((j >= HALO) | (i > 0))
    return dist, allowed


def _att_probs(qh, k, dist, allowed, slope, sink):
    s = _dot(qh, k, _NT) * (1.0 / math.sqrt(HEAD_DIM)) - slope * dist
    s = jnp.where(allowed, s, NEG_INF)
    m = jnp.maximum(jnp.max(s, axis=1, keepdims=True), sink)
    e = jnp.exp(s - m)
    es = jnp.exp(sink - m)
    den = jnp.sum(e, axis=1, keepdims=True) + es
    return e / den, es / den


def _att_specs(t):
    q_spec = pl.BlockSpec((t, ATT_W), lambda i: (i, OFF_Q // ATT_W))
    kv_spec = pl.BlockSpec((t, 2 * KV_W), lambda i: (i, OFF_KV // (2 * KV_W)))
    halo_spec = pl.BlockSpec((HALO, 2 * KV_W), lambda i: (jnp.maximum(i * (t // HALO) - 1, 0), OFF_KV // (2 * KV_W)))
    return q_spec, kv_spec, halo_spec


def _attention_fwd(proj, sinks):
    l = proj.shape[0]
    t = min(ROWS_ATT, l)

    def body(sink_ref, q_ref, kv_ref, halo_ref, o_ref):
        dist, allowed = _att_mask(pl.program_id(0), t)
        kv = jnp.concatenate([halo_ref[...], kv_ref[...]], axis=0)
        for h in range(N_HEADS):
            kh = h // Q_PER_KV
            k = kv[:, kh * HEAD_DIM:(kh + 1) * HEAD_DIM]
            v = kv[:, KV_W + kh * HEAD_DIM:KV_W + (kh + 1) * HEAD_DIM]
            p, _ = _att_probs(q_ref[:, h * HEAD_DIM:(h + 1) * HEAD_DIM], k, dist, allowed, _SLOPES[h], sink_ref[h])
            o_ref[:, h * HEAD_DIM:(h + 1) * HEAD_DIM] = _dot(p, v)

    q_spec, kv_spec, halo_spec = _att_specs(t)
    return pl.pallas_call(
        body, name="attention_fwd", grid=(l // t,),
        in_specs=[pl.BlockSpec(memory_space=pltpu.SMEM), q_spec, kv_spec, halo_spec],
        out_specs=pl.BlockSpec((t, ATT_W), lambda i: (i, 0)), out_shape=jax.ShapeDtypeStruct((l, ATT_W), F32),
        compiler_params=_params(("parallel",)))(sinks, proj, proj, proj)


def _attention_bwd(proj, sinks, do):
    l = proj.shape[0]
    t = min(ROWS_ATT, l)

    def body(sink_ref, q_ref, kv_ref, halo_ref, do_ref, dq_ref, dkv_ref, dsink_ref):
        i = pl.program_id(0)

        @pl.when(i == 0)
        def _():
            dkv_ref[...] = jnp.zeros_like(dkv_ref)
            dsink_ref[...] = jnp.zeros_like(dsink_ref)

        dist, allowed = _att_mask(i, t)
        kv = jnp.concatenate([halo_ref[...], kv_ref[...]], axis=0)
        rows = pl.ds(pl.multiple_of(i * t, t), t + HALO)
        for kh in range(N_KV_HEADS):
            k = kv[:, kh * HEAD_DIM:(kh + 1) * HEAD_DIM]
            v = kv[:, KV_W + kh * HEAD_DIM:KV_W + (kh + 1) * HEAD_DIM]
            dk = jnp.zeros((t + HALO, HEAD_DIM), F32)
            dv = jnp.zeros((t + HALO, HEAD_DIM), F32)
            for h in range(kh * Q_PER_KV, (kh + 1) * Q_PER_KV):
                qh = q_ref[:, h * HEAD_DIM:(h + 1) * HEAD_DIM]
                doh = do_ref[:, h * HEAD_DIM:(h + 1) * HEAD_DIM]
                p, ps = _att_probs(qh, k, dist, allowed, _SLOPES[h], sink_ref[h])
                dp = _dot(doh, v, _NT)
                delta = jnp.sum(p * dp, axis=1, keepdims=True)
                ds = p * (dp - delta) * (1.0 / math.sqrt(HEAD_DIM))
                dsink_ref[h:h + 1, :] += jnp.broadcast_to(-jnp.sum(ps * delta, axis=0, keepdims=True), (1, LANES))
                dq_ref[:, h * HEAD_DIM:(h + 1) * HEAD_DIM] = _dot(ds, k)
                dk = dk + _dot(ds, qh, _TN)
                dv = dv + _dot(p, doh, _TN)
            dkv_ref[rows, kh * HEAD_DIM:(kh + 1) * HEAD_DIM] += dk
            dkv_ref[rows, KV_W + kh * HEAD_DIM:KV_W + (kh + 1) * HEAD_DIM] += dv

    q_spec, kv_spec, halo_spec = _att_specs(t)
    blk = pl.BlockSpec((t, ATT_W), lambda i: (i, 0))
    return pl.pallas_call(
        body, name="attention_bwd", grid=(l // t,),
        in_specs=[pl.BlockSpec(memory_space=pltpu.SMEM), q_spec, kv_spec, halo_spec, blk],
        out_specs=[blk, _full((HALO + l, 2 * KV_W)), _full((N_HEADS, LANES))],
        out_shape=[jax.ShapeDtypeStruct((l, ATT_W), F32), jax.ShapeDtypeStruct((HALO + l, 2 * KV_W), F32),
                   jax.ShapeDtypeStruct((N_HEADS, LANES), F32)],
        compiler_params=_params(("arbitrary",)))(sinks, proj, proj, proj, do)


ROWS_SSM = 512
SCAN_UNROLL = 4


def _ssm_discretize(a_re, a_im, log_dt, b_re, b_im):
    lam = lax.complex(a_re, a_im)
    dt = jnp.exp(log_dt)[:, None]
    lam_bar = jnp.exp(lam * dt)
    b_bar = ((lam_bar - 1.0) / lam)[..., None] * lax.complex(b_re, b_im)
    return lam, dt, lam_bar, b_bar


def _ssm_block_diag(m):
    e = m.reshape(N_SBLK, 2, SSM_GROUP, SSM_STATE)
    e = e[:, :, :, None, :] * jnp.eye(2, dtype=m.dtype)[None, :, None, :, None]
    e = e.reshape(N_SBLK, 2 * SSM_GROUP, LANES)
    oh = jax.nn.one_hot(jnp.arange(N_SBLK) % 4, 4, dtype=m.dtype)
    return (oh[:, :, None, None] * e[:, None]).reshape(N_SBLK, LANES, LANES)


def _ssm_tables(a_re, a_im, log_dt, b_re, b_im, c_re, c_im):
    _, _, lam_bar, b_bar = _ssm_discretize(a_re, a_im, log_dt, b_re, b_im)
    lam_blk = jnp.stack([jnp.real(lam_bar).reshape(N_SBLK, LANES), jnp.imag(lam_bar).reshape(N_SBLK, LANES)], axis=1)
    bt = jnp.transpose(b_bar, (0, 2, 1))
    bb = jnp.concatenate([_ssm_block_diag(jnp.real(bt)), _ssm_block_diag(jnp.imag(bt))], axis=2)
    cb = jnp.concatenate([jnp.transpose(_ssm_block_diag(c_re), (0, 2, 1)),
                          jnp.transpose(_ssm_block_diag(-c_im), (0, 2, 1))], axis=1)
    return lam_blk, bb, cb


def _ssm_powers(a_re, a_im, log_dt):
    lam = lax.complex(a_re, a_im)
    dt = jnp.exp(log_dt)[:, None]
    k = jnp.arange(1, SUBLANES + 1, dtype=F32)
    pw = jnp.exp((lam * dt)[None] * k[:, None, None]).reshape(SUBLANES, N_SBLK, LANES)
    pw = jnp.transpose(pw, (1, 0, 2))
    rev = pw[:, ::-1]
    return jnp.concatenate([jnp.real(pw), jnp.imag(pw), jnp.real(rev), jnp.imag(rev)], axis=1)


def _scan_consts(pw_ref, reverse):
    row = lax.broadcasted_iota(jnp.int32, (SUBLANES, LANES), 0)
    sign = -1.0 if reverse else 1.0

    def power(k):
        return (jnp.broadcast_to(pw_ref[0, k - 1:k, :], (SUBLANES, LANES)),
                sign * jnp.broadcast_to(pw_ref[0, SUBLANES + k - 1:SUBLANES + k, :], (SUBLANES, LANES)))

    steps = []
    for d in (1, 2, 4):
        pr, pi = power(d)
        keep = (row < SUBLANES - d) if reverse else (row >= d)
        steps.append((d, jnp.where(keep, pr, 0.0), jnp.where(keep, pi, 0.0)))
    base = 2 * SUBLANES if reverse else 0
    return (steps, pw_ref[0, base:base + SUBLANES, :], sign * pw_ref[0, base + SUBLANES:base + 2 * SUBLANES, :],
            power(SUBLANES))


def _scan_tile(xr, xi, steps, reverse):
    for d, ar, ai in steps:
        shift = SUBLANES - d if reverse else d
        rr, ri = pltpu.roll(xr, shift, 0), pltpu.roll(xi, shift, 0)
        xr, xi = xr + (ar * rr - ai * ri), xi + (ar * ri + ai * rr)
    return xr, xi


def _bcast_row(v, r):
    return jnp.broadcast_to(v[r:r + 1, :], (SUBLANES, LANES))


def _scan_forward(s_scr, pw_ref, n_tiles):
    steps, pr, pi, (l8r, l8i) = _scan_consts(pw_ref, False)

    def tiles(g, carry):
        cr, ci = carry
        rows = [pl.ds(pl.multiple_of((g * SCAN_UNROLL + u) * SUBLANES, SUBLANES), SUBLANES) for u in range(SCAN_UNROLL)]
        loaded = [(s_scr[r, 0:LANES], s_scr[r, LANES:2 * LANES]) for r in rows]
        local = [_scan_tile(xr, xi, steps, False) for xr, xi in loaded]
        for r, (xr, xi) in zip(rows, local):
            s_scr[r, 0:LANES] = xr + (pr * cr - pi * ci)
            s_scr[r, LANES:2 * LANES] = xi + (pr * ci + pi * cr)
            cr, ci = ((l8r * cr - l8i * ci) + _bcast_row(xr, SUBLANES - 1),
                      (l8r * ci + l8i * cr) + _bcast_row(xi, SUBLANES - 1))
        return cr, ci

    zero = jnp.zeros((SUBLANES, LANES), F32)
    lax.fori_loop(0, n_tiles // SCAN_UNROLL, tiles, (zero, zero))


def _row_chunks(l):
    rc = min(ROWS_SSM, l)
    return rc, l // rc


def _ssm_fwd(proj, bb, cb, pw):
    l = proj.shape[0]
    rc, n_chunks = _row_chunks(l)

    def body(u_ref, bb_ref, cb_ref, pw_ref, y_ref, s_scr):
        j = pl.program_id(0)

        def fill(ci, _):
            rows = pl.ds(pl.multiple_of(ci * rc, rc), rc)
            s_scr[rows, :] = _dot(u_ref[rows, :], bb_ref[0])
            return 0

        lax.fori_loop(0, n_chunks, fill, 0)
        _scan_forward(s_scr, pw_ref, l // SUBLANES)

        @pl.when(j % 4 == 0)
        def _():
            y_ref[...] = jnp.zeros_like(y_ref)

        def emit(ci, _):
            rows = pl.ds(pl.multiple_of(ci * rc, rc), rc)
            y_ref[rows, :] += _dot(s_scr[rows, :], cb_ref[0])
            return 0

        lax.fori_loop(0, n_chunks, emit, 0)

    return pl.pallas_call(
        body, name="ssm_fwd", grid=(N_SBLK,),
        in_specs=[pl.BlockSpec((l, LANES), lambda j: (0, OFF_US // LANES + j // 4)),
                  pl.BlockSpec((1, LANES, 2 * LANES), lambda j: (j, 0, 0)),
                  pl.BlockSpec((1, 2 * LANES, LANES), lambda j: (j, 0, 0)),
                  pl.BlockSpec((1, 4 * SUBLANES, LANES), lambda j: (j, 0, 0))],
        out_specs=pl.BlockSpec((l, LANES), lambda j: (0, j // 4)),
        out_shape=jax.ShapeDtypeStruct((l, SSM_W), F32), scratch_shapes=[pltpu.VMEM((l, 2 * LANES), F32)],
        compiler_params=_params(("arbitrary",)))(proj, bb, cb, pw)


def _ssm_bwd(proj, dy, bb, cb, pw):
    l = proj.shape[0]
    rc, n_chunks = _row_chunks(l)
    n_tiles = l // SUBLANES

    def body(u_ref, dy_ref, bb_ref, cb_ref, pw_ref, du_ref, dbb_ref, dcb_ref, dlam_ref, s_scr, a_scr):
        j = pl.program_id(0)

        def fill(ci, _):
            rows = pl.ds(pl.multiple_of(ci * rc, rc), rc)
            s_scr[rows, :] = _dot(u_ref[rows, :], bb_ref[0])
            return 0

        lax.fori_loop(0, n_chunks, fill, 0)
        _scan_forward(s_scr, pw_ref, n_tiles)

        dcb_ref[...] = jnp.zeros_like(dcb_ref)

        def through_c(ci, _):
            rows = pl.ds(pl.multiple_of(ci * rc, rc), rc)
            dyv = dy_ref[rows, :]
            dcb_ref[0] += _dot(s_scr[rows, :], dyv, _TN)
            a_scr[rows, :] = _dot(dyv, cb_ref[0], _NT)
            return 0

        lax.fori_loop(0, n_chunks, through_c, 0)

        steps, pr, pi, (l8r, l8i) = _scan_consts(pw_ref, True)
        row = lax.broadcasted_iota(jnp.int32, (SUBLANES, LANES), 0)

        last = row == SUBLANES - 1

        def tiles(g, carry):
            cr, ci, accr, acci = carry
            rows = [pl.ds(pl.multiple_of((n_tiles - 1 - (g * SCAN_UNROLL + u)) * SUBLANES, SUBLANES), SUBLANES)
                    for u in range(SCAN_UNROLL)]
            loaded = [(a_scr[r, 0:LANES], a_scr[r, LANES:2 * LANES]) for r in rows]
            states = [(s_scr[r, 0:LANES], s_scr[r, LANES:2 * LANES]) for r in rows]
            local = [_scan_tile(xr, xi, steps, True) for xr, xi in loaded]
            for r, (xr, xi), (sr, si) in zip(rows, local, states):
                gr, gi = xr + (pr * cr - pi * ci), xi + (pr * ci + pi * cr)
                a_scr[r, 0:LANES] = gr
                a_scr[r, LANES:2 * LANES] = gi
                ur = jnp.where(last, cr, pltpu.roll(gr, SUBLANES - 1, 0))
                ui = jnp.where(last, ci, pltpu.roll(gi, SUBLANES - 1, 0))
                accr, acci = accr + (ur * sr + ui * si), acci + (ui * sr - ur * si)
                cr, ci = (l8r * cr - l8i * ci) + _bcast_row(xr, 0), (l8r * ci + l8i * cr) + _bcast_row(xi, 0)
            return cr, ci, accr, acci

        zero = jnp.zeros((SUBLANES, LANES), F32)
        _, _, accr, acci = lax.fori_loop(0, n_tiles // SCAN_UNROLL, tiles, (zero, zero, zero, zero))
        dlr = jnp.broadcast_to(jnp.sum(accr, axis=0, keepdims=True), (SUBLANES, LANES))
        dli = jnp.broadcast_to(jnp.sum(acci, axis=0, keepdims=True), (SUBLANES, LANES))
        dlam_ref[0] = jnp.where(row == 0, dlr, jnp.where(row == 1, dli, 0.0))

        dbb_ref[...] = jnp.zeros_like(dbb_ref)

        @pl.when(j % 4 == 0)
        def _():
            du_ref[...] = jnp.zeros_like(du_ref)

        def through_b(ci, _):
            rows = pl.ds(pl.multiple_of(ci * rc, rc), rc)
            av = a_scr[rows, :]
            dbb_ref[0] += _dot(u_ref[rows, :], av, _TN)
            du_ref[rows, :] += _dot(av, bb_ref[0], _NT)
            return 0

        lax.fori_loop(0, n_chunks, through_b, 0)

    return pl.pallas_call(
        body, name="ssm_bwd", grid=(N_SBLK,),
        in_specs=[pl.BlockSpec((l, LANES), lambda j: (0, OFF_US // LANES + j // 4)),
                  pl.BlockSpec((l, LANES), lambda j: (0, j // 4)),
                  pl.BlockSpec((1, LANES, 2 * LANES), lambda j: (j, 0, 0)),
                  pl.BlockSpec((1, 2 * LANES, LANES), lambda j: (j, 0, 0)),
                  pl.BlockSpec((1, 4 * SUBLANES, LANES), lambda j: (j, 0, 0))],
        out_specs=[pl.BlockSpec((l, LANES), lambda j: (0, j // 4)),
                   pl.BlockSpec((1, LANES, 2 * LANES), lambda j: (j, 0, 0)),
                   pl.BlockSpec((1, 2 * LANES, LANES), lambda j: (j, 0, 0)),
                   pl.BlockSpec((1, SUBLANES, LANES), lambda j: (j, 0, 0))],
        out_shape=[jax.ShapeDtypeStruct((l, SSM_W), F32), jax.ShapeDtypeStruct((N_SBLK, LANES, 2 * LANES), F32),
                   jax.ShapeDtypeStruct((N_SBLK, 2 * LANES, LANES), F32), jax.ShapeDtypeStruct((N_SBLK, SUBLANES, LANES), F32)],
        scratch_shapes=[pltpu.VMEM((l, 2 * LANES), F32), pltpu.VMEM((l, 2 * LANES), F32)],
        compiler_params=_params(("arbitrary",)))(proj, dy, bb, cb, pw)


def _pool_windows(v, t, l, ahead):
    def shifted(a, d):
        if ahead:
            return jnp.where(t < l - d, pltpu.roll(a, l - d, 0), 0.0)
        return jnp.where(t >= d, pltpu.roll(a, d, 0), 0.0)

    w2 = v + shifted(v, 1)
    w4 = w2 + shifted(w2, 2)
    w8 = w4 + shifted(w4, 4)
    w16 = w8 + shifted(w8, 8)
    return w2, w4, w8, w16


def _pool_select(g, ws):
    return jnp.where(g == 0, ws[0], jnp.where(g == 1, ws[1], jnp.where(g == 2, ws[2], ws[3])))


def _pool_count(g, t):
    return jnp.minimum(t + 1, jnp.left_shift(2, g)).astype(F32)


def _pool_specs(l):
    return [pl.BlockSpec((l, POOL_GW), lambda g: (0, OFF_UP // POOL_GW + g)),
            pl.BlockSpec((1, POOL_GW, POOL_GW), lambda g: (g, 0, 0)),
            pl.BlockSpec((1, POOL_GW), lambda g: (0, g))]


def _pool_fwd(proj, w_pool, scale):
    l = proj.shape[0]

    def body(u_ref, w_ref, sc_ref, y_ref):
        g = pl.program_id(0)
        t = lax.broadcasted_iota(jnp.int32, (l, 1), 0)
        u = u_ref[...]
        pooled = _pool_select(g, _pool_windows(u, t, l, False)) / _pool_count(g, t) - u
        y_ref[...] = _dot(pooled, w_ref[0]) * sc_ref[...]

    return pl.pallas_call(
        body, name="pool_fwd", grid=(4,), in_specs=_pool_specs(l),
        out_specs=pl.BlockSpec((l, POOL_GW), lambda g: (0, g)), out_shape=jax.ShapeDtypeStruct((l, POOL_W), F32),
        compiler_params=_params(("parallel",)))(proj, w_pool, scale)


def _pool_bwd(proj, w_pool, scale, dy):
    l = proj.shape[0]

    def body(u_ref, w_ref, sc_ref, dy_ref, du_ref, dw_ref, dsc_ref):
        g = pl.program_id(0)
        t = lax.broadcasted_iota(jnp.int32, (l, 1), 0)
        u = u_ref[...]
        cnt = _pool_count(g, t)
        pooled = _pool_select(g, _pool_windows(u, t, l, False)) / cnt - u
        dyv = dy_ref[...]
        dsc_ref[...] = jnp.sum(dyv * _dot(pooled, w_ref[0]), axis=0, keepdims=True)
        dyl = dyv * sc_ref[...]
        dw_ref[0] = _dot(pooled, dyl, _TN)
        dpooled = _dot(dyl, w_ref[0], _NT)
        du_ref[...] = _pool_select(g, _pool_windows(dpooled / cnt, t, l, True)) - dpooled

    return pl.pallas_call(
        body, name="pool_bwd", grid=(4,), in_specs=_pool_specs(l) + [pl.BlockSpec((l, POOL_GW), lambda g: (0, g))],
        out_specs=[pl.BlockSpec((l, POOL_GW), lambda g: (0, g)), pl.BlockSpec((1, POOL_GW, POOL_GW), lambda g: (g, 0, 0)),
                   pl.BlockSpec((1, POOL_GW), lambda g: (0, g))],
        out_shape=[jax.ShapeDtypeStruct((l, POOL_W), F32), jax.ShapeDtypeStruct((4, POOL_GW, POOL_GW), F32),
                   jax.ShapeDtypeStruct((1, POOL_W), F32)],
        compiler_params=_params(("parallel",)))(proj, w_pool, scale, dy)


ROWS_MERGE = 128


def _merge_inputs(tr):
    def col(off, w):
        return pl.BlockSpec((tr, w), lambda i: (i, off // w))

    def act(w):
        return pl.BlockSpec((tr, w), lambda i: (i, 0))

    d = D_MODEL
    return ([col(OFF_GL, 3 * d), col(OFF_ZA, ATT_W), col(OFF_US, SSM_W), col(OFF_ZS, SSM_W), col(OFF_ZP, POOL_W),
             act(ATT_W), act(SSM_W), act(POOL_W)]
            + [_full((1, SSM_W)), _full((SSM_W, SSM_W)), _full((1, SSM_W)), _full((ATT_W, d)), _full((SSM_W, d)),
               _full((POOL_W, d)), _full((d, d)), _full((1, 3 * d))])


def _merge_forward_math(gl_ref, za_ref, us_ref, zs_ref, zp_ref, o_ref, yl_ref, yp_ref, d_ref, wg_ref, bg_ref,
                        wba_ref, wbs_ref, wbp_ref):
    d = D_MODEL
    r = {}
    r['sa'], r['dsa'] = _silu_and_grad(za_ref[...])
    r['ss'], r['dss'] = _silu_and_grad(zs_ref[...])
    r['sp'], r['dsp'] = _silu_and_grad(zp_ref[...])
    r['y_att'] = o_ref[...] * r['sa']
    y1 = yl_ref[...] + d_ref[...] * us_ref[...]
    r['y2'], r['dgelu'] = _gelu_and_grad(y1)
    r['sg'] = _sigmoid(_dot(r['y2'], wg_ref[...]) + bg_ref[...])
    r['y3'] = r['y2'] * r['sg']
    r['y_ssm'] = r['y3'] * r['ss']
    r['y_pool'] = yp_ref[...] * r['sp']
    r['g'] = [_sigmoid(gl_ref[:, b * d:(b + 1) * d]) for b in range(3)]
    r['p'] = [_dot(r['y_att'], wba_ref[...]), _dot(r['y_ssm'], wbs_ref[...]), _dot(r['y_pool'], wbp_ref[...])]
    r['merged'] = r['g'][0] * r['p'][0] + r['g'][1] * r['p'][1] + r['g'][2] * r['p'][2]
    return r


def _merge_fwd(proj, o, y_lin, y_pool, x, ssm_d, w_glu, b_glu, wba, wbs, wbp, w_out, mod):
    l, d = x.shape
    tr = min(ROWS_MERGE, l)

    def body(gl_ref, za_ref, us_ref, zs_ref, zp_ref, o_ref, yl_ref, yp_ref, d_ref, wg_ref, bg_ref, wba_ref, wbs_ref,
             wbp_ref, wo_ref, mod_ref, x_ref, xn_ref):
        r = _merge_forward_math(gl_ref, za_ref, us_ref, zs_ref, zp_ref, o_ref, yl_ref, yp_ref, d_ref, wg_ref, bg_ref,
                                wba_ref, wbs_ref, wbp_ref)
        xn_ref[...] = x_ref[...] + mod_ref[:, 2 * d:3 * d] * _dot(r['merged'], wo_ref[...])

    blk = pl.BlockSpec((tr, d), lambda i: (i, 0))
    return pl.pallas_call(
        body, name="merge_fwd", grid=(l // tr,), in_specs=_merge_inputs(tr) + [blk], out_specs=blk,
        out_shape=jax.ShapeDtypeStruct((l, d), F32), compiler_params=_params(("parallel",)))(
            proj, proj, proj, proj, proj, o, y_lin, y_pool, ssm_d, w_glu, b_glu, wba, wbs, wbp, w_out, mod, x)


def _merge_bwd(proj, o, y_lin, y_pool, dxo, ssm_d, w_glu, b_glu, wba, wbs, wbp, w_out, mod):
    l, d = dxo.shape
    tr = min(ROWS_MERGE, l)

    def body(gl_ref, za_ref, us_ref, zs_ref, zp_ref, o_ref, yl_ref, yp_ref, d_ref, wg_ref, bg_ref, wba_ref, wbs_ref,
             wbp_ref, wo_ref, mod_ref, dxo_ref,
             do_ref, dza_ref, dyl_ref, dus_ref, dzs_ref, dyp_ref, dzp_ref, dgl_ref,
             ya_ref, ys_ref, ypl_ref, y2_ref, dt_ref, dpa_ref, dps_ref, dpp_ref, mg_ref, dout_ref, st_ref):
        @pl.when(pl.program_id(0) == 0)
        def _():
            st_ref[...] = jnp.zeros_like(st_ref)

        r = _merge_forward_math(gl_ref, za_ref, us_ref, zs_ref, zp_ref, o_ref, yl_ref, yp_ref, d_ref, wg_ref, bg_ref,
                                wba_ref, wbs_ref, wbp_ref)
        dxov = dxo_ref[...]
        out = _dot(r['merged'], wo_ref[...])
        st_ref[0:1, :] += jnp.sum(dxov * out, axis=0, keepdims=True)
        dout = dxov * mod_ref[:, 2 * d:3 * d]
        dmerged = _dot(dout, wo_ref[...], _NT)
        dys = []
        for b, (w_ref, dp_ref) in enumerate(((wba_ref, dpa_ref), (wbs_ref, dps_ref), (wbp_ref, dpp_ref))):
            gb = r['g'][b]
            dgl_ref[:, b * d:(b + 1) * d] = dmerged * r['p'][b] * gb * (1.0 - gb)
            dp = dmerged * gb
            dp_ref[...] = dp.astype(BF16)
            dys.append(_dot(dp, w_ref[...], _NT))
        mg_ref[...] = r['merged'].astype(BF16)
        dout_ref[...] = dout.astype(BF16)
        ya_ref[...] = r['y_att'].astype(BF16)
        ys_ref[...] = r['y_ssm'].astype(BF16)
        ypl_ref[...] = r['y_pool'].astype(BF16)
        y2_ref[...] = r['y2'].astype(BF16)
        do_ref[...] = dys[0] * r['sa']
        dza_ref[...] = dys[0] * o_ref[...] * r['dsa']
        dy3 = dys[1] * r['ss']
        dzs_ref[...] = dys[1] * r['y3'] * r['dss']
        dt = dy3 * r['y2'] * r['sg'] * (1.0 - r['sg'])
        dt_ref[...] = dt.astype(BF16)
        dy1 = (dy3 * r['sg'] + _dot(dt, wg_ref[...], _NT)) * r['dgelu']
        dyl_ref[...] = dy1
        dus_ref[...] = dy1 * d_ref[...]
        st_ref[1:2, 0:SSM_W] += jnp.sum(dt, axis=0, keepdims=True)
        st_ref[1:2, SSM_W:2 * SSM_W] += jnp.sum(dy1 * us_ref[...], axis=0, keepdims=True)
        dyp_ref[...] = dys[2] * r['sp']
        dzp_ref[...] = dys[2] * yp_ref[...] * r['dsp']

    blk = pl.BlockSpec((tr, d), lambda i: (i, 0))
    half = pl.BlockSpec((tr, ATT_W), lambda i: (i, 0))
    wide = pl.BlockSpec((tr, 3 * d), lambda i: (i, 0))
    sds = jax.ShapeDtypeStruct
    return pl.pallas_call(
        body, name="merge_bwd", grid=(l // tr,), in_specs=_merge_inputs(tr) + [blk],
        out_specs=[half] * 7 + [wide] + [half] * 5 + [blk] * 5 + [_full((SUBLANES, d))],
        out_shape=[sds((l, ATT_W), F32)] * 7 + [sds((l, 3 * d), F32)] + [sds((l, ATT_W), BF16)] * 5
        + [sds((l, d), BF16)] * 5 + [sds((SUBLANES, d), F32)],
        compiler_params=_params(("arbitrary",)))(
            proj, proj, proj, proj, proj, o, y_lin, y_pool, ssm_d, w_glu, b_glu, wba, wbs, wbp, w_out, mod, dxo)


def _ada_fwd(c_all, w_ada, b_shard):
    depth, d, n = w_ada.shape

    def body(c_ref, w_ref, b_ref, o_ref):
        act, _ = _silu_and_grad(c_ref[...])
        o_ref[0] = _dot(act, w_ref[0]) + b_ref[0]

    return pl.pallas_call(
        body, name="ada_fwd", grid=(depth,),
        in_specs=[_full((N_DEV, d)), pl.BlockSpec((1, d, n), lambda i: (i, 0, 0)), pl.BlockSpec((1, 1, n), lambda i: (i, 0, 0))],
        out_specs=pl.BlockSpec((1, N_DEV, n), lambda i: (i, 0, 0)), out_shape=jax.ShapeDtypeStruct((depth, N_DEV, n), F32),
        compiler_params=_params(("parallel",)))(c_all, w_ada, b_shard)


def _ada_bwd(c_all_t, dmod):
    d = c_all_t.shape[0]
    depth, _, n = dmod.shape

    def body(c_ref, dm_ref, o_ref):
        act, _ = _silu_and_grad(c_ref[...])
        acc = act[:, 0:1] * dm_ref[0, 0:1, :]
        for b in range(1, N_DEV):
            acc = acc + act[:, b:b + 1] * dm_ref[0, b:b + 1, :]
        o_ref[0] = acc

    return pl.pallas_call(
        body, name="ada_bwd", grid=(depth,),
        in_specs=[_full((d, N_DEV)), pl.BlockSpec((1, N_DEV, n), lambda i: (i, 0, 0))],
        out_specs=pl.BlockSpec((1, d, n), lambda i: (i, 0, 0)), out_shape=jax.ShapeDtypeStruct((depth, d, n), F32),
        compiler_params=_params(("parallel",)))(c_all_t, dmod)


ROWS_FLAT = 256


_RELATION_XOR = (0, 2, 1, 3)


def _sum_chips(own, others, chip_order, name):
    r, c = own.shape
    tr = math.gcd(ROWS_FLAT, r)

    def body(own_ref, oth_ref, o_ref):
        terms = [own_ref[...].astype(F32)] + [oth_ref[k].astype(F32) for k in range(3)]
        if chip_order:
            chip = 2 * lax.axis_index("x") + lax.axis_index("y")
            by_chip = []
            for q in range(N_CHIPS):
                rel = jnp.bitwise_xor(chip, q)
                pick = terms[3]
                for k in (2, 1, 0):
                    pick = jnp.where(rel == _RELATION_XOR[k], terms[k], pick)
                by_chip.append(pick)
            terms = by_chip
        o_ref[...] = ((terms[0] + terms[1]) + terms[2]) + terms[3]

    return pl.pallas_call(
        body, name=name, grid=(r // tr,),
        in_specs=[pl.BlockSpec((tr, c), lambda i: (i, 0)), pl.BlockSpec((3, tr, c), lambda i: (0, i, 0))],
        out_specs=pl.BlockSpec((tr, c), lambda i: (i, 0)), out_shape=jax.ShapeDtypeStruct((r, c), F32),
        compiler_params=_params(("parallel",)))(own, others)


def _add2(a, b, out_dtype, name):
    shape = a.shape
    a, b = a.reshape(-1, shape[-1]), b.reshape(-1, shape[-1])
    r, c = a.shape
    tr = math.gcd(ROWS_FLAT, r)

    def body(a_ref, b_ref, o_ref):
        o_ref[...] = (a_ref[...] + b_ref[...]).astype(out_dtype)

    blk = pl.BlockSpec((tr, c), lambda i: (i, 0))
    return pl.pallas_call(
        body, name=name, grid=(r // tr,), in_specs=[blk, blk], out_specs=blk,
        out_shape=jax.ShapeDtypeStruct((r, c), out_dtype), compiler_params=_params(("parallel",)))(a, b).reshape(shape)


def _adamw(w, g, m, v, name):
    r, c = w.shape
    tr = math.gcd(ROWS_FLAT, r)

    def body(w_ref, g_ref, m_ref, v_ref, d_ref, nm_ref, nv_ref):
        gv = g_ref[...]
        mv = ADAM_B1 * m_ref[...] + (1.0 - ADAM_B1) * gv
        vv = ADAM_B2 * v_ref[...] + (1.0 - ADAM_B2) * (gv * gv)
        m_hat = mv / (1.0 - ADAM_B1 ** ADAM_STEP)
        v_hat = vv / (1.0 - ADAM_B2 ** ADAM_STEP)
        d_ref[...] = -ADAM_LR * (m_hat / (jnp.sqrt(v_hat) + ADAM_EPS) + ADAM_WD * w_ref[...])
        nm_ref[...] = mv
        nv_ref[...] = vv

    blk = pl.BlockSpec((tr, c), lambda i: (i, 0))
    return pl.pallas_call(
        body, name=name, grid=(r // tr,), in_specs=[blk] * 4, out_specs=[blk] * 3,
        out_shape=[jax.ShapeDtypeStruct((r, c), F32)] * 3, compiler_params=_params(("parallel",)))(w, g, m, v)


_GROUP_MASKS = {
    'xy': ((1, 0, 0), (0, 1, 0), (1, 1, 0)),
    'c': ((0, 0, 1),),
    'xyc': ((0, 0, 1), (0, 1, 0), (0, 1, 1), (1, 0, 0), (1, 0, 1), (1, 1, 0), (1, 1, 1)),
}


def _group_rank(group, pos):
    x, y, c = pos
    return {'xy': 2 * x + y, 'c': c, 'xyc': 4 * x + 2 * y + c}[group]


def _exchange(src, group, scatter, name):
    masks = _GROUP_MASKS[group]
    n = len(masks) + 1
    shape = src.shape[1:] if scatter else src.shape
    assert (not scatter) or src.shape[0] == n

    def body(src_ref, out_ref, send_sems, recv_sems, local_sem):
        me = (lax.axis_index("x"), lax.axis_index("y"), lax.axis_index("c"))
        my_rank = _group_rank(group, me)

        def mine_for(rank):
            return src_ref.at[rank] if scatter else src_ref

        local = pltpu.make_async_copy(mine_for(my_rank), out_ref.at[my_rank], local_sem)
        local.start()
        sends = []
        for k, mask in enumerate(masks):
            peer = tuple(1 - p if f else p for p, f in zip(me, mask))
            peer_rank = _group_rank(group, peer)
            send = pltpu.make_async_remote_copy(
                src_ref=mine_for(peer_rank), dst_ref=out_ref.at[my_rank], send_sem=send_sems.at[k],
                recv_sem=recv_sems.at[k], device_id=peer, device_id_type=pl.DeviceIdType.MESH)
            send.start()
            sends.append((send, peer, peer_rank))
        for k, (send, peer, peer_rank) in enumerate(sends):
            pltpu.make_async_remote_copy(
                src_ref=mine_for(peer_rank), dst_ref=out_ref.at[peer_rank], send_sem=send_sems.at[k],
                recv_sem=recv_sems.at[k], device_id=peer, device_id_type=pl.DeviceIdType.MESH).wait_recv()
        for send, _, _ in sends:
            send.wait_send()
        local.wait()

    return pl.pallas_call(
        body, name=name, in_specs=[pl.BlockSpec(memory_space=pl.ANY)], out_specs=pl.BlockSpec(memory_space=pl.ANY),
        out_shape=jax.ShapeDtypeStruct((n,) + tuple(shape), src.dtype),
        scratch_shapes=[pltpu.SemaphoreType.DMA((n - 1,)), pltpu.SemaphoreType.DMA((n - 1,)), pltpu.SemaphoreType.DMA(())],
    )(src)


CHUNK_BYTES = 1 << 20
MIN_CHUNK_ROWS = 64


def _row_parts(rows, cols, itemsize):
    n = 1
    while rows % (2 * n) == 0 and rows // (2 * n) >= MIN_CHUNK_ROWS and rows * cols * itemsize > n * CHUNK_BYTES:
        n *= 2
    return n


def _remote(src, dst, send_sem, recv_sem, peer):
    return pltpu.make_async_remote_copy(src_ref=src, dst_ref=dst, send_sem=send_sem, recv_sem=recv_sem,
                                        device_id=peer, device_id_type=pl.DeviceIdType.MESH)


def _start_rows(src, dst, send_sem, recv_sem, peer):
    rows, cols = src.shape
    n = _row_parts(rows, cols, jnp.dtype(src.dtype).itemsize)
    pr = rows // n
    for i in range(n):
        _remote(src.at[pl.ds(i * pr, pr), :], dst.at[pl.ds(i * pr, pr), :], send_sem, recv_sem, peer).start()


def _mesh_place():
    x, y, c = lax.axis_index("x"), lax.axis_index("y"), lax.axis_index("c")
    other_chips = ((1 - x, y), (x, 1 - y), (1 - x, 1 - y))
    return x, y, c, 2 * x + y, (x, y, 1 - c), other_chips


def _comm_call(body, name, ins, out_shapes, sem_counts):
    any_spec = pl.BlockSpec(memory_space=pl.ANY)
    return pl.pallas_call(
        body, name=name, in_specs=[any_spec] * len(ins), out_specs=[any_spec] * len(out_shapes), out_shape=out_shapes,
        scratch_shapes=[pltpu.SemaphoreType.DMA((n,)) for n in sem_counts])(*ins)


def _gather_weights(shards):
    nt = len(shards)

    def body(*refs):
        ins, over_ici, from_sib = refs[:nt], refs[nt:2 * nt], refs[2 * nt:3 * nt]
        send_a, recv_a, send_b, recv_b = refs[3 * nt:]
        _, _, c, _, sibling, other_chips = _mesh_place()
        for t in range(nt):
            nl, r2 = ins[t].shape[0], ins[t].shape[1] // 2
            for j, (px, py) in enumerate(other_chips):
                for li in range(nl):
                    _remote(ins[t].at[li, pl.ds(pl.multiple_of(c * r2, MIN_CHUNK_ROWS), r2), :], over_ici[t].at[j, li],
                            send_a.at[3 * t + j], recv_a.at[3 * t + j], (px, py, c)).start()
        for t in range(nt):
            for j, (px, py) in enumerate(other_chips):
                landed = over_ici[t].at[j]
                _remote(landed, landed, send_a.at[3 * t + j], recv_a.at[3 * t + j], (px, py, c)).wait_recv()
                for li in range(ins[t].shape[0]):
                    _start_rows(landed.at[li], from_sib[t].at[j, li], send_b.at[3 * t + j], recv_b.at[3 * t + j], sibling)
        for t in range(nt):
            for j in range(3):
                theirs = from_sib[t].at[j]
                _remote(theirs, theirs, send_b.at[3 * t + j], recv_b.at[3 * t + j], sibling).wait_recv()
        for t in range(nt):
            for j, (px, py) in enumerate(other_chips):
                sent = over_ici[t].at[j]
                _remote(sent, sent, send_a.at[3 * t + j], recv_a.at[3 * t + j], (px, py, c)).wait_send()
                _remote(sent, sent, send_b.at[3 * t + j], recv_b.at[3 * t + j], sibling).wait_send()

    slots = [jax.ShapeDtypeStruct((3, s.shape[0], s.shape[1] // 2, s.shape[2]), s.dtype) for s in shards]
    outs = _comm_call(body, "gather_weights", shards, slots + slots, (3 * nt, 3 * nt, 3 * nt, 3 * nt))
    return outs[:nt], outs[nt:]


def _swap_halves(gs):
    nt = len(gs)

    def body(*refs):
        ins, outs = refs[:nt], refs[nt:2 * nt]
        send_sems, recv_sems = refs[2 * nt:]
        _, _, c, _, sibling, _ = _mesh_place()
        for t in range(nt):
            _, nd, nl, _, _ = ins[t].shape
            for q in range(nd):
                for li in range(nl):
                    _start_rows(ins[t].at[1 - c, q, li], outs[t].at[q, li], send_sems.at[t], recv_sems.at[t], sibling)
        for t in range(nt):
            _remote(outs[t], outs[t], send_sems.at[t], recv_sems.at[t], sibling).wait_recv()
        for t in range(nt):
            _remote(outs[t], outs[t], send_sems.at[t], recv_sems.at[t], sibling).wait_send()

    out_shapes = [jax.ShapeDtypeStruct(g.shape[1:], g.dtype) for g in gs]
    return _comm_call(body, "swap_halves", gs, out_shapes, (nt, nt))


def _scatter_chips(hs):
    nt = len(hs)

    def body(*refs):
        ins, outs = refs[:nt], refs[nt:2 * nt]
        send_sems, recv_sems = refs[2 * nt:]
        _, _, c, _, _, other_chips = _mesh_place()
        for t in range(nt):
            for j, (px, py) in enumerate(other_chips):
                slab = ins[t].at[2 * px + py if ins[t].shape[0] == N_CHIPS else 0]
                for li in range(ins[t].shape[1]):
                    _remote(slab.at[li], outs[t].at[j, li], send_sems.at[3 * t + j], recv_sems.at[3 * t + j],
                            (px, py, c)).start()
        for t in range(nt):
            for j, (px, py) in enumerate(other_chips):
                landed = outs[t].at[j]
                _remote(landed, landed, send_sems.at[3 * t + j], recv_sems.at[3 * t + j], (px, py, c)).wait_recv()
        for t in range(nt):
            for j, (px, py) in enumerate(other_chips):
                sent = outs[t].at[j]
                _remote(sent, sent, send_sems.at[3 * t + j], recv_sems.at[3 * t + j], (px, py, c)).wait_send()

    out_shapes = [jax.ShapeDtypeStruct((3,) + h.shape[1:], h.dtype) for h in hs]
    return _comm_call(body, "scatter_chips", hs, out_shapes, (3 * nt, 3 * nt))


def _swap_cores(halves):
    nt = len(halves)

    def body(*refs):
        ins, outs = refs[:nt], refs[nt:2 * nt]
        send_sems, recv_sems = refs[2 * nt:]
        _, _, _, _, sibling, _ = _mesh_place()
        for t in range(nt):
            for li in range(ins[t].shape[0]):
                _start_rows(ins[t].at[li], outs[t].at[li], send_sems.at[t], recv_sems.at[t], sibling)
        for t in range(nt):
            _remote(outs[t], outs[t], send_sems.at[t], recv_sems.at[t], sibling).wait_recv()
        for t in range(nt):
            _remote(ins[t], ins[t], send_sems.at[t], recv_sems.at[t], sibling).wait_send()

    out_shapes = [jax.ShapeDtypeStruct(h.shape, h.dtype) for h in halves]
    return _comm_call(body, "swap_cores", halves, out_shapes, (nt, nt))


def _pack_rows(pieces, cols, dtype):
    tile = SUBLANES * cols
    rows = []
    for p in pieces:
        flat = p.reshape(-1).astype(dtype)
        rows.append(jnp.pad(flat, (0, (-flat.shape[0]) % tile)).reshape(-1, cols))
    total = sum(r.shape[0] for r in rows)
    if total % (2 * SUBLANES):
        rows.append(jnp.zeros((SUBLANES, cols), dtype))
    return jnp.concatenate(rows, axis=0)


def _unpack_rows(buf, shapes):
    out, row = [], 0
    for s in shapes:
        n = math.prod(s)
        nrows = -(-n // (SUBLANES * buf.shape[1])) * SUBLANES
        out.append(buf[row:row + nrows].reshape(-1)[:n].reshape(s))
        row += nrows
    return out


def _permute_cols(w):
    return jnp.concatenate([w[..., s:s + n] for s, n in _PERM_PIECES], axis=-1)


def _unpermute_cols(w):
    off, pieces = 0, []
    for s, n in _PERM_PIECES:
        pieces.append((s, w[..., off:off + n]))
        off += n
    return jnp.concatenate([p for _, p in sorted(pieces, key=lambda sp: sp[0])], axis=-1)


ROW_SHARDED = ('w_out', 'w_glu')


def _assemble_full(name, g):
    nh, nc, nl, r2, cols = g.shape
    if name in ROW_SHARDED:
        return jnp.transpose(g, (2, 1, 0, 3, 4)).reshape(nl, nc * nh * r2, cols)
    return jnp.transpose(g, (2, 0, 3, 1, 4)).reshape(nl, nh * r2, nc * cols)


def _by_chip(own, related, x, y):
    grid = ((own, related[1]), (related[0], related[2]))
    along_x = [[jnp.where(x == 0, grid[px][dy], grid[1 - px][dy]) for dy in range(2)] for px in range(2)]
    return [jnp.where(y == 0, along_x[px][py], along_x[px][1 - py]) for px in range(2) for py in range(2)]


def _join_halves(mine, theirs, core, axis):
    return jnp.where(core == 0, jnp.concatenate([mine, theirs], axis=axis), jnp.concatenate([theirs, mine], axis=axis))


def _split_by_dest(name, full):
    nl, rows, cols = full.shape
    if name in ROW_SHARDED:
        return jnp.transpose(full.reshape(nl, N_CHIPS, 2, rows // (2 * N_CHIPS), cols), (2, 1, 0, 3, 4))
    return jnp.transpose(full.reshape(nl, 2, rows // 2, N_CHIPS, cols // N_CHIPS), (1, 3, 0, 2, 4))


def _layer_tables(p, li):
    raw = tuple(p[k][li] for k in ('ssm_a_re', 'ssm_a_im', 'ssm_log_dt', 'ssm_b_re', 'ssm_b_im', 'ssm_c_re', 'ssm_c_im'))
    (lam_blk, bb, cb), vjp = jax.vjp(_ssm_tables, *raw)
    del lam_blk
    return bb.astype(BF16), cb.astype(BF16), _ssm_powers(*raw[:3]), vjp


def _layer_fwd(x, p, full, li, mod):
    h = _norm_fwd(x, _row(p['norm_g'][li]), mod)
    proj = _matmul(h, full['w_in'], 'nn', tm=512, tn=1280, tk=1024, name="proj")
    bb, cb, pw, tables_vjp = _layer_tables(p, li)
    o = _attention_fwd(proj, p['attn_sinks'][li])
    y_lin = _ssm_fwd(proj, bb, cb, pw)
    y_pool = _pool_fwd(proj, p['w_pool'][li].astype(BF16), _row(p['pool_scale'][li]))
    x_new = _merge_fwd(proj, o, y_lin, y_pool, x, _row(p['ssm_d'][li]), full['w_glu'], _row(p['b_glu'][li]),
                       full['w_br_att'], full['w_br_ssm'], full['w_br_pool'], full['w_out'], mod)
    saved = dict(x=x, h=h, proj=proj, o=o, y_lin=y_lin, y_pool=y_pool, bb=bb, cb=cb, pw=pw, tables_vjp=tables_vjp)
    return x_new, saved


def _layer_bwd(dxo, s, p, full, li, mod):
    l = dxo.shape[0]
    proj = s['proj']
    (do, dza, dyl, dus_skip, dzs, dyp, dzp, dgl, ya, ys, ypl, y2, dt, dpa, dps, dpp, mg, dout, st) = _merge_bwd(
        proj, s['o'], s['y_lin'], s['y_pool'], dxo, _row(p['ssm_d'][li]), full['w_glu'], _row(p['b_glu'][li]),
        full['w_br_att'], full['w_br_ssm'], full['w_br_pool'], full['w_out'], mod)
    g = {}
    g['w_br_att'] = _matmul(ya, dpa, 'tn', tm=512, tn=1024, tk=512, name="grad_w_br")
    g['w_br_ssm'] = _matmul(ys, dps, 'tn', tm=512, tn=1024, tk=512, name="grad_w_br")
    g['w_br_pool'] = _matmul(ypl, dpp, 'tn', tm=512, tn=1024, tk=512, name="grad_w_br")
    g['w_out'] = _matmul(mg, dout, 'tn', tm=512, tn=1024, tk=512, name="grad_w_out")
    g['w_glu'] = _matmul(y2, dt, 'tn', tm=512, tn=512, tk=512, name="grad_w_glu")
    g['b_glu'] = st[1, 0:SSM_W]
    g['ssm_d'] = st[1, SSM_W:2 * SSM_W]
    dgate = st[0]

    dq, dkv, dsink = _attention_bwd(proj, p['attn_sinks'][li], do)
    g['attn_sinks'] = dsink[:, 0]
    dus_scan, dbb, dcb, dlam = _ssm_bwd(proj, dyl, s['bb'], s['cb'], s['pw'])
    raw = s['tables_vjp']((dlam[:, 0:2, :], dbb, dcb))
    for k, v in zip(('ssm_a_re', 'ssm_a_im', 'ssm_log_dt', 'ssm_b_re', 'ssm_b_im', 'ssm_c_re', 'ssm_c_im'), raw):
        g[k] = v
    dup, dwp, dps_scale = _pool_bwd(proj, p['w_pool'][li].astype(BF16), _row(p['pool_scale'][li]), dyp)
    g['w_pool'] = dwp
    g['pool_scale'] = dps_scale[0]

    dproj = jnp.concatenate([piece.astype(BF16) for piece in
                             (dgl, dq, dza, dus_skip + dus_scan, dzs, dup, dzp, dkv[HALO:])], axis=1)
    g['w_in'] = _unpermute_cols(_matmul(s['h'], dproj, 'tn', tm=1024, tn=1280, tk=512, name="grad_w_in"))
    dh = _matmul(dproj, full['w_in'], 'nt', tm=1024, tn=1024, tk=1280, name="grad_h")
    dx, nst = _norm_bwd(s['x'], dh, dxo, _row(p['norm_g'][li]), mod)
    g['norm_g'] = nst[2]
    dmod = jnp.concatenate([nst[0], nst[1], dgate])
    del l
    return dx, g, dmod


def _gather_big(p, place, depth):
    x, y, core, _ = place
    shards = [p[name].astype(BF16) for name, _ in BIG]
    over_ici, from_sibling = _gather_weights(shards)
    fulls = [{} for _ in range(depth)]
    for (name, _), own, a, b in zip(BIG, shards, over_ici, from_sibling):
        nl, r, cols = own.shape
        own_halves = own.reshape(nl, 2, r // 2, cols)
        for li in range(depth):
            halves = [_join_halves(a[j, li][None], b[j, li][None], core, 0) for j in range(3)]
            chips = _by_chip(own_halves[li], halves, x, y)
            fulls[li][name] = _assemble_full(name, jnp.stack(chips, axis=1)[:, :, None])[0]
    for full in fulls:
        full['w_in'] = _permute_cols(full['w_in'])
    return fulls


def _reduce_grads(grads, small_pack, place, depth):
    _, _, core, chip = place
    names = [name for name, _ in BIG]
    mine = [_split_by_dest(name, jnp.stack([grads[li][name] for li in range(depth)])) for name in names]
    mine.append(small_pack.reshape(2, 1, 1, small_pack.shape[0] // 2, small_pack.shape[1]))
    theirs = _swap_halves(mine)
    wire = [BF16] * len(names) + [F32]
    pair_sums = [_add2(lax.dynamic_index_in_dim(g, core, 0, keepdims=False), a, dt, "sum_core_pair")
                 for g, a, dt in zip(mine, theirs, wire)]
    from_chips = _scatter_chips(pair_sums)
    halves = []
    for t, (ps, others) in enumerate(zip(pair_sums, from_chips)):
        small = t == len(names)
        own = ps[0] if small else lax.dynamic_index_in_dim(ps, chip, 0, keepdims=False)
        cols = own.shape[-1]
        total = _sum_chips(own.reshape(-1, cols), others.reshape(3, -1, cols), small,
                           "sum_chips_small" if small else "sum_chips")
        halves.append(total.reshape(own.shape))
    theirs = _swap_cores(halves)
    out = {name: _join_halves(h, o, core, 1) for name, h, o in zip(names, halves[:-1], theirs[:-1])}
    return out, _join_halves(halves[-1], theirs[-1], core, 1).reshape(small_pack.shape)


def _step(p, m, v, x, c, target):
    depth = p['norm_g'].shape[0]
    d = D_MODEL
    ix, iy, ic = lax.axis_index("x"), lax.axis_index("y"), lax.axis_index("c")
    chip = 2 * ix + iy
    dev = 4 * ix + 2 * iy + ic
    x0 = x[0]

    c_pad = jnp.pad(c, ((0, SUBLANES - 1), (0, 0)))
    c_all = _exchange(c_pad, 'xyc', False, "gather_c")[:, 0, :]
    n_ada = p['w_ada'].shape[-1]
    b_shard = lax.dynamic_slice_in_dim(p['b_ada'], chip * n_ada, n_ada, axis=1)[:, None, :]
    mod_shard = _ada_fwd(c_all, p['w_ada'].astype(BF16), b_shard)
    mod_all = _exchange(mod_shard.reshape(depth * N_DEV, n_ada), 'xy', False, "gather_mod")
    mod_all = jnp.transpose(mod_all.reshape(N_CHIPS, depth, N_DEV, n_ada), (1, 2, 0, 3)).reshape(depth, N_DEV, 3 * d)
    mods = lax.dynamic_index_in_dim(mod_all, dev, axis=1, keepdims=True)

    place = (ix, iy, ic, chip)
    fulls = _gather_big(p, place, depth)

    saved = []
    xs = x0
    for li in range(depth):
        xs, s = _layer_fwd(xs, p, fulls[li], li, mods[li])
        saved.append(s)
    dx, hst = _loss_head(xs, _row(p['final_g']), target[0])
    loss = lax.psum(hst[1, 0], ("x", "y", "c"))

    grads = [None] * depth
    dmods = [None] * depth
    for li in reversed(range(depth)):
        dx, grads[li], dmods[li] = _layer_bwd(dx, saved[li], p, fulls[li], li, mods[li])

    dmod_pad = jnp.pad(jnp.stack(dmods), ((0, SUBLANES - depth), (0, 0)))
    dmod_all = _exchange(dmod_pad, 'xyc', False, "gather_dmod")[:, :depth, :]
    dmod_cols = lax.dynamic_slice_in_dim(jnp.transpose(dmod_all, (1, 0, 2)), chip * n_ada, n_ada, axis=2)
    g_w_ada = _ada_bwd(jnp.transpose(c_all), dmod_cols)

    local_small = {k: jnp.stack([grads[li][k] for li in range(depth)]) for k in SMALL if k not in ('final_g', 'b_ada')}
    local_small['final_g'] = hst[0]
    local_small['b_ada'] = jnp.stack(dmods)
    small_pack = _pack_rows([local_small[k] for k in SMALL], PACK_COLS, F32)
    g_big, small_sum = _reduce_grads(grads, small_pack, place, depth)

    small_shapes = [p[k].shape for k in SMALL]
    grad = dict(zip(SMALL, _unpack_rows(small_sum, small_shapes)))
    grad['w_ada'] = g_w_ada
    grad.update(g_big)

    delta, new_m, new_v = {}, {}, {}
    outs = _adamw(_pack_rows([p[k] for k in SMALL], PACK_COLS, F32), small_sum,
                  _pack_rows([m[k] for k in SMALL], PACK_COLS, F32), _pack_rows([v[k] for k in SMALL], PACK_COLS, F32),
                  name="adamw_small")
    for res, o in zip((delta, new_m, new_v), outs):
        res.update(zip(SMALL, _unpack_rows(o, small_shapes)))
    for name in ['w_ada'] + [n for n, _ in BIG]:
        shape = p[name].shape
        two_d = (-1, shape[-1])
        outs = _adamw(p[name].reshape(two_d), grad[name].reshape(two_d), m[name].reshape(two_d), v[name].reshape(two_d),
                      name="adamw_" + name)
        delta[name], new_m[name], new_v[name] = (o.reshape(shape) for o in outs)

    return (loss, dx[None], *[grad[k] for k in WEIGHTS], *[delta[k] for k in WEIGHTS],
            *[new_m[k] for k in WEIGHTS], *[new_v[k] for k in WEIGHTS])


def kernel(x, c, norm_g, w_ada, b_ada, w_in, attn_sinks, ssm_a_re, ssm_a_im, ssm_log_dt, ssm_b_re, ssm_b_im, ssm_c_re, ssm_c_im, ssm_d, w_glu, b_glu, w_pool, pool_scale, w_br_att, w_br_ssm, w_br_pool, w_out, final_g, loss_target, m_norm_g, m_w_ada, m_b_ada, m_w_in, m_attn_sinks, m_ssm_a_re, m_ssm_a_im, m_ssm_log_dt, m_ssm_b_re, m_ssm_b_im, m_ssm_c_re, m_ssm_c_im, m_ssm_d, m_w_glu, m_b_glu, m_w_pool, m_pool_scale, m_w_br_att, m_w_br_ssm, m_w_br_pool, m_w_out, m_final_g, v_norm_g, v_w_ada, v_b_ada, v_w_in, v_attn_sinks, v_ssm_a_re, v_ssm_a_im, v_ssm_log_dt, v_ssm_b_re, v_ssm_b_im, v_ssm_c_re, v_ssm_c_im, v_ssm_d, v_w_glu, v_b_glu, v_w_pool, v_pool_scale, v_w_br_att, v_w_br_ssm, v_w_br_pool, v_w_out, v_final_g):
    p = dict(zip(WEIGHTS, (norm_g, w_ada, b_ada, w_in, attn_sinks, ssm_a_re, ssm_a_im, ssm_log_dt, ssm_b_re, ssm_b_im,
                           ssm_c_re, ssm_c_im, ssm_d, w_glu, b_glu, w_pool, pool_scale, w_br_att, w_br_ssm, w_br_pool,
                           w_out, final_g)))
    m = dict(zip(WEIGHTS, (m_norm_g, m_w_ada, m_b_ada, m_w_in, m_attn_sinks, m_ssm_a_re, m_ssm_a_im, m_ssm_log_dt,
                           m_ssm_b_re, m_ssm_b_im, m_ssm_c_re, m_ssm_c_im, m_ssm_d, m_w_glu, m_b_glu, m_w_pool,
                           m_pool_scale, m_w_br_att, m_w_br_ssm, m_w_br_pool, m_w_out, m_final_g)))
    v = dict(zip(WEIGHTS, (v_norm_g, v_w_ada, v_b_ada, v_w_in, v_attn_sinks, v_ssm_a_re, v_ssm_a_im, v_ssm_log_dt,
                           v_ssm_b_re, v_ssm_b_im, v_ssm_c_re, v_ssm_c_im, v_ssm_d, v_w_glu, v_b_glu, v_w_pool,
                           v_pool_scale, v_w_br_att, v_w_br_ssm, v_w_br_pool, v_w_out, v_final_g)))
    return _step(p, m, v, x, c, loss_target)
```

```python
import functools
import math

import jax
import jax.numpy as jnp
from jax import lax
from jax.experimental import pallas as pl
from jax.experimental.pallas import tpu as pltpu

F32 = jnp.float32
BF16 = jnp.bfloat16

D_MODEL = 1024
CHUNK = 64
N_HEADS = 8
N_KV_HEADS = 2
HEAD_DIM = 64
Q_PER_KV = N_HEADS // N_KV_HEADS
HALO = 128
ATT_W = 512
KV_W = 128
SSM_W = 512
SSM_GROUP = 16
SSM_GROUPS = 32
SSM_STATE = 64
POOL_W = 512
POOL_GW = 128
IN_W = 6400
EPS = 1e-6
NEG_INF = -1e30
ADAM_LR = 0.001
ADAM_B1 = 0.9
ADAM_B2 = 0.999
ADAM_EPS = 1e-08
ADAM_WD = 0.01
ADAM_STEP = 10

OFF_GL, OFF_Q, OFF_ZA, OFF_US, OFF_ZS, OFF_UP, OFF_ZP, OFF_KV = 0, 3072, 3584, 4096, 4608, 5120, 5632, 6144
_PERM_PIECES = ((3328, 3072), (0, 512), (1792, 512), (768, 512), (2304, 512), (1280, 512), (2816, 512), (512, 256))

LANES = 128
SUBLANES = 8
N_SBLK = SSM_GROUPS * SSM_STATE // LANES
VMEM_LIMIT = 48 * 1024 * 1024

N_CHIPS = 4
N_DEV = 8

WEIGHTS = ['norm_g', 'w_ada', 'b_ada', 'w_in', 'attn_sinks', 'ssm_a_re', 'ssm_a_im', 'ssm_log_dt', 'ssm_b_re',
           'ssm_b_im', 'ssm_c_re', 'ssm_c_im', 'ssm_d', 'w_glu', 'b_glu', 'w_pool', 'pool_scale', 'w_br_att',
           'w_br_ssm', 'w_br_pool', 'w_out', 'final_g']
SMALL = ['norm_g', 'b_ada', 'attn_sinks', 'ssm_a_re', 'ssm_a_im', 'ssm_log_dt', 'ssm_b_re', 'ssm_b_im', 'ssm_c_re',
         'ssm_c_im', 'ssm_d', 'b_glu', 'w_pool', 'pool_scale', 'final_g']
BIG = (('w_in', (1024, 1600)), ('w_br_att', (512, 256)), ('w_br_ssm', (512, 256)), ('w_br_pool', (512, 256)),
       ('w_out', (256, 1024)), ('w_glu', (128, 512)))
PACK_COLS = 1024


def _params(sem=None):
    return pltpu.CompilerParams(dimension_semantics=sem, vmem_limit_bytes=VMEM_LIMIT)


def _row(v):
    return v.reshape(1, -1)


def _full(shape):
    nd = len(shape)
    return pl.BlockSpec(shape, lambda *_: (0,) * nd)


def _sigmoid(v):
    return 1.0 / (1.0 + jnp.exp(-v))


def _silu_and_grad(z):
    s = _sigmoid(z)
    return z * s, s * (1.0 + z * (1.0 - s))


_GELU_K = math.sqrt(2.0 / math.pi)


def _gelu_and_grad(v):
    inner = _GELU_K * (v + 0.044715 * v * v * v)
    th = jnp.tanh(inner)
    val = 0.5 * v * (1.0 + th)
    grad = 0.5 * (1.0 + th) + 0.5 * v * (1.0 - th * th) * _GELU_K * (1.0 + 3 * 0.044715 * v * v)
    return val, grad


_NN = (((1,), (0,)), ((), ()))
_NT = (((1,), (1,)), ((), ()))
_TN = (((0,), (0,)), ((), ()))


def _dot(a, b, dims=_NN):
    return lax.dot_general(a.astype(BF16), b.astype(BF16), dims, preferred_element_type=F32)


def _matmul(a, b, mode, *, tm, tn, tk, name, out_dtype=F32, n_outer=False):
    if mode == 'nn':
        (m, k), (_, n) = a.shape, b.shape
    elif mode == 'nt':
        (m, k), (n, _) = a.shape, b.shape
    else:
        (k, m), (_, n) = a.shape, b.shape
    tm, tn, tk = min(tm, m), min(tn, n), min(tk, k)
    assert m % tm == 0 and n % tn == 0 and k % tk == 0, (name, a.shape, b.shape)
    nk = k // tk
    dims = {'nn': _NN, 'nt': _NT, 'tn': _TN}[mode]

    def body(a_ref, b_ref, o_ref, acc_ref):
        if nk == 1:
            o_ref[...] = _dot(a_ref[...], b_ref[...], dims).astype(out_dtype)
            return
        kk = pl.program_id(2)

        @pl.when(kk == 0)
        def _():
            acc_ref[...] = jnp.zeros_like(acc_ref)

        acc_ref[...] += _dot(a_ref[...], b_ref[...], dims)

        @pl.when(kk == nk - 1)
        def _():
            o_ref[...] = acc_ref[...].astype(out_dtype)

    def spec(shape, index):
        if n_outer:
            return pl.BlockSpec(shape, lambda j, i, kk: index(i, j, kk))
        return pl.BlockSpec(shape, index)

    a_spec = spec((tk, tm), lambda i, j, kk: (kk, i)) if mode == 'tn' else spec((tm, tk), lambda i, j, kk: (i, kk))
    b_spec = spec((tn, tk), lambda i, j, kk: (j, kk)) if mode == 'nt' else spec((tk, tn), lambda i, j, kk: (kk, j))
    grid = (n // tn, m // tm, nk) if n_outer else (m // tm, n // tn, nk)
    return pl.pallas_call(
        body, name=name, grid=grid, in_specs=[a_spec, b_spec], out_specs=spec((tm, tn), lambda i, j, kk: (i, j)),
        out_shape=jax.ShapeDtypeStruct((m, n), out_dtype), scratch_shapes=[pltpu.VMEM((tm, tn), F32)],
        compiler_params=_params(("parallel", "parallel", "arbitrary")))(a, b)


ROWS_NORM = 512


def _norm_fwd(x, g, mod):
    l, d = x.shape
    tr = min(ROWS_NORM, l)

    def body(x_ref, g_ref, mod_ref, h_ref):
        xv = x_ref[...]
        r = lax.rsqrt(jnp.mean(xv * xv, axis=-1, keepdims=True) + EPS)
        shift, scale = mod_ref[:, 0:d], mod_ref[:, d:2 * d]
        h_ref[...] = ((xv * r * g_ref[...]) * (1.0 + scale) + shift).astype(BF16)

    return pl.pallas_call(
        body, name="norm_fwd", grid=(l // tr,),
        in_specs=[pl.BlockSpec((tr, d), lambda i: (i, 0)), _full((1, d)), _full((1, 3 * d))],
        out_specs=pl.BlockSpec((tr, d), lambda i: (i, 0)), out_shape=jax.ShapeDtypeStruct((l, d), BF16),
        compiler_params=_params(("parallel",)))(x, g, mod)


def _norm_bwd(x, dh, dxo, g, mod):
    l, d = x.shape
    tr = min(ROWS_NORM, l)

    def body(x_ref, dh_ref, dxo_ref, g_ref, mod_ref, dx_ref, st_ref):
        @pl.when(pl.program_id(0) == 0)
        def _():
            st_ref[...] = jnp.zeros_like(st_ref)

        xv, dhv = x_ref[...], dh_ref[...]
        r = lax.rsqrt(jnp.mean(xv * xv, axis=-1, keepdims=True) + EPS)
        xn = xv * r
        gv = g_ref[...]
        sc1 = 1.0 + mod_ref[:, d:2 * d]
        dxn = dhv * gv * sc1
        dx_ref[...] = dxo_ref[...] + r * (dxn - xn * jnp.mean(dxn * xn, axis=-1, keepdims=True))
        st_ref[0:1, :] += jnp.sum(dhv, axis=0, keepdims=True)
        st_ref[1:2, :] += jnp.sum(dhv * xn * gv, axis=0, keepdims=True)
        st_ref[2:3, :] += jnp.sum(dhv * xn * sc1, axis=0, keepdims=True)

    blk = pl.BlockSpec((tr, d), lambda i: (i, 0))
    return pl.pallas_call(
        body, name="norm_bwd", grid=(l // tr,),
        in_specs=[blk, blk, blk, _full((1, d)), _full((1, 3 * d))],
        out_specs=[blk, _full((SUBLANES, d))],
        out_shape=[jax.ShapeDtypeStruct((l, d), F32), jax.ShapeDtypeStruct((SUBLANES, d), F32)],
        compiler_params=_params(("arbitrary",)))(x, dh, dxo, g, mod)


def _loss_head(x, g, target):
    l, d = x.shape
    tr = min(ROWS_NORM, l)

    def body(x_ref, g_ref, t_ref, dx_ref, st_ref):
        @pl.when(pl.program_id(0) == 0)
        def _():
            st_ref[...] = jnp.zeros_like(st_ref)

        xv = x_ref[...]
        r = lax.rsqrt(jnp.mean(xv * xv, axis=-1, keepdims=True) + EPS)
        xn = xv * r
        gv = g_ref[...]
        err = xn * gv - t_ref[...]
        part = 0.5 * jnp.sum(jnp.mean(err * err, axis=-1, keepdims=True), axis=0, keepdims=True)
        dy = err * (1.0 / d)
        dxn = dy * gv
        dx_ref[...] = r * (dxn - xn * jnp.mean(dxn * xn, axis=-1, keepdims=True))
        st_ref[0:1, :] += jnp.sum(dy * xn, axis=0, keepdims=True)
        st_ref[1:2, :] += jnp.broadcast_to(part, (1, d))

    blk = pl.BlockSpec((tr, d), lambda i: (i, 0))
    return pl.pallas_call(
        body, name="loss_head", grid=(l // tr,), in_specs=[blk, _full((1, d)), blk],
        out_specs=[blk, _full((SUBLANES, d))],
        out_shape=[jax.ShapeDtypeStruct((l, d), F32), jax.ShapeDtypeStruct((SUBLANES, d), F32)],
        compiler_params=_params(("arbitrary",)))(x, g, target)


ROWS_ATT = 128
_SLOPES = tuple(2.0 ** (-8.0 * (h + 1) / N_HEADS) for h in range(N_HEADS))


def _att_mask(i, t):
    r = lax.broadcasted_iota(jnp.int32, (t, t + HALO), 0)
    j = lax.broadcasted_iota(jnp.int32, (t, t + HALO), 1)
    dist = jnp.abs(r + HALO - j).astype(F32)
    rc, jc = r // CHUNK, j // CHUNK
    allowed = (jc >= rc) & (jc <= rc + 2) & ((j >= HALO) | (i > 0))
    return dist, allowed


def _att_probs(qh, k, dist, allowed, slope, sink):
    s = _dot(qh, k, _NT) * (1.0 / math.sqrt(HEAD_DIM)) - slope * dist
    s = jnp.where(allowed, s, NEG_INF)
    m = jnp.maximum(jnp.max(s, axis=1, keepdims=True), sink)
    e = jnp.exp(s - m)
    es = jnp.exp(sink - m)
    den = jnp.sum(e, axis=1, keepdims=True) + es
    return e / den, es / den


def _att_specs(t):
    q_spec = pl.BlockSpec((t, ATT_W), lambda i: (i, OFF_Q // ATT_W))
    kv_spec = pl.BlockSpec((t, 2 * KV_W), lambda i: (i, OFF_KV // (2 * KV_W)))
    halo_spec = pl.BlockSpec((HALO, 2 * KV_W), lambda i: (jnp.maximum(i * (t // HALO) - 1, 0), OFF_KV // (2 * KV_W)))
    return q_spec, kv_spec, halo_spec


def _attention_fwd(proj, sinks):
    l = proj.shape[0]
    t = min(ROWS_ATT, l)

    def body(sink_ref, q_ref, kv_ref, halo_ref, o_ref):
        dist, allowed = _att_mask(pl.program_id(0), t)
        kv = jnp.concatenate([halo_ref[...], kv_ref[...]], axis=0)
        for h in range(N_HEADS):
            kh = h // Q_PER_KV
            k = kv[:, kh * HEAD_DIM:(kh + 1) * HEAD_DIM]
            v = kv[:, KV_W + kh * HEAD_DIM:KV_W + (kh + 1) * HEAD_DIM]
            p, _ = _att_probs(q_ref[:, h * HEAD_DIM:(h + 1) * HEAD_DIM], k, dist, allowed, _SLOPES[h], sink_ref[h])
            o_ref[:, h * HEAD_DIM:(h + 1) * HEAD_DIM] = _dot(p, v)

    q_spec, kv_spec, halo_spec = _att_specs(t)
    return pl.pallas_call(
        body, name="attention_fwd", grid=(l // t,),
        in_specs=[pl.BlockSpec(memory_space=pltpu.SMEM), q_spec, kv_spec, halo_spec],
        out_specs=pl.BlockSpec((t, ATT_W), lambda i: (i, 0)), out_shape=jax.ShapeDtypeStruct((l, ATT_W), F32),
        compiler_params=_params(("parallel",)))(sinks, proj, proj, proj)


def _attention_bwd(proj, sinks, do):
    l = proj.shape[0]
    t = min(ROWS_ATT, l)

    def body(sink_ref, q_ref, kv_ref, halo_ref, do_ref, dq_ref, dkv_ref, dsink_ref):
        i = pl.program_id(0)

        @pl.when(i == 0)
        def _():
            dkv_ref[...] = jnp.zeros_like(dkv_ref)
            dsink_ref[...] = jnp.zeros_like(dsink_ref)

        dist, allowed = _att_mask(i, t)
        kv = jnp.concatenate([halo_ref[...], kv_ref[...]], axis=0)
        rows = pl.ds(pl.multiple_of(i * t, t), t + HALO)
        for kh in range(N_KV_HEADS):
            k = kv[:, kh * HEAD_DIM:(kh + 1) * HEAD_DIM]
            v = kv[:, KV_W + kh * HEAD_DIM:KV_W + (kh + 1) * HEAD_DIM]
            dk = jnp.zeros((t + HALO, HEAD_DIM), F32)
            dv = jnp.zeros((t + HALO, HEAD_DIM), F32)
            for h in range(kh * Q_PER_KV, (kh + 1) * Q_PER_KV):
                qh = q_ref[:, h * HEAD_DIM:(h + 1) * HEAD_DIM]
                doh = do_ref[:, h * HEAD_DIM:(h + 1) * HEAD_DIM]
                p, ps = _att_probs(qh, k, dist, allowed, _SLOPES[h], sink_ref[h])
                dp = _dot(doh, v, _NT)
                delta = jnp.sum(p * dp, axis=1, keepdims=True)
                ds = p * (dp - delta) * (1.0 / math.sqrt(HEAD_DIM))
                dsink_ref[h:h + 1, :] += jnp.broadcast_to(-jnp.sum(ps * delta, axis=0, keepdims=True), (1, LANES))
                dq_ref[:, h * HEAD_DIM:(h + 1) * HEAD_DIM] = _dot(ds, k)
                dk = dk + _dot(ds, qh, _TN)
                dv = dv + _dot(p, doh, _TN)
            dkv_ref[rows, kh * HEAD_DIM:(kh + 1) * HEAD_DIM] += dk
            dkv_ref[rows, KV_W + kh * HEAD_DIM:KV_W + (kh + 1) * HEAD_DIM] += dv

    q_spec, kv_spec, halo_spec = _att_specs(t)
    blk = pl.BlockSpec((t, ATT_W), lambda i: (i, 0))
    return pl.pallas_call(
        body, name="attention_bwd", grid=(l // t,),
        in_specs=[pl.BlockSpec(memory_space=pltpu.SMEM), q_spec, kv_spec, halo_spec, blk],
        out_specs=[blk, _full((HALO + l, 2 * KV_W)), _full((N_HEADS, LANES))],
        out_shape=[jax.ShapeDtypeStruct((l, ATT_W), F32), jax.ShapeDtypeStruct((HALO + l, 2 * KV_W), F32),
                   jax.ShapeDtypeStruct((N_HEADS, LANES), F32)],
        compiler_params=_params(("arbitrary",)))(sinks, proj, proj, proj, do)


ROWS_SSM = 512
SCAN_UNROLL = 4


def _ssm_discretize(a_re, a_im, log_dt, b_re, b_im):
    lam = lax.complex(a_re, a_im)
    dt = jnp.exp(log_dt)[:, None]
    lam_bar = jnp.exp(lam * dt)
    b_bar = ((lam_bar - 1.0) / lam)[..., None] * lax.complex(b_re, b_im)
    return lam, dt, lam_bar, b_bar


def _ssm_block_diag(m):
    e = m.reshape(N_SBLK, 2, SSM_GROUP, SSM_STATE)
    e = e[:, :, :, None, :] * jnp.eye(2, dtype=m.dtype)[None, :, None, :, None]
    e = e.reshape(N_SBLK, 2 * SSM_GROUP, LANES)
    oh = jax.nn.one_hot(jnp.arange(N_SBLK) % 4, 4, dtype=m.dtype)
    return (oh[:, :, None, None] * e[:, None]).reshape(N_SBLK, LANES, LANES)


def _ssm_tables(a_re, a_im, log_dt, b_re, b_im, c_re, c_im):
    _, _, lam_bar, b_bar = _ssm_discretize(a_re, a_im, log_dt, b_re, b_im)
    lam_blk = jnp.stack([jnp.real(lam_bar).reshape(N_SBLK, LANES), jnp.imag(lam_bar).reshape(N_SBLK, LANES)], axis=1)
    bt = jnp.transpose(b_bar, (0, 2, 1))
    bb = jnp.concatenate([_ssm_block_diag(jnp.real(bt)), _ssm_block_diag(jnp.imag(bt))], axis=2)
    cb = jnp.concatenate([jnp.transpose(_ssm_block_diag(c_re), (0, 2, 1)),
                          jnp.transpose(_ssm_block_diag(-c_im), (0, 2, 1))], axis=1)
    return lam_blk, bb, cb


def _ssm_powers(a_re, a_im, log_dt):
    lam = lax.complex(a_re, a_im)
    dt = jnp.exp(log_dt)[:, None]
    k = jnp.arange(1, SUBLANES + 1, dtype=F32)
    pw = jnp.exp((lam * dt)[None] * k[:, None, None]).reshape(SUBLANES, N_SBLK, LANES)
    pw = jnp.transpose(pw, (1, 0, 2))
    rev = pw[:, ::-1]
    return jnp.concatenate([jnp.real(pw), jnp.imag(pw), jnp.real(rev), jnp.imag(rev)], axis=1)


def _scan_consts(pw_ref, reverse):
    row = lax.broadcasted_iota(jnp.int32, (SUBLANES, LANES), 0)
    sign = -1.0 if reverse else 1.0

    def power(k):
        return (jnp.broadcast_to(pw_ref[0, k - 1:k, :], (SUBLANES, LANES)),
                sign * jnp.broadcast_to(pw_ref[0, SUBLANES + k - 1:SUBLANES + k, :], (SUBLANES, LANES)))

    steps = []
    for d in (1, 2, 4):
        pr, pi = power(d)
        keep = (row < SUBLANES - d) if reverse else (row >= d)
        steps.append((d, jnp.where(keep, pr, 0.0), jnp.where(keep, pi, 0.0)))
    base = 2 * SUBLANES if reverse else 0
    return (steps, pw_ref[0, base:base + SUBLANES, :], sign * pw_ref[0, base + SUBLANES:base + 2 * SUBLANES, :],
            power(SUBLANES))


def _scan_tile(xr, xi, steps, reverse):
    for d, ar, ai in steps:
        shift = SUBLANES - d if reverse else d
        rr, ri = pltpu.roll(xr, shift, 0), pltpu.roll(xi, shift, 0)
        xr, xi = xr + (ar * rr - ai * ri), xi + (ar * ri + ai * rr)
    return xr, xi


def _bcast_row(v, r):
    return jnp.broadcast_to(v[r:r + 1, :], (SUBLANES, LANES))


def _scan_forward(s_scr, pw_ref, n_tiles):
    steps, pr, pi, (l8r, l8i) = _scan_consts(pw_ref, False)

    def tiles(g, carry):
        cr, ci = carry
        rows = [pl.ds(pl.multiple_of((g * SCAN_UNROLL + u) * SUBLANES, SUBLANES), SUBLANES) for u in range(SCAN_UNROLL)]
        loaded = [(s_scr[r, 0:LANES], s_scr[r, LANES:2 * LANES]) for r in rows]
        local = [_scan_tile(xr, xi, steps, False) for xr, xi in loaded]
        for r, (xr, xi) in zip(rows, local):
            s_scr[r, 0:LANES] = xr + (pr * cr - pi * ci)
            s_scr[r, LANES:2 * LANES] = xi + (pr * ci + pi * cr)
            cr, ci = ((l8r * cr - l8i * ci) + _bcast_row(xr, SUBLANES - 1),
                      (l8r * ci + l8i * cr) + _bcast_row(xi, SUBLANES - 1))
        return cr, ci

    zero = jnp.zeros((SUBLANES, LANES), F32)
    lax.fori_loop(0, n_tiles // SCAN_UNROLL, tiles, (zero, zero))


def _row_chunks(l):
    rc = min(ROWS_SSM, l)
    return rc, l // rc


def _ssm_fwd(proj, bb, cb, pw):
    l = proj.shape[0]
    rc, n_chunks = _row_chunks(l)

    def body(u_ref, bb_ref, cb_ref, pw_ref, y_ref, s_scr):
        j = pl.program_id(0)

        def fill(ci, _):
            rows = pl.ds(pl.multiple_of(ci * rc, rc), rc)
            s_scr[rows, :] = _dot(u_ref[rows, :], bb_ref[0])
            return 0

        lax.fori_loop(0, n_chunks, fill, 0)
        _scan_forward(s_scr, pw_ref, l // SUBLANES)

        @pl.when(j % 4 == 0)
        def _():
            y_ref[...] = jnp.zeros_like(y_ref)

        def emit(ci, _):
            rows = pl.ds(pl.multiple_of(ci * rc, rc), rc)
            y_ref[rows, :] += _dot(s_scr[rows, :], cb_ref[0])
            return 0

        lax.fori_loop(0, n_chunks, emit, 0)

    return pl.pallas_call(
        body, name="ssm_fwd", grid=(N_SBLK,),
        in_specs=[pl.BlockSpec((l, LANES), lambda j: (0, OFF_US // LANES + j // 4)),
                  pl.BlockSpec((1, LANES, 2 * LANES), lambda j: (j, 0, 0)),
                  pl.BlockSpec((1, 2 * LANES, LANES), lambda j: (j, 0, 0)),
                  pl.BlockSpec((1, 4 * SUBLANES, LANES), lambda j: (j, 0, 0))],
        out_specs=[pl.BlockSpec((l, LANES), lambda j: (0, j // 4)), pl.BlockSpec((l, 2 * LANES), lambda j: (0, j))],
        out_shape=[jax.ShapeDtypeStruct((l, SSM_W), F32), jax.ShapeDtypeStruct((l, N_SBLK * 2 * LANES), F32)],
        compiler_params=_params(("arbitrary",)))(proj, bb, cb, pw)


def _ssm_bwd(proj, states, dy, bb, cb, pw):
    l = proj.shape[0]
    rc, n_chunks = _row_chunks(l)
    n_tiles = l // SUBLANES

    def body(u_ref, s_scr, dy_ref, bb_ref, cb_ref, pw_ref, du_ref, dbb_ref, dcb_ref, dlam_ref, a_scr):
        j = pl.program_id(0)
        dcb_ref[...] = jnp.zeros_like(dcb_ref)

        def through_c(ci, _):
            rows = pl.ds(pl.multiple_of(ci * rc, rc), rc)
            dyv = dy_ref[rows, :]
            dcb_ref[0] += _dot(s_scr[rows, :], dyv, _TN)
            a_scr[rows, :] = _dot(dyv, cb_ref[0], _NT)
            return 0

        lax.fori_loop(0, n_chunks, through_c, 0)

        steps, pr, pi, (l8r, l8i) = _scan_consts(pw_ref, True)
        row = lax.broadcasted_iota(jnp.int32, (SUBLANES, LANES), 0)

        last = row == SUBLANES - 1

        def tiles(g, carry):
            cr, ci, accr, acci = carry
            rows = [pl.ds(pl.multiple_of((n_tiles - 1 - (g * SCAN_UNROLL + u)) * SUBLANES, SUBLANES), SUBLANES)
                    for u in range(SCAN_UNROLL)]
            loaded = [(a_scr[r, 0:LANES], a_scr[r, LANES:2 * LANES]) for r in rows]
            states = [(s_scr[r, 0:LANES], s_scr[r, LANES:2 * LANES]) for r in rows]
            local = [_scan_tile(xr, xi, steps, True) for xr, xi in loaded]
            for r, (xr, xi), (sr, si) in zip(rows, local, states):
                gr, gi = xr + (pr * cr - pi * ci), xi + (pr * ci + pi * cr)
                a_scr[r, 0:LANES] = gr
                a_scr[r, LANES:2 * LANES] = gi
                ur = jnp.where(last, cr, pltpu.roll(gr, SUBLANES - 1, 0))
                ui = jnp.where(last, ci, pltpu.roll(gi, SUBLANES - 1, 0))
                accr, acci = accr + (ur * sr + ui * si), acci + (ui * sr - ur * si)
                cr, ci = (l8r * cr - l8i * ci) + _bcast_row(xr, 0), (l8r * ci + l8i * cr) + _bcast_row(xi, 0)
            return cr, ci, accr, acci

        zero = jnp.zeros((SUBLANES, LANES), F32)
        _, _, accr, acci = lax.fori_loop(0, n_tiles // SCAN_UNROLL, tiles, (zero, zero, zero, zero))
        dlr = jnp.broadcast_to(jnp.sum(accr, axis=0, keepdims=True), (SUBLANES, LANES))
        dli = jnp.broadcast_to(jnp.sum(acci, axis=0, keepdims=True), (SUBLANES, LANES))
        dlam_ref[0] = jnp.where(row == 0, dlr, jnp.where(row == 1, dli, 0.0))

        dbb_ref[...] = jnp.zeros_like(dbb_ref)

        @pl.when(j % 4 == 0)
        def _():
            du_ref[...] = jnp.zeros_like(du_ref)

        def through_b(ci, _):
            rows = pl.ds(pl.multiple_of(ci * rc, rc), rc)
            av = a_scr[rows, :]
            dbb_ref[0] += _dot(u_ref[rows, :], av, _TN)
            du_ref[rows, :] += _dot(av, bb_ref[0], _NT)
            return 0

        lax.fori_loop(0, n_chunks, through_b, 0)

    return pl.pallas_call(
        body, name="ssm_bwd", grid=(N_SBLK,),
        in_specs=[pl.BlockSpec((l, LANES), lambda j: (0, OFF_US // LANES + j // 4)),
                  pl.BlockSpec((l, 2 * LANES), lambda j: (0, j)),
                  pl.BlockSpec((l, LANES), lambda j: (0, j // 4)),
                  pl.BlockSpec((1, LANES, 2 * LANES), lambda j: (j, 0, 0)),
                  pl.BlockSpec((1, 2 * LANES, LANES), lambda j: (j, 0, 0)),
                  pl.BlockSpec((1, 4 * SUBLANES, LANES), lambda j: (j, 0, 0))],
        out_specs=[pl.BlockSpec((l, LANES), lambda j: (0, j // 4)),
                   pl.BlockSpec((1, LANES, 2 * LANES), lambda j: (j, 0, 0)),
                   pl.BlockSpec((1, 2 * LANES, LANES), lambda j: (j, 0, 0)),
                   pl.BlockSpec((1, SUBLANES, LANES), lambda j: (j, 0, 0))],
        out_shape=[jax.ShapeDtypeStruct((l, SSM_W), F32), jax.ShapeDtypeStruct((N_SBLK, LANES, 2 * LANES), F32),
                   jax.ShapeDtypeStruct((N_SBLK, 2 * LANES, LANES), F32), jax.ShapeDtypeStruct((N_SBLK, SUBLANES, LANES), F32)],
        scratch_shapes=[pltpu.VMEM((l, 2 * LANES), F32)],
        compiler_params=_params(("arbitrary",)))(proj, states, dy, bb, cb, pw)


def _pool_windows(v, t, l, ahead):
    def shifted(a, d):
        if ahead:
            return jnp.where(t < l - d, pltpu.roll(a, l - d, 0), 0.0)
        return jnp.where(t >= d, pltpu.roll(a, d, 0), 0.0)

    w2 = v + shifted(v, 1)
    w4 = w2 + shifted(w2, 2)
    w8 = w4 + shifted(w4, 4)
    w16 = w8 + shifted(w8, 8)
    return w2, w4, w8, w16


def _pool_select(g, ws):
    return jnp.where(g == 0, ws[0], jnp.where(g == 1, ws[1], jnp.where(g == 2, ws[2], ws[3])))


def _pool_count(g, t):
    return jnp.minimum(t + 1, jnp.left_shift(2, g)).astype(F32)


def _pool_specs(l):
    return [pl.BlockSpec((l, POOL_GW), lambda g: (0, OFF_UP // POOL_GW + g)),
            pl.BlockSpec((1, POOL_GW, POOL_GW), lambda g: (g, 0, 0)),
            pl.BlockSpec((1, POOL_GW), lambda g: (0, g))]


def _pool_fwd(proj, w_pool, scale):
    l = proj.shape[0]

    def body(u_ref, w_ref, sc_ref, y_ref):
        g = pl.program_id(0)
        t = lax.broadcasted_iota(jnp.int32, (l, 1), 0)
        u = u_ref[...]
        pooled = _pool_select(g, _pool_windows(u, t, l, False)) / _pool_count(g, t) - u
        y_ref[...] = _dot(pooled, w_ref[0]) * sc_ref[...]

    return pl.pallas_call(
        body, name="pool_fwd", grid=(4,), in_specs=_pool_specs(l),
        out_specs=pl.BlockSpec((l, POOL_GW), lambda g: (0, g)), out_shape=jax.ShapeDtypeStruct((l, POOL_W), F32),
        compiler_params=_params(("parallel",)))(proj, w_pool, scale)


def _pool_bwd(proj, w_pool, scale, dy):
    l = proj.shape[0]

    def body(u_ref, w_ref, sc_ref, dy_ref, du_ref, dw_ref, dsc_ref):
        g = pl.program_id(0)
        t = lax.broadcasted_iota(jnp.int32, (l, 1), 0)
        u = u_ref[...]
        cnt = _pool_count(g, t)
        pooled = _pool_select(g, _pool_windows(u, t, l, False)) / cnt - u
        dyv = dy_ref[...]
        dsc_ref[...] = jnp.sum(dyv * _dot(pooled, w_ref[0]), axis=0, keepdims=True)
        dyl = dyv * sc_ref[...]
        dw_ref[0] = _dot(pooled, dyl, _TN)
        dpooled = _dot(dyl, w_ref[0], _NT)
        du_ref[...] = _pool_select(g, _pool_windows(dpooled / cnt, t, l, True)) - dpooled

    return pl.pallas_call(
        body, name="pool_bwd", grid=(4,), in_specs=_pool_specs(l) + [pl.BlockSpec((l, POOL_GW), lambda g: (0, g))],
        out_specs=[pl.BlockSpec((l, POOL_GW), lambda g: (0, g)), pl.BlockSpec((1, POOL_GW, POOL_GW), lambda g: (g, 0, 0)),
                   pl.BlockSpec((1, POOL_GW), lambda g: (0, g))],
        out_shape=[jax.ShapeDtypeStruct((l, POOL_W), F32), jax.ShapeDtypeStruct((4, POOL_GW, POOL_GW), F32),
                   jax.ShapeDtypeStruct((1, POOL_W), F32)],
        compiler_params=_params(("parallel",)))(proj, w_pool, scale, dy)


ROWS_MERGE = 128


def _merge_inputs(tr):
    def col(off, w):
        return pl.BlockSpec((tr, w), lambda i: (i, off // w))

    def act(w):
        return pl.BlockSpec((tr, w), lambda i: (i, 0))

    d = D_MODEL
    return ([col(OFF_GL, 3 * d), col(OFF_ZA, ATT_W), col(OFF_US, SSM_W), col(OFF_ZS, SSM_W), col(OFF_ZP, POOL_W),
             act(ATT_W), act(SSM_W), act(POOL_W)]
            + [_full((1, SSM_W)), _full((SSM_W, SSM_W)), _full((1, SSM_W)), _full((ATT_W, d)), _full((SSM_W, d)),
               _full((POOL_W, d)), _full((d, d)), _full((1, 3 * d))])


def _merge_forward_math(gl_ref, za_ref, us_ref, zs_ref, zp_ref, o_ref, yl_ref, yp_ref, d_ref, wg_ref, bg_ref,
                        wba_ref, wbs_ref, wbp_ref):
    d = D_MODEL
    r = {}
    r['sa'], r['dsa'] = _silu_and_grad(za_ref[...])
    r['ss'], r['dss'] = _silu_and_grad(zs_ref[...])
    r['sp'], r['dsp'] = _silu_and_grad(zp_ref[...])
    r['y_att'] = o_ref[...] * r['sa']
    y1 = yl_ref[...] + d_ref[...] * us_ref[...]
    r['y2'], r['dgelu'] = _gelu_and_grad(y1)
    r['sg'] = _sigmoid(_dot(r['y2'], wg_ref[...]) + bg_ref[...])
    r['y3'] = r['y2'] * r['sg']
    r['y_ssm'] = r['y3'] * r['ss']
    r['y_pool'] = yp_ref[...] * r['sp']
    r['g'] = [_sigmoid(gl_ref[:, b * d:(b + 1) * d]) for b in range(3)]
    r['p'] = [_dot(r['y_att'], wba_ref[...]), _dot(r['y_ssm'], wbs_ref[...]), _dot(r['y_pool'], wbp_ref[...])]
    r['merged'] = r['g'][0] * r['p'][0] + r['g'][1] * r['p'][1] + r['g'][2] * r['p'][2]
    return r


def _merge_fwd(proj, o, y_lin, y_pool, x, ssm_d, w_glu, b_glu, wba, wbs, wbp, w_out, mod):
    l, d = x.shape
    tr = min(ROWS_MERGE, l)

    def body(gl_ref, za_ref, us_ref, zs_ref, zp_ref, o_ref, yl_ref, yp_ref, d_ref, wg_ref, bg_ref, wba_ref, wbs_ref,
             wbp_ref, wo_ref, mod_ref, x_ref, xn_ref):
        r = _merge_forward_math(gl_ref, za_ref, us_ref, zs_ref, zp_ref, o_ref, yl_ref, yp_ref, d_ref, wg_ref, bg_ref,
                                wba_ref, wbs_ref, wbp_ref)
        xn_ref[...] = x_ref[...] + mod_ref[:, 2 * d:3 * d] * _dot(r['merged'], wo_ref[...])

    blk = pl.BlockSpec((tr, d), lambda i: (i, 0))
    return pl.pallas_call(
        body, name="merge_fwd", grid=(l // tr,), in_specs=_merge_inputs(tr) + [blk], out_specs=blk,
        out_shape=jax.ShapeDtypeStruct((l, d), F32), compiler_params=_params(("parallel",)))(
            proj, proj, proj, proj, proj, o, y_lin, y_pool, ssm_d, w_glu, b_glu, wba, wbs, wbp, w_out, mod, x)


def _merge_bwd(proj, o, y_lin, y_pool, dxo, ssm_d, w_glu, b_glu, wba, wbs, wbp, w_out, mod):
    l, d = dxo.shape
    tr = min(ROWS_MERGE, l)

    def body(gl_ref, za_ref, us_ref, zs_ref, zp_ref, o_ref, yl_ref, yp_ref, d_ref, wg_ref, bg_ref, wba_ref, wbs_ref,
             wbp_ref, wo_ref, mod_ref, dxo_ref,
             do_ref, dza_ref, dyl_ref, dus_ref, dzs_ref, dyp_ref, dzp_ref, dgl_ref,
             ya_ref, ys_ref, ypl_ref, y2_ref, dt_ref, dpa_ref, dps_ref, dpp_ref, mg_ref, dout_ref, st_ref):
        @pl.when(pl.program_id(0) == 0)
        def _():
            st_ref[...] = jnp.zeros_like(st_ref)

        r = _merge_forward_math(gl_ref, za_ref, us_ref, zs_ref, zp_ref, o_ref, yl_ref, yp_ref, d_ref, wg_ref, bg_ref,
                                wba_ref, wbs_ref, wbp_ref)
        dxov = dxo_ref[...]
        out = _dot(r['merged'], wo_ref[...])
        st_ref[0:1, :] += jnp.sum(dxov * out, axis=0, keepdims=True)
        dout = dxov * mod_ref[:, 2 * d:3 * d]
        dmerged = _dot(dout, wo_ref[...], _NT)
        dys = []
        for b, (w_ref, dp_ref) in enumerate(((wba_ref, dpa_ref), (wbs_ref, dps_ref), (wbp_ref, dpp_ref))):
            gb = r['g'][b]
            dgl_ref[:, b * d:(b + 1) * d] = dmerged * r['p'][b] * gb * (1.0 - gb)
            dp = dmerged * gb
            dp_ref[...] = dp.astype(BF16)
            dys.append(_dot(dp, w_ref[...], _NT))
        mg_ref[...] = r['merged'].astype(BF16)
        dout_ref[...] = dout.astype(BF16)
        ya_ref[...] = r['y_att'].astype(BF16)
        ys_ref[...] = r['y_ssm'].astype(BF16)
        ypl_ref[...] = r['y_pool'].astype(BF16)
        y2_ref[...] = r['y2'].astype(BF16)
        do_ref[...] = dys[0] * r['sa']
        dza_ref[...] = dys[0] * o_ref[...] * r['dsa']
        dy3 = dys[1] * r['ss']
        dzs_ref[...] = dys[1] * r['y3'] * r['dss']
        dt = dy3 * r['y2'] * r['sg'] * (1.0 - r['sg'])
        dt_ref[...] = dt.astype(BF16)
        dy1 = (dy3 * r['sg'] + _dot(dt, wg_ref[...], _NT)) * r['dgelu']
        dyl_ref[...] = dy1
        dus_ref[...] = dy1 * d_ref[...]
        st_ref[1:2, 0:SSM_W] += jnp.sum(dt, axis=0, keepdims=True)
        st_ref[1:2, SSM_W:2 * SSM_W] += jnp.sum(dy1 * us_ref[...], axis=0, keepdims=True)
        dyp_ref[...] = dys[2] * r['sp']
        dzp_ref[...] = dys[2] * yp_ref[...] * r['dsp']

    blk = pl.BlockSpec((tr, d), lambda i: (i, 0))
    half = pl.BlockSpec((tr, ATT_W), lambda i: (i, 0))
    wide = pl.BlockSpec((tr, 3 * d), lambda i: (i, 0))
    sds = jax.ShapeDtypeStruct
    return pl.pallas_call(
        body, name="merge_bwd", grid=(l // tr,), in_specs=_merge_inputs(tr) + [blk],
        out_specs=[half] * 7 + [wide] + [half] * 5 + [blk] * 5 + [_full((SUBLANES, d))],
        out_shape=[sds((l, ATT_W), F32)] * 7 + [sds((l, 3 * d), F32)] + [sds((l, ATT_W), BF16)] * 5
        + [sds((l, d), BF16)] * 5 + [sds((SUBLANES, d), F32)],
        compiler_params=_params(("arbitrary",)))(
            proj, proj, proj, proj, proj, o, y_lin, y_pool, ssm_d, w_glu, b_glu, wba, wbs, wbp, w_out, mod, dxo)


def _ada_fwd(c_all, w_ada, b_shard):
    depth, d, n = w_ada.shape

    def body(c_ref, w_ref, b_ref, o_ref):
        act, _ = _silu_and_grad(c_ref[...])
        o_ref[0] = _dot(act, w_ref[0]) + b_ref[0]

    return pl.pallas_call(
        body, name="ada_fwd", grid=(depth,),
        in_specs=[_full((N_DEV, d)), pl.BlockSpec((1, d, n), lambda i: (i, 0, 0)), pl.BlockSpec((1, 1, n), lambda i: (i, 0, 0))],
        out_specs=pl.BlockSpec((1, N_DEV, n), lambda i: (i, 0, 0)), out_shape=jax.ShapeDtypeStruct((depth, N_DEV, n), F32),
        compiler_params=_params(("parallel",)))(c_all, w_ada, b_shard)


def _ada_bwd(c_all_t, dmod):
    d = c_all_t.shape[0]
    depth, _, n = dmod.shape

    def body(c_ref, dm_ref, o_ref):
        act, _ = _silu_and_grad(c_ref[...])
        acc = act[:, 0:1] * dm_ref[0, 0:1, :]
        for b in range(1, N_DEV):
            acc = acc + act[:, b:b + 1] * dm_ref[0, b:b + 1, :]
        o_ref[0] = acc

    return pl.pallas_call(
        body, name="ada_bwd", grid=(depth,),
        in_specs=[_full((d, N_DEV)), pl.BlockSpec((1, N_DEV, n), lambda i: (i, 0, 0))],
        out_specs=pl.BlockSpec((1, d, n), lambda i: (i, 0, 0)), out_shape=jax.ShapeDtypeStruct((depth, d, n), F32),
        compiler_params=_params(("parallel",)))(c_all_t, dmod)


ROWS_FLAT = 256


_RELATION_XOR = (0, 2, 1, 3)


def _sum_chips(own, others, chip_order, name):
    r, c = own.shape
    tr = math.gcd(ROWS_FLAT, r)

    def body(own_ref, oth_ref, o_ref):
        terms = [own_ref[...].astype(F32)] + [oth_ref[k].astype(F32) for k in range(3)]
        if chip_order:
            chip = 2 * lax.axis_index("x") + lax.axis_index("y")
            by_chip = []
            for q in range(N_CHIPS):
                rel = jnp.bitwise_xor(chip, q)
                pick = terms[3]
                for k in (2, 1, 0):
                    pick = jnp.where(rel == _RELATION_XOR[k], terms[k], pick)
                by_chip.append(pick)
            terms = by_chip
        o_ref[...] = ((terms[0] + terms[1]) + terms[2]) + terms[3]

    return pl.pallas_call(
        body, name=name, grid=(r // tr,),
        in_specs=[pl.BlockSpec((tr, c), lambda i: (i, 0)), pl.BlockSpec((3, tr, c), lambda i: (0, i, 0))],
        out_specs=pl.BlockSpec((tr, c), lambda i: (i, 0)), out_shape=jax.ShapeDtypeStruct((r, c), F32),
        compiler_params=_params(("parallel",)))(own, others)


def _add2(a, b, out_dtype, name):
    shape = a.shape
    a, b = a.reshape(-1, shape[-1]), b.reshape(-1, shape[-1])
    r, c = a.shape
    tr = math.gcd(ROWS_FLAT, r)

    def body(a_ref, b_ref, o_ref):
        o_ref[...] = (a_ref[...] + b_ref[...]).astype(out_dtype)

    blk = pl.BlockSpec((tr, c), lambda i: (i, 0))
    return pl.pallas_call(
        body, name=name, grid=(r // tr,), in_specs=[blk, blk], out_specs=blk,
        out_shape=jax.ShapeDtypeStruct((r, c), out_dtype), compiler_params=_params(("parallel",)))(a, b).reshape(shape)


def _adamw(w, g, m, v, name):
    r, c = w.shape
    tr = math.gcd(ROWS_FLAT, r)

    def body(w_ref, g_ref, m_ref, v_ref, d_ref, nm_ref, nv_ref):
        gv = g_ref[...]
        mv = ADAM_B1 * m_ref[...] + (1.0 - ADAM_B1) * gv
        vv = ADAM_B2 * v_ref[...] + (1.0 - ADAM_B2) * (gv * gv)
        m_hat = mv / (1.0 - ADAM_B1 ** ADAM_STEP)
        v_hat = vv / (1.0 - ADAM_B2 ** ADAM_STEP)
        d_ref[...] = -ADAM_LR * (m_hat / (jnp.sqrt(v_hat) + ADAM_EPS) + ADAM_WD * w_ref[...])
        nm_ref[...] = mv
        nv_ref[...] = vv

    blk = pl.BlockSpec((tr, c), lambda i: (i, 0))
    return pl.pallas_call(
        body, name=name, grid=(r // tr,), in_specs=[blk] * 4, out_specs=[blk] * 3,
        out_shape=[jax.ShapeDtypeStruct((r, c), F32)] * 3, compiler_params=_params(("parallel",)))(w, g, m, v)


_GROUP_MASKS = {
    'xy': ((1, 0, 0), (0, 1, 0), (1, 1, 0)),
    'c': ((0, 0, 1),),
    'xyc': ((0, 0, 1), (0, 1, 0), (0, 1, 1), (1, 0, 0), (1, 0, 1), (1, 1, 0), (1, 1, 1)),
}


def _group_rank(group, pos):
    x, y, c = pos
    return {'xy': 2 * x + y, 'c': c, 'xyc': 4 * x + 2 * y + c}[group]


def _exchange(src, group, scatter, name):
    masks = _GROUP_MASKS[group]
    n = len(masks) + 1
    shape = src.shape[1:] if scatter else src.shape
    assert (not scatter) or src.shape[0] == n

    def body(src_ref, out_ref, send_sems, recv_sems, local_sem):
        me = (lax.axis_index("x"), lax.axis_index("y"), lax.axis_index("c"))
        my_rank = _group_rank(group, me)

        def mine_for(rank):
            return src_ref.at[rank] if scatter else src_ref

        local = pltpu.make_async_copy(mine_for(my_rank), out_ref.at[my_rank], local_sem)
        local.start()
        sends = []
        for k, mask in enumerate(masks):
            peer = tuple(1 - p if f else p for p, f in zip(me, mask))
            peer_rank = _group_rank(group, peer)
            send = pltpu.make_async_remote_copy(
                src_ref=mine_for(peer_rank), dst_ref=out_ref.at[my_rank], send_sem=send_sems.at[k],
                recv_sem=recv_sems.at[k], device_id=peer, device_id_type=pl.DeviceIdType.MESH)
            send.start()
            sends.append((send, peer, peer_rank))
        for k, (send, peer, peer_rank) in enumerate(sends):
            pltpu.make_async_remote_copy(
                src_ref=mine_for(peer_rank), dst_ref=out_ref.at[peer_rank], send_sem=send_sems.at[k],
                recv_sem=recv_sems.at[k], device_id=peer, device_id_type=pl.DeviceIdType.MESH).wait_recv()
        for send, _, _ in sends:
            send.wait_send()
        local.wait()

    return pl.pallas_call(
        body, name=name, in_specs=[pl.BlockSpec(memory_space=pl.ANY)], out_specs=pl.BlockSpec(memory_space=pl.ANY),
        out_shape=jax.ShapeDtypeStruct((n,) + tuple(shape), src.dtype),
        scratch_shapes=[pltpu.SemaphoreType.DMA((n - 1,)), pltpu.SemaphoreType.DMA((n - 1,)), pltpu.SemaphoreType.DMA(())],
    )(src)


CHUNK_BYTES = 1 << 20
MIN_CHUNK_ROWS = 64


def _row_parts(rows, cols, itemsize):
    n = 1
    while rows % (2 * n) == 0 and rows // (2 * n) >= MIN_CHUNK_ROWS and rows * cols * itemsize > n * CHUNK_BYTES:
        n *= 2
    return n


def _remote(src, dst, send_sem, recv_sem, peer):
    return pltpu.make_async_remote_copy(src_ref=src, dst_ref=dst, send_sem=send_sem, recv_sem=recv_sem,
                                        device_id=peer, device_id_type=pl.DeviceIdType.MESH)


def _start_rows(src, dst, send_sem, recv_sem, peer):
    rows, cols = src.shape
    n = _row_parts(rows, cols, jnp.dtype(src.dtype).itemsize)
    pr = rows // n
    for i in range(n):
        _remote(src.at[pl.ds(i * pr, pr), :], dst.at[pl.ds(i * pr, pr), :], send_sem, recv_sem, peer).start()


def _mesh_place():
    x, y, c = lax.axis_index("x"), lax.axis_index("y"), lax.axis_index("c")
    other_chips = ((1 - x, y), (x, 1 - y), (1 - x, 1 - y))
    return x, y, c, 2 * x + y, (x, y, 1 - c), other_chips


def _comm_call(body, name, ins, out_shapes, sem_counts):
    any_spec = pl.BlockSpec(memory_space=pl.ANY)
    return pl.pallas_call(
        body, name=name, in_specs=[any_spec] * len(ins), out_specs=[any_spec] * len(out_shapes), out_shape=out_shapes,
        scratch_shapes=[pltpu.SemaphoreType.DMA((n,)) for n in sem_counts])(*ins)


def _gather_weights(shards):
    nt = len(shards)

    def body(*refs):
        ins, over_ici, from_sib = refs[:nt], refs[nt:2 * nt], refs[2 * nt:3 * nt]
        send_a, recv_a, send_b, recv_b = refs[3 * nt:]
        _, _, c, _, sibling, other_chips = _mesh_place()
        for t in range(nt):
            nl, r2 = ins[t].shape[0], ins[t].shape[1] // 2
            for j, (px, py) in enumerate(other_chips):
                for li in range(nl):
                    _remote(ins[t].at[li, pl.ds(pl.multiple_of(c * r2, MIN_CHUNK_ROWS), r2), :], over_ici[t].at[j, li],
                            send_a.at[3 * t + j], recv_a.at[3 * t + j], (px, py, c)).start()
        for t in range(nt):
            for j, (px, py) in enumerate(other_chips):
                landed = over_ici[t].at[j]
                _remote(landed, landed, send_a.at[3 * t + j], recv_a.at[3 * t + j], (px, py, c)).wait_recv()
                for li in range(ins[t].shape[0]):
                    _start_rows(landed.at[li], from_sib[t].at[j, li], send_b.at[3 * t + j], recv_b.at[3 * t + j], sibling)
        for t in range(nt):
            for j in range(3):
                theirs = from_sib[t].at[j]
                _remote(theirs, theirs, send_b.at[3 * t + j], recv_b.at[3 * t + j], sibling).wait_recv()
        for t in range(nt):
            for j, (px, py) in enumerate(other_chips):
                sent = over_ici[t].at[j]
                _remote(sent, sent, send_a.at[3 * t + j], recv_a.at[3 * t + j], (px, py, c)).wait_send()
                _remote(sent, sent, send_b.at[3 * t + j], recv_b.at[3 * t + j], sibling).wait_send()

    slots = [jax.ShapeDtypeStruct((3, s.shape[0], s.shape[1] // 2, s.shape[2]), s.dtype) for s in shards]
    outs = _comm_call(body, "gather_weights", shards, slots + slots, (3 * nt, 3 * nt, 3 * nt, 3 * nt))
    return outs[:nt], outs[nt:]


def _swap_halves(gs):
    nt = len(gs)

    def body(*refs):
        ins, outs = refs[:nt], refs[nt:2 * nt]
        send_sems, recv_sems = refs[2 * nt:]
        _, _, c, _, sibling, _ = _mesh_place()
        for t in range(nt):
            _, nd, nl, _, _ = ins[t].shape
            for q in range(nd):
                for li in range(nl):
                    _start_rows(ins[t].at[1 - c, q, li], outs[t].at[q, li], send_sems.at[t], recv_sems.at[t], sibling)
        for t in range(nt):
            _remote(outs[t], outs[t], send_sems.at[t], recv_sems.at[t], sibling).wait_recv()
        for t in range(nt):
            _remote(outs[t], outs[t], send_sems.at[t], recv_sems.at[t], sibling).wait_send()

    out_shapes = [jax.ShapeDtypeStruct(g.shape[1:], g.dtype) for g in gs]
    return _comm_call(body, "swap_halves", gs, out_shapes, (nt, nt))


def _scatter_chips(hs):
    nt = len(hs)

    def body(*refs):
        ins, outs = refs[:nt], refs[nt:2 * nt]
        send_sems, recv_sems = refs[2 * nt:]
        _, _, c, _, _, other_chips = _mesh_place()
        for t in range(nt):
            for j, (px, py) in enumerate(other_chips):
                slab = ins[t].at[2 * px + py if ins[t].shape[0] == N_CHIPS else 0]
                for li in range(ins[t].shape[1]):
                    _remote(slab.at[li], outs[t].at[j, li], send_sems.at[3 * t + j], recv_sems.at[3 * t + j],
                            (px, py, c)).start()
        for t in range(nt):
            for j, (px, py) in enumerate(other_chips):
                landed = outs[t].at[j]
                _remote(landed, landed, send_sems.at[3 * t + j], recv_sems.at[3 * t + j], (px, py, c)).wait_recv()
        for t in range(nt):
            for j, (px, py) in enumerate(other_chips):
                sent = outs[t].at[j]
                _remote(sent, sent, send_sems.at[3 * t + j], recv_sems.at[3 * t + j], (px, py, c)).wait_send()

    out_shapes = [jax.ShapeDtypeStruct((3,) + h.shape[1:], h.dtype) for h in hs]
    return _comm_call(body, "scatter_chips", hs, out_shapes, (3 * nt, 3 * nt))


def _swap_cores(halves):
    nt = len(halves)

    def body(*refs):
        ins, outs = refs[:nt], refs[nt:2 * nt]
        send_sems, recv_sems = refs[2 * nt:]
        _, _, _, _, sibling, _ = _mesh_place()
        for t in range(nt):
            for li in range(ins[t].shape[0]):
                _start_rows(ins[t].at[li], outs[t].at[li], send_sems.at[t], recv_sems.at[t], sibling)
        for t in range(nt):
            _remote(outs[t], outs[t], send_sems.at[t], recv_sems.at[t], sibling).wait_recv()
        for t in range(nt):
            _remote(ins[t], ins[t], send_sems.at[t], recv_sems.at[t], sibling).wait_send()

    out_shapes = [jax.ShapeDtypeStruct(h.shape, h.dtype) for h in halves]
    return _comm_call(body, "swap_cores", halves, out_shapes, (nt, nt))


def _pack_rows(pieces, cols, dtype):
    tile = SUBLANES * cols
    rows = []
    for p in pieces:
        flat = p.reshape(-1).astype(dtype)
        rows.append(jnp.pad(flat, (0, (-flat.shape[0]) % tile)).reshape(-1, cols))
    total = sum(r.shape[0] for r in rows)
    if total % (2 * SUBLANES):
        rows.append(jnp.zeros((SUBLANES, cols), dtype))
    return jnp.concatenate(rows, axis=0)


def _unpack_rows(buf, shapes):
    out, row = [], 0
    for s in shapes:
        n = math.prod(s)
        nrows = -(-n // (SUBLANES * buf.shape[1])) * SUBLANES
        out.append(buf[row:row + nrows].reshape(-1)[:n].reshape(s))
        row += nrows
    return out


def _permute_cols(w):
    return jnp.concatenate([w[..., s:s + n] for s, n in _PERM_PIECES], axis=-1)


def _unpermute_cols(w):
    off, pieces = 0, []
    for s, n in _PERM_PIECES:
        pieces.append((s, w[..., off:off + n]))
        off += n
    return jnp.concatenate([p for _, p in sorted(pieces, key=lambda sp: sp[0])], axis=-1)


ROW_SHARDED = ('w_out', 'w_glu')


def _assemble_full(name, g):
    nh, nc, nl, r2, cols = g.shape
    if name in ROW_SHARDED:
        return jnp.transpose(g, (2, 1, 0, 3, 4)).reshape(nl, nc * nh * r2, cols)
    return jnp.transpose(g, (2, 0, 3, 1, 4)).reshape(nl, nh * r2, nc * cols)


def _by_chip(own, related, x, y):
    grid = ((own, related[1]), (related[0], related[2]))
    along_x = [[jnp.where(x == 0, grid[px][dy], grid[1 - px][dy]) for dy in range(2)] for px in range(2)]
    return [jnp.where(y == 0, along_x[px][py], along_x[px][1 - py]) for px in range(2) for py in range(2)]


def _join_halves(mine, theirs, core, axis):
    return jnp.where(core == 0, jnp.concatenate([mine, theirs], axis=axis), jnp.concatenate([theirs, mine], axis=axis))


def _split_by_dest(name, full):
    nl, rows, cols = full.shape
    if name in ROW_SHARDED:
        return jnp.transpose(full.reshape(nl, N_CHIPS, 2, rows // (2 * N_CHIPS), cols), (2, 1, 0, 3, 4))
    return jnp.transpose(full.reshape(nl, 2, rows // 2, N_CHIPS, cols // N_CHIPS), (1, 3, 0, 2, 4))


SSM_RAW = ('ssm_a_re', 'ssm_a_im', 'ssm_log_dt', 'ssm_b_re', 'ssm_b_im', 'ssm_c_re', 'ssm_c_im')


def _all_tables(p):
    raw = tuple(p[k] for k in SSM_RAW)
    (_, bb, cb), vjp = jax.vjp(jax.vmap(_ssm_tables), *raw)
    return (bb.astype(BF16), cb.astype(BF16), jax.vmap(_ssm_powers)(*raw[:3])), vjp


def _layer_fwd(x, p, full, li, mod, tables):
    bb, cb, pw = (t[li] for t in tables)
    h = _norm_fwd(x, _row(p['norm_g'][li]), mod)
    proj = _matmul(h, full['w_in'], 'nn', tm=512, tn=1280, tk=1024, name="proj", n_outer=True)
    o = _attention_fwd(proj, p['attn_sinks'][li])
    y_lin, states = _ssm_fwd(proj, bb, cb, pw)
    y_pool = _pool_fwd(proj, p['w_pool'][li].astype(BF16), _row(p['pool_scale'][li]))
    x_new = _merge_fwd(proj, o, y_lin, y_pool, x, _row(p['ssm_d'][li]), full['w_glu'], _row(p['b_glu'][li]),
                       full['w_br_att'], full['w_br_ssm'], full['w_br_pool'], full['w_out'], mod)
    saved = dict(x=x, h=h, proj=proj, o=o, y_lin=y_lin, y_pool=y_pool, states=states, bb=bb, cb=cb, pw=pw)
    return x_new, saved


def _layer_bwd(dxo, s, p, full, li, mod):
    l = dxo.shape[0]
    proj = s['proj']
    (do, dza, dyl, dus_skip, dzs, dyp, dzp, dgl, ya, ys, ypl, y2, dt, dpa, dps, dpp, mg, dout, st) = _merge_bwd(
        proj, s['o'], s['y_lin'], s['y_pool'], dxo, _row(p['ssm_d'][li]), full['w_glu'], _row(p['b_glu'][li]),
        full['w_br_att'], full['w_br_ssm'], full['w_br_pool'], full['w_out'], mod)
    g = {}
    g['w_br_att'] = _matmul(ya, dpa, 'tn', tm=512, tn=1024, tk=512, name="grad_w_br")
    g['w_br_ssm'] = _matmul(ys, dps, 'tn', tm=512, tn=1024, tk=512, name="grad_w_br")
    g['w_br_pool'] = _matmul(ypl, dpp, 'tn', tm=512, tn=1024, tk=512, name="grad_w_br")
    g['w_out'] = _matmul(mg, dout, 'tn', tm=512, tn=1024, tk=512, name="grad_w_out")
    g['w_glu'] = _matmul(y2, dt, 'tn', tm=512, tn=512, tk=512, name="grad_w_glu")
    g['b_glu'] = st[1, 0:SSM_W]
    g['ssm_d'] = st[1, SSM_W:2 * SSM_W]
    dgate = st[0]

    dq, dkv, dsink = _attention_bwd(proj, p['attn_sinks'][li], do)
    g['attn_sinks'] = dsink[:, 0]
    dus_scan, dbb, dcb, dlam = _ssm_bwd(proj, s['states'], dyl, s['bb'], s['cb'], s['pw'])
    g['ssm_tables'] = (dlam[:, 0:2, :], dbb, dcb)
    dup, dwp, dps_scale = _pool_bwd(proj, p['w_pool'][li].astype(BF16), _row(p['pool_scale'][li]), dyp)
    g['w_pool'] = dwp
    g['pool_scale'] = dps_scale[0]

    dproj = jnp.concatenate([piece.astype(BF16) for piece in
                             (dgl, dq, dza, dus_skip + dus_scan, dzs, dup, dzp, dkv[HALO:])], axis=1)
    g['w_in'] = _unpermute_cols(_matmul(s['h'], dproj, 'tn', tm=1024, tn=1280, tk=512, name="grad_w_in"))
    dh = _matmul(dproj, full['w_in'], 'nt', tm=1024, tn=1024, tk=1280, name="grad_h")
    dx, nst = _norm_bwd(s['x'], dh, dxo, _row(p['norm_g'][li]), mod)
    g['norm_g'] = nst[2]
    dmod = jnp.concatenate([nst[0], nst[1], dgate])
    del l
    return dx, g, dmod


def _gather_big(p, place, depth):
    x, y, core, _ = place
    shards = [p[name].astype(BF16) for name, _ in BIG]
    over_ici, from_sibling = _gather_weights(shards)
    fulls = [{} for _ in range(depth)]
    for (name, _), own, a, b in zip(BIG, shards, over_ici, from_sibling):
        nl, r, cols = own.shape
        own_halves = own.reshape(nl, 2, r // 2, cols)
        for li in range(depth):
            halves = [_join_halves(a[j, li][None], b[j, li][None], core, 0) for j in range(3)]
            chips = _by_chip(own_halves[li], halves, x, y)
            fulls[li][name] = _assemble_full(name, jnp.stack(chips, axis=1)[:, :, None])[0]
    for full in fulls:
        full['w_in'] = _permute_cols(full['w_in'])
    return fulls


def _reduce_grads(grads, small_pack, place, depth):
    _, _, core, chip = place
    names = [name for name, _ in BIG]
    mine = [_split_by_dest(name, jnp.stack([grads[li][name] for li in range(depth)])) for name in names]
    mine.append(small_pack.reshape(2, 1, 1, small_pack.shape[0] // 2, small_pack.shape[1]))
    theirs = _swap_halves(mine)
    wire = [BF16] * len(names) + [F32]
    pair_sums = [_add2(lax.dynamic_index_in_dim(g, core, 0, keepdims=False), a, dt, "sum_core_pair")
                 for g, a, dt in zip(mine, theirs, wire)]
    from_chips = _scatter_chips(pair_sums)
    halves = []
    for t, (ps, others) in enumerate(zip(pair_sums, from_chips)):
        small = t == len(names)
        own = ps[0] if small else lax.dynamic_index_in_dim(ps, chip, 0, keepdims=False)
        cols = own.shape[-1]
        total = _sum_chips(own.reshape(-1, cols), others.reshape(3, -1, cols), small,
                           "sum_chips_small" if small else "sum_chips")
        halves.append(total.reshape(own.shape))
    theirs = _swap_cores(halves)
    out = {name: _join_halves(h, o, core, 1) for name, h, o in zip(names, halves[:-1], theirs[:-1])}
    return out, _join_halves(halves[-1], theirs[-1], core, 1).reshape(small_pack.shape)


def _step(p, m, v, x, c, target):
    depth = p['norm_g'].shape[0]
    d = D_MODEL
    ix, iy, ic = lax.axis_index("x"), lax.axis_index("y"), lax.axis_index("c")
    chip = 2 * ix + iy
    dev = 4 * ix + 2 * iy + ic
    x0 = x[0]

    c_pad = jnp.pad(c, ((0, SUBLANES - 1), (0, 0)))
    c_all = _exchange(c_pad, 'xyc', False, "gather_c")[:, 0, :]
    n_ada = p['w_ada'].shape[-1]
    b_shard = lax.dynamic_slice_in_dim(p['b_ada'], chip * n_ada, n_ada, axis=1)[:, None, :]
    mod_shard = _ada_fwd(c_all, p['w_ada'].astype(BF16), b_shard)
    mod_all = _exchange(mod_shard.reshape(depth * N_DEV, n_ada), 'xy', False, "gather_mod")
    mod_all = jnp.transpose(mod_all.reshape(N_CHIPS, depth, N_DEV, n_ada), (1, 2, 0, 3)).reshape(depth, N_DEV, 3 * d)
    mods = lax.dynamic_index_in_dim(mod_all, dev, axis=1, keepdims=True)

    place = (ix, iy, ic, chip)
    fulls = _gather_big(p, place, depth)

    tables, tables_vjp = _all_tables(p)
    saved = []
    xs = x0
    for li in range(depth):
        xs, s = _layer_fwd(xs, p, fulls[li], li, mods[li], tables)
        saved.append(s)
    dx, hst = _loss_head(xs, _row(p['final_g']), target[0])
    loss = lax.psum(hst[1, 0], ("x", "y", "c"))

    grads = [None] * depth
    dmods = [None] * depth
    for li in reversed(range(depth)):
        dx, grads[li], dmods[li] = _layer_bwd(dx, saved[li], p, fulls[li], li, mods[li])

    dmod_pad = jnp.pad(jnp.stack(dmods), ((0, SUBLANES - depth), (0, 0)))
    dmod_all = _exchange(dmod_pad, 'xyc', False, "gather_dmod")[:, :depth, :]
    dmod_cols = lax.dynamic_slice_in_dim(jnp.transpose(dmod_all, (1, 0, 2)), chip * n_ada, n_ada, axis=2)
    g_w_ada = _ada_bwd(jnp.transpose(c_all), dmod_cols)

    local_small = {k: jnp.stack([grads[li][k] for li in range(depth)])
                   for k in SMALL if k not in ('final_g', 'b_ada') + SSM_RAW}
    table_cotangents = tuple(jnp.stack([grads[li]['ssm_tables'][k] for li in range(depth)]) for k in range(3))
    local_small.update(zip(SSM_RAW, tables_vjp(table_cotangents)))
    local_small['final_g'] = hst[0]
    local_small['b_ada'] = jnp.stack(dmods)
    small_pack = _pack_rows([local_small[k] for k in SMALL], PACK_COLS, F32)
    g_big, small_sum = _reduce_grads(grads, small_pack, place, depth)

    small_shapes = [p[k].shape for k in SMALL]
    grad = dict(zip(SMALL, _unpack_rows(small_sum, small_shapes)))
    grad['w_ada'] = g_w_ada
    grad.update(g_big)

    delta, new_m, new_v = {}, {}, {}
    outs = _adamw(_pack_rows([p[k] for k in SMALL], PACK_COLS, F32), small_sum,
                  _pack_rows([m[k] for k in SMALL], PACK_COLS, F32), _pack_rows([v[k] for k in SMALL], PACK_COLS, F32),
                  name="adamw_small")
    for res, o in zip((delta, new_m, new_v), outs):
        res.update(zip(SMALL, _unpack_rows(o, small_shapes)))
    for name in ['w_ada'] + [n for n, _ in BIG]:
        shape = p[name].shape
        two_d = (-1, shape[-1])
        outs = _adamw(p[name].reshape(two_d), grad[name].reshape(two_d), m[name].reshape(two_d), v[name].reshape(two_d),
                      name="adamw_" + name)
        delta[name], new_m[name], new_v[name] = (o.reshape(shape) for o in outs)

    return (loss, dx[None], *[grad[k] for k in WEIGHTS], *[delta[k] for k in WEIGHTS],
            *[new_m[k] for k in WEIGHTS], *[new_v[k] for k in WEIGHTS])


def kernel(x, c, norm_g, w_ada, b_ada, w_in, attn_sinks, ssm_a_re, ssm_a_im, ssm_log_dt, ssm_b_re, ssm_b_im, ssm_c_re, ssm_c_im, ssm_d, w_glu, b_glu, w_pool, pool_scale, w_br_att, w_br_ssm, w_br_pool, w_out, final_g, loss_target, m_norm_g, m_w_ada, m_b_ada, m_w_in, m_attn_sinks, m_ssm_a_re, m_ssm_a_im, m_ssm_log_dt, m_ssm_b_re, m_ssm_b_im, m_ssm_c_re, m_ssm_c_im, m_ssm_d, m_w_glu, m_b_glu, m_w_pool, m_pool_scale, m_w_br_att, m_w_br_ssm, m_w_br_pool, m_w_out, m_final_g, v_norm_g, v_w_ada, v_b_ada, v_w_in, v_attn_sinks, v_ssm_a_re, v_ssm_a_im, v_ssm_log_dt, v_ssm_b_re, v_ssm_b_im, v_ssm_c_re, v_ssm_c_im, v_ssm_d, v_w_glu, v_b_glu, v_w_pool, v_pool_scale, v_w_br_att, v_w_br_ssm, v_w_br_pool, v_w_out, v_final_g):
    p = dict(zip(WEIGHTS, (norm_g, w_ada, b_ada, w_in, attn_sinks, ssm_a_re, ssm_a_im, ssm_log_dt, ssm_b_re, ssm_b_im,
                           ssm_c_re, ssm_c_im, ssm_d, w_glu, b_glu, w_pool, pool_scale, w_br_att, w_br_ssm, w_br_pool,
                           w_out, final_g)))
    m = dict(zip(WEIGHTS, (m_norm_g, m_w_ada, m_b_ada, m_w_in, m_attn_sinks, m_ssm_a_re, m_ssm_a_im, m_ssm_log_dt,
                           m_ssm_b_re, m_ssm_b_im, m_ssm_c_re, m_ssm_c_im, m_ssm_d, m_w_glu, m_b_glu, m_w_pool,
                           m_pool_scale, m_w_br_att, m_w_br_ssm, m_w_br_pool, m_w_out, m_final_g)))
    v = dict(zip(WEIGHTS, (v_norm_g, v_w_ada, v_b_ada, v_w_in, v_attn_sinks, v_ssm_a_re, v_ssm_a_im, v_ssm_log_dt,
                           v_ssm_b_re, v_ssm_b_im, v_ssm_c_re, v_ssm_c_im, v_ssm_d, v_w_glu, v_b_glu, v_w_pool,
                           v_pool_scale, v_w_br_att, v_w_br_ssm, v_w_br_pool, v_w_out, v_final_g)))
    return _step(p, m, v, x, c, loss_target)
```

```python
import functools
import math

import jax
import jax.numpy as jnp
from jax import lax
from jax.experimental import pallas as pl
from jax.experimental.pallas import tpu as pltpu

F32 = jnp.float32
BF16 = jnp.bfloat16

D_MODEL = 1024
CHUNK = 64
N_HEADS = 8
N_KV_HEADS = 2
HEAD_DIM = 64
Q_PER_KV = N_HEADS // N_KV_HEADS
HALO = 128
ATT_W = 512
KV_W = 128
SSM_W = 512
SSM_GROUP = 16
SSM_GROUPS = 32
SSM_STATE = 64
POOL_W = 512
POOL_GW = 128
IN_W = 6400
EPS = 1e-6
NEG_INF = -1e30
ADAM_LR = 0.001
ADAM_B1 = 0.9
ADAM_B2 = 0.999
ADAM_EPS = 1e-08
ADAM_WD = 0.01
ADAM_STEP = 10

OFF_GL, OFF_Q, OFF_ZA, OFF_US, OFF_ZS, OFF_UP, OFF_ZP, OFF_KV = 0, 3072, 3584, 4096, 4608, 5120, 5632, 6144
_PERM_PIECES = ((3328, 3072), (0, 512), (1792, 512), (768, 512), (2304, 512), (1280, 512), (2816, 512), (512, 256))

LANES = 128
SUBLANES = 8
N_SBLK = SSM_GROUPS * SSM_STATE // LANES
VMEM_LIMIT = 48 * 1024 * 1024

N_CHIPS = 4
N_DEV = 8

WEIGHTS = ['norm_g', 'w_ada', 'b_ada', 'w_in', 'attn_sinks', 'ssm_a_re', 'ssm_a_im', 'ssm_log_dt', 'ssm_b_re',
           'ssm_b_im', 'ssm_c_re', 'ssm_c_im', 'ssm_d', 'w_glu', 'b_glu', 'w_pool', 'pool_scale', 'w_br_att',
           'w_br_ssm', 'w_br_pool', 'w_out', 'final_g']
SMALL = ['norm_g', 'b_ada', 'attn_sinks', 'ssm_a_re', 'ssm_a_im', 'ssm_log_dt', 'ssm_b_re', 'ssm_b_im', 'ssm_c_re',
         'ssm_c_im', 'ssm_d', 'b_glu', 'w_pool', 'pool_scale', 'final_g']
BIG = (('w_in', (1024, 1600)), ('w_br_att', (512, 256)), ('w_br_ssm', (512, 256)), ('w_br_pool', (512, 256)),
       ('w_out', (256, 1024)), ('w_glu', (128, 512)))
PACK_COLS = 1024


def _params(sem=None):
    return pltpu.CompilerParams(dimension_semantics=sem, vmem_limit_bytes=VMEM_LIMIT)


def _row(v):
    return v.reshape(1, -1)


def _full(shape):
    nd = len(shape)
    return pl.BlockSpec(shape, lambda *_: (0,) * nd)


def _sigmoid(v):
    return 1.0 / (1.0 + jnp.exp(-v))


def _silu_and_grad(z):
    s = _sigmoid(z)
    return z * s, s * (1.0 + z * (1.0 - s))


_GELU_K = math.sqrt(2.0 / math.pi)


def _gelu_and_grad(v):
    inner = _GELU_K * (v + 0.044715 * v * v * v)
    th = jnp.tanh(inner)
    val = 0.5 * v * (1.0 + th)
    grad = 0.5 * (1.0 + th) + 0.5 * v * (1.0 - th * th) * _GELU_K * (1.0 + 3 * 0.044715 * v * v)
    return val, grad


_NN = (((1,), (0,)), ((), ()))
_NT = (((1,), (1,)), ((), ()))
_TN = (((0,), (0,)), ((), ()))


def _dot(a, b, dims=_NN):
    return lax.dot_general(a.astype(BF16), b.astype(BF16), dims, preferred_element_type=F32)


def _matmul(a, b, mode, *, tm, tn, tk, name, out_dtype=F32, n_outer=False):
    if mode == 'nn':
        (m, k), (_, n) = a.shape, b.shape
    elif mode == 'nt':
        (m, k), (n, _) = a.shape, b.shape
    else:
        (k, m), (_, n) = a.shape, b.shape
    tm, tn, tk = min(tm, m), min(tn, n), min(tk, k)
    assert m % tm == 0 and n % tn == 0 and k % tk == 0, (name, a.shape, b.shape)
    nk = k // tk
    dims = {'nn': _NN, 'nt': _NT, 'tn': _TN}[mode]

    def body(a_ref, b_ref, o_ref, acc_ref):
        if nk == 1:
            o_ref[...] = _dot(a_ref[...], b_ref[...], dims).astype(out_dtype)
            return
        kk = pl.program_id(2)

        @pl.when(kk == 0)
        def _():
            acc_ref[...] = jnp.zeros_like(acc_ref)

        acc_ref[...] += _dot(a_ref[...], b_ref[...], dims)

        @pl.when(kk == nk - 1)
        def _():
            o_ref[...] = acc_ref[...].astype(out_dtype)

    def spec(shape, index):
        if n_outer:
            return pl.BlockSpec(shape, lambda j, i, kk: index(i, j, kk))
        return pl.BlockSpec(shape, index)

    a_spec = spec((tk, tm), lambda i, j, kk: (kk, i)) if mode == 'tn' else spec((tm, tk), lambda i, j, kk: (i, kk))
    b_spec = spec((tn, tk), lambda i, j, kk: (j, kk)) if mode == 'nt' else spec((tk, tn), lambda i, j, kk: (kk, j))
    grid = (n // tn, m // tm, nk) if n_outer else (m // tm, n // tn, nk)
    return pl.pallas_call(
        body, name=name, grid=grid, in_specs=[a_spec, b_spec], out_specs=spec((tm, tn), lambda i, j, kk: (i, j)),
        out_shape=jax.ShapeDtypeStruct((m, n), out_dtype), scratch_shapes=[pltpu.VMEM((tm, tn), F32)],
        compiler_params=_params(("parallel", "parallel", "arbitrary")))(a, b)


ROWS_NORM = 512


def _norm_fwd(x, g, mod):
    l, d = x.shape
    tr = min(ROWS_NORM, l)

    def body(x_ref, g_ref, mod_ref, h_ref):
        xv = x_ref[...]
        r = lax.rsqrt(jnp.mean(xv * xv, axis=-1, keepdims=True) + EPS)
        shift, scale = mod_ref[:, 0:d], mod_ref[:, d:2 * d]
        h_ref[...] = ((xv * r * g_ref[...]) * (1.0 + scale) + shift).astype(BF16)

    return pl.pallas_call(
        body, name="norm_fwd", grid=(l // tr,),
        in_specs=[pl.BlockSpec((tr, d), lambda i: (i, 0)), _full((1, d)), _full((1, 3 * d))],
        out_specs=pl.BlockSpec((tr, d), lambda i: (i, 0)), out_shape=jax.ShapeDtypeStruct((l, d), BF16),
        compiler_params=_params(("parallel",)))(x, g, mod)


def _norm_bwd(x, dh, dxo, g, mod):
    l, d = x.shape
    tr = min(ROWS_NORM, l)

    def body(x_ref, dh_ref, dxo_ref, g_ref, mod_ref, dx_ref, st_ref):
        @pl.when(pl.program_id(0) == 0)
        def _():
            st_ref[...] = jnp.zeros_like(st_ref)

        xv, dhv = x_ref[...], dh_ref[...]
        r = lax.rsqrt(jnp.mean(xv * xv, axis=-1, keepdims=True) + EPS)
        xn = xv * r
        gv = g_ref[...]
        sc1 = 1.0 + mod_ref[:, d:2 * d]
        dxn = dhv * gv * sc1
        dx_ref[...] = dxo_ref[...] + r * (dxn - xn * jnp.mean(dxn * xn, axis=-1, keepdims=True))
        st_ref[0:1, :] += jnp.sum(dhv, axis=0, keepdims=True)
        st_ref[1:2, :] += jnp.sum(dhv * xn * gv, axis=0, keepdims=True)
        st_ref[2:3, :] += jnp.sum(dhv * xn * sc1, axis=0, keepdims=True)

    blk = pl.BlockSpec((tr, d), lambda i: (i, 0))
    return pl.pallas_call(
        body, name="norm_bwd", grid=(l // tr,),
        in_specs=[blk, blk, blk, _full((1, d)), _full((1, 3 * d))],
        out_specs=[blk, _full((SUBLANES, d))],
        out_shape=[jax.ShapeDtypeStruct((l, d), F32), jax.ShapeDtypeStruct((SUBLANES, d), F32)],
        compiler_params=_params(("arbitrary",)))(x, dh, dxo, g, mod)


def _loss_head(x, g, target):
    l, d = x.shape
    tr = min(ROWS_NORM, l)

    def body(x_ref, g_ref, t_ref, dx_ref, st_ref):
        @pl.when(pl.program_id(0) == 0)
        def _():
            st_ref[...] = jnp.zeros_like(st_ref)

        xv = x_ref[...]
        r = lax.rsqrt(jnp.mean(xv * xv, axis=-1, keepdims=True) + EPS)
        xn = xv * r
        gv = g_ref[...]
        err = xn * gv - t_ref[...]
        part = 0.5 * jnp.sum(jnp.mean(err * err, axis=-1, keepdims=True), axis=0, keepdims=True)
        dy = err * (1.0 / d)
        dxn = dy * gv
        dx_ref[...] = r * (dxn - xn * jnp.mean(dxn * xn, axis=-1, keepdims=True))
        st_ref[0:1, :] += jnp.sum(dy * xn, axis=0, keepdims=True)
        st_ref[1:2, :] += jnp.broadcast_to(part, (1, d))

    blk = pl.BlockSpec((tr, d), lambda i: (i, 0))
    return pl.pallas_call(
        body, name="loss_head", grid=(l // tr,), in_specs=[blk, _full((1, d)), blk],
        out_specs=[blk, _full((SUBLANES, d))],
        out_shape=[jax.ShapeDtypeStruct((l, d), F32), jax.ShapeDtypeStruct((SUBLANES, d), F32)],
        compiler_params=_params(("arbitrary",)))(x, g, target)


ROWS_ATT = 128
ROWS_ATT_BWD = 256
_SLOPES = tuple(2.0 ** (-8.0 * (h + 1) / N_HEADS) for h in range(N_HEADS))


def _att_mask(i, t):
    r = lax.broadcasted_iota(jnp.int32, (t, t + HALO), 0)
    j = lax.broadcasted_iota(jnp.int32, (t, t + HALO), 1)
    dist = jnp.abs(r + HALO - j).astype(F32)
    rc, jc = r // CHUNK, j // CHUNK
    allowed = (jc >= rc) & (jc <= rc + 2) & ((j >= HALO) | (i > 0))
    return dist, allowed


def _att_probs(qh, k, dist, allowed, slope, sink):
    s = _dot(qh, k, _NT) * (1.0 / math.sqrt(HEAD_DIM)) - slope * dist
    s = jnp.where(allowed, s, NEG_INF)
    m = jnp.maximum(jnp.max(s, axis=1, keepdims=True), sink)
    e = jnp.exp(s - m)
    es = jnp.exp(sink - m)
    den = jnp.sum(e, axis=1, keepdims=True) + es
    return e / den, es / den


def _att_specs(t):
    q_spec = pl.BlockSpec((t, ATT_W), lambda i: (i, OFF_Q // ATT_W))
    kv_spec = pl.BlockSpec((t, 2 * KV_W), lambda i: (i, OFF_KV // (2 * KV_W)))
    halo_spec = pl.BlockSpec((HALO, 2 * KV_W), lambda i: (jnp.maximum(i * (t // HALO) - 1, 0), OFF_KV // (2 * KV_W)))
    return q_spec, kv_spec, halo_spec


def _attention_fwd(proj, sinks):
    l = proj.shape[0]
    t = min(ROWS_ATT, l)

    def body(sink_ref, q_ref, kv_ref, halo_ref, o_ref):
        dist, allowed = _att_mask(pl.program_id(0), t)
        kv = jnp.concatenate([halo_ref[...], kv_ref[...]], axis=0)
        for h in range(N_HEADS):
            kh = h // Q_PER_KV
            k = kv[:, kh * HEAD_DIM:(kh + 1) * HEAD_DIM]
            v = kv[:, KV_W + kh * HEAD_DIM:KV_W + (kh + 1) * HEAD_DIM]
            p, _ = _att_probs(q_ref[:, h * HEAD_DIM:(h + 1) * HEAD_DIM], k, dist, allowed, _SLOPES[h], sink_ref[h])
            o_ref[:, h * HEAD_DIM:(h + 1) * HEAD_DIM] = _dot(p, v)

    q_spec, kv_spec, halo_spec = _att_specs(t)
    return pl.pallas_call(
        body, name="attention_fwd", grid=(l // t,),
        in_specs=[pl.BlockSpec(memory_space=pltpu.SMEM), q_spec, kv_spec, halo_spec],
        out_specs=pl.BlockSpec((t, ATT_W), lambda i: (i, 0)), out_shape=jax.ShapeDtypeStruct((l, ATT_W), F32),
        compiler_params=_params(("parallel",)))(sinks, proj, proj, proj)


def _attention_bwd(proj, sinks, do):
    l = proj.shape[0]
    t = min(ROWS_ATT_BWD, l)

    def body(sink_ref, q_ref, kv_ref, halo_ref, do_ref, dq_ref, dkv_ref, dsink_ref):
        i = pl.program_id(0)

        @pl.when(i == 0)
        def _():
            dkv_ref[...] = jnp.zeros_like(dkv_ref)
            dsink_ref[...] = jnp.zeros_like(dsink_ref)

        dist, allowed = _att_mask(i, t)
        kv = jnp.concatenate([halo_ref[...], kv_ref[...]], axis=0)
        rows = pl.ds(pl.multiple_of(i * t, t), t + HALO)
        for kh in range(N_KV_HEADS):
            k = kv[:, kh * HEAD_DIM:(kh + 1) * HEAD_DIM]
            v = kv[:, KV_W + kh * HEAD_DIM:KV_W + (kh + 1) * HEAD_DIM]
            dk = jnp.zeros((t + HALO, HEAD_DIM), F32)
            dv = jnp.zeros((t + HALO, HEAD_DIM), F32)
            for h in range(kh * Q_PER_KV, (kh + 1) * Q_PER_KV):
                qh = q_ref[:, h * HEAD_DIM:(h + 1) * HEAD_DIM]
                doh = do_ref[:, h * HEAD_DIM:(h + 1) * HEAD_DIM]
                p, ps = _att_probs(qh, k, dist, allowed, _SLOPES[h], sink_ref[h])
                dp = _dot(doh, v, _NT)
                delta = jnp.sum(p * dp, axis=1, keepdims=True)
                ds = p * (dp - delta) * (1.0 / math.sqrt(HEAD_DIM))
                dsink_ref[h:h + 1, :] += jnp.broadcast_to(-jnp.sum(ps * delta, axis=0, keepdims=True), (1, LANES))
                dq_ref[:, h * HEAD_DIM:(h + 1) * HEAD_DIM] = _dot(ds, k)
                dk = dk + _dot(ds, qh, _TN)
                dv = dv + _dot(p, doh, _TN)
            dkv_ref[rows, kh * HEAD_DIM:(kh + 1) * HEAD_DIM] += dk
            dkv_ref[rows, KV_W + kh * HEAD_DIM:KV_W + (kh + 1) * HEAD_DIM] += dv

    q_spec, kv_spec, halo_spec = _att_specs(t)
    blk = pl.BlockSpec((t, ATT_W), lambda i: (i, 0))
    return pl.pallas_call(
        body, name="attention_bwd", grid=(l // t,),
        in_specs=[pl.BlockSpec(memory_space=pltpu.SMEM), q_spec, kv_spec, halo_spec, blk],
        out_specs=[blk, _full((HALO + l, 2 * KV_W)), _full((N_HEADS, LANES))],
        out_shape=[jax.ShapeDtypeStruct((l, ATT_W), F32), jax.ShapeDtypeStruct((HALO + l, 2 * KV_W), F32),
                   jax.ShapeDtypeStruct((N_HEADS, LANES), F32)],
        compiler_params=_params(("arbitrary",)))(sinks, proj, proj, proj, do)


ROWS_SSM = 512
SCAN_UNROLL = 4


def _ssm_discretize(a_re, a_im, log_dt, b_re, b_im):
    lam = lax.complex(a_re, a_im)
    dt = jnp.exp(log_dt)[:, None]
    lam_bar = jnp.exp(lam * dt)
    b_bar = ((lam_bar - 1.0) / lam)[..., None] * lax.complex(b_re, b_im)
    return lam, dt, lam_bar, b_bar


def _ssm_block_diag(m):
    e = m.reshape(N_SBLK, 2, SSM_GROUP, SSM_STATE)
    e = e[:, :, :, None, :] * jnp.eye(2, dtype=m.dtype)[None, :, None, :, None]
    e = e.reshape(N_SBLK, 2 * SSM_GROUP, LANES)
    oh = jax.nn.one_hot(jnp.arange(N_SBLK) % 4, 4, dtype=m.dtype)
    return (oh[:, :, None, None] * e[:, None]).reshape(N_SBLK, LANES, LANES)


def _ssm_tables(a_re, a_im, log_dt, b_re, b_im, c_re, c_im):
    _, _, lam_bar, b_bar = _ssm_discretize(a_re, a_im, log_dt, b_re, b_im)
    lam_blk = jnp.stack([jnp.real(lam_bar).reshape(N_SBLK, LANES), jnp.imag(lam_bar).reshape(N_SBLK, LANES)], axis=1)
    bt = jnp.transpose(b_bar, (0, 2, 1))
    bb = jnp.concatenate([_ssm_block_diag(jnp.real(bt)), _ssm_block_diag(jnp.imag(bt))], axis=2)
    cb = jnp.concatenate([jnp.transpose(_ssm_block_diag(c_re), (0, 2, 1)),
                          jnp.transpose(_ssm_block_diag(-c_im), (0, 2, 1))], axis=1)
    return lam_blk, bb, cb


def _ssm_powers(a_re, a_im, log_dt):
    lam = lax.complex(a_re, a_im)
    dt = jnp.exp(log_dt)[:, None]
    k = jnp.arange(1, SUBLANES + 1, dtype=F32)
    pw = jnp.exp((lam * dt)[None] * k[:, None, None]).reshape(SUBLANES, N_SBLK, LANES)
    pw = jnp.transpose(pw, (1, 0, 2))
    rev = pw[:, ::-1]
    return jnp.concatenate([jnp.real(pw), jnp.imag(pw), jnp.real(rev), jnp.imag(rev)], axis=1)


def _scan_consts(pw_ref, reverse):
    row = lax.broadcasted_iota(jnp.int32, (SUBLANES, LANES), 0)
    sign = -1.0 if reverse else 1.0

    def power(k):
        return (jnp.broadcast_to(pw_ref[0, k - 1:k, :], (SUBLANES, LANES)),
                sign * jnp.broadcast_to(pw_ref[0, SUBLANES + k - 1:SUBLANES + k, :], (SUBLANES, LANES)))

    steps = []
    for d in (1, 2, 4):
        pr, pi = power(d)
        keep = (row < SUBLANES - d) if reverse else (row >= d)
        steps.append((d, jnp.where(keep, pr, 0.0), jnp.where(keep, pi, 0.0)))
    base = 2 * SUBLANES if reverse else 0
    return (steps, pw_ref[0, base:base + SUBLANES, :], sign * pw_ref[0, base + SUBLANES:base + 2 * SUBLANES, :],
            power(SUBLANES))


def _scan_tile(xr, xi, steps, reverse):
    for d, ar, ai in steps:
        shift = SUBLANES - d if reverse else d
        rr, ri = pltpu.roll(xr, shift, 0), pltpu.roll(xi, shift, 0)
        xr, xi = xr + (ar * rr - ai * ri), xi + (ar * ri + ai * rr)
    return xr, xi


def _bcast_row(v, r):
    return jnp.broadcast_to(v[r:r + 1, :], (SUBLANES, LANES))


def _scan_forward(s_scr, pw_ref, n_tiles):
    steps, pr, pi, (l8r, l8i) = _scan_consts(pw_ref, False)

    def tiles(g, carry):
        cr, ci = carry
        rows = [pl.ds(pl.multiple_of((g * SCAN_UNROLL + u) * SUBLANES, SUBLANES), SUBLANES) for u in range(SCAN_UNROLL)]
        loaded = [(s_scr[r, 0:LANES], s_scr[r, LANES:2 * LANES]) for r in rows]
        local = [_scan_tile(xr, xi, steps, False) for xr, xi in loaded]
        for r, (xr, xi) in zip(rows, local):
            s_scr[r, 0:LANES] = xr + (pr * cr - pi * ci)
            s_scr[r, LANES:2 * LANES] = xi + (pr * ci + pi * cr)
            cr, ci = ((l8r * cr - l8i * ci) + _bcast_row(xr, SUBLANES - 1),
                      (l8r * ci + l8i * cr) + _bcast_row(xi, SUBLANES - 1))
        return cr, ci

    zero = jnp.zeros((SUBLANES, LANES), F32)
    lax.fori_loop(0, n_tiles // SCAN_UNROLL, tiles, (zero, zero))


def _row_chunks(l):
    rc = min(ROWS_SSM, l)
    return rc, l // rc


def _carry_sends(body, n_in, n_out, n_scratch, n_steps, sends):
    if sends is None:
        return body, [], [], [], [], []
    k = len(sends.srcs)

    def carrying(*refs):
        ins, send_ins = refs[:n_in], refs[n_in:n_in + k]
        outs, send_outs = refs[n_in + k:n_in + k + n_out], refs[n_in + k + n_out:n_in + 2 * k + n_out]
        rest = refs[n_in + 2 * k + n_out:]
        scratch, sems = rest[:n_scratch], rest[n_scratch:]
        step = pl.program_id(0)

        @pl.when(step == 0)
        def _():
            sends.start(send_ins, send_outs, *sems)

        body(*ins, *outs, *scratch)

        @pl.when(step == n_steps - 1)
        def _():
            sends.finish(send_ins, send_outs, *sems)

    in_specs, out_specs, scratch = sends.specs()
    return carrying, in_specs, out_specs, sends.out_shapes, scratch, sends.srcs


def _ssm_fwd(proj, bb, cb, pw, sends=None):
    l = proj.shape[0]
    rc, n_chunks = _row_chunks(l)

    def body(u_ref, bb_ref, cb_ref, pw_ref, y_ref, s_scr):
        j = pl.program_id(0)

        def fill(ci, _):
            rows = pl.ds(pl.multiple_of(ci * rc, rc), rc)
            s_scr[rows, :] = _dot(u_ref[rows, :], bb_ref[0])
            return 0

        lax.fori_loop(0, n_chunks, fill, 0)
        _scan_forward(s_scr, pw_ref, l // SUBLANES)

        @pl.when(j % 4 == 0)
        def _():
            y_ref[...] = jnp.zeros_like(y_ref)

        def emit(ci, _):
            rows = pl.ds(pl.multiple_of(ci * rc, rc), rc)
            y_ref[rows, :] += _dot(s_scr[rows, :], cb_ref[0])
            return 0

        lax.fori_loop(0, n_chunks, emit, 0)

    body, x_in, x_out, x_shapes, x_scratch, x_args = _carry_sends(body, 4, 2, 0, N_SBLK, sends)
    outs = pl.pallas_call(
        body, name="ssm_fwd" if sends is None else "ssm_fwd_sending", grid=(N_SBLK,),
        in_specs=[pl.BlockSpec((l, LANES), lambda j: (0, OFF_US // LANES + j // 4)),
                  pl.BlockSpec((1, LANES, 2 * LANES), lambda j: (j, 0, 0)),
                  pl.BlockSpec((1, 2 * LANES, LANES), lambda j: (j, 0, 0)),
                  pl.BlockSpec((1, 4 * SUBLANES, LANES), lambda j: (j, 0, 0))] + x_in,
        out_specs=[pl.BlockSpec((l, LANES), lambda j: (0, j // 4)), pl.BlockSpec((l, 2 * LANES), lambda j: (0, j))] + x_out,
        out_shape=[jax.ShapeDtypeStruct((l, SSM_W), F32), jax.ShapeDtypeStruct((l, N_SBLK * 2 * LANES), F32)] + x_shapes,
        scratch_shapes=x_scratch, compiler_params=_params(("arbitrary",)))(proj, bb, cb, pw, *x_args)
    return outs[0], outs[1], list(outs[2:])


def _ssm_bwd(proj, states, dy, bb, cb, pw, sends=None):
    l = proj.shape[0]
    rc, n_chunks = _row_chunks(l)
    n_tiles = l // SUBLANES

    def body(u_ref, s_scr, dy_ref, bb_ref, cb_ref, pw_ref, du_ref, dbb_ref, dcb_ref, dlam_ref, a_scr):
        j = pl.program_id(0)
        dcb_ref[...] = jnp.zeros_like(dcb_ref)

        def through_c(ci, _):
            rows = pl.ds(pl.multiple_of(ci * rc, rc), rc)
            dyv = dy_ref[rows, :]
            dcb_ref[0] += _dot(s_scr[rows, :], dyv, _TN)
            a_scr[rows, :] = _dot(dyv, cb_ref[0], _NT)
            return 0

        lax.fori_loop(0, n_chunks, through_c, 0)

        steps, pr, pi, (l8r, l8i) = _scan_consts(pw_ref, True)
        row = lax.broadcasted_iota(jnp.int32, (SUBLANES, LANES), 0)

        last = row == SUBLANES - 1

        def tiles(g, carry):
            cr, ci, accr, acci = carry
            rows = [pl.ds(pl.multiple_of((n_tiles - 1 - (g * SCAN_UNROLL + u)) * SUBLANES, SUBLANES), SUBLANES)
                    for u in range(SCAN_UNROLL)]
            loaded = [(a_scr[r, 0:LANES], a_scr[r, LANES:2 * LANES]) for r in rows]
            states = [(s_scr[r, 0:LANES], s_scr[r, LANES:2 * LANES]) for r in rows]
            local = [_scan_tile(xr, xi, steps, True) for xr, xi in loaded]
            for r, (xr, xi), (sr, si) in zip(rows, local, states):
                gr, gi = xr + (pr * cr - pi * ci), xi + (pr * ci + pi * cr)
                a_scr[r, 0:LANES] = gr
                a_scr[r, LANES:2 * LANES] = gi
                ur = jnp.where(last, cr, pltpu.roll(gr, SUBLANES - 1, 0))
                ui = jnp.where(last, ci, pltpu.roll(gi, SUBLANES - 1, 0))
                accr, acci = accr + (ur * sr + ui * si), acci + (ui * sr - ur * si)
                cr, ci = (l8r * cr - l8i * ci) + _bcast_row(xr, 0), (l8r * ci + l8i * cr) + _bcast_row(xi, 0)
            return cr, ci, accr, acci

        zero = jnp.zeros((SUBLANES, LANES), F32)
        _, _, accr, acci = lax.fori_loop(0, n_tiles // SCAN_UNROLL, tiles, (zero, zero, zero, zero))
        dlr = jnp.broadcast_to(jnp.sum(accr, axis=0, keepdims=True), (SUBLANES, LANES))
        dli = jnp.broadcast_to(jnp.sum(acci, axis=0, keepdims=True), (SUBLANES, LANES))
        dlam_ref[0] = jnp.where(row == 0, dlr, jnp.where(row == 1, dli, 0.0))

        dbb_ref[...] = jnp.zeros_like(dbb_ref)

        @pl.when(j % 4 == 0)
        def _():
            du_ref[...] = jnp.zeros_like(du_ref)

        def through_b(ci, _):
            rows = pl.ds(pl.multiple_of(ci * rc, rc), rc)
            av = a_scr[rows, :]
            dbb_ref[0] += _dot(u_ref[rows, :], av, _TN)
            du_ref[rows, :] += _dot(av, bb_ref[0], _NT)
            return 0

        lax.fori_loop(0, n_chunks, through_b, 0)

    body, x_in, x_out, x_shapes, x_scratch, x_args = _carry_sends(body, 6, 4, 1, N_SBLK, sends)
    outs = pl.pallas_call(
        body, name="ssm_bwd" if sends is None else "ssm_bwd_sending", grid=(N_SBLK,),
        in_specs=[pl.BlockSpec((l, LANES), lambda j: (0, OFF_US // LANES + j // 4)),
                  pl.BlockSpec((l, 2 * LANES), lambda j: (0, j)),
                  pl.BlockSpec((l, LANES), lambda j: (0, j // 4)),
                  pl.BlockSpec((1, LANES, 2 * LANES), lambda j: (j, 0, 0)),
                  pl.BlockSpec((1, 2 * LANES, LANES), lambda j: (j, 0, 0)),
                  pl.BlockSpec((1, 4 * SUBLANES, LANES), lambda j: (j, 0, 0))] + x_in,
        out_specs=[pl.BlockSpec((l, LANES), lambda j: (0, j // 4)),
                   pl.BlockSpec((1, LANES, 2 * LANES), lambda j: (j, 0, 0)),
                   pl.BlockSpec((1, 2 * LANES, LANES), lambda j: (j, 0, 0)),
                   pl.BlockSpec((1, SUBLANES, LANES), lambda j: (j, 0, 0))] + x_out,
        out_shape=[jax.ShapeDtypeStruct((l, SSM_W), F32), jax.ShapeDtypeStruct((N_SBLK, LANES, 2 * LANES), F32),
                   jax.ShapeDtypeStruct((N_SBLK, 2 * LANES, LANES), F32),
                   jax.ShapeDtypeStruct((N_SBLK, SUBLANES, LANES), F32)] + x_shapes,
        scratch_shapes=[pltpu.VMEM((l, 2 * LANES), F32)] + x_scratch,
        compiler_params=_params(("arbitrary",)))(proj, states, dy, bb, cb, pw, *x_args)
    return outs[0], outs[1], outs[2], outs[3], list(outs[4:])


def _pool_windows(v, t, l, ahead):
    def shifted(a, d):
        if ahead:
            return jnp.where(t < l - d, pltpu.roll(a, l - d, 0), 0.0)
        return jnp.where(t >= d, pltpu.roll(a, d, 0), 0.0)

    w2 = v + shifted(v, 1)
    w4 = w2 + shifted(w2, 2)
    w8 = w4 + shifted(w4, 4)
    w16 = w8 + shifted(w8, 8)
    return w2, w4, w8, w16


def _pool_select(g, ws):
    return jnp.where(g == 0, ws[0], jnp.where(g == 1, ws[1], jnp.where(g == 2, ws[2], ws[3])))


def _pool_count(g, t):
    return jnp.minimum(t + 1, jnp.left_shift(2, g)).astype(F32)


def _pool_specs(l):
    return [pl.BlockSpec((l, POOL_GW), lambda g: (0, OFF_UP // POOL_GW + g)),
            pl.BlockSpec((1, POOL_GW, POOL_GW), lambda g: (g, 0, 0)),
            pl.BlockSpec((1, POOL_GW), lambda g: (0, g))]


def _pool_fwd(proj, w_pool, scale):
    l = proj.shape[0]

    def body(u_ref, w_ref, sc_ref, y_ref):
        g = pl.program_id(0)
        t = lax.broadcasted_iota(jnp.int32, (l, 1), 0)
        u = u_ref[...]
        pooled = _pool_select(g, _pool_windows(u, t, l, False)) / _pool_count(g, t) - u
        y_ref[...] = _dot(pooled, w_ref[0]) * sc_ref[...]

    return pl.pallas_call(
        body, name="pool_fwd", grid=(4,), in_specs=_pool_specs(l),
        out_specs=pl.BlockSpec((l, POOL_GW), lambda g: (0, g)), out_shape=jax.ShapeDtypeStruct((l, POOL_W), F32),
        compiler_params=_params(("parallel",)))(proj, w_pool, scale)


def _pool_bwd(proj, w_pool, scale, dy):
    l = proj.shape[0]

    def body(u_ref, w_ref, sc_ref, dy_ref, du_ref, dw_ref, dsc_ref):
        g = pl.program_id(0)
        t = lax.broadcasted_iota(jnp.int32, (l, 1), 0)
        u = u_ref[...]
        cnt = _pool_count(g, t)
        pooled = _pool_select(g, _pool_windows(u, t, l, False)) / cnt - u
        dyv = dy_ref[...]
        dsc_ref[...] = jnp.sum(dyv * _dot(pooled, w_ref[0]), axis=0, keepdims=True)
        dyl = dyv * sc_ref[...]
        dw_ref[0] = _dot(pooled, dyl, _TN)
        dpooled = _dot(dyl, w_ref[0], _NT)
        du_ref[...] = _pool_select(g, _pool_windows(dpooled / cnt, t, l, True)) - dpooled

    return pl.pallas_call(
        body, name="pool_bwd", grid=(4,), in_specs=_pool_specs(l) + [pl.BlockSpec((l, POOL_GW), lambda g: (0, g))],
        out_specs=[pl.BlockSpec((l, POOL_GW), lambda g: (0, g)), pl.BlockSpec((1, POOL_GW, POOL_GW), lambda g: (g, 0, 0)),
                   pl.BlockSpec((1, POOL_GW), lambda g: (0, g))],
        out_shape=[jax.ShapeDtypeStruct((l, POOL_W), F32), jax.ShapeDtypeStruct((4, POOL_GW, POOL_GW), F32),
                   jax.ShapeDtypeStruct((1, POOL_W), F32)],
        compiler_params=_params(("parallel",)))(proj, w_pool, scale, dy)


ROWS_MERGE = 128


def _merge_inputs(tr):
    def col(off, w):
        return pl.BlockSpec((tr, w), lambda i: (i, off // w))

    def act(w):
        return pl.BlockSpec((tr, w), lambda i: (i, 0))

    d = D_MODEL
    return ([col(OFF_GL, 3 * d), col(OFF_ZA, ATT_W), col(OFF_US, SSM_W), col(OFF_ZS, SSM_W), col(OFF_ZP, POOL_W),
             act(ATT_W), act(SSM_W), act(POOL_W)]
            + [_full((1, SSM_W)), _full((SSM_W, SSM_W)), _full((1, SSM_W)), _full((ATT_W, d)), _full((SSM_W, d)),
               _full((POOL_W, d)), _full((d, d)), _full((1, 3 * d))])


def _merge_forward_math(gl_ref, za_ref, us_ref, zs_ref, zp_ref, o_ref, yl_ref, yp_ref, d_ref, wg_ref, bg_ref,
                        wba_ref, wbs_ref, wbp_ref):
    d = D_MODEL
    r = {}
    r['sa'], r['dsa'] = _silu_and_grad(za_ref[...])
    r['ss'], r['dss'] = _silu_and_grad(zs_ref[...])
    r['sp'], r['dsp'] = _silu_and_grad(zp_ref[...])
    r['y_att'] = o_ref[...] * r['sa']
    y1 = yl_ref[...] + d_ref[...] * us_ref[...]
    r['y2'], r['dgelu'] = _gelu_and_grad(y1)
    r['sg'] = _sigmoid(_dot(r['y2'], wg_ref[...]) + bg_ref[...])
    r['y3'] = r['y2'] * r['sg']
    r['y_ssm'] = r['y3'] * r['ss']
    r['y_pool'] = yp_ref[...] * r['sp']
    r['g'] = [_sigmoid(gl_ref[:, b * d:(b + 1) * d]) for b in range(3)]
    r['p'] = [_dot(r['y_att'], wba_ref[...]), _dot(r['y_ssm'], wbs_ref[...]), _dot(r['y_pool'], wbp_ref[...])]
    r['merged'] = r['g'][0] * r['p'][0] + r['g'][1] * r['p'][1] + r['g'][2] * r['p'][2]
    return r


def _merge_fwd(proj, o, y_lin, y_pool, x, ssm_d, w_glu, b_glu, wba, wbs, wbp, w_out, mod):
    l, d = x.shape
    tr = min(ROWS_MERGE, l)

    def body(gl_ref, za_ref, us_ref, zs_ref, zp_ref, o_ref, yl_ref, yp_ref, d_ref, wg_ref, bg_ref, wba_ref, wbs_ref,
             wbp_ref, wo_ref, mod_ref, x_ref, xn_ref):
        r = _merge_forward_math(gl_ref, za_ref, us_ref, zs_ref, zp_ref, o_ref, yl_ref, yp_ref, d_ref, wg_ref, bg_ref,
                                wba_ref, wbs_ref, wbp_ref)
        xn_ref[...] = x_ref[...] + mod_ref[:, 2 * d:3 * d] * _dot(r['merged'], wo_ref[...])

    blk = pl.BlockSpec((tr, d), lambda i: (i, 0))
    return pl.pallas_call(
        body, name="merge_fwd", grid=(l // tr,), in_specs=_merge_inputs(tr) + [blk], out_specs=blk,
        out_shape=jax.ShapeDtypeStruct((l, d), F32), compiler_params=_params(("parallel",)))(
            proj, proj, proj, proj, proj, o, y_lin, y_pool, ssm_d, w_glu, b_glu, wba, wbs, wbp, w_out, mod, x)


def _merge_bwd(proj, o, y_lin, y_pool, dxo, ssm_d, w_glu, b_glu, wba, wbs, wbp, w_out, mod):
    l, d = dxo.shape
    tr = min(ROWS_MERGE, l)

    def body(gl_ref, za_ref, us_ref, zs_ref, zp_ref, o_ref, yl_ref, yp_ref, d_ref, wg_ref, bg_ref, wba_ref, wbs_ref,
             wbp_ref, wo_ref, mod_ref, dxo_ref,
             do_ref, dza_ref, dyl_ref, dus_ref, dzs_ref, dyp_ref, dzp_ref, dgl_ref,
             ya_ref, ys_ref, ypl_ref, y2_ref, dt_ref, dpa_ref, dps_ref, dpp_ref, mg_ref, dout_ref, st_ref):
        @pl.when(pl.program_id(0) == 0)
        def _():
            st_ref[...] = jnp.zeros_like(st_ref)

        r = _merge_forward_math(gl_ref, za_ref, us_ref, zs_ref, zp_ref, o_ref, yl_ref, yp_ref, d_ref, wg_ref, bg_ref,
                                wba_ref, wbs_ref, wbp_ref)
        dxov = dxo_ref[...]
        out = _dot(r['merged'], wo_ref[...])
        st_ref[0:1, :] += jnp.sum(dxov * out, axis=0, keepdims=True)
        dout = dxov * mod_ref[:, 2 * d:3 * d]
        dmerged = _dot(dout, wo_ref[...], _NT)
        dys = []
        for b, (w_ref, dp_ref) in enumerate(((wba_ref, dpa_ref), (wbs_ref, dps_ref), (wbp_ref, dpp_ref))):
            gb = r['g'][b]
            dgl_ref[:, b * d:(b + 1) * d] = dmerged * r['p'][b] * gb * (1.0 - gb)
            dp = dmerged * gb
            dp_ref[...] = dp.astype(BF16)
            dys.append(_dot(dp, w_ref[...], _NT))
        mg_ref[...] = r['merged'].astype(BF16)
        dout_ref[...] = dout.astype(BF16)
        ya_ref[...] = r['y_att'].astype(BF16)
        ys_ref[...] = r['y_ssm'].astype(BF16)
        ypl_ref[...] = r['y_pool'].astype(BF16)
        y2_ref[...] = r['y2'].astype(BF16)
        do_ref[...] = dys[0] * r['sa']
        dza_ref[...] = dys[0] * o_ref[...] * r['dsa']
        dy3 = dys[1] * r['ss']
        dzs_ref[...] = dys[1] * r['y3'] * r['dss']
        dt = dy3 * r['y2'] * r['sg'] * (1.0 - r['sg'])
        dt_ref[...] = dt.astype(BF16)
        dy1 = (dy3 * r['sg'] + _dot(dt, wg_ref[...], _NT)) * r['dgelu']
        dyl_ref[...] = dy1
        dus_ref[...] = dy1 * d_ref[...]
        st_ref[1:2, 0:SSM_W] += jnp.sum(dt, axis=0, keepdims=True)
        st_ref[1:2, SSM_W:2 * SSM_W] += jnp.sum(dy1 * us_ref[...], axis=0, keepdims=True)
        dyp_ref[...] = dys[2] * r['sp']
        dzp_ref[...] = dys[2] * yp_ref[...] * r['dsp']

    blk = pl.BlockSpec((tr, d), lambda i: (i, 0))
    half = pl.BlockSpec((tr, ATT_W), lambda i: (i, 0))
    wide = pl.BlockSpec((tr, 3 * d), lambda i: (i, 0))
    sds = jax.ShapeDtypeStruct
    return pl.pallas_call(
        body, name="merge_bwd", grid=(l // tr,), in_specs=_merge_inputs(tr) + [blk],
        out_specs=[half] * 7 + [wide] + [half] * 5 + [blk] * 5 + [_full((SUBLANES, d))],
        out_shape=[sds((l, ATT_W), F32)] * 7 + [sds((l, 3 * d), F32)] + [sds((l, ATT_W), BF16)] * 5
        + [sds((l, d), BF16)] * 5 + [sds((SUBLANES, d), F32)],
        compiler_params=_params(("arbitrary",)))(
            proj, proj, proj, proj, proj, o, y_lin, y_pool, ssm_d, w_glu, b_glu, wba, wbs, wbp, w_out, mod, dxo)


def _ada_fwd(c_all, w_ada, b_shard):
    depth, d, n = w_ada.shape

    def body(c_ref, w_ref, b_ref, o_ref):
        act, _ = _silu_and_grad(c_ref[...])
        o_ref[0] = _dot(act, w_ref[0]) + b_ref[0]

    return pl.pallas_call(
        body, name="ada_fwd", grid=(depth,),
        in_specs=[_full((N_DEV, d)), pl.BlockSpec((1, d, n), lambda i: (i, 0, 0)), pl.BlockSpec((1, 1, n), lambda i: (i, 0, 0))],
        out_specs=pl.BlockSpec((1, N_DEV, n), lambda i: (i, 0, 0)), out_shape=jax.ShapeDtypeStruct((depth, N_DEV, n), F32),
        compiler_params=_params(("parallel",)))(c_all, w_ada, b_shard)


def _ada_bwd(c_all_t, dmod):
    d = c_all_t.shape[0]
    depth, _, n = dmod.shape

    def body(c_ref, dm_ref, o_ref):
        act, _ = _silu_and_grad(c_ref[...])
        acc = act[:, 0:1] * dm_ref[0, 0:1, :]
        for b in range(1, N_DEV):
            acc = acc + act[:, b:b + 1] * dm_ref[0, b:b + 1, :]
        o_ref[0] = acc

    return pl.pallas_call(
        body, name="ada_bwd", grid=(depth,),
        in_specs=[_full((d, N_DEV)), pl.BlockSpec((1, N_DEV, n), lambda i: (i, 0, 0))],
        out_specs=pl.BlockSpec((1, d, n), lambda i: (i, 0, 0)), out_shape=jax.ShapeDtypeStruct((depth, d, n), F32),
        compiler_params=_params(("parallel",)))(c_all_t, dmod)


ROWS_FLAT = 256


_RELATION_XOR = (0, 2, 1, 3)


def _sum_chips(own, others, chip_order, name):
    r, c = own.shape
    tr = math.gcd(ROWS_FLAT, r)

    def body(own_ref, oth_ref, o_ref):
        terms = [own_ref[...].astype(F32)] + [oth_ref[k].astype(F32) for k in range(3)]
        if chip_order:
            chip = 2 * lax.axis_index("x") + lax.axis_index("y")
            by_chip = []
            for q in range(N_CHIPS):
                rel = jnp.bitwise_xor(chip, q)
                pick = terms[3]
                for k in (2, 1, 0):
                    pick = jnp.where(rel == _RELATION_XOR[k], terms[k], pick)
                by_chip.append(pick)
            terms = by_chip
        o_ref[...] = ((terms[0] + terms[1]) + terms[2]) + terms[3]

    return pl.pallas_call(
        body, name=name, grid=(r // tr,),
        in_specs=[pl.BlockSpec((tr, c), lambda i: (i, 0)), pl.BlockSpec((3, tr, c), lambda i: (0, i, 0))],
        out_specs=pl.BlockSpec((tr, c), lambda i: (i, 0)), out_shape=jax.ShapeDtypeStruct((r, c), F32),
        compiler_params=_params(("parallel",)))(own, others)


def _add2(a, b, out_dtype, name):
    shape = a.shape
    a, b = a.reshape(-1, shape[-1]), b.reshape(-1, shape[-1])
    r, c = a.shape
    tr = math.gcd(ROWS_FLAT, r)

    def body(a_ref, b_ref, o_ref):
        o_ref[...] = (a_ref[...] + b_ref[...]).astype(out_dtype)

    blk = pl.BlockSpec((tr, c), lambda i: (i, 0))
    return pl.pallas_call(
        body, name=name, grid=(r // tr,), in_specs=[blk, blk], out_specs=blk,
        out_shape=jax.ShapeDtypeStruct((r, c), out_dtype), compiler_params=_params(("parallel",)))(a, b).reshape(shape)


def _adamw(w, g, m, v, name):
    r, c = w.shape
    tr = math.gcd(ROWS_FLAT, r)

    def body(w_ref, g_ref, m_ref, v_ref, d_ref, nm_ref, nv_ref):
        gv = g_ref[...]
        mv = ADAM_B1 * m_ref[...] + (1.0 - ADAM_B1) * gv
        vv = ADAM_B2 * v_ref[...] + (1.0 - ADAM_B2) * (gv * gv)
        m_hat = mv / (1.0 - ADAM_B1 ** ADAM_STEP)
        v_hat = vv / (1.0 - ADAM_B2 ** ADAM_STEP)
        d_ref[...] = -ADAM_LR * (m_hat / (jnp.sqrt(v_hat) + ADAM_EPS) + ADAM_WD * w_ref[...])
        nm_ref[...] = mv
        nv_ref[...] = vv

    blk = pl.BlockSpec((tr, c), lambda i: (i, 0))
    return pl.pallas_call(
        body, name=name, grid=(r // tr,), in_specs=[blk] * 4, out_specs=[blk] * 3,
        out_shape=[jax.ShapeDtypeStruct((r, c), F32)] * 3, compiler_params=_params(("parallel",)))(w, g, m, v)


_GROUP_MASKS = {
    'xy': ((1, 0, 0), (0, 1, 0), (1, 1, 0)),
    'c': ((0, 0, 1),),
    'xyc': ((0, 0, 1), (0, 1, 0), (0, 1, 1), (1, 0, 0), (1, 0, 1), (1, 1, 0), (1, 1, 1)),
}


def _group_rank(group, pos):
    x, y, c = pos
    return {'xy': 2 * x + y, 'c': c, 'xyc': 4 * x + 2 * y + c}[group]


def _exchange(src, group, scatter, name):
    masks = _GROUP_MASKS[group]
    n = len(masks) + 1
    shape = src.shape[1:] if scatter else src.shape
    assert (not scatter) or src.shape[0] == n

    def body(src_ref, out_ref, send_sems, recv_sems, local_sem):
        me = (lax.axis_index("x"), lax.axis_index("y"), lax.axis_index("c"))
        my_rank = _group_rank(group, me)

        def mine_for(rank):
            return src_ref.at[rank] if scatter else src_ref

        local = pltpu.make_async_copy(mine_for(my_rank), out_ref.at[my_rank], local_sem)
        local.start()
        sends = []
        for k, mask in enumerate(masks):
            peer = tuple(1 - p if f else p for p, f in zip(me, mask))
            peer_rank = _group_rank(group, peer)
            send = pltpu.make_async_remote_copy(
                src_ref=mine_for(peer_rank), dst_ref=out_ref.at[my_rank], send_sem=send_sems.at[k],
                recv_sem=recv_sems.at[k], device_id=peer, device_id_type=pl.DeviceIdType.MESH)
            send.start()
            sends.append((send, peer, peer_rank))
        for k, (send, peer, peer_rank) in enumerate(sends):
            pltpu.make_async_remote_copy(
                src_ref=mine_for(peer_rank), dst_ref=out_ref.at[peer_rank], send_sem=send_sems.at[k],
                recv_sem=recv_sems.at[k], device_id=peer, device_id_type=pl.DeviceIdType.MESH).wait_recv()
        for send, _, _ in sends:
            send.wait_send()
        local.wait()

    return pl.pallas_call(
        body, name=name, in_specs=[pl.BlockSpec(memory_space=pl.ANY)], out_specs=pl.BlockSpec(memory_space=pl.ANY),
        out_shape=jax.ShapeDtypeStruct((n,) + tuple(shape), src.dtype),
        scratch_shapes=[pltpu.SemaphoreType.DMA((n - 1,)), pltpu.SemaphoreType.DMA((n - 1,)), pltpu.SemaphoreType.DMA(())],
    )(src)


CHUNK_BYTES = 1 << 20
MIN_CHUNK_ROWS = 64


def _row_parts(rows, cols, itemsize):
    n = 1
    while rows % (2 * n) == 0 and rows // (2 * n) >= MIN_CHUNK_ROWS and rows * cols * itemsize > n * CHUNK_BYTES:
        n *= 2
    return n


def _remote(src, dst, send_sem, recv_sem, peer):
    return pltpu.make_async_remote_copy(src_ref=src, dst_ref=dst, send_sem=send_sem, recv_sem=recv_sem,
                                        device_id=peer, device_id_type=pl.DeviceIdType.MESH)


def _start_rows(src, dst, send_sem, recv_sem, peer):
    rows, cols = src.shape
    n = _row_parts(rows, cols, jnp.dtype(src.dtype).itemsize)
    pr = rows // n
    for i in range(n):
        _remote(src.at[pl.ds(i * pr, pr), :], dst.at[pl.ds(i * pr, pr), :], send_sem, recv_sem, peer).start()


def _mesh_place():
    x, y, c = lax.axis_index("x"), lax.axis_index("y"), lax.axis_index("c")
    other_chips = ((1 - x, y), (x, 1 - y), (1 - x, 1 - y))
    return x, y, c, 2 * x + y, (x, y, 1 - c), other_chips


def _comm_call(body, name, ins, out_shapes, sem_counts):
    any_spec = pl.BlockSpec(memory_space=pl.ANY)
    return pl.pallas_call(
        body, name=name, in_specs=[any_spec] * len(ins), out_specs=[any_spec] * len(out_shapes), out_shape=out_shapes,
        scratch_shapes=[pltpu.SemaphoreType.DMA((n,)) for n in sem_counts])(*ins)


class _ChipSends:
    def __init__(self, srcs, pick):
        self.srcs, self.pick = list(srcs), pick
        nt = len(self.srcs)
        self.n_sems = 3 * nt
        self.out_shapes = []
        for s in self.srcs:
            r2 = s.shape[-2] // 2 if pick is _pick_weight_half else s.shape[-2]
            self.out_shapes.append(jax.ShapeDtypeStruct((3, r2, s.shape[-1]), s.dtype))

    def start(self, ins, outs, send_sems, recv_sems):
        _, _, c, _, _, other_chips = _mesh_place()
        for t in range(len(ins)):
            for j, (px, py) in enumerate(other_chips):
                _remote(self.pick(ins[t], c, 2 * px + py), outs[t].at[j], send_sems.at[3 * t + j], recv_sems.at[3 * t + j],
                        (px, py, c)).start()

    def finish(self, ins, outs, send_sems, recv_sems):
        _, _, c, _, _, other_chips = _mesh_place()
        for t in range(len(ins)):
            for j, (px, py) in enumerate(other_chips):
                landed = _remote(outs[t].at[j], outs[t].at[j], send_sems.at[3 * t + j], recv_sems.at[3 * t + j], (px, py, c))
                landed.wait_recv()
                landed.wait_send()

    def run(self, name):
        nt = len(self.srcs)

        def body(*refs):
            ins, outs = refs[:nt], refs[nt:2 * nt]
            self.start(ins, outs, *refs[2 * nt:])
            self.finish(ins, outs, *refs[2 * nt:])

        return _comm_call(body, name, self.srcs, self.out_shapes, (self.n_sems, self.n_sems))

    def specs(self):
        any_spec = pl.BlockSpec(memory_space=pl.ANY)
        return [any_spec] * len(self.srcs), [any_spec] * len(self.srcs), [pltpu.SemaphoreType.DMA((self.n_sems,))] * 2


def _pick_weight_half(src, core, chip):
    r2 = src.shape[0] // 2
    return src.at[pl.ds(pl.multiple_of(core * r2, MIN_CHUNK_ROWS), r2), :]


def _pick_slab(src, core, chip):
    return src.at[chip if src.shape[0] == N_CHIPS else 0]


def _swap_halves(gs):
    nt = len(gs)

    def body(*refs):
        ins, outs = refs[:nt], refs[nt:2 * nt]
        send_sems, recv_sems = refs[2 * nt:]
        _, _, c, _, sibling, _ = _mesh_place()
        for t in range(nt):
            for q in range(ins[t].shape[1]):
                _start_rows(ins[t].at[1 - c, q], outs[t].at[q], send_sems.at[t], recv_sems.at[t], sibling)
        for t in range(nt):
            _remote(outs[t], outs[t], send_sems.at[t], recv_sems.at[t], sibling).wait_recv()
        for t in range(nt):
            _remote(outs[t], outs[t], send_sems.at[t], recv_sems.at[t], sibling).wait_send()

    out_shapes = [jax.ShapeDtypeStruct(g.shape[1:], g.dtype) for g in gs]
    return _comm_call(body, "swap_halves", gs, out_shapes, (nt, nt))


def _swap_cores(halves):
    nt = len(halves)

    def body(*refs):
        ins, outs = refs[:nt], refs[nt:2 * nt]
        send_sems, recv_sems = refs[2 * nt:]
        _, _, _, _, sibling, _ = _mesh_place()
        for t in range(nt):
            for li in range(ins[t].shape[0]):
                _start_rows(ins[t].at[li], outs[t].at[li], send_sems.at[t], recv_sems.at[t], sibling)
        for t in range(nt):
            _remote(outs[t], outs[t], send_sems.at[t], recv_sems.at[t], sibling).wait_recv()
        for t in range(nt):
            _remote(ins[t], ins[t], send_sems.at[t], recv_sems.at[t], sibling).wait_send()

    out_shapes = [jax.ShapeDtypeStruct(h.shape, h.dtype) for h in halves]
    return _comm_call(body, "swap_cores", halves, out_shapes, (nt, nt))


def _pack_rows(pieces, cols, dtype):
    tile = SUBLANES * cols
    rows = []
    for p in pieces:
        flat = p.reshape(-1).astype(dtype)
        rows.append(jnp.pad(flat, (0, (-flat.shape[0]) % tile)).reshape(-1, cols))
    total = sum(r.shape[0] for r in rows)
    if total % (2 * SUBLANES):
        rows.append(jnp.zeros((SUBLANES, cols), dtype))
    return jnp.concatenate(rows, axis=0)


def _unpack_rows(buf, shapes):
    out, row = [], 0
    for s in shapes:
        n = math.prod(s)
        nrows = -(-n // (SUBLANES * buf.shape[1])) * SUBLANES
        out.append(buf[row:row + nrows].reshape(-1)[:n].reshape(s))
        row += nrows
    return out


def _permute_cols(w):
    return jnp.concatenate([w[..., s:s + n] for s, n in _PERM_PIECES], axis=-1)


def _unpermute_cols(w):
    off, pieces = 0, []
    for s, n in _PERM_PIECES:
        pieces.append((s, w[..., off:off + n]))
        off += n
    return jnp.concatenate([p for _, p in sorted(pieces, key=lambda sp: sp[0])], axis=-1)


ROW_SHARDED = ('w_out', 'w_glu')


def _assemble_full(name, g):
    nh, nc, nl, r2, cols = g.shape
    if name in ROW_SHARDED:
        return jnp.transpose(g, (2, 1, 0, 3, 4)).reshape(nl, nc * nh * r2, cols)
    return jnp.transpose(g, (2, 0, 3, 1, 4)).reshape(nl, nh * r2, nc * cols)


def _by_chip(own, related, x, y):
    grid = ((own, related[1]), (related[0], related[2]))
    along_x = [[jnp.where(x == 0, grid[px][dy], grid[1 - px][dy]) for dy in range(2)] for px in range(2)]
    return [jnp.where(y == 0, along_x[px][py], along_x[px][1 - py]) for px in range(2) for py in range(2)]


def _join_halves(mine, theirs, core, axis):
    return jnp.where(core == 0, jnp.concatenate([mine, theirs], axis=axis), jnp.concatenate([theirs, mine], axis=axis))


def _split_by_dest(name, full):
    nl, rows, cols = full.shape
    if name in ROW_SHARDED:
        return jnp.transpose(full.reshape(nl, N_CHIPS, 2, rows // (2 * N_CHIPS), cols), (2, 1, 0, 3, 4))
    return jnp.transpose(full.reshape(nl, 2, rows // 2, N_CHIPS, cols // N_CHIPS), (1, 3, 0, 2, 4))


SSM_RAW = ('ssm_a_re', 'ssm_a_im', 'ssm_log_dt', 'ssm_b_re', 'ssm_b_im', 'ssm_c_re', 'ssm_c_im')


def _all_tables(p):
    raw = tuple(p[k] for k in SSM_RAW)
    (_, bb, cb), vjp = jax.vjp(jax.vmap(_ssm_tables), *raw)
    return (bb.astype(BF16), cb.astype(BF16), jax.vmap(_ssm_powers)(*raw[:3])), vjp


def _layer_fwd(x, p, full, li, mod, tables, sends=None):
    bb, cb, pw = (t[li] for t in tables)
    h = _norm_fwd(x, _row(p['norm_g'][li]), mod)
    proj = _matmul(h, full['w_in'], 'nn', tm=512, tn=1280, tk=1024, name="proj", n_outer=True)
    o = _attention_fwd(proj, p['attn_sinks'][li])
    y_lin, states, brought = _ssm_fwd(proj, bb, cb, pw, sends)
    y_pool = _pool_fwd(proj, p['w_pool'][li].astype(BF16), _row(p['pool_scale'][li]))
    x_new = _merge_fwd(proj, o, y_lin, y_pool, x, _row(p['ssm_d'][li]), full['w_glu'], _row(p['b_glu'][li]),
                       full['w_br_att'], full['w_br_ssm'], full['w_br_pool'], full['w_out'], mod)
    saved = dict(x=x, h=h, proj=proj, o=o, y_lin=y_lin, y_pool=y_pool, states=states, bb=bb, cb=cb, pw=pw)
    return x_new, saved, brought


def _layer_bwd(dxo, s, p, full, li, mod, sends=None):
    l = dxo.shape[0]
    proj = s['proj']
    (do, dza, dyl, dus_skip, dzs, dyp, dzp, dgl, ya, ys, ypl, y2, dt, dpa, dps, dpp, mg, dout, st) = _merge_bwd(
        proj, s['o'], s['y_lin'], s['y_pool'], dxo, _row(p['ssm_d'][li]), full['w_glu'], _row(p['b_glu'][li]),
        full['w_br_att'], full['w_br_ssm'], full['w_br_pool'], full['w_out'], mod)
    g = {}
    g['w_br_att'] = _matmul(ya, dpa, 'tn', tm=512, tn=1024, tk=512, name="grad_w_br")
    g['w_br_ssm'] = _matmul(ys, dps, 'tn', tm=512, tn=1024, tk=512, name="grad_w_br")
    g['w_br_pool'] = _matmul(ypl, dpp, 'tn', tm=512, tn=1024, tk=512, name="grad_w_br")
    g['w_out'] = _matmul(mg, dout, 'tn', tm=512, tn=1024, tk=512, name="grad_w_out")
    g['w_glu'] = _matmul(y2, dt, 'tn', tm=512, tn=512, tk=512, name="grad_w_glu")
    g['b_glu'] = st[1, 0:SSM_W]
    g['ssm_d'] = st[1, SSM_W:2 * SSM_W]
    dgate = st[0]

    dq, dkv, dsink = _attention_bwd(proj, p['attn_sinks'][li], do)
    g['attn_sinks'] = dsink[:, 0]
    dus_scan, dbb, dcb, dlam, brought = _ssm_bwd(proj, s['states'], dyl, s['bb'], s['cb'], s['pw'], sends)
    g['ssm_tables'] = (dlam[:, 0:2, :], dbb, dcb)
    dup, dwp, dps_scale = _pool_bwd(proj, p['w_pool'][li].astype(BF16), _row(p['pool_scale'][li]), dyp)
    g['w_pool'] = dwp
    g['pool_scale'] = dps_scale[0]

    dproj = jnp.concatenate([piece.astype(BF16) for piece in
                             (dgl, dq, dza, dus_skip + dus_scan, dzs, dup, dzp, dkv[HALO:])], axis=1)
    g['w_in'] = _unpermute_cols(_matmul(s['h'], dproj, 'tn', tm=1024, tn=1280, tk=512, name="grad_w_in"))
    dh = _matmul(dproj, full['w_in'], 'nt', tm=1024, tn=1024, tk=1280, name="grad_h")
    dx, nst = _norm_bwd(s['x'], dh, dxo, _row(p['norm_g'][li]), mod)
    g['norm_g'] = nst[2]
    dmod = jnp.concatenate([nst[0], nst[1], dgate])
    del l
    return dx, g, dmod, brought


def _weight_sends(shards, li):
    return _ChipSends([shards[name][li] for name, _ in BIG], _pick_weight_half)


def _assemble_layer(shards, li, over_ici, place):
    x, y, core, _ = place
    from_sibling = _swap_cores(over_ici)
    full = {}
    for (name, _), a, b in zip(BIG, over_ici, from_sibling):
        own = shards[name][li]
        own_halves = own.reshape(2, own.shape[0] // 2, own.shape[1])
        halves = [_join_halves(a[j][None], b[j][None], core, 0) for j in range(3)]
        chips = _by_chip(own_halves, halves, x, y)
        full[name] = _assemble_full(name, jnp.stack(chips, axis=1)[:, :, None])[0]
    full['w_in'] = _permute_cols(full['w_in'])
    return full


def _pair_sums(tensors, wire, core):
    theirs = _swap_halves(tensors)
    return [_add2(lax.dynamic_index_in_dim(g, core, 0, keepdims=False), a, dt, "sum_core_pair")
            for g, a, dt in zip(tensors, theirs, wire)]


def _layer_grad_halves(layer_grads):
    return [_split_by_dest(name, layer_grads[name][None])[:, :, 0] for name, _ in BIG]


def _step(p, m, v, x, c, target):
    depth = p['norm_g'].shape[0]
    d = D_MODEL
    ix, iy, ic = lax.axis_index("x"), lax.axis_index("y"), lax.axis_index("c")
    chip = 2 * ix + iy
    dev = 4 * ix + 2 * iy + ic
    x0 = x[0]

    c_pad = jnp.pad(c, ((0, SUBLANES - 1), (0, 0)))
    c_all = _exchange(c_pad, 'xyc', False, "gather_c")[:, 0, :]
    n_ada = p['w_ada'].shape[-1]
    b_shard = lax.dynamic_slice_in_dim(p['b_ada'], chip * n_ada, n_ada, axis=1)[:, None, :]
    mod_shard = _ada_fwd(c_all, p['w_ada'].astype(BF16), b_shard)
    mod_all = _exchange(mod_shard.reshape(depth * N_DEV, n_ada), 'xy', False, "gather_mod")
    mod_all = jnp.transpose(mod_all.reshape(N_CHIPS, depth, N_DEV, n_ada), (1, 2, 0, 3)).reshape(depth, N_DEV, 3 * d)
    mods = lax.dynamic_index_in_dim(mod_all, dev, axis=1, keepdims=True)

    place = (ix, iy, ic, chip)
    n_big = len(BIG)

    shards = {name: p[name].astype(BF16) for name, _ in BIG}
    tables, tables_vjp = _all_tables(p)
    over_ici = _weight_sends(shards, 0).run("send_weight_halves")
    fulls, saved = [], []
    xs = x0
    for li in range(depth):
        fulls.append(_assemble_layer(shards, li, over_ici, place))
        sends = _weight_sends(shards, li + 1) if li + 1 < depth else None
        xs, s, over_ici = _layer_fwd(xs, p, fulls[li], li, mods[li], tables, sends)
        saved.append(s)
    dx, hst = _loss_head(xs, _row(p['final_g']), target[0])
    loss = lax.psum(hst[1, 0], ("x", "y", "c"))

    grads, dmods, pair, from_chips = [None] * depth, [None] * depth, [None] * depth, [None] * depth
    sends = None
    for li in reversed(range(depth)):
        dx, grads[li], dmods[li], brought = _layer_bwd(dx, saved[li], p, fulls[li], li, mods[li], sends)
        if sends is not None:
            from_chips[li + 1] = brought
        if li > 0:
            pair[li] = _pair_sums(_layer_grad_halves(grads[li]), [BF16] * n_big, ic)
            sends = _ChipSends(pair[li], _pick_slab)

    dmod_pad = jnp.pad(jnp.stack(dmods), ((0, SUBLANES - depth), (0, 0)))
    dmod_all = _exchange(dmod_pad, 'xyc', False, "gather_dmod")[:, :depth, :]
    dmod_cols = lax.dynamic_slice_in_dim(jnp.transpose(dmod_all, (1, 0, 2)), chip * n_ada, n_ada, axis=2)
    g_w_ada = _ada_bwd(jnp.transpose(c_all), dmod_cols)

    local_small = {k: jnp.stack([grads[li][k] for li in range(depth)])
                   for k in SMALL if k not in ('final_g', 'b_ada') + SSM_RAW}
    table_cotangents = tuple(jnp.stack([grads[li]['ssm_tables'][k] for li in range(depth)]) for k in range(3))
    local_small.update(zip(SSM_RAW, tables_vjp(table_cotangents)))
    local_small['final_g'] = hst[0]
    local_small['b_ada'] = jnp.stack(dmods)
    small_pack = _pack_rows([local_small[k] for k in SMALL], PACK_COLS, F32)

    small_halves = small_pack.reshape(2, 1, small_pack.shape[0] // 2, small_pack.shape[1])
    pair[0] = _pair_sums(_layer_grad_halves(grads[0]) + [small_halves], [BF16] * n_big + [F32], ic)
    from_chips[0] = _ChipSends(pair[0], _pick_slab).run("scatter_chips")
    halves = []
    for li in range(depth):
        for t, (ps, others) in enumerate(zip(pair[li], from_chips[li])):
            small = t == n_big
            own = ps[0] if small else lax.dynamic_index_in_dim(ps, chip, 0, keepdims=False)
            halves.append(_sum_chips(own, others, small, "sum_chips_small" if small else "sum_chips")[None])
    totals = [_join_halves(h[0], o[0], ic, 0) for h, o in zip(halves, _swap_cores(halves))]
    small_sum = totals[n_big]
    per_layer = [totals[:n_big]] + [totals[n_big + 1 + (li - 1) * n_big:n_big + 1 + li * n_big] for li in range(1, depth)]

    small_shapes = [p[k].shape for k in SMALL]
    grad = dict(zip(SMALL, _unpack_rows(small_sum, small_shapes)))
    grad['w_ada'] = g_w_ada
    for t, (name, _) in enumerate(BIG):
        grad[name] = jnp.stack([per_layer[li][t] for li in range(depth)])

    delta, new_m, new_v = {}, {}, {}
    outs = _adamw(_pack_rows([p[k] for k in SMALL], PACK_COLS, F32), small_sum,
                  _pack_rows([m[k] for k in SMALL], PACK_COLS, F32), _pack_rows([v[k] for k in SMALL], PACK_COLS, F32),
                  name="adamw_small")
    for res, o in zip((delta, new_m, new_v), outs):
        res.update(zip(SMALL, _unpack_rows(o, small_shapes)))
    for name in ['w_ada'] + [n for n, _ in BIG]:
        shape = p[name].shape
        two_d = (-1, shape[-1])
        outs = _adamw(p[name].reshape(two_d), grad[name].reshape(two_d), m[name].reshape(two_d), v[name].reshape(two_d),
                      name="adamw_" + name)
        delta[name], new_m[name], new_v[name] = (o.reshape(shape) for o in outs)

    return (loss, dx[None], *[grad[k] for k in WEIGHTS], *[delta[k] for k in WEIGHTS],
            *[new_m[k] for k in WEIGHTS], *[new_v[k] for k in WEIGHTS])


def kernel(x, c, norm_g, w_ada, b_ada, w_in, attn_sinks, ssm_a_re, ssm_a_im, ssm_log_dt, ssm_b_re, ssm_b_im, ssm_c_re, ssm_c_im, ssm_d, w_glu, b_glu, w_pool, pool_scale, w_br_att, w_br_ssm, w_br_pool, w_out, final_g, loss_target, m_norm_g, m_w_ada, m_b_ada, m_w_in, m_attn_sinks, m_ssm_a_re, m_ssm_a_im, m_ssm_log_dt, m_ssm_b_re, m_ssm_b_im, m_ssm_c_re, m_ssm_c_im, m_ssm_d, m_w_glu, m_b_glu, m_w_pool, m_pool_scale, m_w_br_att, m_w_br_ssm, m_w_br_pool, m_w_out, m_final_g, v_norm_g, v_w_ada, v_b_ada, v_w_in, v_attn_sinks, v_ssm_a_re, v_ssm_a_im, v_ssm_log_dt, v_ssm_b_re, v_ssm_b_im, v_ssm_c_re, v_ssm_c_im, v_ssm_d, v_w_glu, v_b_glu, v_w_pool, v_pool_scale, v_w_br_att, v_w_br_ssm, v_w_br_pool, v_w_out, v_final_g):
    p = dict(zip(WEIGHTS, (norm_g, w_ada, b_ada, w_in, attn_sinks, ssm_a_re, ssm_a_im, ssm_log_dt, ssm_b_re, ssm_b_im,
                           ssm_c_re, ssm_c_im, ssm_d, w_glu, b_glu, w_pool, pool_scale, w_br_att, w_br_ssm, w_br_pool,
                           w_out, final_g)))
    m = dict(zip(WEIGHTS, (m_norm_g, m_w_ada, m_b_ada, m_w_in, m_attn_sinks, m_ssm_a_re, m_ssm_a_im, m_ssm_log_dt,
                           m_ssm_b_re, m_ssm_b_im, m_ssm_c_re, m_ssm_c_im, m_ssm_d, m_w_glu, m_b_glu, m_w_pool,
                           m_pool_scale, m_w_br_att, m_w_br_ssm, m_w_br_pool, m_w_out, m_final_g)))
    v = dict(zip(WEIGHTS, (v_norm_g, v_w_ada, v_b_ada, v_w_in, v_attn_sinks, v_ssm_a_re, v_ssm_a_im, v_ssm_log_dt,
                           v_ssm_b_re, v_ssm_b_im, v_ssm_c_re, v_ssm_c_im, v_ssm_d, v_w_glu, v_b_glu, v_w_pool,
                           v_pool_scale, v_w_br_att, v_w_br_ssm, v_w_br_pool, v_w_out, v_final_g)))
    return _step(p, m, v, x, c, loss_target)
```

```python
import functools
import math

import jax
import jax.numpy as jnp
from jax import lax
from jax.experimental import pallas as pl
from jax.experimental.pallas import tpu as pltpu

F32 = jnp.float32
BF16 = jnp.bfloat16

D_MODEL = 1024
CHUNK = 64
N_HEADS = 8
N_KV_HEADS = 2
HEAD_DIM = 64
Q_PER_KV = N_HEADS // N_KV_HEADS
HALO = 128
ATT_W = 512
KV_W = 128
SSM_W = 512
SSM_GROUP = 16
SSM_GROUPS = 32
SSM_STATE = 64
POOL_W = 512
POOL_GW = 128
IN_W = 6400
EPS = 1e-6
NEG_INF = -1e30
ADAM_LR = 0.001
ADAM_B1 = 0.9
ADAM_B2 = 0.999
ADAM_EPS = 1e-08
ADAM_WD = 0.01
ADAM_STEP = 10

OFF_GL, OFF_Q, OFF_ZA, OFF_US, OFF_ZS, OFF_UP, OFF_ZP, OFF_KV = 0, 3072, 3584, 4096, 4608, 5120, 5632, 6144
_PERM_PIECES = ((3328, 3072), (0, 512), (1792, 512), (768, 512), (2304, 512), (1280, 512), (2816, 512), (512, 256))

LANES = 128
SUBLANES = 8
N_SBLK = SSM_GROUPS * SSM_STATE // LANES
VMEM_LIMIT = 48 * 1024 * 1024

N_CHIPS = 4
N_DEV = 8

WEIGHTS = ['norm_g', 'w_ada', 'b_ada', 'w_in', 'attn_sinks', 'ssm_a_re', 'ssm_a_im', 'ssm_log_dt', 'ssm_b_re',
           'ssm_b_im', 'ssm_c_re', 'ssm_c_im', 'ssm_d', 'w_glu', 'b_glu', 'w_pool', 'pool_scale', 'w_br_att',
           'w_br_ssm', 'w_br_pool', 'w_out', 'final_g']
SMALL = ['norm_g', 'b_ada', 'attn_sinks', 'ssm_a_re', 'ssm_a_im', 'ssm_log_dt', 'ssm_b_re', 'ssm_b_im', 'ssm_c_re',
         'ssm_c_im', 'ssm_d', 'b_glu', 'w_pool', 'pool_scale', 'final_g']
BIG = (('w_in', (1024, 1600)), ('w_br_att', (512, 256)), ('w_br_ssm', (512, 256)), ('w_br_pool', (512, 256)),
       ('w_out', (256, 1024)), ('w_glu', (128, 512)))
PACK_COLS = 1024


def _params(sem=None):
    return pltpu.CompilerParams(dimension_semantics=sem, vmem_limit_bytes=VMEM_LIMIT)


def _row(v):
    return v.reshape(1, -1)


def _full(shape):
    nd = len(shape)
    return pl.BlockSpec(shape, lambda *_: (0,) * nd)


def _sigmoid(v):
    return 1.0 / (1.0 + jnp.exp(-v))


def _silu_and_grad(z):
    s = _sigmoid(z)
    return z * s, s * (1.0 + z * (1.0 - s))


_GELU_K = math.sqrt(2.0 / math.pi)


def _gelu_and_grad(v):
    inner = _GELU_K * (v + 0.044715 * v * v * v)
    th = jnp.tanh(inner)
    val = 0.5 * v * (1.0 + th)
    grad = 0.5 * (1.0 + th) + 0.5 * v * (1.0 - th * th) * _GELU_K * (1.0 + 3 * 0.044715 * v * v)
    return val, grad


_NN = (((1,), (0,)), ((), ()))
_NT = (((1,), (1,)), ((), ()))
_TN = (((0,), (0,)), ((), ()))


def _dot(a, b, dims=_NN):
    return lax.dot_general(a.astype(BF16), b.astype(BF16), dims, preferred_element_type=F32)


def _matmul(a, b, mode, *, tm, tn, tk, name, out_dtype=F32, n_outer=False):
    if mode == 'nn':
        (m, k), (_, n) = a.shape, b.shape
    elif mode == 'nt':
        (m, k), (n, _) = a.shape, b.shape
    else:
        (k, m), (_, n) = a.shape, b.shape
    tm, tn, tk = min(tm, m), min(tn, n), min(tk, k)
    assert m % tm == 0 and n % tn == 0 and k % tk == 0, (name, a.shape, b.shape)
    nk = k // tk
    dims = {'nn': _NN, 'nt': _NT, 'tn': _TN}[mode]

    def body(a_ref, b_ref, o_ref, acc_ref):
        if nk == 1:
            o_ref[...] = _dot(a_ref[...], b_ref[...], dims).astype(out_dtype)
            return
        kk = pl.program_id(2)

        @pl.when(kk == 0)
        def _():
            acc_ref[...] = jnp.zeros_like(acc_ref)

        acc_ref[...] += _dot(a_ref[...], b_ref[...], dims)

        @pl.when(kk == nk - 1)
        def _():
            o_ref[...] = acc_ref[...].astype(out_dtype)

    def spec(shape, index):
        if n_outer:
            return pl.BlockSpec(shape, lambda j, i, kk: index(i, j, kk))
        return pl.BlockSpec(shape, index)

    a_spec = spec((tk, tm), lambda i, j, kk: (kk, i)) if mode == 'tn' else spec((tm, tk), lambda i, j, kk: (i, kk))
    b_spec = spec((tn, tk), lambda i, j, kk: (j, kk)) if mode == 'nt' else spec((tk, tn), lambda i, j, kk: (kk, j))
    grid = (n // tn, m // tm, nk) if n_outer else (m // tm, n // tn, nk)
    return pl.pallas_call(
        body, name=name, grid=grid, in_specs=[a_spec, b_spec], out_specs=spec((tm, tn), lambda i, j, kk: (i, j)),
        out_shape=jax.ShapeDtypeStruct((m, n), out_dtype), scratch_shapes=[pltpu.VMEM((tm, tn), F32)],
        compiler_params=_params(("parallel", "parallel", "arbitrary")))(a, b)


ROWS_NORM = 512


def _norm_fwd(x, g, mod):
    l, d = x.shape
    tr = min(ROWS_NORM, l)

    def body(x_ref, g_ref, mod_ref, h_ref):
        xv = x_ref[...]
        r = lax.rsqrt(jnp.mean(xv * xv, axis=-1, keepdims=True) + EPS)
        shift, scale = mod_ref[:, 0:d], mod_ref[:, d:2 * d]
        h_ref[...] = ((xv * r * g_ref[...]) * (1.0 + scale) + shift).astype(BF16)

    return pl.pallas_call(
        body, name="norm_fwd", grid=(l // tr,),
        in_specs=[pl.BlockSpec((tr, d), lambda i: (i, 0)), _full((1, d)), _full((1, 3 * d))],
        out_specs=pl.BlockSpec((tr, d), lambda i: (i, 0)), out_shape=jax.ShapeDtypeStruct((l, d), BF16),
        compiler_params=_params(("parallel",)))(x, g, mod)


def _norm_bwd(x, dh, dxo, g, mod):
    l, d = x.shape
    tr = min(ROWS_NORM, l)

    def body(x_ref, dh_ref, dxo_ref, g_ref, mod_ref, dx_ref, st_ref):
        @pl.when(pl.program_id(0) == 0)
        def _():
            st_ref[...] = jnp.zeros_like(st_ref)

        xv, dhv = x_ref[...], dh_ref[...]
        r = lax.rsqrt(jnp.mean(xv * xv, axis=-1, keepdims=True) + EPS)
        xn = xv * r
        gv = g_ref[...]
        sc1 = 1.0 + mod_ref[:, d:2 * d]
        dxn = dhv * gv * sc1
        dx_ref[...] = dxo_ref[...] + r * (dxn - xn * jnp.mean(dxn * xn, axis=-1, keepdims=True))
        st_ref[0:1, :] += jnp.sum(dhv, axis=0, keepdims=True)
        st_ref[1:2, :] += jnp.sum(dhv * xn * gv, axis=0, keepdims=True)
        st_ref[2:3, :] += jnp.sum(dhv * xn * sc1, axis=0, keepdims=True)

    blk = pl.BlockSpec((tr, d), lambda i: (i, 0))
    return pl.pallas_call(
        body, name="norm_bwd", grid=(l // tr,),
        in_specs=[blk, blk, blk, _full((1, d)), _full((1, 3 * d))],
        out_specs=[blk, _full((SUBLANES, d))],
        out_shape=[jax.ShapeDtypeStruct((l, d), F32), jax.ShapeDtypeStruct((SUBLANES, d), F32)],
        compiler_params=_params(("arbitrary",)))(x, dh, dxo, g, mod)


def _loss_head(x, g, target):
    l, d = x.shape
    tr = min(ROWS_NORM, l)

    def body(x_ref, g_ref, t_ref, dx_ref, st_ref):
        @pl.when(pl.program_id(0) == 0)
        def _():
            st_ref[...] = jnp.zeros_like(st_ref)

        xv = x_ref[...]
        r = lax.rsqrt(jnp.mean(xv * xv, axis=-1, keepdims=True) + EPS)
        xn = xv * r
        gv = g_ref[...]
        err = xn * gv - t_ref[...]
        part = 0.5 * jnp.sum(jnp.mean(err * err, axis=-1, keepdims=True), axis=0, keepdims=True)
        dy = err * (1.0 / d)
        dxn = dy * gv
        dx_ref[...] = r * (dxn - xn * jnp.mean(dxn * xn, axis=-1, keepdims=True))
        st_ref[0:1, :] += jnp.sum(dy * xn, axis=0, keepdims=True)
        st_ref[1:2, :] += jnp.broadcast_to(part, (1, d))

    blk = pl.BlockSpec((tr, d), lambda i: (i, 0))
    return pl.pallas_call(
        body, name="loss_head", grid=(l // tr,), in_specs=[blk, _full((1, d)), blk],
        out_specs=[blk, _full((SUBLANES, d))],
        out_shape=[jax.ShapeDtypeStruct((l, d), F32), jax.ShapeDtypeStruct((SUBLANES, d), F32)],
        compiler_params=_params(("arbitrary",)))(x, g, target)


ROWS_ATT = 128
ROWS_ATT_BWD = 256
_SLOPES = tuple(2.0 ** (-8.0 * (h + 1) / N_HEADS) for h in range(N_HEADS))


def _att_mask(i, t):
    r = lax.broadcasted_iota(jnp.int32, (t, t + HALO), 0)
    j = lax.broadcasted_iota(jnp.int32, (t, t + HALO), 1)
    dist = jnp.abs(r + HALO - j).astype(F32)
    rc, jc = r // CHUNK, j // CHUNK
    allowed = (jc >= rc) & (jc <= rc + 2) & ((j >= HALO) | (i > 0))
    return dist, allowed


def _att_probs(qh, k, dist, allowed, slope, sink):
    s = _dot(qh, k, _NT) * (1.0 / math.sqrt(HEAD_DIM)) - slope * dist
    s = jnp.where(allowed, s, NEG_INF)
    m = jnp.maximum(jnp.max(s, axis=1, keepdims=True), sink)
    e = jnp.exp(s - m)
    es = jnp.exp(sink - m)
    den = jnp.sum(e, axis=1, keepdims=True) + es
    return e / den, es / den


def _att_specs(t):
    q_spec = pl.BlockSpec((t, ATT_W), lambda i: (i, OFF_Q // ATT_W))
    kv_spec = pl.BlockSpec((t, 2 * KV_W), lambda i: (i, OFF_KV // (2 * KV_W)))
    halo_spec = pl.BlockSpec((HALO, 2 * KV_W), lambda i: (jnp.maximum(i * (t // HALO) - 1, 0), OFF_KV // (2 * KV_W)))
    return q_spec, kv_spec, halo_spec


def _attention_fwd(proj, sinks):
    l = proj.shape[0]
    t = min(ROWS_ATT, l)

    def body(sink_ref, q_ref, kv_ref, halo_ref, o_ref):
        dist, allowed = _att_mask(pl.program_id(0), t)
        kv = jnp.concatenate([halo_ref[...], kv_ref[...]], axis=0)
        for h in range(N_HEADS):
            kh = h // Q_PER_KV
            k = kv[:, kh * HEAD_DIM:(kh + 1) * HEAD_DIM]
            v = kv[:, KV_W + kh * HEAD_DIM:KV_W + (kh + 1) * HEAD_DIM]
            p, _ = _att_probs(q_ref[:, h * HEAD_DIM:(h + 1) * HEAD_DIM], k, dist, allowed, _SLOPES[h], sink_ref[h])
            o_ref[:, h * HEAD_DIM:(h + 1) * HEAD_DIM] = _dot(p, v)

    q_spec, kv_spec, halo_spec = _att_specs(t)
    return pl.pallas_call(
        body, name="attention_fwd", grid=(l // t,),
        in_specs=[pl.BlockSpec(memory_space=pltpu.SMEM), q_spec, kv_spec, halo_spec],
        out_specs=pl.BlockSpec((t, ATT_W), lambda i: (i, 0)), out_shape=jax.ShapeDtypeStruct((l, ATT_W), F32),
        compiler_params=_params(("parallel",)))(sinks, proj, proj, proj)


def _attention_bwd(proj, sinks, do):
    l = proj.shape[0]
    t = min(ROWS_ATT_BWD, l)

    def body(sink_ref, q_ref, kv_ref, halo_ref, do_ref, dq_ref, dkv_ref, dsink_ref):
        i = pl.program_id(0)

        @pl.when(i == 0)
        def _():
            dkv_ref[...] = jnp.zeros_like(dkv_ref)
            dsink_ref[...] = jnp.zeros_like(dsink_ref)

        dist, allowed = _att_mask(i, t)
        kv = jnp.concatenate([halo_ref[...], kv_ref[...]], axis=0)
        rows = pl.ds(pl.multiple_of(i * t, t), t + HALO)
        for kh in range(N_KV_HEADS):
            k = kv[:, kh * HEAD_DIM:(kh + 1) * HEAD_DIM]
            v = kv[:, KV_W + kh * HEAD_DIM:KV_W + (kh + 1) * HEAD_DIM]
            dk = jnp.zeros((t + HALO, HEAD_DIM), F32)
            dv = jnp.zeros((t + HALO, HEAD_DIM), F32)
            for h in range(kh * Q_PER_KV, (kh + 1) * Q_PER_KV):
                qh = q_ref[:, h * HEAD_DIM:(h + 1) * HEAD_DIM]
                doh = do_ref[:, h * HEAD_DIM:(h + 1) * HEAD_DIM]
                p, ps = _att_probs(qh, k, dist, allowed, _SLOPES[h], sink_ref[h])
                dp = _dot(doh, v, _NT)
                delta = jnp.sum(p * dp, axis=1, keepdims=True)
                ds = p * (dp - delta) * (1.0 / math.sqrt(HEAD_DIM))
                dsink_ref[h:h + 1, :] += jnp.broadcast_to(-jnp.sum(ps * delta, axis=0, keepdims=True), (1, LANES))
                dq_ref[:, h * HEAD_DIM:(h + 1) * HEAD_DIM] = _dot(ds, k)
                dk = dk + _dot(ds, qh, _TN)
                dv = dv + _dot(p, doh, _TN)
            dkv_ref[rows, kh * HEAD_DIM:(kh + 1) * HEAD_DIM] += dk
            dkv_ref[rows, KV_W + kh * HEAD_DIM:KV_W + (kh + 1) * HEAD_DIM] += dv

    q_spec, kv_spec, halo_spec = _att_specs(t)
    blk = pl.BlockSpec((t, ATT_W), lambda i: (i, 0))
    return pl.pallas_call(
        body, name="attention_bwd", grid=(l // t,),
        in_specs=[pl.BlockSpec(memory_space=pltpu.SMEM), q_spec, kv_spec, halo_spec, blk],
        out_specs=[blk, _full((HALO + l, 2 * KV_W)), _full((N_HEADS, LANES))],
        out_shape=[jax.ShapeDtypeStruct((l, ATT_W), F32), jax.ShapeDtypeStruct((HALO + l, 2 * KV_W), F32),
                   jax.ShapeDtypeStruct((N_HEADS, LANES), F32)],
        compiler_params=_params(("arbitrary",)))(sinks, proj, proj, proj, do)


ROWS_SSM = 512
SCAN_UNROLL = 4


def _ssm_discretize(a_re, a_im, log_dt, b_re, b_im):
    lam = lax.complex(a_re, a_im)
    dt = jnp.exp(log_dt)[:, None]
    lam_bar = jnp.exp(lam * dt)
    b_bar = ((lam_bar - 1.0) / lam)[..., None] * lax.complex(b_re, b_im)
    return lam, dt, lam_bar, b_bar


def _ssm_block_diag(m):
    e = m.reshape(N_SBLK, 2, SSM_GROUP, SSM_STATE)
    e = e[:, :, :, None, :] * jnp.eye(2, dtype=m.dtype)[None, :, None, :, None]
    e = e.reshape(N_SBLK, 2 * SSM_GROUP, LANES)
    oh = jax.nn.one_hot(jnp.arange(N_SBLK) % 4, 4, dtype=m.dtype)
    return (oh[:, :, None, None] * e[:, None]).reshape(N_SBLK, LANES, LANES)


def _ssm_tables(a_re, a_im, log_dt, b_re, b_im, c_re, c_im):
    _, _, lam_bar, b_bar = _ssm_discretize(a_re, a_im, log_dt, b_re, b_im)
    lam_blk = jnp.stack([jnp.real(lam_bar).reshape(N_SBLK, LANES), jnp.imag(lam_bar).reshape(N_SBLK, LANES)], axis=1)
    bt = jnp.transpose(b_bar, (0, 2, 1))
    bb = jnp.concatenate([_ssm_block_diag(jnp.real(bt)), _ssm_block_diag(jnp.imag(bt))], axis=2)
    cb = jnp.concatenate([jnp.transpose(_ssm_block_diag(c_re), (0, 2, 1)),
                          jnp.transpose(_ssm_block_diag(-c_im), (0, 2, 1))], axis=1)
    return lam_blk, bb, cb


def _ssm_powers(a_re, a_im, log_dt):
    lam = lax.complex(a_re, a_im)
    dt = jnp.exp(log_dt)[:, None]
    k = jnp.arange(1, SUBLANES + 1, dtype=F32)
    pw = jnp.exp((lam * dt)[None] * k[:, None, None]).reshape(SUBLANES, N_SBLK, LANES)
    pw = jnp.transpose(pw, (1, 0, 2))
    rev = pw[:, ::-1]
    return jnp.concatenate([jnp.real(pw), jnp.imag(pw), jnp.real(rev), jnp.imag(rev)], axis=1)


def _scan_consts(pw_ref, reverse):
    row = lax.broadcasted_iota(jnp.int32, (SUBLANES, LANES), 0)
    sign = -1.0 if reverse else 1.0

    def power(k):
        return (jnp.broadcast_to(pw_ref[0, k - 1:k, :], (SUBLANES, LANES)),
                sign * jnp.broadcast_to(pw_ref[0, SUBLANES + k - 1:SUBLANES + k, :], (SUBLANES, LANES)))

    steps = []
    for d in (1, 2, 4):
        pr, pi = power(d)
        keep = (row < SUBLANES - d) if reverse else (row >= d)
        steps.append((d, jnp.where(keep, pr, 0.0), jnp.where(keep, pi, 0.0)))
    base = 2 * SUBLANES if reverse else 0
    return (steps, pw_ref[0, base:base + SUBLANES, :], sign * pw_ref[0, base + SUBLANES:base + 2 * SUBLANES, :],
            power(SUBLANES))


def _scan_tile(xr, xi, steps, reverse):
    for d, ar, ai in steps:
        shift = SUBLANES - d if reverse else d
        rr, ri = pltpu.roll(xr, shift, 0), pltpu.roll(xi, shift, 0)
        xr, xi = xr + (ar * rr - ai * ri), xi + (ar * ri + ai * rr)
    return xr, xi


def _bcast_row(v, r):
    return jnp.broadcast_to(v[r:r + 1, :], (SUBLANES, LANES))


def _scan_forward(s_scr, pw_ref, n_tiles):
    steps, pr, pi, (l8r, l8i) = _scan_consts(pw_ref, False)

    def tiles(g, carry):
        cr, ci = carry
        rows = [pl.ds(pl.multiple_of((g * SCAN_UNROLL + u) * SUBLANES, SUBLANES), SUBLANES) for u in range(SCAN_UNROLL)]
        loaded = [(s_scr[r, 0:LANES], s_scr[r, LANES:2 * LANES]) for r in rows]
        local = [_scan_tile(xr, xi, steps, False) for xr, xi in loaded]
        for r, (xr, xi) in zip(rows, local):
            s_scr[r, 0:LANES] = xr + (pr * cr - pi * ci)
            s_scr[r, LANES:2 * LANES] = xi + (pr * ci + pi * cr)
            cr, ci = ((l8r * cr - l8i * ci) + _bcast_row(xr, SUBLANES - 1),
                      (l8r * ci + l8i * cr) + _bcast_row(xi, SUBLANES - 1))
        return cr, ci

    zero = jnp.zeros((SUBLANES, LANES), F32)
    lax.fori_loop(0, n_tiles // SCAN_UNROLL, tiles, (zero, zero))


def _row_chunks(l):
    rc = min(ROWS_SSM, l)
    return rc, l // rc


def _carry_sends(body, n_in, n_out, n_scratch, n_steps, sends):
    if sends is None:
        return body, [], [], [], [], []
    k = len(sends.srcs)

    def carrying(*refs):
        ins, send_ins = refs[:n_in], refs[n_in:n_in + k]
        outs, send_outs = refs[n_in + k:n_in + k + n_out], refs[n_in + k + n_out:n_in + 2 * k + n_out]
        rest = refs[n_in + 2 * k + n_out:]
        scratch, sems = rest[:n_scratch], rest[n_scratch:]
        step = pl.program_id(0)

        @pl.when(step == 0)
        def _():
            sends.start(send_ins, send_outs, *sems)

        body(*ins, *outs, *scratch)

        @pl.when(step == n_steps - 1)
        def _():
            sends.finish(send_ins, send_outs, *sems)

    in_specs, out_specs, scratch = sends.specs()
    return carrying, in_specs, out_specs, sends.out_shapes, scratch, sends.srcs


def _ssm_fwd(proj, bb, cb, pw, sends=None):
    l = proj.shape[0]
    rc, n_chunks = _row_chunks(l)

    def body(u_ref, bb_ref, cb_ref, pw_ref, y_ref, s_scr):
        j = pl.program_id(0)

        def fill(ci, _):
            rows = pl.ds(pl.multiple_of(ci * rc, rc), rc)
            s_scr[rows, :] = _dot(u_ref[rows, :], bb_ref[0])
            return 0

        lax.fori_loop(0, n_chunks, fill, 0)
        _scan_forward(s_scr, pw_ref, l // SUBLANES)

        @pl.when(j % 4 == 0)
        def _():
            y_ref[...] = jnp.zeros_like(y_ref)

        def emit(ci, _):
            rows = pl.ds(pl.multiple_of(ci * rc, rc), rc)
            y_ref[rows, :] += _dot(s_scr[rows, :], cb_ref[0])
            return 0

        lax.fori_loop(0, n_chunks, emit, 0)

    body, x_in, x_out, x_shapes, x_scratch, x_args = _carry_sends(body, 4, 2, 0, N_SBLK, sends)
    outs = pl.pallas_call(
        body, name="ssm_fwd" if sends is None else "ssm_fwd_sending", grid=(N_SBLK,),
        in_specs=[pl.BlockSpec((l, LANES), lambda j: (0, OFF_US // LANES + j // 4)),
                  pl.BlockSpec((1, LANES, 2 * LANES), lambda j: (j, 0, 0)),
                  pl.BlockSpec((1, 2 * LANES, LANES), lambda j: (j, 0, 0)),
                  pl.BlockSpec((1, 4 * SUBLANES, LANES), lambda j: (j, 0, 0))] + x_in,
        out_specs=[pl.BlockSpec((l, LANES), lambda j: (0, j // 4)), pl.BlockSpec((l, 2 * LANES), lambda j: (0, j))] + x_out,
        out_shape=[jax.ShapeDtypeStruct((l, SSM_W), F32), jax.ShapeDtypeStruct((l, N_SBLK * 2 * LANES), F32)] + x_shapes,
        scratch_shapes=x_scratch, compiler_params=_params(("arbitrary",)))(proj, bb, cb, pw, *x_args)
    return outs[0], outs[1], list(outs[2:])


def _ssm_bwd(proj, states, dy, bb, cb, pw, sends=None):
    l = proj.shape[0]
    rc, n_chunks = _row_chunks(l)
    n_tiles = l // SUBLANES

    def body(u_ref, s_scr, dy_ref, bb_ref, cb_ref, pw_ref, du_ref, dbb_ref, dcb_ref, dlam_ref, a_scr):
        j = pl.program_id(0)
        dcb_ref[...] = jnp.zeros_like(dcb_ref)

        def through_c(ci, _):
            rows = pl.ds(pl.multiple_of(ci * rc, rc), rc)
            dyv = dy_ref[rows, :]
            dcb_ref[0] += _dot(s_scr[rows, :], dyv, _TN)
            a_scr[rows, :] = _dot(dyv, cb_ref[0], _NT)
            return 0

        lax.fori_loop(0, n_chunks, through_c, 0)

        steps, pr, pi, (l8r, l8i) = _scan_consts(pw_ref, True)
        row = lax.broadcasted_iota(jnp.int32, (SUBLANES, LANES), 0)

        last = row == SUBLANES - 1

        def tiles(g, carry):
            cr, ci, accr, acci = carry
            rows = [pl.ds(pl.multiple_of((n_tiles - 1 - (g * SCAN_UNROLL + u)) * SUBLANES, SUBLANES), SUBLANES)
                    for u in range(SCAN_UNROLL)]
            loaded = [(a_scr[r, 0:LANES], a_scr[r, LANES:2 * LANES]) for r in rows]
            states = [(s_scr[r, 0:LANES], s_scr[r, LANES:2 * LANES]) for r in rows]
            local = [_scan_tile(xr, xi, steps, True) for xr, xi in loaded]
            for r, (xr, xi), (sr, si) in zip(rows, local, states):
                gr, gi = xr + (pr * cr - pi * ci), xi + (pr * ci + pi * cr)
                a_scr[r, 0:LANES] = gr
                a_scr[r, LANES:2 * LANES] = gi
                ur = jnp.where(last, cr, pltpu.roll(gr, SUBLANES - 1, 0))
                ui = jnp.where(last, ci, pltpu.roll(gi, SUBLANES - 1, 0))
                accr, acci = accr + (ur * sr + ui * si), acci + (ui * sr - ur * si)
                cr, ci = (l8r * cr - l8i * ci) + _bcast_row(xr, 0), (l8r * ci + l8i * cr) + _bcast_row(xi, 0)
            return cr, ci, accr, acci

        zero = jnp.zeros((SUBLANES, LANES), F32)
        _, _, accr, acci = lax.fori_loop(0, n_tiles // SCAN_UNROLL, tiles, (zero, zero, zero, zero))
        dlr = jnp.broadcast_to(jnp.sum(accr, axis=0, keepdims=True), (SUBLANES, LANES))
        dli = jnp.broadcast_to(jnp.sum(acci, axis=0, keepdims=True), (SUBLANES, LANES))
        dlam_ref[0] = jnp.where(row == 0, dlr, jnp.where(row == 1, dli, 0.0))

        dbb_ref[...] = jnp.zeros_like(dbb_ref)

        @pl.when(j % 4 == 0)
        def _():
            du_ref[...] = jnp.zeros_like(du_ref)

        def through_b(ci, _):
            rows = pl.ds(pl.multiple_of(ci * rc, rc), rc)
            av = a_scr[rows, :]
            dbb_ref[0] += _dot(u_ref[rows, :], av, _TN)
            du_ref[rows, :] += _dot(av, bb_ref[0], _NT)
            return 0

        lax.fori_loop(0, n_chunks, through_b, 0)

    body, x_in, x_out, x_shapes, x_scratch, x_args = _carry_sends(body, 6, 4, 1, N_SBLK, sends)
    outs = pl.pallas_call(
        body, name="ssm_bwd" if sends is None else "ssm_bwd_sending", grid=(N_SBLK,),
        in_specs=[pl.BlockSpec((l, LANES), lambda j: (0, OFF_US // LANES + j // 4)),
                  pl.BlockSpec((l, 2 * LANES), lambda j: (0, j)),
                  pl.BlockSpec((l, LANES), lambda j: (0, j // 4)),
                  pl.BlockSpec((1, LANES, 2 * LANES), lambda j: (j, 0, 0)),
                  pl.BlockSpec((1, 2 * LANES, LANES), lambda j: (j, 0, 0)),
                  pl.BlockSpec((1, 4 * SUBLANES, LANES), lambda j: (j, 0, 0))] + x_in,
        out_specs=[pl.BlockSpec((l, LANES), lambda j: (0, j // 4)),
                   pl.BlockSpec((1, LANES, 2 * LANES), lambda j: (j, 0, 0)),
                   pl.BlockSpec((1, 2 * LANES, LANES), lambda j: (j, 0, 0)),
                   pl.BlockSpec((1, SUBLANES, LANES), lambda j: (j, 0, 0))] + x_out,
        out_shape=[jax.ShapeDtypeStruct((l, SSM_W), F32), jax.ShapeDtypeStruct((N_SBLK, LANES, 2 * LANES), F32),
                   jax.ShapeDtypeStruct((N_SBLK, 2 * LANES, LANES), F32),
                   jax.ShapeDtypeStruct((N_SBLK, SUBLANES, LANES), F32)] + x_shapes,
        scratch_shapes=[pltpu.VMEM((l, 2 * LANES), F32)] + x_scratch,
        compiler_params=_params(("arbitrary",)))(proj, states, dy, bb, cb, pw, *x_args)
    return outs[0], outs[1], outs[2], outs[3], list(outs[4:])


def _pool_windows(v, t, l, ahead):
    def shifted(a, d):
        if ahead:
            return jnp.where(t < l - d, pltpu.roll(a, l - d, 0), 0.0)
        return jnp.where(t >= d, pltpu.roll(a, d, 0), 0.0)

    w2 = v + shifted(v, 1)
    w4 = w2 + shifted(w2, 2)
    w8 = w4 + shifted(w4, 4)
    w16 = w8 + shifted(w8, 8)
    return w2, w4, w8, w16


def _pool_select(g, ws):
    return jnp.where(g == 0, ws[0], jnp.where(g == 1, ws[1], jnp.where(g == 2, ws[2], ws[3])))


def _pool_count(g, t):
    return jnp.minimum(t + 1, jnp.left_shift(2, g)).astype(F32)


def _pool_specs(l):
    return [pl.BlockSpec((l, POOL_GW), lambda g: (0, OFF_UP // POOL_GW + g)),
            pl.BlockSpec((1, POOL_GW, POOL_GW), lambda g: (g, 0, 0)),
            pl.BlockSpec((1, POOL_GW), lambda g: (0, g))]


def _pool_fwd(proj, w_pool, scale):
    l = proj.shape[0]

    def body(u_ref, w_ref, sc_ref, y_ref):
        g = pl.program_id(0)
        t = lax.broadcasted_iota(jnp.int32, (l, 1), 0)
        u = u_ref[...]
        pooled = _pool_select(g, _pool_windows(u, t, l, False)) / _pool_count(g, t) - u
        y_ref[...] = _dot(pooled, w_ref[0]) * sc_ref[...]

    return pl.pallas_call(
        body, name="pool_fwd", grid=(4,), in_specs=_pool_specs(l),
        out_specs=pl.BlockSpec((l, POOL_GW), lambda g: (0, g)), out_shape=jax.ShapeDtypeStruct((l, POOL_W), F32),
        compiler_params=_params(("parallel",)))(proj, w_pool, scale)


def _pool_bwd(proj, w_pool, scale, dy):
    l = proj.shape[0]

    def body(u_ref, w_ref, sc_ref, dy_ref, du_ref, dw_ref, dsc_ref):
        g = pl.program_id(0)
        t = lax.broadcasted_iota(jnp.int32, (l, 1), 0)
        u = u_ref[...]
        cnt = _pool_count(g, t)
        pooled = _pool_select(g, _pool_windows(u, t, l, False)) / cnt - u
        dyv = dy_ref[...]
        dsc_ref[...] = jnp.sum(dyv * _dot(pooled, w_ref[0]), axis=0, keepdims=True)
        dyl = dyv * sc_ref[...]
        dw_ref[0] = _dot(pooled, dyl, _TN)
        dpooled = _dot(dyl, w_ref[0], _NT)
        du_ref[...] = _pool_select(g, _pool_windows(dpooled / cnt, t, l, True)) - dpooled

    return pl.pallas_call(
        body, name="pool_bwd", grid=(4,), in_specs=_pool_specs(l) + [pl.BlockSpec((l, POOL_GW), lambda g: (0, g))],
        out_specs=[pl.BlockSpec((l, POOL_GW), lambda g: (0, g)), pl.BlockSpec((1, POOL_GW, POOL_GW), lambda g: (g, 0, 0)),
                   pl.BlockSpec((1, POOL_GW), lambda g: (0, g))],
        out_shape=[jax.ShapeDtypeStruct((l, POOL_W), F32), jax.ShapeDtypeStruct((4, POOL_GW, POOL_GW), F32),
                   jax.ShapeDtypeStruct((1, POOL_W), F32)],
        compiler_params=_params(("parallel",)))(proj, w_pool, scale, dy)


ROWS_MERGE = 128
ROWS_MERGE_FWD = 256


def _merge_inputs(tr):
    def col(off, w):
        return pl.BlockSpec((tr, w), lambda i: (i, off // w))

    def act(w):
        return pl.BlockSpec((tr, w), lambda i: (i, 0))

    d = D_MODEL
    return ([col(OFF_GL, 3 * d), col(OFF_ZA, ATT_W), col(OFF_US, SSM_W), col(OFF_ZS, SSM_W), col(OFF_ZP, POOL_W),
             act(ATT_W), act(SSM_W), act(POOL_W)]
            + [_full((1, SSM_W)), _full((SSM_W, SSM_W)), _full((1, SSM_W)), _full((ATT_W, d)), _full((SSM_W, d)),
               _full((POOL_W, d)), _full((d, d)), _full((1, 3 * d))])


def _merge_forward_math(gl_ref, za_ref, us_ref, zs_ref, zp_ref, o_ref, yl_ref, yp_ref, d_ref, wg_ref, bg_ref,
                        wba_ref, wbs_ref, wbp_ref):
    d = D_MODEL
    r = {}
    r['sa'], r['dsa'] = _silu_and_grad(za_ref[...])
    r['ss'], r['dss'] = _silu_and_grad(zs_ref[...])
    r['sp'], r['dsp'] = _silu_and_grad(zp_ref[...])
    r['y_att'] = o_ref[...] * r['sa']
    y1 = yl_ref[...] + d_ref[...] * us_ref[...]
    r['y2'], r['dgelu'] = _gelu_and_grad(y1)
    r['sg'] = _sigmoid(_dot(r['y2'], wg_ref[...]) + bg_ref[...])
    r['y3'] = r['y2'] * r['sg']
    r['y_ssm'] = r['y3'] * r['ss']
    r['y_pool'] = yp_ref[...] * r['sp']
    r['g'] = [_sigmoid(gl_ref[:, b * d:(b + 1) * d]) for b in range(3)]
    r['p'] = [_dot(r['y_att'], wba_ref[...]), _dot(r['y_ssm'], wbs_ref[...]), _dot(r['y_pool'], wbp_ref[...])]
    r['merged'] = r['g'][0] * r['p'][0] + r['g'][1] * r['p'][1] + r['g'][2] * r['p'][2]
    return r


def _merge_fwd(proj, o, y_lin, y_pool, x, ssm_d, w_glu, b_glu, wba, wbs, wbp, w_out, mod):
    l, d = x.shape
    tr = min(ROWS_MERGE_FWD, l)

    def body(gl_ref, za_ref, us_ref, zs_ref, zp_ref, o_ref, yl_ref, yp_ref, d_ref, wg_ref, bg_ref, wba_ref, wbs_ref,
             wbp_ref, wo_ref, mod_ref, x_ref, xn_ref):
        r = _merge_forward_math(gl_ref, za_ref, us_ref, zs_ref, zp_ref, o_ref, yl_ref, yp_ref, d_ref, wg_ref, bg_ref,
                                wba_ref, wbs_ref, wbp_ref)
        xn_ref[...] = x_ref[...] + mod_ref[:, 2 * d:3 * d] * _dot(r['merged'], wo_ref[...])

    blk = pl.BlockSpec((tr, d), lambda i: (i, 0))
    return pl.pallas_call(
        body, name="merge_fwd", grid=(l // tr,), in_specs=_merge_inputs(tr) + [blk], out_specs=blk,
        out_shape=jax.ShapeDtypeStruct((l, d), F32), compiler_params=_params(("parallel",)))(
            proj, proj, proj, proj, proj, o, y_lin, y_pool, ssm_d, w_glu, b_glu, wba, wbs, wbp, w_out, mod, x)


def _merge_bwd(proj, o, y_lin, y_pool, dxo, ssm_d, w_glu, b_glu, wba, wbs, wbp, w_out, mod):
    l, d = dxo.shape
    tr = min(ROWS_MERGE, l)

    def body(gl_ref, za_ref, us_ref, zs_ref, zp_ref, o_ref, yl_ref, yp_ref, d_ref, wg_ref, bg_ref, wba_ref, wbs_ref,
             wbp_ref, wo_ref, mod_ref, dxo_ref,
             do_ref, dza_ref, dyl_ref, dus_ref, dzs_ref, dyp_ref, dzp_ref, dgl_ref,
             ya_ref, ys_ref, ypl_ref, y2_ref, dt_ref, dpa_ref, dps_ref, dpp_ref, mg_ref, dout_ref, st_ref):
        @pl.when(pl.program_id(0) == 0)
        def _():
            st_ref[...] = jnp.zeros_like(st_ref)

        r = _merge_forward_math(gl_ref, za_ref, us_ref, zs_ref, zp_ref, o_ref, yl_ref, yp_ref, d_ref, wg_ref, bg_ref,
                                wba_ref, wbs_ref, wbp_ref)
        dxov = dxo_ref[...]
        out = _dot(r['merged'], wo_ref[...])
        st_ref[0:1, :] += jnp.sum(dxov * out, axis=0, keepdims=True)
        dout = dxov * mod_ref[:, 2 * d:3 * d]
        dmerged = _dot(dout, wo_ref[...], _NT)
        dys = []
        for b, (w_ref, dp_ref) in enumerate(((wba_ref, dpa_ref), (wbs_ref, dps_ref), (wbp_ref, dpp_ref))):
            gb = r['g'][b]
            dgl_ref[:, b * d:(b + 1) * d] = (dmerged * r['p'][b] * gb * (1.0 - gb)).astype(BF16)
            dp = dmerged * gb
            dp_ref[...] = dp.astype(BF16)
            dys.append(_dot(dp, w_ref[...], _NT))
        mg_ref[...] = r['merged'].astype(BF16)
        dout_ref[...] = dout.astype(BF16)
        ya_ref[...] = r['y_att'].astype(BF16)
        ys_ref[...] = r['y_ssm'].astype(BF16)
        ypl_ref[...] = r['y_pool'].astype(BF16)
        y2_ref[...] = r['y2'].astype(BF16)
        do_ref[...] = (dys[0] * r['sa']).astype(BF16)
        dza_ref[...] = (dys[0] * o_ref[...] * r['dsa']).astype(BF16)
        dy3 = dys[1] * r['ss']
        dzs_ref[...] = (dys[1] * r['y3'] * r['dss']).astype(BF16)
        dt = dy3 * r['y2'] * r['sg'] * (1.0 - r['sg'])
        dt_ref[...] = dt.astype(BF16)
        dy1 = (dy3 * r['sg'] + _dot(dt, wg_ref[...], _NT)) * r['dgelu']
        dyl_ref[...] = dy1.astype(BF16)
        dus_ref[...] = dy1 * d_ref[...]
        st_ref[1:2, 0:SSM_W] += jnp.sum(dt, axis=0, keepdims=True)
        st_ref[1:2, SSM_W:2 * SSM_W] += jnp.sum(dy1 * us_ref[...], axis=0, keepdims=True)
        dyp_ref[...] = dys[2] * r['sp']
        dzp_ref[...] = (dys[2] * yp_ref[...] * r['dsp']).astype(BF16)

    blk = pl.BlockSpec((tr, d), lambda i: (i, 0))
    half = pl.BlockSpec((tr, ATT_W), lambda i: (i, 0))
    wide = pl.BlockSpec((tr, 3 * d), lambda i: (i, 0))
    sds = jax.ShapeDtypeStruct
    half_types = [BF16, BF16, BF16, F32, BF16, F32, BF16]
    return pl.pallas_call(
        body, name="merge_bwd", grid=(l // tr,), in_specs=_merge_inputs(tr) + [blk],
        out_specs=[half] * 7 + [wide] + [half] * 5 + [blk] * 5 + [_full((SUBLANES, d))],
        out_shape=[sds((l, ATT_W), t) for t in half_types] + [sds((l, 3 * d), BF16)] + [sds((l, ATT_W), BF16)] * 5
        + [sds((l, d), BF16)] * 5 + [sds((SUBLANES, d), F32)],
        compiler_params=_params(("arbitrary",)))(
            proj, proj, proj, proj, proj, o, y_lin, y_pool, ssm_d, w_glu, b_glu, wba, wbs, wbp, w_out, mod, dxo)


def _ada_fwd(c_all, w_ada, b_shard):
    depth, d, n = w_ada.shape

    def body(c_ref, w_ref, b_ref, o_ref):
        act, _ = _silu_and_grad(c_ref[...])
        o_ref[0] = _dot(act, w_ref[0]) + b_ref[0]

    return pl.pallas_call(
        body, name="ada_fwd", grid=(depth,),
        in_specs=[_full((N_DEV, d)), pl.BlockSpec((1, d, n), lambda i: (i, 0, 0)), pl.BlockSpec((1, 1, n), lambda i: (i, 0, 0))],
        out_specs=pl.BlockSpec((1, N_DEV, n), lambda i: (i, 0, 0)), out_shape=jax.ShapeDtypeStruct((depth, N_DEV, n), F32),
        compiler_params=_params(("parallel",)))(c_all, w_ada, b_shard)


def _ada_bwd(c_all_t, dmod):
    d = c_all_t.shape[0]
    depth, _, n = dmod.shape

    def body(c_ref, dm_ref, o_ref):
        act, _ = _silu_and_grad(c_ref[...])
        acc = act[:, 0:1] * dm_ref[0, 0:1, :]
        for b in range(1, N_DEV):
            acc = acc + act[:, b:b + 1] * dm_ref[0, b:b + 1, :]
        o_ref[0] = acc

    return pl.pallas_call(
        body, name="ada_bwd", grid=(depth,),
        in_specs=[_full((d, N_DEV)), pl.BlockSpec((1, N_DEV, n), lambda i: (i, 0, 0))],
        out_specs=pl.BlockSpec((1, d, n), lambda i: (i, 0, 0)), out_shape=jax.ShapeDtypeStruct((depth, d, n), F32),
        compiler_params=_params(("parallel",)))(c_all_t, dmod)


ROWS_FLAT = 256


_RELATION_XOR = (0, 2, 1, 3)


def _sum_chips(own, others, chip_order, name):
    r, c = own.shape
    tr = math.gcd(ROWS_FLAT, r)

    def body(own_ref, oth_ref, o_ref):
        terms = [own_ref[...].astype(F32)] + [oth_ref[k].astype(F32) for k in range(3)]
        if chip_order:
            chip = 2 * lax.axis_index("x") + lax.axis_index("y")
            by_chip = []
            for q in range(N_CHIPS):
                rel = jnp.bitwise_xor(chip, q)
                pick = terms[3]
                for k in (2, 1, 0):
                    pick = jnp.where(rel == _RELATION_XOR[k], terms[k], pick)
                by_chip.append(pick)
            terms = by_chip
        o_ref[...] = ((terms[0] + terms[1]) + terms[2]) + terms[3]

    return pl.pallas_call(
        body, name=name, grid=(r // tr,),
        in_specs=[pl.BlockSpec((tr, c), lambda i: (i, 0)), pl.BlockSpec((3, tr, c), lambda i: (0, i, 0))],
        out_specs=pl.BlockSpec((tr, c), lambda i: (i, 0)), out_shape=jax.ShapeDtypeStruct((r, c), F32),
        compiler_params=_params(("parallel",)))(own, others)


def _add2(a, b, out_dtype, name):
    shape = a.shape
    a, b = a.reshape(-1, shape[-1]), b.reshape(-1, shape[-1])
    r, c = a.shape
    tr = math.gcd(ROWS_FLAT, r)

    def body(a_ref, b_ref, o_ref):
        o_ref[...] = (a_ref[...] + b_ref[...]).astype(out_dtype)

    blk = pl.BlockSpec((tr, c), lambda i: (i, 0))
    return pl.pallas_call(
        body, name=name, grid=(r // tr,), in_specs=[blk, blk], out_specs=blk,
        out_shape=jax.ShapeDtypeStruct((r, c), out_dtype), compiler_params=_params(("parallel",)))(a, b).reshape(shape)


def _adamw(w, g, m, v, name):
    r, c = w.shape
    tr = math.gcd(ROWS_FLAT, r)

    def body(w_ref, g_ref, m_ref, v_ref, d_ref, nm_ref, nv_ref):
        gv = g_ref[...]
        mv = ADAM_B1 * m_ref[...] + (1.0 - ADAM_B1) * gv
        vv = ADAM_B2 * v_ref[...] + (1.0 - ADAM_B2) * (gv * gv)
        m_hat = mv / (1.0 - ADAM_B1 ** ADAM_STEP)
        v_hat = vv / (1.0 - ADAM_B2 ** ADAM_STEP)
        d_ref[...] = -ADAM_LR * (m_hat / (jnp.sqrt(v_hat) + ADAM_EPS) + ADAM_WD * w_ref[...])
        nm_ref[...] = mv
        nv_ref[...] = vv

    blk = pl.BlockSpec((tr, c), lambda i: (i, 0))
    return pl.pallas_call(
        body, name=name, grid=(r // tr,), in_specs=[blk] * 4, out_specs=[blk] * 3,
        out_shape=[jax.ShapeDtypeStruct((r, c), F32)] * 3, compiler_params=_params(("parallel",)))(w, g, m, v)


_GROUP_MASKS = {
    'xy': ((1, 0, 0), (0, 1, 0), (1, 1, 0)),
    'c': ((0, 0, 1),),
    'xyc': ((0, 0, 1), (0, 1, 0), (0, 1, 1), (1, 0, 0), (1, 0, 1), (1, 1, 0), (1, 1, 1)),
}


def _group_rank(group, pos):
    x, y, c = pos
    return {'xy': 2 * x + y, 'c': c, 'xyc': 4 * x + 2 * y + c}[group]


def _exchange(src, group, scatter, name):
    masks = _GROUP_MASKS[group]
    n = len(masks) + 1
    shape = src.shape[1:] if scatter else src.shape
    assert (not scatter) or src.shape[0] == n

    def body(src_ref, out_ref, send_sems, recv_sems, local_sem):
        me = (lax.axis_index("x"), lax.axis_index("y"), lax.axis_index("c"))
        my_rank = _group_rank(group, me)

        def mine_for(rank):
            return src_ref.at[rank] if scatter else src_ref

        local = pltpu.make_async_copy(mine_for(my_rank), out_ref.at[my_rank], local_sem)
        local.start()
        sends = []
        for k, mask in enumerate(masks):
            peer = tuple(1 - p if f else p for p, f in zip(me, mask))
            peer_rank = _group_rank(group, peer)
            send = pltpu.make_async_remote_copy(
                src_ref=mine_for(peer_rank), dst_ref=out_ref.at[my_rank], send_sem=send_sems.at[k],
                recv_sem=recv_sems.at[k], device_id=peer, device_id_type=pl.DeviceIdType.MESH)
            send.start()
            sends.append((send, peer, peer_rank))
        for k, (send, peer, peer_rank) in enumerate(sends):
            pltpu.make_async_remote_copy(
                src_ref=mine_for(peer_rank), dst_ref=out_ref.at[peer_rank], send_sem=send_sems.at[k],
                recv_sem=recv_sems.at[k], device_id=peer, device_id_type=pl.DeviceIdType.MESH).wait_recv()
        for send, _, _ in sends:
            send.wait_send()
        local.wait()

    return pl.pallas_call(
        body, name=name, in_specs=[pl.BlockSpec(memory_space=pl.ANY)], out_specs=pl.BlockSpec(memory_space=pl.ANY),
        out_shape=jax.ShapeDtypeStruct((n,) + tuple(shape), src.dtype),
        scratch_shapes=[pltpu.SemaphoreType.DMA((n - 1,)), pltpu.SemaphoreType.DMA((n - 1,)), pltpu.SemaphoreType.DMA(())],
    )(src)


CHUNK_BYTES = 1 << 20
MIN_CHUNK_ROWS = 64


def _row_parts(rows, cols, itemsize):
    n = 1
    while rows % (2 * n) == 0 and rows // (2 * n) >= MIN_CHUNK_ROWS and rows * cols * itemsize > n * CHUNK_BYTES:
        n *= 2
    return n


def _remote(src, dst, send_sem, recv_sem, peer):
    return pltpu.make_async_remote_copy(src_ref=src, dst_ref=dst, send_sem=send_sem, recv_sem=recv_sem,
                                        device_id=peer, device_id_type=pl.DeviceIdType.MESH)


def _start_rows(src, dst, send_sem, recv_sem, peer):
    rows, cols = src.shape
    n = _row_parts(rows, cols, jnp.dtype(src.dtype).itemsize)
    pr = rows // n
    for i in range(n):
        _remote(src.at[pl.ds(i * pr, pr), :], dst.at[pl.ds(i * pr, pr), :], send_sem, recv_sem, peer).start()


def _mesh_place():
    x, y, c = lax.axis_index("x"), lax.axis_index("y"), lax.axis_index("c")
    other_chips = ((1 - x, y), (x, 1 - y), (1 - x, 1 - y))
    return x, y, c, 2 * x + y, (x, y, 1 - c), other_chips


def _comm_call(body, name, ins, out_shapes, sem_counts):
    any_spec = pl.BlockSpec(memory_space=pl.ANY)
    return pl.pallas_call(
        body, name=name, in_specs=[any_spec] * len(ins), out_specs=[any_spec] * len(out_shapes), out_shape=out_shapes,
        scratch_shapes=[pltpu.SemaphoreType.DMA((n,)) for n in sem_counts])(*ins)


class _ChipSends:
    def __init__(self, srcs, pick):
        self.srcs, self.pick = list(srcs), pick
        nt = len(self.srcs)
        self.n_sems = 3 * nt
        self.out_shapes = []
        for s in self.srcs:
            r2 = s.shape[-2] // 2 if pick is _pick_weight_half else s.shape[-2]
            self.out_shapes.append(jax.ShapeDtypeStruct((3, r2, s.shape[-1]), s.dtype))

    def start(self, ins, outs, send_sems, recv_sems):
        _, _, c, _, _, other_chips = _mesh_place()
        for t in range(len(ins)):
            for j, (px, py) in enumerate(other_chips):
                _remote(self.pick(ins[t], c, 2 * px + py), outs[t].at[j], send_sems.at[3 * t + j], recv_sems.at[3 * t + j],
                        (px, py, c)).start()

    def finish(self, ins, outs, send_sems, recv_sems):
        _, _, c, _, _, other_chips = _mesh_place()
        for t in range(len(ins)):
            for j, (px, py) in enumerate(other_chips):
                landed = _remote(outs[t].at[j], outs[t].at[j], send_sems.at[3 * t + j], recv_sems.at[3 * t + j], (px, py, c))
                landed.wait_recv()
                landed.wait_send()

    def run(self, name):
        nt = len(self.srcs)

        def body(*refs):
            ins, outs = refs[:nt], refs[nt:2 * nt]
            self.start(ins, outs, *refs[2 * nt:])
            self.finish(ins, outs, *refs[2 * nt:])

        return _comm_call(body, name, self.srcs, self.out_shapes, (self.n_sems, self.n_sems))

    def specs(self):
        any_spec = pl.BlockSpec(memory_space=pl.ANY)
        return [any_spec] * len(self.srcs), [any_spec] * len(self.srcs), [pltpu.SemaphoreType.DMA((self.n_sems,))] * 2


def _pick_weight_half(src, core, chip):
    r2 = src.shape[0] // 2
    return src.at[pl.ds(pl.multiple_of(core * r2, MIN_CHUNK_ROWS), r2), :]


def _pick_slab(src, core, chip):
    return src.at[chip if src.shape[0] == N_CHIPS else 0]


def _swap_halves(gs):
    nt = len(gs)

    def body(*refs):
        ins, outs = refs[:nt], refs[nt:2 * nt]
        send_sems, recv_sems = refs[2 * nt:]
        _, _, c, _, sibling, _ = _mesh_place()
        for t in range(nt):
            for q in range(ins[t].shape[1]):
                _start_rows(ins[t].at[1 - c, q], outs[t].at[q], send_sems.at[t], recv_sems.at[t], sibling)
        for t in range(nt):
            _remote(outs[t], outs[t], send_sems.at[t], recv_sems.at[t], sibling).wait_recv()
        for t in range(nt):
            _remote(outs[t], outs[t], send_sems.at[t], recv_sems.at[t], sibling).wait_send()

    out_shapes = [jax.ShapeDtypeStruct(g.shape[1:], g.dtype) for g in gs]
    return _comm_call(body, "swap_halves", gs, out_shapes, (nt, nt))


def _swap_cores(halves):
    nt = len(halves)

    def body(*refs):
        ins, outs = refs[:nt], refs[nt:2 * nt]
        send_sems, recv_sems = refs[2 * nt:]
        _, _, _, _, sibling, _ = _mesh_place()
        for t in range(nt):
            for li in range(ins[t].shape[0]):
                _start_rows(ins[t].at[li], outs[t].at[li], send_sems.at[t], recv_sems.at[t], sibling)
        for t in range(nt):
            _remote(outs[t], outs[t], send_sems.at[t], recv_sems.at[t], sibling).wait_recv()
        for t in range(nt):
            _remote(ins[t], ins[t], send_sems.at[t], recv_sems.at[t], sibling).wait_send()

    out_shapes = [jax.ShapeDtypeStruct(h.shape, h.dtype) for h in halves]
    return _comm_call(body, "swap_cores", halves, out_shapes, (nt, nt))


def _pack_rows(pieces, cols, dtype):
    tile = SUBLANES * cols
    rows = []
    for p in pieces:
        flat = p.reshape(-1).astype(dtype)
        rows.append(jnp.pad(flat, (0, (-flat.shape[0]) % tile)).reshape(-1, cols))
    total = sum(r.shape[0] for r in rows)
    if total % (2 * SUBLANES):
        rows.append(jnp.zeros((SUBLANES, cols), dtype))
    return jnp.concatenate(rows, axis=0)


def _unpack_rows(buf, shapes):
    out, row = [], 0
    for s in shapes:
        n = math.prod(s)
        nrows = -(-n // (SUBLANES * buf.shape[1])) * SUBLANES
        out.append(buf[row:row + nrows].reshape(-1)[:n].reshape(s))
        row += nrows
    return out


ROW_SHARDED = ('w_out', 'w_glu')


def _chips_to_full(name, chips):
    mats = [c.reshape(-1, c.shape[-1]) for c in chips]
    if name in ROW_SHARDED:
        return jnp.concatenate(mats, axis=0)
    if name != 'w_in':
        return jnp.concatenate(mats, axis=1)
    cs = mats[0].shape[1]
    pieces = []
    for start, size in _PERM_PIECES:
        lo = start
        while lo < start + size:
            q = lo // cs
            hi = min(start + size, (q + 1) * cs)
            pieces.append(mats[q][:, lo - q * cs:hi - q * cs])
            lo = hi
    return jnp.concatenate(pieces, axis=1)


def _full_to_dests(name, full):
    rows, cols = full.shape
    if name in ROW_SHARDED:
        return jnp.transpose(full.reshape(N_CHIPS, 2, rows // (2 * N_CHIPS), cols), (1, 0, 2, 3))
    cs = cols // N_CHIPS
    if name != 'w_in':
        dests = [full[:, q * cs:(q + 1) * cs] for q in range(N_CHIPS)]
    else:
        offsets, off = [], 0
        for _, size in _PERM_PIECES:
            offsets.append(off)
            off += size
        by_start = sorted(zip(_PERM_PIECES, offsets))
        dests = []
        for q in range(N_CHIPS):
            parts = []
            for (start, size), at in by_start:
                lo, hi = max(start, q * cs), min(start + size, (q + 1) * cs)
                if lo < hi:
                    parts.append(full[:, at + lo - start:at + hi - start])
            dests.append(jnp.concatenate(parts, axis=1))
    return jnp.stack([d.reshape(2, rows // 2, cs) for d in dests], axis=1)


def _by_chip(own, related, x, y):
    grid = ((own, related[1]), (related[0], related[2]))
    along_x = [[jnp.where(x == 0, grid[px][dy], grid[1 - px][dy]) for dy in range(2)] for px in range(2)]
    return [jnp.where(y == 0, along_x[px][py], along_x[px][1 - py]) for px in range(2) for py in range(2)]


def _join_halves(mine, theirs, core, axis):
    return jnp.where(core == 0, jnp.concatenate([mine, theirs], axis=axis), jnp.concatenate([theirs, mine], axis=axis))


SSM_RAW = ('ssm_a_re', 'ssm_a_im', 'ssm_log_dt', 'ssm_b_re', 'ssm_b_im', 'ssm_c_re', 'ssm_c_im')


def _all_tables(p):
    raw = tuple(p[k] for k in SSM_RAW)
    (_, bb, cb), vjp = jax.vjp(jax.vmap(_ssm_tables), *raw)
    return (bb.astype(BF16), cb.astype(BF16), jax.vmap(_ssm_powers)(*raw[:3])), vjp


def _layer_fwd(x, p, full, li, mod, tables, sends=None):
    bb, cb, pw = (t[li] for t in tables)
    h = _norm_fwd(x, _row(p['norm_g'][li]), mod)
    proj = _matmul(h, full['w_in'], 'nn', tm=512, tn=1280, tk=1024, name="proj", n_outer=True)
    o = _attention_fwd(proj, p['attn_sinks'][li])
    y_lin, states, brought = _ssm_fwd(proj, bb, cb, pw, sends)
    y_pool = _pool_fwd(proj, p['w_pool'][li].astype(BF16), _row(p['pool_scale'][li]))
    x_new = _merge_fwd(proj, o, y_lin, y_pool, x, _row(p['ssm_d'][li]), full['w_glu'], _row(p['b_glu'][li]),
                       full['w_br_att'], full['w_br_ssm'], full['w_br_pool'], full['w_out'], mod)
    saved = dict(x=x, h=h, proj=proj, o=o, y_lin=y_lin, y_pool=y_pool, states=states, bb=bb, cb=cb, pw=pw)
    return x_new, saved, brought


def _layer_bwd(dxo, s, p, full, li, mod, sends=None):
    l = dxo.shape[0]
    proj = s['proj']
    (do, dza, dyl, dus_skip, dzs, dyp, dzp, dgl, ya, ys, ypl, y2, dt, dpa, dps, dpp, mg, dout, st) = _merge_bwd(
        proj, s['o'], s['y_lin'], s['y_pool'], dxo, _row(p['ssm_d'][li]), full['w_glu'], _row(p['b_glu'][li]),
        full['w_br_att'], full['w_br_ssm'], full['w_br_pool'], full['w_out'], mod)
    g = {}
    g['w_br_att'] = _matmul(ya, dpa, 'tn', tm=512, tn=1024, tk=512, name="grad_w_br")
    g['w_br_ssm'] = _matmul(ys, dps, 'tn', tm=512, tn=1024, tk=512, name="grad_w_br")
    g['w_br_pool'] = _matmul(ypl, dpp, 'tn', tm=512, tn=1024, tk=512, name="grad_w_br")
    g['w_out'] = _matmul(mg, dout, 'tn', tm=512, tn=1024, tk=512, name="grad_w_out")
    g['w_glu'] = _matmul(y2, dt, 'tn', tm=512, tn=512, tk=512, name="grad_w_glu")
    g['b_glu'] = st[1, 0:SSM_W]
    g['ssm_d'] = st[1, SSM_W:2 * SSM_W]
    dgate = st[0]

    dq, dkv, dsink = _attention_bwd(proj, p['attn_sinks'][li], do)
    g['attn_sinks'] = dsink[:, 0]
    dus_scan, dbb, dcb, dlam, brought = _ssm_bwd(proj, s['states'], dyl, s['bb'], s['cb'], s['pw'], sends)
    g['ssm_tables'] = (dlam[:, 0:2, :], dbb, dcb)
    dup, dwp, dps_scale = _pool_bwd(proj, p['w_pool'][li].astype(BF16), _row(p['pool_scale'][li]), dyp)
    g['w_pool'] = dwp
    g['pool_scale'] = dps_scale[0]

    dproj = jnp.concatenate([piece.astype(BF16) for piece in
                             (dgl, dq, dza, dus_skip + dus_scan, dzs, dup, dzp, dkv[HALO:])], axis=1)
    g['w_in'] = _matmul(s['h'], dproj, 'tn', tm=1024, tn=1280, tk=512, name="grad_w_in")
    dh = _matmul(dproj, full['w_in'], 'nt', tm=1024, tn=1024, tk=1280, name="grad_h")
    dx, nst = _norm_bwd(s['x'], dh, dxo, _row(p['norm_g'][li]), mod)
    g['norm_g'] = nst[2]
    dmod = jnp.concatenate([nst[0], nst[1], dgate])
    del l
    return dx, g, dmod, brought


def _weight_sends(shards, li):
    return _ChipSends([shards[name][li] for name, _ in BIG], _pick_weight_half)


def _assemble_layer(shards, li, over_ici, place):
    x, y, core, _ = place
    from_sibling = _swap_cores(over_ici)
    full = {}
    for (name, _), a, b in zip(BIG, over_ici, from_sibling):
        own = shards[name][li]
        own_halves = own.reshape(2, own.shape[0] // 2, own.shape[1])
        halves = [_join_halves(a[j][None], b[j][None], core, 0) for j in range(3)]
        full[name] = _chips_to_full(name, _by_chip(own_halves, halves, x, y))
    return full


def _pair_sums(tensors, wire, core):
    theirs = _swap_halves(tensors)
    return [_add2(lax.dynamic_index_in_dim(g, core, 0, keepdims=False), a, dt, "sum_core_pair")
            for g, a, dt in zip(tensors, theirs, wire)]


def _layer_grad_halves(layer_grads):
    return [_full_to_dests(name, layer_grads[name]) for name, _ in BIG]


def _step(p, m, v, x, c, target):
    depth = p['norm_g'].shape[0]
    d = D_MODEL
    ix, iy, ic = lax.axis_index("x"), lax.axis_index("y"), lax.axis_index("c")
    chip = 2 * ix + iy
    dev = 4 * ix + 2 * iy + ic
    x0 = x[0]

    c_pad = jnp.pad(c, ((0, SUBLANES - 1), (0, 0)))
    c_all = _exchange(c_pad, 'xyc', False, "gather_c")[:, 0, :]
    n_ada = p['w_ada'].shape[-1]
    b_shard = lax.dynamic_slice_in_dim(p['b_ada'], chip * n_ada, n_ada, axis=1)[:, None, :]
    mod_shard = _ada_fwd(c_all, p['w_ada'].astype(BF16), b_shard)
    mod_all = _exchange(mod_shard.reshape(depth * N_DEV, n_ada), 'xy', False, "gather_mod")
    mod_all = jnp.transpose(mod_all.reshape(N_CHIPS, depth, N_DEV, n_ada), (1, 2, 0, 3)).reshape(depth, N_DEV, 3 * d)
    mods = lax.dynamic_index_in_dim(mod_all, dev, axis=1, keepdims=True)

    place = (ix, iy, ic, chip)
    n_big = len(BIG)

    shards = {name: p[name].astype(BF16) for name, _ in BIG}
    tables, tables_vjp = _all_tables(p)
    over_ici = _weight_sends(shards, 0).run("send_weight_halves")
    fulls, saved = [], []
    xs = x0
    for li in range(depth):
        fulls.append(_assemble_layer(shards, li, over_ici, place))
        sends = _weight_sends(shards, li + 1) if li + 1 < depth else None
        xs, s, over_ici = _layer_fwd(xs, p, fulls[li], li, mods[li], tables, sends)
        saved.append(s)
    dx, hst = _loss_head(xs, _row(p['final_g']), target[0])
    loss = lax.psum(hst[1, 0], ("x", "y", "c"))

    grads, dmods, pair, from_chips = [None] * depth, [None] * depth, [None] * depth, [None] * depth
    sends = None
    for li in reversed(range(depth)):
        dx, grads[li], dmods[li], brought = _layer_bwd(dx, saved[li], p, fulls[li], li, mods[li], sends)
        if sends is not None:
            from_chips[li + 1] = brought
        if li > 0:
            pair[li] = _pair_sums(_layer_grad_halves(grads[li]), [BF16] * n_big, ic)
            sends = _ChipSends(pair[li], _pick_slab)

    dmod_pad = jnp.pad(jnp.stack(dmods), ((0, SUBLANES - depth), (0, 0)))
    dmod_all = _exchange(dmod_pad, 'xyc', False, "gather_dmod")[:, :depth, :]
    dmod_cols = lax.dynamic_slice_in_dim(jnp.transpose(dmod_all, (1, 0, 2)), chip * n_ada, n_ada, axis=2)
    g_w_ada = _ada_bwd(jnp.transpose(c_all), dmod_cols)

    local_small = {k: jnp.stack([grads[li][k] for li in range(depth)])
                   for k in SMALL if k not in ('final_g', 'b_ada') + SSM_RAW}
    table_cotangents = tuple(jnp.stack([grads[li]['ssm_tables'][k] for li in range(depth)]) for k in range(3))
    local_small.update(zip(SSM_RAW, tables_vjp(table_cotangents)))
    local_small['final_g'] = hst[0]
    local_small['b_ada'] = jnp.stack(dmods)
    small_pack = _pack_rows([local_small[k] for k in SMALL], PACK_COLS, F32)

    small_halves = small_pack.reshape(2, 1, small_pack.shape[0] // 2, small_pack.shape[1])
    pair[0] = _pair_sums(_layer_grad_halves(grads[0]) + [small_halves], [BF16] * n_big + [F32], ic)
    from_chips[0] = _ChipSends(pair[0], _pick_slab).run("scatter_chips")
    halves = []
    for li in range(depth):
        for t, (ps, others) in enumerate(zip(pair[li], from_chips[li])):
            small = t == n_big
            own = ps[0] if small else lax.dynamic_index_in_dim(ps, chip, 0, keepdims=False)
            halves.append(_sum_chips(own, others, small, "sum_chips_small" if small else "sum_chips")[None])
    totals = [_join_halves(h[0], o[0], ic, 0) for h, o in zip(halves, _swap_cores(halves))]
    small_sum = totals[n_big]
    per_layer = [totals[:n_big]] + [totals[n_big + 1 + (li - 1) * n_big:n_big + 1 + li * n_big] for li in range(1, depth)]

    small_shapes = [p[k].shape for k in SMALL]
    grad = dict(zip(SMALL, _unpack_rows(small_sum, small_shapes)))
    grad['w_ada'] = g_w_ada
    for t, (name, _) in enumerate(BIG):
        grad[name] = jnp.stack([per_layer[li][t] for li in range(depth)])

    delta, new_m, new_v = {}, {}, {}
    outs = _adamw(_pack_rows([p[k] for k in SMALL], PACK_COLS, F32), small_sum,
                  _pack_rows([m[k] for k in SMALL], PACK_COLS, F32), _pack_rows([v[k] for k in SMALL], PACK_COLS, F32),
                  name="adamw_small")
    for res, o in zip((delta, new_m, new_v), outs):
        res.update(zip(SMALL, _unpack_rows(o, small_shapes)))
    for name in ['w_ada'] + [n for n, _ in BIG]:
        shape = p[name].shape
        two_d = (-1, shape[-1])
        outs = _adamw(p[name].reshape(two_d), grad[name].reshape(two_d), m[name].reshape(two_d), v[name].reshape(two_d),
                      name="adamw_" + name)
        delta[name], new_m[name], new_v[name] = (o.reshape(shape) for o in outs)

    return (loss, dx[None], *[grad[k] for k in WEIGHTS], *[delta[k] for k in WEIGHTS],
            *[new_m[k] for k in WEIGHTS], *[new_v[k] for k in WEIGHTS])


def kernel(x, c, norm_g, w_ada, b_ada, w_in, attn_sinks, ssm_a_re, ssm_a_im, ssm_log_dt, ssm_b_re, ssm_b_im, ssm_c_re, ssm_c_im, ssm_d, w_glu, b_glu, w_pool, pool_scale, w_br_att, w_br_ssm, w_br_pool, w_out, final_g, loss_target, m_norm_g, m_w_ada, m_b_ada, m_w_in, m_attn_sinks, m_ssm_a_re, m_ssm_a_im, m_ssm_log_dt, m_ssm_b_re, m_ssm_b_im, m_ssm_c_re, m_ssm_c_im, m_ssm_d, m_w_glu, m_b_glu, m_w_pool, m_pool_scale, m_w_br_att, m_w_br_ssm, m_w_br_pool, m_w_out, m_final_g, v_norm_g, v_w_ada, v_b_ada, v_w_in, v_attn_sinks, v_ssm_a_re, v_ssm_a_im, v_ssm_log_dt, v_ssm_b_re, v_ssm_b_im, v_ssm_c_re, v_ssm_c_im, v_ssm_d, v_w_glu, v_b_glu, v_w_pool, v_pool_scale, v_w_br_att, v_w_br_ssm, v_w_br_pool, v_w_out, v_final_g):
    p = dict(zip(WEIGHTS, (norm_g, w_ada, b_ada, w_in, attn_sinks, ssm_a_re, ssm_a_im, ssm_log_dt, ssm_b_re, ssm_b_im,
                           ssm_c_re, ssm_c_im, ssm_d, w_glu, b_glu, w_pool, pool_scale, w_br_att, w_br_ssm, w_br_pool,
                           w_out, final_g)))
    m = dict(zip(WEIGHTS, (m_norm_g, m_w_ada, m_b_ada, m_w_in, m_attn_sinks, m_ssm_a_re, m_ssm_a_im, m_ssm_log_dt,
                           m_ssm_b_re, m_ssm_b_im, m_ssm_c_re, m_ssm_c_im, m_ssm_d, m_w_glu, m_b_glu, m_w_pool,
                           m_pool_scale, m_w_br_att, m_w_br_ssm, m_w_br_pool, m_w_out, m_final_g)))
    v = dict(zip(WEIGHTS, (v_norm_g, v_w_ada, v_b_ada, v_w_in, v_attn_sinks, v_ssm_a_re, v_ssm_a_im, v_ssm_log_dt,
                           v_ssm_b_re, v_ssm_b_im, v_ssm_c_re, v_ssm_c_im, v_ssm_d, v_w_glu, v_b_glu, v_w_pool,
                           v_pool_scale, v_w_br_att, v_w_br_ssm, v_w_br_pool, v_w_out, v_final_g)))
    return _step(p, m, v, x, c, loss_target)
```

```python
import functools
import math

import jax
import jax.numpy as jnp
from jax import lax
from jax.experimental import pallas as pl
from jax.experimental.pallas import tpu as pltpu

F32 = jnp.float32
BF16 = jnp.bfloat16

D_MODEL = 1024
CHUNK = 64
N_HEADS = 8
N_KV_HEADS = 2
HEAD_DIM = 64
Q_PER_KV = N_HEADS // N_KV_HEADS
HALO = 128
ATT_W = 512
KV_W = 128
SSM_W = 512
SSM_GROUP = 16
SSM_GROUPS = 32
SSM_STATE = 64
POOL_W = 512
POOL_GW = 128
IN_W = 6400
EPS = 1e-6
NEG_INF = -1e30
ADAM_LR = 0.001
ADAM_B1 = 0.9
ADAM_B2 = 0.999
ADAM_EPS = 1e-08
ADAM_WD = 0.01
ADAM_STEP = 10

OFF_GL, OFF_Q, OFF_ZA, OFF_US, OFF_ZS, OFF_UP, OFF_ZP, OFF_KV = 0, 3072, 3584, 4096, 4608, 5120, 5632, 6144
_PERM_PIECES = ((3328, 3072), (0, 512), (1792, 512), (768, 512), (2304, 512), (1280, 512), (2816, 512), (512, 256))

LANES = 128
SUBLANES = 8
N_SBLK = SSM_GROUPS * SSM_STATE // LANES
VMEM_LIMIT = 48 * 1024 * 1024

N_CHIPS = 4
N_DEV = 8

WEIGHTS = ['norm_g', 'w_ada', 'b_ada', 'w_in', 'attn_sinks', 'ssm_a_re', 'ssm_a_im', 'ssm_log_dt', 'ssm_b_re',
           'ssm_b_im', 'ssm_c_re', 'ssm_c_im', 'ssm_d', 'w_glu', 'b_glu', 'w_pool', 'pool_scale', 'w_br_att',
           'w_br_ssm', 'w_br_pool', 'w_out', 'final_g']
SMALL = ['norm_g', 'b_ada', 'attn_sinks', 'ssm_a_re', 'ssm_a_im', 'ssm_log_dt', 'ssm_b_re', 'ssm_b_im', 'ssm_c_re',
         'ssm_c_im', 'ssm_d', 'b_glu', 'w_pool', 'pool_scale', 'final_g']
BIG = (('w_in', (1024, 1600)), ('w_br_att', (512, 256)), ('w_br_ssm', (512, 256)), ('w_br_pool', (512, 256)),
       ('w_out', (256, 1024)), ('w_glu', (128, 512)))
PACK_COLS = 1024


def _params(sem=None):
    return pltpu.CompilerParams(dimension_semantics=sem, vmem_limit_bytes=VMEM_LIMIT)


def _row(v):
    return v.reshape(1, -1)


def _full(shape):
    nd = len(shape)
    return pl.BlockSpec(shape, lambda *_: (0,) * nd)


def _sigmoid(v):
    return 1.0 / (1.0 + jnp.exp(-v))


def _silu_and_grad(z):
    s = _sigmoid(z)
    return z * s, s * (1.0 + z * (1.0 - s))


_GELU_K = math.sqrt(2.0 / math.pi)


def _gelu_and_grad(v):
    inner = _GELU_K * (v + 0.044715 * v * v * v)
    th = jnp.tanh(inner)
    val = 0.5 * v * (1.0 + th)
    grad = 0.5 * (1.0 + th) + 0.5 * v * (1.0 - th * th) * _GELU_K * (1.0 + 3 * 0.044715 * v * v)
    return val, grad


_NN = (((1,), (0,)), ((), ()))
_NT = (((1,), (1,)), ((), ()))
_TN = (((0,), (0,)), ((), ()))


def _dot(a, b, dims=_NN):
    return lax.dot_general(a.astype(BF16), b.astype(BF16), dims, preferred_element_type=F32)


def _matmul(a, b, mode, *, tm, tn, tk, name, out_dtype=F32, n_outer=False, sends=None):
    if mode == 'nn':
        (m, k), (_, n) = a.shape, b.shape
    elif mode == 'nt':
        (m, k), (n, _) = a.shape, b.shape
    else:
        (k, m), (_, n) = a.shape, b.shape
    tm, tn, tk = min(tm, m), min(tn, n), min(tk, k)
    assert m % tm == 0 and n % tn == 0 and k % tk == 0, (name, a.shape, b.shape)
    nk = k // tk
    dims = {'nn': _NN, 'nt': _NT, 'tn': _TN}[mode]

    def body(a_ref, b_ref, o_ref, acc_ref):
        if nk == 1:
            o_ref[...] = _dot(a_ref[...], b_ref[...], dims).astype(out_dtype)
            return
        kk = pl.program_id(2)

        @pl.when(kk == 0)
        def _():
            acc_ref[...] = jnp.zeros_like(acc_ref)

        acc_ref[...] += _dot(a_ref[...], b_ref[...], dims)

        @pl.when(kk == nk - 1)
        def _():
            o_ref[...] = acc_ref[...].astype(out_dtype)

    def spec(shape, index):
        if n_outer:
            return pl.BlockSpec(shape, lambda j, i, kk: index(i, j, kk))
        return pl.BlockSpec(shape, index)

    a_spec = spec((tk, tm), lambda i, j, kk: (kk, i)) if mode == 'tn' else spec((tm, tk), lambda i, j, kk: (i, kk))
    b_spec = spec((tn, tk), lambda i, j, kk: (j, kk)) if mode == 'nt' else spec((tk, tn), lambda i, j, kk: (kk, j))
    grid = (n // tn, m // tm, nk) if n_outer else (m // tm, n // tn, nk)
    body, x_in, x_out, x_shapes, x_scratch, x_args = _carry_sends(body, 2, 1, 1, grid, sends)
    semantics = ("parallel", "parallel", "arbitrary") if sends is None else ("arbitrary",) * 3
    outs = pl.pallas_call(
        body, name=name, grid=grid, in_specs=[a_spec, b_spec] + x_in,
        out_specs=[spec((tm, tn), lambda i, j, kk: (i, j))] + x_out,
        out_shape=[jax.ShapeDtypeStruct((m, n), out_dtype)] + x_shapes,
        scratch_shapes=[pltpu.VMEM((tm, tn), F32)] + x_scratch, compiler_params=_params(semantics))(a, b, *x_args)
    return outs[0] if sends is None else (outs[0], list(outs[1:]))


ROWS_NORM = 512


def _norm_fwd(x, g, mod):
    l, d = x.shape
    tr = min(ROWS_NORM, l)

    def body(x_ref, g_ref, mod_ref, h_ref):
        xv = x_ref[...]
        r = lax.rsqrt(jnp.mean(xv * xv, axis=-1, keepdims=True) + EPS)
        shift, scale = mod_ref[:, 0:d], mod_ref[:, d:2 * d]
        h_ref[...] = ((xv * r * g_ref[...]) * (1.0 + scale) + shift).astype(BF16)

    return pl.pallas_call(
        body, name="norm_fwd", grid=(l // tr,),
        in_specs=[pl.BlockSpec((tr, d), lambda i: (i, 0)), _full((1, d)), _full((1, 3 * d))],
        out_specs=pl.BlockSpec((tr, d), lambda i: (i, 0)), out_shape=jax.ShapeDtypeStruct((l, d), BF16),
        compiler_params=_params(("parallel",)))(x, g, mod)


def _norm_bwd(x, dh, dxo, g, mod):
    l, d = x.shape
    tr = min(ROWS_NORM, l)

    def body(x_ref, dh_ref, dxo_ref, g_ref, mod_ref, dx_ref, st_ref):
        @pl.when(pl.program_id(0) == 0)
        def _():
            st_ref[...] = jnp.zeros_like(st_ref)

        xv, dhv = x_ref[...], dh_ref[...]
        r = lax.rsqrt(jnp.mean(xv * xv, axis=-1, keepdims=True) + EPS)
        xn = xv * r
        gv = g_ref[...]
        sc1 = 1.0 + mod_ref[:, d:2 * d]
        dxn = dhv * gv * sc1
        dx_ref[...] = dxo_ref[...] + r * (dxn - xn * jnp.mean(dxn * xn, axis=-1, keepdims=True))
        st_ref[0:1, :] += jnp.sum(dhv, axis=0, keepdims=True)
        st_ref[1:2, :] += jnp.sum(dhv * xn * gv, axis=0, keepdims=True)
        st_ref[2:3, :] += jnp.sum(dhv * xn * sc1, axis=0, keepdims=True)

    blk = pl.BlockSpec((tr, d), lambda i: (i, 0))
    return pl.pallas_call(
        body, name="norm_bwd", grid=(l // tr,),
        in_specs=[blk, blk, blk, _full((1, d)), _full((1, 3 * d))],
        out_specs=[blk, _full((SUBLANES, d))],
        out_shape=[jax.ShapeDtypeStruct((l, d), F32), jax.ShapeDtypeStruct((SUBLANES, d), F32)],
        compiler_params=_params(("arbitrary",)))(x, dh, dxo, g, mod)


def _loss_head(x, g, target):
    l, d = x.shape
    tr = min(ROWS_NORM, l)

    def body(x_ref, g_ref, t_ref, dx_ref, st_ref):
        @pl.when(pl.program_id(0) == 0)
        def _():
            st_ref[...] = jnp.zeros_like(st_ref)

        xv = x_ref[...]
        r = lax.rsqrt(jnp.mean(xv * xv, axis=-1, keepdims=True) + EPS)
        xn = xv * r
        gv = g_ref[...]
        err = xn * gv - t_ref[...]
        part = 0.5 * jnp.sum(jnp.mean(err * err, axis=-1, keepdims=True), axis=0, keepdims=True)
        dy = err * (1.0 / d)
        dxn = dy * gv
        dx_ref[...] = r * (dxn - xn * jnp.mean(dxn * xn, axis=-1, keepdims=True))
        st_ref[0:1, :] += jnp.sum(dy * xn, axis=0, keepdims=True)
        st_ref[1:2, :] += jnp.broadcast_to(part, (1, d))

    blk = pl.BlockSpec((tr, d), lambda i: (i, 0))
    return pl.pallas_call(
        body, name="loss_head", grid=(l // tr,), in_specs=[blk, _full((1, d)), blk],
        out_specs=[blk, _full((SUBLANES, d))],
        out_shape=[jax.ShapeDtypeStruct((l, d), F32), jax.ShapeDtypeStruct((SUBLANES, d), F32)],
        compiler_params=_params(("arbitrary",)))(x, g, target)


ROWS_ATT = 128
ROWS_ATT_BWD = 256
_SLOPES = tuple(2.0 ** (-8.0 * (h + 1) / N_HEADS) for h in range(N_HEADS))


def _att_mask(i, t):
    r = lax.broadcasted_iota(jnp.int32, (t, t + HALO), 0)
    j = lax.broadcasted_iota(jnp.int32, (t, t + HALO), 1)
    dist = jnp.abs(r + HALO - j).astype(F32)
    rc, jc = r // CHUNK, j // CHUNK
    allowed = (jc >= rc) & (jc <= rc + 2) & ((j >= HALO) | (i > 0))
    return dist, allowed


def _att_probs(qh, k, dist, allowed, slope, sink):
    s = _dot(qh, k, _NT) * (1.0 / math.sqrt(HEAD_DIM)) - slope * dist
    s = jnp.where(allowed, s, NEG_INF)
    m = jnp.maximum(jnp.max(s, axis=1, keepdims=True), sink)
    e = jnp.exp(s - m)
    es = jnp.exp(sink - m)
    den = jnp.sum(e, axis=1, keepdims=True) + es
    return e / den, es / den


def _att_specs(t):
    q_spec = pl.BlockSpec((t, ATT_W), lambda i: (i, OFF_Q // ATT_W))
    kv_spec = pl.BlockSpec((t, 2 * KV_W), lambda i: (i, OFF_KV // (2 * KV_W)))
    halo_spec = pl.BlockSpec((HALO, 2 * KV_W), lambda i: (jnp.maximum(i * (t // HALO) - 1, 0), OFF_KV // (2 * KV_W)))
    return q_spec, kv_spec, halo_spec


def _attention_fwd(proj, sinks):
    l = proj.shape[0]
    t = min(ROWS_ATT, l)

    def body(sink_ref, q_ref, kv_ref, halo_ref, o_ref):
        dist, allowed = _att_mask(pl.program_id(0), t)
        kv = jnp.concatenate([halo_ref[...], kv_ref[...]], axis=0)
        for h in range(N_HEADS):
            kh = h // Q_PER_KV
            k = kv[:, kh * HEAD_DIM:(kh + 1) * HEAD_DIM]
            v = kv[:, KV_W + kh * HEAD_DIM:KV_W + (kh + 1) * HEAD_DIM]
            p, _ = _att_probs(q_ref[:, h * HEAD_DIM:(h + 1) * HEAD_DIM], k, dist, allowed, _SLOPES[h], sink_ref[h])
            o_ref[:, h * HEAD_DIM:(h + 1) * HEAD_DIM] = _dot(p, v)

    q_spec, kv_spec, halo_spec = _att_specs(t)
    return pl.pallas_call(
        body, name="attention_fwd", grid=(l // t,),
        in_specs=[pl.BlockSpec(memory_space=pltpu.SMEM), q_spec, kv_spec, halo_spec],
        out_specs=pl.BlockSpec((t, ATT_W), lambda i: (i, 0)), out_shape=jax.ShapeDtypeStruct((l, ATT_W), F32),
        compiler_params=_params(("parallel",)))(sinks, proj, proj, proj)


def _attention_bwd(proj, sinks, do):
    l = proj.shape[0]
    t = min(ROWS_ATT_BWD, l)

    def body(sink_ref, q_ref, kv_ref, halo_ref, do_ref, dq_ref, dkv_ref, dsink_ref):
        i = pl.program_id(0)

        @pl.when(i == 0)
        def _():
            dkv_ref[...] = jnp.zeros_like(dkv_ref)
            dsink_ref[...] = jnp.zeros_like(dsink_ref)

        dist, allowed = _att_mask(i, t)
        kv = jnp.concatenate([halo_ref[...], kv_ref[...]], axis=0)
        rows = pl.ds(pl.multiple_of(i * t, t), t + HALO)
        for kh in range(N_KV_HEADS):
            k = kv[:, kh * HEAD_DIM:(kh + 1) * HEAD_DIM]
            v = kv[:, KV_W + kh * HEAD_DIM:KV_W + (kh + 1) * HEAD_DIM]
            dk = jnp.zeros((t + HALO, HEAD_DIM), F32)
            dv = jnp.zeros((t + HALO, HEAD_DIM), F32)
            for h in range(kh * Q_PER_KV, (kh + 1) * Q_PER_KV):
                qh = q_ref[:, h * HEAD_DIM:(h + 1) * HEAD_DIM]
                doh = do_ref[:, h * HEAD_DIM:(h + 1) * HEAD_DIM]
                p, ps = _att_probs(qh, k, dist, allowed, _SLOPES[h], sink_ref[h])
                dp = _dot(doh, v, _NT)
                delta = jnp.sum(p * dp, axis=1, keepdims=True)
                ds = p * (dp - delta) * (1.0 / math.sqrt(HEAD_DIM))
                dsink_ref[h:h + 1, :] += jnp.broadcast_to(-jnp.sum(ps * delta, axis=0, keepdims=True), (1, LANES))
                dq_ref[:, h * HEAD_DIM:(h + 1) * HEAD_DIM] = _dot(ds, k)
                dk = dk + _dot(ds, qh, _TN)
                dv = dv + _dot(p, doh, _TN)
            dkv_ref[rows, kh * HEAD_DIM:(kh + 1) * HEAD_DIM] += dk
            dkv_ref[rows, KV_W + kh * HEAD_DIM:KV_W + (kh + 1) * HEAD_DIM] += dv

    q_spec, kv_spec, halo_spec = _att_specs(t)
    blk = pl.BlockSpec((t, ATT_W), lambda i: (i, 0))
    return pl.pallas_call(
        body, name="attention_bwd", grid=(l // t,),
        in_specs=[pl.BlockSpec(memory_space=pltpu.SMEM), q_spec, kv_spec, halo_spec, blk],
        out_specs=[blk, _full((HALO + l, 2 * KV_W)), _full((N_HEADS, LANES))],
        out_shape=[jax.ShapeDtypeStruct((l, ATT_W), F32), jax.ShapeDtypeStruct((HALO + l, 2 * KV_W), F32),
                   jax.ShapeDtypeStruct((N_HEADS, LANES), F32)],
        compiler_params=_params(("arbitrary",)))(sinks, proj, proj, proj, do)


ROWS_SSM = 2048
SCAN_UNROLL = 4


def _ssm_discretize(a_re, a_im, log_dt, b_re, b_im):
    lam = lax.complex(a_re, a_im)
    dt = jnp.exp(log_dt)[:, None]
    lam_bar = jnp.exp(lam * dt)
    b_bar = ((lam_bar - 1.0) / lam)[..., None] * lax.complex(b_re, b_im)
    return lam, dt, lam_bar, b_bar


def _ssm_block_diag(m):
    e = m.reshape(N_SBLK, 2, SSM_GROUP, SSM_STATE)
    e = e[:, :, :, None, :] * jnp.eye(2, dtype=m.dtype)[None, :, None, :, None]
    e = e.reshape(N_SBLK, 2 * SSM_GROUP, LANES)
    oh = jax.nn.one_hot(jnp.arange(N_SBLK) % 4, 4, dtype=m.dtype)
    return (oh[:, :, None, None] * e[:, None]).reshape(N_SBLK, LANES, LANES)


def _ssm_tables(a_re, a_im, log_dt, b_re, b_im, c_re, c_im):
    _, _, lam_bar, b_bar = _ssm_discretize(a_re, a_im, log_dt, b_re, b_im)
    lam_blk = jnp.stack([jnp.real(lam_bar).reshape(N_SBLK, LANES), jnp.imag(lam_bar).reshape(N_SBLK, LANES)], axis=1)
    bt = jnp.transpose(b_bar, (0, 2, 1))
    bb = jnp.concatenate([_ssm_block_diag(jnp.real(bt)), _ssm_block_diag(jnp.imag(bt))], axis=2)
    cb = jnp.concatenate([jnp.transpose(_ssm_block_diag(c_re), (0, 2, 1)),
                          jnp.transpose(_ssm_block_diag(-c_im), (0, 2, 1))], axis=1)
    return lam_blk, bb, cb


def _ssm_powers(a_re, a_im, log_dt):
    lam = lax.complex(a_re, a_im)
    dt = jnp.exp(log_dt)[:, None]
    k = jnp.arange(1, SUBLANES + 1, dtype=F32)
    pw = jnp.exp((lam * dt)[None] * k[:, None, None]).reshape(SUBLANES, N_SBLK, LANES)
    pw = jnp.transpose(pw, (1, 0, 2))
    rev = pw[:, ::-1]
    return jnp.concatenate([jnp.real(pw), jnp.imag(pw), jnp.real(rev), jnp.imag(rev)], axis=1)


def _scan_consts(pw_ref, reverse):
    row = lax.broadcasted_iota(jnp.int32, (SUBLANES, LANES), 0)
    sign = -1.0 if reverse else 1.0

    def power(k):
        return (jnp.broadcast_to(pw_ref[0, k - 1:k, :], (SUBLANES, LANES)),
                sign * jnp.broadcast_to(pw_ref[0, SUBLANES + k - 1:SUBLANES + k, :], (SUBLANES, LANES)))

    steps = []
    for d in (1, 2, 4):
        pr, pi = power(d)
        keep = (row < SUBLANES - d) if reverse else (row >= d)
        steps.append((d, jnp.where(keep, pr, 0.0), jnp.where(keep, pi, 0.0)))
    base = 2 * SUBLANES if reverse else 0
    return steps, pw_ref[0, base:base + SUBLANES, :], sign * pw_ref[0, base + SUBLANES:base + 2 * SUBLANES, :]


def _scan_tile(xr, xi, steps, reverse):
    for d, ar, ai in steps:
        shift = SUBLANES - d if reverse else d
        rr, ri = pltpu.roll(xr, shift, 0), pltpu.roll(xi, shift, 0)
        xr, xi = xr + (ar * rr - ai * ri), xi + (ar * ri + ai * rr)
    return xr, xi


def _bcast_row(v, r):
    return jnp.broadcast_to(v[r:r + 1, :], (SUBLANES, LANES))


def _scan_forward(s_scr, pw_ref, n_tiles):
    steps, pr, pi = _scan_consts(pw_ref, False)

    def tiles(g, carry):
        cr, ci = carry
        rows = [pl.ds(pl.multiple_of((g * SCAN_UNROLL + u) * SUBLANES, SUBLANES), SUBLANES) for u in range(SCAN_UNROLL)]
        loaded = [(s_scr[r, 0:LANES], s_scr[r, LANES:2 * LANES]) for r in rows]
        local = [_scan_tile(xr, xi, steps, False) for xr, xi in loaded]
        for r, (xr, xi) in zip(rows, local):
            sr, si = xr + (pr * cr - pi * ci), xi + (pr * ci + pi * cr)
            s_scr[r, 0:LANES] = sr
            s_scr[r, LANES:2 * LANES] = si
            cr, ci = _bcast_row(sr, SUBLANES - 1), _bcast_row(si, SUBLANES - 1)
        return cr, ci

    zero = jnp.zeros((SUBLANES, LANES), F32)
    lax.fori_loop(0, n_tiles // SCAN_UNROLL, tiles, (zero, zero))


def _row_chunks(l):
    rc = min(ROWS_SSM, l)
    return rc, l // rc


def _carry_sends(body, n_in, n_out, n_scratch, grid, sends):
    if sends is None:
        return body, [], [], [], [], []
    k = len(sends.srcs)

    def carrying(*refs):
        ins, send_ins = refs[:n_in], refs[n_in:n_in + k]
        outs, send_outs = refs[n_in + k:n_in + k + n_out], refs[n_in + k + n_out:n_in + 2 * k + n_out]
        rest = refs[n_in + 2 * k + n_out:]
        scratch, sems = rest[:n_scratch], rest[n_scratch:]
        ids = [pl.program_id(axis) for axis in range(len(grid))]
        first = functools.reduce(jnp.logical_and, [i == 0 for i in ids])
        last = functools.reduce(jnp.logical_and, [i == n - 1 for i, n in zip(ids, grid)])

        @pl.when(first)
        def _():
            sends.start(send_ins, send_outs, *sems)

        body(*ins, *outs, *scratch)

        @pl.when(last)
        def _():
            sends.finish(send_ins, send_outs, *sems)

    in_specs, out_specs, scratch = sends.specs()
    return carrying, in_specs, out_specs, sends.out_shapes, scratch, sends.srcs


def _ssm_fwd(proj, bb, cb, pw, sends=None):
    l = proj.shape[0]
    rc, n_chunks = _row_chunks(l)

    def body(u_ref, bb_ref, cb_ref, pw_ref, y_ref, s_scr):
        j = pl.program_id(0)

        def fill(ci, _):
            rows = pl.ds(pl.multiple_of(ci * rc, rc), rc)
            s_scr[rows, :] = _dot(u_ref[rows, :], bb_ref[0])
            return 0

        lax.fori_loop(0, n_chunks, fill, 0)
        _scan_forward(s_scr, pw_ref, l // SUBLANES)

        @pl.when(j % 4 == 0)
        def _():
            y_ref[...] = jnp.zeros_like(y_ref)

        def emit(ci, _):
            rows = pl.ds(pl.multiple_of(ci * rc, rc), rc)
            y_ref[rows, :] += _dot(s_scr[rows, :], cb_ref[0])
            return 0

        lax.fori_loop(0, n_chunks, emit, 0)

    body, x_in, x_out, x_shapes, x_scratch, x_args = _carry_sends(body, 4, 2, 0, (N_SBLK,), sends)
    outs = pl.pallas_call(
        body, name="ssm_fwd" if sends is None else "ssm_fwd_sending", grid=(N_SBLK,),
        in_specs=[pl.BlockSpec((l, LANES), lambda j: (0, OFF_US // LANES + j // 4)),
                  pl.BlockSpec((1, LANES, 2 * LANES), lambda j: (j, 0, 0)),
                  pl.BlockSpec((1, 2 * LANES, LANES), lambda j: (j, 0, 0)),
                  pl.BlockSpec((1, 4 * SUBLANES, LANES), lambda j: (j, 0, 0))] + x_in,
        out_specs=[pl.BlockSpec((l, LANES), lambda j: (0, j // 4)), pl.BlockSpec((l, 2 * LANES), lambda j: (0, j))] + x_out,
        out_shape=[jax.ShapeDtypeStruct((l, SSM_W), F32), jax.ShapeDtypeStruct((l, N_SBLK * 2 * LANES), F32)] + x_shapes,
        scratch_shapes=x_scratch, compiler_params=_params(("arbitrary",)))(proj, bb, cb, pw, *x_args)
    return outs[0], outs[1], list(outs[2:])


def _ssm_bwd(proj, states, dy, bb, cb, pw, sends=None):
    l = proj.shape[0]
    rc, n_chunks = _row_chunks(l)
    n_tiles = l // SUBLANES

    def body(u_ref, s_scr, dy_ref, bb_ref, cb_ref, pw_ref, du_ref, dbb_ref, dcb_ref, dlam_ref, a_scr):
        j = pl.program_id(0)
        dcb_ref[...] = jnp.zeros_like(dcb_ref)

        def through_c(ci, _):
            rows = pl.ds(pl.multiple_of(ci * rc, rc), rc)
            dyv = dy_ref[rows, :]
            dcb_ref[0] += _dot(s_scr[rows, :], dyv, _TN)
            a_scr[rows, :] = _dot(dyv, cb_ref[0], _NT)
            return 0

        lax.fori_loop(0, n_chunks, through_c, 0)

        steps, pr, pi = _scan_consts(pw_ref, True)
        row = lax.broadcasted_iota(jnp.int32, (SUBLANES, LANES), 0)

        last = row == SUBLANES - 1

        def tiles(g, carry):
            cr, ci, accr, acci = carry
            rows = [pl.ds(pl.multiple_of((n_tiles - 1 - (g * SCAN_UNROLL + u)) * SUBLANES, SUBLANES), SUBLANES)
                    for u in range(SCAN_UNROLL)]
            loaded = [(a_scr[r, 0:LANES], a_scr[r, LANES:2 * LANES]) for r in rows]
            states = [(s_scr[r, 0:LANES], s_scr[r, LANES:2 * LANES]) for r in rows]
            local = [_scan_tile(xr, xi, steps, True) for xr, xi in loaded]
            for r, (xr, xi), (sr, si) in zip(rows, local, states):
                gr, gi = xr + (pr * cr - pi * ci), xi + (pr * ci + pi * cr)
                a_scr[r, 0:LANES] = gr
                a_scr[r, LANES:2 * LANES] = gi
                ur = jnp.where(last, cr, pltpu.roll(gr, SUBLANES - 1, 0))
                ui = jnp.where(last, ci, pltpu.roll(gi, SUBLANES - 1, 0))
                accr, acci = accr + (ur * sr + ui * si), acci + (ui * sr - ur * si)
                cr, ci = _bcast_row(gr, 0), _bcast_row(gi, 0)
            return cr, ci, accr, acci

        zero = jnp.zeros((SUBLANES, LANES), F32)
        _, _, accr, acci = lax.fori_loop(0, n_tiles // SCAN_UNROLL, tiles, (zero, zero, zero, zero))
        dlr = jnp.broadcast_to(jnp.sum(accr, axis=0, keepdims=True), (SUBLANES, LANES))
        dli = jnp.broadcast_to(jnp.sum(acci, axis=0, keepdims=True), (SUBLANES, LANES))
        dlam_ref[0] = jnp.where(row == 0, dlr, jnp.where(row == 1, dli, 0.0))

        dbb_ref[...] = jnp.zeros_like(dbb_ref)

        @pl.when(j % 4 == 0)
        def _():
            du_ref[...] = jnp.zeros_like(du_ref)

        def through_b(ci, _):
            rows = pl.ds(pl.multiple_of(ci * rc, rc), rc)
            av = a_scr[rows, :]
            dbb_ref[0] += _dot(u_ref[rows, :], av, _TN)
            du_ref[rows, :] += _dot(av, bb_ref[0], _NT)
            return 0

        lax.fori_loop(0, n_chunks, through_b, 0)

    body, x_in, x_out, x_shapes, x_scratch, x_args = _carry_sends(body, 6, 4, 1, (N_SBLK,), sends)
    outs = pl.pallas_call(
        body, name="ssm_bwd" if sends is None else "ssm_bwd_sending", grid=(N_SBLK,),
        in_specs=[pl.BlockSpec((l, LANES), lambda j: (0, OFF_US // LANES + j // 4)),
                  pl.BlockSpec((l, 2 * LANES), lambda j: (0, j)),
                  pl.BlockSpec((l, LANES), lambda j: (0, j // 4)),
                  pl.BlockSpec((1, LANES, 2 * LANES), lambda j: (j, 0, 0)),
                  pl.BlockSpec((1, 2 * LANES, LANES), lambda j: (j, 0, 0)),
                  pl.BlockSpec((1, 4 * SUBLANES, LANES), lambda j: (j, 0, 0))] + x_in,
        out_specs=[pl.BlockSpec((l, LANES), lambda j: (0, j // 4)),
                   pl.BlockSpec((1, LANES, 2 * LANES), lambda j: (j, 0, 0)),
                   pl.BlockSpec((1, 2 * LANES, LANES), lambda j: (j, 0, 0)),
                   pl.BlockSpec((1, SUBLANES, LANES), lambda j: (j, 0, 0))] + x_out,
        out_shape=[jax.ShapeDtypeStruct((l, SSM_W), F32), jax.ShapeDtypeStruct((N_SBLK, LANES, 2 * LANES), F32),
                   jax.ShapeDtypeStruct((N_SBLK, 2 * LANES, LANES), F32),
                   jax.ShapeDtypeStruct((N_SBLK, SUBLANES, LANES), F32)] + x_shapes,
        scratch_shapes=[pltpu.VMEM((l, 2 * LANES), F32)] + x_scratch,
        compiler_params=_params(("arbitrary",)))(proj, states, dy, bb, cb, pw, *x_args)
    return outs[0], outs[1], outs[2], outs[3], list(outs[4:])


def _pool_windows(v, t, l, ahead):
    def shifted(a, d):
        if ahead:
            return jnp.where(t < l - d, pltpu.roll(a, l - d, 0), 0.0)
        return jnp.where(t >= d, pltpu.roll(a, d, 0), 0.0)

    w2 = v + shifted(v, 1)
    w4 = w2 + shifted(w2, 2)
    w8 = w4 + shifted(w4, 4)
    w16 = w8 + shifted(w8, 8)
    return w2, w4, w8, w16


def _pool_select(g, ws):
    return jnp.where(g == 0, ws[0], jnp.where(g == 1, ws[1], jnp.where(g == 2, ws[2], ws[3])))


def _pool_count(g, t):
    return jnp.minimum(t + 1, jnp.left_shift(2, g)).astype(F32)


def _pool_specs(l):
    return [pl.BlockSpec((l, POOL_GW), lambda g: (0, OFF_UP // POOL_GW + g)),
            pl.BlockSpec((1, POOL_GW, POOL_GW), lambda g: (g, 0, 0)),
            pl.BlockSpec((1, POOL_GW), lambda g: (0, g))]


def _pool_fwd(proj, w_pool, scale):
    l = proj.shape[0]

    def body(u_ref, w_ref, sc_ref, y_ref):
        g = pl.program_id(0)
        t = lax.broadcasted_iota(jnp.int32, (l, 1), 0)
        u = u_ref[...]
        pooled = _pool_select(g, _pool_windows(u, t, l, False)) / _pool_count(g, t) - u
        y_ref[...] = _dot(pooled, w_ref[0]) * sc_ref[...]

    return pl.pallas_call(
        body, name="pool_fwd", grid=(4,), in_specs=_pool_specs(l),
        out_specs=pl.BlockSpec((l, POOL_GW), lambda g: (0, g)), out_shape=jax.ShapeDtypeStruct((l, POOL_W), F32),
        compiler_params=_params(("parallel",)))(proj, w_pool, scale)


def _pool_bwd(proj, w_pool, scale, dy):
    l = proj.shape[0]

    def body(u_ref, w_ref, sc_ref, dy_ref, du_ref, dw_ref, dsc_ref):
        g = pl.program_id(0)
        t = lax.broadcasted_iota(jnp.int32, (l, 1), 0)
        u = u_ref[...]
        cnt = _pool_count(g, t)
        pooled = _pool_select(g, _pool_windows(u, t, l, False)) / cnt - u
        dyv = dy_ref[...]
        dsc_ref[...] = jnp.sum(dyv * _dot(pooled, w_ref[0]), axis=0, keepdims=True)
        dyl = dyv * sc_ref[...]
        dw_ref[0] = _dot(pooled, dyl, _TN)
        dpooled = _dot(dyl, w_ref[0], _NT)
        du_ref[...] = _pool_select(g, _pool_windows(dpooled / cnt, t, l, True)) - dpooled

    return pl.pallas_call(
        body, name="pool_bwd", grid=(4,), in_specs=_pool_specs(l) + [pl.BlockSpec((l, POOL_GW), lambda g: (0, g))],
        out_specs=[pl.BlockSpec((l, POOL_GW), lambda g: (0, g)), pl.BlockSpec((1, POOL_GW, POOL_GW), lambda g: (g, 0, 0)),
                   pl.BlockSpec((1, POOL_GW), lambda g: (0, g))],
        out_shape=[jax.ShapeDtypeStruct((l, POOL_W), F32), jax.ShapeDtypeStruct((4, POOL_GW, POOL_GW), F32),
                   jax.ShapeDtypeStruct((1, POOL_W), F32)],
        compiler_params=_params(("parallel",)))(proj, w_pool, scale, dy)


ROWS_MERGE = 128
ROWS_MERGE_FWD = 256


def _merge_inputs(tr):
    def col(off, w):
        return pl.BlockSpec((tr, w), lambda i: (i, off // w))

    def act(w):
        return pl.BlockSpec((tr, w), lambda i: (i, 0))

    d = D_MODEL
    return ([col(OFF_GL, 3 * d), col(OFF_ZA, ATT_W), col(OFF_US, SSM_W), col(OFF_ZS, SSM_W), col(OFF_ZP, POOL_W),
             act(ATT_W), act(SSM_W), act(POOL_W)]
            + [_full((1, SSM_W)), _full((SSM_W, SSM_W)), _full((1, SSM_W)), _full((ATT_W, d)), _full((SSM_W, d)),
               _full((POOL_W, d)), _full((d, d)), _full((1, 3 * d))])


def _merge_forward_math(gl_ref, za_ref, us_ref, zs_ref, zp_ref, o_ref, yl_ref, yp_ref, d_ref, wg_ref, bg_ref,
                        wba_ref, wbs_ref, wbp_ref):
    d = D_MODEL
    r = {}
    r['sa'], r['dsa'] = _silu_and_grad(za_ref[...])
    r['ss'], r['dss'] = _silu_and_grad(zs_ref[...])
    r['sp'], r['dsp'] = _silu_and_grad(zp_ref[...])
    r['y_att'] = o_ref[...] * r['sa']
    y1 = yl_ref[...] + d_ref[...] * us_ref[...]
    r['y2'], r['dgelu'] = _gelu_and_grad(y1)
    r['sg'] = _sigmoid(_dot(r['y2'], wg_ref[...]) + bg_ref[...])
    r['y3'] = r['y2'] * r['sg']
    r['y_ssm'] = r['y3'] * r['ss']
    r['y_pool'] = yp_ref[...] * r['sp']
    r['g'] = [_sigmoid(gl_ref[:, b * d:(b + 1) * d]) for b in range(3)]
    r['p'] = [_dot(r['y_att'], wba_ref[...]), _dot(r['y_ssm'], wbs_ref[...]), _dot(r['y_pool'], wbp_ref[...])]
    r['merged'] = r['g'][0] * r['p'][0] + r['g'][1] * r['p'][1] + r['g'][2] * r['p'][2]
    return r


def _merge_fwd(proj, o, y_lin, y_pool, x, ssm_d, w_glu, b_glu, wba, wbs, wbp, w_out, mod):
    l, d = x.shape
    tr = min(ROWS_MERGE_FWD, l)

    def body(gl_ref, za_ref, us_ref, zs_ref, zp_ref, o_ref, yl_ref, yp_ref, d_ref, wg_ref, bg_ref, wba_ref, wbs_ref,
             wbp_ref, wo_ref, mod_ref, x_ref, xn_ref):
        r = _merge_forward_math(gl_ref, za_ref, us_ref, zs_ref, zp_ref, o_ref, yl_ref, yp_ref, d_ref, wg_ref, bg_ref,
                                wba_ref, wbs_ref, wbp_ref)
        xn_ref[...] = x_ref[...] + mod_ref[:, 2 * d:3 * d] * _dot(r['merged'], wo_ref[...])

    blk = pl.BlockSpec((tr, d), lambda i: (i, 0))
    return pl.pallas_call(
        body, name="merge_fwd", grid=(l // tr,), in_specs=_merge_inputs(tr) + [blk], out_specs=blk,
        out_shape=jax.ShapeDtypeStruct((l, d), F32), compiler_params=_params(("parallel",)))(
            proj, proj, proj, proj, proj, o, y_lin, y_pool, ssm_d, w_glu, b_glu, wba, wbs, wbp, w_out, mod, x)


def _merge_bwd(proj, o, y_lin, y_pool, dxo, ssm_d, w_glu, b_glu, wba, wbs, wbp, w_out, mod):
    l, d = dxo.shape
    tr = min(ROWS_MERGE, l)

    def body(gl_ref, za_ref, us_ref, zs_ref, zp_ref, o_ref, yl_ref, yp_ref, d_ref, wg_ref, bg_ref, wba_ref, wbs_ref,
             wbp_ref, wo_ref, mod_ref, dxo_ref,
             do_ref, dza_ref, dyl_ref, dus_ref, dzs_ref, dyp_ref, dzp_ref, dgl_ref,
             ya_ref, ys_ref, ypl_ref, y2_ref, dt_ref, dpa_ref, dps_ref, dpp_ref, mg_ref, dout_ref, st_ref):
        @pl.when(pl.program_id(0) == 0)
        def _():
            st_ref[...] = jnp.zeros_like(st_ref)

        r = _merge_forward_math(gl_ref, za_ref, us_ref, zs_ref, zp_ref, o_ref, yl_ref, yp_ref, d_ref, wg_ref, bg_ref,
                                wba_ref, wbs_ref, wbp_ref)
        dxov = dxo_ref[...]
        out = _dot(r['merged'], wo_ref[...])
        st_ref[0:1, :] += jnp.sum(dxov * out, axis=0, keepdims=True)
        dout = dxov * mod_ref[:, 2 * d:3 * d]
        dmerged = _dot(dout, wo_ref[...], _NT)
        dys = []
        for b, (w_ref, dp_ref) in enumerate(((wba_ref, dpa_ref), (wbs_ref, dps_ref), (wbp_ref, dpp_ref))):
            gb = r['g'][b]
            dgl_ref[:, b * d:(b + 1) * d] = (dmerged * r['p'][b] * gb * (1.0 - gb)).astype(BF16)
            dp = dmerged * gb
            dp_ref[...] = dp.astype(BF16)
            dys.append(_dot(dp, w_ref[...], _NT))
        mg_ref[...] = r['merged'].astype(BF16)
        dout_ref[...] = dout.astype(BF16)
        ya_ref[...] = r['y_att'].astype(BF16)
        ys_ref[...] = r['y_ssm'].astype(BF16)
        ypl_ref[...] = r['y_pool'].astype(BF16)
        y2_ref[...] = r['y2'].astype(BF16)
        do_ref[...] = (dys[0] * r['sa']).astype(BF16)
        dza_ref[...] = (dys[0] * o_ref[...] * r['dsa']).astype(BF16)
        dy3 = dys[1] * r['ss']
        dzs_ref[...] = (dys[1] * r['y3'] * r['dss']).astype(BF16)
        dt = dy3 * r['y2'] * r['sg'] * (1.0 - r['sg'])
        dt_ref[...] = dt.astype(BF16)
        dy1 = (dy3 * r['sg'] + _dot(dt, wg_ref[...], _NT)) * r['dgelu']
        dyl_ref[...] = dy1.astype(BF16)
        dus_ref[...] = dy1 * d_ref[...]
        st_ref[1:2, 0:SSM_W] += jnp.sum(dt, axis=0, keepdims=True)
        st_ref[1:2, SSM_W:2 * SSM_W] += jnp.sum(dy1 * us_ref[...], axis=0, keepdims=True)
        dyp_ref[...] = dys[2] * r['sp']
        dzp_ref[...] = (dys[2] * yp_ref[...] * r['dsp']).astype(BF16)

    blk = pl.BlockSpec((tr, d), lambda i: (i, 0))
    half = pl.BlockSpec((tr, ATT_W), lambda i: (i, 0))
    wide = pl.BlockSpec((tr, 3 * d), lambda i: (i, 0))
    sds = jax.ShapeDtypeStruct
    half_types = [BF16, BF16, BF16, F32, BF16, F32, BF16]
    return pl.pallas_call(
        body, name="merge_bwd", grid=(l // tr,), in_specs=_merge_inputs(tr) + [blk],
        out_specs=[half] * 7 + [wide] + [half] * 5 + [blk] * 5 + [_full((SUBLANES, d))],
        out_shape=[sds((l, ATT_W), t) for t in half_types] + [sds((l, 3 * d), BF16)] + [sds((l, ATT_W), BF16)] * 5
        + [sds((l, d), BF16)] * 5 + [sds((SUBLANES, d), F32)],
        compiler_params=_params(("arbitrary",)))(
            proj, proj, proj, proj, proj, o, y_lin, y_pool, ssm_d, w_glu, b_glu, wba, wbs, wbp, w_out, mod, dxo)


def _ada_fwd(c_all, w_ada, b_shard):
    depth, d, n = w_ada.shape

    def body(c_ref, w_ref, b_ref, o_ref):
        act, _ = _silu_and_grad(c_ref[...])
        o_ref[0] = _dot(act, w_ref[0]) + b_ref[0]

    return pl.pallas_call(
        body, name="ada_fwd", grid=(depth,),
        in_specs=[_full((N_DEV, d)), pl.BlockSpec((1, d, n), lambda i: (i, 0, 0)), pl.BlockSpec((1, 1, n), lambda i: (i, 0, 0))],
        out_specs=pl.BlockSpec((1, N_DEV, n), lambda i: (i, 0, 0)), out_shape=jax.ShapeDtypeStruct((depth, N_DEV, n), F32),
        compiler_params=_params(("parallel",)))(c_all, w_ada, b_shard)


def _ada_bwd(c_all_t, dmod):
    d = c_all_t.shape[0]
    depth, _, n = dmod.shape

    def body(c_ref, dm_ref, o_ref):
        act, _ = _silu_and_grad(c_ref[...])
        acc = act[:, 0:1] * dm_ref[0, 0:1, :]
        for b in range(1, N_DEV):
            acc = acc + act[:, b:b + 1] * dm_ref[0, b:b + 1, :]
        o_ref[0] = acc

    return pl.pallas_call(
        body, name="ada_bwd", grid=(depth,),
        in_specs=[_full((d, N_DEV)), pl.BlockSpec((1, N_DEV, n), lambda i: (i, 0, 0))],
        out_specs=pl.BlockSpec((1, d, n), lambda i: (i, 0, 0)), out_shape=jax.ShapeDtypeStruct((depth, d, n), F32),
        compiler_params=_params(("parallel",)))(c_all_t, dmod)


ROWS_FLAT = 256


_RELATION_XOR = (0, 2, 1, 3)


def _sum_chips(own, others, chip_order, name):
    r, c = own.shape
    tr = math.gcd(ROWS_FLAT, r)

    def body(own_ref, oth_ref, o_ref):
        terms = [own_ref[...].astype(F32)] + [oth_ref[k].astype(F32) for k in range(3)]
        if chip_order:
            chip = 2 * lax.axis_index("x") + lax.axis_index("y")
            by_chip = []
            for q in range(N_CHIPS):
                rel = jnp.bitwise_xor(chip, q)
                pick = terms[3]
                for k in (2, 1, 0):
                    pick = jnp.where(rel == _RELATION_XOR[k], terms[k], pick)
                by_chip.append(pick)
            terms = by_chip
        o_ref[...] = ((terms[0] + terms[1]) + terms[2]) + terms[3]

    return pl.pallas_call(
        body, name=name, grid=(r // tr,),
        in_specs=[pl.BlockSpec((tr, c), lambda i: (i, 0)), pl.BlockSpec((3, tr, c), lambda i: (0, i, 0))],
        out_specs=pl.BlockSpec((tr, c), lambda i: (i, 0)), out_shape=jax.ShapeDtypeStruct((r, c), F32),
        compiler_params=_params(("parallel",)))(own, others)


def _add2(a, b, out_dtype, name):
    shape = a.shape
    a, b = a.reshape(-1, shape[-1]), b.reshape(-1, shape[-1])
    r, c = a.shape
    tr = math.gcd(ROWS_FLAT, r)

    def body(a_ref, b_ref, o_ref):
        o_ref[...] = (a_ref[...] + b_ref[...]).astype(out_dtype)

    blk = pl.BlockSpec((tr, c), lambda i: (i, 0))
    return pl.pallas_call(
        body, name=name, grid=(r // tr,), in_specs=[blk, blk], out_specs=blk,
        out_shape=jax.ShapeDtypeStruct((r, c), out_dtype), compiler_params=_params(("parallel",)))(a, b).reshape(shape)


def _adamw(w, g, m, v, name):
    r, c = w.shape
    tr = math.gcd(ROWS_FLAT, r)

    def body(w_ref, g_ref, m_ref, v_ref, d_ref, nm_ref, nv_ref):
        gv = g_ref[...]
        mv = ADAM_B1 * m_ref[...] + (1.0 - ADAM_B1) * gv
        vv = ADAM_B2 * v_ref[...] + (1.0 - ADAM_B2) * (gv * gv)
        m_hat = mv / (1.0 - ADAM_B1 ** ADAM_STEP)
        v_hat = vv / (1.0 - ADAM_B2 ** ADAM_STEP)
        d_ref[...] = -ADAM_LR * (m_hat / (jnp.sqrt(v_hat) + ADAM_EPS) + ADAM_WD * w_ref[...])
        nm_ref[...] = mv
        nv_ref[...] = vv

    blk = pl.BlockSpec((tr, c), lambda i: (i, 0))
    return pl.pallas_call(
        body, name=name, grid=(r // tr,), in_specs=[blk] * 4, out_specs=[blk] * 3,
        out_shape=[jax.ShapeDtypeStruct((r, c), F32)] * 3, compiler_params=_params(("parallel",)))(w, g, m, v)


_GROUP_MASKS = {
    'xy': ((1, 0, 0), (0, 1, 0), (1, 1, 0)),
    'c': ((0, 0, 1),),
    'xyc': ((0, 0, 1), (0, 1, 0), (0, 1, 1), (1, 0, 0), (1, 0, 1), (1, 1, 0), (1, 1, 1)),
}


def _group_rank(group, pos):
    x, y, c = pos
    return {'xy': 2 * x + y, 'c': c, 'xyc': 4 * x + 2 * y + c}[group]


def _exchange(src, group, scatter, name):
    masks = _GROUP_MASKS[group]
    n = len(masks) + 1
    shape = src.shape[1:] if scatter else src.shape
    assert (not scatter) or src.shape[0] == n

    def body(src_ref, out_ref, send_sems, recv_sems, local_sem):
        me = (lax.axis_index("x"), lax.axis_index("y"), lax.axis_index("c"))
        my_rank = _group_rank(group, me)

        def mine_for(rank):
            return src_ref.at[rank] if scatter else src_ref

        local = pltpu.make_async_copy(mine_for(my_rank), out_ref.at[my_rank], local_sem)
        local.start()
        sends = []
        for k, mask in enumerate(masks):
            peer = tuple(1 - p if f else p for p, f in zip(me, mask))
            peer_rank = _group_rank(group, peer)
            send = pltpu.make_async_remote_copy(
                src_ref=mine_for(peer_rank), dst_ref=out_ref.at[my_rank], send_sem=send_sems.at[k],
                recv_sem=recv_sems.at[k], device_id=peer, device_id_type=pl.DeviceIdType.MESH)
            send.start()
            sends.append((send, peer, peer_rank))
        for k, (send, peer, peer_rank) in enumerate(sends):
            pltpu.make_async_remote_copy(
                src_ref=mine_for(peer_rank), dst_ref=out_ref.at[peer_rank], send_sem=send_sems.at[k],
                recv_sem=recv_sems.at[k], device_id=peer, device_id_type=pl.DeviceIdType.MESH).wait_recv()
        for send, _, _ in sends:
            send.wait_send()
        local.wait()

    return pl.pallas_call(
        body, name=name, in_specs=[pl.BlockSpec(memory_space=pl.ANY)], out_specs=pl.BlockSpec(memory_space=pl.ANY),
        out_shape=jax.ShapeDtypeStruct((n,) + tuple(shape), src.dtype),
        scratch_shapes=[pltpu.SemaphoreType.DMA((n - 1,)), pltpu.SemaphoreType.DMA((n - 1,)), pltpu.SemaphoreType.DMA(())],
    )(src)


CHUNK_BYTES = 1 << 20
MIN_CHUNK_ROWS = 64


def _row_parts(rows, cols, itemsize):
    n = 1
    while rows % (2 * n) == 0 and rows // (2 * n) >= MIN_CHUNK_ROWS and rows * cols * itemsize > n * CHUNK_BYTES:
        n *= 2
    return n


def _remote(src, dst, send_sem, recv_sem, peer):
    return pltpu.make_async_remote_copy(src_ref=src, dst_ref=dst, send_sem=send_sem, recv_sem=recv_sem,
                                        device_id=peer, device_id_type=pl.DeviceIdType.MESH)


def _start_rows(src, dst, send_sem, recv_sem, peer):
    rows, cols = src.shape
    n = _row_parts(rows, cols, jnp.dtype(src.dtype).itemsize)
    pr = rows // n
    for i in range(n):
        _remote(src.at[pl.ds(i * pr, pr), :], dst.at[pl.ds(i * pr, pr), :], send_sem, recv_sem, peer).start()


def _mesh_place():
    x, y, c = lax.axis_index("x"), lax.axis_index("y"), lax.axis_index("c")
    other_chips = ((1 - x, y), (x, 1 - y), (1 - x, 1 - y))
    return x, y, c, 2 * x + y, (x, y, 1 - c), other_chips


def _comm_call(body, name, ins, out_shapes, sem_counts):
    any_spec = pl.BlockSpec(memory_space=pl.ANY)
    return pl.pallas_call(
        body, name=name, in_specs=[any_spec] * len(ins), out_specs=[any_spec] * len(out_shapes), out_shape=out_shapes,
        scratch_shapes=[pltpu.SemaphoreType.DMA((n,)) for n in sem_counts])(*ins)


class _ChipSends:
    def __init__(self, srcs, pick):
        self.srcs, self.pick = list(srcs), pick
        nt = len(self.srcs)
        self.n_sems = 3 * nt
        self.out_shapes = []
        for s in self.srcs:
            r2 = s.shape[-2] // 2 if pick is _pick_weight_half else s.shape[-2]
            self.out_shapes.append(jax.ShapeDtypeStruct((3, r2, s.shape[-1]), s.dtype))

    def start(self, ins, outs, send_sems, recv_sems):
        _, _, c, _, _, other_chips = _mesh_place()
        for t in range(len(ins)):
            for j, (px, py) in enumerate(other_chips):
                _remote(self.pick(ins[t], c, 2 * px + py), outs[t].at[j], send_sems.at[3 * t + j], recv_sems.at[3 * t + j],
                        (px, py, c)).start()

    def finish(self, ins, outs, send_sems, recv_sems):
        _, _, c, _, _, other_chips = _mesh_place()
        for t in range(len(ins)):
            for j, (px, py) in enumerate(other_chips):
                landed = _remote(outs[t].at[j], outs[t].at[j], send_sems.at[3 * t + j], recv_sems.at[3 * t + j], (px, py, c))
                landed.wait_recv()
                landed.wait_send()

    def run(self, name):
        nt = len(self.srcs)

        def body(*refs):
            ins, outs = refs[:nt], refs[nt:2 * nt]
            self.start(ins, outs, *refs[2 * nt:])
            self.finish(ins, outs, *refs[2 * nt:])

        return _comm_call(body, name, self.srcs, self.out_shapes, (self.n_sems, self.n_sems))

    def specs(self):
        any_spec = pl.BlockSpec(memory_space=pl.ANY)
        return [any_spec] * len(self.srcs), [any_spec] * len(self.srcs), [pltpu.SemaphoreType.DMA((self.n_sems,))] * 2


def _pick_weight_half(src, core, chip):
    r2 = src.shape[0] // 2
    return src.at[pl.ds(pl.multiple_of(core * r2, MIN_CHUNK_ROWS), r2), :]


def _pick_slab(src, core, chip):
    return src.at[chip if src.shape[0] == N_CHIPS else 0]


def _swap_halves(gs):
    nt = len(gs)

    def body(*refs):
        ins, outs = refs[:nt], refs[nt:2 * nt]
        send_sems, recv_sems = refs[2 * nt:]
        _, _, c, _, sibling, _ = _mesh_place()
        for t in range(nt):
            for q in range(ins[t].shape[1]):
                _start_rows(ins[t].at[1 - c, q], outs[t].at[q], send_sems.at[t], recv_sems.at[t], sibling)
        for t in range(nt):
            _remote(outs[t], outs[t], send_sems.at[t], recv_sems.at[t], sibling).wait_recv()
        for t in range(nt):
            _remote(outs[t], outs[t], send_sems.at[t], recv_sems.at[t], sibling).wait_send()

    out_shapes = [jax.ShapeDtypeStruct(g.shape[1:], g.dtype) for g in gs]
    return _comm_call(body, "swap_halves", gs, out_shapes, (nt, nt))


def _swap_cores(halves):
    nt = len(halves)

    def body(*refs):
        ins, outs = refs[:nt], refs[nt:2 * nt]
        send_sems, recv_sems = refs[2 * nt:]
        _, _, _, _, sibling, _ = _mesh_place()
        for t in range(nt):
            for li in range(ins[t].shape[0]):
                _start_rows(ins[t].at[li], outs[t].at[li], send_sems.at[t], recv_sems.at[t], sibling)
        for t in range(nt):
            _remote(outs[t], outs[t], send_sems.at[t], recv_sems.at[t], sibling).wait_recv()
        for t in range(nt):
            _remote(ins[t], ins[t], send_sems.at[t], recv_sems.at[t], sibling).wait_send()

    out_shapes = [jax.ShapeDtypeStruct(h.shape, h.dtype) for h in halves]
    return _comm_call(body, "swap_cores", halves, out_shapes, (nt, nt))


def _pack_rows(pieces, cols, dtype):
    tile = SUBLANES * cols
    rows = []
    for p in pieces:
        flat = p.reshape(-1).astype(dtype)
        rows.append(jnp.pad(flat, (0, (-flat.shape[0]) % tile)).reshape(-1, cols))
    total = sum(r.shape[0] for r in rows)
    if total % (2 * SUBLANES):
        rows.append(jnp.zeros((SUBLANES, cols), dtype))
    return jnp.concatenate(rows, axis=0)


def _unpack_rows(buf, shapes):
    out, row = [], 0
    for s in shapes:
        n = math.prod(s)
        nrows = -(-n // (SUBLANES * buf.shape[1])) * SUBLANES
        out.append(buf[row:row + nrows].reshape(-1)[:n].reshape(s))
        row += nrows
    return out


ROW_SHARDED = ('w_out', 'w_glu')


def _chips_to_full(name, chips):
    mats = [c.reshape(-1, c.shape[-1]) for c in chips]
    if name in ROW_SHARDED:
        return jnp.concatenate(mats, axis=0)
    if name != 'w_in':
        return jnp.concatenate(mats, axis=1)
    cs = mats[0].shape[1]
    pieces = []
    for start, size in _PERM_PIECES:
        lo = start
        while lo < start + size:
            q = lo // cs
            hi = min(start + size, (q + 1) * cs)
            pieces.append(mats[q][:, lo - q * cs:hi - q * cs])
            lo = hi
    return jnp.concatenate(pieces, axis=1)


def _full_to_dests(name, full):
    rows, cols = full.shape
    if name in ROW_SHARDED:
        return jnp.transpose(full.reshape(N_CHIPS, 2, rows // (2 * N_CHIPS), cols), (1, 0, 2, 3))
    cs = cols // N_CHIPS
    if name != 'w_in':
        dests = [full[:, q * cs:(q + 1) * cs] for q in range(N_CHIPS)]
    else:
        offsets, off = [], 0
        for _, size in _PERM_PIECES:
            offsets.append(off)
            off += size
        by_start = sorted(zip(_PERM_PIECES, offsets))
        dests = []
        for q in range(N_CHIPS):
            parts = []
            for (start, size), at in by_start:
                lo, hi = max(start, q * cs), min(start + size, (q + 1) * cs)
                if lo < hi:
                    parts.append(full[:, at + lo - start:at + hi - start])
            dests.append(jnp.concatenate(parts, axis=1))
    return jnp.stack([d.reshape(2, rows // 2, cs) for d in dests], axis=1)


def _by_chip(own, related, x, y):
    grid = ((own, related[1]), (related[0], related[2]))
    along_x = [[jnp.where(x == 0, grid[px][dy], grid[1 - px][dy]) for dy in range(2)] for px in range(2)]
    return [jnp.where(y == 0, along_x[px][py], along_x[px][1 - py]) for px in range(2) for py in range(2)]


def _join_halves(mine, theirs, core, axis):
    return jnp.where(core == 0, jnp.concatenate([mine, theirs], axis=axis), jnp.concatenate([theirs, mine], axis=axis))


SSM_RAW = ('ssm_a_re', 'ssm_a_im', 'ssm_log_dt', 'ssm_b_re', 'ssm_b_im', 'ssm_c_re', 'ssm_c_im')


def _all_tables(p):
    raw = tuple(p[k] for k in SSM_RAW)
    (_, bb, cb), vjp = jax.vjp(jax.vmap(_ssm_tables), *raw)
    return (bb.astype(BF16), cb.astype(BF16), jax.vmap(_ssm_powers)(*raw[:3])), vjp


def _layer_fwd(x, p, w_in, rest, li, mod, tables, sends=None):
    bb, cb, pw = (t[li] for t in tables)
    h = _norm_fwd(x, _row(p['norm_g'][li]), mod)
    if isinstance(rest, dict):
        proj = _matmul(h, w_in, 'nn', tm=512, tn=1280, tk=1024, name="proj", n_outer=True)
    else:
        proj, arrived = _matmul(h, w_in, 'nn', tm=512, tn=1280, tk=1024, name="proj_sending", n_outer=True, sends=rest[0])
        rest = rest[1](arrived)
    full = dict(rest, w_in=w_in)
    o = _attention_fwd(proj, p['attn_sinks'][li])
    y_lin, states, brought = _ssm_fwd(proj, bb, cb, pw, sends)
    y_pool = _pool_fwd(proj, p['w_pool'][li].astype(BF16), _row(p['pool_scale'][li]))
    x_new = _merge_fwd(proj, o, y_lin, y_pool, x, _row(p['ssm_d'][li]), full['w_glu'], _row(p['b_glu'][li]),
                       full['w_br_att'], full['w_br_ssm'], full['w_br_pool'], full['w_out'], mod)
    saved = dict(x=x, h=h, proj=proj, o=o, y_lin=y_lin, y_pool=y_pool, states=states, bb=bb, cb=cb, pw=pw, full=full)
    return x_new, saved, brought


def _layer_bwd(dxo, s, p, li, mod, sends=None, w_in_sender=None):
    l = dxo.shape[0]
    proj, full = s['proj'], s['full']
    (do, dza, dyl, dus_skip, dzs, dyp, dzp, dgl, ya, ys, ypl, y2, dt, dpa, dps, dpp, mg, dout, st) = _merge_bwd(
        proj, s['o'], s['y_lin'], s['y_pool'], dxo, _row(p['ssm_d'][li]), full['w_glu'], _row(p['b_glu'][li]),
        full['w_br_att'], full['w_br_ssm'], full['w_br_pool'], full['w_out'], mod)
    g = {}
    g['w_br_att'] = _matmul(ya, dpa, 'tn', tm=512, tn=1024, tk=512, name="grad_w_br")
    g['w_br_ssm'] = _matmul(ys, dps, 'tn', tm=512, tn=1024, tk=512, name="grad_w_br")
    g['w_br_pool'] = _matmul(ypl, dpp, 'tn', tm=512, tn=1024, tk=512, name="grad_w_br")
    g['w_out'] = _matmul(mg, dout, 'tn', tm=512, tn=1024, tk=512, name="grad_w_out")
    g['w_glu'] = _matmul(y2, dt, 'tn', tm=512, tn=512, tk=512, name="grad_w_glu")
    g['b_glu'] = st[1, 0:SSM_W]
    g['ssm_d'] = st[1, SSM_W:2 * SSM_W]
    dgate = st[0]

    dq, dkv, dsink = _attention_bwd(proj, p['attn_sinks'][li], do)
    g['attn_sinks'] = dsink[:, 0]
    dus_scan, dbb, dcb, dlam, brought = _ssm_bwd(proj, s['states'], dyl, s['bb'], s['cb'], s['pw'], sends)
    g['ssm_tables'] = (dlam[:, 0:2, :], dbb, dcb)
    dup, dwp, dps_scale = _pool_bwd(proj, p['w_pool'][li].astype(BF16), _row(p['pool_scale'][li]), dyp)
    g['w_pool'] = dwp
    g['pool_scale'] = dps_scale[0]

    dproj = jnp.concatenate([piece.astype(BF16) for piece in
                             (dgl, dq, dza, dus_skip + dus_scan, dzs, dup, dzp, dkv[HALO:])], axis=1)
    g['w_in'] = _matmul(s['h'], dproj, 'tn', tm=1024, tn=1280, tk=512, name="grad_w_in")
    if w_in_sender is None:
        dh, brought_late = _matmul(dproj, full['w_in'], 'nt', tm=1024, tn=1024, tk=1280, name="grad_h"), None
    else:
        dh, brought_late = _matmul(dproj, full['w_in'], 'nt', tm=1024, tn=1024, tk=1280, name="grad_h_sending",
                                   sends=w_in_sender(g['w_in']))
    dx, nst = _norm_bwd(s['x'], dh, dxo, _row(p['norm_g'][li]), mod)
    g['norm_g'] = nst[2]
    dmod = jnp.concatenate([nst[0], nst[1], dgate])
    del l
    return dx, g, dmod, brought, brought_late


BIG_NAMES = tuple(name for name, _ in BIG)


def _weight_sends(shards, names, li):
    return _ChipSends([shards[name][li] for name in names], _pick_weight_half)


def _assemble_layer(shards, names, li, over_ici, place):
    x, y, core, _ = place
    from_sibling = _swap_cores(over_ici)
    full = {}
    for name, a, b in zip(names, over_ici, from_sibling):
        own = shards[name][li]
        own_halves = own.reshape(2, own.shape[0] // 2, own.shape[1])
        halves = [_join_halves(a[j][None], b[j][None], core, 0) for j in range(3)]
        full[name] = _chips_to_full(name, _by_chip(own_halves, halves, x, y))
    return full


def _pair_sums(tensors, wire, core):
    theirs = _swap_halves(tensors)
    return [_add2(lax.dynamic_index_in_dim(g, core, 0, keepdims=False), a, dt, "sum_core_pair")
            for g, a, dt in zip(tensors, theirs, wire)]


def _layer_grad_halves(layer_grads, names):
    return [_full_to_dests(name, layer_grads[name]) for name in names]


def _step(p, m, v, x, c, target):
    depth = p['norm_g'].shape[0]
    d = D_MODEL
    ix, iy, ic = lax.axis_index("x"), lax.axis_index("y"), lax.axis_index("c")
    chip = 2 * ix + iy
    dev = 4 * ix + 2 * iy + ic
    x0 = x[0]

    c_pad = jnp.pad(c, ((0, SUBLANES - 1), (0, 0)))
    c_all = _exchange(c_pad, 'xyc', False, "gather_c")[:, 0, :]
    n_ada = p['w_ada'].shape[-1]
    b_shard = lax.dynamic_slice_in_dim(p['b_ada'], chip * n_ada, n_ada, axis=1)[:, None, :]
    mod_shard = _ada_fwd(c_all, p['w_ada'].astype(BF16), b_shard)
    mod_all = _exchange(mod_shard.reshape(depth * N_DEV, n_ada), 'xy', False, "gather_mod")
    mod_all = jnp.transpose(mod_all.reshape(N_CHIPS, depth, N_DEV, n_ada), (1, 2, 0, 3)).reshape(depth, N_DEV, 3 * d)
    mods = lax.dynamic_index_in_dim(mod_all, dev, axis=1, keepdims=True)

    place = (ix, iy, ic, chip)
    n_big = len(BIG)

    shards = {name: p[name].astype(BF16) for name, _ in BIG}
    tables, tables_vjp = _all_tables(p)
    others = BIG_NAMES[1:]
    w_in = _assemble_layer(shards, BIG_NAMES[:1], 0, _weight_sends(shards, BIG_NAMES[:1], 0).run("send_weight_halves"),
                           place)['w_in']
    rest = (_weight_sends(shards, others, 0), lambda arrived: _assemble_layer(shards, others, 0, arrived, place))
    saved = []
    xs = x0
    for li in range(depth):
        sends = _weight_sends(shards, BIG_NAMES, li + 1) if li + 1 < depth else None
        xs, s, arrived = _layer_fwd(xs, p, w_in, rest, li, mods[li], tables, sends)
        saved.append(s)
        if sends is not None:
            rest = _assemble_layer(shards, BIG_NAMES, li + 1, arrived, place)
            w_in = rest.pop('w_in')
    dx, hst = _loss_head(xs, _row(p['final_g']), target[0])
    loss = lax.psum(hst[1, 0], ("x", "y", "c"))

    grads, dmods, pair, from_chips = [None] * depth, [None] * depth, [None] * depth, [None] * depth
    first_w_in = {}

    def send_first_w_in(g_w_in):
        first_w_in['pair'] = _pair_sums([_full_to_dests('w_in', g_w_in)], [BF16], ic)
        return _ChipSends(first_w_in['pair'], _pick_slab)

    sends = None
    for li in reversed(range(depth)):
        dx, grads[li], dmods[li], arrived, arrived_late = _layer_bwd(dx, saved[li], p, li, mods[li], sends,
                                                                     send_first_w_in if li == 0 else None)
        if sends is not None:
            from_chips[li + 1] = arrived
        if li > 0:
            pair[li] = _pair_sums(_layer_grad_halves(grads[li], BIG_NAMES), [BF16] * n_big, ic)
            sends = _ChipSends(pair[li], _pick_slab)

    dmod_pad = jnp.pad(jnp.stack(dmods), ((0, SUBLANES - depth), (0, 0)))
    dmod_all = _exchange(dmod_pad, 'xyc', False, "gather_dmod")[:, :depth, :]
    dmod_cols = lax.dynamic_slice_in_dim(jnp.transpose(dmod_all, (1, 0, 2)), chip * n_ada, n_ada, axis=2)
    g_w_ada = _ada_bwd(jnp.transpose(c_all), dmod_cols)

    local_small = {k: jnp.stack([grads[li][k] for li in range(depth)])
                   for k in SMALL if k not in ('final_g', 'b_ada') + SSM_RAW}
    table_cotangents = tuple(jnp.stack([grads[li]['ssm_tables'][k] for li in range(depth)]) for k in range(3))
    local_small.update(zip(SSM_RAW, tables_vjp(table_cotangents)))
    local_small['final_g'] = hst[0]
    local_small['b_ada'] = jnp.stack(dmods)
    small_pack = _pack_rows([local_small[k] for k in SMALL], PACK_COLS, F32)

    small_halves = small_pack.reshape(2, 1, small_pack.shape[0] // 2, small_pack.shape[1])
    late = _pair_sums(_layer_grad_halves(grads[0], others) + [small_halves], [BF16] * len(others) + [F32], ic)
    pair[0] = first_w_in['pair'] + late
    from_chips[0] = list(arrived_late) + list(_ChipSends(late, _pick_slab).run("scatter_chips"))
    halves = []
    for li in range(depth):
        for t, (ps, others) in enumerate(zip(pair[li], from_chips[li])):
            small = t == n_big
            own = ps[0] if small else lax.dynamic_index_in_dim(ps, chip, 0, keepdims=False)
            halves.append(_sum_chips(own, others, small, "sum_chips_small" if small else "sum_chips")[None])
    totals = [_join_halves(h[0], o[0], ic, 0) for h, o in zip(halves, _swap_cores(halves))]
    small_sum = totals[n_big]
    per_layer = [totals[:n_big]] + [totals[n_big + 1 + (li - 1) * n_big:n_big + 1 + li * n_big] for li in range(1, depth)]

    small_shapes = [p[k].shape for k in SMALL]
    grad = dict(zip(SMALL, _unpack_rows(small_sum, small_shapes)))
    grad['w_ada'] = g_w_ada
    for t, (name, _) in enumerate(BIG):
        grad[name] = jnp.stack([per_layer[li][t] for li in range(depth)])

    delta, new_m, new_v = {}, {}, {}
    outs = _adamw(_pack_rows([p[k] for k in SMALL], PACK_COLS, F32), small_sum,
                  _pack_rows([m[k] for k in SMALL], PACK_COLS, F32), _pack_rows([v[k] for k in SMALL], PACK_COLS, F32),
                  name="adamw_small")
    for res, o in zip((delta, new_m, new_v), outs):
        res.update(zip(SMALL, _unpack_rows(o, small_shapes)))
    for name in ['w_ada'] + [n for n, _ in BIG]:
        shape = p[name].shape
        two_d = (-1, shape[-1])
        outs = _adamw(p[name].reshape(two_d), grad[name].reshape(two_d), m[name].reshape(two_d), v[name].reshape(two_d),
                      name="adamw_" + name)
        delta[name], new_m[name], new_v[name] = (o.reshape(shape) for o in outs)

    return (loss, dx[None], *[grad[k] for k in WEIGHTS], *[delta[k] for k in WEIGHTS],
            *[new_m[k] for k in WEIGHTS], *[new_v[k] for k in WEIGHTS])


def kernel(x, c, norm_g, w_ada, b_ada, w_in, attn_sinks, ssm_a_re, ssm_a_im, ssm_log_dt, ssm_b_re, ssm_b_im, ssm_c_re, ssm_c_im, ssm_d, w_glu, b_glu, w_pool, pool_scale, w_br_att, w_br_ssm, w_br_pool, w_out, final_g, loss_target, m_norm_g, m_w_ada, m_b_ada, m_w_in, m_attn_sinks, m_ssm_a_re, m_ssm_a_im, m_ssm_log_dt, m_ssm_b_re, m_ssm_b_im, m_ssm_c_re, m_ssm_c_im, m_ssm_d, m_w_glu, m_b_glu, m_w_pool, m_pool_scale, m_w_br_att, m_w_br_ssm, m_w_br_pool, m_w_out, m_final_g, v_norm_g, v_w_ada, v_b_ada, v_w_in, v_attn_sinks, v_ssm_a_re, v_ssm_a_im, v_ssm_log_dt, v_ssm_b_re, v_ssm_b_im, v_ssm_c_re, v_ssm_c_im, v_ssm_d, v_w_glu, v_b_glu, v_w_pool, v_pool_scale, v_w_br_att, v_w_br_ssm, v_w_br_pool, v_w_out, v_final_g):
    p = dict(zip(WEIGHTS, (norm_g, w_ada, b_ada, w_in, attn_sinks, ssm_a_re, ssm_a_im, ssm_log_dt, ssm_b_re, ssm_b_im,
                           ssm_c_re, ssm_c_im, ssm_d, w_glu, b_glu, w_pool, pool_scale, w_br_att, w_br_ssm, w_br_pool,
                           w_out, final_g)))
    m = dict(zip(WEIGHTS, (m_norm_g, m_w_ada, m_b_ada, m_w_in, m_attn_sinks, m_ssm_a_re, m_ssm_a_im, m_ssm_log_dt,
                           m_ssm_b_re, m_ssm_b_im, m_ssm_c_re, m_ssm_c_im, m_ssm_d, m_w_glu, m_b_glu, m_w_pool,
                           m_pool_scale, m_w_br_att, m_w_br_ssm, m_w_br_pool, m_w_out, m_final_g)))
    v = dict(zip(WEIGHTS, (v_norm_g, v_w_ada, v_b_ada, v_w_in, v_attn_sinks, v_ssm_a_re, v_ssm_a_im, v_ssm_log_dt,
                           v_ssm_b_re, v_ssm_b_im, v_ssm_c_re, v_ssm_c_im, v_ssm_d, v_w_glu, v_b_glu, v_w_pool,
                           v_pool_scale, v_w_br_att, v_w_br_ssm, v_w_br_pool, v_w_out, v_final_g)))
    return _step(p, m, v, x, c, loss_target)
```

```python
import functools
import math

import jax
import jax.numpy as jnp
from jax import lax
from jax.experimental import pallas as pl
from jax.experimental.pallas import tpu as pltpu

F32 = jnp.float32
BF16 = jnp.bfloat16

D_MODEL = 1024
CHUNK = 64
N_HEADS = 8
N_KV_HEADS = 2
HEAD_DIM = 64
Q_PER_KV = N_HEADS // N_KV_HEADS
HALO = 128
ATT_W = 512
KV_W = 128
SSM_W = 512
SSM_GROUP = 16
SSM_GROUPS = 32
SSM_STATE = 64
POOL_W = 512
POOL_GW = 128
IN_W = 6400
EPS = 1e-6
NEG_INF = -1e30
ADAM_LR = 0.001
ADAM_B1 = 0.9
ADAM_B2 = 0.999
ADAM_EPS = 1e-08
ADAM_WD = 0.01
ADAM_STEP = 10

OFF_GL, OFF_Q, OFF_ZA, OFF_US, OFF_ZS, OFF_UP, OFF_ZP, OFF_KV = 0, 3072, 3584, 4096, 4608, 5120, 5632, 6144
_PERM_PIECES = ((3328, 3072), (0, 512), (1792, 512), (768, 512), (2304, 512), (1280, 512), (2816, 512), (512, 256))

LANES = 128
SUBLANES = 8
N_SBLK = SSM_GROUPS * SSM_STATE // LANES
VMEM_LIMIT = 48 * 1024 * 1024

N_CHIPS = 4
N_DEV = 8

WEIGHTS = ['norm_g', 'w_ada', 'b_ada', 'w_in', 'attn_sinks', 'ssm_a_re', 'ssm_a_im', 'ssm_log_dt', 'ssm_b_re',
           'ssm_b_im', 'ssm_c_re', 'ssm_c_im', 'ssm_d', 'w_glu', 'b_glu', 'w_pool', 'pool_scale', 'w_br_att',
           'w_br_ssm', 'w_br_pool', 'w_out', 'final_g']
SMALL = ['norm_g', 'b_ada', 'attn_sinks', 'ssm_a_re', 'ssm_a_im', 'ssm_log_dt', 'ssm_b_re', 'ssm_b_im', 'ssm_c_re',
         'ssm_c_im', 'ssm_d', 'b_glu', 'w_pool', 'pool_scale', 'final_g']
BIG = (('w_in', (1024, 1600)), ('w_br_att', (512, 256)), ('w_br_ssm', (512, 256)), ('w_br_pool', (512, 256)),
       ('w_out', (256, 1024)), ('w_glu', (128, 512)))
PACK_COLS = 1024


def _params(sem=None):
    return pltpu.CompilerParams(dimension_semantics=sem, vmem_limit_bytes=VMEM_LIMIT)


def _row(v):
    return v.reshape(1, -1)


def _full(shape):
    nd = len(shape)
    return pl.BlockSpec(shape, lambda *_: (0,) * nd)


def _sigmoid(v):
    return 1.0 / (1.0 + jnp.exp(-v))


def _silu_and_grad(z):
    s = _sigmoid(z)
    return z * s, s * (1.0 + z * (1.0 - s))


_GELU_K = math.sqrt(2.0 / math.pi)


def _gelu_and_grad(v):
    inner = _GELU_K * (v + 0.044715 * v * v * v)
    th = jnp.tanh(inner)
    val = 0.5 * v * (1.0 + th)
    grad = 0.5 * (1.0 + th) + 0.5 * v * (1.0 - th * th) * _GELU_K * (1.0 + 3 * 0.044715 * v * v)
    return val, grad


_NN = (((1,), (0,)), ((), ()))
_NT = (((1,), (1,)), ((), ()))
_TN = (((0,), (0,)), ((), ()))


def _dot(a, b, dims=_NN):
    return lax.dot_general(a.astype(BF16), b.astype(BF16), dims, preferred_element_type=F32)


def _matmul(a, b, mode, *, tm, tn, tk, name, out_dtype=F32, n_outer=False, sends=None):
    if mode == 'nn':
        (m, k), (_, n) = a.shape, b.shape
    elif mode == 'nt':
        (m, k), (n, _) = a.shape, b.shape
    else:
        (k, m), (_, n) = a.shape, b.shape
    tm, tn, tk = min(tm, m), min(tn, n), min(tk, k)
    assert m % tm == 0 and n % tn == 0 and k % tk == 0, (name, a.shape, b.shape)
    nk = k // tk
    dims = {'nn': _NN, 'nt': _NT, 'tn': _TN}[mode]

    def body(a_ref, b_ref, o_ref, acc_ref):
        if nk == 1:
            o_ref[...] = _dot(a_ref[...], b_ref[...], dims).astype(out_dtype)
            return
        kk = pl.program_id(2)

        @pl.when(kk == 0)
        def _():
            acc_ref[...] = jnp.zeros_like(acc_ref)

        acc_ref[...] += _dot(a_ref[...], b_ref[...], dims)

        @pl.when(kk == nk - 1)
        def _():
            o_ref[...] = acc_ref[...].astype(out_dtype)

    def spec(shape, index):
        if n_outer:
            return pl.BlockSpec(shape, lambda j, i, kk: index(i, j, kk))
        return pl.BlockSpec(shape, index)

    a_spec = spec((tk, tm), lambda i, j, kk: (kk, i)) if mode == 'tn' else spec((tm, tk), lambda i, j, kk: (i, kk))
    b_spec = spec((tn, tk), lambda i, j, kk: (j, kk)) if mode == 'nt' else spec((tk, tn), lambda i, j, kk: (kk, j))
    grid = (n // tn, m // tm, nk) if n_outer else (m // tm, n // tn, nk)
    body, x_in, x_out, x_shapes, x_scratch, x_args = _carry_sends(body, 2, 1, 1, grid, sends)
    semantics = ("parallel", "parallel", "arbitrary") if sends is None else ("arbitrary",) * 3
    outs = pl.pallas_call(
        body, name=name, grid=grid, in_specs=[a_spec, b_spec] + x_in,
        out_specs=[spec((tm, tn), lambda i, j, kk: (i, j))] + x_out,
        out_shape=[jax.ShapeDtypeStruct((m, n), out_dtype)] + x_shapes,
        scratch_shapes=[pltpu.VMEM((tm, tn), F32)] + x_scratch, compiler_params=_params(semantics))(a, b, *x_args)
    return outs[0] if sends is None else (outs[0], list(outs[1:]))


ROWS_NORM = 512


def _norm_fwd(x, g, mod):
    l, d = x.shape
    tr = min(ROWS_NORM, l)

    def body(x_ref, g_ref, mod_ref, h_ref):
        xv = x_ref[...]
        r = lax.rsqrt(jnp.mean(xv * xv, axis=-1, keepdims=True) + EPS)
        shift, scale = mod_ref[:, 0:d], mod_ref[:, d:2 * d]
        h_ref[...] = ((xv * r * g_ref[...]) * (1.0 + scale) + shift).astype(BF16)

    return pl.pallas_call(
        body, name="norm_fwd", grid=(l // tr,),
        in_specs=[pl.BlockSpec((tr, d), lambda i: (i, 0)), _full((1, d)), _full((1, 3 * d))],
        out_specs=pl.BlockSpec((tr, d), lambda i: (i, 0)), out_shape=jax.ShapeDtypeStruct((l, d), BF16),
        compiler_params=_params(("parallel",)))(x, g, mod)


def _norm_bwd(x, dh, dxo, g, mod):
    l, d = x.shape
    tr = min(ROWS_NORM, l)

    def body(x_ref, dh_ref, dxo_ref, g_ref, mod_ref, dx_ref, st_ref):
        @pl.when(pl.program_id(0) == 0)
        def _():
            st_ref[...] = jnp.zeros_like(st_ref)

        xv, dhv = x_ref[...], dh_ref[...]
        r = lax.rsqrt(jnp.mean(xv * xv, axis=-1, keepdims=True) + EPS)
        xn = xv * r
        gv = g_ref[...]
        sc1 = 1.0 + mod_ref[:, d:2 * d]
        dxn = dhv * gv * sc1
        dx_ref[...] = dxo_ref[...] + r * (dxn - xn * jnp.mean(dxn * xn, axis=-1, keepdims=True))
        st_ref[0:1, :] += jnp.sum(dhv, axis=0, keepdims=True)
        st_ref[1:2, :] += jnp.sum(dhv * xn * gv, axis=0, keepdims=True)
        st_ref[2:3, :] += jnp.sum(dhv * xn * sc1, axis=0, keepdims=True)

    blk = pl.BlockSpec((tr, d), lambda i: (i, 0))
    return pl.pallas_call(
        body, name="norm_bwd", grid=(l // tr,),
        in_specs=[blk, blk, blk, _full((1, d)), _full((1, 3 * d))],
        out_specs=[blk, _full((SUBLANES, d))],
        out_shape=[jax.ShapeDtypeStruct((l, d), F32), jax.ShapeDtypeStruct((SUBLANES, d), F32)],
        compiler_params=_params(("arbitrary",)))(x, dh, dxo, g, mod)


def _loss_head(x, g, target):
    l, d = x.shape
    tr = min(ROWS_NORM, l)

    def body(x_ref, g_ref, t_ref, dx_ref, st_ref):
        @pl.when(pl.program_id(0) == 0)
        def _():
            st_ref[...] = jnp.zeros_like(st_ref)

        xv = x_ref[...]
        r = lax.rsqrt(jnp.mean(xv * xv, axis=-1, keepdims=True) + EPS)
        xn = xv * r
        gv = g_ref[...]
        err = xn * gv - t_ref[...]
        part = 0.5 * jnp.sum(jnp.mean(err * err, axis=-1, keepdims=True), axis=0, keepdims=True)
        dy = err * (1.0 / d)
        dxn = dy * gv
        dx_ref[...] = r * (dxn - xn * jnp.mean(dxn * xn, axis=-1, keepdims=True))
        st_ref[0:1, :] += jnp.sum(dy * xn, axis=0, keepdims=True)
        st_ref[1:2, :] += jnp.broadcast_to(part, (1, d))

    blk = pl.BlockSpec((tr, d), lambda i: (i, 0))
    return pl.pallas_call(
        body, name="loss_head", grid=(l // tr,), in_specs=[blk, _full((1, d)), blk],
        out_specs=[blk, _full((SUBLANES, d))],
        out_shape=[jax.ShapeDtypeStruct((l, d), F32), jax.ShapeDtypeStruct((SUBLANES, d), F32)],
        compiler_params=_params(("arbitrary",)))(x, g, target)


ROWS_ATT = 128
ROWS_ATT_BWD = 256
_SLOPES = tuple(2.0 ** (-8.0 * (h + 1) / N_HEADS) for h in range(N_HEADS))


def _att_mask(i, t):
    r = lax.broadcasted_iota(jnp.int32, (t, t + HALO), 0)
    j = lax.broadcasted_iota(jnp.int32, (t, t + HALO), 1)
    dist = jnp.abs(r + HALO - j).astype(F32)
    rc, jc = r // CHUNK, j // CHUNK
    allowed = (jc >= rc) & (jc <= rc + 2) & ((j >= HALO) | (i > 0))
    return dist, allowed


def _att_probs(qh, k, dist, allowed, slope, sink):
    s = _dot(qh, k, _NT) * (1.0 / math.sqrt(HEAD_DIM)) - slope * dist
    s = jnp.where(allowed, s, NEG_INF)
    m = jnp.maximum(jnp.max(s, axis=1, keepdims=True), sink)
    e = jnp.exp(s - m)
    es = jnp.exp(sink - m)
    den = jnp.sum(e, axis=1, keepdims=True) + es
    return e / den, es / den


def _att_specs(t):
    q_spec = pl.BlockSpec((t, ATT_W), lambda i: (i, OFF_Q // ATT_W))
    kv_spec = pl.BlockSpec((t, 2 * KV_W), lambda i: (i, OFF_KV // (2 * KV_W)))
    halo_spec = pl.BlockSpec((HALO, 2 * KV_W), lambda i: (jnp.maximum(i * (t // HALO) - 1, 0), OFF_KV // (2 * KV_W)))
    return q_spec, kv_spec, halo_spec


def _attention_fwd(proj, sinks):
    l = proj.shape[0]
    t = min(ROWS_ATT, l)

    def body(sink_ref, q_ref, kv_ref, halo_ref, o_ref):
        dist, allowed = _att_mask(pl.program_id(0), t)
        kv = jnp.concatenate([halo_ref[...], kv_ref[...]], axis=0)
        for h in range(N_HEADS):
            kh = h // Q_PER_KV
            k = kv[:, kh * HEAD_DIM:(kh + 1) * HEAD_DIM]
            v = kv[:, KV_W + kh * HEAD_DIM:KV_W + (kh + 1) * HEAD_DIM]
            p, _ = _att_probs(q_ref[:, h * HEAD_DIM:(h + 1) * HEAD_DIM], k, dist, allowed, _SLOPES[h], sink_ref[h])
            o_ref[:, h * HEAD_DIM:(h + 1) * HEAD_DIM] = _dot(p, v)

    q_spec, kv_spec, halo_spec = _att_specs(t)
    return pl.pallas_call(
        body, name="attention_fwd", grid=(l // t,),
        in_specs=[pl.BlockSpec(memory_space=pltpu.SMEM), q_spec, kv_spec, halo_spec],
        out_specs=pl.BlockSpec((t, ATT_W), lambda i: (i, 0)), out_shape=jax.ShapeDtypeStruct((l, ATT_W), F32),
        compiler_params=_params(("parallel",)))(sinks, proj, proj, proj)


def _attention_bwd(proj, sinks, do):
    l = proj.shape[0]
    t = min(ROWS_ATT_BWD, l)

    def body(sink_ref, q_ref, kv_ref, halo_ref, do_ref, dq_ref, dkv_ref, dsink_ref):
        i = pl.program_id(0)

        @pl.when(i == 0)
        def _():
            dkv_ref[...] = jnp.zeros_like(dkv_ref)
            dsink_ref[...] = jnp.zeros_like(dsink_ref)

        dist, allowed = _att_mask(i, t)
        kv = jnp.concatenate([halo_ref[...], kv_ref[...]], axis=0)
        rows = pl.ds(pl.multiple_of(i * t, t), t + HALO)
        for kh in range(N_KV_HEADS):
            k = kv[:, kh * HEAD_DIM:(kh + 1) * HEAD_DIM]
            v = kv[:, KV_W + kh * HEAD_DIM:KV_W + (kh + 1) * HEAD_DIM]
            dk = jnp.zeros((t + HALO, HEAD_DIM), F32)
            dv = jnp.zeros((t + HALO, HEAD_DIM), F32)
            for h in range(kh * Q_PER_KV, (kh + 1) * Q_PER_KV):
                qh = q_ref[:, h * HEAD_DIM:(h + 1) * HEAD_DIM]
                doh = do_ref[:, h * HEAD_DIM:(h + 1) * HEAD_DIM]
                p, ps = _att_probs(qh, k, dist, allowed, _SLOPES[h], sink_ref[h])
                dp = _dot(doh, v, _NT)
                delta = jnp.sum(p * dp, axis=1, keepdims=True)
                ds = p * (dp - delta) * (1.0 / math.sqrt(HEAD_DIM))
                dsink_ref[h:h + 1, :] += jnp.broadcast_to(-jnp.sum(ps * delta, axis=0, keepdims=True), (1, LANES))
                dq_ref[:, h * HEAD_DIM:(h + 1) * HEAD_DIM] = _dot(ds, k).astype(BF16)
                dk = dk + _dot(ds, qh, _TN)
                dv = dv + _dot(p, doh, _TN)
            dkv_ref[rows, kh * HEAD_DIM:(kh + 1) * HEAD_DIM] += dk
            dkv_ref[rows, KV_W + kh * HEAD_DIM:KV_W + (kh + 1) * HEAD_DIM] += dv

    q_spec, kv_spec, halo_spec = _att_specs(t)
    blk = pl.BlockSpec((t, ATT_W), lambda i: (i, 0))
    return pl.pallas_call(
        body, name="attention_bwd", grid=(l // t,),
        in_specs=[pl.BlockSpec(memory_space=pltpu.SMEM), q_spec, kv_spec, halo_spec, blk],
        out_specs=[blk, _full((HALO + l, 2 * KV_W)), _full((N_HEADS, LANES))],
        out_shape=[jax.ShapeDtypeStruct((l, ATT_W), BF16), jax.ShapeDtypeStruct((HALO + l, 2 * KV_W), F32),
                   jax.ShapeDtypeStruct((N_HEADS, LANES), F32)],
        compiler_params=_params(("arbitrary",)))(sinks, proj, proj, proj, do)


ROWS_SSM = 2048
SCAN_UNROLL = 4


def _ssm_discretize(a_re, a_im, log_dt, b_re, b_im):
    lam = lax.complex(a_re, a_im)
    dt = jnp.exp(log_dt)[:, None]
    lam_bar = jnp.exp(lam * dt)
    b_bar = ((lam_bar - 1.0) / lam)[..., None] * lax.complex(b_re, b_im)
    return lam, dt, lam_bar, b_bar


def _ssm_block_diag(m):
    e = m.reshape(N_SBLK, 2, SSM_GROUP, SSM_STATE)
    e = e[:, :, :, None, :] * jnp.eye(2, dtype=m.dtype)[None, :, None, :, None]
    e = e.reshape(N_SBLK, 2 * SSM_GROUP, LANES)
    oh = jax.nn.one_hot(jnp.arange(N_SBLK) % 4, 4, dtype=m.dtype)
    return (oh[:, :, None, None] * e[:, None]).reshape(N_SBLK, LANES, LANES)


def _ssm_tables(a_re, a_im, log_dt, b_re, b_im, c_re, c_im):
    _, _, lam_bar, b_bar = _ssm_discretize(a_re, a_im, log_dt, b_re, b_im)
    lam_blk = jnp.stack([jnp.real(lam_bar).reshape(N_SBLK, LANES), jnp.imag(lam_bar).reshape(N_SBLK, LANES)], axis=1)
    bt = jnp.transpose(b_bar, (0, 2, 1))
    bb = jnp.concatenate([_ssm_block_diag(jnp.real(bt)), _ssm_block_diag(jnp.imag(bt))], axis=2)
    cb = jnp.concatenate([jnp.transpose(_ssm_block_diag(c_re), (0, 2, 1)),
                          jnp.transpose(_ssm_block_diag(-c_im), (0, 2, 1))], axis=1)
    return lam_blk, bb, cb


def _ssm_powers(a_re, a_im, log_dt):
    lam = lax.complex(a_re, a_im)
    dt = jnp.exp(log_dt)[:, None]
    k = jnp.arange(1, SUBLANES + 1, dtype=F32)
    pw = jnp.exp((lam * dt)[None] * k[:, None, None]).reshape(SUBLANES, N_SBLK, LANES)
    pw = jnp.transpose(pw, (1, 0, 2))
    rev = pw[:, ::-1]
    return jnp.concatenate([jnp.real(pw), jnp.imag(pw), jnp.real(rev), jnp.imag(rev)], axis=1)


def _scan_consts(pw_ref, reverse):
    row = lax.broadcasted_iota(jnp.int32, (SUBLANES, LANES), 0)
    sign = -1.0 if reverse else 1.0

    def power(k):
        return (jnp.broadcast_to(pw_ref[0, k - 1:k, :], (SUBLANES, LANES)),
                sign * jnp.broadcast_to(pw_ref[0, SUBLANES + k - 1:SUBLANES + k, :], (SUBLANES, LANES)))

    steps = []
    for d in (1, 2, 4):
        pr, pi = power(d)
        keep = (row < SUBLANES - d) if reverse else (row >= d)
        steps.append((d, jnp.where(keep, pr, 0.0), jnp.where(keep, pi, 0.0)))
    base = 2 * SUBLANES if reverse else 0
    return steps, pw_ref[0, base:base + SUBLANES, :], sign * pw_ref[0, base + SUBLANES:base + 2 * SUBLANES, :]


def _scan_tile(xr, xi, steps, reverse):
    for d, ar, ai in steps:
        shift = SUBLANES - d if reverse else d
        rr, ri = pltpu.roll(xr, shift, 0), pltpu.roll(xi, shift, 0)
        xr, xi = xr + (ar * rr - ai * ri), xi + (ar * ri + ai * rr)
    return xr, xi


def _bcast_row(v, r):
    return jnp.broadcast_to(v[r:r + 1, :], (SUBLANES, LANES))


def _scan_forward(s_scr, pw_ref, n_tiles):
    steps, pr, pi = _scan_consts(pw_ref, False)

    def tiles(g, carry):
        cr, ci = carry
        rows = [pl.ds(pl.multiple_of((g * SCAN_UNROLL + u) * SUBLANES, SUBLANES), SUBLANES) for u in range(SCAN_UNROLL)]
        loaded = [(s_scr[r, 0:LANES], s_scr[r, LANES:2 * LANES]) for r in rows]
        local = [_scan_tile(xr, xi, steps, False) for xr, xi in loaded]
        for r, (xr, xi) in zip(rows, local):
            sr, si = xr + (pr * cr - pi * ci), xi + (pr * ci + pi * cr)
            s_scr[r, 0:LANES] = sr
            s_scr[r, LANES:2 * LANES] = si
            cr, ci = _bcast_row(sr, SUBLANES - 1), _bcast_row(si, SUBLANES - 1)
        return cr, ci

    zero = jnp.zeros((SUBLANES, LANES), F32)
    lax.fori_loop(0, n_tiles // SCAN_UNROLL, tiles, (zero, zero))


def _row_chunks(l):
    rc = min(ROWS_SSM, l)
    return rc, l // rc


def _carry_sends(body, n_in, n_out, n_scratch, grid, sends):
    if sends is None:
        return body, [], [], [], [], []
    k = len(sends.srcs)

    def carrying(*refs):
        ins, send_ins = refs[:n_in], refs[n_in:n_in + k]
        outs, send_outs = refs[n_in + k:n_in + k + n_out], refs[n_in + k + n_out:n_in + 2 * k + n_out]
        rest = refs[n_in + 2 * k + n_out:]
        scratch, sems = rest[:n_scratch], rest[n_scratch:]
        ids = [pl.program_id(axis) for axis in range(len(grid))]
        first = functools.reduce(jnp.logical_and, [i == 0 for i in ids])
        last = functools.reduce(jnp.logical_and, [i == n - 1 for i, n in zip(ids, grid)])

        @pl.when(first)
        def _():
            sends.start(send_ins, send_outs, *sems)

        body(*ins, *outs, *scratch)

        @pl.when(last)
        def _():
            sends.finish(send_ins, send_outs, *sems)

    in_specs, out_specs, scratch = sends.specs()
    return carrying, in_specs, out_specs, sends.out_shapes, scratch, sends.srcs


def _ssm_fwd(proj, bb, cb, pw, sends=None):
    l = proj.shape[0]
    rc, n_chunks = _row_chunks(l)

    def body(u_ref, bb_ref, cb_ref, pw_ref, y_ref, s_scr):
        j = pl.program_id(0)

        def fill(ci, _):
            rows = pl.ds(pl.multiple_of(ci * rc, rc), rc)
            s_scr[rows, :] = _dot(u_ref[rows, :], bb_ref[0])
            return 0

        lax.fori_loop(0, n_chunks, fill, 0)
        _scan_forward(s_scr, pw_ref, l // SUBLANES)

        @pl.when(j % 4 == 0)
        def _():
            y_ref[...] = jnp.zeros_like(y_ref)

        def emit(ci, _):
            rows = pl.ds(pl.multiple_of(ci * rc, rc), rc)
            y_ref[rows, :] += _dot(s_scr[rows, :], cb_ref[0])
            return 0

        lax.fori_loop(0, n_chunks, emit, 0)

    body, x_in, x_out, x_shapes, x_scratch, x_args = _carry_sends(body, 4, 2, 0, (N_SBLK,), sends)
    outs = pl.pallas_call(
        body, name="ssm_fwd" if sends is None else "ssm_fwd_sending", grid=(N_SBLK,),
        in_specs=[pl.BlockSpec((l, LANES), lambda j: (0, OFF_US // LANES + j // 4)),
                  pl.BlockSpec((1, LANES, 2 * LANES), lambda j: (j, 0, 0)),
                  pl.BlockSpec((1, 2 * LANES, LANES), lambda j: (j, 0, 0)),
                  pl.BlockSpec((1, 4 * SUBLANES, LANES), lambda j: (j, 0, 0))] + x_in,
        out_specs=[pl.BlockSpec((l, LANES), lambda j: (0, j // 4)), pl.BlockSpec((l, 2 * LANES), lambda j: (0, j))] + x_out,
        out_shape=[jax.ShapeDtypeStruct((l, SSM_W), F32), jax.ShapeDtypeStruct((l, N_SBLK * 2 * LANES), F32)] + x_shapes,
        scratch_shapes=x_scratch, compiler_params=_params(("arbitrary",)))(proj, bb, cb, pw, *x_args)
    return outs[0], outs[1], list(outs[2:])


def _ssm_bwd(proj, states, dy, bb, cb, pw, sends=None):
    l = proj.shape[0]
    rc, n_chunks = _row_chunks(l)
    n_tiles = l // SUBLANES

    def body(u_ref, s_scr, dy_ref, bb_ref, cb_ref, pw_ref, du_ref, dbb_ref, dcb_ref, dlam_ref, a_scr):
        j = pl.program_id(0)
        dcb_ref[...] = jnp.zeros_like(dcb_ref)

        def through_c(ci, _):
            rows = pl.ds(pl.multiple_of(ci * rc, rc), rc)
            dyv = dy_ref[rows, :]
            dcb_ref[0] += _dot(s_scr[rows, :], dyv, _TN)
            a_scr[rows, :] = _dot(dyv, cb_ref[0], _NT)
            return 0

        lax.fori_loop(0, n_chunks, through_c, 0)

        steps, pr, pi = _scan_consts(pw_ref, True)
        row = lax.broadcasted_iota(jnp.int32, (SUBLANES, LANES), 0)

        last = row == SUBLANES - 1

        def tiles(g, carry):
            cr, ci, accr, acci = carry
            rows = [pl.ds(pl.multiple_of((n_tiles - 1 - (g * SCAN_UNROLL + u)) * SUBLANES, SUBLANES), SUBLANES)
                    for u in range(SCAN_UNROLL)]
            loaded = [(a_scr[r, 0:LANES], a_scr[r, LANES:2 * LANES]) for r in rows]
            states = [(s_scr[r, 0:LANES], s_scr[r, LANES:2 * LANES]) for r in rows]
            local = [_scan_tile(xr, xi, steps, True) for xr, xi in loaded]
            for r, (xr, xi), (sr, si) in zip(rows, local, states):
                gr, gi = xr + (pr * cr - pi * ci), xi + (pr * ci + pi * cr)
                a_scr[r, 0:LANES] = gr
                a_scr[r, LANES:2 * LANES] = gi
                ur = jnp.where(last, cr, pltpu.roll(gr, SUBLANES - 1, 0))
                ui = jnp.where(last, ci, pltpu.roll(gi, SUBLANES - 1, 0))
                accr, acci = accr + (ur * sr + ui * si), acci + (ui * sr - ur * si)
                cr, ci = _bcast_row(gr, 0), _bcast_row(gi, 0)
            return cr, ci, accr, acci

        zero = jnp.zeros((SUBLANES, LANES), F32)
        _, _, accr, acci = lax.fori_loop(0, n_tiles // SCAN_UNROLL, tiles, (zero, zero, zero, zero))
        dlr = jnp.broadcast_to(jnp.sum(accr, axis=0, keepdims=True), (SUBLANES, LANES))
        dli = jnp.broadcast_to(jnp.sum(acci, axis=0, keepdims=True), (SUBLANES, LANES))
        dlam_ref[0] = jnp.where(row == 0, dlr, jnp.where(row == 1, dli, 0.0))

        dbb_ref[...] = jnp.zeros_like(dbb_ref)

        @pl.when(j % 4 == 0)
        def _():
            du_ref[...] = jnp.zeros_like(du_ref)

        def through_b(ci, _):
            rows = pl.ds(pl.multiple_of(ci * rc, rc), rc)
            av = a_scr[rows, :]
            dbb_ref[0] += _dot(u_ref[rows, :], av, _TN)
            du_ref[rows, :] += _dot(av, bb_ref[0], _NT)
            return 0

        lax.fori_loop(0, n_chunks, through_b, 0)

    body, x_in, x_out, x_shapes, x_scratch, x_args = _carry_sends(body, 6, 4, 1, (N_SBLK,), sends)
    outs = pl.pallas_call(
        body, name="ssm_bwd" if sends is None else "ssm_bwd_sending", grid=(N_SBLK,),
        in_specs=[pl.BlockSpec((l, LANES), lambda j: (0, OFF_US // LANES + j // 4)),
                  pl.BlockSpec((l, 2 * LANES), lambda j: (0, j)),
                  pl.BlockSpec((l, LANES), lambda j: (0, j // 4)),
                  pl.BlockSpec((1, LANES, 2 * LANES), lambda j: (j, 0, 0)),
                  pl.BlockSpec((1, 2 * LANES, LANES), lambda j: (j, 0, 0)),
                  pl.BlockSpec((1, 4 * SUBLANES, LANES), lambda j: (j, 0, 0))] + x_in,
        out_specs=[pl.BlockSpec((l, LANES), lambda j: (0, j // 4)),
                   pl.BlockSpec((1, LANES, 2 * LANES), lambda j: (j, 0, 0)),
                   pl.BlockSpec((1, 2 * LANES, LANES), lambda j: (j, 0, 0)),
                   pl.BlockSpec((1, SUBLANES, LANES), lambda j: (j, 0, 0))] + x_out,
        out_shape=[jax.ShapeDtypeStruct((l, SSM_W), F32), jax.ShapeDtypeStruct((N_SBLK, LANES, 2 * LANES), F32),
                   jax.ShapeDtypeStruct((N_SBLK, 2 * LANES, LANES), F32),
                   jax.ShapeDtypeStruct((N_SBLK, SUBLANES, LANES), F32)] + x_shapes,
        scratch_shapes=[pltpu.VMEM((l, 2 * LANES), F32)] + x_scratch,
        compiler_params=_params(("arbitrary",)))(proj, states, dy, bb, cb, pw, *x_args)
    return outs[0], outs[1], outs[2], outs[3], list(outs[4:])


def _pool_windows(v, t, l, ahead):
    def shifted(a, d):
        if ahead:
            return jnp.where(t < l - d, pltpu.roll(a, l - d, 0), 0.0)
        return jnp.where(t >= d, pltpu.roll(a, d, 0), 0.0)

    w2 = v + shifted(v, 1)
    w4 = w2 + shifted(w2, 2)
    w8 = w4 + shifted(w4, 4)
    w16 = w8 + shifted(w8, 8)
    return w2, w4, w8, w16


def _pool_select(g, ws):
    return jnp.where(g == 0, ws[0], jnp.where(g == 1, ws[1], jnp.where(g == 2, ws[2], ws[3])))


def _pool_count(g, t):
    return jnp.minimum(t + 1, jnp.left_shift(2, g)).astype(F32)


def _pool_specs(l):
    return [pl.BlockSpec((l, POOL_GW), lambda g: (0, OFF_UP // POOL_GW + g)),
            pl.BlockSpec((1, POOL_GW, POOL_GW), lambda g: (g, 0, 0)),
            pl.BlockSpec((1, POOL_GW), lambda g: (0, g))]


def _pool_fwd(proj, w_pool, scale):
    l = proj.shape[0]

    def body(u_ref, w_ref, sc_ref, y_ref):
        g = pl.program_id(0)
        t = lax.broadcasted_iota(jnp.int32, (l, 1), 0)
        u = u_ref[...]
        pooled = _pool_select(g, _pool_windows(u, t, l, False)) / _pool_count(g, t) - u
        y_ref[...] = _dot(pooled, w_ref[0]) * sc_ref[...]

    return pl.pallas_call(
        body, name="pool_fwd", grid=(4,), in_specs=_pool_specs(l),
        out_specs=pl.BlockSpec((l, POOL_GW), lambda g: (0, g)), out_shape=jax.ShapeDtypeStruct((l, POOL_W), F32),
        compiler_params=_params(("parallel",)))(proj, w_pool, scale)


def _pool_bwd(proj, w_pool, scale, dy):
    l = proj.shape[0]

    def body(u_ref, w_ref, sc_ref, dy_ref, du_ref, dw_ref, dsc_ref):
        g = pl.program_id(0)
        t = lax.broadcasted_iota(jnp.int32, (l, 1), 0)
        u = u_ref[...]
        cnt = _pool_count(g, t)
        pooled = _pool_select(g, _pool_windows(u, t, l, False)) / cnt - u
        dyv = dy_ref[...]
        dsc_ref[...] = jnp.sum(dyv * _dot(pooled, w_ref[0]), axis=0, keepdims=True)
        dyl = dyv * sc_ref[...]
        dw_ref[0] = _dot(pooled, dyl, _TN)
        dpooled = _dot(dyl, w_ref[0], _NT)
        du_ref[...] = (_pool_select(g, _pool_windows(dpooled / cnt, t, l, True)) - dpooled).astype(BF16)

    return pl.pallas_call(
        body, name="pool_bwd", grid=(4,), in_specs=_pool_specs(l) + [pl.BlockSpec((l, POOL_GW), lambda g: (0, g))],
        out_specs=[pl.BlockSpec((l, POOL_GW), lambda g: (0, g)), pl.BlockSpec((1, POOL_GW, POOL_GW), lambda g: (g, 0, 0)),
                   pl.BlockSpec((1, POOL_GW), lambda g: (0, g))],
        out_shape=[jax.ShapeDtypeStruct((l, POOL_W), BF16), jax.ShapeDtypeStruct((4, POOL_GW, POOL_GW), F32),
                   jax.ShapeDtypeStruct((1, POOL_W), F32)],
        compiler_params=_params(("parallel",)))(proj, w_pool, scale, dy)


ROWS_MERGE = 128
ROWS_MERGE_FWD = 256


def _merge_inputs(tr):
    def col(off, w):
        return pl.BlockSpec((tr, w), lambda i: (i, off // w))

    def act(w):
        return pl.BlockSpec((tr, w), lambda i: (i, 0))

    d = D_MODEL
    return ([col(OFF_GL, 3 * d), col(OFF_ZA, ATT_W), col(OFF_US, SSM_W), col(OFF_ZS, SSM_W), col(OFF_ZP, POOL_W),
             act(ATT_W), act(SSM_W), act(POOL_W)]
            + [_full((1, SSM_W)), _full((SSM_W, SSM_W)), _full((1, SSM_W)), _full((ATT_W, d)), _full((SSM_W, d)),
               _full((POOL_W, d)), _full((d, d)), _full((1, 3 * d))])


def _merge_forward_math(gl_ref, za_ref, us_ref, zs_ref, zp_ref, o_ref, yl_ref, yp_ref, d_ref, wg_ref, bg_ref,
                        wba_ref, wbs_ref, wbp_ref):
    d = D_MODEL
    r = {}
    r['sa'], r['dsa'] = _silu_and_grad(za_ref[...])
    r['ss'], r['dss'] = _silu_and_grad(zs_ref[...])
    r['sp'], r['dsp'] = _silu_and_grad(zp_ref[...])
    r['y_att'] = o_ref[...] * r['sa']
    y1 = yl_ref[...] + d_ref[...] * us_ref[...]
    r['y2'], r['dgelu'] = _gelu_and_grad(y1)
    r['sg'] = _sigmoid(_dot(r['y2'], wg_ref[...]) + bg_ref[...])
    r['y3'] = r['y2'] * r['sg']
    r['y_ssm'] = r['y3'] * r['ss']
    r['y_pool'] = yp_ref[...] * r['sp']
    r['g'] = [_sigmoid(gl_ref[:, b * d:(b + 1) * d]) for b in range(3)]
    r['p'] = [_dot(r['y_att'], wba_ref[...]), _dot(r['y_ssm'], wbs_ref[...]), _dot(r['y_pool'], wbp_ref[...])]
    r['merged'] = r['g'][0] * r['p'][0] + r['g'][1] * r['p'][1] + r['g'][2] * r['p'][2]
    return r


def _merge_fwd(proj, o, y_lin, y_pool, x, ssm_d, w_glu, b_glu, wba, wbs, wbp, w_out, mod):
    l, d = x.shape
    tr = min(ROWS_MERGE_FWD, l)

    def body(gl_ref, za_ref, us_ref, zs_ref, zp_ref, o_ref, yl_ref, yp_ref, d_ref, wg_ref, bg_ref, wba_ref, wbs_ref,
             wbp_ref, wo_ref, mod_ref, x_ref, xn_ref):
        r = _merge_forward_math(gl_ref, za_ref, us_ref, zs_ref, zp_ref, o_ref, yl_ref, yp_ref, d_ref, wg_ref, bg_ref,
                                wba_ref, wbs_ref, wbp_ref)
        xn_ref[...] = x_ref[...] + mod_ref[:, 2 * d:3 * d] * _dot(r['merged'], wo_ref[...])

    blk = pl.BlockSpec((tr, d), lambda i: (i, 0))
    return pl.pallas_call(
        body, name="merge_fwd", grid=(l // tr,), in_specs=_merge_inputs(tr) + [blk], out_specs=blk,
        out_shape=jax.ShapeDtypeStruct((l, d), F32), compiler_params=_params(("parallel",)))(
            proj, proj, proj, proj, proj, o, y_lin, y_pool, ssm_d, w_glu, b_glu, wba, wbs, wbp, w_out, mod, x)


def _merge_bwd(proj, o, y_lin, y_pool, dxo, ssm_d, w_glu, b_glu, wba, wbs, wbp, w_out, mod):
    l, d = dxo.shape
    tr = min(ROWS_MERGE, l)

    def body(gl_ref, za_ref, us_ref, zs_ref, zp_ref, o_ref, yl_ref, yp_ref, d_ref, wg_ref, bg_ref, wba_ref, wbs_ref,
             wbp_ref, wo_ref, mod_ref, dxo_ref,
             do_ref, dyl_ref, dus_ref, dyp_ref, dpj_ref,
             ya_ref, ys_ref, ypl_ref, y2_ref, dt_ref, dpa_ref, dps_ref, dpp_ref, mg_ref, dout_ref, st_ref):
        @pl.when(pl.program_id(0) == 0)
        def _():
            st_ref[...] = jnp.zeros_like(st_ref)

        for off, width in ((OFF_Q, ATT_W), (OFF_US, SSM_W), (OFF_UP, POOL_W), (OFF_KV, 2 * KV_W)):
            dpj_ref[:, off:off + width] = jnp.zeros((tr, width), BF16)

        r = _merge_forward_math(gl_ref, za_ref, us_ref, zs_ref, zp_ref, o_ref, yl_ref, yp_ref, d_ref, wg_ref, bg_ref,
                                wba_ref, wbs_ref, wbp_ref)
        dxov = dxo_ref[...]
        out = _dot(r['merged'], wo_ref[...])
        st_ref[0:1, :] += jnp.sum(dxov * out, axis=0, keepdims=True)
        dout = dxov * mod_ref[:, 2 * d:3 * d]
        dmerged = _dot(dout, wo_ref[...], _NT)
        dys = []
        for b, (w_ref, dp_ref) in enumerate(((wba_ref, dpa_ref), (wbs_ref, dps_ref), (wbp_ref, dpp_ref))):
            gb = r['g'][b]
            dpj_ref[:, OFF_GL + b * d:OFF_GL + (b + 1) * d] = (dmerged * r['p'][b] * gb * (1.0 - gb)).astype(BF16)
            dp = dmerged * gb
            dp_ref[...] = dp.astype(BF16)
            dys.append(_dot(dp, w_ref[...], _NT))
        mg_ref[...] = r['merged'].astype(BF16)
        dout_ref[...] = dout.astype(BF16)
        ya_ref[...] = r['y_att'].astype(BF16)
        ys_ref[...] = r['y_ssm'].astype(BF16)
        ypl_ref[...] = r['y_pool'].astype(BF16)
        y2_ref[...] = r['y2'].astype(BF16)
        do_ref[...] = (dys[0] * r['sa']).astype(BF16)
        dpj_ref[:, OFF_ZA:OFF_ZA + ATT_W] = (dys[0] * o_ref[...] * r['dsa']).astype(BF16)
        dy3 = dys[1] * r['ss']
        dpj_ref[:, OFF_ZS:OFF_ZS + SSM_W] = (dys[1] * r['y3'] * r['dss']).astype(BF16)
        dt = dy3 * r['y2'] * r['sg'] * (1.0 - r['sg'])
        dt_ref[...] = dt.astype(BF16)
        dy1 = (dy3 * r['sg'] + _dot(dt, wg_ref[...], _NT)) * r['dgelu']
        dyl_ref[...] = dy1.astype(BF16)
        dus_ref[...] = dy1 * d_ref[...]
        st_ref[1:2, 0:SSM_W] += jnp.sum(dt, axis=0, keepdims=True)
        st_ref[1:2, SSM_W:2 * SSM_W] += jnp.sum(dy1 * us_ref[...], axis=0, keepdims=True)
        dyp_ref[...] = dys[2] * r['sp']
        dpj_ref[:, OFF_ZP:OFF_ZP + POOL_W] = (dys[2] * yp_ref[...] * r['dsp']).astype(BF16)

    blk = pl.BlockSpec((tr, d), lambda i: (i, 0))
    half = pl.BlockSpec((tr, ATT_W), lambda i: (i, 0))
    wide = pl.BlockSpec((tr, IN_W), lambda i: (i, 0))
    sds = jax.ShapeDtypeStruct
    return pl.pallas_call(
        body, name="merge_bwd", grid=(l // tr,), in_specs=_merge_inputs(tr) + [blk],
        out_specs=[half] * 4 + [wide] + [half] * 5 + [blk] * 5 + [_full((SUBLANES, d))],
        out_shape=[sds((l, ATT_W), t) for t in (BF16, BF16, F32, F32)] + [sds((l, IN_W), BF16)]
        + [sds((l, ATT_W), BF16)] * 5 + [sds((l, d), BF16)] * 5 + [sds((SUBLANES, d), F32)],
        compiler_params=_params(("arbitrary",)))(
            proj, proj, proj, proj, proj, o, y_lin, y_pool, ssm_d, w_glu, b_glu, wba, wbs, wbp, w_out, mod, dxo)


def _ada_fwd(c_all, w_ada, b_shard):
    depth, d, n = w_ada.shape

    def body(c_ref, w_ref, b_ref, o_ref):
        act, _ = _silu_and_grad(c_ref[...])
        o_ref[0] = _dot(act, w_ref[0]) + b_ref[0]

    return pl.pallas_call(
        body, name="ada_fwd", grid=(depth,),
        in_specs=[_full((N_DEV, d)), pl.BlockSpec((1, d, n), lambda i: (i, 0, 0)), pl.BlockSpec((1, 1, n), lambda i: (i, 0, 0))],
        out_specs=pl.BlockSpec((1, N_DEV, n), lambda i: (i, 0, 0)), out_shape=jax.ShapeDtypeStruct((depth, N_DEV, n), F32),
        compiler_params=_params(("parallel",)))(c_all, w_ada, b_shard)


def _ada_bwd(c_all_t, dmod):
    d = c_all_t.shape[0]
    depth, _, n = dmod.shape

    def body(c_ref, dm_ref, o_ref):
        act, _ = _silu_and_grad(c_ref[...])
        acc = act[:, 0:1] * dm_ref[0, 0:1, :]
        for b in range(1, N_DEV):
            acc = acc + act[:, b:b + 1] * dm_ref[0, b:b + 1, :]
        o_ref[0] = acc

    return pl.pallas_call(
        body, name="ada_bwd", grid=(depth,),
        in_specs=[_full((d, N_DEV)), pl.BlockSpec((1, N_DEV, n), lambda i: (i, 0, 0))],
        out_specs=pl.BlockSpec((1, d, n), lambda i: (i, 0, 0)), out_shape=jax.ShapeDtypeStruct((depth, d, n), F32),
        compiler_params=_params(("parallel",)))(c_all_t, dmod)


ROWS_FLAT = 256


_RELATION_XOR = (0, 2, 1, 3)


def _sum_chips(own, others, chip_order, name):
    r, c = own.shape
    tr = math.gcd(ROWS_FLAT, r)

    def body(own_ref, oth_ref, o_ref):
        terms = [own_ref[...].astype(F32)] + [oth_ref[k].astype(F32) for k in range(3)]
        if chip_order:
            chip = 2 * lax.axis_index("x") + lax.axis_index("y")
            by_chip = []
            for q in range(N_CHIPS):
                rel = jnp.bitwise_xor(chip, q)
                pick = terms[3]
                for k in (2, 1, 0):
                    pick = jnp.where(rel == _RELATION_XOR[k], terms[k], pick)
                by_chip.append(pick)
            terms = by_chip
        o_ref[...] = ((terms[0] + terms[1]) + terms[2]) + terms[3]

    return pl.pallas_call(
        body, name=name, grid=(r // tr,),
        in_specs=[pl.BlockSpec((tr, c), lambda i: (i, 0)), pl.BlockSpec((3, tr, c), lambda i: (0, i, 0))],
        out_specs=pl.BlockSpec((tr, c), lambda i: (i, 0)), out_shape=jax.ShapeDtypeStruct((r, c), F32),
        compiler_params=_params(("parallel",)))(own, others)


def _add2(a, b, out_dtype, name):
    shape = a.shape
    a, b = a.reshape(-1, shape[-1]), b.reshape(-1, shape[-1])
    r, c = a.shape
    tr = math.gcd(ROWS_FLAT, r)

    def body(a_ref, b_ref, o_ref):
        o_ref[...] = (a_ref[...] + b_ref[...]).astype(out_dtype)

    blk = pl.BlockSpec((tr, c), lambda i: (i, 0))
    return pl.pallas_call(
        body, name=name, grid=(r // tr,), in_specs=[blk, blk], out_specs=blk,
        out_shape=jax.ShapeDtypeStruct((r, c), out_dtype), compiler_params=_params(("parallel",)))(a, b).reshape(shape)


def _adamw(w, g, m, v, name):
    r, c = w.shape
    tr = math.gcd(ROWS_FLAT, r)

    def body(w_ref, g_ref, m_ref, v_ref, d_ref, nm_ref, nv_ref):
        gv = g_ref[...]
        mv = ADAM_B1 * m_ref[...] + (1.0 - ADAM_B1) * gv
        vv = ADAM_B2 * v_ref[...] + (1.0 - ADAM_B2) * (gv * gv)
        m_hat = mv / (1.0 - ADAM_B1 ** ADAM_STEP)
        v_hat = vv / (1.0 - ADAM_B2 ** ADAM_STEP)
        d_ref[...] = -ADAM_LR * (m_hat / (jnp.sqrt(v_hat) + ADAM_EPS) + ADAM_WD * w_ref[...])
        nm_ref[...] = mv
        nv_ref[...] = vv

    blk = pl.BlockSpec((tr, c), lambda i: (i, 0))
    return pl.pallas_call(
        body, name=name, grid=(r // tr,), in_specs=[blk] * 4, out_specs=[blk] * 3,
        out_shape=[jax.ShapeDtypeStruct((r, c), F32)] * 3, compiler_params=_params(("parallel",)))(w, g, m, v)


_GROUP_MASKS = {
    'xy': ((1, 0, 0), (0, 1, 0), (1, 1, 0)),
    'c': ((0, 0, 1),),
    'xyc': ((0, 0, 1), (0, 1, 0), (0, 1, 1), (1, 0, 0), (1, 0, 1), (1, 1, 0), (1, 1, 1)),
}


def _group_rank(group, pos):
    x, y, c = pos
    return {'xy': 2 * x + y, 'c': c, 'xyc': 4 * x + 2 * y + c}[group]


def _exchange(src, group, scatter, name):
    masks = _GROUP_MASKS[group]
    n = len(masks) + 1
    shape = src.shape[1:] if scatter else src.shape
    assert (not scatter) or src.shape[0] == n

    def body(src_ref, out_ref, send_sems, recv_sems, local_sem):
        me = (lax.axis_index("x"), lax.axis_index("y"), lax.axis_index("c"))
        my_rank = _group_rank(group, me)

        def mine_for(rank):
            return src_ref.at[rank] if scatter else src_ref

        local = pltpu.make_async_copy(mine_for(my_rank), out_ref.at[my_rank], local_sem)
        local.start()
        sends = []
        for k, mask in enumerate(masks):
            peer = tuple(1 - p if f else p for p, f in zip(me, mask))
            peer_rank = _group_rank(group, peer)
            send = pltpu.make_async_remote_copy(
                src_ref=mine_for(peer_rank), dst_ref=out_ref.at[my_rank], send_sem=send_sems.at[k],
                recv_sem=recv_sems.at[k], device_id=peer, device_id_type=pl.DeviceIdType.MESH)
            send.start()
            sends.append((send, peer, peer_rank))
        for k, (send, peer, peer_rank) in enumerate(sends):
            pltpu.make_async_remote_copy(
                src_ref=mine_for(peer_rank), dst_ref=out_ref.at[peer_rank], send_sem=send_sems.at[k],
                recv_sem=recv_sems.at[k], device_id=peer, device_id_type=pl.DeviceIdType.MESH).wait_recv()
        for send, _, _ in sends:
            send.wait_send()
        local.wait()

    return pl.pallas_call(
        body, name=name, in_specs=[pl.BlockSpec(memory_space=pl.ANY)], out_specs=pl.BlockSpec(memory_space=pl.ANY),
        out_shape=jax.ShapeDtypeStruct((n,) + tuple(shape), src.dtype),
        scratch_shapes=[pltpu.SemaphoreType.DMA((n - 1,)), pltpu.SemaphoreType.DMA((n - 1,)), pltpu.SemaphoreType.DMA(())],
    )(src)


CHUNK_BYTES = 1 << 20
MIN_CHUNK_ROWS = 64


def _row_parts(rows, cols, itemsize):
    n = 1
    while rows % (2 * n) == 0 and rows // (2 * n) >= MIN_CHUNK_ROWS and rows * cols * itemsize > n * CHUNK_BYTES:
        n *= 2
    return n


def _remote(src, dst, send_sem, recv_sem, peer):
    return pltpu.make_async_remote_copy(src_ref=src, dst_ref=dst, send_sem=send_sem, recv_sem=recv_sem,
                                        device_id=peer, device_id_type=pl.DeviceIdType.MESH)


def _start_rows(src, dst, send_sem, recv_sem, peer):
    rows, cols = src.shape
    n = _row_parts(rows, cols, jnp.dtype(src.dtype).itemsize)
    pr = rows // n
    for i in range(n):
        _remote(src.at[pl.ds(i * pr, pr), :], dst.at[pl.ds(i * pr, pr), :], send_sem, recv_sem, peer).start()


def _mesh_place():
    x, y, c = lax.axis_index("x"), lax.axis_index("y"), lax.axis_index("c")
    other_chips = ((1 - x, y), (x, 1 - y), (1 - x, 1 - y))
    return x, y, c, 2 * x + y, (x, y, 1 - c), other_chips


def _comm_call(body, name, ins, out_shapes, sem_counts):
    any_spec = pl.BlockSpec(memory_space=pl.ANY)
    return pl.pallas_call(
        body, name=name, in_specs=[any_spec] * len(ins), out_specs=[any_spec] * len(out_shapes), out_shape=out_shapes,
        scratch_shapes=[pltpu.SemaphoreType.DMA((n,)) for n in sem_counts])(*ins)


class _ChipSends:
    def __init__(self, srcs, pick):
        self.srcs, self.pick = list(srcs), pick
        nt = len(self.srcs)
        self.n_sems = 3 * nt
        self.out_shapes = []
        for s in self.srcs:
            r2 = s.shape[-2] // 2 if pick is _pick_weight_half else s.shape[-2]
            self.out_shapes.append(jax.ShapeDtypeStruct((3, r2, s.shape[-1]), s.dtype))

    def start(self, ins, outs, send_sems, recv_sems):
        _, _, c, _, _, other_chips = _mesh_place()
        for t in range(len(ins)):
            for j, (px, py) in enumerate(other_chips):
                _remote(self.pick(ins[t], c, 2 * px + py), outs[t].at[j], send_sems.at[3 * t + j], recv_sems.at[3 * t + j],
                        (px, py, c)).start()

    def finish(self, ins, outs, send_sems, recv_sems):
        _, _, c, _, _, other_chips = _mesh_place()
        for t in range(len(ins)):
            for j, (px, py) in enumerate(other_chips):
                landed = _remote(outs[t].at[j], outs[t].at[j], send_sems.at[3 * t + j], recv_sems.at[3 * t + j], (px, py, c))
                landed.wait_recv()
                landed.wait_send()

    def run(self, name):
        nt = len(self.srcs)

        def body(*refs):
            ins, outs = refs[:nt], refs[nt:2 * nt]
            self.start(ins, outs, *refs[2 * nt:])
            self.finish(ins, outs, *refs[2 * nt:])

        return _comm_call(body, name, self.srcs, self.out_shapes, (self.n_sems, self.n_sems))

    def specs(self):
        any_spec = pl.BlockSpec(memory_space=pl.ANY)
        return [any_spec] * len(self.srcs), [any_spec] * len(self.srcs), [pltpu.SemaphoreType.DMA((self.n_sems,))] * 2


def _pick_weight_half(src, core, chip):
    r2 = src.shape[0] // 2
    return src.at[pl.ds(pl.multiple_of(core * r2, MIN_CHUNK_ROWS), r2), :]


def _pick_slab(src, core, chip):
    return src.at[chip if src.shape[0] == N_CHIPS else 0]


def _swap_halves(gs):
    nt = len(gs)

    def body(*refs):
        ins, outs = refs[:nt], refs[nt:2 * nt]
        send_sems, recv_sems = refs[2 * nt:]
        _, _, c, _, sibling, _ = _mesh_place()
        for t in range(nt):
            for q in range(ins[t].shape[1]):
                _start_rows(ins[t].at[1 - c, q], outs[t].at[q], send_sems.at[t], recv_sems.at[t], sibling)
        for t in range(nt):
            _remote(outs[t], outs[t], send_sems.at[t], recv_sems.at[t], sibling).wait_recv()
        for t in range(nt):
            _remote(outs[t], outs[t], send_sems.at[t], recv_sems.at[t], sibling).wait_send()

    out_shapes = [jax.ShapeDtypeStruct(g.shape[1:], g.dtype) for g in gs]
    return _comm_call(body, "swap_halves", gs, out_shapes, (nt, nt))


def _swap_cores(halves):
    nt = len(halves)

    def body(*refs):
        ins, outs = refs[:nt], refs[nt:2 * nt]
        send_sems, recv_sems = refs[2 * nt:]
        _, _, _, _, sibling, _ = _mesh_place()
        for t in range(nt):
            for li in range(ins[t].shape[0]):
                _start_rows(ins[t].at[li], outs[t].at[li], send_sems.at[t], recv_sems.at[t], sibling)
        for t in range(nt):
            _remote(outs[t], outs[t], send_sems.at[t], recv_sems.at[t], sibling).wait_recv()
        for t in range(nt):
            _remote(ins[t], ins[t], send_sems.at[t], recv_sems.at[t], sibling).wait_send()

    out_shapes = [jax.ShapeDtypeStruct(h.shape, h.dtype) for h in halves]
    return _comm_call(body, "swap_cores", halves, out_shapes, (nt, nt))


def _pack_rows(pieces, cols, dtype):
    tile = SUBLANES * cols
    rows = []
    for p in pieces:
        flat = p.reshape(-1).astype(dtype)
        rows.append(jnp.pad(flat, (0, (-flat.shape[0]) % tile)).reshape(-1, cols))
    total = sum(r.shape[0] for r in rows)
    if total % (2 * SUBLANES):
        rows.append(jnp.zeros((SUBLANES, cols), dtype))
    return jnp.concatenate(rows, axis=0)


def _unpack_rows(buf, shapes):
    out, row = [], 0
    for s in shapes:
        n = math.prod(s)
        nrows = -(-n // (SUBLANES * buf.shape[1])) * SUBLANES
        out.append(buf[row:row + nrows].reshape(-1)[:n].reshape(s))
        row += nrows
    return out


ROW_SHARDED = ('w_out', 'w_glu')


def _chips_to_full(name, chips):
    mats = [c.reshape(-1, c.shape[-1]) for c in chips]
    if name in ROW_SHARDED:
        return jnp.concatenate(mats, axis=0)
    if name != 'w_in':
        return jnp.concatenate(mats, axis=1)
    cs = mats[0].shape[1]
    pieces = []
    for start, size in _PERM_PIECES:
        lo = start
        while lo < start + size:
            q = lo // cs
            hi = min(start + size, (q + 1) * cs)
            pieces.append(mats[q][:, lo - q * cs:hi - q * cs])
            lo = hi
    return jnp.concatenate(pieces, axis=1)


def _full_to_dests(name, full):
    rows, cols = full.shape
    if name in ROW_SHARDED:
        return jnp.transpose(full.reshape(N_CHIPS, 2, rows // (2 * N_CHIPS), cols), (1, 0, 2, 3))
    cs = cols // N_CHIPS
    if name != 'w_in':
        dests = [full[:, q * cs:(q + 1) * cs] for q in range(N_CHIPS)]
    else:
        offsets, off = [], 0
        for _, size in _PERM_PIECES:
            offsets.append(off)
            off += size
        by_start = sorted(zip(_PERM_PIECES, offsets))
        dests = []
        for q in range(N_CHIPS):
            parts = []
            for (start, size), at in by_start:
                lo, hi = max(start, q * cs), min(start + size, (q + 1) * cs)
                if lo < hi:
                    parts.append(full[:, at + lo - start:at + hi - start])
            dests.append(jnp.concatenate(parts, axis=1))
    return jnp.stack([d.reshape(2, rows // 2, cs) for d in dests], axis=1)


def _by_chip(own, related, x, y):
    grid = ((own, related[1]), (related[0], related[2]))
    along_x = [[jnp.where(x == 0, grid[px][dy], grid[1 - px][dy]) for dy in range(2)] for px in range(2)]
    return [jnp.where(y == 0, along_x[px][py], along_x[px][1 - py]) for px in range(2) for py in range(2)]


def _join_halves(mine, theirs, core, axis):
    return jnp.where(core == 0, jnp.concatenate([mine, theirs], axis=axis), jnp.concatenate([theirs, mine], axis=axis))


ROWS_WGRAD = 2048
SSM_RAW = ('ssm_a_re', 'ssm_a_im', 'ssm_log_dt', 'ssm_b_re', 'ssm_b_im', 'ssm_c_re', 'ssm_c_im')


def _all_tables(p):
    raw = tuple(p[k] for k in SSM_RAW)
    (_, bb, cb), vjp = jax.vjp(jax.vmap(_ssm_tables), *raw)
    return (bb.astype(BF16), cb.astype(BF16), jax.vmap(_ssm_powers)(*raw[:3])), vjp


def _layer_fwd(x, p, w_in, rest, li, mod, tables, sends=None):
    bb, cb, pw = (t[li] for t in tables)
    h = _norm_fwd(x, _row(p['norm_g'][li]), mod)
    if isinstance(rest, dict):
        proj = _matmul(h, w_in, 'nn', tm=512, tn=1280, tk=1024, name="proj", n_outer=True)
    else:
        proj, arrived = _matmul(h, w_in, 'nn', tm=512, tn=1280, tk=1024, name="proj_sending", n_outer=True, sends=rest[0])
        rest = rest[1](arrived)
    full = dict(rest, w_in=w_in)
    o = _attention_fwd(proj, p['attn_sinks'][li])
    y_lin, states, brought = _ssm_fwd(proj, bb, cb, pw, sends)
    y_pool = _pool_fwd(proj, p['w_pool'][li].astype(BF16), _row(p['pool_scale'][li]))
    x_new = _merge_fwd(proj, o, y_lin, y_pool, x, _row(p['ssm_d'][li]), full['w_glu'], _row(p['b_glu'][li]),
                       full['w_br_att'], full['w_br_ssm'], full['w_br_pool'], full['w_out'], mod)
    saved = dict(x=x, h=h, proj=proj, o=o, y_lin=y_lin, y_pool=y_pool, states=states, bb=bb, cb=cb, pw=pw, full=full)
    return x_new, saved, brought


def _layer_bwd(dxo, s, p, li, mod, sends=None, w_in_sender=None):
    l = dxo.shape[0]
    proj, full = s['proj'], s['full']
    (do, dyl, dus_skip, dyp, dproj, ya, ys, ypl, y2, dt, dpa, dps, dpp, mg, dout, st) = _merge_bwd(
        proj, s['o'], s['y_lin'], s['y_pool'], dxo, _row(p['ssm_d'][li]), full['w_glu'], _row(p['b_glu'][li]),
        full['w_br_att'], full['w_br_ssm'], full['w_br_pool'], full['w_out'], mod)
    g = {}
    g['w_br_att'] = _matmul(ya, dpa, 'tn', tm=512, tn=1024, tk=ROWS_WGRAD, name="grad_w_br")
    g['w_br_ssm'] = _matmul(ys, dps, 'tn', tm=512, tn=1024, tk=ROWS_WGRAD, name="grad_w_br")
    g['w_br_pool'] = _matmul(ypl, dpp, 'tn', tm=512, tn=1024, tk=ROWS_WGRAD, name="grad_w_br")
    g['w_out'] = _matmul(mg, dout, 'tn', tm=512, tn=1024, tk=ROWS_WGRAD, name="grad_w_out")
    g['w_glu'] = _matmul(y2, dt, 'tn', tm=512, tn=512, tk=ROWS_WGRAD, name="grad_w_glu")
    g['b_glu'] = st[1, 0:SSM_W]
    g['ssm_d'] = st[1, SSM_W:2 * SSM_W]
    dgate = st[0]

    dq, dkv, dsink = _attention_bwd(proj, p['attn_sinks'][li], do)
    g['attn_sinks'] = dsink[:, 0]
    dus_scan, dbb, dcb, dlam, brought = _ssm_bwd(proj, s['states'], dyl, s['bb'], s['cb'], s['pw'], sends)
    g['ssm_tables'] = (dlam[:, 0:2, :], dbb, dcb)
    dup, dwp, dps_scale = _pool_bwd(proj, p['w_pool'][li].astype(BF16), _row(p['pool_scale'][li]), dyp)
    g['w_pool'] = dwp
    g['pool_scale'] = dps_scale[0]

    for off, piece in ((OFF_Q, dq), (OFF_US, dus_skip + dus_scan), (OFF_UP, dup), (OFF_KV, dkv[HALO:])):
        dproj = lax.dynamic_update_slice(dproj, piece.astype(BF16), (0, off))
    g['w_in'] = _matmul(s['h'], dproj, 'tn', tm=1024, tn=1280, tk=ROWS_WGRAD, name="grad_w_in")
    if w_in_sender is None:
        dh, brought_late = _matmul(dproj, full['w_in'], 'nt', tm=1024, tn=1024, tk=1280, name="grad_h"), None
    else:
        dh, brought_late = _matmul(dproj, full['w_in'], 'nt', tm=1024, tn=1024, tk=1280, name="grad_h_sending",
                                   sends=w_in_sender(g['w_in']))
    dx, nst = _norm_bwd(s['x'], dh, dxo, _row(p['norm_g'][li]), mod)
    g['norm_g'] = nst[2]
    dmod = jnp.concatenate([nst[0], nst[1], dgate])
    del l
    return dx, g, dmod, brought, brought_late


BIG_NAMES = tuple(name for name, _ in BIG)


def _weight_sends(shards, names, li):
    return _ChipSends([shards[name][li] for name in names], _pick_weight_half)


def _assemble_layer(shards, names, li, over_ici, place):
    x, y, core, _ = place
    from_sibling = _swap_cores(over_ici)
    full = {}
    for name, a, b in zip(names, over_ici, from_sibling):
        own = shards[name][li]
        own_halves = own.reshape(2, own.shape[0] // 2, own.shape[1])
        halves = [_join_halves(a[j][None], b[j][None], core, 0) for j in range(3)]
        full[name] = _chips_to_full(name, _by_chip(own_halves, halves, x, y))
    return full


def _pair_sums(tensors, wire, core):
    theirs = _swap_halves(tensors)
    return [_add2(lax.dynamic_index_in_dim(g, core, 0, keepdims=False), a, dt, "sum_core_pair")
            for g, a, dt in zip(tensors, theirs, wire)]


def _layer_grad_halves(layer_grads, names):
    return [_full_to_dests(name, layer_grads[name]) for name in names]


def _step(p, m, v, x, c, target):
    depth = p['norm_g'].shape[0]
    d = D_MODEL
    ix, iy, ic = lax.axis_index("x"), lax.axis_index("y"), lax.axis_index("c")
    chip = 2 * ix + iy
    dev = 4 * ix + 2 * iy + ic
    x0 = x[0]

    c_pad = jnp.pad(c, ((0, SUBLANES - 1), (0, 0)))
    c_all = _exchange(c_pad, 'xyc', False, "gather_c")[:, 0, :]
    n_ada = p['w_ada'].shape[-1]
    b_shard = lax.dynamic_slice_in_dim(p['b_ada'], chip * n_ada, n_ada, axis=1)[:, None, :]
    mod_shard = _ada_fwd(c_all, p['w_ada'].astype(BF16), b_shard)
    mod_all = _exchange(mod_shard.reshape(depth * N_DEV, n_ada), 'xy', False, "gather_mod")
    mod_all = jnp.transpose(mod_all.reshape(N_CHIPS, depth, N_DEV, n_ada), (1, 2, 0, 3)).reshape(depth, N_DEV, 3 * d)
    mods = lax.dynamic_index_in_dim(mod_all, dev, axis=1, keepdims=True)

    place = (ix, iy, ic, chip)
    n_big = len(BIG)

    shards = {name: p[name].astype(BF16) for name, _ in BIG}
    tables, tables_vjp = _all_tables(p)
    others = BIG_NAMES[1:]
    w_in = _assemble_layer(shards, BIG_NAMES[:1], 0, _weight_sends(shards, BIG_NAMES[:1], 0).run("send_weight_halves"),
                           place)['w_in']
    rest = (_weight_sends(shards, others, 0), lambda arrived: _assemble_layer(shards, others, 0, arrived, place))
    saved = []
    xs = x0
    for li in range(depth):
        sends = _weight_sends(shards, BIG_NAMES, li + 1) if li + 1 < depth else None
        xs, s, arrived = _layer_fwd(xs, p, w_in, rest, li, mods[li], tables, sends)
        saved.append(s)
        if sends is not None:
            rest = _assemble_layer(shards, BIG_NAMES, li + 1, arrived, place)
            w_in = rest.pop('w_in')
    dx, hst = _loss_head(xs, _row(p['final_g']), target[0])
    loss = lax.psum(hst[1, 0], ("x", "y", "c"))

    grads, dmods, pair, from_chips = [None] * depth, [None] * depth, [None] * depth, [None] * depth
    first_w_in = {}

    def send_first_w_in(g_w_in):
        first_w_in['pair'] = _pair_sums([_full_to_dests('w_in', g_w_in)], [BF16], ic)
        return _ChipSends(first_w_in['pair'], _pick_slab)

    sends = None
    for li in reversed(range(depth)):
        dx, grads[li], dmods[li], arrived, arrived_late = _layer_bwd(dx, saved[li], p, li, mods[li], sends,
                                                                     send_first_w_in if li == 0 else None)
        if sends is not None:
            from_chips[li + 1] = arrived
        if li > 0:
            pair[li] = _pair_sums(_layer_grad_halves(grads[li], BIG_NAMES), [BF16] * n_big, ic)
            sends = _ChipSends(pair[li], _pick_slab)

    dmod_pad = jnp.pad(jnp.stack(dmods), ((0, SUBLANES - depth), (0, 0)))
    dmod_all = _exchange(dmod_pad, 'xyc', False, "gather_dmod")[:, :depth, :]
    dmod_cols = lax.dynamic_slice_in_dim(jnp.transpose(dmod_all, (1, 0, 2)), chip * n_ada, n_ada, axis=2)
    g_w_ada = _ada_bwd(jnp.transpose(c_all), dmod_cols)

    local_small = {k: jnp.stack([grads[li][k] for li in range(depth)])
                   for k in SMALL if k not in ('final_g', 'b_ada') + SSM_RAW}
    table_cotangents = tuple(jnp.stack([grads[li]['ssm_tables'][k] for li in range(depth)]) for k in range(3))
    local_small.update(zip(SSM_RAW, tables_vjp(table_cotangents)))
    local_small['final_g'] = hst[0]
    local_small['b_ada'] = jnp.stack(dmods)
    small_pack = _pack_rows([local_small[k] for k in SMALL], PACK_COLS, F32)

    small_halves = small_pack.reshape(2, 1, small_pack.shape[0] // 2, small_pack.shape[1])
    late = _pair_sums(_layer_grad_halves(grads[0], others) + [small_halves], [BF16] * len(others) + [F32], ic)
    pair[0] = first_w_in['pair'] + late
    from_chips[0] = list(arrived_late) + list(_ChipSends(late, _pick_slab).run("scatter_chips"))
    halves = []
    for li in range(depth):
        for t, (ps, others) in enumerate(zip(pair[li], from_chips[li])):
            small = t == n_big
            own = ps[0] if small else lax.dynamic_index_in_dim(ps, chip, 0, keepdims=False)
            halves.append(_sum_chips(own, others, small, "sum_chips_small" if small else "sum_chips")[None])
    totals = [_join_halves(h[0], o[0], ic, 0) for h, o in zip(halves, _swap_cores(halves))]
    small_sum = totals[n_big]
    per_layer = [totals[:n_big]] + [totals[n_big + 1 + (li - 1) * n_big:n_big + 1 + li * n_big] for li in range(1, depth)]

    small_shapes = [p[k].shape for k in SMALL]
    grad = dict(zip(SMALL, _unpack_rows(small_sum, small_shapes)))
    grad['w_ada'] = g_w_ada
    for t, (name, _) in enumerate(BIG):
        grad[name] = jnp.stack([per_layer[li][t] for li in range(depth)])

    delta, new_m, new_v = {}, {}, {}
    outs = _adamw(_pack_rows([p[k] for k in SMALL], PACK_COLS, F32), small_sum,
                  _pack_rows([m[k] for k in SMALL], PACK_COLS, F32), _pack_rows([v[k] for k in SMALL], PACK_COLS, F32),
                  name="adamw_small")
    for res, o in zip((delta, new_m, new_v), outs):
        res.update(zip(SMALL, _unpack_rows(o, small_shapes)))
    for name in ['w_ada'] + [n for n, _ in BIG]:
        shape = p[name].shape
        two_d = (-1, shape[-1])
        outs = _adamw(p[name].reshape(two_d), grad[name].reshape(two_d), m[name].reshape(two_d), v[name].reshape(two_d),
                      name="adamw_" + name)
        delta[name], new_m[name], new_v[name] = (o.reshape(shape) for o in outs)

    return (loss, dx[None], *[grad[k] for k in WEIGHTS], *[delta[k] for k in WEIGHTS],
            *[new_m[k] for k in WEIGHTS], *[new_v[k] for k in WEIGHTS])


def kernel(x, c, norm_g, w_ada, b_ada, w_in, attn_sinks, ssm_a_re, ssm_a_im, ssm_log_dt, ssm_b_re, ssm_b_im, ssm_c_re, ssm_c_im, ssm_d, w_glu, b_glu, w_pool, pool_scale, w_br_att, w_br_ssm, w_br_pool, w_out, final_g, loss_target, m_norm_g, m_w_ada, m_b_ada, m_w_in, m_attn_sinks, m_ssm_a_re, m_ssm_a_im, m_ssm_log_dt, m_ssm_b_re, m_ssm_b_im, m_ssm_c_re, m_ssm_c_im, m_ssm_d, m_w_glu, m_b_glu, m_w_pool, m_pool_scale, m_w_br_att, m_w_br_ssm, m_w_br_pool, m_w_out, m_final_g, v_norm_g, v_w_ada, v_b_ada, v_w_in, v_attn_sinks, v_ssm_a_re, v_ssm_a_im, v_ssm_log_dt, v_ssm_b_re, v_ssm_b_im, v_ssm_c_re, v_ssm_c_im, v_ssm_d, v_w_glu, v_b_glu, v_w_pool, v_pool_scale, v_w_br_att, v_w_br_ssm, v_w_br_pool, v_w_out, v_final_g):
    p = dict(zip(WEIGHTS, (norm_g, w_ada, b_ada, w_in, attn_sinks, ssm_a_re, ssm_a_im, ssm_log_dt, ssm_b_re, ssm_b_im,
                           ssm_c_re, ssm_c_im, ssm_d, w_glu, b_glu, w_pool, pool_scale, w_br_att, w_br_ssm, w_br_pool,
                           w_out, final_g)))
    m = dict(zip(WEIGHTS, (m_norm_g, m_w_ada, m_b_ada, m_w_in, m_attn_sinks, m_ssm_a_re, m_ssm_a_im, m_ssm_log_dt,
                           m_ssm_b_re, m_ssm_b_im, m_ssm_c_re, m_ssm_c_im, m_ssm_d, m_w_glu, m_b_glu, m_w_pool,
                           m_pool_scale, m_w_br_att, m_w_br_ssm, m_w_br_pool, m_w_out, m_final_g)))
    v = dict(zip(WEIGHTS, (v_norm_g, v_w_ada, v_b_ada, v_w_in, v_attn_sinks, v_ssm_a_re, v_ssm_a_im, v_ssm_log_dt,
                           v_ssm_b_re, v_ssm_b_im, v_ssm_c_re, v_ssm_c_im, v_ssm_d, v_w_glu, v_b_glu, v_w_pool,
                           v_pool_scale, v_w_br_att, v_w_br_ssm, v_w_br_pool, v_w_out, v_final_g)))
    return _step(p, m, v, x, c, loss_target)
```

```python
import functools
import math

import jax
import jax.numpy as jnp
from jax import lax
from jax.experimental import pallas as pl
from jax.experimental.pallas import tpu as pltpu

F32 = jnp.float32
BF16 = jnp.bfloat16

D_MODEL = 1024
CHUNK = 64
N_HEADS = 8
N_KV_HEADS = 2
HEAD_DIM = 64
Q_PER_KV = N_HEADS // N_KV_HEADS
HALO = 128
ATT_W = 512
KV_W = 128
SSM_W = 512
SSM_GROUP = 16
SSM_GROUPS = 32
SSM_STATE = 64
POOL_W = 512
POOL_GW = 128
IN_W = 6400
EPS = 1e-6
NEG_INF = -1e30
ADAM_LR = 0.001
ADAM_B1 = 0.9
ADAM_B2 = 0.999
ADAM_EPS = 1e-08
ADAM_WD = 0.01
ADAM_STEP = 10

OFF_GL, OFF_Q, OFF_ZA, OFF_US, OFF_ZS, OFF_UP, OFF_ZP, OFF_KV = 0, 3072, 3584, 4096, 4608, 5120, 5632, 6144
_PERM_PIECES = ((3328, 3072), (0, 512), (1792, 512), (768, 512), (2304, 512), (1280, 512), (2816, 512), (512, 256))

LANES = 128
SUBLANES = 8
N_SBLK = SSM_GROUPS * SSM_STATE // LANES
VMEM_LIMIT = 48 * 1024 * 1024

N_CHIPS = 4
N_DEV = 8

WEIGHTS = ['norm_g', 'w_ada', 'b_ada', 'w_in', 'attn_sinks', 'ssm_a_re', 'ssm_a_im', 'ssm_log_dt', 'ssm_b_re',
           'ssm_b_im', 'ssm_c_re', 'ssm_c_im', 'ssm_d', 'w_glu', 'b_glu', 'w_pool', 'pool_scale', 'w_br_att',
           'w_br_ssm', 'w_br_pool', 'w_out', 'final_g']
SMALL = ['norm_g', 'b_ada', 'attn_sinks', 'ssm_a_re', 'ssm_a_im', 'ssm_log_dt', 'ssm_b_re', 'ssm_b_im', 'ssm_c_re',
         'ssm_c_im', 'ssm_d', 'b_glu', 'w_pool', 'pool_scale', 'final_g']
BIG = (('w_in', (1024, 1600)), ('w_br_att', (512, 256)), ('w_br_ssm', (512, 256)), ('w_br_pool', (512, 256)),
       ('w_out', (256, 1024)), ('w_glu', (128, 512)))
PACK_COLS = 1024


def _params(sem=None):
    return pltpu.CompilerParams(dimension_semantics=sem, vmem_limit_bytes=VMEM_LIMIT)


def _row(v):
    return v.reshape(1, -1)


def _full(shape):
    nd = len(shape)
    return pl.BlockSpec(shape, lambda *_: (0,) * nd)


def _sigmoid(v):
    return 1.0 / (1.0 + jnp.exp(-v))


def _silu_and_grad(z):
    s = _sigmoid(z)
    return z * s, s * (1.0 + z * (1.0 - s))


_GELU_K = math.sqrt(2.0 / math.pi)


def _gelu_and_grad(v):
    inner = _GELU_K * (v + 0.044715 * v * v * v)
    th = jnp.tanh(inner)
    val = 0.5 * v * (1.0 + th)
    grad = 0.5 * (1.0 + th) + 0.5 * v * (1.0 - th * th) * _GELU_K * (1.0 + 3 * 0.044715 * v * v)
    return val, grad


_NN = (((1,), (0,)), ((), ()))
_NT = (((1,), (1,)), ((), ()))
_TN = (((0,), (0,)), ((), ()))


def _dot(a, b, dims=_NN):
    return lax.dot_general(a.astype(BF16), b.astype(BF16), dims, preferred_element_type=F32)


def _matmul(a, b, mode, *, tm, tn, tk, name, out_dtype=F32, n_outer=False, sends=None):
    if mode == 'nn':
        (m, k), (_, n) = a.shape, b.shape
    elif mode == 'nt':
        (m, k), (n, _) = a.shape, b.shape
    else:
        (k, m), (_, n) = a.shape, b.shape
    tm, tn, tk = min(tm, m), min(tn, n), min(tk, k)
    assert m % tm == 0 and n % tn == 0 and k % tk == 0, (name, a.shape, b.shape)
    nk = k // tk
    dims = {'nn': _NN, 'nt': _NT, 'tn': _TN}[mode]

    def body(a_ref, b_ref, o_ref, acc_ref):
        if nk == 1:
            o_ref[...] = _dot(a_ref[...], b_ref[...], dims).astype(out_dtype)
            return
        kk = pl.program_id(2)

        @pl.when(kk == 0)
        def _():
            acc_ref[...] = jnp.zeros_like(acc_ref)

        acc_ref[...] += _dot(a_ref[...], b_ref[...], dims)

        @pl.when(kk == nk - 1)
        def _():
            o_ref[...] = acc_ref[...].astype(out_dtype)

    def spec(shape, index):
        if n_outer:
            return pl.BlockSpec(shape, lambda j, i, kk: index(i, j, kk))
        return pl.BlockSpec(shape, index)

    a_spec = spec((tk, tm), lambda i, j, kk: (kk, i)) if mode == 'tn' else spec((tm, tk), lambda i, j, kk: (i, kk))
    b_spec = spec((tn, tk), lambda i, j, kk: (j, kk)) if mode == 'nt' else spec((tk, tn), lambda i, j, kk: (kk, j))
    grid = (n // tn, m // tm, nk) if n_outer else (m // tm, n // tn, nk)
    body, x_in, x_out, x_shapes, x_scratch, x_args = _carry_sends(body, 2, 1, 1, grid, sends)
    semantics = ("parallel", "parallel", "arbitrary") if sends is None else ("arbitrary",) * 3
    outs = pl.pallas_call(
        body, name=name, grid=grid, in_specs=[a_spec, b_spec] + x_in,
        out_specs=[spec((tm, tn), lambda i, j, kk: (i, j))] + x_out,
        out_shape=[jax.ShapeDtypeStruct((m, n), out_dtype)] + x_shapes,
        scratch_shapes=[pltpu.VMEM((tm, tn), F32)] + x_scratch, compiler_params=_params(semantics))(a, b, *x_args)
    return outs[0] if sends is None else (outs[0], list(outs[1:]))


ROWS_NORM = 512


def _norm_fwd(x, g, mod):
    l, d = x.shape
    tr = min(ROWS_NORM, l)

    def body(x_ref, g_ref, mod_ref, h_ref):
        xv = x_ref[...]
        r = lax.rsqrt(jnp.mean(xv * xv, axis=-1, keepdims=True) + EPS)
        shift, scale = mod_ref[:, 0:d], mod_ref[:, d:2 * d]
        h_ref[...] = ((xv * r * g_ref[...]) * (1.0 + scale) + shift).astype(BF16)

    return pl.pallas_call(
        body, name="norm_fwd", grid=(l // tr,),
        in_specs=[pl.BlockSpec((tr, d), lambda i: (i, 0)), _full((1, d)), _full((1, 3 * d))],
        out_specs=pl.BlockSpec((tr, d), lambda i: (i, 0)), out_shape=jax.ShapeDtypeStruct((l, d), BF16),
        compiler_params=_params(("parallel",)))(x, g, mod)


def _norm_bwd(x, dh, dxo, g, mod):
    l, d = x.shape
    tr = min(ROWS_NORM, l)

    def body(x_ref, dh_ref, dxo_ref, g_ref, mod_ref, dx_ref, st_ref):
        @pl.when(pl.program_id(0) == 0)
        def _():
            st_ref[...] = jnp.zeros_like(st_ref)

        xv, dhv = x_ref[...], dh_ref[...]
        r = lax.rsqrt(jnp.mean(xv * xv, axis=-1, keepdims=True) + EPS)
        xn = xv * r
        gv = g_ref[...]
        sc1 = 1.0 + mod_ref[:, d:2 * d]
        dxn = dhv * gv * sc1
        dx_ref[...] = dxo_ref[...] + r * (dxn - xn * jnp.mean(dxn * xn, axis=-1, keepdims=True))
        st_ref[0:1, :] += jnp.sum(dhv, axis=0, keepdims=True)
        st_ref[1:2, :] += jnp.sum(dhv * xn * gv, axis=0, keepdims=True)
        st_ref[2:3, :] += jnp.sum(dhv * xn * sc1, axis=0, keepdims=True)

    blk = pl.BlockSpec((tr, d), lambda i: (i, 0))
    return pl.pallas_call(
        body, name="norm_bwd", grid=(l // tr,),
        in_specs=[blk, blk, blk, _full((1, d)), _full((1, 3 * d))],
        out_specs=[blk, _full((SUBLANES, d))],
        out_shape=[jax.ShapeDtypeStruct((l, d), F32), jax.ShapeDtypeStruct((SUBLANES, d), F32)],
        compiler_params=_params(("arbitrary",)))(x, dh, dxo, g, mod)


def _loss_head(x, g, target):
    l, d = x.shape
    tr = min(ROWS_NORM, l)

    def body(x_ref, g_ref, t_ref, dx_ref, st_ref):
        @pl.when(pl.program_id(0) == 0)
        def _():
            st_ref[...] = jnp.zeros_like(st_ref)

        xv = x_ref[...]
        r = lax.rsqrt(jnp.mean(xv * xv, axis=-1, keepdims=True) + EPS)
        xn = xv * r
        gv = g_ref[...]
        err = xn * gv - t_ref[...]
        part = 0.5 * jnp.sum(jnp.mean(err * err, axis=-1, keepdims=True), axis=0, keepdims=True)
        dy = err * (1.0 / d)
        dxn = dy * gv
        dx_ref[...] = r * (dxn - xn * jnp.mean(dxn * xn, axis=-1, keepdims=True))
        st_ref[0:1, :] += jnp.sum(dy * xn, axis=0, keepdims=True)
        st_ref[1:2, :] += jnp.broadcast_to(part, (1, d))

    blk = pl.BlockSpec((tr, d), lambda i: (i, 0))
    return pl.pallas_call(
        body, name="loss_head", grid=(l // tr,), in_specs=[blk, _full((1, d)), blk],
        out_specs=[blk, _full((SUBLANES, d))],
        out_shape=[jax.ShapeDtypeStruct((l, d), F32), jax.ShapeDtypeStruct((SUBLANES, d), F32)],
        compiler_params=_params(("arbitrary",)))(x, g, target)


ROWS_ATT = 128
ROWS_ATT_BWD = 256
_SLOPES = tuple(2.0 ** (-8.0 * (h + 1) / N_HEADS) for h in range(N_HEADS))


def _att_mask(i, t):
    r = lax.broadcasted_iota(jnp.int32, (t, t + HALO), 0)
    j = lax.broadcasted_iota(jnp.int32, (t, t + HALO), 1)
    dist = jnp.abs(r + HALO - j).astype(F32)
    rc, jc = r // CHUNK, j // CHUNK
    allowed = (jc >= rc) & (jc <= rc + 2) & ((j >= HALO) | (i > 0))
    return dist, allowed


def _att_probs(qh, k, dist, allowed, slope, sink):
    s = _dot(qh, k, _NT) * (1.0 / math.sqrt(HEAD_DIM)) - slope * dist
    s = jnp.where(allowed, s, NEG_INF)
    m = jnp.maximum(jnp.max(s, axis=1, keepdims=True), sink)
    e = jnp.exp(s - m)
    es = jnp.exp(sink - m)
    den = jnp.sum(e, axis=1, keepdims=True) + es
    return e / den, es / den


def _att_specs(t):
    q_spec = pl.BlockSpec((t, ATT_W), lambda i: (i, OFF_Q // ATT_W))
    kv_spec = pl.BlockSpec((t, 2 * KV_W), lambda i: (i, OFF_KV // (2 * KV_W)))
    halo_spec = pl.BlockSpec((HALO, 2 * KV_W), lambda i: (jnp.maximum(i * (t // HALO) - 1, 0), OFF_KV // (2 * KV_W)))
    return q_spec, kv_spec, halo_spec


def _attention_fwd(proj, sinks):
    l = proj.shape[0]
    t = min(ROWS_ATT, l)

    def body(sink_ref, q_ref, kv_ref, halo_ref, o_ref):
        dist, allowed = _att_mask(pl.program_id(0), t)
        kv = jnp.concatenate([halo_ref[...], kv_ref[...]], axis=0)
        for h in range(N_HEADS):
            kh = h // Q_PER_KV
            k = kv[:, kh * HEAD_DIM:(kh + 1) * HEAD_DIM]
            v = kv[:, KV_W + kh * HEAD_DIM:KV_W + (kh + 1) * HEAD_DIM]
            p, _ = _att_probs(q_ref[:, h * HEAD_DIM:(h + 1) * HEAD_DIM], k, dist, allowed, _SLOPES[h], sink_ref[h])
            o_ref[:, h * HEAD_DIM:(h + 1) * HEAD_DIM] = _dot(p, v)

    q_spec, kv_spec, halo_spec = _att_specs(t)
    return pl.pallas_call(
        body, name="attention_fwd", grid=(l // t,),
        in_specs=[pl.BlockSpec(memory_space=pltpu.SMEM), q_spec, kv_spec, halo_spec],
        out_specs=pl.BlockSpec((t, ATT_W), lambda i: (i, 0)), out_shape=jax.ShapeDtypeStruct((l, ATT_W), F32),
        compiler_params=_params(("parallel",)))(sinks, proj, proj, proj)


def _attention_bwd(proj, sinks, do):
    l = proj.shape[0]
    t = min(ROWS_ATT_BWD, l)

    def body(sink_ref, q_ref, kv_ref, halo_ref, do_ref, dq_ref, dkv_ref, dsink_ref):
        i = pl.program_id(0)

        @pl.when(i == 0)
        def _():
            dkv_ref[...] = jnp.zeros_like(dkv_ref)
            dsink_ref[...] = jnp.zeros_like(dsink_ref)

        dist, allowed = _att_mask(i, t)
        kv = jnp.concatenate([halo_ref[...], kv_ref[...]], axis=0)
        rows = pl.ds(pl.multiple_of(i * t, t), t + HALO)
        for kh in range(N_KV_HEADS):
            k = kv[:, kh * HEAD_DIM:(kh + 1) * HEAD_DIM]
            v = kv[:, KV_W + kh * HEAD_DIM:KV_W + (kh + 1) * HEAD_DIM]
            dk = jnp.zeros((t + HALO, HEAD_DIM), F32)
            dv = jnp.zeros((t + HALO, HEAD_DIM), F32)
            for h in range(kh * Q_PER_KV, (kh + 1) * Q_PER_KV):
                qh = q_ref[:, h * HEAD_DIM:(h + 1) * HEAD_DIM]
                doh = do_ref[:, h * HEAD_DIM:(h + 1) * HEAD_DIM]
                p, ps = _att_probs(qh, k, dist, allowed, _SLOPES[h], sink_ref[h])
                dp = _dot(doh, v, _NT)
                delta = jnp.sum(p * dp, axis=1, keepdims=True)
                ds = p * (dp - delta) * (1.0 / math.sqrt(HEAD_DIM))
                dsink_ref[h:h + 1, :] += jnp.broadcast_to(-jnp.sum(ps * delta, axis=0, keepdims=True), (1, LANES))
                dq_ref[:, h * HEAD_DIM:(h + 1) * HEAD_DIM] = _dot(ds, k).astype(BF16)
                dk = dk + _dot(ds, qh, _TN)
                dv = dv + _dot(p, doh, _TN)
            dkv_ref[rows, kh * HEAD_DIM:(kh + 1) * HEAD_DIM] += dk
            dkv_ref[rows, KV_W + kh * HEAD_DIM:KV_W + (kh + 1) * HEAD_DIM] += dv

    q_spec, kv_spec, halo_spec = _att_specs(t)
    blk = pl.BlockSpec((t, ATT_W), lambda i: (i, 0))
    return pl.pallas_call(
        body, name="attention_bwd", grid=(l // t,),
        in_specs=[pl.BlockSpec(memory_space=pltpu.SMEM), q_spec, kv_spec, halo_spec, blk],
        out_specs=[blk, _full((HALO + l, 2 * KV_W)), _full((N_HEADS, LANES))],
        out_shape=[jax.ShapeDtypeStruct((l, ATT_W), BF16), jax.ShapeDtypeStruct((HALO + l, 2 * KV_W), F32),
                   jax.ShapeDtypeStruct((N_HEADS, LANES), F32)],
        compiler_params=_params(("arbitrary",)))(sinks, proj, proj, proj, do)


ROWS_SSM = 2048
SSM_BLOCKS = 2
SCAN_UNROLL = 2
SSM_STEPS_PER_U = LANES // (SSM_BLOCKS * 2 * SSM_GROUP)


def _ssm_discretize(a_re, a_im, log_dt, b_re, b_im):
    lam = lax.complex(a_re, a_im)
    dt = jnp.exp(log_dt)[:, None]
    lam_bar = jnp.exp(lam * dt)
    b_bar = ((lam_bar - 1.0) / lam)[..., None] * lax.complex(b_re, b_im)
    return lam, dt, lam_bar, b_bar


def _ssm_block_diag(m):
    e = m.reshape(N_SBLK, 2, SSM_GROUP, SSM_STATE)
    e = e[:, :, :, None, :] * jnp.eye(2, dtype=m.dtype)[None, :, None, :, None]
    e = e.reshape(N_SBLK, 2 * SSM_GROUP, LANES)
    oh = jax.nn.one_hot(jnp.arange(N_SBLK) % 4, 4, dtype=m.dtype)
    return (oh[:, :, None, None] * e[:, None]).reshape(N_SBLK, LANES, LANES)


def _ssm_tables(a_re, a_im, log_dt, b_re, b_im, c_re, c_im):
    _, _, lam_bar, b_bar = _ssm_discretize(a_re, a_im, log_dt, b_re, b_im)
    lam_blk = jnp.stack([jnp.real(lam_bar).reshape(N_SBLK, LANES), jnp.imag(lam_bar).reshape(N_SBLK, LANES)], axis=1)
    bt = jnp.transpose(b_bar, (0, 2, 1))
    bb = jnp.concatenate([_ssm_block_diag(jnp.real(bt)), _ssm_block_diag(jnp.imag(bt))], axis=2)
    cb = jnp.concatenate([jnp.transpose(_ssm_block_diag(c_re), (0, 2, 1)),
                          jnp.transpose(_ssm_block_diag(-c_im), (0, 2, 1))], axis=1)
    return lam_blk, bb, cb


def _ssm_powers(a_re, a_im, log_dt):
    lam = lax.complex(a_re, a_im)
    dt = jnp.exp(log_dt)[:, None]
    k = jnp.arange(1, SUBLANES + 1, dtype=F32)
    pw = jnp.exp((lam * dt)[None] * k[:, None, None]).reshape(SUBLANES, N_SBLK, LANES)
    pw = jnp.transpose(pw, (1, 0, 2))
    rev = pw[:, ::-1]
    return jnp.concatenate([jnp.real(pw), jnp.imag(pw), jnp.real(rev), jnp.imag(rev)], axis=1)


def _scan_consts(pw_ref, b, reverse):
    row = lax.broadcasted_iota(jnp.int32, (SUBLANES, LANES), 0)
    sign = -1.0 if reverse else 1.0

    def power(k):
        return (jnp.broadcast_to(pw_ref[b, k - 1:k, :], (SUBLANES, LANES)),
                sign * jnp.broadcast_to(pw_ref[b, SUBLANES + k - 1:SUBLANES + k, :], (SUBLANES, LANES)))

    steps = []
    for d in (1, 2, 4):
        pr, pi = power(d)
        keep = (row < SUBLANES - d) if reverse else (row >= d)
        steps.append((d, jnp.where(keep, pr, 0.0), jnp.where(keep, pi, 0.0)))
    base = 2 * SUBLANES if reverse else 0
    return steps, pw_ref[b, base:base + SUBLANES, :], sign * pw_ref[b, base + SUBLANES:base + 2 * SUBLANES, :]


def _scan_tile(xr, xi, steps, reverse):
    for d, ar, ai in steps:
        shift = SUBLANES - d if reverse else d
        rr, ri = pltpu.roll(xr, shift, 0), pltpu.roll(xi, shift, 0)
        xr, xi = xr + (ar * rr - ai * ri), xi + (ar * ri + ai * rr)
    return xr, xi


def _bcast_row(v, r):
    return jnp.broadcast_to(v[r:r + 1, :], (SUBLANES, LANES))


def _re_im(b):
    return slice(2 * b * LANES, (2 * b + 1) * LANES), slice((2 * b + 1) * LANES, (2 * b + 2) * LANES)


def _scan_forward(s_scr, pw_ref, n_tiles):
    consts = [_scan_consts(pw_ref, b, False) for b in range(SSM_BLOCKS)]

    def tiles(g, carry):
        carry = list(carry)
        rows = [pl.ds(pl.multiple_of((g * SCAN_UNROLL + u) * SUBLANES, SUBLANES), SUBLANES) for u in range(SCAN_UNROLL)]
        loaded = [[(s_scr[r, _re_im(b)[0]], s_scr[r, _re_im(b)[1]]) for b in range(SSM_BLOCKS)] for r in rows]
        local = [[_scan_tile(xr, xi, consts[b][0], False) for b, (xr, xi) in enumerate(per_row)] for per_row in loaded]
        for r, per_row in zip(rows, local):
            for b, (xr, xi) in enumerate(per_row):
                _, pr, pi = consts[b]
                cr, ci = carry[b]
                sr, si = xr + (pr * cr - pi * ci), xi + (pr * ci + pi * cr)
                s_scr[r, _re_im(b)[0]] = sr
                s_scr[r, _re_im(b)[1]] = si
                carry[b] = (_bcast_row(sr, SUBLANES - 1), _bcast_row(si, SUBLANES - 1))
        return tuple(carry)

    zero = jnp.zeros((SUBLANES, LANES), F32)
    lax.fori_loop(0, n_tiles // SCAN_UNROLL, tiles, ((zero, zero),) * SSM_BLOCKS)


def _row_chunks(l):
    rc = min(ROWS_SSM, l)
    return rc, l // rc


def _carry_sends(body, n_in, n_out, n_scratch, grid, sends):
    if sends is None:
        return body, [], [], [], [], []
    k = len(sends.srcs)

    def carrying(*refs):
        ins, send_ins = refs[:n_in], refs[n_in:n_in + k]
        outs, send_outs = refs[n_in + k:n_in + k + n_out], refs[n_in + k + n_out:n_in + 2 * k + n_out]
        rest = refs[n_in + 2 * k + n_out:]
        scratch, sems = rest[:n_scratch], rest[n_scratch:]
        ids = [pl.program_id(axis) for axis in range(len(grid))]
        first = functools.reduce(jnp.logical_and, [i == 0 for i in ids])
        last = functools.reduce(jnp.logical_and, [i == n - 1 for i, n in zip(ids, grid)])

        @pl.when(first)
        def _():
            sends.start(send_ins, send_outs, *sems)

        body(*ins, *outs, *scratch)

        @pl.when(last)
        def _():
            sends.finish(send_ins, send_outs, *sems)

    in_specs, out_specs, scratch = sends.specs()
    return carrying, in_specs, out_specs, sends.out_shapes, scratch, sends.srcs


def _ssm_fwd(proj, bb, cb, pw, sends=None):
    l = proj.shape[0]
    rc, n_chunks = _row_chunks(l)

    def body(u_ref, bb_ref, cb_ref, pw_ref, y_ref, s_scr):
        j = pl.program_id(0)

        def fill(ci, _):
            rows = pl.ds(pl.multiple_of(ci * rc, rc), rc)
            uv = u_ref[rows, :]
            for b in range(SSM_BLOCKS):
                s_scr[rows, 2 * b * LANES:2 * (b + 1) * LANES] = _dot(uv, bb_ref[b])
            return 0

        lax.fori_loop(0, n_chunks, fill, 0)
        _scan_forward(s_scr, pw_ref, l // SUBLANES)

        @pl.when(j % SSM_STEPS_PER_U == 0)
        def _():
            y_ref[...] = jnp.zeros_like(y_ref)

        def emit(ci, _):
            rows = pl.ds(pl.multiple_of(ci * rc, rc), rc)
            for b in range(SSM_BLOCKS):
                y_ref[rows, :] += _dot(s_scr[rows, 2 * b * LANES:2 * (b + 1) * LANES], cb_ref[b])
            return 0

        lax.fori_loop(0, n_chunks, emit, 0)

    steps = N_SBLK // SSM_BLOCKS
    body, x_in, x_out, x_shapes, x_scratch, x_args = _carry_sends(body, 4, 2, 0, (steps,), sends)
    outs = pl.pallas_call(
        body, name="ssm_fwd" if sends is None else "ssm_fwd_sending", grid=(steps,),
        in_specs=[pl.BlockSpec((l, LANES), lambda j: (0, OFF_US // LANES + j // SSM_STEPS_PER_U)),
                  pl.BlockSpec((SSM_BLOCKS, LANES, 2 * LANES), lambda j: (j, 0, 0)),
                  pl.BlockSpec((SSM_BLOCKS, 2 * LANES, LANES), lambda j: (j, 0, 0)),
                  pl.BlockSpec((SSM_BLOCKS, 4 * SUBLANES, LANES), lambda j: (j, 0, 0))] + x_in,
        out_specs=[pl.BlockSpec((l, LANES), lambda j: (0, j // SSM_STEPS_PER_U)),
                   pl.BlockSpec((l, SSM_BLOCKS * 2 * LANES), lambda j: (0, j))] + x_out,
        out_shape=[jax.ShapeDtypeStruct((l, SSM_W), F32), jax.ShapeDtypeStruct((l, N_SBLK * 2 * LANES), F32)] + x_shapes,
        scratch_shapes=x_scratch, compiler_params=_params(("arbitrary",)))(proj, bb, cb, pw, *x_args)
    return outs[0], outs[1], list(outs[2:])


def _ssm_bwd(proj, states, dy, bb, cb, pw, sends=None):
    l = proj.shape[0]
    rc, n_chunks = _row_chunks(l)
    n_tiles = l // SUBLANES

    def body(u_ref, s_scr, dy_ref, bb_ref, cb_ref, pw_ref, du_ref, dbb_ref, dcb_ref, dlam_ref, a_scr):
        j = pl.program_id(0)
        dcb_ref[...] = jnp.zeros_like(dcb_ref)

        def through_c(ci, _):
            rows = pl.ds(pl.multiple_of(ci * rc, rc), rc)
            dyv = dy_ref[rows, :]
            for b in range(SSM_BLOCKS):
                lanes = slice(2 * b * LANES, 2 * (b + 1) * LANES)
                dcb_ref[b] += _dot(s_scr[rows, lanes], dyv, _TN)
                a_scr[rows, lanes] = _dot(dyv, cb_ref[b], _NT)
            return 0

        lax.fori_loop(0, n_chunks, through_c, 0)

        consts = [_scan_consts(pw_ref, b, True) for b in range(SSM_BLOCKS)]
        row = lax.broadcasted_iota(jnp.int32, (SUBLANES, LANES), 0)
        last = row == SUBLANES - 1

        def tiles(g, carry):
            carry = list(carry)
            rows = [pl.ds(pl.multiple_of((n_tiles - 1 - (g * SCAN_UNROLL + u)) * SUBLANES, SUBLANES), SUBLANES)
                    for u in range(SCAN_UNROLL)]
            loaded = [[(a_scr[r, _re_im(b)[0]], a_scr[r, _re_im(b)[1]]) for b in range(SSM_BLOCKS)] for r in rows]
            states = [[(s_scr[r, _re_im(b)[0]], s_scr[r, _re_im(b)[1]]) for b in range(SSM_BLOCKS)] for r in rows]
            local = [[_scan_tile(xr, xi, consts[b][0], True) for b, (xr, xi) in enumerate(per_row)] for per_row in loaded]
            for r, per_row, state_row in zip(rows, local, states):
                for b, ((xr, xi), (sr, si)) in enumerate(zip(per_row, state_row)):
                    _, pr, pi = consts[b]
                    cr, ci, accr, acci = carry[b]
                    gr, gi = xr + (pr * cr - pi * ci), xi + (pr * ci + pi * cr)
                    a_scr[r, _re_im(b)[0]] = gr
                    a_scr[r, _re_im(b)[1]] = gi
                    ur = jnp.where(last, cr, pltpu.roll(gr, SUBLANES - 1, 0))
                    ui = jnp.where(last, ci, pltpu.roll(gi, SUBLANES - 1, 0))
                    carry[b] = (_bcast_row(gr, 0), _bcast_row(gi, 0),
                                accr + (ur * sr + ui * si), acci + (ui * sr - ur * si))
            return tuple(carry)

        zero = jnp.zeros((SUBLANES, LANES), F32)
        done = lax.fori_loop(0, n_tiles // SCAN_UNROLL, tiles, ((zero, zero, zero, zero),) * SSM_BLOCKS)
        for b in range(SSM_BLOCKS):
            dlr = jnp.broadcast_to(jnp.sum(done[b][2], axis=0, keepdims=True), (SUBLANES, LANES))
            dli = jnp.broadcast_to(jnp.sum(done[b][3], axis=0, keepdims=True), (SUBLANES, LANES))
            dlam_ref[b] = jnp.where(row == 0, dlr, jnp.where(row == 1, dli, 0.0))

        dbb_ref[...] = jnp.zeros_like(dbb_ref)

        @pl.when(j % SSM_STEPS_PER_U == 0)
        def _():
            du_ref[...] = jnp.zeros_like(du_ref)

        def through_b(ci, _):
            rows = pl.ds(pl.multiple_of(ci * rc, rc), rc)
            uv = u_ref[rows, :]
            for b in range(SSM_BLOCKS):
                av = a_scr[rows, 2 * b * LANES:2 * (b + 1) * LANES]
                dbb_ref[b] += _dot(uv, av, _TN)
                du_ref[rows, :] += _dot(av, bb_ref[b], _NT)
            return 0

        lax.fori_loop(0, n_chunks, through_b, 0)

    steps = N_SBLK // SSM_BLOCKS
    body, x_in, x_out, x_shapes, x_scratch, x_args = _carry_sends(body, 6, 4, 1, (steps,), sends)
    outs = pl.pallas_call(
        body, name="ssm_bwd" if sends is None else "ssm_bwd_sending", grid=(steps,),
        in_specs=[pl.BlockSpec((l, LANES), lambda j: (0, OFF_US // LANES + j // SSM_STEPS_PER_U)),
                  pl.BlockSpec((l, SSM_BLOCKS * 2 * LANES), lambda j: (0, j)),
                  pl.BlockSpec((l, LANES), lambda j: (0, j // SSM_STEPS_PER_U)),
                  pl.BlockSpec((SSM_BLOCKS, LANES, 2 * LANES), lambda j: (j, 0, 0)),
                  pl.BlockSpec((SSM_BLOCKS, 2 * LANES, LANES), lambda j: (j, 0, 0)),
                  pl.BlockSpec((SSM_BLOCKS, 4 * SUBLANES, LANES), lambda j: (j, 0, 0))] + x_in,
        out_specs=[pl.BlockSpec((l, LANES), lambda j: (0, j // SSM_STEPS_PER_U)),
                   pl.BlockSpec((SSM_BLOCKS, LANES, 2 * LANES), lambda j: (j, 0, 0)),
                   pl.BlockSpec((SSM_BLOCKS, 2 * LANES, LANES), lambda j: (j, 0, 0)),
                   pl.BlockSpec((SSM_BLOCKS, SUBLANES, LANES), lambda j: (j, 0, 0))] + x_out,
        out_shape=[jax.ShapeDtypeStruct((l, SSM_W), F32), jax.ShapeDtypeStruct((N_SBLK, LANES, 2 * LANES), F32),
                   jax.ShapeDtypeStruct((N_SBLK, 2 * LANES, LANES), F32),
                   jax.ShapeDtypeStruct((N_SBLK, SUBLANES, LANES), F32)] + x_shapes,
        scratch_shapes=[pltpu.VMEM((l, SSM_BLOCKS * 2 * LANES), F32)] + x_scratch,
        compiler_params=_params(("arbitrary",)))(proj, states, dy, bb, cb, pw, *x_args)
    return outs[0], outs[1], outs[2], outs[3], list(outs[4:])


def _pool_windows(v, t, l, ahead):
    def shifted(a, d):
        if ahead:
            return jnp.where(t < l - d, pltpu.roll(a, l - d, 0), 0.0)
        return jnp.where(t >= d, pltpu.roll(a, d, 0), 0.0)

    w2 = v + shifted(v, 1)
    w4 = w2 + shifted(w2, 2)
    w8 = w4 + shifted(w4, 4)
    w16 = w8 + shifted(w8, 8)
    return w2, w4, w8, w16


def _pool_select(g, ws):
    return jnp.where(g == 0, ws[0], jnp.where(g == 1, ws[1], jnp.where(g == 2, ws[2], ws[3])))


def _pool_count(g, t):
    return jnp.minimum(t + 1, jnp.left_shift(2, g)).astype(F32)


def _pool_specs(l):
    return [pl.BlockSpec((l, POOL_GW), lambda g: (0, OFF_UP // POOL_GW + g)),
            pl.BlockSpec((1, POOL_GW, POOL_GW), lambda g: (g, 0, 0)),
            pl.BlockSpec((1, POOL_GW), lambda g: (0, g))]


def _pool_fwd(proj, w_pool, scale):
    l = proj.shape[0]

    def body(u_ref, w_ref, sc_ref, y_ref):
        g = pl.program_id(0)
        t = lax.broadcasted_iota(jnp.int32, (l, 1), 0)
        u = u_ref[...]
        pooled = _pool_select(g, _pool_windows(u, t, l, False)) / _pool_count(g, t) - u
        y_ref[...] = _dot(pooled, w_ref[0]) * sc_ref[...]

    return pl.pallas_call(
        body, name="pool_fwd", grid=(4,), in_specs=_pool_specs(l),
        out_specs=pl.BlockSpec((l, POOL_GW), lambda g: (0, g)), out_shape=jax.ShapeDtypeStruct((l, POOL_W), F32),
        compiler_params=_params(("parallel",)))(proj, w_pool, scale)


def _pool_bwd(proj, w_pool, scale, dy):
    l = proj.shape[0]

    def body(u_ref, w_ref, sc_ref, dy_ref, du_ref, dw_ref, dsc_ref):
        g = pl.program_id(0)
        t = lax.broadcasted_iota(jnp.int32, (l, 1), 0)
        u = u_ref[...]
        cnt = _pool_count(g, t)
        pooled = _pool_select(g, _pool_windows(u, t, l, False)) / cnt - u
        dyv = dy_ref[...]
        dsc_ref[...] = jnp.sum(dyv * _dot(pooled, w_ref[0]), axis=0, keepdims=True)
        dyl = dyv * sc_ref[...]
        dw_ref[0] = _dot(pooled, dyl, _TN)
        dpooled = _dot(dyl, w_ref[0], _NT)
        du_ref[...] = (_pool_select(g, _pool_windows(dpooled / cnt, t, l, True)) - dpooled).astype(BF16)

    return pl.pallas_call(
        body, name="pool_bwd", grid=(4,), in_specs=_pool_specs(l) + [pl.BlockSpec((l, POOL_GW), lambda g: (0, g))],
        out_specs=[pl.BlockSpec((l, POOL_GW), lambda g: (0, g)), pl.BlockSpec((1, POOL_GW, POOL_GW), lambda g: (g, 0, 0)),
                   pl.BlockSpec((1, POOL_GW), lambda g: (0, g))],
        out_shape=[jax.ShapeDtypeStruct((l, POOL_W), BF16), jax.ShapeDtypeStruct((4, POOL_GW, POOL_GW), F32),
                   jax.ShapeDtypeStruct((1, POOL_W), F32)],
        compiler_params=_params(("parallel",)))(proj, w_pool, scale, dy)


ROWS_MERGE = 128
ROWS_MERGE_FWD = 256


def _merge_inputs(tr):
    def col(off, w):
        return pl.BlockSpec((tr, w), lambda i: (i, off // w))

    def act(w):
        return pl.BlockSpec((tr, w), lambda i: (i, 0))

    d = D_MODEL
    return ([col(OFF_GL, 3 * d), col(OFF_ZA, ATT_W), col(OFF_US, SSM_W), col(OFF_ZS, SSM_W), col(OFF_ZP, POOL_W),
             act(ATT_W), act(SSM_W), act(POOL_W)]
            + [_full((1, SSM_W)), _full((SSM_W, SSM_W)), _full((1, SSM_W)), _full((ATT_W, d)), _full((SSM_W, d)),
               _full((POOL_W, d)), _full((d, d)), _full((1, 3 * d))])


def _merge_forward_math(gl_ref, za_ref, us_ref, zs_ref, zp_ref, o_ref, yl_ref, yp_ref, d_ref, wg_ref, bg_ref,
                        wba_ref, wbs_ref, wbp_ref):
    d = D_MODEL
    r = {}
    r['sa'], r['dsa'] = _silu_and_grad(za_ref[...])
    r['ss'], r['dss'] = _silu_and_grad(zs_ref[...])
    r['sp'], r['dsp'] = _silu_and_grad(zp_ref[...])
    r['y_att'] = o_ref[...] * r['sa']
    y1 = yl_ref[...] + d_ref[...] * us_ref[...]
    r['y2'], r['dgelu'] = _gelu_and_grad(y1)
    r['sg'] = _sigmoid(_dot(r['y2'], wg_ref[...]) + bg_ref[...])
    r['y3'] = r['y2'] * r['sg']
    r['y_ssm'] = r['y3'] * r['ss']
    r['y_pool'] = yp_ref[...] * r['sp']
    r['g'] = [_sigmoid(gl_ref[:, b * d:(b + 1) * d]) for b in range(3)]
    r['p'] = [_dot(r['y_att'], wba_ref[...]), _dot(r['y_ssm'], wbs_ref[...]), _dot(r['y_pool'], wbp_ref[...])]
    r['merged'] = r['g'][0] * r['p'][0] + r['g'][1] * r['p'][1] + r['g'][2] * r['p'][2]
    return r


def _merge_fwd(proj, o, y_lin, y_pool, x, ssm_d, w_glu, b_glu, wba, wbs, wbp, w_out, mod):
    l, d = x.shape
    tr = min(ROWS_MERGE_FWD, l)

    def body(gl_ref, za_ref, us_ref, zs_ref, zp_ref, o_ref, yl_ref, yp_ref, d_ref, wg_ref, bg_ref, wba_ref, wbs_ref,
             wbp_ref, wo_ref, mod_ref, x_ref, xn_ref):
        r = _merge_forward_math(gl_ref, za_ref, us_ref, zs_ref, zp_ref, o_ref, yl_ref, yp_ref, d_ref, wg_ref, bg_ref,
                                wba_ref, wbs_ref, wbp_ref)
        xn_ref[...] = x_ref[...] + mod_ref[:, 2 * d:3 * d] * _dot(r['merged'], wo_ref[...])

    blk = pl.BlockSpec((tr, d), lambda i: (i, 0))
    return pl.pallas_call(
        body, name="merge_fwd", grid=(l // tr,), in_specs=_merge_inputs(tr) + [blk], out_specs=blk,
        out_shape=jax.ShapeDtypeStruct((l, d), F32), compiler_params=_params(("parallel",)))(
            proj, proj, proj, proj, proj, o, y_lin, y_pool, ssm_d, w_glu, b_glu, wba, wbs, wbp, w_out, mod, x)


def _merge_bwd(proj, o, y_lin, y_pool, dxo, ssm_d, w_glu, b_glu, wba, wbs, wbp, w_out, mod):
    l, d = dxo.shape
    tr = min(ROWS_MERGE, l)

    def body(gl_ref, za_ref, us_ref, zs_ref, zp_ref, o_ref, yl_ref, yp_ref, d_ref, wg_ref, bg_ref, wba_ref, wbs_ref,
             wbp_ref, wo_ref, mod_ref, dxo_ref,
             do_ref, dyl_ref, dus_ref, dyp_ref, dpj_ref,
             ya_ref, ys_ref, ypl_ref, y2_ref, dt_ref, dpa_ref, dps_ref, dpp_ref, mg_ref, dout_ref, st_ref):
        @pl.when(pl.program_id(0) == 0)
        def _():
            st_ref[...] = jnp.zeros_like(st_ref)

        for off, width in ((OFF_Q, ATT_W), (OFF_US, SSM_W), (OFF_UP, POOL_W), (OFF_KV, 2 * KV_W)):
            dpj_ref[:, off:off + width] = jnp.zeros((tr, width), BF16)

        r = _merge_forward_math(gl_ref, za_ref, us_ref, zs_ref, zp_ref, o_ref, yl_ref, yp_ref, d_ref, wg_ref, bg_ref,
                                wba_ref, wbs_ref, wbp_ref)
        dxov = dxo_ref[...]
        out = _dot(r['merged'], wo_ref[...])
        st_ref[0:1, :] += jnp.sum(dxov * out, axis=0, keepdims=True)
        dout = dxov * mod_ref[:, 2 * d:3 * d]
        dmerged = _dot(dout, wo_ref[...], _NT)
        dys = []
        for b, (w_ref, dp_ref) in enumerate(((wba_ref, dpa_ref), (wbs_ref, dps_ref), (wbp_ref, dpp_ref))):
            gb = r['g'][b]
            dpj_ref[:, OFF_GL + b * d:OFF_GL + (b + 1) * d] = (dmerged * r['p'][b] * gb * (1.0 - gb)).astype(BF16)
            dp = dmerged * gb
            dp_ref[...] = dp.astype(BF16)
            dys.append(_dot(dp, w_ref[...], _NT))
        mg_ref[...] = r['merged'].astype(BF16)
        dout_ref[...] = dout.astype(BF16)
        ya_ref[...] = r['y_att'].astype(BF16)
        ys_ref[...] = r['y_ssm'].astype(BF16)
        ypl_ref[...] = r['y_pool'].astype(BF16)
        y2_ref[...] = r['y2'].astype(BF16)
        do_ref[...] = (dys[0] * r['sa']).astype(BF16)
        dpj_ref[:, OFF_ZA:OFF_ZA + ATT_W] = (dys[0] * o_ref[...] * r['dsa']).astype(BF16)
        dy3 = dys[1] * r['ss']
        dpj_ref[:, OFF_ZS:OFF_ZS + SSM_W] = (dys[1] * r['y3'] * r['dss']).astype(BF16)
        dt = dy3 * r['y2'] * r['sg'] * (1.0 - r['sg'])
        dt_ref[...] = dt.astype(BF16)
        dy1 = (dy3 * r['sg'] + _dot(dt, wg_ref[...], _NT)) * r['dgelu']
        dyl_ref[...] = dy1.astype(BF16)
        dus_ref[...] = dy1 * d_ref[...]
        st_ref[1:2, 0:SSM_W] += jnp.sum(dt, axis=0, keepdims=True)
        st_ref[1:2, SSM_W:2 * SSM_W] += jnp.sum(dy1 * us_ref[...], axis=0, keepdims=True)
        dyp_ref[...] = dys[2] * r['sp']
        dpj_ref[:, OFF_ZP:OFF_ZP + POOL_W] = (dys[2] * yp_ref[...] * r['dsp']).astype(BF16)

    blk = pl.BlockSpec((tr, d), lambda i: (i, 0))
    half = pl.BlockSpec((tr, ATT_W), lambda i: (i, 0))
    wide = pl.BlockSpec((tr, IN_W), lambda i: (i, 0))
    sds = jax.ShapeDtypeStruct
    return pl.pallas_call(
        body, name="merge_bwd", grid=(l // tr,), in_specs=_merge_inputs(tr) + [blk],
        out_specs=[half] * 4 + [wide] + [half] * 5 + [blk] * 5 + [_full((SUBLANES, d))],
        out_shape=[sds((l, ATT_W), t) for t in (BF16, BF16, F32, F32)] + [sds((l, IN_W), BF16)]
        + [sds((l, ATT_W), BF16)] * 5 + [sds((l, d), BF16)] * 5 + [sds((SUBLANES, d), F32)],
        compiler_params=_params(("arbitrary",)))(
            proj, proj, proj, proj, proj, o, y_lin, y_pool, ssm_d, w_glu, b_glu, wba, wbs, wbp, w_out, mod, dxo)


def _ada_fwd(c_all, w_ada, b_shard):
    depth, d, n = w_ada.shape

    def body(c_ref, w_ref, b_ref, o_ref):
        act, _ = _silu_and_grad(c_ref[...])
        o_ref[0] = _dot(act, w_ref[0]) + b_ref[0]

    return pl.pallas_call(
        body, name="ada_fwd", grid=(depth,),
        in_specs=[_full((N_DEV, d)), pl.BlockSpec((1, d, n), lambda i: (i, 0, 0)), pl.BlockSpec((1, 1, n), lambda i: (i, 0, 0))],
        out_specs=pl.BlockSpec((1, N_DEV, n), lambda i: (i, 0, 0)), out_shape=jax.ShapeDtypeStruct((depth, N_DEV, n), F32),
        compiler_params=_params(("parallel",)))(c_all, w_ada, b_shard)


def _ada_bwd(c_all_t, dmod):
    d = c_all_t.shape[0]
    depth, _, n = dmod.shape

    def body(c_ref, dm_ref, o_ref):
        act, _ = _silu_and_grad(c_ref[...])
        acc = act[:, 0:1] * dm_ref[0, 0:1, :]
        for b in range(1, N_DEV):
            acc = acc + act[:, b:b + 1] * dm_ref[0, b:b + 1, :]
        o_ref[0] = acc

    return pl.pallas_call(
        body, name="ada_bwd", grid=(depth,),
        in_specs=[_full((d, N_DEV)), pl.BlockSpec((1, N_DEV, n), lambda i: (i, 0, 0))],
        out_specs=pl.BlockSpec((1, d, n), lambda i: (i, 0, 0)), out_shape=jax.ShapeDtypeStruct((depth, d, n), F32),
        compiler_params=_params(("parallel",)))(c_all_t, dmod)


ROWS_FLAT = 256


_RELATION_XOR = (0, 2, 1, 3)


def _sum_chips(own, others, chip_order, name):
    r, c = own.shape
    tr = math.gcd(ROWS_FLAT, r)

    def body(own_ref, oth_ref, o_ref):
        terms = [own_ref[...].astype(F32)] + [oth_ref[k].astype(F32) for k in range(3)]
        if chip_order:
            chip = 2 * lax.axis_index("x") + lax.axis_index("y")
            by_chip = []
            for q in range(N_CHIPS):
                rel = jnp.bitwise_xor(chip, q)
                pick = terms[3]
                for k in (2, 1, 0):
                    pick = jnp.where(rel == _RELATION_XOR[k], terms[k], pick)
                by_chip.append(pick)
            terms = by_chip
        o_ref[...] = ((terms[0] + terms[1]) + terms[2]) + terms[3]

    return pl.pallas_call(
        body, name=name, grid=(r // tr,),
        in_specs=[pl.BlockSpec((tr, c), lambda i: (i, 0)), pl.BlockSpec((3, tr, c), lambda i: (0, i, 0))],
        out_specs=pl.BlockSpec((tr, c), lambda i: (i, 0)), out_shape=jax.ShapeDtypeStruct((r, c), F32),
        compiler_params=_params(("parallel",)))(own, others)


def _add2(a, b, out_dtype, name):
    shape = a.shape
    a, b = a.reshape(-1, shape[-1]), b.reshape(-1, shape[-1])
    r, c = a.shape
    tr = math.gcd(ROWS_FLAT, r)

    def body(a_ref, b_ref, o_ref):
        o_ref[...] = (a_ref[...] + b_ref[...]).astype(out_dtype)

    blk = pl.BlockSpec((tr, c), lambda i: (i, 0))
    return pl.pallas_call(
        body, name=name, grid=(r // tr,), in_specs=[blk, blk], out_specs=blk,
        out_shape=jax.ShapeDtypeStruct((r, c), out_dtype), compiler_params=_params(("parallel",)))(a, b).reshape(shape)


def _adamw(w, g, m, v, name):
    r, c = w.shape
    tr = math.gcd(ROWS_FLAT, r)

    def body(w_ref, g_ref, m_ref, v_ref, d_ref, nm_ref, nv_ref):
        gv = g_ref[...]
        mv = ADAM_B1 * m_ref[...] + (1.0 - ADAM_B1) * gv
        vv = ADAM_B2 * v_ref[...] + (1.0 - ADAM_B2) * (gv * gv)
        m_hat = mv / (1.0 - ADAM_B1 ** ADAM_STEP)
        v_hat = vv / (1.0 - ADAM_B2 ** ADAM_STEP)
        d_ref[...] = -ADAM_LR * (m_hat / (jnp.sqrt(v_hat) + ADAM_EPS) + ADAM_WD * w_ref[...])
        nm_ref[...] = mv
        nv_ref[...] = vv

    blk = pl.BlockSpec((tr, c), lambda i: (i, 0))
    return pl.pallas_call(
        body, name=name, grid=(r // tr,), in_specs=[blk] * 4, out_specs=[blk] * 3,
        out_shape=[jax.ShapeDtypeStruct((r, c), F32)] * 3, compiler_params=_params(("parallel",)))(w, g, m, v)


_GROUP_MASKS = {
    'xy': ((1, 0, 0), (0, 1, 0), (1, 1, 0)),
    'c': ((0, 0, 1),),
    'xyc': ((0, 0, 1), (0, 1, 0), (0, 1, 1), (1, 0, 0), (1, 0, 1), (1, 1, 0), (1, 1, 1)),
}


def _group_rank(group, pos):
    x, y, c = pos
    return {'xy': 2 * x + y, 'c': c, 'xyc': 4 * x + 2 * y + c}[group]


def _exchange(src, group, scatter, name):
    masks = _GROUP_MASKS[group]
    n = len(masks) + 1
    shape = src.shape[1:] if scatter else src.shape
    assert (not scatter) or src.shape[0] == n

    def body(src_ref, out_ref, send_sems, recv_sems, local_sem):
        me = (lax.axis_index("x"), lax.axis_index("y"), lax.axis_index("c"))
        my_rank = _group_rank(group, me)

        def mine_for(rank):
            return src_ref.at[rank] if scatter else src_ref

        local = pltpu.make_async_copy(mine_for(my_rank), out_ref.at[my_rank], local_sem)
        local.start()
        sends = []
        for k, mask in enumerate(masks):
            peer = tuple(1 - p if f else p for p, f in zip(me, mask))
            peer_rank = _group_rank(group, peer)
            send = pltpu.make_async_remote_copy(
                src_ref=mine_for(peer_rank), dst_ref=out_ref.at[my_rank], send_sem=send_sems.at[k],
                recv_sem=recv_sems.at[k], device_id=peer, device_id_type=pl.DeviceIdType.MESH)
            send.start()
            sends.append((send, peer, peer_rank))
        for k, (send, peer, peer_rank) in enumerate(sends):
            pltpu.make_async_remote_copy(
                src_ref=mine_for(peer_rank), dst_ref=out_ref.at[peer_rank], send_sem=send_sems.at[k],
                recv_sem=recv_sems.at[k], device_id=peer, device_id_type=pl.DeviceIdType.MESH).wait_recv()
        for send, _, _ in sends:
            send.wait_send()
        local.wait()

    return pl.pallas_call(
        body, name=name, in_specs=[pl.BlockSpec(memory_space=pl.ANY)], out_specs=pl.BlockSpec(memory_space=pl.ANY),
        out_shape=jax.ShapeDtypeStruct((n,) + tuple(shape), src.dtype),
        scratch_shapes=[pltpu.SemaphoreType.DMA((n - 1,)), pltpu.SemaphoreType.DMA((n - 1,)), pltpu.SemaphoreType.DMA(())],
    )(src)


CHUNK_BYTES = 1 << 20
MIN_CHUNK_ROWS = 64


def _row_parts(rows, cols, itemsize):
    n = 1
    while rows % (2 * n) == 0 and rows // (2 * n) >= MIN_CHUNK_ROWS and rows * cols * itemsize > n * CHUNK_BYTES:
        n *= 2
    return n


def _remote(src, dst, send_sem, recv_sem, peer):
    return pltpu.make_async_remote_copy(src_ref=src, dst_ref=dst, send_sem=send_sem, recv_sem=recv_sem,
                                        device_id=peer, device_id_type=pl.DeviceIdType.MESH)


def _start_rows(src, dst, send_sem, recv_sem, peer):
    rows, cols = src.shape
    n = _row_parts(rows, cols, jnp.dtype(src.dtype).itemsize)
    pr = rows // n
    for i in range(n):
        _remote(src.at[pl.ds(i * pr, pr), :], dst.at[pl.ds(i * pr, pr), :], send_sem, recv_sem, peer).start()


def _mesh_place():
    x, y, c = lax.axis_index("x"), lax.axis_index("y"), lax.axis_index("c")
    other_chips = ((1 - x, y), (x, 1 - y), (1 - x, 1 - y))
    return x, y, c, 2 * x + y, (x, y, 1 - c), other_chips


def _comm_call(body, name, ins, out_shapes, sem_counts):
    any_spec = pl.BlockSpec(memory_space=pl.ANY)
    return pl.pallas_call(
        body, name=name, in_specs=[any_spec] * len(ins), out_specs=[any_spec] * len(out_shapes), out_shape=out_shapes,
        scratch_shapes=[pltpu.SemaphoreType.DMA((n,)) for n in sem_counts])(*ins)


class _ChipSends:
    def __init__(self, srcs, pick):
        self.srcs, self.pick = list(srcs), pick
        nt = len(self.srcs)
        self.n_sems = 3 * nt
        self.out_shapes = []
        for s in self.srcs:
            r2 = s.shape[-2] // 2 if pick is _pick_weight_half else s.shape[-2]
            self.out_shapes.append(jax.ShapeDtypeStruct((3, r2, s.shape[-1]), s.dtype))

    def start(self, ins, outs, send_sems, recv_sems):
        _, _, c, _, _, other_chips = _mesh_place()
        for t in range(len(ins)):
            for j, (px, py) in enumerate(other_chips):
                _remote(self.pick(ins[t], c, 2 * px + py), outs[t].at[j], send_sems.at[3 * t + j], recv_sems.at[3 * t + j],
                        (px, py, c)).start()

    def finish(self, ins, outs, send_sems, recv_sems):
        _, _, c, _, _, other_chips = _mesh_place()
        for t in range(len(ins)):
            for j, (px, py) in enumerate(other_chips):
                landed = _remote(outs[t].at[j], outs[t].at[j], send_sems.at[3 * t + j], recv_sems.at[3 * t + j], (px, py, c))
                landed.wait_recv()
                landed.wait_send()

    def run(self, name):
        nt = len(self.srcs)

        def body(*refs):
            ins, outs = refs[:nt], refs[nt:2 * nt]
            self.start(ins, outs, *refs[2 * nt:])
            self.finish(ins, outs, *refs[2 * nt:])

        return _comm_call(body, name, self.srcs, self.out_shapes, (self.n_sems, self.n_sems))

    def specs(self):
        any_spec = pl.BlockSpec(memory_space=pl.ANY)
        return [any_spec] * len(self.srcs), [any_spec] * len(self.srcs), [pltpu.SemaphoreType.DMA((self.n_sems,))] * 2


def _pick_weight_half(src, core, chip):
    r2 = src.shape[0] // 2
    return src.at[pl.ds(pl.multiple_of(core * r2, MIN_CHUNK_ROWS), r2), :]


def _pick_slab(src, core, chip):
    return src.at[chip if src.shape[0] == N_CHIPS else 0]


def _swap_halves(gs):
    nt = len(gs)

    def body(*refs):
        ins, outs = refs[:nt], refs[nt:2 * nt]
        send_sems, recv_sems = refs[2 * nt:]
        _, _, c, _, sibling, _ = _mesh_place()
        for t in range(nt):
            for q in range(ins[t].shape[1]):
                _start_rows(ins[t].at[1 - c, q], outs[t].at[q], send_sems.at[t], recv_sems.at[t], sibling)
        for t in range(nt):
            _remote(outs[t], outs[t], send_sems.at[t], recv_sems.at[t], sibling).wait_recv()
        for t in range(nt):
            _remote(outs[t], outs[t], send_sems.at[t], recv_sems.at[t], sibling).wait_send()

    out_shapes = [jax.ShapeDtypeStruct(g.shape[1:], g.dtype) for g in gs]
    return _comm_call(body, "swap_halves", gs, out_shapes, (nt, nt))


def _swap_cores(halves):
    nt = len(halves)

    def body(*refs):
        ins, outs = refs[:nt], refs[nt:2 * nt]
        send_sems, recv_sems = refs[2 * nt:]
        _, _, _, _, sibling, _ = _mesh_place()
        for t in range(nt):
            for li in range(ins[t].shape[0]):
                _start_rows(ins[t].at[li], outs[t].at[li], send_sems.at[t], recv_sems.at[t], sibling)
        for t in range(nt):
            _remote(outs[t], outs[t], send_sems.at[t], recv_sems.at[t], sibling).wait_recv()
        for t in range(nt):
            _remote(ins[t], ins[t], send_sems.at[t], recv_sems.at[t], sibling).wait_send()

    out_shapes = [jax.ShapeDtypeStruct(h.shape, h.dtype) for h in halves]
    return _comm_call(body, "swap_cores", halves, out_shapes, (nt, nt))


def _pack_rows(pieces, cols, dtype):
    tile = SUBLANES * cols
    rows = []
    for p in pieces:
        flat = p.reshape(-1).astype(dtype)
        rows.append(jnp.pad(flat, (0, (-flat.shape[0]) % tile)).reshape(-1, cols))
    total = sum(r.shape[0] for r in rows)
    if total % (2 * SUBLANES):
        rows.append(jnp.zeros((SUBLANES, cols), dtype))
    return jnp.concatenate(rows, axis=0)


def _unpack_rows(buf, shapes):
    out, row = [], 0
    for s in shapes:
        n = math.prod(s)
        nrows = -(-n // (SUBLANES * buf.shape[1])) * SUBLANES
        out.append(buf[row:row + nrows].reshape(-1)[:n].reshape(s))
        row += nrows
    return out


ROW_SHARDED = ('w_out', 'w_glu')


def _chips_to_full(name, chips):
    mats = [c.reshape(-1, c.shape[-1]) for c in chips]
    if name in ROW_SHARDED:
        return jnp.concatenate(mats, axis=0)
    if name != 'w_in':
        return jnp.concatenate(mats, axis=1)
    cs = mats[0].shape[1]
    pieces = []
    for start, size in _PERM_PIECES:
        lo = start
        while lo < start + size:
            q = lo // cs
            hi = min(start + size, (q + 1) * cs)
            pieces.append(mats[q][:, lo - q * cs:hi - q * cs])
            lo = hi
    return jnp.concatenate(pieces, axis=1)


def _full_to_dests(name, full):
    rows, cols = full.shape
    if name in ROW_SHARDED:
        return jnp.transpose(full.reshape(N_CHIPS, 2, rows // (2 * N_CHIPS), cols), (1, 0, 2, 3))
    cs = cols // N_CHIPS
    if name != 'w_in':
        dests = [full[:, q * cs:(q + 1) * cs] for q in range(N_CHIPS)]
    else:
        offsets, off = [], 0
        for _, size in _PERM_PIECES:
            offsets.append(off)
            off += size
        by_start = sorted(zip(_PERM_PIECES, offsets))
        dests = []
        for q in range(N_CHIPS):
            parts = []
            for (start, size), at in by_start:
                lo, hi = max(start, q * cs), min(start + size, (q + 1) * cs)
                if lo < hi:
                    parts.append(full[:, at + lo - start:at + hi - start])
            dests.append(jnp.concatenate(parts, axis=1))
    return jnp.stack([d.reshape(2, rows // 2, cs) for d in dests], axis=1)


def _by_chip(own, related, x, y):
    grid = ((own, related[1]), (related[0], related[2]))
    along_x = [[jnp.where(x == 0, grid[px][dy], grid[1 - px][dy]) for dy in range(2)] for px in range(2)]
    return [jnp.where(y == 0, along_x[px][py], along_x[px][1 - py]) for px in range(2) for py in range(2)]


def _join_halves(mine, theirs, core, axis):
    return jnp.where(core == 0, jnp.concatenate([mine, theirs], axis=axis), jnp.concatenate([theirs, mine], axis=axis))


ROWS_WGRAD = 2048
SSM_RAW = ('ssm_a_re', 'ssm_a_im', 'ssm_log_dt', 'ssm_b_re', 'ssm_b_im', 'ssm_c_re', 'ssm_c_im')


def _all_tables(p):
    raw = tuple(p[k] for k in SSM_RAW)
    (_, bb, cb), vjp = jax.vjp(jax.vmap(_ssm_tables), *raw)
    return (bb.astype(BF16), cb.astype(BF16), jax.vmap(_ssm_powers)(*raw[:3])), vjp


def _layer_fwd(x, p, w_in, rest, li, mod, tables, sends=None):
    bb, cb, pw = (t[li] for t in tables)
    h = _norm_fwd(x, _row(p['norm_g'][li]), mod)
    if isinstance(rest, dict):
        proj = _matmul(h, w_in, 'nn', tm=512, tn=1280, tk=1024, name="proj", n_outer=True)
    else:
        proj, arrived = _matmul(h, w_in, 'nn', tm=512, tn=1280, tk=1024, name="proj_sending", n_outer=True, sends=rest[0])
        rest = rest[1](arrived)
    full = dict(rest, w_in=w_in)
    o = _attention_fwd(proj, p['attn_sinks'][li])
    y_lin, states, brought = _ssm_fwd(proj, bb, cb, pw, sends)
    y_pool = _pool_fwd(proj, p['w_pool'][li].astype(BF16), _row(p['pool_scale'][li]))
    x_new = _merge_fwd(proj, o, y_lin, y_pool, x, _row(p['ssm_d'][li]), full['w_glu'], _row(p['b_glu'][li]),
                       full['w_br_att'], full['w_br_ssm'], full['w_br_pool'], full['w_out'], mod)
    saved = dict(x=x, h=h, proj=proj, o=o, y_lin=y_lin, y_pool=y_pool, states=states, bb=bb, cb=cb, pw=pw, full=full)
    return x_new, saved, brought


def _layer_bwd(dxo, s, p, li, mod, sends=None, w_in_sender=None):
    l = dxo.shape[0]
    proj, full = s['proj'], s['full']
    (do, dyl, dus_skip, dyp, dproj, ya, ys, ypl, y2, dt, dpa, dps, dpp, mg, dout, st) = _merge_bwd(
        proj, s['o'], s['y_lin'], s['y_pool'], dxo, _row(p['ssm_d'][li]), full['w_glu'], _row(p['b_glu'][li]),
        full['w_br_att'], full['w_br_ssm'], full['w_br_pool'], full['w_out'], mod)
    g = {}
    g['w_br_att'] = _matmul(ya, dpa, 'tn', tm=512, tn=1024, tk=ROWS_WGRAD, name="grad_w_br")
    g['w_br_ssm'] = _matmul(ys, dps, 'tn', tm=512, tn=1024, tk=ROWS_WGRAD, name="grad_w_br")
    g['w_br_pool'] = _matmul(ypl, dpp, 'tn', tm=512, tn=1024, tk=ROWS_WGRAD, name="grad_w_br")
    g['w_out'] = _matmul(mg, dout, 'tn', tm=512, tn=1024, tk=ROWS_WGRAD, name="grad_w_out")
    g['w_glu'] = _matmul(y2, dt, 'tn', tm=512, tn=512, tk=ROWS_WGRAD, name="grad_w_glu")
    g['b_glu'] = st[1, 0:SSM_W]
    g['ssm_d'] = st[1, SSM_W:2 * SSM_W]
    dgate = st[0]

    dq, dkv, dsink = _attention_bwd(proj, p['attn_sinks'][li], do)
    g['attn_sinks'] = dsink[:, 0]
    dus_scan, dbb, dcb, dlam, brought = _ssm_bwd(proj, s['states'], dyl, s['bb'], s['cb'], s['pw'], sends)
    g['ssm_tables'] = (dlam[:, 0:2, :], dbb, dcb)
    dup, dwp, dps_scale = _pool_bwd(proj, p['w_pool'][li].astype(BF16), _row(p['pool_scale'][li]), dyp)
    g['w_pool'] = dwp
    g['pool_scale'] = dps_scale[0]

    for off, piece in ((OFF_Q, dq), (OFF_US, dus_skip + dus_scan), (OFF_UP, dup), (OFF_KV, dkv[HALO:])):
        dproj = lax.dynamic_update_slice(dproj, piece.astype(BF16), (0, off))
    g['w_in'] = _matmul(s['h'], dproj, 'tn', tm=1024, tn=1280, tk=ROWS_WGRAD, name="grad_w_in")
    if w_in_sender is None:
        dh, brought_late = _matmul(dproj, full['w_in'], 'nt', tm=1024, tn=1024, tk=1280, name="grad_h"), None
    else:
        dh, brought_late = _matmul(dproj, full['w_in'], 'nt', tm=1024, tn=1024, tk=1280, name="grad_h_sending",
                                   sends=w_in_sender(g['w_in']))
    dx, nst = _norm_bwd(s['x'], dh, dxo, _row(p['norm_g'][li]), mod)
    g['norm_g'] = nst[2]
    dmod = jnp.concatenate([nst[0], nst[1], dgate])
    del l
    return dx, g, dmod, brought, brought_late


BIG_NAMES = tuple(name for name, _ in BIG)


def _weight_sends(shards, names, li):
    return _ChipSends([shards[name][li] for name in names], _pick_weight_half)


def _assemble_layer(shards, names, li, over_ici, place):
    x, y, core, _ = place
    from_sibling = _swap_cores(over_ici)
    full = {}
    for name, a, b in zip(names, over_ici, from_sibling):
        own = shards[name][li]
        own_halves = own.reshape(2, own.shape[0] // 2, own.shape[1])
        halves = [_join_halves(a[j][None], b[j][None], core, 0) for j in range(3)]
        full[name] = _chips_to_full(name, _by_chip(own_halves, halves, x, y))
    return full


def _pair_sums(tensors, wire, core):
    theirs = _swap_halves(tensors)
    return [_add2(lax.dynamic_index_in_dim(g, core, 0, keepdims=False), a, dt, "sum_core_pair")
            for g, a, dt in zip(tensors, theirs, wire)]


def _layer_grad_halves(layer_grads, names):
    return [_full_to_dests(name, layer_grads[name]) for name in names]


def _step(p, m, v, x, c, target):
    depth = p['norm_g'].shape[0]
    d = D_MODEL
    ix, iy, ic = lax.axis_index("x"), lax.axis_index("y"), lax.axis_index("c")
    chip = 2 * ix + iy
    dev = 4 * ix + 2 * iy + ic
    x0 = x[0]

    c_pad = jnp.pad(c, ((0, SUBLANES - 1), (0, 0)))
    c_all = _exchange(c_pad, 'xyc', False, "gather_c")[:, 0, :]
    n_ada = p['w_ada'].shape[-1]
    b_shard = lax.dynamic_slice_in_dim(p['b_ada'], chip * n_ada, n_ada, axis=1)[:, None, :]
    mod_shard = _ada_fwd(c_all, p['w_ada'].astype(BF16), b_shard)
    mod_all = _exchange(mod_shard.reshape(depth * N_DEV, n_ada), 'xy', False, "gather_mod")
    mod_all = jnp.transpose(mod_all.reshape(N_CHIPS, depth, N_DEV, n_ada), (1, 2, 0, 3)).reshape(depth, N_DEV, 3 * d)
    mods = lax.dynamic_index_in_dim(mod_all, dev, axis=1, keepdims=True)

    place = (ix, iy, ic, chip)
    n_big = len(BIG)

    shards = {name: p[name].astype(BF16) for name, _ in BIG}
    tables, tables_vjp = _all_tables(p)
    others = BIG_NAMES[1:]
    w_in = _assemble_layer(shards, BIG_NAMES[:1], 0, _weight_sends(shards, BIG_NAMES[:1], 0).run("send_weight_halves"),
                           place)['w_in']
    rest = (_weight_sends(shards, others, 0), lambda arrived: _assemble_layer(shards, others, 0, arrived, place))
    saved = []
    xs = x0
    for li in range(depth):
        sends = _weight_sends(shards, BIG_NAMES, li + 1) if li + 1 < depth else None
        xs, s, arrived = _layer_fwd(xs, p, w_in, rest, li, mods[li], tables, sends)
        saved.append(s)
        if sends is not None:
            rest = _assemble_layer(shards, BIG_NAMES, li + 1, arrived, place)
            w_in = rest.pop('w_in')
    dx, hst = _loss_head(xs, _row(p['final_g']), target[0])
    loss = lax.psum(hst[1, 0], ("x", "y", "c"))

    grads, dmods, pair, from_chips = [None] * depth, [None] * depth, [None] * depth, [None] * depth
    first_w_in = {}

    def send_first_w_in(g_w_in):
        first_w_in['pair'] = _pair_sums([_full_to_dests('w_in', g_w_in)], [BF16], ic)
        return _ChipSends(first_w_in['pair'], _pick_slab)

    sends = None
    for li in reversed(range(depth)):
        dx, grads[li], dmods[li], arrived, arrived_late = _layer_bwd(dx, saved[li], p, li, mods[li], sends,
                                                                     send_first_w_in if li == 0 else None)
        if sends is not None:
            from_chips[li + 1] = arrived
        if li > 0:
            pair[li] = _pair_sums(_layer_grad_halves(grads[li], BIG_NAMES), [BF16] * n_big, ic)
            sends = _ChipSends(pair[li], _pick_slab)

    dmod_pad = jnp.pad(jnp.stack(dmods), ((0, SUBLANES - depth), (0, 0)))
    dmod_all = _exchange(dmod_pad, 'xyc', False, "gather_dmod")[:, :depth, :]
    dmod_cols = lax.dynamic_slice_in_dim(jnp.transpose(dmod_all, (1, 0, 2)), chip * n_ada, n_ada, axis=2)
    g_w_ada = _ada_bwd(jnp.transpose(c_all), dmod_cols)

    local_small = {k: jnp.stack([grads[li][k] for li in range(depth)])
                   for k in SMALL if k not in ('final_g', 'b_ada') + SSM_RAW}
    table_cotangents = tuple(jnp.stack([grads[li]['ssm_tables'][k] for li in range(depth)]) for k in range(3))
    local_small.update(zip(SSM_RAW, tables_vjp(table_cotangents)))
    local_small['final_g'] = hst[0]
    local_small['b_ada'] = jnp.stack(dmods)
    small_pack = _pack_rows([local_small[k] for k in SMALL], PACK_COLS, F32)

    small_halves = small_pack.reshape(2, 1, small_pack.shape[0] // 2, small_pack.shape[1])
    late = _pair_sums(_layer_grad_halves(grads[0], others) + [small_halves], [BF16] * len(others) + [F32], ic)
    pair[0] = first_w_in['pair'] + late
    from_chips[0] = list(arrived_late) + list(_ChipSends(late, _pick_slab).run("scatter_chips"))
    halves = []
    for li in range(depth):
        for t, (ps, others) in enumerate(zip(pair[li], from_chips[li])):
            small = t == n_big
            own = ps[0] if small else lax.dynamic_index_in_dim(ps, chip, 0, keepdims=False)
            halves.append(_sum_chips(own, others, small, "sum_chips_small" if small else "sum_chips")[None])
    totals = [_join_halves(h[0], o[0], ic, 0) for h, o in zip(halves, _swap_cores(halves))]
    small_sum = totals[n_big]
    per_layer = [totals[:n_big]] + [totals[n_big + 1 + (li - 1) * n_big:n_big + 1 + li * n_big] for li in range(1, depth)]

    small_shapes = [p[k].shape for k in SMALL]
    grad = dict(zip(SMALL, _unpack_rows(small_sum, small_shapes)))
    grad['w_ada'] = g_w_ada
    for t, (name, _) in enumerate(BIG):
        grad[name] = jnp.stack([per_layer[li][t] for li in range(depth)])

    delta, new_m, new_v = {}, {}, {}
    outs = _adamw(_pack_rows([p[k] for k in SMALL], PACK_COLS, F32), small_sum,
                  _pack_rows([m[k] for k in SMALL], PACK_COLS, F32), _pack_rows([v[k] for k in SMALL], PACK_COLS, F32),
                  name="adamw_small")
    for res, o in zip((delta, new_m, new_v), outs):
        res.update(zip(SMALL, _unpack_rows(o, small_shapes)))
    for name in ['w_ada'] + [n for n, _ in BIG]:
        shape = p[name].shape
        two_d = (-1, shape[-1])
        outs = _adamw(p[name].reshape(two_d), grad[name].reshape(two_d), m[name].reshape(two_d), v[name].reshape(two_d),
                      name="adamw_" + name)
        delta[name], new_m[name], new_v[name] = (o.reshape(shape) for o in outs)

    return (loss, dx[None], *[grad[k] for k in WEIGHTS], *[delta[k] for k in WEIGHTS],
            *[new_m[k] for k in WEIGHTS], *[new_v[k] for k in WEIGHTS])


def kernel(x, c, norm_g, w_ada, b_ada, w_in, attn_sinks, ssm_a_re, ssm_a_im, ssm_log_dt, ssm_b_re, ssm_b_im, ssm_c_re, ssm_c_im, ssm_d, w_glu, b_glu, w_pool, pool_scale, w_br_att, w_br_ssm, w_br_pool, w_out, final_g, loss_target, m_norm_g, m_w_ada, m_b_ada, m_w_in, m_attn_sinks, m_ssm_a_re, m_ssm_a_im, m_ssm_log_dt, m_ssm_b_re, m_ssm_b_im, m_ssm_c_re, m_ssm_c_im, m_ssm_d, m_w_glu, m_b_glu, m_w_pool, m_pool_scale, m_w_br_att, m_w_br_ssm, m_w_br_pool, m_w_out, m_final_g, v_norm_g, v_w_ada, v_b_ada, v_w_in, v_attn_sinks, v_ssm_a_re, v_ssm_a_im, v_ssm_log_dt, v_ssm_b_re, v_ssm_b_im, v_ssm_c_re, v_ssm_c_im, v_ssm_d, v_w_glu, v_b_glu, v_w_pool, v_pool_scale, v_w_br_att, v_w_br_ssm, v_w_br_pool, v_w_out, v_final_g):
    p = dict(zip(WEIGHTS, (norm_g, w_ada, b_ada, w_in, attn_sinks, ssm_a_re, ssm_a_im, ssm_log_dt, ssm_b_re, ssm_b_im,
                           ssm_c_re, ssm_c_im, ssm_d, w_glu, b_glu, w_pool, pool_scale, w_br_att, w_br_ssm, w_br_pool,
                           w_out, final_g)))
    m = dict(zip(WEIGHTS, (m_norm_g, m_w_ada, m_b_ada, m_w_in, m_attn_sinks, m_ssm_a_re, m_ssm_a_im, m_ssm_log_dt,
                           m_ssm_b_re, m_ssm_b_im, m_ssm_c_re, m_ssm_c_im, m_ssm_d, m_w_glu, m_b_glu, m_w_pool,
                           m_pool_scale, m_w_br_att, m_w_br_ssm, m_w_br_pool, m_w_out, m_final_g)))
    v = dict(zip(WEIGHTS, (v_norm_g, v_w_ada, v_b_ada, v_w_in, v_attn_sinks, v_ssm_a_re, v_ssm_a_im, v_ssm_log_dt,
                           v_ssm_b_re, v_ssm_b_im, v_ssm_c_re, v_ssm_c_im, v_ssm_d, v_w_glu, v_b_glu, v_w_pool,
                           v_pool_scale, v_w_br_att, v_w_br_ssm, v_w_br_pool, v_w_out, v_final_g)))
    return _step(p, m, v, x, c, loss_target)
```

```python
import functools
import math

import jax
import jax.numpy as jnp
from jax import lax
from jax.experimental import pallas as pl
from jax.experimental.pallas import tpu as pltpu

F32 = jnp.float32
BF16 = jnp.bfloat16

D_MODEL = 1024
CHUNK = 64
N_HEADS = 8
N_KV_HEADS = 2
HEAD_DIM = 64
Q_PER_KV = N_HEADS // N_KV_HEADS
HALO = 128
ATT_W = 512
KV_W = 128
SSM_W = 512
SSM_GROUP = 16
SSM_GROUPS = 32
SSM_STATE = 64
POOL_W = 512
POOL_GW = 128
IN_W = 6400
EPS = 1e-6
NEG_INF = -1e30
ADAM_LR = 0.001
ADAM_B1 = 0.9
ADAM_B2 = 0.999
ADAM_EPS = 1e-08
ADAM_WD = 0.01
ADAM_STEP = 10

OFF_GL, OFF_Q, OFF_ZA, OFF_US, OFF_ZS, OFF_UP, OFF_ZP, OFF_KV = 0, 3072, 3584, 4096, 4608, 5120, 5632, 6144
_PERM_PIECES = ((3328, 3072), (0, 512), (1792, 512), (768, 512), (2304, 512), (1280, 512), (2816, 512), (512, 256))

LANES = 128
SUBLANES = 8
N_SBLK = SSM_GROUPS * SSM_STATE // LANES
VMEM_LIMIT = 48 * 1024 * 1024

N_CHIPS = 4
N_DEV = 8

WEIGHTS = ['norm_g', 'w_ada', 'b_ada', 'w_in', 'attn_sinks', 'ssm_a_re', 'ssm_a_im', 'ssm_log_dt', 'ssm_b_re',
           'ssm_b_im', 'ssm_c_re', 'ssm_c_im', 'ssm_d', 'w_glu', 'b_glu', 'w_pool', 'pool_scale', 'w_br_att',
           'w_br_ssm', 'w_br_pool', 'w_out', 'final_g']
SMALL = ['norm_g', 'b_ada', 'attn_sinks', 'ssm_a_re', 'ssm_a_im', 'ssm_log_dt', 'ssm_b_re', 'ssm_b_im', 'ssm_c_re',
         'ssm_c_im', 'ssm_d', 'b_glu', 'w_pool', 'pool_scale', 'final_g']
BIG = (('w_in', (1024, 1600)), ('w_br_att', (512, 256)), ('w_br_ssm', (512, 256)), ('w_br_pool', (512, 256)),
       ('w_out', (256, 1024)), ('w_glu', (128, 512)))
PACK_COLS = 1024


def _params(sem=None):
    return pltpu.CompilerParams(dimension_semantics=sem, vmem_limit_bytes=VMEM_LIMIT)


def _row(v):
    return v.reshape(1, -1)


def _full(shape):
    nd = len(shape)
    return pl.BlockSpec(shape, lambda *_: (0,) * nd)


def _sigmoid(v):
    return 1.0 / (1.0 + jnp.exp(-v))


def _silu_and_grad(z):
    s = _sigmoid(z)
    return z * s, s * (1.0 + z * (1.0 - s))


_GELU_K = math.sqrt(2.0 / math.pi)


def _gelu_and_grad(v):
    inner = _GELU_K * (v + 0.044715 * v * v * v)
    th = jnp.tanh(inner)
    val = 0.5 * v * (1.0 + th)
    grad = 0.5 * (1.0 + th) + 0.5 * v * (1.0 - th * th) * _GELU_K * (1.0 + 3 * 0.044715 * v * v)
    return val, grad


_NN = (((1,), (0,)), ((), ()))
_NT = (((1,), (1,)), ((), ()))
_TN = (((0,), (0,)), ((), ()))


def _dot(a, b, dims=_NN):
    return lax.dot_general(a.astype(BF16), b.astype(BF16), dims, preferred_element_type=F32)


def _matmul(a, b, mode, *, tm, tn, tk, name, out_dtype=F32, n_outer=False, sends=None):
    if mode == 'nn':
        (m, k), (_, n) = a.shape, b.shape
    elif mode == 'nt':
        (m, k), (n, _) = a.shape, b.shape
    else:
        (k, m), (_, n) = a.shape, b.shape
    tm, tn, tk = min(tm, m), min(tn, n), min(tk, k)
    assert m % tm == 0 and n % tn == 0 and k % tk == 0, (name, a.shape, b.shape)
    nk = k // tk
    dims = {'nn': _NN, 'nt': _NT, 'tn': _TN}[mode]

    def body(a_ref, b_ref, o_ref, acc_ref):
        if nk == 1:
            o_ref[...] = _dot(a_ref[...], b_ref[...], dims).astype(out_dtype)
            return
        kk = pl.program_id(2)

        @pl.when(kk == 0)
        def _():
            acc_ref[...] = jnp.zeros_like(acc_ref)

        acc_ref[...] += _dot(a_ref[...], b_ref[...], dims)

        @pl.when(kk == nk - 1)
        def _():
            o_ref[...] = acc_ref[...].astype(out_dtype)

    def spec(shape, index):
        if n_outer:
            return pl.BlockSpec(shape, lambda j, i, kk: index(i, j, kk))
        return pl.BlockSpec(shape, index)

    a_spec = spec((tk, tm), lambda i, j, kk: (kk, i)) if mode == 'tn' else spec((tm, tk), lambda i, j, kk: (i, kk))
    b_spec = spec((tn, tk), lambda i, j, kk: (j, kk)) if mode == 'nt' else spec((tk, tn), lambda i, j, kk: (kk, j))
    grid = (n // tn, m // tm, nk) if n_outer else (m // tm, n // tn, nk)
    body, x_in, x_out, x_shapes, x_scratch, x_args = _carry_sends(body, 2, 1, 1, grid, sends)
    semantics = ("parallel", "parallel", "arbitrary") if sends is None else ("arbitrary",) * 3
    outs = pl.pallas_call(
        body, name=name, grid=grid, in_specs=[a_spec, b_spec] + x_in,
        out_specs=[spec((tm, tn), lambda i, j, kk: (i, j))] + x_out,
        out_shape=[jax.ShapeDtypeStruct((m, n), out_dtype)] + x_shapes,
        scratch_shapes=[pltpu.VMEM((tm, tn), F32)] + x_scratch, compiler_params=_params(semantics))(a, b, *x_args)
    return outs[0] if sends is None else (outs[0], list(outs[1:]))


ROWS_NORM = 512


def _norm_fwd(x, g, mod):
    l, d = x.shape
    tr = min(ROWS_NORM, l)

    def body(x_ref, g_ref, mod_ref, h_ref):
        xv = x_ref[...]
        r = lax.rsqrt(jnp.mean(xv * xv, axis=-1, keepdims=True) + EPS)
        shift, scale = mod_ref[:, 0:d], mod_ref[:, d:2 * d]
        h_ref[...] = ((xv * r * g_ref[...]) * (1.0 + scale) + shift).astype(BF16)

    return pl.pallas_call(
        body, name="norm_fwd", grid=(l // tr,),
        in_specs=[pl.BlockSpec((tr, d), lambda i: (i, 0)), _full((1, d)), _full((1, 3 * d))],
        out_specs=pl.BlockSpec((tr, d), lambda i: (i, 0)), out_shape=jax.ShapeDtypeStruct((l, d), BF16),
        compiler_params=_params(("parallel",)))(x, g, mod)


def _norm_bwd(x, dh, dxo, g, mod):
    l, d = x.shape
    tr = min(ROWS_NORM, l)

    def body(x_ref, dh_ref, dxo_ref, g_ref, mod_ref, dx_ref, st_ref):
        @pl.when(pl.program_id(0) == 0)
        def _():
            st_ref[...] = jnp.zeros_like(st_ref)

        xv, dhv = x_ref[...], dh_ref[...]
        r = lax.rsqrt(jnp.mean(xv * xv, axis=-1, keepdims=True) + EPS)
        xn = xv * r
        gv = g_ref[...]
        sc1 = 1.0 + mod_ref[:, d:2 * d]
        dxn = dhv * gv * sc1
        dx_ref[...] = dxo_ref[...] + r * (dxn - xn * jnp.mean(dxn * xn, axis=-1, keepdims=True))
        st_ref[0:1, :] += jnp.sum(dhv, axis=0, keepdims=True)
        st_ref[1:2, :] += jnp.sum(dhv * xn * gv, axis=0, keepdims=True)
        st_ref[2:3, :] += jnp.sum(dhv * xn * sc1, axis=0, keepdims=True)

    blk = pl.BlockSpec((tr, d), lambda i: (i, 0))
    return pl.pallas_call(
        body, name="norm_bwd", grid=(l // tr,),
        in_specs=[blk, blk, blk, _full((1, d)), _full((1, 3 * d))],
        out_specs=[blk, _full((SUBLANES, d))],
        out_shape=[jax.ShapeDtypeStruct((l, d), F32), jax.ShapeDtypeStruct((SUBLANES, d), F32)],
        compiler_params=_params(("arbitrary",)))(x, dh, dxo, g, mod)


def _loss_head(x, g, target):
    l, d = x.shape
    tr = min(ROWS_NORM, l)

    def body(x_ref, g_ref, t_ref, dx_ref, st_ref):
        @pl.when(pl.program_id(0) == 0)
        def _():
            st_ref[...] = jnp.zeros_like(st_ref)

        xv = x_ref[...]
        r = lax.rsqrt(jnp.mean(xv * xv, axis=-1, keepdims=True) + EPS)
        xn = xv * r
        gv = g_ref[...]
        err = xn * gv - t_ref[...]
        part = 0.5 * jnp.sum(jnp.mean(err * err, axis=-1, keepdims=True), axis=0, keepdims=True)
        dy = err * (1.0 / d)
        dxn = dy * gv
        dx_ref[...] = r * (dxn - xn * jnp.mean(dxn * xn, axis=-1, keepdims=True))
        st_ref[0:1, :] += jnp.sum(dy * xn, axis=0, keepdims=True)
        st_ref[1:2, :] += jnp.broadcast_to(part, (1, d))

    blk = pl.BlockSpec((tr, d), lambda i: (i, 0))
    return pl.pallas_call(
        body, name="loss_head", grid=(l // tr,), in_specs=[blk, _full((1, d)), blk],
        out_specs=[blk, _full((SUBLANES, d))],
        out_shape=[jax.ShapeDtypeStruct((l, d), F32), jax.ShapeDtypeStruct((SUBLANES, d), F32)],
        compiler_params=_params(("arbitrary",)))(x, g, target)


ROWS_ATT = 128
ROWS_ATT_BWD = 256
_SLOPES = tuple(2.0 ** (-8.0 * (h + 1) / N_HEADS) for h in range(N_HEADS))


def _att_mask(i, t):
    r = lax.broadcasted_iota(jnp.int32, (t, t + HALO), 0)
    j = lax.broadcasted_iota(jnp.int32, (t, t + HALO), 1)
    dist = jnp.abs(r + HALO - j).astype(F32)
    rc, jc = r // CHUNK, j // CHUNK
    allowed = (jc >= rc) & (jc <= rc + 2) & ((j >= HALO) | (i > 0))
    return dist, allowed


def _att_probs(qh, k, dist, allowed, slope, sink):
    s = _dot(qh, k, _NT) * (1.0 / math.sqrt(HEAD_DIM)) - slope * dist
    s = jnp.where(allowed, s, NEG_INF)
    m = jnp.maximum(jnp.max(s, axis=1, keepdims=True), sink)
    e = jnp.exp(s - m)
    es = jnp.exp(sink - m)
    den = jnp.sum(e, axis=1, keepdims=True) + es
    return e / den, es / den


def _att_specs(t):
    q_spec = pl.BlockSpec((t, ATT_W), lambda i: (i, OFF_Q // ATT_W))
    kv_spec = pl.BlockSpec((t, 2 * KV_W), lambda i: (i, OFF_KV // (2 * KV_W)))
    halo_spec = pl.BlockSpec((HALO, 2 * KV_W), lambda i: (jnp.maximum(i * (t // HALO) - 1, 0), OFF_KV // (2 * KV_W)))
    return q_spec, kv_spec, halo_spec


def _attention_fwd(proj, sinks):
    l = proj.shape[0]
    t = min(ROWS_ATT, l)

    def body(sink_ref, q_ref, kv_ref, halo_ref, o_ref):
        dist, allowed = _att_mask(pl.program_id(0), t)
        kv = jnp.concatenate([halo_ref[...], kv_ref[...]], axis=0)
        for h in range(N_HEADS):
            kh = h // Q_PER_KV
            k = kv[:, kh * HEAD_DIM:(kh + 1) * HEAD_DIM]
            v = kv[:, KV_W + kh * HEAD_DIM:KV_W + (kh + 1) * HEAD_DIM]
            p, _ = _att_probs(q_ref[:, h * HEAD_DIM:(h + 1) * HEAD_DIM], k, dist, allowed, _SLOPES[h], sink_ref[h])
            o_ref[:, h * HEAD_DIM:(h + 1) * HEAD_DIM] = _dot(p, v)

    q_spec, kv_spec, halo_spec = _att_specs(t)
    return pl.pallas_call(
        body, name="attention_fwd", grid=(l // t,),
        in_specs=[pl.BlockSpec(memory_space=pltpu.SMEM), q_spec, kv_spec, halo_spec],
        out_specs=pl.BlockSpec((t, ATT_W), lambda i: (i, 0)), out_shape=jax.ShapeDtypeStruct((l, ATT_W), F32),
        compiler_params=_params(("parallel",)))(sinks, proj, proj, proj)


def _attention_bwd(proj, sinks, do):
    l = proj.shape[0]
    t = min(ROWS_ATT_BWD, l)

    def body(sink_ref, q_ref, kv_ref, halo_ref, do_ref, dq_ref, dkv_ref, dsink_ref):
        i = pl.program_id(0)

        @pl.when(i == 0)
        def _():
            dkv_ref[...] = jnp.zeros_like(dkv_ref)
            dsink_ref[...] = jnp.zeros_like(dsink_ref)

        dist, allowed = _att_mask(i, t)
        kv = jnp.concatenate([halo_ref[...], kv_ref[...]], axis=0)
        rows = pl.ds(pl.multiple_of(i * t, t), t + HALO)
        for kh in range(N_KV_HEADS):
            k = kv[:, kh * HEAD_DIM:(kh + 1) * HEAD_DIM]
            v = kv[:, KV_W + kh * HEAD_DIM:KV_W + (kh + 1) * HEAD_DIM]
            dk = jnp.zeros((t + HALO, HEAD_DIM), F32)
            dv = jnp.zeros((t + HALO, HEAD_DIM), F32)
            for h in range(kh * Q_PER_KV, (kh + 1) * Q_PER_KV):
                qh = q_ref[:, h * HEAD_DIM:(h + 1) * HEAD_DIM]
                doh = do_ref[:, h * HEAD_DIM:(h + 1) * HEAD_DIM]
                p, ps = _att_probs(qh, k, dist, allowed, _SLOPES[h], sink_ref[h])
                dp = _dot(doh, v, _NT)
                delta = jnp.sum(p * dp, axis=1, keepdims=True)
                ds = p * (dp - delta) * (1.0 / math.sqrt(HEAD_DIM))
                dsink_ref[h:h + 1, :] += jnp.broadcast_to(-jnp.sum(ps * delta, axis=0, keepdims=True), (1, LANES))
                dq_ref[:, h * HEAD_DIM:(h + 1) * HEAD_DIM] = _dot(ds, k).astype(BF16)
                dk = dk + _dot(ds, qh, _TN)
                dv = dv + _dot(p, doh, _TN)
            dkv_ref[rows, kh * HEAD_DIM:(kh + 1) * HEAD_DIM] += dk
            dkv_ref[rows, KV_W + kh * HEAD_DIM:KV_W + (kh + 1) * HEAD_DIM] += dv

    q_spec, kv_spec, halo_spec = _att_specs(t)
    blk = pl.BlockSpec((t, ATT_W), lambda i: (i, 0))
    return pl.pallas_call(
        body, name="attention_bwd", grid=(l // t,),
        in_specs=[pl.BlockSpec(memory_space=pltpu.SMEM), q_spec, kv_spec, halo_spec, blk],
        out_specs=[blk, _full((HALO + l, 2 * KV_W)), _full((N_HEADS, LANES))],
        out_shape=[jax.ShapeDtypeStruct((l, ATT_W), BF16), jax.ShapeDtypeStruct((HALO + l, 2 * KV_W), F32),
                   jax.ShapeDtypeStruct((N_HEADS, LANES), F32)],
        compiler_params=_params(("arbitrary",)))(sinks, proj, proj, proj, do)


ROWS_SSM = 2048
SSM_BLOCKS = 2
SCAN_UNROLL = 2
SSM_STEPS_PER_U = LANES // (SSM_BLOCKS * 2 * SSM_GROUP)


def _ssm_discretize(a_re, a_im, log_dt, b_re, b_im):
    lam = lax.complex(a_re, a_im)
    dt = jnp.exp(log_dt)[:, None]
    lam_bar = jnp.exp(lam * dt)
    b_bar = ((lam_bar - 1.0) / lam)[..., None] * lax.complex(b_re, b_im)
    return lam, dt, lam_bar, b_bar


def _ssm_block_diag(m):
    e = m.reshape(N_SBLK, 2, SSM_GROUP, SSM_STATE)
    e = e[:, :, :, None, :] * jnp.eye(2, dtype=m.dtype)[None, :, None, :, None]
    e = e.reshape(N_SBLK, 2 * SSM_GROUP, LANES)
    oh = jax.nn.one_hot(jnp.arange(N_SBLK) % 4, 4, dtype=m.dtype)
    return (oh[:, :, None, None] * e[:, None]).reshape(N_SBLK, LANES, LANES)


def _ssm_tables(a_re, a_im, log_dt, b_re, b_im, c_re, c_im):
    _, _, lam_bar, b_bar = _ssm_discretize(a_re, a_im, log_dt, b_re, b_im)
    lam_blk = jnp.stack([jnp.real(lam_bar).reshape(N_SBLK, LANES), jnp.imag(lam_bar).reshape(N_SBLK, LANES)], axis=1)
    bt = jnp.transpose(b_bar, (0, 2, 1))
    bb = jnp.concatenate([_ssm_block_diag(jnp.real(bt)), _ssm_block_diag(jnp.imag(bt))], axis=2)
    cb = jnp.concatenate([jnp.transpose(_ssm_block_diag(c_re), (0, 2, 1)),
                          jnp.transpose(_ssm_block_diag(-c_im), (0, 2, 1))], axis=1)
    return lam_blk, bb, cb


def _ssm_powers(a_re, a_im, log_dt):
    lam = lax.complex(a_re, a_im)
    dt = jnp.exp(log_dt)[:, None]
    k = jnp.arange(1, SUBLANES + 1, dtype=F32)
    pw = jnp.exp((lam * dt)[None] * k[:, None, None]).reshape(SUBLANES, N_SBLK, LANES)
    pw = jnp.transpose(pw, (1, 0, 2))
    rev = pw[:, ::-1]
    return jnp.concatenate([jnp.real(pw), jnp.imag(pw), jnp.real(rev), jnp.imag(rev)], axis=1)


def _scan_consts(pw_ref, b, reverse):
    row = lax.broadcasted_iota(jnp.int32, (SUBLANES, LANES), 0)
    sign = -1.0 if reverse else 1.0

    def power(k):
        return (jnp.broadcast_to(pw_ref[b, k - 1:k, :], (SUBLANES, LANES)),
                sign * jnp.broadcast_to(pw_ref[b, SUBLANES + k - 1:SUBLANES + k, :], (SUBLANES, LANES)))

    steps = []
    for d in (1, 2, 4):
        pr, pi = power(d)
        keep = (row < SUBLANES - d) if reverse else (row >= d)
        steps.append((d, jnp.where(keep, pr, 0.0), jnp.where(keep, pi, 0.0)))
    base = 2 * SUBLANES if reverse else 0
    return steps, pw_ref[b, base:base + SUBLANES, :], sign * pw_ref[b, base + SUBLANES:base + 2 * SUBLANES, :]


def _scan_tile(xr, xi, steps, reverse):
    for d, ar, ai in steps:
        shift = SUBLANES - d if reverse else d
        rr, ri = pltpu.roll(xr, shift, 0), pltpu.roll(xi, shift, 0)
        xr, xi = xr + (ar * rr - ai * ri), xi + (ar * ri + ai * rr)
    return xr, xi


def _bcast_row(v, r):
    return jnp.broadcast_to(v[r:r + 1, :], (SUBLANES, LANES))


def _re_im(b):
    return slice(2 * b * LANES, (2 * b + 1) * LANES), slice((2 * b + 1) * LANES, (2 * b + 2) * LANES)


def _scan_forward(s_scr, pw_ref, n_tiles):
    consts = [_scan_consts(pw_ref, b, False) for b in range(SSM_BLOCKS)]

    def tiles(g, carry):
        carry = list(carry)
        rows = [pl.ds(pl.multiple_of((g * SCAN_UNROLL + u) * SUBLANES, SUBLANES), SUBLANES) for u in range(SCAN_UNROLL)]
        loaded = [[(s_scr[r, _re_im(b)[0]], s_scr[r, _re_im(b)[1]]) for b in range(SSM_BLOCKS)] for r in rows]
        local = [[_scan_tile(xr, xi, consts[b][0], False) for b, (xr, xi) in enumerate(per_row)] for per_row in loaded]
        for r, per_row in zip(rows, local):
            for b, (xr, xi) in enumerate(per_row):
                _, pr, pi = consts[b]
                cr, ci = carry[b]
                sr, si = xr + (pr * cr - pi * ci), xi + (pr * ci + pi * cr)
                s_scr[r, _re_im(b)[0]] = sr
                s_scr[r, _re_im(b)[1]] = si
                carry[b] = (_bcast_row(sr, SUBLANES - 1), _bcast_row(si, SUBLANES - 1))
        return tuple(carry)

    zero = jnp.zeros((SUBLANES, LANES), F32)
    lax.fori_loop(0, n_tiles // SCAN_UNROLL, tiles, ((zero, zero),) * SSM_BLOCKS)


def _row_chunks(l):
    rc = min(ROWS_SSM, l)
    return rc, l // rc


def _carry_sends(body, n_in, n_out, n_scratch, grid, sends):
    if sends is None:
        return body, [], [], [], [], []
    k = len(sends.srcs)

    def carrying(*refs):
        ins, send_ins = refs[:n_in], refs[n_in:n_in + k]
        outs, send_outs = refs[n_in + k:n_in + k + n_out], refs[n_in + k + n_out:n_in + 2 * k + n_out]
        rest = refs[n_in + 2 * k + n_out:]
        scratch, sems = rest[:n_scratch], rest[n_scratch:]
        ids = [pl.program_id(axis) for axis in range(len(grid))]
        first = functools.reduce(jnp.logical_and, [i == 0 for i in ids])
        last = functools.reduce(jnp.logical_and, [i == n - 1 for i, n in zip(ids, grid)])

        @pl.when(first)
        def _():
            sends.start(send_ins, send_outs, *sems)

        body(*ins, *outs, *scratch)

        @pl.when(last)
        def _():
            sends.finish(send_ins, send_outs, *sems)

    in_specs, out_specs, scratch = sends.specs()
    return carrying, in_specs, out_specs, sends.out_shapes, scratch, sends.srcs


def _ssm_fwd(proj, bb, cb, pw, sends=None):
    l = proj.shape[0]
    rc, n_chunks = _row_chunks(l)

    def body(u_ref, bb_ref, cb_ref, pw_ref, y_ref, s_scr):
        j = pl.program_id(0)

        def fill(ci, _):
            rows = pl.ds(pl.multiple_of(ci * rc, rc), rc)
            uv = u_ref[rows, :]
            for b in range(SSM_BLOCKS):
                s_scr[rows, 2 * b * LANES:2 * (b + 1) * LANES] = _dot(uv, bb_ref[b])
            return 0

        lax.fori_loop(0, n_chunks, fill, 0)
        _scan_forward(s_scr, pw_ref, l // SUBLANES)

        @pl.when(j % SSM_STEPS_PER_U == 0)
        def _():
            y_ref[...] = jnp.zeros_like(y_ref)

        def emit(ci, _):
            rows = pl.ds(pl.multiple_of(ci * rc, rc), rc)
            for b in range(SSM_BLOCKS):
                y_ref[rows, :] += _dot(s_scr[rows, 2 * b * LANES:2 * (b + 1) * LANES], cb_ref[b])
            return 0

        lax.fori_loop(0, n_chunks, emit, 0)

    steps = N_SBLK // SSM_BLOCKS
    body, x_in, x_out, x_shapes, x_scratch, x_args = _carry_sends(body, 4, 2, 0, (steps,), sends)
    outs = pl.pallas_call(
        body, name="ssm_fwd" if sends is None else "ssm_fwd_sending", grid=(steps,),
        in_specs=[pl.BlockSpec((l, LANES), lambda j: (0, OFF_US // LANES + j // SSM_STEPS_PER_U)),
                  pl.BlockSpec((SSM_BLOCKS, LANES, 2 * LANES), lambda j: (j, 0, 0)),
                  pl.BlockSpec((SSM_BLOCKS, 2 * LANES, LANES), lambda j: (j, 0, 0)),
                  pl.BlockSpec((SSM_BLOCKS, 4 * SUBLANES, LANES), lambda j: (j, 0, 0))] + x_in,
        out_specs=[pl.BlockSpec((l, LANES), lambda j: (0, j // SSM_STEPS_PER_U)),
                   pl.BlockSpec((l, SSM_BLOCKS * 2 * LANES), lambda j: (0, j))] + x_out,
        out_shape=[jax.ShapeDtypeStruct((l, SSM_W), F32), jax.ShapeDtypeStruct((l, N_SBLK * 2 * LANES), F32)] + x_shapes,
        scratch_shapes=x_scratch, compiler_params=_params(("arbitrary",)))(proj, bb, cb, pw, *x_args)
    return outs[0], outs[1], list(outs[2:])


def _ssm_bwd(proj, states, dy, bb, cb, pw, sends=None):
    l = proj.shape[0]
    rc, n_chunks = _row_chunks(l)
    n_tiles = l // SUBLANES

    def body(u_ref, s_scr, dy_ref, bb_ref, cb_ref, pw_ref, du_ref, dbb_ref, dcb_ref, dlam_ref, a_scr):
        j = pl.program_id(0)
        dcb_ref[...] = jnp.zeros_like(dcb_ref)

        def through_c(ci, _):
            rows = pl.ds(pl.multiple_of(ci * rc, rc), rc)
            dyv = dy_ref[rows, :]
            for b in range(SSM_BLOCKS):
                lanes = slice(2 * b * LANES, 2 * (b + 1) * LANES)
                dcb_ref[b] += _dot(s_scr[rows, lanes], dyv, _TN)
                a_scr[rows, lanes] = _dot(dyv, cb_ref[b], _NT)
            return 0

        lax.fori_loop(0, n_chunks, through_c, 0)

        consts = [_scan_consts(pw_ref, b, True) for b in range(SSM_BLOCKS)]
        row = lax.broadcasted_iota(jnp.int32, (SUBLANES, LANES), 0)
        last = row == SUBLANES - 1

        def tiles(g, carry):
            carry = list(carry)
            rows = [pl.ds(pl.multiple_of((n_tiles - 1 - (g * SCAN_UNROLL + u)) * SUBLANES, SUBLANES), SUBLANES)
                    for u in range(SCAN_UNROLL)]
            loaded = [[(a_scr[r, _re_im(b)[0]], a_scr[r, _re_im(b)[1]]) for b in range(SSM_BLOCKS)] for r in rows]
            states = [[(s_scr[r, _re_im(b)[0]], s_scr[r, _re_im(b)[1]]) for b in range(SSM_BLOCKS)] for r in rows]
            local = [[_scan_tile(xr, xi, consts[b][0], True) for b, (xr, xi) in enumerate(per_row)] for per_row in loaded]
            for r, per_row, state_row in zip(rows, local, states):
                for b, ((xr, xi), (sr, si)) in enumerate(zip(per_row, state_row)):
                    _, pr, pi = consts[b]
                    cr, ci, accr, acci = carry[b]
                    gr, gi = xr + (pr * cr - pi * ci), xi + (pr * ci + pi * cr)
                    a_scr[r, _re_im(b)[0]] = gr
                    a_scr[r, _re_im(b)[1]] = gi
                    ur = jnp.where(last, cr, pltpu.roll(gr, SUBLANES - 1, 0))
                    ui = jnp.where(last, ci, pltpu.roll(gi, SUBLANES - 1, 0))
                    carry[b] = (_bcast_row(gr, 0), _bcast_row(gi, 0),
                                accr + (ur * sr + ui * si), acci + (ui * sr - ur * si))
            return tuple(carry)

        zero = jnp.zeros((SUBLANES, LANES), F32)
        done = lax.fori_loop(0, n_tiles // SCAN_UNROLL, tiles, ((zero, zero, zero, zero),) * SSM_BLOCKS)
        for b in range(SSM_BLOCKS):
            dlr = jnp.broadcast_to(jnp.sum(done[b][2], axis=0, keepdims=True), (SUBLANES, LANES))
            dli = jnp.broadcast_to(jnp.sum(done[b][3], axis=0, keepdims=True), (SUBLANES, LANES))
            dlam_ref[b] = jnp.where(row == 0, dlr, jnp.where(row == 1, dli, 0.0))

        dbb_ref[...] = jnp.zeros_like(dbb_ref)

        @pl.when(j % SSM_STEPS_PER_U == 0)
        def _():
            du_ref[...] = jnp.zeros_like(du_ref)

        def through_b(ci, _):
            rows = pl.ds(pl.multiple_of(ci * rc, rc), rc)
            uv = u_ref[rows, :]
            for b in range(SSM_BLOCKS):
                av = a_scr[rows, 2 * b * LANES:2 * (b + 1) * LANES]
                dbb_ref[b] += _dot(uv, av, _TN)
                du_ref[rows, :] += _dot(av, bb_ref[b], _NT)
            return 0

        lax.fori_loop(0, n_chunks, through_b, 0)

    steps = N_SBLK // SSM_BLOCKS
    body, x_in, x_out, x_shapes, x_scratch, x_args = _carry_sends(body, 6, 4, 1, (steps,), sends)
    outs = pl.pallas_call(
        body, name="ssm_bwd" if sends is None else "ssm_bwd_sending", grid=(steps,),
        in_specs=[pl.BlockSpec((l, LANES), lambda j: (0, OFF_US // LANES + j // SSM_STEPS_PER_U)),
                  pl.BlockSpec((l, SSM_BLOCKS * 2 * LANES), lambda j: (0, j)),
                  pl.BlockSpec((l, LANES), lambda j: (0, j // SSM_STEPS_PER_U)),
                  pl.BlockSpec((SSM_BLOCKS, LANES, 2 * LANES), lambda j: (j, 0, 0)),
                  pl.BlockSpec((SSM_BLOCKS, 2 * LANES, LANES), lambda j: (j, 0, 0)),
                  pl.BlockSpec((SSM_BLOCKS, 4 * SUBLANES, LANES), lambda j: (j, 0, 0))] + x_in,
        out_specs=[pl.BlockSpec((l, LANES), lambda j: (0, j // SSM_STEPS_PER_U)),
                   pl.BlockSpec((SSM_BLOCKS, LANES, 2 * LANES), lambda j: (j, 0, 0)),
                   pl.BlockSpec((SSM_BLOCKS, 2 * LANES, LANES), lambda j: (j, 0, 0)),
                   pl.BlockSpec((SSM_BLOCKS, SUBLANES, LANES), lambda j: (j, 0, 0))] + x_out,
        out_shape=[jax.ShapeDtypeStruct((l, SSM_W), F32), jax.ShapeDtypeStruct((N_SBLK, LANES, 2 * LANES), F32),
                   jax.ShapeDtypeStruct((N_SBLK, 2 * LANES, LANES), F32),
                   jax.ShapeDtypeStruct((N_SBLK, SUBLANES, LANES), F32)] + x_shapes,
        scratch_shapes=[pltpu.VMEM((l, SSM_BLOCKS * 2 * LANES), F32)] + x_scratch,
        compiler_params=_params(("arbitrary",)))(proj, states, dy, bb, cb, pw, *x_args)
    return outs[0], outs[1], outs[2], outs[3], list(outs[4:])


def _pool_windows(v, t, l, ahead):
    def shifted(a, d):
        if ahead:
            return jnp.where(t < l - d, pltpu.roll(a, l - d, 0), 0.0)
        return jnp.where(t >= d, pltpu.roll(a, d, 0), 0.0)

    w2 = v + shifted(v, 1)
    w4 = w2 + shifted(w2, 2)
    w8 = w4 + shifted(w4, 4)
    w16 = w8 + shifted(w8, 8)
    return w2, w4, w8, w16


def _pool_select(g, ws):
    return jnp.where(g == 0, ws[0], jnp.where(g == 1, ws[1], jnp.where(g == 2, ws[2], ws[3])))


def _pool_count(g, t):
    return jnp.minimum(t + 1, jnp.left_shift(2, g)).astype(F32)


def _pool_specs(l):
    return [pl.BlockSpec((l, POOL_GW), lambda g: (0, OFF_UP // POOL_GW + g)),
            pl.BlockSpec((1, POOL_GW, POOL_GW), lambda g: (g, 0, 0)),
            pl.BlockSpec((1, POOL_GW), lambda g: (0, g))]


def _pool_fwd(proj, w_pool, scale):
    l = proj.shape[0]

    def body(u_ref, w_ref, sc_ref, y_ref):
        g = pl.program_id(0)
        t = lax.broadcasted_iota(jnp.int32, (l, 1), 0)
        u = u_ref[...]
        pooled = _pool_select(g, _pool_windows(u, t, l, False)) / _pool_count(g, t) - u
        y_ref[...] = _dot(pooled, w_ref[0]) * sc_ref[...]

    return pl.pallas_call(
        body, name="pool_fwd", grid=(4,), in_specs=_pool_specs(l),
        out_specs=pl.BlockSpec((l, POOL_GW), lambda g: (0, g)), out_shape=jax.ShapeDtypeStruct((l, POOL_W), F32),
        compiler_params=_params(("parallel",)))(proj, w_pool, scale)


def _pool_bwd(proj, w_pool, scale, dy):
    l = proj.shape[0]

    def body(u_ref, w_ref, sc_ref, dy_ref, du_ref, dw_ref, dsc_ref):
        g = pl.program_id(0)
        t = lax.broadcasted_iota(jnp.int32, (l, 1), 0)
        u = u_ref[...]
        cnt = _pool_count(g, t)
        pooled = _pool_select(g, _pool_windows(u, t, l, False)) / cnt - u
        dyv = dy_ref[...]
        dsc_ref[...] = jnp.sum(dyv * _dot(pooled, w_ref[0]), axis=0, keepdims=True)
        dyl = dyv * sc_ref[...]
        dw_ref[0] = _dot(pooled, dyl, _TN)
        dpooled = _dot(dyl, w_ref[0], _NT)
        du_ref[...] = (_pool_select(g, _pool_windows(dpooled / cnt, t, l, True)) - dpooled).astype(BF16)

    return pl.pallas_call(
        body, name="pool_bwd", grid=(4,), in_specs=_pool_specs(l) + [pl.BlockSpec((l, POOL_GW), lambda g: (0, g))],
        out_specs=[pl.BlockSpec((l, POOL_GW), lambda g: (0, g)), pl.BlockSpec((1, POOL_GW, POOL_GW), lambda g: (g, 0, 0)),
                   pl.BlockSpec((1, POOL_GW), lambda g: (0, g))],
        out_shape=[jax.ShapeDtypeStruct((l, POOL_W), BF16), jax.ShapeDtypeStruct((4, POOL_GW, POOL_GW), F32),
                   jax.ShapeDtypeStruct((1, POOL_W), F32)],
        compiler_params=_params(("parallel",)))(proj, w_pool, scale, dy)


ROWS_MERGE = 128
ROWS_MERGE_FWD = 256


def _merge_inputs(tr):
    def col(off, w):
        return pl.BlockSpec((tr, w), lambda i: (i, off // w))

    def act(w):
        return pl.BlockSpec((tr, w), lambda i: (i, 0))

    d = D_MODEL
    return ([col(OFF_GL, 3 * d), col(OFF_ZA, ATT_W), col(OFF_US, SSM_W), col(OFF_ZS, SSM_W), col(OFF_ZP, POOL_W),
             act(ATT_W), act(SSM_W), act(POOL_W)]
            + [_full((1, SSM_W)), _full((SSM_W, SSM_W)), _full((1, SSM_W)), _full((ATT_W, d)), _full((SSM_W, d)),
               _full((POOL_W, d)), _full((d, d)), _full((1, 3 * d))])


def _merge_forward_math(gl_ref, za_ref, us_ref, zs_ref, zp_ref, o_ref, yl_ref, yp_ref, d_ref, wg_ref, bg_ref,
                        wba_ref, wbs_ref, wbp_ref):
    d = D_MODEL
    r = {}
    r['sa'], r['dsa'] = _silu_and_grad(za_ref[...])
    r['ss'], r['dss'] = _silu_and_grad(zs_ref[...])
    r['sp'], r['dsp'] = _silu_and_grad(zp_ref[...])
    r['y_att'] = o_ref[...] * r['sa']
    y1 = yl_ref[...] + d_ref[...] * us_ref[...]
    r['y2'], r['dgelu'] = _gelu_and_grad(y1)
    r['sg'] = _sigmoid(_dot(r['y2'], wg_ref[...]) + bg_ref[...])
    r['y3'] = r['y2'] * r['sg']
    r['y_ssm'] = r['y3'] * r['ss']
    r['y_pool'] = yp_ref[...] * r['sp']
    r['g'] = [_sigmoid(gl_ref[:, b * d:(b + 1) * d]) for b in range(3)]
    r['p'] = [_dot(r['y_att'], wba_ref[...]), _dot(r['y_ssm'], wbs_ref[...]), _dot(r['y_pool'], wbp_ref[...])]
    r['merged'] = r['g'][0] * r['p'][0] + r['g'][1] * r['p'][1] + r['g'][2] * r['p'][2]
    return r


def _merge_fwd(proj, o, y_lin, y_pool, x, ssm_d, w_glu, b_glu, wba, wbs, wbp, w_out, mod, sends=None):
    l, d = x.shape
    tr = min(ROWS_MERGE_FWD, l)

    def body(gl_ref, za_ref, us_ref, zs_ref, zp_ref, o_ref, yl_ref, yp_ref, d_ref, wg_ref, bg_ref, wba_ref, wbs_ref,
             wbp_ref, wo_ref, mod_ref, x_ref, xn_ref):
        r = _merge_forward_math(gl_ref, za_ref, us_ref, zs_ref, zp_ref, o_ref, yl_ref, yp_ref, d_ref, wg_ref, bg_ref,
                                wba_ref, wbs_ref, wbp_ref)
        xn_ref[...] = x_ref[...] + mod_ref[:, 2 * d:3 * d] * _dot(r['merged'], wo_ref[...])

    blk = pl.BlockSpec((tr, d), lambda i: (i, 0))
    in_specs = _merge_inputs(tr) + [blk]
    body, x_in, x_out, x_shapes, x_scratch, x_args = _carry_sends(body, len(in_specs), 1, 0, (l // tr,), sends)
    outs = pl.pallas_call(
        body, name="merge_fwd" if sends is None else "merge_fwd_sending", grid=(l // tr,), in_specs=in_specs + x_in,
        out_specs=[blk] + x_out, out_shape=[jax.ShapeDtypeStruct((l, d), F32)] + x_shapes, scratch_shapes=x_scratch,
        compiler_params=_params(("arbitrary",)))(
            proj, proj, proj, proj, proj, o, y_lin, y_pool, ssm_d, w_glu, b_glu, wba, wbs, wbp, w_out, mod, x, *x_args)
    return outs[0], list(outs[1:])


def _merge_bwd(proj, o, y_lin, y_pool, dxo, ssm_d, w_glu, b_glu, wba, wbs, wbp, w_out, mod, sends=None):
    l, d = dxo.shape
    tr = min(ROWS_MERGE, l)

    def body(gl_ref, za_ref, us_ref, zs_ref, zp_ref, o_ref, yl_ref, yp_ref, d_ref, wg_ref, bg_ref, wba_ref, wbs_ref,
             wbp_ref, wo_ref, mod_ref, dxo_ref,
             do_ref, dyl_ref, dus_ref, dyp_ref, dpj_ref,
             ya_ref, ys_ref, ypl_ref, y2_ref, dt_ref, dpa_ref, dps_ref, dpp_ref, mg_ref, dout_ref, st_ref):
        @pl.when(pl.program_id(0) == 0)
        def _():
            st_ref[...] = jnp.zeros_like(st_ref)

        for off, width in ((OFF_Q, ATT_W), (OFF_US, SSM_W), (OFF_UP, POOL_W), (OFF_KV, 2 * KV_W)):
            dpj_ref[:, off:off + width] = jnp.zeros((tr, width), BF16)

        r = _merge_forward_math(gl_ref, za_ref, us_ref, zs_ref, zp_ref, o_ref, yl_ref, yp_ref, d_ref, wg_ref, bg_ref,
                                wba_ref, wbs_ref, wbp_ref)
        dxov = dxo_ref[...]
        out = _dot(r['merged'], wo_ref[...])
        st_ref[0:1, :] += jnp.sum(dxov * out, axis=0, keepdims=True)
        dout = dxov * mod_ref[:, 2 * d:3 * d]
        dmerged = _dot(dout, wo_ref[...], _NT)
        dys = []
        for b, (w_ref, dp_ref) in enumerate(((wba_ref, dpa_ref), (wbs_ref, dps_ref), (wbp_ref, dpp_ref))):
            gb = r['g'][b]
            dpj_ref[:, OFF_GL + b * d:OFF_GL + (b + 1) * d] = (dmerged * r['p'][b] * gb * (1.0 - gb)).astype(BF16)
            dp = dmerged * gb
            dp_ref[...] = dp.astype(BF16)
            dys.append(_dot(dp, w_ref[...], _NT))
        mg_ref[...] = r['merged'].astype(BF16)
        dout_ref[...] = dout.astype(BF16)
        ya_ref[...] = r['y_att'].astype(BF16)
        ys_ref[...] = r['y_ssm'].astype(BF16)
        ypl_ref[...] = r['y_pool'].astype(BF16)
        y2_ref[...] = r['y2'].astype(BF16)
        do_ref[...] = (dys[0] * r['sa']).astype(BF16)
        dpj_ref[:, OFF_ZA:OFF_ZA + ATT_W] = (dys[0] * o_ref[...] * r['dsa']).astype(BF16)
        dy3 = dys[1] * r['ss']
        dpj_ref[:, OFF_ZS:OFF_ZS + SSM_W] = (dys[1] * r['y3'] * r['dss']).astype(BF16)
        dt = dy3 * r['y2'] * r['sg'] * (1.0 - r['sg'])
        dt_ref[...] = dt.astype(BF16)
        dy1 = (dy3 * r['sg'] + _dot(dt, wg_ref[...], _NT)) * r['dgelu']
        dyl_ref[...] = dy1.astype(BF16)
        dus_ref[...] = dy1 * d_ref[...]
        st_ref[1:2, 0:SSM_W] += jnp.sum(dt, axis=0, keepdims=True)
        st_ref[1:2, SSM_W:2 * SSM_W] += jnp.sum(dy1 * us_ref[...], axis=0, keepdims=True)
        dyp_ref[...] = dys[2] * r['sp']
        dpj_ref[:, OFF_ZP:OFF_ZP + POOL_W] = (dys[2] * yp_ref[...] * r['dsp']).astype(BF16)

    blk = pl.BlockSpec((tr, d), lambda i: (i, 0))
    half = pl.BlockSpec((tr, ATT_W), lambda i: (i, 0))
    wide = pl.BlockSpec((tr, IN_W), lambda i: (i, 0))
    sds = jax.ShapeDtypeStruct
    in_specs = _merge_inputs(tr) + [blk]
    out_specs = [half] * 4 + [wide] + [half] * 5 + [blk] * 5 + [_full((SUBLANES, d))]
    body, x_in, x_out, x_shapes, x_scratch, x_args = _carry_sends(body, len(in_specs), len(out_specs), 0, (l // tr,), sends)
    outs = pl.pallas_call(
        body, name="merge_bwd" if sends is None else "merge_bwd_sending", grid=(l // tr,), in_specs=in_specs + x_in,
        out_specs=out_specs + x_out,
        out_shape=[sds((l, ATT_W), t) for t in (BF16, BF16, F32, F32)] + [sds((l, IN_W), BF16)]
        + [sds((l, ATT_W), BF16)] * 5 + [sds((l, d), BF16)] * 5 + [sds((SUBLANES, d), F32)] + x_shapes,
        scratch_shapes=x_scratch, compiler_params=_params(("arbitrary",)))(
            proj, proj, proj, proj, proj, o, y_lin, y_pool, ssm_d, w_glu, b_glu, wba, wbs, wbp, w_out, mod, dxo, *x_args)
    return list(outs[:len(out_specs)]), list(outs[len(out_specs):])


def _ada_fwd(c_all, w_ada, b_shard):
    depth, d, n = w_ada.shape

    def body(c_ref, w_ref, b_ref, o_ref):
        act, _ = _silu_and_grad(c_ref[...])
        o_ref[0] = _dot(act, w_ref[0]) + b_ref[0]

    return pl.pallas_call(
        body, name="ada_fwd", grid=(depth,),
        in_specs=[_full((N_DEV, d)), pl.BlockSpec((1, d, n), lambda i: (i, 0, 0)), pl.BlockSpec((1, 1, n), lambda i: (i, 0, 0))],
        out_specs=pl.BlockSpec((1, N_DEV, n), lambda i: (i, 0, 0)), out_shape=jax.ShapeDtypeStruct((depth, N_DEV, n), F32),
        compiler_params=_params(("parallel",)))(c_all, w_ada, b_shard)


def _ada_bwd(c_all_t, dmod):
    d = c_all_t.shape[0]
    depth, _, n = dmod.shape

    def body(c_ref, dm_ref, o_ref):
        act, _ = _silu_and_grad(c_ref[...])
        acc = act[:, 0:1] * dm_ref[0, 0:1, :]
        for b in range(1, N_DEV):
            acc = acc + act[:, b:b + 1] * dm_ref[0, b:b + 1, :]
        o_ref[0] = acc

    return pl.pallas_call(
        body, name="ada_bwd", grid=(depth,),
        in_specs=[_full((d, N_DEV)), pl.BlockSpec((1, N_DEV, n), lambda i: (i, 0, 0))],
        out_specs=pl.BlockSpec((1, d, n), lambda i: (i, 0, 0)), out_shape=jax.ShapeDtypeStruct((depth, d, n), F32),
        compiler_params=_params(("parallel",)))(c_all_t, dmod)


ROWS_FLAT = 256


_RELATION_XOR = (0, 2, 1, 3)


def _sum_chips(own, others, chip_order, name):
    r, c = own.shape
    tr = math.gcd(ROWS_FLAT, r)

    def body(own_ref, oth_ref, o_ref):
        terms = [own_ref[...].astype(F32)] + [oth_ref[k].astype(F32) for k in range(3)]
        if chip_order:
            chip = 2 * lax.axis_index("x") + lax.axis_index("y")
            by_chip = []
            for q in range(N_CHIPS):
                rel = jnp.bitwise_xor(chip, q)
                pick = terms[3]
                for k in (2, 1, 0):
                    pick = jnp.where(rel == _RELATION_XOR[k], terms[k], pick)
                by_chip.append(pick)
            terms = by_chip
        o_ref[...] = ((terms[0] + terms[1]) + terms[2]) + terms[3]

    return pl.pallas_call(
        body, name=name, grid=(r // tr,),
        in_specs=[pl.BlockSpec((tr, c), lambda i: (i, 0)), pl.BlockSpec((3, tr, c), lambda i: (0, i, 0))],
        out_specs=pl.BlockSpec((tr, c), lambda i: (i, 0)), out_shape=jax.ShapeDtypeStruct((r, c), F32),
        compiler_params=_params(("parallel",)))(own, others)


def _add2(a, b, out_dtype, name):
    shape = a.shape
    a, b = a.reshape(-1, shape[-1]), b.reshape(-1, shape[-1])
    r, c = a.shape
    tr = math.gcd(ROWS_FLAT, r)

    def body(a_ref, b_ref, o_ref):
        o_ref[...] = (a_ref[...] + b_ref[...]).astype(out_dtype)

    blk = pl.BlockSpec((tr, c), lambda i: (i, 0))
    return pl.pallas_call(
        body, name=name, grid=(r // tr,), in_specs=[blk, blk], out_specs=blk,
        out_shape=jax.ShapeDtypeStruct((r, c), out_dtype), compiler_params=_params(("parallel",)))(a, b).reshape(shape)


def _adamw(w, g, m, v, name):
    r, c = w.shape
    tr = math.gcd(ROWS_FLAT, r)

    def body(w_ref, g_ref, m_ref, v_ref, d_ref, nm_ref, nv_ref):
        gv = g_ref[...]
        mv = ADAM_B1 * m_ref[...] + (1.0 - ADAM_B1) * gv
        vv = ADAM_B2 * v_ref[...] + (1.0 - ADAM_B2) * (gv * gv)
        m_hat = mv / (1.0 - ADAM_B1 ** ADAM_STEP)
        v_hat = vv / (1.0 - ADAM_B2 ** ADAM_STEP)
        d_ref[...] = -ADAM_LR * (m_hat / (jnp.sqrt(v_hat) + ADAM_EPS) + ADAM_WD * w_ref[...])
        nm_ref[...] = mv
        nv_ref[...] = vv

    blk = pl.BlockSpec((tr, c), lambda i: (i, 0))
    return pl.pallas_call(
        body, name=name, grid=(r // tr,), in_specs=[blk] * 4, out_specs=[blk] * 3,
        out_shape=[jax.ShapeDtypeStruct((r, c), F32)] * 3, compiler_params=_params(("parallel",)))(w, g, m, v)


_GROUP_MASKS = {
    'xy': ((1, 0, 0), (0, 1, 0), (1, 1, 0)),
    'c': ((0, 0, 1),),
    'xyc': ((0, 0, 1), (0, 1, 0), (0, 1, 1), (1, 0, 0), (1, 0, 1), (1, 1, 0), (1, 1, 1)),
}


def _group_rank(group, pos):
    x, y, c = pos
    return {'xy': 2 * x + y, 'c': c, 'xyc': 4 * x + 2 * y + c}[group]


def _exchange(src, group, scatter, name):
    masks = _GROUP_MASKS[group]
    n = len(masks) + 1
    shape = src.shape[1:] if scatter else src.shape
    assert (not scatter) or src.shape[0] == n

    def body(src_ref, out_ref, send_sems, recv_sems, local_sem):
        me = (lax.axis_index("x"), lax.axis_index("y"), lax.axis_index("c"))
        my_rank = _group_rank(group, me)

        def mine_for(rank):
            return src_ref.at[rank] if scatter else src_ref

        local = pltpu.make_async_copy(mine_for(my_rank), out_ref.at[my_rank], local_sem)
        local.start()
        sends = []
        for k, mask in enumerate(masks):
            peer = tuple(1 - p if f else p for p, f in zip(me, mask))
            peer_rank = _group_rank(group, peer)
            send = pltpu.make_async_remote_copy(
                src_ref=mine_for(peer_rank), dst_ref=out_ref.at[my_rank], send_sem=send_sems.at[k],
                recv_sem=recv_sems.at[k], device_id=peer, device_id_type=pl.DeviceIdType.MESH)
            send.start()
            sends.append((send, peer, peer_rank))
        for k, (send, peer, peer_rank) in enumerate(sends):
            pltpu.make_async_remote_copy(
                src_ref=mine_for(peer_rank), dst_ref=out_ref.at[peer_rank], send_sem=send_sems.at[k],
                recv_sem=recv_sems.at[k], device_id=peer, device_id_type=pl.DeviceIdType.MESH).wait_recv()
        for send, _, _ in sends:
            send.wait_send()
        local.wait()

    return pl.pallas_call(
        body, name=name, in_specs=[pl.BlockSpec(memory_space=pl.ANY)], out_specs=pl.BlockSpec(memory_space=pl.ANY),
        out_shape=jax.ShapeDtypeStruct((n,) + tuple(shape), src.dtype),
        scratch_shapes=[pltpu.SemaphoreType.DMA((n - 1,)), pltpu.SemaphoreType.DMA((n - 1,)), pltpu.SemaphoreType.DMA(())],
    )(src)


CHUNK_BYTES = 1 << 20
MIN_CHUNK_ROWS = 64


def _row_parts(rows, cols, itemsize):
    n = 1
    while rows % (2 * n) == 0 and rows // (2 * n) >= MIN_CHUNK_ROWS and rows * cols * itemsize > n * CHUNK_BYTES:
        n *= 2
    return n


def _remote(src, dst, send_sem, recv_sem, peer):
    return pltpu.make_async_remote_copy(src_ref=src, dst_ref=dst, send_sem=send_sem, recv_sem=recv_sem,
                                        device_id=peer, device_id_type=pl.DeviceIdType.MESH)


def _start_rows(src, dst, send_sem, recv_sem, peer):
    rows, cols = src.shape
    n = _row_parts(rows, cols, jnp.dtype(src.dtype).itemsize)
    pr = rows // n
    for i in range(n):
        _remote(src.at[pl.ds(i * pr, pr), :], dst.at[pl.ds(i * pr, pr), :], send_sem, recv_sem, peer).start()


def _mesh_place():
    x, y, c = lax.axis_index("x"), lax.axis_index("y"), lax.axis_index("c")
    other_chips = ((1 - x, y), (x, 1 - y), (1 - x, 1 - y))
    return x, y, c, 2 * x + y, (x, y, 1 - c), other_chips


def _comm_call(body, name, ins, out_shapes, sem_counts):
    any_spec = pl.BlockSpec(memory_space=pl.ANY)
    return pl.pallas_call(
        body, name=name, in_specs=[any_spec] * len(ins), out_specs=[any_spec] * len(out_shapes), out_shape=out_shapes,
        scratch_shapes=[pltpu.SemaphoreType.DMA((n,)) for n in sem_counts])(*ins)


class _Sends:
    def run(self, name):
        nt = len(self.srcs)

        def body(*refs):
            ins, outs = refs[:nt], refs[nt:2 * nt]
            self.start(ins, outs, *refs[2 * nt:])
            self.finish(ins, outs, *refs[2 * nt:])

        return _comm_call(body, name, self.srcs, self.out_shapes, (self.n_sems, self.n_sems))

    def specs(self):
        any_spec = pl.BlockSpec(memory_space=pl.ANY)
        return [any_spec] * len(self.srcs), [any_spec] * len(self.srcs), [pltpu.SemaphoreType.DMA((self.n_sems,))] * 2


class _SiblingSends(_Sends):
    def __init__(self, srcs):
        self.srcs = list(srcs)
        self.n_sems = len(self.srcs)
        self.out_shapes = [jax.ShapeDtypeStruct(g.shape[1:], g.dtype) for g in self.srcs]

    def start(self, ins, outs, send_sems, recv_sems):
        _, _, c, _, sibling, _ = _mesh_place()
        for t in range(len(ins)):
            for q in range(ins[t].shape[1]):
                _start_rows(ins[t].at[1 - c, q], outs[t].at[q], send_sems.at[t], recv_sems.at[t], sibling)

    def finish(self, ins, outs, send_sems, recv_sems):
        _, _, _, _, sibling, _ = _mesh_place()
        for t in range(len(ins)):
            arrived = _remote(outs[t], outs[t], send_sems.at[t], recv_sems.at[t], sibling)
            arrived.wait_recv()
            arrived.wait_send()


class _ChipSends(_Sends):
    def __init__(self, srcs, pick):
        self.srcs, self.pick = list(srcs), pick
        nt = len(self.srcs)
        self.n_sems = 3 * nt
        self.out_shapes = []
        for s in self.srcs:
            r2 = s.shape[-2] // 2 if pick is _pick_weight_half else s.shape[-2]
            self.out_shapes.append(jax.ShapeDtypeStruct((3, r2, s.shape[-1]), s.dtype))

    def start(self, ins, outs, send_sems, recv_sems):
        _, _, c, _, _, other_chips = _mesh_place()
        for t in range(len(ins)):
            for j, (px, py) in enumerate(other_chips):
                _remote(self.pick(ins[t], c, 2 * px + py), outs[t].at[j], send_sems.at[3 * t + j], recv_sems.at[3 * t + j],
                        (px, py, c)).start()

    def finish(self, ins, outs, send_sems, recv_sems):
        _, _, c, _, _, other_chips = _mesh_place()
        for t in range(len(ins)):
            for j, (px, py) in enumerate(other_chips):
                landed = _remote(outs[t].at[j], outs[t].at[j], send_sems.at[3 * t + j], recv_sems.at[3 * t + j], (px, py, c))
                landed.wait_recv()
                landed.wait_send()


def _pick_weight_half(src, core, chip):
    r2 = src.shape[0] // 2
    return src.at[pl.ds(pl.multiple_of(core * r2, MIN_CHUNK_ROWS), r2), :]


def _pick_slab(src, core, chip):
    return src.at[chip if src.shape[0] == N_CHIPS else 0]


def _swap_cores(halves):
    nt = len(halves)

    def body(*refs):
        ins, outs = refs[:nt], refs[nt:2 * nt]
        send_sems, recv_sems = refs[2 * nt:]
        _, _, _, _, sibling, _ = _mesh_place()
        for t in range(nt):
            for li in range(ins[t].shape[0]):
                _start_rows(ins[t].at[li], outs[t].at[li], send_sems.at[t], recv_sems.at[t], sibling)
        for t in range(nt):
            _remote(outs[t], outs[t], send_sems.at[t], recv_sems.at[t], sibling).wait_recv()
        for t in range(nt):
            _remote(ins[t], ins[t], send_sems.at[t], recv_sems.at[t], sibling).wait_send()

    out_shapes = [jax.ShapeDtypeStruct(h.shape, h.dtype) for h in halves]
    return _comm_call(body, "swap_cores", halves, out_shapes, (nt, nt))


def _pack_rows(pieces, cols, dtype):
    tile = SUBLANES * cols
    rows = []
    for p in pieces:
        flat = p.reshape(-1).astype(dtype)
        rows.append(jnp.pad(flat, (0, (-flat.shape[0]) % tile)).reshape(-1, cols))
    total = sum(r.shape[0] for r in rows)
    if total % (2 * SUBLANES):
        rows.append(jnp.zeros((SUBLANES, cols), dtype))
    return jnp.concatenate(rows, axis=0)


def _unpack_rows(buf, shapes):
    out, row = [], 0
    for s in shapes:
        n = math.prod(s)
        nrows = -(-n // (SUBLANES * buf.shape[1])) * SUBLANES
        out.append(buf[row:row + nrows].reshape(-1)[:n].reshape(s))
        row += nrows
    return out


ROW_SHARDED = ('w_out', 'w_glu')


def _chips_to_full(name, chips):
    mats = [c.reshape(-1, c.shape[-1]) for c in chips]
    if name in ROW_SHARDED:
        return jnp.concatenate(mats, axis=0)
    if name != 'w_in':
        return jnp.concatenate(mats, axis=1)
    cs = mats[0].shape[1]
    pieces = []
    for start, size in _PERM_PIECES:
        lo = start
        while lo < start + size:
            q = lo // cs
            hi = min(start + size, (q + 1) * cs)
            pieces.append(mats[q][:, lo - q * cs:hi - q * cs])
            lo = hi
    return jnp.concatenate(pieces, axis=1)


def _full_to_dests(name, full):
    rows, cols = full.shape
    if name in ROW_SHARDED:
        return jnp.transpose(full.reshape(N_CHIPS, 2, rows // (2 * N_CHIPS), cols), (1, 0, 2, 3))
    cs = cols // N_CHIPS
    if name != 'w_in':
        dests = [full[:, q * cs:(q + 1) * cs] for q in range(N_CHIPS)]
    else:
        offsets, off = [], 0
        for _, size in _PERM_PIECES:
            offsets.append(off)
            off += size
        by_start = sorted(zip(_PERM_PIECES, offsets))
        dests = []
        for q in range(N_CHIPS):
            parts = []
            for (start, size), at in by_start:
                lo, hi = max(start, q * cs), min(start + size, (q + 1) * cs)
                if lo < hi:
                    parts.append(full[:, at + lo - start:at + hi - start])
            dests.append(jnp.concatenate(parts, axis=1))
    return jnp.stack([d.reshape(2, rows // 2, cs) for d in dests], axis=1)


def _by_chip(own, related, x, y):
    grid = ((own, related[1]), (related[0], related[2]))
    along_x = [[jnp.where(x == 0, grid[px][dy], grid[1 - px][dy]) for dy in range(2)] for px in range(2)]
    return [jnp.where(y == 0, along_x[px][py], along_x[px][1 - py]) for px in range(2) for py in range(2)]


def _join_halves(mine, theirs, core, axis):
    return jnp.where(core == 0, jnp.concatenate([mine, theirs], axis=axis), jnp.concatenate([theirs, mine], axis=axis))


ROWS_WGRAD = 2048
SSM_RAW = ('ssm_a_re', 'ssm_a_im', 'ssm_log_dt', 'ssm_b_re', 'ssm_b_im', 'ssm_c_re', 'ssm_c_im')


def _all_tables(p):
    raw = tuple(p[k] for k in SSM_RAW)
    (_, bb, cb), vjp = jax.vjp(jax.vmap(_ssm_tables), *raw)
    return (bb.astype(BF16), cb.astype(BF16), jax.vmap(_ssm_powers)(*raw[:3])), vjp


def _layer_fwd(x, p, w_in, rest, li, mod, tables, sends=None, late_sends=None):
    bb, cb, pw = (t[li] for t in tables)
    h = _norm_fwd(x, _row(p['norm_g'][li]), mod)
    if isinstance(rest, dict):
        proj = _matmul(h, w_in, 'nn', tm=512, tn=1280, tk=1024, name="proj", n_outer=True)
    else:
        proj, arrived = _matmul(h, w_in, 'nn', tm=512, tn=1280, tk=1024, name="proj_sending", n_outer=True, sends=rest[0])
        rest = rest[1](arrived)
    full = dict(rest, w_in=w_in)
    o = _attention_fwd(proj, p['attn_sinks'][li])
    y_lin, states, brought = _ssm_fwd(proj, bb, cb, pw, sends)
    y_pool = _pool_fwd(proj, p['w_pool'][li].astype(BF16), _row(p['pool_scale'][li]))
    x_new, brought_late = _merge_fwd(proj, o, y_lin, y_pool, x, _row(p['ssm_d'][li]), full['w_glu'],
                                     _row(p['b_glu'][li]), full['w_br_att'], full['w_br_ssm'], full['w_br_pool'],
                                     full['w_out'], mod, late_sends)
    saved = dict(x=x, h=h, proj=proj, o=o, y_lin=y_lin, y_pool=y_pool, states=states, bb=bb, cb=cb, pw=pw, full=full)
    return x_new, saved, brought + brought_late


def _layer_bwd(dxo, s, p, li, mod, core, upper=None, w_in_sender=None):
    l = dxo.shape[0]
    proj, full = s['proj'], s['full']
    outs, theirs = _merge_bwd(
        proj, s['o'], s['y_lin'], s['y_pool'], dxo, _row(p['ssm_d'][li]), full['w_glu'], _row(p['b_glu'][li]),
        full['w_br_att'], full['w_br_ssm'], full['w_br_pool'], full['w_out'], mod,
        None if upper is None else _SiblingSends(upper))
    (do, dyl, dus_skip, dyp, dproj, ya, ys, ypl, y2, dt, dpa, dps, dpp, mg, dout, st) = outs
    pair = None if upper is None else _pair_sums(upper, [BF16] * len(upper), core, theirs)
    sends = None if upper is None else _ChipSends(pair, _pick_slab)
    g = {}
    g['w_br_att'] = _matmul(ya, dpa, 'tn', tm=512, tn=1024, tk=ROWS_WGRAD, name="grad_w_br")
    g['w_br_ssm'] = _matmul(ys, dps, 'tn', tm=512, tn=1024, tk=ROWS_WGRAD, name="grad_w_br")
    g['w_br_pool'] = _matmul(ypl, dpp, 'tn', tm=512, tn=1024, tk=ROWS_WGRAD, name="grad_w_br")
    g['w_out'] = _matmul(mg, dout, 'tn', tm=512, tn=1024, tk=ROWS_WGRAD, name="grad_w_out")
    g['w_glu'] = _matmul(y2, dt, 'tn', tm=512, tn=512, tk=ROWS_WGRAD, name="grad_w_glu")
    g['b_glu'] = st[1, 0:SSM_W]
    g['ssm_d'] = st[1, SSM_W:2 * SSM_W]
    dgate = st[0]

    dq, dkv, dsink = _attention_bwd(proj, p['attn_sinks'][li], do)
    g['attn_sinks'] = dsink[:, 0]
    dus_scan, dbb, dcb, dlam, brought = _ssm_bwd(proj, s['states'], dyl, s['bb'], s['cb'], s['pw'], sends)
    g['ssm_tables'] = (dlam[:, 0:2, :], dbb, dcb)
    dup, dwp, dps_scale = _pool_bwd(proj, p['w_pool'][li].astype(BF16), _row(p['pool_scale'][li]), dyp)
    g['w_pool'] = dwp
    g['pool_scale'] = dps_scale[0]

    for off, piece in ((OFF_Q, dq), (OFF_US, dus_skip + dus_scan), (OFF_UP, dup), (OFF_KV, dkv[HALO:])):
        dproj = lax.dynamic_update_slice(dproj, piece.astype(BF16), (0, off))
    g['w_in'] = _matmul(s['h'], dproj, 'tn', tm=1024, tn=1280, tk=ROWS_WGRAD, name="grad_w_in")
    if w_in_sender is None:
        dh, brought_late = _matmul(dproj, full['w_in'], 'nt', tm=1024, tn=1024, tk=1280, name="grad_h"), None
    else:
        dh, brought_late = _matmul(dproj, full['w_in'], 'nt', tm=1024, tn=1024, tk=1280, name="grad_h_sending",
                                   sends=w_in_sender(g['w_in']))
    dx, nst = _norm_bwd(s['x'], dh, dxo, _row(p['norm_g'][li]), mod)
    g['norm_g'] = nst[2]
    dmod = jnp.concatenate([nst[0], nst[1], dgate])
    del l
    return dx, g, dmod, (pair, brought), brought_late


BIG_NAMES = tuple(name for name, _ in BIG)


def _weight_sends(shards, names, li):
    return _ChipSends([shards[name][li] for name in names], _pick_weight_half)


def _assemble_layer(shards, names, li, over_ici, place):
    x, y, core, _ = place
    from_sibling = _swap_cores(over_ici)
    full = {}
    for name, a, b in zip(names, over_ici, from_sibling):
        own = shards[name][li]
        own_halves = own.reshape(2, own.shape[0] // 2, own.shape[1])
        halves = [_join_halves(a[j][None], b[j][None], core, 0) for j in range(3)]
        full[name] = _chips_to_full(name, _by_chip(own_halves, halves, x, y))
    return full


def _pair_sums(tensors, wire, core, theirs=None):
    if theirs is None:
        theirs = _SiblingSends(tensors).run("swap_halves")
    return [_add2(lax.dynamic_index_in_dim(g, core, 0, keepdims=False), a, dt, "sum_core_pair")
            for g, a, dt in zip(tensors, theirs, wire)]


def _layer_grad_halves(layer_grads, names):
    return [_full_to_dests(name, layer_grads[name]) for name in names]


def _step(p, m, v, x, c, target):
    depth = p['norm_g'].shape[0]
    d = D_MODEL
    ix, iy, ic = lax.axis_index("x"), lax.axis_index("y"), lax.axis_index("c")
    chip = 2 * ix + iy
    dev = 4 * ix + 2 * iy + ic
    x0 = x[0]

    c_pad = jnp.pad(c, ((0, SUBLANES - 1), (0, 0)))
    c_all = _exchange(c_pad, 'xyc', False, "gather_c")[:, 0, :]
    n_ada = p['w_ada'].shape[-1]
    b_shard = lax.dynamic_slice_in_dim(p['b_ada'], chip * n_ada, n_ada, axis=1)[:, None, :]
    mod_shard = _ada_fwd(c_all, p['w_ada'].astype(BF16), b_shard)
    mod_all = _exchange(mod_shard.reshape(depth * N_DEV, n_ada), 'xy', False, "gather_mod")
    mod_all = jnp.transpose(mod_all.reshape(N_CHIPS, depth, N_DEV, n_ada), (1, 2, 0, 3)).reshape(depth, N_DEV, 3 * d)
    mods = lax.dynamic_index_in_dim(mod_all, dev, axis=1, keepdims=True)

    place = (ix, iy, ic, chip)
    n_big = len(BIG)

    shards = {name: p[name].astype(BF16) for name, _ in BIG}
    tables, tables_vjp = _all_tables(p)
    others = BIG_NAMES[1:]
    w_in = _assemble_layer(shards, BIG_NAMES[:1], 0, _weight_sends(shards, BIG_NAMES[:1], 0).run("send_weight_halves"),
                           place)['w_in']
    rest = (_weight_sends(shards, others, 0), lambda arrived: _assemble_layer(shards, others, 0, arrived, place))
    saved = []
    xs = x0
    for li in range(depth):
        more = li + 1 < depth
        xs, s, arrived = _layer_fwd(xs, p, w_in, rest, li, mods[li], tables,
                                    _weight_sends(shards, BIG_NAMES[:1], li + 1) if more else None,
                                    _weight_sends(shards, others, li + 1) if more else None)
        saved.append(s)
        if more:
            rest = _assemble_layer(shards, BIG_NAMES, li + 1, arrived, place)
            w_in = rest.pop('w_in')
    dx, hst = _loss_head(xs, _row(p['final_g']), target[0])
    loss = lax.psum(hst[1, 0], ("x", "y", "c"))

    grads, dmods, pair, from_chips = [None] * depth, [None] * depth, [None] * depth, [None] * depth
    first_w_in = {}

    def send_first_w_in(g_w_in):
        first_w_in['pair'] = _pair_sums([_full_to_dests('w_in', g_w_in)], [BF16], ic)
        return _ChipSends(first_w_in['pair'], _pick_slab)

    upper = None
    for li in reversed(range(depth)):
        dx, grads[li], dmods[li], reduced, arrived_late = _layer_bwd(dx, saved[li], p, li, mods[li], ic, upper,
                                                                     send_first_w_in if li == 0 else None)
        if upper is not None:
            pair[li + 1], from_chips[li + 1] = reduced
        upper = _layer_grad_halves(grads[li], BIG_NAMES) if li > 0 else None

    dmod_pad = jnp.pad(jnp.stack(dmods), ((0, SUBLANES - depth), (0, 0)))
    dmod_all = _exchange(dmod_pad, 'xyc', False, "gather_dmod")[:, :depth, :]
    dmod_cols = lax.dynamic_slice_in_dim(jnp.transpose(dmod_all, (1, 0, 2)), chip * n_ada, n_ada, axis=2)
    g_w_ada = _ada_bwd(jnp.transpose(c_all), dmod_cols)

    local_small = {k: jnp.stack([grads[li][k] for li in range(depth)])
                   for k in SMALL if k not in ('final_g', 'b_ada') + SSM_RAW}
    table_cotangents = tuple(jnp.stack([grads[li]['ssm_tables'][k] for li in range(depth)]) for k in range(3))
    local_small.update(zip(SSM_RAW, tables_vjp(table_cotangents)))
    local_small['final_g'] = hst[0]
    local_small['b_ada'] = jnp.stack(dmods)
    small_pack = _pack_rows([local_small[k] for k in SMALL], PACK_COLS, F32)

    small_halves = small_pack.reshape(2, 1, small_pack.shape[0] // 2, small_pack.shape[1])
    late = _pair_sums(_layer_grad_halves(grads[0], others) + [small_halves], [BF16] * len(others) + [F32], ic)
    pair[0] = first_w_in['pair'] + late
    from_chips[0] = list(arrived_late) + list(_ChipSends(late, _pick_slab).run("scatter_chips"))
    halves = []
    for li in range(depth):
        for t, (ps, others) in enumerate(zip(pair[li], from_chips[li])):
            small = t == n_big
            own = ps[0] if small else lax.dynamic_index_in_dim(ps, chip, 0, keepdims=False)
            halves.append(_sum_chips(own, others, small, "sum_chips_small" if small else "sum_chips")[None])
    totals = [_join_halves(h[0], o[0], ic, 0) for h, o in zip(halves, _swap_cores(halves))]
    small_sum = totals[n_big]
    per_layer = [totals[:n_big]] + [totals[n_big + 1 + (li - 1) * n_big:n_big + 1 + li * n_big] for li in range(1, depth)]

    small_shapes = [p[k].shape for k in SMALL]
    grad = dict(zip(SMALL, _unpack_rows(small_sum, small_shapes)))
    grad['w_ada'] = g_w_ada
    for t, (name, _) in enumerate(BIG):
        grad[name] = jnp.stack([per_layer[li][t] for li in range(depth)])

    delta, new_m, new_v = {}, {}, {}
    outs = _adamw(_pack_rows([p[k] for k in SMALL], PACK_COLS, F32), small_sum,
                  _pack_rows([m[k] for k in SMALL], PACK_COLS, F32), _pack_rows([v[k] for k in SMALL], PACK_COLS, F32),
                  name="adamw_small")
    for res, o in zip((delta, new_m, new_v), outs):
        res.update(zip(SMALL, _unpack_rows(o, small_shapes)))
    for name in ['w_ada'] + [n for n, _ in BIG]:
        shape = p[name].shape
        two_d = (-1, shape[-1])
        outs = _adamw(p[name].reshape(two_d), grad[name].reshape(two_d), m[name].reshape(two_d), v[name].reshape(two_d),
                      name="adamw_" + name)
        delta[name], new_m[name], new_v[name] = (o.reshape(shape) for o in outs)

    return (loss, dx[None], *[grad[k] for k in WEIGHTS], *[delta[k] for k in WEIGHTS],
            *[new_m[k] for k in WEIGHTS], *[new_v[k] for k in WEIGHTS])


def kernel(x, c, norm_g, w_ada, b_ada, w_in, attn_sinks, ssm_a_re, ssm_a_im, ssm_log_dt, ssm_b_re, ssm_b_im, ssm_c_re, ssm_c_im, ssm_d, w_glu, b_glu, w_pool, pool_scale, w_br_att, w_br_ssm, w_br_pool, w_out, final_g, loss_target, m_norm_g, m_w_ada, m_b_ada, m_w_in, m_attn_sinks, m_ssm_a_re, m_ssm_a_im, m_ssm_log_dt, m_ssm_b_re, m_ssm_b_im, m_ssm_c_re, m_ssm_c_im, m_ssm_d, m_w_glu, m_b_glu, m_w_pool, m_pool_scale, m_w_br_att, m_w_br_ssm, m_w_br_pool, m_w_out, m_final_g, v_norm_g, v_w_ada, v_b_ada, v_w_in, v_attn_sinks, v_ssm_a_re, v_ssm_a_im, v_ssm_log_dt, v_ssm_b_re, v_ssm_b_im, v_ssm_c_re, v_ssm_c_im, v_ssm_d, v_w_glu, v_b_glu, v_w_pool, v_pool_scale, v_w_br_att, v_w_br_ssm, v_w_br_pool, v_w_out, v_final_g):
    p = dict(zip(WEIGHTS, (norm_g, w_ada, b_ada, w_in, attn_sinks, ssm_a_re, ssm_a_im, ssm_log_dt, ssm_b_re, ssm_b_im,
                           ssm_c_re, ssm_c_im, ssm_d, w_glu, b_glu, w_pool, pool_scale, w_br_att, w_br_ssm, w_br_pool,
                           w_out, final_g)))
    m = dict(zip(WEIGHTS, (m_norm_g, m_w_ada, m_b_ada, m_w_in, m_attn_sinks, m_ssm_a_re, m_ssm_a_im, m_ssm_log_dt,
                           m_ssm_b_re, m_ssm_b_im, m_ssm_c_re, m_ssm_c_im, m_ssm_d, m_w_glu, m_b_glu, m_w_pool,
                           m_pool_scale, m_w_br_att, m_w_br_ssm, m_w_br_pool, m_w_out, m_final_g)))
    v = dict(zip(WEIGHTS, (v_norm_g, v_w_ada, v_b_ada, v_w_in, v_attn_sinks, v_ssm_a_re, v_ssm_a_im, v_ssm_log_dt,
                           v_ssm_b_re, v_ssm_b_im, v_ssm_c_re, v_ssm_c_im, v_ssm_d, v_w_glu, v_b_glu, v_w_pool,
                           v_pool_scale, v_w_br_att, v_w_br_ssm, v_w_br_pool, v_w_out, v_final_g)))
    return _step(p, m, v, x, c, loss_target)
```

```python
import functools
import math

import jax
import jax.numpy as jnp
from jax import lax
from jax.experimental import pallas as pl
from jax.experimental.pallas import tpu as pltpu

F32 = jnp.float32
BF16 = jnp.bfloat16

D_MODEL = 1024
CHUNK = 64
N_HEADS = 8
N_KV_HEADS = 2
HEAD_DIM = 64
Q_PER_KV = N_HEADS // N_KV_HEADS
HALO = 128
ATT_W = 512
KV_W = 128
SSM_W = 512
SSM_GROUP = 16
SSM_GROUPS = 32
SSM_STATE = 64
POOL_W = 512
POOL_GW = 128
IN_W = 6400
EPS = 1e-6
NEG_INF = -1e30
ADAM_LR = 0.001
ADAM_B1 = 0.9
ADAM_B2 = 0.999
ADAM_EPS = 1e-08
ADAM_WD = 0.01
ADAM_STEP = 10

OFF_GL, OFF_Q, OFF_ZA, OFF_US, OFF_ZS, OFF_UP, OFF_ZP, OFF_KV = 0, 3072, 3584, 4096, 4608, 5120, 5632, 6144
_PERM_PIECES = ((3328, 3072), (0, 512), (1792, 512), (768, 512), (2304, 512), (1280, 512), (2816, 512), (512, 256))

LANES = 128
SUBLANES = 8
N_SBLK = SSM_GROUPS * SSM_STATE // LANES
VMEM_LIMIT = 48 * 1024 * 1024

N_CHIPS = 4
N_DEV = 8

WEIGHTS = ['norm_g', 'w_ada', 'b_ada', 'w_in', 'attn_sinks', 'ssm_a_re', 'ssm_a_im', 'ssm_log_dt', 'ssm_b_re',
           'ssm_b_im', 'ssm_c_re', 'ssm_c_im', 'ssm_d', 'w_glu', 'b_glu', 'w_pool', 'pool_scale', 'w_br_att',
           'w_br_ssm', 'w_br_pool', 'w_out', 'final_g']
SMALL = ['norm_g', 'b_ada', 'attn_sinks', 'ssm_a_re', 'ssm_a_im', 'ssm_log_dt', 'ssm_b_re', 'ssm_b_im', 'ssm_c_re',
         'ssm_c_im', 'ssm_d', 'b_glu', 'w_pool', 'pool_scale', 'final_g']
BIG = (('w_in', (1024, 1600)), ('w_br_att', (512, 256)), ('w_br_ssm', (512, 256)), ('w_br_pool', (512, 256)),
       ('w_out', (256, 1024)), ('w_glu', (128, 512)))
PACK_COLS = 1024


def _params(sem=None):
    return pltpu.CompilerParams(dimension_semantics=sem, vmem_limit_bytes=VMEM_LIMIT)


def _row(v):
    return v.reshape(1, -1)


def _full(shape):
    nd = len(shape)
    return pl.BlockSpec(shape, lambda *_: (0,) * nd)


def _sigmoid(v):
    return 1.0 / (1.0 + jnp.exp(-v))


def _silu_and_grad(z):
    s = _sigmoid(z)
    return z * s, s * (1.0 + z * (1.0 - s))


_GELU_K = math.sqrt(2.0 / math.pi)


def _gelu_and_grad(v):
    inner = _GELU_K * (v + 0.044715 * v * v * v)
    th = jnp.tanh(inner)
    val = 0.5 * v * (1.0 + th)
    grad = 0.5 * (1.0 + th) + 0.5 * v * (1.0 - th * th) * _GELU_K * (1.0 + 3 * 0.044715 * v * v)
    return val, grad


_NN = (((1,), (0,)), ((), ()))
_NT = (((1,), (1,)), ((), ()))
_TN = (((0,), (0,)), ((), ()))


def _dot(a, b, dims=_NN):
    return lax.dot_general(a.astype(BF16), b.astype(BF16), dims, preferred_element_type=F32)


def _matmul(a, b, mode, *, tm, tn, tk, name, out_dtype=F32, n_outer=False, sends=None):
    if mode == 'nn':
        (m, k), (_, n) = a.shape, b.shape
    elif mode == 'nt':
        (m, k), (n, _) = a.shape, b.shape
    else:
        (k, m), (_, n) = a.shape, b.shape
    tm, tn, tk = min(tm, m), min(tn, n), min(tk, k)
    assert m % tm == 0 and n % tn == 0 and k % tk == 0, (name, a.shape, b.shape)
    nk = k // tk
    dims = {'nn': _NN, 'nt': _NT, 'tn': _TN}[mode]

    def body(a_ref, b_ref, o_ref, acc_ref):
        if nk == 1:
            o_ref[...] = _dot(a_ref[...], b_ref[...], dims).astype(out_dtype)
            return
        kk = pl.program_id(2)

        @pl.when(kk == 0)
        def _():
            acc_ref[...] = jnp.zeros_like(acc_ref)

        acc_ref[...] += _dot(a_ref[...], b_ref[...], dims)

        @pl.when(kk == nk - 1)
        def _():
            o_ref[...] = acc_ref[...].astype(out_dtype)

    def spec(shape, index):
        if n_outer:
            return pl.BlockSpec(shape, lambda j, i, kk: index(i, j, kk))
        return pl.BlockSpec(shape, index)

    a_spec = spec((tk, tm), lambda i, j, kk: (kk, i)) if mode == 'tn' else spec((tm, tk), lambda i, j, kk: (i, kk))
    b_spec = spec((tn, tk), lambda i, j, kk: (j, kk)) if mode == 'nt' else spec((tk, tn), lambda i, j, kk: (kk, j))
    grid = (n // tn, m // tm, nk) if n_outer else (m // tm, n // tn, nk)
    body, x_in, x_out, x_shapes, x_scratch, x_args = _carry_sends(body, 2, 1, 1, grid, sends)
    semantics = ("parallel", "parallel", "arbitrary") if sends is None else ("arbitrary",) * 3
    outs = pl.pallas_call(
        body, name=name, grid=grid, in_specs=[a_spec, b_spec] + x_in,
        out_specs=[spec((tm, tn), lambda i, j, kk: (i, j))] + x_out,
        out_shape=[jax.ShapeDtypeStruct((m, n), out_dtype)] + x_shapes,
        scratch_shapes=[pltpu.VMEM((tm, tn), F32)] + x_scratch, compiler_params=_params(semantics))(a, b, *x_args)
    return outs[0] if sends is None else (outs[0], list(outs[1:]))


ROWS_NORM = 512


def _norm_fwd(x, g, mod):
    l, d = x.shape
    tr = min(ROWS_NORM, l)

    def body(x_ref, g_ref, mod_ref, h_ref):
        xv = x_ref[...]
        r = lax.rsqrt(jnp.mean(xv * xv, axis=-1, keepdims=True) + EPS)
        shift, scale = mod_ref[:, 0:d], mod_ref[:, d:2 * d]
        h_ref[...] = ((xv * r * g_ref[...]) * (1.0 + scale) + shift).astype(BF16)

    return pl.pallas_call(
        body, name="norm_fwd", grid=(l // tr,),
        in_specs=[pl.BlockSpec((tr, d), lambda i: (i, 0)), _full((1, d)), _full((1, 3 * d))],
        out_specs=pl.BlockSpec((tr, d), lambda i: (i, 0)), out_shape=jax.ShapeDtypeStruct((l, d), BF16),
        compiler_params=_params(("parallel",)))(x, g, mod)


def _norm_bwd(x, dh, dxo, g, mod):
    l, d = x.shape
    tr = min(ROWS_NORM, l)

    def body(x_ref, dh_ref, dxo_ref, g_ref, mod_ref, dx_ref, st_ref):
        @pl.when(pl.program_id(0) == 0)
        def _():
            st_ref[...] = jnp.zeros_like(st_ref)

        xv, dhv = x_ref[...], dh_ref[...]
        r = lax.rsqrt(jnp.mean(xv * xv, axis=-1, keepdims=True) + EPS)
        xn = xv * r
        gv = g_ref[...]
        sc1 = 1.0 + mod_ref[:, d:2 * d]
        dxn = dhv * gv * sc1
        dx_ref[...] = dxo_ref[...] + r * (dxn - xn * jnp.mean(dxn * xn, axis=-1, keepdims=True))
        st_ref[0:1, :] += jnp.sum(dhv, axis=0, keepdims=True)
        st_ref[1:2, :] += jnp.sum(dhv * xn * gv, axis=0, keepdims=True)
        st_ref[2:3, :] += jnp.sum(dhv * xn * sc1, axis=0, keepdims=True)

    blk = pl.BlockSpec((tr, d), lambda i: (i, 0))
    return pl.pallas_call(
        body, name="norm_bwd", grid=(l // tr,),
        in_specs=[blk, blk, blk, _full((1, d)), _full((1, 3 * d))],
        out_specs=[blk, _full((SUBLANES, d))],
        out_shape=[jax.ShapeDtypeStruct((l, d), F32), jax.ShapeDtypeStruct((SUBLANES, d), F32)],
        compiler_params=_params(("arbitrary",)))(x, dh, dxo, g, mod)


def _loss_head(x, g, target):
    l, d = x.shape
    tr = min(ROWS_NORM, l)

    def body(x_ref, g_ref, t_ref, dx_ref, st_ref):
        @pl.when(pl.program_id(0) == 0)
        def _():
            st_ref[...] = jnp.zeros_like(st_ref)

        xv = x_ref[...]
        r = lax.rsqrt(jnp.mean(xv * xv, axis=-1, keepdims=True) + EPS)
        xn = xv * r
        gv = g_ref[...]
        err = xn * gv - t_ref[...]
        part = 0.5 * jnp.sum(jnp.mean(err * err, axis=-1, keepdims=True), axis=0, keepdims=True)
        dy = err * (1.0 / d)
        dxn = dy * gv
        dx_ref[...] = r * (dxn - xn * jnp.mean(dxn * xn, axis=-1, keepdims=True))
        st_ref[0:1, :] += jnp.sum(dy * xn, axis=0, keepdims=True)
        st_ref[1:2, :] += jnp.broadcast_to(part, (1, d))

    blk = pl.BlockSpec((tr, d), lambda i: (i, 0))
    return pl.pallas_call(
        body, name="loss_head", grid=(l // tr,), in_specs=[blk, _full((1, d)), blk],
        out_specs=[blk, _full((SUBLANES, d))],
        out_shape=[jax.ShapeDtypeStruct((l, d), F32), jax.ShapeDtypeStruct((SUBLANES, d), F32)],
        compiler_params=_params(("arbitrary",)))(x, g, target)


ROWS_ATT = 128
ROWS_ATT_BWD = 256
_SLOPES = tuple(2.0 ** (-8.0 * (h + 1) / N_HEADS) for h in range(N_HEADS))


def _att_mask(i, t):
    r = lax.broadcasted_iota(jnp.int32, (t, t + HALO), 0)
    j = lax.broadcasted_iota(jnp.int32, (t, t + HALO), 1)
    dist = jnp.abs(r + HALO - j).astype(F32)
    rc, jc = r // CHUNK, j // CHUNK
    allowed = (jc >= rc) & (jc <= rc + 2) & ((j >= HALO) | (i > 0))
    return dist, allowed


def _att_probs(qh, k, dist, allowed, slope, sink):
    s = _dot(qh, k, _NT) * (1.0 / math.sqrt(HEAD_DIM)) - slope * dist
    s = jnp.where(allowed, s, NEG_INF)
    m = jnp.maximum(jnp.max(s, axis=1, keepdims=True), sink)
    e = jnp.exp(s - m)
    es = jnp.exp(sink - m)
    den = jnp.sum(e, axis=1, keepdims=True) + es
    return e / den, es / den


def _att_specs(t):
    q_spec = pl.BlockSpec((t, ATT_W), lambda i: (i, OFF_Q // ATT_W))
    kv_spec = pl.BlockSpec((t, 2 * KV_W), lambda i: (i, OFF_KV // (2 * KV_W)))
    halo_spec = pl.BlockSpec((HALO, 2 * KV_W), lambda i: (jnp.maximum(i * (t // HALO) - 1, 0), OFF_KV // (2 * KV_W)))
    return q_spec, kv_spec, halo_spec


def _attention_fwd(proj, sinks):
    l = proj.shape[0]
    t = min(ROWS_ATT, l)

    def body(sink_ref, q_ref, kv_ref, halo_ref, o_ref):
        dist, allowed = _att_mask(pl.program_id(0), t)
        kv = jnp.concatenate([halo_ref[...], kv_ref[...]], axis=0)
        for h in range(N_HEADS):
            kh = h // Q_PER_KV
            k = kv[:, kh * HEAD_DIM:(kh + 1) * HEAD_DIM]
            v = kv[:, KV_W + kh * HEAD_DIM:KV_W + (kh + 1) * HEAD_DIM]
            p, _ = _att_probs(q_ref[:, h * HEAD_DIM:(h + 1) * HEAD_DIM], k, dist, allowed, _SLOPES[h], sink_ref[h])
            o_ref[:, h * HEAD_DIM:(h + 1) * HEAD_DIM] = _dot(p, v)

    q_spec, kv_spec, halo_spec = _att_specs(t)
    return pl.pallas_call(
        body, name="attention_fwd", grid=(l // t,),
        in_specs=[pl.BlockSpec(memory_space=pltpu.SMEM), q_spec, kv_spec, halo_spec],
        out_specs=pl.BlockSpec((t, ATT_W), lambda i: (i, 0)), out_shape=jax.ShapeDtypeStruct((l, ATT_W), F32),
        compiler_params=_params(("parallel",)))(sinks, proj, proj, proj)


def _attention_bwd(proj, sinks, do, sends=None):
    l = proj.shape[0]
    t = min(ROWS_ATT_BWD, l)

    def body(sink_ref, q_ref, kv_ref, halo_ref, do_ref, dq_ref, dkv_ref, dsink_ref):
        i = pl.program_id(0)

        @pl.when(i == 0)
        def _():
            dkv_ref[...] = jnp.zeros_like(dkv_ref)
            dsink_ref[...] = jnp.zeros_like(dsink_ref)

        dist, allowed = _att_mask(i, t)
        kv = jnp.concatenate([halo_ref[...], kv_ref[...]], axis=0)
        rows = pl.ds(pl.multiple_of(i * t, t), t + HALO)
        for kh in range(N_KV_HEADS):
            k = kv[:, kh * HEAD_DIM:(kh + 1) * HEAD_DIM]
            v = kv[:, KV_W + kh * HEAD_DIM:KV_W + (kh + 1) * HEAD_DIM]
            dk = jnp.zeros((t + HALO, HEAD_DIM), F32)
            dv = jnp.zeros((t + HALO, HEAD_DIM), F32)
            for h in range(kh * Q_PER_KV, (kh + 1) * Q_PER_KV):
                qh = q_ref[:, h * HEAD_DIM:(h + 1) * HEAD_DIM]
                doh = do_ref[:, h * HEAD_DIM:(h + 1) * HEAD_DIM]
                p, ps = _att_probs(qh, k, dist, allowed, _SLOPES[h], sink_ref[h])
                dp = _dot(doh, v, _NT)
                delta = jnp.sum(p * dp, axis=1, keepdims=True)
                ds = p * (dp - delta) * (1.0 / math.sqrt(HEAD_DIM))
                dsink_ref[h:h + 1, :] += jnp.broadcast_to(-jnp.sum(ps * delta, axis=0, keepdims=True), (1, LANES))
                dq_ref[:, h * HEAD_DIM:(h + 1) * HEAD_DIM] = _dot(ds, k).astype(BF16)
                dk = dk + _dot(ds, qh, _TN)
                dv = dv + _dot(p, doh, _TN)
            dkv_ref[rows, kh * HEAD_DIM:(kh + 1) * HEAD_DIM] += dk
            dkv_ref[rows, KV_W + kh * HEAD_DIM:KV_W + (kh + 1) * HEAD_DIM] += dv

    q_spec, kv_spec, halo_spec = _att_specs(t)
    blk = pl.BlockSpec((t, ATT_W), lambda i: (i, 0))
    body, x_in, x_out, x_shapes, x_scratch, x_args = _carry_sends(body, 5, 3, 0, (l // t,), sends)
    outs = pl.pallas_call(
        body, name="attention_bwd" if sends is None else "attention_bwd_sending", grid=(l // t,),
        in_specs=[pl.BlockSpec(memory_space=pltpu.SMEM), q_spec, kv_spec, halo_spec, blk] + x_in,
        out_specs=[blk, _full((HALO + l, 2 * KV_W)), _full((N_HEADS, LANES))] + x_out,
        out_shape=[jax.ShapeDtypeStruct((l, ATT_W), BF16), jax.ShapeDtypeStruct((HALO + l, 2 * KV_W), F32),
                   jax.ShapeDtypeStruct((N_HEADS, LANES), F32)] + x_shapes,
        scratch_shapes=x_scratch, compiler_params=_params(("arbitrary",)))(sinks, proj, proj, proj, do, *x_args)
    return outs[0], outs[1], outs[2], list(outs[3:])


ROWS_SSM = 2048
SSM_BLOCKS = 2
SCAN_UNROLL = 2
SSM_STEPS_PER_U = LANES // (SSM_BLOCKS * 2 * SSM_GROUP)


def _ssm_discretize(a_re, a_im, log_dt, b_re, b_im):
    lam = lax.complex(a_re, a_im)
    dt = jnp.exp(log_dt)[:, None]
    lam_bar = jnp.exp(lam * dt)
    b_bar = ((lam_bar - 1.0) / lam)[..., None] * lax.complex(b_re, b_im)
    return lam, dt, lam_bar, b_bar


def _ssm_block_diag(m):
    e = m.reshape(N_SBLK, 2, SSM_GROUP, SSM_STATE)
    e = e[:, :, :, None, :] * jnp.eye(2, dtype=m.dtype)[None, :, None, :, None]
    e = e.reshape(N_SBLK, 2 * SSM_GROUP, LANES)
    oh = jax.nn.one_hot(jnp.arange(N_SBLK) % 4, 4, dtype=m.dtype)
    return (oh[:, :, None, None] * e[:, None]).reshape(N_SBLK, LANES, LANES)


def _ssm_tables(a_re, a_im, log_dt, b_re, b_im, c_re, c_im):
    _, _, lam_bar, b_bar = _ssm_discretize(a_re, a_im, log_dt, b_re, b_im)
    lam_blk = jnp.stack([jnp.real(lam_bar).reshape(N_SBLK, LANES), jnp.imag(lam_bar).reshape(N_SBLK, LANES)], axis=1)
    bt = jnp.transpose(b_bar, (0, 2, 1))
    bb = jnp.concatenate([_ssm_block_diag(jnp.real(bt)), _ssm_block_diag(jnp.imag(bt))], axis=2)
    cb = jnp.concatenate([jnp.transpose(_ssm_block_diag(c_re), (0, 2, 1)),
                          jnp.transpose(_ssm_block_diag(-c_im), (0, 2, 1))], axis=1)
    return lam_blk, bb, cb


def _ssm_powers(a_re, a_im, log_dt):
    lam = lax.complex(a_re, a_im)
    dt = jnp.exp(log_dt)[:, None]
    k = jnp.arange(1, SUBLANES + 1, dtype=F32)
    pw = jnp.exp((lam * dt)[None] * k[:, None, None]).reshape(SUBLANES, N_SBLK, LANES)
    pw = jnp.transpose(pw, (1, 0, 2))
    rev = pw[:, ::-1]
    return jnp.concatenate([jnp.real(pw), jnp.imag(pw), jnp.real(rev), jnp.imag(rev)], axis=1)


def _scan_consts(pw_ref, b, reverse):
    row = lax.broadcasted_iota(jnp.int32, (SUBLANES, LANES), 0)
    sign = -1.0 if reverse else 1.0

    def power(k):
        return (jnp.broadcast_to(pw_ref[b, k - 1:k, :], (SUBLANES, LANES)),
                sign * jnp.broadcast_to(pw_ref[b, SUBLANES + k - 1:SUBLANES + k, :], (SUBLANES, LANES)))

    steps = []
    for d in (1, 2, 4):
        pr, pi = power(d)
        keep = (row < SUBLANES - d) if reverse else (row >= d)
        steps.append((d, jnp.where(keep, pr, 0.0), jnp.where(keep, pi, 0.0)))
    base = 2 * SUBLANES if reverse else 0
    return steps, pw_ref[b, base:base + SUBLANES, :], sign * pw_ref[b, base + SUBLANES:base + 2 * SUBLANES, :]


def _scan_tile(xr, xi, steps, reverse):
    for d, ar, ai in steps:
        shift = SUBLANES - d if reverse else d
        rr, ri = pltpu.roll(xr, shift, 0), pltpu.roll(xi, shift, 0)
        xr, xi = xr + (ar * rr - ai * ri), xi + (ar * ri + ai * rr)
    return xr, xi


def _bcast_row(v, r):
    return jnp.broadcast_to(v[r:r + 1, :], (SUBLANES, LANES))


def _re_im(b):
    return slice(2 * b * LANES, (2 * b + 1) * LANES), slice((2 * b + 1) * LANES, (2 * b + 2) * LANES)


def _scan_forward(s_scr, pw_ref, n_tiles):
    consts = [_scan_consts(pw_ref, b, False) for b in range(SSM_BLOCKS)]

    def tiles(g, carry):
        carry = list(carry)
        rows = [pl.ds(pl.multiple_of((g * SCAN_UNROLL + u) * SUBLANES, SUBLANES), SUBLANES) for u in range(SCAN_UNROLL)]
        loaded = [[(s_scr[r, _re_im(b)[0]], s_scr[r, _re_im(b)[1]]) for b in range(SSM_BLOCKS)] for r in rows]
        local = [[_scan_tile(xr, xi, consts[b][0], False) for b, (xr, xi) in enumerate(per_row)] for per_row in loaded]
        for r, per_row in zip(rows, local):
            for b, (xr, xi) in enumerate(per_row):
                _, pr, pi = consts[b]
                cr, ci = carry[b]
                sr, si = xr + (pr * cr - pi * ci), xi + (pr * ci + pi * cr)
                s_scr[r, _re_im(b)[0]] = sr
                s_scr[r, _re_im(b)[1]] = si
                carry[b] = (_bcast_row(sr, SUBLANES - 1), _bcast_row(si, SUBLANES - 1))
        return tuple(carry)

    zero = jnp.zeros((SUBLANES, LANES), F32)
    lax.fori_loop(0, n_tiles // SCAN_UNROLL, tiles, ((zero, zero),) * SSM_BLOCKS)


def _row_chunks(l):
    rc = min(ROWS_SSM, l)
    return rc, l // rc


def _carry_sends(body, n_in, n_out, n_scratch, grid, sends):
    if sends is None:
        return body, [], [], [], [], []
    k = len(sends.srcs)

    def carrying(*refs):
        ins, send_ins = refs[:n_in], refs[n_in:n_in + k]
        outs, send_outs = refs[n_in + k:n_in + k + n_out], refs[n_in + k + n_out:n_in + 2 * k + n_out]
        rest = refs[n_in + 2 * k + n_out:]
        scratch, sems = rest[:n_scratch], rest[n_scratch:]
        ids = [pl.program_id(axis) for axis in range(len(grid))]
        first = functools.reduce(jnp.logical_and, [i == 0 for i in ids])
        last = functools.reduce(jnp.logical_and, [i == n - 1 for i, n in zip(ids, grid)])

        @pl.when(first)
        def _():
            sends.start(send_ins, send_outs, *sems)

        body(*ins, *outs, *scratch)

        @pl.when(last)
        def _():
            sends.finish(send_ins, send_outs, *sems)

    in_specs, out_specs, scratch = sends.specs()
    return carrying, in_specs, out_specs, sends.out_shapes, scratch, sends.srcs


def _ssm_fwd(proj, bb, cb, pw, sends=None):
    l = proj.shape[0]
    rc, n_chunks = _row_chunks(l)

    def body(u_ref, bb_ref, cb_ref, pw_ref, y_ref, s_scr):
        j = pl.program_id(0)

        def fill(ci, _):
            rows = pl.ds(pl.multiple_of(ci * rc, rc), rc)
            uv = u_ref[rows, :]
            for b in range(SSM_BLOCKS):
                s_scr[rows, 2 * b * LANES:2 * (b + 1) * LANES] = _dot(uv, bb_ref[b])
            return 0

        lax.fori_loop(0, n_chunks, fill, 0)
        _scan_forward(s_scr, pw_ref, l // SUBLANES)

        @pl.when(j % SSM_STEPS_PER_U == 0)
        def _():
            y_ref[...] = jnp.zeros_like(y_ref)

        def emit(ci, _):
            rows = pl.ds(pl.multiple_of(ci * rc, rc), rc)
            for b in range(SSM_BLOCKS):
                y_ref[rows, :] += _dot(s_scr[rows, 2 * b * LANES:2 * (b + 1) * LANES], cb_ref[b])
            return 0

        lax.fori_loop(0, n_chunks, emit, 0)

    steps = N_SBLK // SSM_BLOCKS
    body, x_in, x_out, x_shapes, x_scratch, x_args = _carry_sends(body, 4, 2, 0, (steps,), sends)
    outs = pl.pallas_call(
        body, name="ssm_fwd" if sends is None else "ssm_fwd_sending", grid=(steps,),
        in_specs=[pl.BlockSpec((l, LANES), lambda j: (0, OFF_US // LANES + j // SSM_STEPS_PER_U)),
                  pl.BlockSpec((SSM_BLOCKS, LANES, 2 * LANES), lambda j: (j, 0, 0)),
                  pl.BlockSpec((SSM_BLOCKS, 2 * LANES, LANES), lambda j: (j, 0, 0)),
                  pl.BlockSpec((SSM_BLOCKS, 4 * SUBLANES, LANES), lambda j: (j, 0, 0))] + x_in,
        out_specs=[pl.BlockSpec((l, LANES), lambda j: (0, j // SSM_STEPS_PER_U)),
                   pl.BlockSpec((l, SSM_BLOCKS * 2 * LANES), lambda j: (0, j))] + x_out,
        out_shape=[jax.ShapeDtypeStruct((l, SSM_W), F32), jax.ShapeDtypeStruct((l, N_SBLK * 2 * LANES), F32)] + x_shapes,
        scratch_shapes=x_scratch, compiler_params=_params(("arbitrary",)))(proj, bb, cb, pw, *x_args)
    return outs[0], outs[1], list(outs[2:])


def _ssm_bwd(proj, states, dy, bb, cb, pw, sends=None):
    l = proj.shape[0]
    rc, n_chunks = _row_chunks(l)
    n_tiles = l // SUBLANES

    def body(u_ref, s_scr, dy_ref, bb_ref, cb_ref, pw_ref, du_ref, dbb_ref, dcb_ref, dlam_ref, a_scr):
        j = pl.program_id(0)
        dcb_ref[...] = jnp.zeros_like(dcb_ref)

        def through_c(ci, _):
            rows = pl.ds(pl.multiple_of(ci * rc, rc), rc)
            dyv = dy_ref[rows, :]
            for b in range(SSM_BLOCKS):
                lanes = slice(2 * b * LANES, 2 * (b + 1) * LANES)
                dcb_ref[b] += _dot(s_scr[rows, lanes], dyv, _TN)
                a_scr[rows, lanes] = _dot(dyv, cb_ref[b], _NT)
            return 0

        lax.fori_loop(0, n_chunks, through_c, 0)

        consts = [_scan_consts(pw_ref, b, True) for b in range(SSM_BLOCKS)]
        row = lax.broadcasted_iota(jnp.int32, (SUBLANES, LANES), 0)
        last = row == SUBLANES - 1

        def tiles(g, carry):
            carry = list(carry)
            rows = [pl.ds(pl.multiple_of((n_tiles - 1 - (g * SCAN_UNROLL + u)) * SUBLANES, SUBLANES), SUBLANES)
                    for u in range(SCAN_UNROLL)]
            loaded = [[(a_scr[r, _re_im(b)[0]], a_scr[r, _re_im(b)[1]]) for b in range(SSM_BLOCKS)] for r in rows]
            states = [[(s_scr[r, _re_im(b)[0]], s_scr[r, _re_im(b)[1]]) for b in range(SSM_BLOCKS)] for r in rows]
            local = [[_scan_tile(xr, xi, consts[b][0], True) for b, (xr, xi) in enumerate(per_row)] for per_row in loaded]
            for r, per_row, state_row in zip(rows, local, states):
                for b, ((xr, xi), (sr, si)) in enumerate(zip(per_row, state_row)):
                    _, pr, pi = consts[b]
                    cr, ci, accr, acci = carry[b]
                    gr, gi = xr + (pr * cr - pi * ci), xi + (pr * ci + pi * cr)
                    a_scr[r, _re_im(b)[0]] = gr
                    a_scr[r, _re_im(b)[1]] = gi
                    ur = jnp.where(last, cr, pltpu.roll(gr, SUBLANES - 1, 0))
                    ui = jnp.where(last, ci, pltpu.roll(gi, SUBLANES - 1, 0))
                    carry[b] = (_bcast_row(gr, 0), _bcast_row(gi, 0),
                                accr + (ur * sr + ui * si), acci + (ui * sr - ur * si))
            return tuple(carry)

        zero = jnp.zeros((SUBLANES, LANES), F32)
        done = lax.fori_loop(0, n_tiles // SCAN_UNROLL, tiles, ((zero, zero, zero, zero),) * SSM_BLOCKS)
        for b in range(SSM_BLOCKS):
            dlr = jnp.broadcast_to(jnp.sum(done[b][2], axis=0, keepdims=True), (SUBLANES, LANES))
            dli = jnp.broadcast_to(jnp.sum(done[b][3], axis=0, keepdims=True), (SUBLANES, LANES))
            dlam_ref[b] = jnp.where(row == 0, dlr, jnp.where(row == 1, dli, 0.0))

        dbb_ref[...] = jnp.zeros_like(dbb_ref)

        @pl.when(j % SSM_STEPS_PER_U == 0)
        def _():
            du_ref[...] = jnp.zeros_like(du_ref)

        def through_b(ci, _):
            rows = pl.ds(pl.multiple_of(ci * rc, rc), rc)
            uv = u_ref[rows, :]
            for b in range(SSM_BLOCKS):
                av = a_scr[rows, 2 * b * LANES:2 * (b + 1) * LANES]
                dbb_ref[b] += _dot(uv, av, _TN)
                du_ref[rows, :] += _dot(av, bb_ref[b], _NT)
            return 0

        lax.fori_loop(0, n_chunks, through_b, 0)

    steps = N_SBLK // SSM_BLOCKS
    body, x_in, x_out, x_shapes, x_scratch, x_args = _carry_sends(body, 6, 4, 1, (steps,), sends)
    outs = pl.pallas_call(
        body, name="ssm_bwd" if sends is None else "ssm_bwd_sending", grid=(steps,),
        in_specs=[pl.BlockSpec((l, LANES), lambda j: (0, OFF_US // LANES + j // SSM_STEPS_PER_U)),
                  pl.BlockSpec((l, SSM_BLOCKS * 2 * LANES), lambda j: (0, j)),
                  pl.BlockSpec((l, LANES), lambda j: (0, j // SSM_STEPS_PER_U)),
                  pl.BlockSpec((SSM_BLOCKS, LANES, 2 * LANES), lambda j: (j, 0, 0)),
                  pl.BlockSpec((SSM_BLOCKS, 2 * LANES, LANES), lambda j: (j, 0, 0)),
                  pl.BlockSpec((SSM_BLOCKS, 4 * SUBLANES, LANES), lambda j: (j, 0, 0))] + x_in,
        out_specs=[pl.BlockSpec((l, LANES), lambda j: (0, j // SSM_STEPS_PER_U)),
                   pl.BlockSpec((SSM_BLOCKS, LANES, 2 * LANES), lambda j: (j, 0, 0)),
                   pl.BlockSpec((SSM_BLOCKS, 2 * LANES, LANES), lambda j: (j, 0, 0)),
                   pl.BlockSpec((SSM_BLOCKS, SUBLANES, LANES), lambda j: (j, 0, 0))] + x_out,
        out_shape=[jax.ShapeDtypeStruct((l, SSM_W), F32), jax.ShapeDtypeStruct((N_SBLK, LANES, 2 * LANES), F32),
                   jax.ShapeDtypeStruct((N_SBLK, 2 * LANES, LANES), F32),
                   jax.ShapeDtypeStruct((N_SBLK, SUBLANES, LANES), F32)] + x_shapes,
        scratch_shapes=[pltpu.VMEM((l, SSM_BLOCKS * 2 * LANES), F32)] + x_scratch,
        compiler_params=_params(("arbitrary",)))(proj, states, dy, bb, cb, pw, *x_args)
    return outs[0], outs[1], outs[2], outs[3], list(outs[4:])


def _pool_windows(v, t, l, ahead):
    def shifted(a, d):
        if ahead:
            return jnp.where(t < l - d, pltpu.roll(a, l - d, 0), 0.0)
        return jnp.where(t >= d, pltpu.roll(a, d, 0), 0.0)

    w2 = v + shifted(v, 1)
    w4 = w2 + shifted(w2, 2)
    w8 = w4 + shifted(w4, 4)
    w16 = w8 + shifted(w8, 8)
    return w2, w4, w8, w16


def _pool_select(g, ws):
    return jnp.where(g == 0, ws[0], jnp.where(g == 1, ws[1], jnp.where(g == 2, ws[2], ws[3])))


def _pool_count(g, t):
    return jnp.minimum(t + 1, jnp.left_shift(2, g)).astype(F32)


def _pool_specs(l):
    return [pl.BlockSpec((l, POOL_GW), lambda g: (0, OFF_UP // POOL_GW + g)),
            pl.BlockSpec((1, POOL_GW, POOL_GW), lambda g: (g, 0, 0)),
            pl.BlockSpec((1, POOL_GW), lambda g: (0, g))]


def _pool_fwd(proj, w_pool, scale):
    l = proj.shape[0]

    def body(u_ref, w_ref, sc_ref, y_ref):
        g = pl.program_id(0)
        t = lax.broadcasted_iota(jnp.int32, (l, 1), 0)
        u = u_ref[...]
        pooled = _pool_select(g, _pool_windows(u, t, l, False)) / _pool_count(g, t) - u
        y_ref[...] = _dot(pooled, w_ref[0]) * sc_ref[...]

    return pl.pallas_call(
        body, name="pool_fwd", grid=(4,), in_specs=_pool_specs(l),
        out_specs=pl.BlockSpec((l, POOL_GW), lambda g: (0, g)), out_shape=jax.ShapeDtypeStruct((l, POOL_W), F32),
        compiler_params=_params(("parallel",)))(proj, w_pool, scale)


def _pool_bwd(proj, w_pool, scale, dy):
    l = proj.shape[0]

    def body(u_ref, w_ref, sc_ref, dy_ref, du_ref, dw_ref, dsc_ref):
        g = pl.program_id(0)
        t = lax.broadcasted_iota(jnp.int32, (l, 1), 0)
        u = u_ref[...]
        cnt = _pool_count(g, t)
        pooled = _pool_select(g, _pool_windows(u, t, l, False)) / cnt - u
        dyv = dy_ref[...]
        dsc_ref[...] = jnp.sum(dyv * _dot(pooled, w_ref[0]), axis=0, keepdims=True)
        dyl = dyv * sc_ref[...]
        dw_ref[0] = _dot(pooled, dyl, _TN)
        dpooled = _dot(dyl, w_ref[0], _NT)
        du_ref[...] = (_pool_select(g, _pool_windows(dpooled / cnt, t, l, True)) - dpooled).astype(BF16)

    return pl.pallas_call(
        body, name="pool_bwd", grid=(4,), in_specs=_pool_specs(l) + [pl.BlockSpec((l, POOL_GW), lambda g: (0, g))],
        out_specs=[pl.BlockSpec((l, POOL_GW), lambda g: (0, g)), pl.BlockSpec((1, POOL_GW, POOL_GW), lambda g: (g, 0, 0)),
                   pl.BlockSpec((1, POOL_GW), lambda g: (0, g))],
        out_shape=[jax.ShapeDtypeStruct((l, POOL_W), BF16), jax.ShapeDtypeStruct((4, POOL_GW, POOL_GW), F32),
                   jax.ShapeDtypeStruct((1, POOL_W), F32)],
        compiler_params=_params(("parallel",)))(proj, w_pool, scale, dy)


ROWS_MERGE = 128
ROWS_MERGE_FWD = 256


def _merge_inputs(tr):
    def col(off, w):
        return pl.BlockSpec((tr, w), lambda i: (i, off // w))

    def act(w):
        return pl.BlockSpec((tr, w), lambda i: (i, 0))

    d = D_MODEL
    return ([col(OFF_GL, 3 * d), col(OFF_ZA, ATT_W), col(OFF_US, SSM_W), col(OFF_ZS, SSM_W), col(OFF_ZP, POOL_W),
             act(ATT_W), act(SSM_W), act(POOL_W)]
            + [_full((1, SSM_W)), _full((SSM_W, SSM_W)), _full((1, SSM_W)), _full((ATT_W, d)), _full((SSM_W, d)),
               _full((POOL_W, d)), _full((d, d)), _full((1, 3 * d))])


def _merge_forward_math(gl_ref, za_ref, us_ref, zs_ref, zp_ref, o_ref, yl_ref, yp_ref, d_ref, wg_ref, bg_ref,
                        wba_ref, wbs_ref, wbp_ref):
    d = D_MODEL
    r = {}
    r['sa'], r['dsa'] = _silu_and_grad(za_ref[...])
    r['ss'], r['dss'] = _silu_and_grad(zs_ref[...])
    r['sp'], r['dsp'] = _silu_and_grad(zp_ref[...])
    r['y_att'] = o_ref[...] * r['sa']
    y1 = yl_ref[...] + d_ref[...] * us_ref[...]
    r['y2'], r['dgelu'] = _gelu_and_grad(y1)
    r['sg'] = _sigmoid(_dot(r['y2'], wg_ref[...]) + bg_ref[...])
    r['y3'] = r['y2'] * r['sg']
    r['y_ssm'] = r['y3'] * r['ss']
    r['y_pool'] = yp_ref[...] * r['sp']
    r['g'] = [_sigmoid(gl_ref[:, b * d:(b + 1) * d]) for b in range(3)]
    r['p'] = [_dot(r['y_att'], wba_ref[...]), _dot(r['y_ssm'], wbs_ref[...]), _dot(r['y_pool'], wbp_ref[...])]
    r['merged'] = r['g'][0] * r['p'][0] + r['g'][1] * r['p'][1] + r['g'][2] * r['p'][2]
    return r


def _merge_fwd(proj, o, y_lin, y_pool, x, ssm_d, w_glu, b_glu, wba, wbs, wbp, w_out, mod, sends=None):
    l, d = x.shape
    tr = min(ROWS_MERGE_FWD, l)

    def body(gl_ref, za_ref, us_ref, zs_ref, zp_ref, o_ref, yl_ref, yp_ref, d_ref, wg_ref, bg_ref, wba_ref, wbs_ref,
             wbp_ref, wo_ref, mod_ref, x_ref, xn_ref):
        r = _merge_forward_math(gl_ref, za_ref, us_ref, zs_ref, zp_ref, o_ref, yl_ref, yp_ref, d_ref, wg_ref, bg_ref,
                                wba_ref, wbs_ref, wbp_ref)
        xn_ref[...] = x_ref[...] + mod_ref[:, 2 * d:3 * d] * _dot(r['merged'], wo_ref[...])

    blk = pl.BlockSpec((tr, d), lambda i: (i, 0))
    in_specs = _merge_inputs(tr) + [blk]
    body, x_in, x_out, x_shapes, x_scratch, x_args = _carry_sends(body, len(in_specs), 1, 0, (l // tr,), sends)
    outs = pl.pallas_call(
        body, name="merge_fwd" if sends is None else "merge_fwd_sending", grid=(l // tr,), in_specs=in_specs + x_in,
        out_specs=[blk] + x_out, out_shape=[jax.ShapeDtypeStruct((l, d), F32)] + x_shapes, scratch_shapes=x_scratch,
        compiler_params=_params(("arbitrary",)))(
            proj, proj, proj, proj, proj, o, y_lin, y_pool, ssm_d, w_glu, b_glu, wba, wbs, wbp, w_out, mod, x, *x_args)
    return outs[0], list(outs[1:])


def _merge_bwd(proj, o, y_lin, y_pool, dxo, ssm_d, w_glu, b_glu, wba, wbs, wbp, w_out, mod, sends=None):
    l, d = dxo.shape
    tr = min(ROWS_MERGE, l)

    def body(gl_ref, za_ref, us_ref, zs_ref, zp_ref, o_ref, yl_ref, yp_ref, d_ref, wg_ref, bg_ref, wba_ref, wbs_ref,
             wbp_ref, wo_ref, mod_ref, dxo_ref,
             do_ref, dyl_ref, dus_ref, dyp_ref, dpj_ref,
             ya_ref, ys_ref, ypl_ref, y2_ref, dt_ref, dpa_ref, dps_ref, dpp_ref, mg_ref, dout_ref, st_ref):
        @pl.when(pl.program_id(0) == 0)
        def _():
            st_ref[...] = jnp.zeros_like(st_ref)

        for off, width in ((OFF_Q, ATT_W), (OFF_US, SSM_W), (OFF_UP, POOL_W), (OFF_KV, 2 * KV_W)):
            dpj_ref[:, off:off + width] = jnp.zeros((tr, width), BF16)

        r = _merge_forward_math(gl_ref, za_ref, us_ref, zs_ref, zp_ref, o_ref, yl_ref, yp_ref, d_ref, wg_ref, bg_ref,
                                wba_ref, wbs_ref, wbp_ref)
        dxov = dxo_ref[...]
        out = _dot(r['merged'], wo_ref[...])
        st_ref[0:1, :] += jnp.sum(dxov * out, axis=0, keepdims=True)
        dout = dxov * mod_ref[:, 2 * d:3 * d]
        dmerged = _dot(dout, wo_ref[...], _NT)
        dys = []
        for b, (w_ref, dp_ref) in enumerate(((wba_ref, dpa_ref), (wbs_ref, dps_ref), (wbp_ref, dpp_ref))):
            gb = r['g'][b]
            dpj_ref[:, OFF_GL + b * d:OFF_GL + (b + 1) * d] = (dmerged * r['p'][b] * gb * (1.0 - gb)).astype(BF16)
            dp = dmerged * gb
            dp_ref[...] = dp.astype(BF16)
            dys.append(_dot(dp, w_ref[...], _NT))
        mg_ref[...] = r['merged'].astype(BF16)
        dout_ref[...] = dout.astype(BF16)
        ya_ref[...] = r['y_att'].astype(BF16)
        ys_ref[...] = r['y_ssm'].astype(BF16)
        ypl_ref[...] = r['y_pool'].astype(BF16)
        y2_ref[...] = r['y2'].astype(BF16)
        do_ref[...] = (dys[0] * r['sa']).astype(BF16)
        dpj_ref[:, OFF_ZA:OFF_ZA + ATT_W] = (dys[0] * o_ref[...] * r['dsa']).astype(BF16)
        dy3 = dys[1] * r['ss']
        dpj_ref[:, OFF_ZS:OFF_ZS + SSM_W] = (dys[1] * r['y3'] * r['dss']).astype(BF16)
        dt = dy3 * r['y2'] * r['sg'] * (1.0 - r['sg'])
        dt_ref[...] = dt.astype(BF16)
        dy1 = (dy3 * r['sg'] + _dot(dt, wg_ref[...], _NT)) * r['dgelu']
        dyl_ref[...] = dy1.astype(BF16)
        dus_ref[...] = dy1 * d_ref[...]
        st_ref[1:2, 0:SSM_W] += jnp.sum(dt, axis=0, keepdims=True)
        st_ref[1:2, SSM_W:2 * SSM_W] += jnp.sum(dy1 * us_ref[...], axis=0, keepdims=True)
        dyp_ref[...] = dys[2] * r['sp']
        dpj_ref[:, OFF_ZP:OFF_ZP + POOL_W] = (dys[2] * yp_ref[...] * r['dsp']).astype(BF16)

    blk = pl.BlockSpec((tr, d), lambda i: (i, 0))
    half = pl.BlockSpec((tr, ATT_W), lambda i: (i, 0))
    wide = pl.BlockSpec((tr, IN_W), lambda i: (i, 0))
    sds = jax.ShapeDtypeStruct
    in_specs = _merge_inputs(tr) + [blk]
    out_specs = [half] * 4 + [wide] + [half] * 5 + [blk] * 5 + [_full((SUBLANES, d))]
    body, x_in, x_out, x_shapes, x_scratch, x_args = _carry_sends(body, len(in_specs), len(out_specs), 0, (l // tr,), sends)
    outs = pl.pallas_call(
        body, name="merge_bwd" if sends is None else "merge_bwd_sending", grid=(l // tr,), in_specs=in_specs + x_in,
        out_specs=out_specs + x_out,
        out_shape=[sds((l, ATT_W), t) for t in (BF16, BF16, F32, F32)] + [sds((l, IN_W), BF16)]
        + [sds((l, ATT_W), BF16)] * 5 + [sds((l, d), BF16)] * 5 + [sds((SUBLANES, d), F32)] + x_shapes,
        scratch_shapes=x_scratch, compiler_params=_params(("arbitrary",)))(
            proj, proj, proj, proj, proj, o, y_lin, y_pool, ssm_d, w_glu, b_glu, wba, wbs, wbp, w_out, mod, dxo, *x_args)
    return list(outs[:len(out_specs)]), list(outs[len(out_specs):])


def _ada_fwd(c_all, w_ada, b_shard):
    depth, d, n = w_ada.shape

    def body(c_ref, w_ref, b_ref, o_ref):
        act, _ = _silu_and_grad(c_ref[...])
        o_ref[0] = _dot(act, w_ref[0]) + b_ref[0]

    return pl.pallas_call(
        body, name="ada_fwd", grid=(depth,),
        in_specs=[_full((N_DEV, d)), pl.BlockSpec((1, d, n), lambda i: (i, 0, 0)), pl.BlockSpec((1, 1, n), lambda i: (i, 0, 0))],
        out_specs=pl.BlockSpec((1, N_DEV, n), lambda i: (i, 0, 0)), out_shape=jax.ShapeDtypeStruct((depth, N_DEV, n), F32),
        compiler_params=_params(("parallel",)))(c_all, w_ada, b_shard)


def _ada_bwd(c_all_t, dmod):
    d = c_all_t.shape[0]
    depth, _, n = dmod.shape

    def body(c_ref, dm_ref, o_ref):
        act, _ = _silu_and_grad(c_ref[...])
        acc = act[:, 0:1] * dm_ref[0, 0:1, :]
        for b in range(1, N_DEV):
            acc = acc + act[:, b:b + 1] * dm_ref[0, b:b + 1, :]
        o_ref[0] = acc

    return pl.pallas_call(
        body, name="ada_bwd", grid=(depth,),
        in_specs=[_full((d, N_DEV)), pl.BlockSpec((1, N_DEV, n), lambda i: (i, 0, 0))],
        out_specs=pl.BlockSpec((1, d, n), lambda i: (i, 0, 0)), out_shape=jax.ShapeDtypeStruct((depth, d, n), F32),
        compiler_params=_params(("parallel",)))(c_all_t, dmod)


ROWS_FLAT = 256


_RELATION_XOR = (0, 2, 1, 3)


def _sum_chips(own, others, chip_order, name):
    r, c = own.shape
    tr = math.gcd(ROWS_FLAT, r)

    def body(own_ref, oth_ref, o_ref):
        terms = [own_ref[...].astype(F32)] + [oth_ref[k].astype(F32) for k in range(3)]
        if chip_order:
            chip = 2 * lax.axis_index("x") + lax.axis_index("y")
            by_chip = []
            for q in range(N_CHIPS):
                rel = jnp.bitwise_xor(chip, q)
                pick = terms[3]
                for k in (2, 1, 0):
                    pick = jnp.where(rel == _RELATION_XOR[k], terms[k], pick)
                by_chip.append(pick)
            terms = by_chip
        o_ref[...] = ((terms[0] + terms[1]) + terms[2]) + terms[3]

    return pl.pallas_call(
        body, name=name, grid=(r // tr,),
        in_specs=[pl.BlockSpec((tr, c), lambda i: (i, 0)), pl.BlockSpec((3, tr, c), lambda i: (0, i, 0))],
        out_specs=pl.BlockSpec((tr, c), lambda i: (i, 0)), out_shape=jax.ShapeDtypeStruct((r, c), F32),
        compiler_params=_params(("parallel",)))(own, others)


def _add2(a, b, out_dtype, name):
    shape = a.shape
    a, b = a.reshape(-1, shape[-1]), b.reshape(-1, shape[-1])
    r, c = a.shape
    tr = math.gcd(ROWS_FLAT, r)

    def body(a_ref, b_ref, o_ref):
        o_ref[...] = (a_ref[...] + b_ref[...]).astype(out_dtype)

    blk = pl.BlockSpec((tr, c), lambda i: (i, 0))
    return pl.pallas_call(
        body, name=name, grid=(r // tr,), in_specs=[blk, blk], out_specs=blk,
        out_shape=jax.ShapeDtypeStruct((r, c), out_dtype), compiler_params=_params(("parallel",)))(a, b).reshape(shape)


def _adamw(w, g, m, v, name):
    r, c = w.shape
    tr = math.gcd(ROWS_FLAT, r)

    def body(w_ref, g_ref, m_ref, v_ref, d_ref, nm_ref, nv_ref):
        gv = g_ref[...]
        mv = ADAM_B1 * m_ref[...] + (1.0 - ADAM_B1) * gv
        vv = ADAM_B2 * v_ref[...] + (1.0 - ADAM_B2) * (gv * gv)
        m_hat = mv / (1.0 - ADAM_B1 ** ADAM_STEP)
        v_hat = vv / (1.0 - ADAM_B2 ** ADAM_STEP)
        d_ref[...] = -ADAM_LR * (m_hat / (jnp.sqrt(v_hat) + ADAM_EPS) + ADAM_WD * w_ref[...])
        nm_ref[...] = mv
        nv_ref[...] = vv

    blk = pl.BlockSpec((tr, c), lambda i: (i, 0))
    return pl.pallas_call(
        body, name=name, grid=(r // tr,), in_specs=[blk] * 4, out_specs=[blk] * 3,
        out_shape=[jax.ShapeDtypeStruct((r, c), F32)] * 3, compiler_params=_params(("parallel",)))(w, g, m, v)


_GROUP_MASKS = {
    'xy': ((1, 0, 0), (0, 1, 0), (1, 1, 0)),
    'c': ((0, 0, 1),),
    'xyc': ((0, 0, 1), (0, 1, 0), (0, 1, 1), (1, 0, 0), (1, 0, 1), (1, 1, 0), (1, 1, 1)),
}


def _group_rank(group, pos):
    x, y, c = pos
    return {'xy': 2 * x + y, 'c': c, 'xyc': 4 * x + 2 * y + c}[group]


def _exchange(src, group, scatter, name):
    masks = _GROUP_MASKS[group]
    n = len(masks) + 1
    shape = src.shape[1:] if scatter else src.shape
    assert (not scatter) or src.shape[0] == n

    def body(src_ref, out_ref, send_sems, recv_sems, local_sem):
        me = (lax.axis_index("x"), lax.axis_index("y"), lax.axis_index("c"))
        my_rank = _group_rank(group, me)

        def mine_for(rank):
            return src_ref.at[rank] if scatter else src_ref

        local = pltpu.make_async_copy(mine_for(my_rank), out_ref.at[my_rank], local_sem)
        local.start()
        sends = []
        for k, mask in enumerate(masks):
            peer = tuple(1 - p if f else p for p, f in zip(me, mask))
            peer_rank = _group_rank(group, peer)
            send = pltpu.make_async_remote_copy(
                src_ref=mine_for(peer_rank), dst_ref=out_ref.at[my_rank], send_sem=send_sems.at[k],
                recv_sem=recv_sems.at[k], device_id=peer, device_id_type=pl.DeviceIdType.MESH)
            send.start()
            sends.append((send, peer, peer_rank))
        for k, (send, peer, peer_rank) in enumerate(sends):
            pltpu.make_async_remote_copy(
                src_ref=mine_for(peer_rank), dst_ref=out_ref.at[peer_rank], send_sem=send_sems.at[k],
                recv_sem=recv_sems.at[k], device_id=peer, device_id_type=pl.DeviceIdType.MESH).wait_recv()
        for send, _, _ in sends:
            send.wait_send()
        local.wait()

    return pl.pallas_call(
        body, name=name, in_specs=[pl.BlockSpec(memory_space=pl.ANY)], out_specs=pl.BlockSpec(memory_space=pl.ANY),
        out_shape=jax.ShapeDtypeStruct((n,) + tuple(shape), src.dtype),
        scratch_shapes=[pltpu.SemaphoreType.DMA((n - 1,)), pltpu.SemaphoreType.DMA((n - 1,)), pltpu.SemaphoreType.DMA(())],
    )(src)


CHUNK_BYTES = 1 << 20
MIN_CHUNK_ROWS = 64


def _row_parts(rows, cols, itemsize):
    n = 1
    while rows % (2 * n) == 0 and rows // (2 * n) >= MIN_CHUNK_ROWS and rows * cols * itemsize > n * CHUNK_BYTES:
        n *= 2
    return n


def _remote(src, dst, send_sem, recv_sem, peer):
    return pltpu.make_async_remote_copy(src_ref=src, dst_ref=dst, send_sem=send_sem, recv_sem=recv_sem,
                                        device_id=peer, device_id_type=pl.DeviceIdType.MESH)


def _start_rows(src, dst, send_sem, recv_sem, peer):
    rows, cols = src.shape
    n = _row_parts(rows, cols, jnp.dtype(src.dtype).itemsize)
    pr = rows // n
    for i in range(n):
        _remote(src.at[pl.ds(i * pr, pr), :], dst.at[pl.ds(i * pr, pr), :], send_sem, recv_sem, peer).start()


def _mesh_place():
    x, y, c = lax.axis_index("x"), lax.axis_index("y"), lax.axis_index("c")
    other_chips = ((1 - x, y), (x, 1 - y), (1 - x, 1 - y))
    return x, y, c, 2 * x + y, (x, y, 1 - c), other_chips


def _comm_call(body, name, ins, out_shapes, sem_counts):
    any_spec = pl.BlockSpec(memory_space=pl.ANY)
    return pl.pallas_call(
        body, name=name, in_specs=[any_spec] * len(ins), out_specs=[any_spec] * len(out_shapes), out_shape=out_shapes,
        scratch_shapes=[pltpu.SemaphoreType.DMA((n,)) for n in sem_counts])(*ins)


class _Sends:
    def run(self, name):
        nt = len(self.srcs)

        def body(*refs):
            ins, outs = refs[:nt], refs[nt:2 * nt]
            self.start(ins, outs, *refs[2 * nt:])
            self.finish(ins, outs, *refs[2 * nt:])

        return _comm_call(body, name, self.srcs, self.out_shapes, (self.n_sems, self.n_sems))

    def specs(self):
        any_spec = pl.BlockSpec(memory_space=pl.ANY)
        return [any_spec] * len(self.srcs), [any_spec] * len(self.srcs), [pltpu.SemaphoreType.DMA((self.n_sems,))] * 2


class _SiblingSends(_Sends):
    def __init__(self, srcs):
        self.srcs = list(srcs)
        self.n_sems = len(self.srcs)
        self.out_shapes = [jax.ShapeDtypeStruct(g.shape[1:], g.dtype) for g in self.srcs]

    def start(self, ins, outs, send_sems, recv_sems):
        _, _, c, _, sibling, _ = _mesh_place()
        for t in range(len(ins)):
            for q in range(ins[t].shape[1]):
                _start_rows(ins[t].at[1 - c, q], outs[t].at[q], send_sems.at[t], recv_sems.at[t], sibling)

    def finish(self, ins, outs, send_sems, recv_sems):
        _, _, _, _, sibling, _ = _mesh_place()
        for t in range(len(ins)):
            arrived = _remote(outs[t], outs[t], send_sems.at[t], recv_sems.at[t], sibling)
            arrived.wait_recv()
            arrived.wait_send()


class _ChipSends(_Sends):
    def __init__(self, srcs, pick):
        self.srcs, self.pick = list(srcs), pick
        nt = len(self.srcs)
        self.n_sems = 3 * nt
        self.out_shapes = []
        for s in self.srcs:
            r2 = s.shape[-2] // 2 if pick is _pick_weight_half else s.shape[-2]
            self.out_shapes.append(jax.ShapeDtypeStruct((3, r2, s.shape[-1]), s.dtype))

    def start(self, ins, outs, send_sems, recv_sems):
        _, _, c, _, _, other_chips = _mesh_place()
        for t in range(len(ins)):
            for j, (px, py) in enumerate(other_chips):
                _remote(self.pick(ins[t], c, 2 * px + py), outs[t].at[j], send_sems.at[3 * t + j], recv_sems.at[3 * t + j],
                        (px, py, c)).start()

    def finish(self, ins, outs, send_sems, recv_sems):
        _, _, c, _, _, other_chips = _mesh_place()
        for t in range(len(ins)):
            for j, (px, py) in enumerate(other_chips):
                landed = _remote(outs[t].at[j], outs[t].at[j], send_sems.at[3 * t + j], recv_sems.at[3 * t + j], (px, py, c))
                landed.wait_recv()
                landed.wait_send()


def _pick_weight_half(src, core, chip):
    r2 = src.shape[0] // 2
    return src.at[pl.ds(pl.multiple_of(core * r2, MIN_CHUNK_ROWS), r2), :]


def _pick_slab(src, core, chip):
    return src.at[chip if src.shape[0] == N_CHIPS else 0]


def _swap_cores(halves):
    nt = len(halves)

    def body(*refs):
        ins, outs = refs[:nt], refs[nt:2 * nt]
        send_sems, recv_sems = refs[2 * nt:]
        _, _, _, _, sibling, _ = _mesh_place()
        for t in range(nt):
            for li in range(ins[t].shape[0]):
                _start_rows(ins[t].at[li], outs[t].at[li], send_sems.at[t], recv_sems.at[t], sibling)
        for t in range(nt):
            _remote(outs[t], outs[t], send_sems.at[t], recv_sems.at[t], sibling).wait_recv()
        for t in range(nt):
            _remote(ins[t], ins[t], send_sems.at[t], recv_sems.at[t], sibling).wait_send()

    out_shapes = [jax.ShapeDtypeStruct(h.shape, h.dtype) for h in halves]
    return _comm_call(body, "swap_cores", halves, out_shapes, (nt, nt))


def _pack_rows(pieces, cols, dtype):
    tile = SUBLANES * cols
    rows = []
    for p in pieces:
        flat = p.reshape(-1).astype(dtype)
        rows.append(jnp.pad(flat, (0, (-flat.shape[0]) % tile)).reshape(-1, cols))
    total = sum(r.shape[0] for r in rows)
    if total % (2 * SUBLANES):
        rows.append(jnp.zeros((SUBLANES, cols), dtype))
    return jnp.concatenate(rows, axis=0)


def _unpack_rows(buf, shapes):
    out, row = [], 0
    for s in shapes:
        n = math.prod(s)
        nrows = -(-n // (SUBLANES * buf.shape[1])) * SUBLANES
        out.append(buf[row:row + nrows].reshape(-1)[:n].reshape(s))
        row += nrows
    return out


ROW_SHARDED = ('w_out', 'w_glu')


def _chips_to_full(name, chips):
    mats = [c.reshape(-1, c.shape[-1]) for c in chips]
    if name in ROW_SHARDED:
        return jnp.concatenate(mats, axis=0)
    if name != 'w_in':
        return jnp.concatenate(mats, axis=1)
    cs = mats[0].shape[1]
    pieces = []
    for start, size in _PERM_PIECES:
        lo = start
        while lo < start + size:
            q = lo // cs
            hi = min(start + size, (q + 1) * cs)
            pieces.append(mats[q][:, lo - q * cs:hi - q * cs])
            lo = hi
    return jnp.concatenate(pieces, axis=1)


def _full_to_dests(name, full):
    rows, cols = full.shape
    if name in ROW_SHARDED:
        return jnp.transpose(full.reshape(N_CHIPS, 2, rows // (2 * N_CHIPS), cols), (1, 0, 2, 3))
    cs = cols // N_CHIPS
    if name != 'w_in':
        dests = [full[:, q * cs:(q + 1) * cs] for q in range(N_CHIPS)]
    else:
        offsets, off = [], 0
        for _, size in _PERM_PIECES:
            offsets.append(off)
            off += size
        by_start = sorted(zip(_PERM_PIECES, offsets))
        dests = []
        for q in range(N_CHIPS):
            parts = []
            for (start, size), at in by_start:
                lo, hi = max(start, q * cs), min(start + size, (q + 1) * cs)
                if lo < hi:
                    parts.append(full[:, at + lo - start:at + hi - start])
            dests.append(jnp.concatenate(parts, axis=1))
    return jnp.stack([d.reshape(2, rows // 2, cs) for d in dests], axis=1)


def _by_chip(own, related, x, y):
    grid = ((own, related[1]), (related[0], related[2]))
    along_x = [[jnp.where(x == 0, grid[px][dy], grid[1 - px][dy]) for dy in range(2)] for px in range(2)]
    return [jnp.where(y == 0, along_x[px][py], along_x[px][1 - py]) for px in range(2) for py in range(2)]


def _join_halves(mine, theirs, core, axis):
    return jnp.where(core == 0, jnp.concatenate([mine, theirs], axis=axis), jnp.concatenate([theirs, mine], axis=axis))


ROWS_WGRAD = 2048
SSM_RAW = ('ssm_a_re', 'ssm_a_im', 'ssm_log_dt', 'ssm_b_re', 'ssm_b_im', 'ssm_c_re', 'ssm_c_im')


def _all_tables(p):
    raw = tuple(p[k] for k in SSM_RAW)
    (_, bb, cb), vjp = jax.vjp(jax.vmap(_ssm_tables), *raw)
    return (bb.astype(BF16), cb.astype(BF16), jax.vmap(_ssm_powers)(*raw[:3])), vjp


def _layer_fwd(x, p, w_in, rest, li, mod, tables, sends=None, late_sends=None):
    bb, cb, pw = (t[li] for t in tables)
    h = _norm_fwd(x, _row(p['norm_g'][li]), mod)
    if isinstance(rest, dict):
        proj = _matmul(h, w_in, 'nn', tm=512, tn=1280, tk=1024, name="proj", n_outer=True)
    else:
        proj, arrived = _matmul(h, w_in, 'nn', tm=512, tn=1280, tk=1024, name="proj_sending", n_outer=True, sends=rest[0])
        rest = rest[1](arrived)
    full = dict(rest, w_in=w_in)
    o = _attention_fwd(proj, p['attn_sinks'][li])
    y_lin, states, brought = _ssm_fwd(proj, bb, cb, pw, sends)
    y_pool = _pool_fwd(proj, p['w_pool'][li].astype(BF16), _row(p['pool_scale'][li]))
    x_new, brought_late = _merge_fwd(proj, o, y_lin, y_pool, x, _row(p['ssm_d'][li]), full['w_glu'],
                                     _row(p['b_glu'][li]), full['w_br_att'], full['w_br_ssm'], full['w_br_pool'],
                                     full['w_out'], mod, late_sends)
    saved = dict(x=x, h=h, proj=proj, o=o, y_lin=y_lin, y_pool=y_pool, states=states, bb=bb, cb=cb, pw=pw, full=full)
    return x_new, saved, brought + brought_late


def _layer_bwd(dxo, s, p, li, mod, core, upper=None, own_early=(), w_in_sender=None):
    l = dxo.shape[0]
    proj, full = s['proj'], s['full']
    outs, theirs = _merge_bwd(
        proj, s['o'], s['y_lin'], s['y_pool'], dxo, _row(p['ssm_d'][li]), full['w_glu'], _row(p['b_glu'][li]),
        full['w_br_att'], full['w_br_ssm'], full['w_br_pool'], full['w_out'], mod,
        None if upper is None else _SiblingSends(upper))
    (do, dyl, dus_skip, dyp, dproj, ya, ys, ypl, y2, dt, dpa, dps, dpp, mg, dout, st) = outs
    pair = [] if upper is None else _pair_sums(upper, [BF16] * len(upper), core, theirs)
    g = {}
    g['w_br_att'] = _matmul(ya, dpa, 'tn', tm=512, tn=1024, tk=ROWS_WGRAD, name="grad_w_br")
    g['w_br_ssm'] = _matmul(ys, dps, 'tn', tm=512, tn=1024, tk=ROWS_WGRAD, name="grad_w_br")
    g['w_br_pool'] = _matmul(ypl, dpp, 'tn', tm=512, tn=1024, tk=ROWS_WGRAD, name="grad_w_br")
    g['w_out'] = _matmul(mg, dout, 'tn', tm=512, tn=1024, tk=ROWS_WGRAD, name="grad_w_out")
    g['w_glu'] = _matmul(y2, dt, 'tn', tm=512, tn=512, tk=ROWS_WGRAD, name="grad_w_glu")
    g['b_glu'] = st[1, 0:SSM_W]
    g['ssm_d'] = st[1, SSM_W:2 * SSM_W]
    dgate = st[0]

    early = _layer_grad_halves(g, own_early)
    dq, dkv, dsink, theirs = _attention_bwd(proj, p['attn_sinks'][li], do, _SiblingSends(early) if early else None)
    g['attn_sinks'] = dsink[:, 0]
    if early:
        pair = pair + _pair_sums(early, [BF16] * len(early), core, theirs)
    sends = _ChipSends(pair, _pick_slab) if pair else None
    dus_scan, dbb, dcb, dlam, brought = _ssm_bwd(proj, s['states'], dyl, s['bb'], s['cb'], s['pw'], sends)
    g['ssm_tables'] = (dlam[:, 0:2, :], dbb, dcb)
    dup, dwp, dps_scale = _pool_bwd(proj, p['w_pool'][li].astype(BF16), _row(p['pool_scale'][li]), dyp)
    g['w_pool'] = dwp
    g['pool_scale'] = dps_scale[0]

    for off, piece in ((OFF_Q, dq), (OFF_US, dus_skip + dus_scan), (OFF_UP, dup), (OFF_KV, dkv[HALO:])):
        dproj = lax.dynamic_update_slice(dproj, piece.astype(BF16), (0, off))
    g['w_in'] = _matmul(s['h'], dproj, 'tn', tm=1024, tn=1280, tk=ROWS_WGRAD, name="grad_w_in")
    if w_in_sender is None:
        dh, brought_late = _matmul(dproj, full['w_in'], 'nt', tm=1024, tn=1024, tk=1280, name="grad_h"), None
    else:
        dh, brought_late = _matmul(dproj, full['w_in'], 'nt', tm=1024, tn=1024, tk=1280, name="grad_h_sending",
                                   sends=w_in_sender(g['w_in']))
    dx, nst = _norm_bwd(s['x'], dh, dxo, _row(p['norm_g'][li]), mod)
    g['norm_g'] = nst[2]
    dmod = jnp.concatenate([nst[0], nst[1], dgate])
    del l
    return dx, g, dmod, (pair, brought), brought_late


BIG_NAMES = tuple(name for name, _ in BIG)


def _weight_sends(shards, names, li):
    return _ChipSends([shards[name][li] for name in names], _pick_weight_half)


def _assemble_layer(shards, names, li, over_ici, place):
    x, y, core, _ = place
    from_sibling = _swap_cores(over_ici)
    full = {}
    for name, a, b in zip(names, over_ici, from_sibling):
        own = shards[name][li]
        own_halves = own.reshape(2, own.shape[0] // 2, own.shape[1])
        halves = [_join_halves(a[j][None], b[j][None], core, 0) for j in range(3)]
        full[name] = _chips_to_full(name, _by_chip(own_halves, halves, x, y))
    return full


def _pair_sums(tensors, wire, core, theirs=None):
    if theirs is None:
        theirs = _SiblingSends(tensors).run("swap_halves")
    return [_add2(lax.dynamic_index_in_dim(g, core, 0, keepdims=False), a, dt, "sum_core_pair")
            for g, a, dt in zip(tensors, theirs, wire)]


def _layer_grad_halves(layer_grads, names):
    return [_full_to_dests(name, layer_grads[name]) for name in names]


def _step(p, m, v, x, c, target):
    depth = p['norm_g'].shape[0]
    d = D_MODEL
    ix, iy, ic = lax.axis_index("x"), lax.axis_index("y"), lax.axis_index("c")
    chip = 2 * ix + iy
    dev = 4 * ix + 2 * iy + ic
    x0 = x[0]

    c_pad = jnp.pad(c, ((0, SUBLANES - 1), (0, 0)))
    c_all = _exchange(c_pad, 'xyc', False, "gather_c")[:, 0, :]
    n_ada = p['w_ada'].shape[-1]
    b_shard = lax.dynamic_slice_in_dim(p['b_ada'], chip * n_ada, n_ada, axis=1)[:, None, :]
    mod_shard = _ada_fwd(c_all, p['w_ada'].astype(BF16), b_shard)
    mod_all = _exchange(mod_shard.reshape(depth * N_DEV, n_ada), 'xy', False, "gather_mod")
    mod_all = jnp.transpose(mod_all.reshape(N_CHIPS, depth, N_DEV, n_ada), (1, 2, 0, 3)).reshape(depth, N_DEV, 3 * d)
    mods = lax.dynamic_index_in_dim(mod_all, dev, axis=1, keepdims=True)

    place = (ix, iy, ic, chip)
    n_big = len(BIG)

    shards = {name: p[name].astype(BF16) for name, _ in BIG}
    tables, tables_vjp = _all_tables(p)
    others = BIG_NAMES[1:]
    w_in = _assemble_layer(shards, BIG_NAMES[:1], 0, _weight_sends(shards, BIG_NAMES[:1], 0).run("send_weight_halves"),
                           place)['w_in']
    rest = (_weight_sends(shards, others, 0), lambda arrived: _assemble_layer(shards, others, 0, arrived, place))
    saved = []
    xs = x0
    for li in range(depth):
        more = li + 1 < depth
        xs, s, arrived = _layer_fwd(xs, p, w_in, rest, li, mods[li], tables,
                                    _weight_sends(shards, BIG_NAMES[:1], li + 1) if more else None,
                                    _weight_sends(shards, others, li + 1) if more else None)
        saved.append(s)
        if more:
            rest = _assemble_layer(shards, BIG_NAMES, li + 1, arrived, place)
            w_in = rest.pop('w_in')
    dx, hst = _loss_head(xs, _row(p['final_g']), target[0])
    loss = lax.psum(hst[1, 0], ("x", "y", "c"))

    grads, dmods, pair, from_chips = [None] * depth, [None] * depth, [None] * depth, [None] * depth
    first_w_in = {}

    def send_first_w_in(g_w_in):
        first_w_in['pair'] = _pair_sums([_full_to_dests('w_in', g_w_in)], [BF16], ic)
        return _ChipSends(first_w_in['pair'], _pick_slab)

    upper = None
    for li in reversed(range(depth)):
        dx, grads[li], dmods[li], (pairs, arrived), arrived_late = _layer_bwd(
            dx, saved[li], p, li, mods[li], ic, upper, others if li == 0 else (), send_first_w_in if li == 0 else None)
        if upper is not None:
            pair[li + 1], from_chips[li + 1] = pairs[:n_big], arrived[:n_big]
        upper = _layer_grad_halves(grads[li], BIG_NAMES) if li > 0 else None
    first_others = (pairs[-len(others):], arrived[-len(others):])

    dmod_pad = jnp.pad(jnp.stack(dmods), ((0, SUBLANES - depth), (0, 0)))
    dmod_all = _exchange(dmod_pad, 'xyc', False, "gather_dmod")[:, :depth, :]
    dmod_cols = lax.dynamic_slice_in_dim(jnp.transpose(dmod_all, (1, 0, 2)), chip * n_ada, n_ada, axis=2)
    g_w_ada = _ada_bwd(jnp.transpose(c_all), dmod_cols)

    local_small = {k: jnp.stack([grads[li][k] for li in range(depth)])
                   for k in SMALL if k not in ('final_g', 'b_ada') + SSM_RAW}
    table_cotangents = tuple(jnp.stack([grads[li]['ssm_tables'][k] for li in range(depth)]) for k in range(3))
    local_small.update(zip(SSM_RAW, tables_vjp(table_cotangents)))
    local_small['final_g'] = hst[0]
    local_small['b_ada'] = jnp.stack(dmods)
    small_pack = _pack_rows([local_small[k] for k in SMALL], PACK_COLS, F32)

    small_halves = small_pack.reshape(2, 1, small_pack.shape[0] // 2, small_pack.shape[1])
    small_pair = _pair_sums([small_halves], [F32], ic)
    pair[0] = first_w_in['pair'] + first_others[0] + small_pair
    from_chips[0] = (list(arrived_late) + list(first_others[1])
                     + list(_ChipSends(small_pair, _pick_slab).run("scatter_chips")))
    halves = []
    for li in range(depth):
        for t, (ps, others) in enumerate(zip(pair[li], from_chips[li])):
            small = t == n_big
            own = ps[0] if small else lax.dynamic_index_in_dim(ps, chip, 0, keepdims=False)
            halves.append(_sum_chips(own, others, small, "sum_chips_small" if small else "sum_chips")[None])
    totals = [_join_halves(h[0], o[0], ic, 0) for h, o in zip(halves, _swap_cores(halves))]
    small_sum = totals[n_big]
    per_layer = [totals[:n_big]] + [totals[n_big + 1 + (li - 1) * n_big:n_big + 1 + li * n_big] for li in range(1, depth)]

    small_shapes = [p[k].shape for k in SMALL]
    grad = dict(zip(SMALL, _unpack_rows(small_sum, small_shapes)))
    grad['w_ada'] = g_w_ada
    for t, (name, _) in enumerate(BIG):
        grad[name] = jnp.stack([per_layer[li][t] for li in range(depth)])

    delta, new_m, new_v = {}, {}, {}
    outs = _adamw(_pack_rows([p[k] for k in SMALL], PACK_COLS, F32), small_sum,
                  _pack_rows([m[k] for k in SMALL], PACK_COLS, F32), _pack_rows([v[k] for k in SMALL], PACK_COLS, F32),
                  name="adamw_small")
    for res, o in zip((delta, new_m, new_v), outs):
        res.update(zip(SMALL, _unpack_rows(o, small_shapes)))
    for name in ['w_ada'] + [n for n, _ in BIG]:
        shape = p[name].shape
        two_d = (-1, shape[-1])
        outs = _adamw(p[name].reshape(two_d), grad[name].reshape(two_d), m[name].reshape(two_d), v[name].reshape(two_d),
                      name="adamw_" + name)
        delta[name], new_m[name], new_v[name] = (o.reshape(shape) for o in outs)

    return (loss, dx[None], *[grad[k] for k in WEIGHTS], *[delta[k] for k in WEIGHTS],
            *[new_m[k] for k in WEIGHTS], *[new_v[k] for k in WEIGHTS])


def kernel(x, c, norm_g, w_ada, b_ada, w_in, attn_sinks, ssm_a_re, ssm_a_im, ssm_log_dt, ssm_b_re, ssm_b_im, ssm_c_re, ssm_c_im, ssm_d, w_glu, b_glu, w_pool, pool_scale, w_br_att, w_br_ssm, w_br_pool, w_out, final_g, loss_target, m_norm_g, m_w_ada, m_b_ada, m_w_in, m_attn_sinks, m_ssm_a_re, m_ssm_a_im, m_ssm_log_dt, m_ssm_b_re, m_ssm_b_im, m_ssm_c_re, m_ssm_c_im, m_ssm_d, m_w_glu, m_b_glu, m_w_pool, m_pool_scale, m_w_br_att, m_w_br_ssm, m_w_br_pool, m_w_out, m_final_g, v_norm_g, v_w_ada, v_b_ada, v_w_in, v_attn_sinks, v_ssm_a_re, v_ssm_a_im, v_ssm_log_dt, v_ssm_b_re, v_ssm_b_im, v_ssm_c_re, v_ssm_c_im, v_ssm_d, v_w_glu, v_b_glu, v_w_pool, v_pool_scale, v_w_br_att, v_w_br_ssm, v_w_br_pool, v_w_out, v_final_g):
    p = dict(zip(WEIGHTS, (norm_g, w_ada, b_ada, w_in, attn_sinks, ssm_a_re, ssm_a_im, ssm_log_dt, ssm_b_re, ssm_b_im,
                           ssm_c_re, ssm_c_im, ssm_d, w_glu, b_glu, w_pool, pool_scale, w_br_att, w_br_ssm, w_br_pool,
                           w_out, final_g)))
    m = dict(zip(WEIGHTS, (m_norm_g, m_w_ada, m_b_ada, m_w_in, m_attn_sinks, m_ssm_a_re, m_ssm_a_im, m_ssm_log_dt,
                           m_ssm_b_re, m_ssm_b_im, m_ssm_c_re, m_ssm_c_im, m_ssm_d, m_w_glu, m_b_glu, m_w_pool,
                           m_pool_scale, m_w_br_att, m_w_br_ssm, m_w_br_pool, m_w_out, m_final_g)))
    v = dict(zip(WEIGHTS, (v_norm_g, v_w_ada, v_b_ada, v_w_in, v_attn_sinks, v_ssm_a_re, v_ssm_a_im, v_ssm_log_dt,
                           v_ssm_b_re, v_ssm_b_im, v_ssm_c_re, v_ssm_c_im, v_ssm_d, v_w_glu, v_b_glu, v_w_pool,
                           v_pool_scale, v_w_br_att, v_w_br_ssm, v_w_br_pool, v_w_out, v_final_g)))
    return _step(p, m, v, x, c, loss_target)
```

```python
import functools
import math

import jax
import jax.numpy as jnp
from jax import lax
from jax.experimental import pallas as pl
from jax.experimental.pallas import tpu as pltpu

F32 = jnp.float32
BF16 = jnp.bfloat16

D_MODEL = 1024
CHUNK = 64
N_HEADS = 8
N_KV_HEADS = 2
HEAD_DIM = 64
Q_PER_KV = N_HEADS // N_KV_HEADS
HALO = 128
ATT_W = 512
KV_W = 128
SSM_W = 512
SSM_GROUP = 16
SSM_GROUPS = 32
SSM_STATE = 64
POOL_W = 512
POOL_GW = 128
IN_W = 6400
EPS = 1e-6
NEG_INF = -1e30
ADAM_LR = 0.001
ADAM_B1 = 0.9
ADAM_B2 = 0.999
ADAM_EPS = 1e-08
ADAM_WD = 0.01
ADAM_STEP = 10

OFF_GL, OFF_Q, OFF_ZA, OFF_US, OFF_ZS, OFF_UP, OFF_ZP, OFF_KV = 0, 3072, 3584, 4096, 4608, 5120, 5632, 6144
_PERM_PIECES = ((3328, 3072), (0, 512), (1792, 512), (768, 512), (2304, 512), (1280, 512), (2816, 512), (512, 256))

LANES = 128
SUBLANES = 8
N_SBLK = SSM_GROUPS * SSM_STATE // LANES
VMEM_LIMIT = 48 * 1024 * 1024

N_CHIPS = 4
N_DEV = 8

WEIGHTS = ['norm_g', 'w_ada', 'b_ada', 'w_in', 'attn_sinks', 'ssm_a_re', 'ssm_a_im', 'ssm_log_dt', 'ssm_b_re',
           'ssm_b_im', 'ssm_c_re', 'ssm_c_im', 'ssm_d', 'w_glu', 'b_glu', 'w_pool', 'pool_scale', 'w_br_att',
           'w_br_ssm', 'w_br_pool', 'w_out', 'final_g']
SMALL = ['norm_g', 'b_ada', 'attn_sinks', 'ssm_a_re', 'ssm_a_im', 'ssm_log_dt', 'ssm_b_re', 'ssm_b_im', 'ssm_c_re',
         'ssm_c_im', 'ssm_d', 'b_glu', 'w_pool', 'pool_scale', 'final_g']
BIG = (('w_in', (1024, 1600)), ('w_br_att', (512, 256)), ('w_br_ssm', (512, 256)), ('w_br_pool', (512, 256)),
       ('w_out', (256, 1024)), ('w_glu', (128, 512)))
PACK_COLS = 1024


def _params(sem=None):
    return pltpu.CompilerParams(dimension_semantics=sem, vmem_limit_bytes=VMEM_LIMIT)


def _row(v):
    return v.reshape(1, -1)


def _full(shape):
    nd = len(shape)
    return pl.BlockSpec(shape, lambda *_: (0,) * nd)


def _sigmoid(v):
    return 1.0 / (1.0 + jnp.exp(-v))


def _silu_and_grad(z):
    s = _sigmoid(z)
    return z * s, s * (1.0 + z * (1.0 - s))


_GELU_K = math.sqrt(2.0 / math.pi)


def _gelu_and_grad(v):
    inner = _GELU_K * (v + 0.044715 * v * v * v)
    th = jnp.tanh(inner)
    val = 0.5 * v * (1.0 + th)
    grad = 0.5 * (1.0 + th) + 0.5 * v * (1.0 - th * th) * _GELU_K * (1.0 + 3 * 0.044715 * v * v)
    return val, grad


_NN = (((1,), (0,)), ((), ()))
_NT = (((1,), (1,)), ((), ()))
_TN = (((0,), (0,)), ((), ()))


def _dot(a, b, dims=_NN):
    return lax.dot_general(a.astype(BF16), b.astype(BF16), dims, preferred_element_type=F32)


def _matmul(a, b, mode, *, tm, tn, tk, name, out_dtype=F32, n_outer=False, sends=None):
    if mode == 'nn':
        (m, k), (_, n) = a.shape, b.shape
    elif mode == 'nt':
        (m, k), (n, _) = a.shape, b.shape
    else:
        (k, m), (_, n) = a.shape, b.shape
    tm, tn, tk = min(tm, m), min(tn, n), min(tk, k)
    assert m % tm == 0 and n % tn == 0 and k % tk == 0, (name, a.shape, b.shape)
    nk = k // tk
    dims = {'nn': _NN, 'nt': _NT, 'tn': _TN}[mode]

    def body(a_ref, b_ref, o_ref, acc_ref):
        if nk == 1:
            o_ref[...] = _dot(a_ref[...], b_ref[...], dims).astype(out_dtype)
            return
        kk = pl.program_id(2)

        @pl.when(kk == 0)
        def _():
            acc_ref[...] = jnp.zeros_like(acc_ref)

        acc_ref[...] += _dot(a_ref[...], b_ref[...], dims)

        @pl.when(kk == nk - 1)
        def _():
            o_ref[...] = acc_ref[...].astype(out_dtype)

    def spec(shape, index):
        if n_outer:
            return pl.BlockSpec(shape, lambda j, i, kk: index(i, j, kk))
        return pl.BlockSpec(shape, index)

    a_spec = spec((tk, tm), lambda i, j, kk: (kk, i)) if mode == 'tn' else spec((tm, tk), lambda i, j, kk: (i, kk))
    b_spec = spec((tn, tk), lambda i, j, kk: (j, kk)) if mode == 'nt' else spec((tk, tn), lambda i, j, kk: (kk, j))
    grid = (n // tn, m // tm, nk) if n_outer else (m // tm, n // tn, nk)
    body, x_in, x_out, x_shapes, x_scratch, x_args = _carry_sends(body, 2, 1, 1, grid, sends)
    semantics = ("parallel", "parallel", "arbitrary") if sends is None else ("arbitrary",) * 3
    outs = pl.pallas_call(
        body, name=name, grid=grid, in_specs=[a_spec, b_spec] + x_in,
        out_specs=[spec((tm, tn), lambda i, j, kk: (i, j))] + x_out,
        out_shape=[jax.ShapeDtypeStruct((m, n), out_dtype)] + x_shapes,
        scratch_shapes=[pltpu.VMEM((tm, tn), F32)] + x_scratch, compiler_params=_params(semantics))(a, b, *x_args)
    return outs[0] if sends is None else (outs[0], list(outs[1:]))


ROWS_NORM = 512


def _norm_fwd(x, g, mod):
    l, d = x.shape
    tr = min(ROWS_NORM, l)

    def body(x_ref, g_ref, mod_ref, h_ref):
        xv = x_ref[...]
        r = lax.rsqrt(jnp.mean(xv * xv, axis=-1, keepdims=True) + EPS)
        shift, scale = mod_ref[:, 0:d], mod_ref[:, d:2 * d]
        h_ref[...] = ((xv * r * g_ref[...]) * (1.0 + scale) + shift).astype(BF16)

    return pl.pallas_call(
        body, name="norm_fwd", grid=(l // tr,),
        in_specs=[pl.BlockSpec((tr, d), lambda i: (i, 0)), _full((1, d)), _full((1, 3 * d))],
        out_specs=pl.BlockSpec((tr, d), lambda i: (i, 0)), out_shape=jax.ShapeDtypeStruct((l, d), BF16),
        compiler_params=_params(("parallel",)))(x, g, mod)


def _norm_bwd(x, dh, dxo, g, mod):
    l, d = x.shape
    tr = min(ROWS_NORM, l)

    def body(x_ref, dh_ref, dxo_ref, g_ref, mod_ref, dx_ref, st_ref):
        @pl.when(pl.program_id(0) == 0)
        def _():
            st_ref[...] = jnp.zeros_like(st_ref)

        xv, dhv = x_ref[...], dh_ref[...]
        r = lax.rsqrt(jnp.mean(xv * xv, axis=-1, keepdims=True) + EPS)
        xn = xv * r
        gv = g_ref[...]
        sc1 = 1.0 + mod_ref[:, d:2 * d]
        dxn = dhv * gv * sc1
        dx_ref[...] = dxo_ref[...] + r * (dxn - xn * jnp.mean(dxn * xn, axis=-1, keepdims=True))
        st_ref[0:1, :] += jnp.sum(dhv, axis=0, keepdims=True)
        st_ref[1:2, :] += jnp.sum(dhv * xn * gv, axis=0, keepdims=True)
        st_ref[2:3, :] += jnp.sum(dhv * xn * sc1, axis=0, keepdims=True)

    blk = pl.BlockSpec((tr, d), lambda i: (i, 0))
    return pl.pallas_call(
        body, name="norm_bwd", grid=(l // tr,),
        in_specs=[blk, blk, blk, _full((1, d)), _full((1, 3 * d))],
        out_specs=[blk, _full((SUBLANES, d))],
        out_shape=[jax.ShapeDtypeStruct((l, d), F32), jax.ShapeDtypeStruct((SUBLANES, d), F32)],
        compiler_params=_params(("arbitrary",)))(x, dh, dxo, g, mod)


def _loss_head(x, g, target):
    l, d = x.shape
    tr = min(ROWS_NORM, l)

    def body(x_ref, g_ref, t_ref, dx_ref, st_ref):
        @pl.when(pl.program_id(0) == 0)
        def _():
            st_ref[...] = jnp.zeros_like(st_ref)

        xv = x_ref[...]
        r = lax.rsqrt(jnp.mean(xv * xv, axis=-1, keepdims=True) + EPS)
        xn = xv * r
        gv = g_ref[...]
        err = xn * gv - t_ref[...]
        part = 0.5 * jnp.sum(jnp.mean(err * err, axis=-1, keepdims=True), axis=0, keepdims=True)
        dy = err * (1.0 / d)
        dxn = dy * gv
        dx_ref[...] = r * (dxn - xn * jnp.mean(dxn * xn, axis=-1, keepdims=True))
        st_ref[0:1, :] += jnp.sum(dy * xn, axis=0, keepdims=True)
        st_ref[1:2, :] += jnp.broadcast_to(part, (1, d))

    blk = pl.BlockSpec((tr, d), lambda i: (i, 0))
    return pl.pallas_call(
        body, name="loss_head", grid=(l // tr,), in_specs=[blk, _full((1, d)), blk],
        out_specs=[blk, _full((SUBLANES, d))],
        out_shape=[jax.ShapeDtypeStruct((l, d), F32), jax.ShapeDtypeStruct((SUBLANES, d), F32)],
        compiler_params=_params(("arbitrary",)))(x, g, target)


ROWS_ATT = 128
ROWS_ATT_BWD = 256
_SLOPES = tuple(2.0 ** (-8.0 * (h + 1) / N_HEADS) for h in range(N_HEADS))


def _att_mask(i, t):
    r = lax.broadcasted_iota(jnp.int32, (t, t + HALO), 0)
    j = lax.broadcasted_iota(jnp.int32, (t, t + HALO), 1)
    dist = jnp.abs(r + HALO - j).astype(F32)
    rc, jc = r // CHUNK, j // CHUNK
    allowed = (jc >= rc) & (jc <= rc + 2) & ((j >= HALO) | (i > 0))
    return dist, allowed


def _att_probs(qh, k, dist, allowed, slope, sink):
    s = _dot(qh, k, _NT) * (1.0 / math.sqrt(HEAD_DIM)) - slope * dist
    s = jnp.where(allowed, s, NEG_INF)
    m = jnp.maximum(jnp.max(s, axis=1, keepdims=True), sink)
    e = jnp.exp(s - m)
    es = jnp.exp(sink - m)
    den = jnp.sum(e, axis=1, keepdims=True) + es
    return e / den, es / den


def _att_specs(t):
    q_spec = pl.BlockSpec((t, ATT_W), lambda i: (i, OFF_Q // ATT_W))
    kv_spec = pl.BlockSpec((t, 2 * KV_W), lambda i: (i, OFF_KV // (2 * KV_W)))
    halo_spec = pl.BlockSpec((HALO, 2 * KV_W), lambda i: (jnp.maximum(i * (t // HALO) - 1, 0), OFF_KV // (2 * KV_W)))
    return q_spec, kv_spec, halo_spec


def _attention_fwd(proj, sinks):
    l = proj.shape[0]
    t = min(ROWS_ATT, l)

    def body(sink_ref, q_ref, kv_ref, halo_ref, o_ref):
        dist, allowed = _att_mask(pl.program_id(0), t)
        kv = jnp.concatenate([halo_ref[...], kv_ref[...]], axis=0)
        for h in range(N_HEADS):
            kh = h // Q_PER_KV
            k = kv[:, kh * HEAD_DIM:(kh + 1) * HEAD_DIM]
            v = kv[:, KV_W + kh * HEAD_DIM:KV_W + (kh + 1) * HEAD_DIM]
            p, _ = _att_probs(q_ref[:, h * HEAD_DIM:(h + 1) * HEAD_DIM], k, dist, allowed, _SLOPES[h], sink_ref[h])
            o_ref[:, h * HEAD_DIM:(h + 1) * HEAD_DIM] = _dot(p, v)

    q_spec, kv_spec, halo_spec = _att_specs(t)
    return pl.pallas_call(
        body, name="attention_fwd", grid=(l // t,),
        in_specs=[pl.BlockSpec(memory_space=pltpu.SMEM), q_spec, kv_spec, halo_spec],
        out_specs=pl.BlockSpec((t, ATT_W), lambda i: (i, 0)), out_shape=jax.ShapeDtypeStruct((l, ATT_W), F32),
        compiler_params=_params(("parallel",)))(sinks, proj, proj, proj)


def _attention_bwd(proj, sinks, do, sends=None):
    l = proj.shape[0]
    t = min(ROWS_ATT_BWD, l)

    def body(sink_ref, q_ref, kv_ref, halo_ref, do_ref, dq_ref, dkv_ref, dsink_ref):
        i = pl.program_id(0)

        @pl.when(i == 0)
        def _():
            dkv_ref[...] = jnp.zeros_like(dkv_ref)
            dsink_ref[...] = jnp.zeros_like(dsink_ref)

        dist, allowed = _att_mask(i, t)
        kv = jnp.concatenate([halo_ref[...], kv_ref[...]], axis=0)
        rows = pl.ds(pl.multiple_of(i * t, t), t + HALO)
        for kh in range(N_KV_HEADS):
            k = kv[:, kh * HEAD_DIM:(kh + 1) * HEAD_DIM]
            v = kv[:, KV_W + kh * HEAD_DIM:KV_W + (kh + 1) * HEAD_DIM]
            dk = jnp.zeros((t + HALO, HEAD_DIM), F32)
            dv = jnp.zeros((t + HALO, HEAD_DIM), F32)
            for h in range(kh * Q_PER_KV, (kh + 1) * Q_PER_KV):
                qh = q_ref[:, h * HEAD_DIM:(h + 1) * HEAD_DIM]
                doh = do_ref[:, h * HEAD_DIM:(h + 1) * HEAD_DIM]
                p, ps = _att_probs(qh, k, dist, allowed, _SLOPES[h], sink_ref[h])
                dp = _dot(doh, v, _NT)
                delta = jnp.sum(p * dp, axis=1, keepdims=True)
                ds = p * (dp - delta) * (1.0 / math.sqrt(HEAD_DIM))
                dsink_ref[h:h + 1, :] += jnp.broadcast_to(-jnp.sum(ps * delta, axis=0, keepdims=True), (1, LANES))
                dq_ref[:, h * HEAD_DIM:(h + 1) * HEAD_DIM] = _dot(ds, k).astype(BF16)
                dk = dk + _dot(ds, qh, _TN)
                dv = dv + _dot(p, doh, _TN)
            dkv_ref[rows, kh * HEAD_DIM:(kh + 1) * HEAD_DIM] += dk
            dkv_ref[rows, KV_W + kh * HEAD_DIM:KV_W + (kh + 1) * HEAD_DIM] += dv

    q_spec, kv_spec, halo_spec = _att_specs(t)
    blk = pl.BlockSpec((t, ATT_W), lambda i: (i, 0))
    body, x_in, x_out, x_shapes, x_scratch, x_args = _carry_sends(body, 5, 3, 0, (l // t,), sends)
    outs = pl.pallas_call(
        body, name="attention_bwd" if sends is None else "attention_bwd_sending", grid=(l // t,),
        in_specs=[pl.BlockSpec(memory_space=pltpu.SMEM), q_spec, kv_spec, halo_spec, blk] + x_in,
        out_specs=[blk, _full((HALO + l, 2 * KV_W)), _full((N_HEADS, LANES))] + x_out,
        out_shape=[jax.ShapeDtypeStruct((l, ATT_W), BF16), jax.ShapeDtypeStruct((HALO + l, 2 * KV_W), F32),
                   jax.ShapeDtypeStruct((N_HEADS, LANES), F32)] + x_shapes,
        scratch_shapes=x_scratch, compiler_params=_params(("arbitrary",)))(sinks, proj, proj, proj, do, *x_args)
    return outs[0], outs[1], outs[2], list(outs[3:])


ROWS_SSM = 2048
SSM_BLOCKS = 2
SCAN_UNROLL = 2
SSM_STEPS_PER_U = LANES // (SSM_BLOCKS * 2 * SSM_GROUP)


def _ssm_discretize(a_re, a_im, log_dt, b_re, b_im):
    lam = lax.complex(a_re, a_im)
    dt = jnp.exp(log_dt)[:, None]
    lam_bar = jnp.exp(lam * dt)
    b_bar = ((lam_bar - 1.0) / lam)[..., None] * lax.complex(b_re, b_im)
    return lam, dt, lam_bar, b_bar


def _ssm_block_diag(m):
    e = m.reshape(N_SBLK, 2, SSM_GROUP, SSM_STATE)
    e = e[:, :, :, None, :] * jnp.eye(2, dtype=m.dtype)[None, :, None, :, None]
    e = e.reshape(N_SBLK, 2 * SSM_GROUP, LANES)
    oh = jax.nn.one_hot(jnp.arange(N_SBLK) % 4, 4, dtype=m.dtype)
    return (oh[:, :, None, None] * e[:, None]).reshape(N_SBLK, LANES, LANES)


def _ssm_tables(a_re, a_im, log_dt, b_re, b_im, c_re, c_im):
    _, _, lam_bar, b_bar = _ssm_discretize(a_re, a_im, log_dt, b_re, b_im)
    lam_blk = jnp.stack([jnp.real(lam_bar).reshape(N_SBLK, LANES), jnp.imag(lam_bar).reshape(N_SBLK, LANES)], axis=1)
    bt = jnp.transpose(b_bar, (0, 2, 1))
    bb = jnp.concatenate([_ssm_block_diag(jnp.real(bt)), _ssm_block_diag(jnp.imag(bt))], axis=2)
    cb = jnp.concatenate([jnp.transpose(_ssm_block_diag(c_re), (0, 2, 1)),
                          jnp.transpose(_ssm_block_diag(-c_im), (0, 2, 1))], axis=1)
    return lam_blk, bb, cb


def _ssm_powers(a_re, a_im, log_dt):
    lam = lax.complex(a_re, a_im)
    dt = jnp.exp(log_dt)[:, None]
    k = jnp.arange(1, SUBLANES + 1, dtype=F32)
    pw = jnp.exp((lam * dt)[None] * k[:, None, None]).reshape(SUBLANES, N_SBLK, LANES)
    pw = jnp.transpose(pw, (1, 0, 2))
    rev = pw[:, ::-1]
    return jnp.concatenate([jnp.real(pw), jnp.imag(pw), jnp.real(rev), jnp.imag(rev)], axis=1)


def _scan_consts(pw_ref, b, reverse):
    row = lax.broadcasted_iota(jnp.int32, (SUBLANES, LANES), 0)
    sign = -1.0 if reverse else 1.0

    def power(k):
        return (jnp.broadcast_to(pw_ref[b, k - 1:k, :], (SUBLANES, LANES)),
                sign * jnp.broadcast_to(pw_ref[b, SUBLANES + k - 1:SUBLANES + k, :], (SUBLANES, LANES)))

    steps = []
    for d in (1, 2, 4):
        pr, pi = power(d)
        keep = (row < SUBLANES - d) if reverse else (row >= d)
        steps.append((d, jnp.where(keep, pr, 0.0), jnp.where(keep, pi, 0.0)))
    base = 2 * SUBLANES if reverse else 0
    return steps, pw_ref[b, base:base + SUBLANES, :], sign * pw_ref[b, base + SUBLANES:base + 2 * SUBLANES, :]


def _scan_tile(xr, xi, steps, reverse):
    for d, ar, ai in steps:
        shift = SUBLANES - d if reverse else d
        rr, ri = pltpu.roll(xr, shift, 0), pltpu.roll(xi, shift, 0)
        xr, xi = xr + (ar * rr - ai * ri), xi + (ar * ri + ai * rr)
    return xr, xi


def _bcast_row(v, r):
    return jnp.broadcast_to(v[r:r + 1, :], (SUBLANES, LANES))


def _re_im(b):
    return slice(2 * b * LANES, (2 * b + 1) * LANES), slice((2 * b + 1) * LANES, (2 * b + 2) * LANES)


def _scan_forward(s_scr, pw_ref, n_tiles):
    consts = [_scan_consts(pw_ref, b, False) for b in range(SSM_BLOCKS)]

    def tiles(g, carry):
        carry = list(carry)
        rows = [pl.ds(pl.multiple_of((g * SCAN_UNROLL + u) * SUBLANES, SUBLANES), SUBLANES) for u in range(SCAN_UNROLL)]
        loaded = [[(s_scr[r, _re_im(b)[0]], s_scr[r, _re_im(b)[1]]) for b in range(SSM_BLOCKS)] for r in rows]
        local = [[_scan_tile(xr, xi, consts[b][0], False) for b, (xr, xi) in enumerate(per_row)] for per_row in loaded]
        for r, per_row in zip(rows, local):
            for b, (xr, xi) in enumerate(per_row):
                _, pr, pi = consts[b]
                cr, ci = carry[b]
                sr, si = xr + (pr * cr - pi * ci), xi + (pr * ci + pi * cr)
                s_scr[r, _re_im(b)[0]] = sr
                s_scr[r, _re_im(b)[1]] = si
                carry[b] = (_bcast_row(sr, SUBLANES - 1), _bcast_row(si, SUBLANES - 1))
        return tuple(carry)

    zero = jnp.zeros((SUBLANES, LANES), F32)
    lax.fori_loop(0, n_tiles // SCAN_UNROLL, tiles, ((zero, zero),) * SSM_BLOCKS)


def _row_chunks(l):
    rc = min(ROWS_SSM, l)
    return rc, l // rc


def _carry_sends(body, n_in, n_out, n_scratch, grid, sends):
    if sends is None:
        return body, [], [], [], [], []
    k = len(sends.srcs)

    def carrying(*refs):
        ins, send_ins = refs[:n_in], refs[n_in:n_in + k]
        outs, send_outs = refs[n_in + k:n_in + k + n_out], refs[n_in + k + n_out:n_in + 2 * k + n_out]
        rest = refs[n_in + 2 * k + n_out:]
        scratch, sems = rest[:n_scratch], rest[n_scratch:]
        ids = [pl.program_id(axis) for axis in range(len(grid))]
        first = functools.reduce(jnp.logical_and, [i == 0 for i in ids])
        last = functools.reduce(jnp.logical_and, [i == n - 1 for i, n in zip(ids, grid)])

        @pl.when(first)
        def _():
            sends.start(send_ins, send_outs, *sems)

        body(*ins, *outs, *scratch)

        @pl.when(last)
        def _():
            sends.finish(send_ins, send_outs, *sems)

    in_specs, out_specs, scratch = sends.specs()
    return carrying, in_specs, out_specs, sends.out_shapes, scratch, sends.srcs


def _ssm_fwd(proj, bb, cb, pw, sends=None):
    l = proj.shape[0]
    rc, n_chunks = _row_chunks(l)

    def body(u_ref, bb_ref, cb_ref, pw_ref, y_ref, s_scr):
        j = pl.program_id(0)

        def fill(ci, _):
            rows = pl.ds(pl.multiple_of(ci * rc, rc), rc)
            uv = u_ref[rows, :]
            for b in range(SSM_BLOCKS):
                s_scr[rows, 2 * b * LANES:2 * (b + 1) * LANES] = _dot(uv, bb_ref[b])
            return 0

        lax.fori_loop(0, n_chunks, fill, 0)
        _scan_forward(s_scr, pw_ref, l // SUBLANES)

        @pl.when(j % SSM_STEPS_PER_U == 0)
        def _():
            y_ref[...] = jnp.zeros_like(y_ref)

        def emit(ci, _):
            rows = pl.ds(pl.multiple_of(ci * rc, rc), rc)
            for b in range(SSM_BLOCKS):
                y_ref[rows, :] += _dot(s_scr[rows, 2 * b * LANES:2 * (b + 1) * LANES], cb_ref[b])
            return 0

        lax.fori_loop(0, n_chunks, emit, 0)

    steps = N_SBLK // SSM_BLOCKS
    body, x_in, x_out, x_shapes, x_scratch, x_args = _carry_sends(body, 4, 2, 0, (steps,), sends)
    outs = pl.pallas_call(
        body, name="ssm_fwd" if sends is None else "ssm_fwd_sending", grid=(steps,),
        in_specs=[pl.BlockSpec((l, LANES), lambda j: (0, OFF_US // LANES + j // SSM_STEPS_PER_U)),
                  pl.BlockSpec((SSM_BLOCKS, LANES, 2 * LANES), lambda j: (j, 0, 0)),
                  pl.BlockSpec((SSM_BLOCKS, 2 * LANES, LANES), lambda j: (j, 0, 0)),
                  pl.BlockSpec((SSM_BLOCKS, 4 * SUBLANES, LANES), lambda j: (j, 0, 0))] + x_in,
        out_specs=[pl.BlockSpec((l, LANES), lambda j: (0, j // SSM_STEPS_PER_U)),
                   pl.BlockSpec((l, SSM_BLOCKS * 2 * LANES), lambda j: (0, j))] + x_out,
        out_shape=[jax.ShapeDtypeStruct((l, SSM_W), F32), jax.ShapeDtypeStruct((l, N_SBLK * 2 * LANES), F32)] + x_shapes,
        scratch_shapes=x_scratch, compiler_params=_params(("arbitrary",)))(proj, bb, cb, pw, *x_args)
    return outs[0], outs[1], list(outs[2:])


def _ssm_bwd(proj, states, dy, bb, cb, pw, sends=None):
    l = proj.shape[0]
    rc, n_chunks = _row_chunks(l)
    n_tiles = l // SUBLANES

    def body(u_ref, s_scr, dy_ref, bb_ref, cb_ref, pw_ref, du_ref, dbb_ref, dcb_ref, dlam_ref, a_scr):
        j = pl.program_id(0)
        dcb_ref[...] = jnp.zeros_like(dcb_ref)

        def through_c(ci, _):
            rows = pl.ds(pl.multiple_of(ci * rc, rc), rc)
            dyv = dy_ref[rows, :]
            for b in range(SSM_BLOCKS):
                lanes = slice(2 * b * LANES, 2 * (b + 1) * LANES)
                dcb_ref[b] += _dot(s_scr[rows, lanes], dyv, _TN)
                a_scr[rows, lanes] = _dot(dyv, cb_ref[b], _NT)
            return 0

        lax.fori_loop(0, n_chunks, through_c, 0)

        consts = [_scan_consts(pw_ref, b, True) for b in range(SSM_BLOCKS)]
        row = lax.broadcasted_iota(jnp.int32, (SUBLANES, LANES), 0)
        last = row == SUBLANES - 1

        def tiles(g, carry):
            carry = list(carry)
            rows = [pl.ds(pl.multiple_of((n_tiles - 1 - (g * SCAN_UNROLL + u)) * SUBLANES, SUBLANES), SUBLANES)
                    for u in range(SCAN_UNROLL)]
            loaded = [[(a_scr[r, _re_im(b)[0]], a_scr[r, _re_im(b)[1]]) for b in range(SSM_BLOCKS)] for r in rows]
            states = [[(s_scr[r, _re_im(b)[0]], s_scr[r, _re_im(b)[1]]) for b in range(SSM_BLOCKS)] for r in rows]
            local = [[_scan_tile(xr, xi, consts[b][0], True) for b, (xr, xi) in enumerate(per_row)] for per_row in loaded]
            for r, per_row, state_row in zip(rows, local, states):
                for b, ((xr, xi), (sr, si)) in enumerate(zip(per_row, state_row)):
                    _, pr, pi = consts[b]
                    cr, ci, accr, acci = carry[b]
                    gr, gi = xr + (pr * cr - pi * ci), xi + (pr * ci + pi * cr)
                    a_scr[r, _re_im(b)[0]] = gr
                    a_scr[r, _re_im(b)[1]] = gi
                    ur = jnp.where(last, cr, pltpu.roll(gr, SUBLANES - 1, 0))
                    ui = jnp.where(last, ci, pltpu.roll(gi, SUBLANES - 1, 0))
                    carry[b] = (_bcast_row(gr, 0), _bcast_row(gi, 0),
                                accr + (ur * sr + ui * si), acci + (ui * sr - ur * si))
            return tuple(carry)

        zero = jnp.zeros((SUBLANES, LANES), F32)
        done = lax.fori_loop(0, n_tiles // SCAN_UNROLL, tiles, ((zero, zero, zero, zero),) * SSM_BLOCKS)
        for b in range(SSM_BLOCKS):
            dlr = jnp.broadcast_to(jnp.sum(done[b][2], axis=0, keepdims=True), (SUBLANES, LANES))
            dli = jnp.broadcast_to(jnp.sum(done[b][3], axis=0, keepdims=True), (SUBLANES, LANES))
            dlam_ref[b] = jnp.where(row == 0, dlr, jnp.where(row == 1, dli, 0.0))

        dbb_ref[...] = jnp.zeros_like(dbb_ref)

        @pl.when(j % SSM_STEPS_PER_U == 0)
        def _():
            du_ref[...] = jnp.zeros_like(du_ref)

        def through_b(ci, _):
            rows = pl.ds(pl.multiple_of(ci * rc, rc), rc)
            uv = u_ref[rows, :]
            for b in range(SSM_BLOCKS):
                av = a_scr[rows, 2 * b * LANES:2 * (b + 1) * LANES]
                dbb_ref[b] += _dot(uv, av, _TN)
                du_ref[rows, :] += _dot(av, bb_ref[b], _NT)
            return 0

        lax.fori_loop(0, n_chunks, through_b, 0)

    steps = N_SBLK // SSM_BLOCKS
    body, x_in, x_out, x_shapes, x_scratch, x_args = _carry_sends(body, 6, 4, 1, (steps,), sends)
    outs = pl.pallas_call(
        body, name="ssm_bwd" if sends is None else "ssm_bwd_sending", grid=(steps,),
        in_specs=[pl.BlockSpec((l, LANES), lambda j: (0, OFF_US // LANES + j // SSM_STEPS_PER_U)),
                  pl.BlockSpec((l, SSM_BLOCKS * 2 * LANES), lambda j: (0, j)),
                  pl.BlockSpec((l, LANES), lambda j: (0, j // SSM_STEPS_PER_U)),
                  pl.BlockSpec((SSM_BLOCKS, LANES, 2 * LANES), lambda j: (j, 0, 0)),
                  pl.BlockSpec((SSM_BLOCKS, 2 * LANES, LANES), lambda j: (j, 0, 0)),
                  pl.BlockSpec((SSM_BLOCKS, 4 * SUBLANES, LANES), lambda j: (j, 0, 0))] + x_in,
        out_specs=[pl.BlockSpec((l, LANES), lambda j: (0, j // SSM_STEPS_PER_U)),
                   pl.BlockSpec((SSM_BLOCKS, LANES, 2 * LANES), lambda j: (j, 0, 0)),
                   pl.BlockSpec((SSM_BLOCKS, 2 * LANES, LANES), lambda j: (j, 0, 0)),
                   pl.BlockSpec((SSM_BLOCKS, SUBLANES, LANES), lambda j: (j, 0, 0))] + x_out,
        out_shape=[jax.ShapeDtypeStruct((l, SSM_W), F32), jax.ShapeDtypeStruct((N_SBLK, LANES, 2 * LANES), F32),
                   jax.ShapeDtypeStruct((N_SBLK, 2 * LANES, LANES), F32),
                   jax.ShapeDtypeStruct((N_SBLK, SUBLANES, LANES), F32)] + x_shapes,
        scratch_shapes=[pltpu.VMEM((l, SSM_BLOCKS * 2 * LANES), F32)] + x_scratch,
        compiler_params=_params(("arbitrary",)))(proj, states, dy, bb, cb, pw, *x_args)
    return outs[0], outs[1], outs[2], outs[3], list(outs[4:])


def _pool_windows(v, t, l, ahead):
    def shifted(a, d):
        if ahead:
            return jnp.where(t < l - d, pltpu.roll(a, l - d, 0), 0.0)
        return jnp.where(t >= d, pltpu.roll(a, d, 0), 0.0)

    w2 = v + shifted(v, 1)
    w4 = w2 + shifted(w2, 2)
    w8 = w4 + shifted(w4, 4)
    w16 = w8 + shifted(w8, 8)
    return w2, w4, w8, w16


def _pool_select(g, ws):
    return jnp.where(g == 0, ws[0], jnp.where(g == 1, ws[1], jnp.where(g == 2, ws[2], ws[3])))


def _pool_count(g, t):
    return jnp.minimum(t + 1, jnp.left_shift(2, g)).astype(F32)


def _pool_specs(l):
    return [pl.BlockSpec((l, POOL_GW), lambda g: (0, OFF_UP // POOL_GW + g)),
            pl.BlockSpec((1, POOL_GW, POOL_GW), lambda g: (g, 0, 0)),
            pl.BlockSpec((1, POOL_GW), lambda g: (0, g))]


def _pool_fwd(proj, w_pool, scale):
    l = proj.shape[0]

    def body(u_ref, w_ref, sc_ref, y_ref):
        g = pl.program_id(0)
        t = lax.broadcasted_iota(jnp.int32, (l, 1), 0)
        u = u_ref[...]
        pooled = _pool_select(g, _pool_windows(u, t, l, False)) / _pool_count(g, t) - u
        y_ref[...] = _dot(pooled, w_ref[0]) * sc_ref[...]

    return pl.pallas_call(
        body, name="pool_fwd", grid=(4,), in_specs=_pool_specs(l),
        out_specs=pl.BlockSpec((l, POOL_GW), lambda g: (0, g)), out_shape=jax.ShapeDtypeStruct((l, POOL_W), F32),
        compiler_params=_params(("parallel",)))(proj, w_pool, scale)


def _pool_bwd(proj, w_pool, scale, dy):
    l = proj.shape[0]

    def body(u_ref, w_ref, sc_ref, dy_ref, du_ref, dw_ref, dsc_ref):
        g = pl.program_id(0)
        t = lax.broadcasted_iota(jnp.int32, (l, 1), 0)
        u = u_ref[...]
        cnt = _pool_count(g, t)
        pooled = _pool_select(g, _pool_windows(u, t, l, False)) / cnt - u
        dyv = dy_ref[...]
        dsc_ref[...] = jnp.sum(dyv * _dot(pooled, w_ref[0]), axis=0, keepdims=True)
        dyl = dyv * sc_ref[...]
        dw_ref[0] = _dot(pooled, dyl, _TN)
        dpooled = _dot(dyl, w_ref[0], _NT)
        du_ref[...] = (_pool_select(g, _pool_windows(dpooled / cnt, t, l, True)) - dpooled).astype(BF16)

    return pl.pallas_call(
        body, name="pool_bwd", grid=(4,), in_specs=_pool_specs(l) + [pl.BlockSpec((l, POOL_GW), lambda g: (0, g))],
        out_specs=[pl.BlockSpec((l, POOL_GW), lambda g: (0, g)), pl.BlockSpec((1, POOL_GW, POOL_GW), lambda g: (g, 0, 0)),
                   pl.BlockSpec((1, POOL_GW), lambda g: (0, g))],
        out_shape=[jax.ShapeDtypeStruct((l, POOL_W), BF16), jax.ShapeDtypeStruct((4, POOL_GW, POOL_GW), F32),
                   jax.ShapeDtypeStruct((1, POOL_W), F32)],
        compiler_params=_params(("parallel",)))(proj, w_pool, scale, dy)


ROWS_MERGE = 128
ROWS_MERGE_FWD = 256


def _merge_inputs(tr):
    def col(off, w):
        return pl.BlockSpec((tr, w), lambda i: (i, off // w))

    def act(w):
        return pl.BlockSpec((tr, w), lambda i: (i, 0))

    d = D_MODEL
    return ([col(OFF_GL, 3 * d), col(OFF_ZA, ATT_W), col(OFF_US, SSM_W), col(OFF_ZS, SSM_W), col(OFF_ZP, POOL_W),
             act(ATT_W), act(SSM_W), act(POOL_W)]
            + [_full((1, SSM_W)), _full((SSM_W, SSM_W)), _full((1, SSM_W)), _full((ATT_W, d)), _full((SSM_W, d)),
               _full((POOL_W, d)), _full((d, d)), _full((1, 3 * d))])


def _merge_forward_math(gl_ref, za_ref, us_ref, zs_ref, zp_ref, o_ref, yl_ref, yp_ref, d_ref, wg_ref, bg_ref,
                        wba_ref, wbs_ref, wbp_ref):
    d = D_MODEL
    r = {}
    r['sa'], r['dsa'] = _silu_and_grad(za_ref[...])
    r['ss'], r['dss'] = _silu_and_grad(zs_ref[...])
    r['sp'], r['dsp'] = _silu_and_grad(zp_ref[...])
    r['y_att'] = o_ref[...] * r['sa']
    y1 = yl_ref[...] + d_ref[...] * us_ref[...]
    r['y2'], r['dgelu'] = _gelu_and_grad(y1)
    r['sg'] = _sigmoid(_dot(r['y2'], wg_ref[...]) + bg_ref[...])
    r['y3'] = r['y2'] * r['sg']
    r['y_ssm'] = r['y3'] * r['ss']
    r['y_pool'] = yp_ref[...] * r['sp']
    r['g'] = [_sigmoid(gl_ref[:, b * d:(b + 1) * d]) for b in range(3)]
    r['p'] = [_dot(r['y_att'], wba_ref[...]), _dot(r['y_ssm'], wbs_ref[...]), _dot(r['y_pool'], wbp_ref[...])]
    r['merged'] = r['g'][0] * r['p'][0] + r['g'][1] * r['p'][1] + r['g'][2] * r['p'][2]
    return r


def _merge_fwd(proj, o, y_lin, y_pool, x, ssm_d, w_glu, b_glu, wba, wbs, wbp, w_out, mod, sends=None):
    l, d = x.shape
    tr = min(ROWS_MERGE_FWD, l)

    def body(gl_ref, za_ref, us_ref, zs_ref, zp_ref, o_ref, yl_ref, yp_ref, d_ref, wg_ref, bg_ref, wba_ref, wbs_ref,
             wbp_ref, wo_ref, mod_ref, x_ref, xn_ref):
        r = _merge_forward_math(gl_ref, za_ref, us_ref, zs_ref, zp_ref, o_ref, yl_ref, yp_ref, d_ref, wg_ref, bg_ref,
                                wba_ref, wbs_ref, wbp_ref)
        xn_ref[...] = x_ref[...] + mod_ref[:, 2 * d:3 * d] * _dot(r['merged'], wo_ref[...])

    blk = pl.BlockSpec((tr, d), lambda i: (i, 0))
    in_specs = _merge_inputs(tr) + [blk]
    body, x_in, x_out, x_shapes, x_scratch, x_args = _carry_sends(body, len(in_specs), 1, 0, (l // tr,), sends)
    outs = pl.pallas_call(
        body, name="merge_fwd" if sends is None else "merge_fwd_sending", grid=(l // tr,), in_specs=in_specs + x_in,
        out_specs=[blk] + x_out, out_shape=[jax.ShapeDtypeStruct((l, d), F32)] + x_shapes, scratch_shapes=x_scratch,
        compiler_params=_params(("arbitrary",)))(
            proj, proj, proj, proj, proj, o, y_lin, y_pool, ssm_d, w_glu, b_glu, wba, wbs, wbp, w_out, mod, x, *x_args)
    return outs[0], list(outs[1:])


def _merge_bwd(proj, o, y_lin, y_pool, dxo, ssm_d, w_glu, b_glu, wba, wbs, wbp, w_out, mod, sends=None):
    l, d = dxo.shape
    tr = min(ROWS_MERGE, l)

    def body(gl_ref, za_ref, us_ref, zs_ref, zp_ref, o_ref, yl_ref, yp_ref, d_ref, wg_ref, bg_ref, wba_ref, wbs_ref,
             wbp_ref, wo_ref, mod_ref, dxo_ref,
             do_ref, dyl_ref, dus_ref, dyp_ref, dpj_ref,
             ya_ref, ys_ref, ypl_ref, y2_ref, dt_ref, dpa_ref, dps_ref, dpp_ref, mg_ref, dout_ref, st_ref):
        @pl.when(pl.program_id(0) == 0)
        def _():
            st_ref[...] = jnp.zeros_like(st_ref)

        for off, width in ((OFF_Q, ATT_W), (OFF_US, SSM_W), (OFF_UP, POOL_W), (OFF_KV, 2 * KV_W)):
            dpj_ref[:, off:off + width] = jnp.zeros((tr, width), BF16)

        r = _merge_forward_math(gl_ref, za_ref, us_ref, zs_ref, zp_ref, o_ref, yl_ref, yp_ref, d_ref, wg_ref, bg_ref,
                                wba_ref, wbs_ref, wbp_ref)
        dxov = dxo_ref[...]
        out = _dot(r['merged'], wo_ref[...])
        st_ref[0:1, :] += jnp.sum(dxov * out, axis=0, keepdims=True)
        dout = dxov * mod_ref[:, 2 * d:3 * d]
        dmerged = _dot(dout, wo_ref[...], _NT)
        dys = []
        for b, (w_ref, dp_ref) in enumerate(((wba_ref, dpa_ref), (wbs_ref, dps_ref), (wbp_ref, dpp_ref))):
            gb = r['g'][b]
            dpj_ref[:, OFF_GL + b * d:OFF_GL + (b + 1) * d] = (dmerged * r['p'][b] * gb * (1.0 - gb)).astype(BF16)
            dp = dmerged * gb
            dp_ref[...] = dp.astype(BF16)
            dys.append(_dot(dp, w_ref[...], _NT))
        mg_ref[...] = r['merged'].astype(BF16)
        dout_ref[...] = dout.astype(BF16)
        ya_ref[...] = r['y_att'].astype(BF16)
        ys_ref[...] = r['y_ssm'].astype(BF16)
        ypl_ref[...] = r['y_pool'].astype(BF16)
        y2_ref[...] = r['y2'].astype(BF16)
        do_ref[...] = (dys[0] * r['sa']).astype(BF16)
        dpj_ref[:, OFF_ZA:OFF_ZA + ATT_W] = (dys[0] * o_ref[...] * r['dsa']).astype(BF16)
        dy3 = dys[1] * r['ss']
        dpj_ref[:, OFF_ZS:OFF_ZS + SSM_W] = (dys[1] * r['y3'] * r['dss']).astype(BF16)
        dt = dy3 * r['y2'] * r['sg'] * (1.0 - r['sg'])
        dt_ref[...] = dt.astype(BF16)
        dy1 = (dy3 * r['sg'] + _dot(dt, wg_ref[...], _NT)) * r['dgelu']
        dyl_ref[...] = dy1.astype(BF16)
        dus_ref[...] = dy1 * d_ref[...]
        st_ref[1:2, 0:SSM_W] += jnp.sum(dt, axis=0, keepdims=True)
        st_ref[1:2, SSM_W:2 * SSM_W] += jnp.sum(dy1 * us_ref[...], axis=0, keepdims=True)
        dyp_ref[...] = dys[2] * r['sp']
        dpj_ref[:, OFF_ZP:OFF_ZP + POOL_W] = (dys[2] * yp_ref[...] * r['dsp']).astype(BF16)

    blk = pl.BlockSpec((tr, d), lambda i: (i, 0))
    half = pl.BlockSpec((tr, ATT_W), lambda i: (i, 0))
    wide = pl.BlockSpec((tr, IN_W), lambda i: (i, 0))
    sds = jax.ShapeDtypeStruct
    in_specs = _merge_inputs(tr) + [blk]
    out_specs = [half] * 4 + [wide] + [half] * 5 + [blk] * 5 + [_full((SUBLANES, d))]
    body, x_in, x_out, x_shapes, x_scratch, x_args = _carry_sends(body, len(in_specs), len(out_specs), 0, (l // tr,), sends)
    outs = pl.pallas_call(
        body, name="merge_bwd" if sends is None else "merge_bwd_sending", grid=(l // tr,), in_specs=in_specs + x_in,
        out_specs=out_specs + x_out,
        out_shape=[sds((l, ATT_W), t) for t in (BF16, BF16, F32, F32)] + [sds((l, IN_W), BF16)]
        + [sds((l, ATT_W), BF16)] * 5 + [sds((l, d), BF16)] * 5 + [sds((SUBLANES, d), F32)] + x_shapes,
        scratch_shapes=x_scratch, compiler_params=_params(("arbitrary",)))(
            proj, proj, proj, proj, proj, o, y_lin, y_pool, ssm_d, w_glu, b_glu, wba, wbs, wbp, w_out, mod, dxo, *x_args)
    return list(outs[:len(out_specs)]), list(outs[len(out_specs):])


def _ada_fwd(c_all, w_ada, b_shard):
    depth, d, n = w_ada.shape

    def body(c_ref, w_ref, b_ref, o_ref):
        act, _ = _silu_and_grad(c_ref[...])
        o_ref[0] = _dot(act, w_ref[0]) + b_ref[0]

    return pl.pallas_call(
        body, name="ada_fwd", grid=(depth,),
        in_specs=[_full((N_DEV, d)), pl.BlockSpec((1, d, n), lambda i: (i, 0, 0)), pl.BlockSpec((1, 1, n), lambda i: (i, 0, 0))],
        out_specs=pl.BlockSpec((1, N_DEV, n), lambda i: (i, 0, 0)), out_shape=jax.ShapeDtypeStruct((depth, N_DEV, n), F32),
        compiler_params=_params(("parallel",)))(c_all, w_ada, b_shard)


def _ada_bwd(c_all_t, dmod):
    d = c_all_t.shape[0]
    depth, _, n = dmod.shape

    def body(c_ref, dm_ref, o_ref):
        act, _ = _silu_and_grad(c_ref[...])
        acc = act[:, 0:1] * dm_ref[0, 0:1, :]
        for b in range(1, N_DEV):
            acc = acc + act[:, b:b + 1] * dm_ref[0, b:b + 1, :]
        o_ref[0] = acc

    return pl.pallas_call(
        body, name="ada_bwd", grid=(depth,),
        in_specs=[_full((d, N_DEV)), pl.BlockSpec((1, N_DEV, n), lambda i: (i, 0, 0))],
        out_specs=pl.BlockSpec((1, d, n), lambda i: (i, 0, 0)), out_shape=jax.ShapeDtypeStruct((depth, d, n), F32),
        compiler_params=_params(("parallel",)))(c_all_t, dmod)


ROWS_FLAT = 256


_RELATION_XOR = (0, 2, 1, 3)


def _sum_chips(own, others, chip_order, name):
    r, c = own.shape
    tr = math.gcd(ROWS_FLAT, r)

    def body(own_ref, oth_ref, o_ref):
        terms = [own_ref[...].astype(F32)] + [oth_ref[k].astype(F32) for k in range(3)]
        if chip_order:
            chip = 2 * lax.axis_index("x") + lax.axis_index("y")
            by_chip = []
            for q in range(N_CHIPS):
                rel = jnp.bitwise_xor(chip, q)
                pick = terms[3]
                for k in (2, 1, 0):
                    pick = jnp.where(rel == _RELATION_XOR[k], terms[k], pick)
                by_chip.append(pick)
            terms = by_chip
        o_ref[...] = ((terms[0] + terms[1]) + terms[2]) + terms[3]

    return pl.pallas_call(
        body, name=name, grid=(r // tr,),
        in_specs=[pl.BlockSpec((tr, c), lambda i: (i, 0)), pl.BlockSpec((3, tr, c), lambda i: (0, i, 0))],
        out_specs=pl.BlockSpec((tr, c), lambda i: (i, 0)), out_shape=jax.ShapeDtypeStruct((r, c), F32),
        compiler_params=_params(("parallel",)))(own, others)


def _add2(a, b, out_dtype, name):
    shape = a.shape
    a, b = a.reshape(-1, shape[-1]), b.reshape(-1, shape[-1])
    r, c = a.shape
    tr = math.gcd(ROWS_FLAT, r)

    def body(a_ref, b_ref, o_ref):
        o_ref[...] = (a_ref[...] + b_ref[...]).astype(out_dtype)

    blk = pl.BlockSpec((tr, c), lambda i: (i, 0))
    return pl.pallas_call(
        body, name=name, grid=(r // tr,), in_specs=[blk, blk], out_specs=blk,
        out_shape=jax.ShapeDtypeStruct((r, c), out_dtype), compiler_params=_params(("parallel",)))(a, b).reshape(shape)


def _adamw(w, g, m, v, name):
    r, c = w.shape
    tr = math.gcd(ROWS_FLAT, r)

    def body(w_ref, g_ref, m_ref, v_ref, d_ref, nm_ref, nv_ref):
        gv = g_ref[...]
        mv = ADAM_B1 * m_ref[...] + (1.0 - ADAM_B1) * gv
        vv = ADAM_B2 * v_ref[...] + (1.0 - ADAM_B2) * (gv * gv)
        m_hat = mv / (1.0 - ADAM_B1 ** ADAM_STEP)
        v_hat = vv / (1.0 - ADAM_B2 ** ADAM_STEP)
        d_ref[...] = -ADAM_LR * (m_hat / (jnp.sqrt(v_hat) + ADAM_EPS) + ADAM_WD * w_ref[...])
        nm_ref[...] = mv
        nv_ref[...] = vv

    blk = pl.BlockSpec((tr, c), lambda i: (i, 0))
    return pl.pallas_call(
        body, name=name, grid=(r // tr,), in_specs=[blk] * 4, out_specs=[blk] * 3,
        out_shape=[jax.ShapeDtypeStruct((r, c), F32)] * 3, compiler_params=_params(("parallel",)))(w, g, m, v)


_GROUP_MASKS = {
    'xy': ((1, 0, 0), (0, 1, 0), (1, 1, 0)),
    'c': ((0, 0, 1),),
    'xyc': ((0, 0, 1), (0, 1, 0), (0, 1, 1), (1, 0, 0), (1, 0, 1), (1, 1, 0), (1, 1, 1)),
}


def _group_rank(group, pos):
    x, y, c = pos
    return {'xy': 2 * x + y, 'c': c, 'xyc': 4 * x + 2 * y + c}[group]


def _exchange(src, group, scatter, name):
    masks = _GROUP_MASKS[group]
    n = len(masks) + 1
    shape = src.shape[1:] if scatter else src.shape
    assert (not scatter) or src.shape[0] == n

    def body(src_ref, out_ref, send_sems, recv_sems, local_sem):
        me = (lax.axis_index("x"), lax.axis_index("y"), lax.axis_index("c"))
        my_rank = _group_rank(group, me)

        def mine_for(rank):
            return src_ref.at[rank] if scatter else src_ref

        local = pltpu.make_async_copy(mine_for(my_rank), out_ref.at[my_rank], local_sem)
        local.start()
        sends = []
        for k, mask in enumerate(masks):
            peer = tuple(1 - p if f else p for p, f in zip(me, mask))
            peer_rank = _group_rank(group, peer)
            send = pltpu.make_async_remote_copy(
                src_ref=mine_for(peer_rank), dst_ref=out_ref.at[my_rank], send_sem=send_sems.at[k],
                recv_sem=recv_sems.at[k], device_id=peer, device_id_type=pl.DeviceIdType.MESH)
            send.start()
            sends.append((send, peer, peer_rank))
        for k, (send, peer, peer_rank) in enumerate(sends):
            pltpu.make_async_remote_copy(
                src_ref=mine_for(peer_rank), dst_ref=out_ref.at[peer_rank], send_sem=send_sems.at[k],
                recv_sem=recv_sems.at[k], device_id=peer, device_id_type=pl.DeviceIdType.MESH).wait_recv()
        for send, _, _ in sends:
            send.wait_send()
        local.wait()

    return pl.pallas_call(
        body, name=name, in_specs=[pl.BlockSpec(memory_space=pl.ANY)], out_specs=pl.BlockSpec(memory_space=pl.ANY),
        out_shape=jax.ShapeDtypeStruct((n,) + tuple(shape), src.dtype),
        scratch_shapes=[pltpu.SemaphoreType.DMA((n - 1,)), pltpu.SemaphoreType.DMA((n - 1,)), pltpu.SemaphoreType.DMA(())],
    )(src)


CHUNK_BYTES = 1 << 20
MIN_CHUNK_ROWS = 64


def _row_parts(rows, cols, itemsize):
    n = 1
    while rows % (2 * n) == 0 and rows // (2 * n) >= MIN_CHUNK_ROWS and rows * cols * itemsize > n * CHUNK_BYTES:
        n *= 2
    return n


def _remote(src, dst, send_sem, recv_sem, peer):
    return pltpu.make_async_remote_copy(src_ref=src, dst_ref=dst, send_sem=send_sem, recv_sem=recv_sem,
                                        device_id=peer, device_id_type=pl.DeviceIdType.MESH)


def _start_rows(src, dst, send_sem, recv_sem, peer):
    rows, cols = src.shape
    n = _row_parts(rows, cols, jnp.dtype(src.dtype).itemsize)
    pr = rows // n
    for i in range(n):
        _remote(src.at[pl.ds(i * pr, pr), :], dst.at[pl.ds(i * pr, pr), :], send_sem, recv_sem, peer).start()


def _mesh_place():
    x, y, c = lax.axis_index("x"), lax.axis_index("y"), lax.axis_index("c")
    other_chips = ((1 - x, y), (x, 1 - y), (1 - x, 1 - y))
    return x, y, c, 2 * x + y, (x, y, 1 - c), other_chips


def _comm_call(body, name, ins, out_shapes, sem_counts):
    any_spec = pl.BlockSpec(memory_space=pl.ANY)
    return pl.pallas_call(
        body, name=name, in_specs=[any_spec] * len(ins), out_specs=[any_spec] * len(out_shapes), out_shape=out_shapes,
        scratch_shapes=[pltpu.SemaphoreType.DMA((n,)) for n in sem_counts])(*ins)


class _Sends:
    def run(self, name):
        nt = len(self.srcs)

        def body(*refs):
            ins, outs = refs[:nt], refs[nt:2 * nt]
            self.start(ins, outs, *refs[2 * nt:])
            self.finish(ins, outs, *refs[2 * nt:])

        return _comm_call(body, name, self.srcs, self.out_shapes, (self.n_sems, self.n_sems))

    def specs(self):
        any_spec = pl.BlockSpec(memory_space=pl.ANY)
        return [any_spec] * len(self.srcs), [any_spec] * len(self.srcs), [pltpu.SemaphoreType.DMA((self.n_sems,))] * 2


class _SiblingSends(_Sends):
    def __init__(self, srcs):
        self.srcs = list(srcs)
        self.n_sems = len(self.srcs)
        self.out_shapes = [jax.ShapeDtypeStruct(g.shape[1:], g.dtype) for g in self.srcs]

    def start(self, ins, outs, send_sems, recv_sems):
        _, _, c, _, sibling, _ = _mesh_place()
        for t in range(len(ins)):
            for q in range(ins[t].shape[1]):
                _start_rows(ins[t].at[1 - c, q], outs[t].at[q], send_sems.at[t], recv_sems.at[t], sibling)

    def finish(self, ins, outs, send_sems, recv_sems):
        _, _, _, _, sibling, _ = _mesh_place()
        for t in range(len(ins)):
            arrived = _remote(outs[t], outs[t], send_sems.at[t], recv_sems.at[t], sibling)
            arrived.wait_recv()
            arrived.wait_send()


class _ChipSends(_Sends):
    def __init__(self, srcs, pick):
        self.srcs, self.pick = list(srcs), pick
        nt = len(self.srcs)
        self.n_sems = 3 * nt
        self.out_shapes = []
        for s in self.srcs:
            r2 = s.shape[-2] // 2 if pick is _pick_weight_half else s.shape[-2]
            self.out_shapes.append(jax.ShapeDtypeStruct((3, r2, s.shape[-1]), s.dtype))

    def start(self, ins, outs, send_sems, recv_sems):
        _, _, c, _, _, other_chips = _mesh_place()
        for t in range(len(ins)):
            for j, (px, py) in enumerate(other_chips):
                _remote(self.pick(ins[t], c, 2 * px + py), outs[t].at[j], send_sems.at[3 * t + j], recv_sems.at[3 * t + j],
                        (px, py, c)).start()

    def finish(self, ins, outs, send_sems, recv_sems):
        _, _, c, _, _, other_chips = _mesh_place()
        for t in range(len(ins)):
            for j, (px, py) in enumerate(other_chips):
                landed = _remote(outs[t].at[j], outs[t].at[j], send_sems.at[3 * t + j], recv_sems.at[3 * t + j], (px, py, c))
                landed.wait_recv()
                landed.wait_send()


def _pick_weight_half(src, core, chip):
    r2 = src.shape[0] // 2
    return src.at[pl.ds(pl.multiple_of(core * r2, MIN_CHUNK_ROWS), r2), :]


def _pick_slab(src, core, chip):
    return src.at[chip if src.shape[0] == N_CHIPS else 0]


def _swap_cores(halves):
    nt = len(halves)

    def body(*refs):
        ins, outs = refs[:nt], refs[nt:2 * nt]
        send_sems, recv_sems = refs[2 * nt:]
        _, _, _, _, sibling, _ = _mesh_place()
        for t in range(nt):
            for li in range(ins[t].shape[0]):
                _start_rows(ins[t].at[li], outs[t].at[li], send_sems.at[t], recv_sems.at[t], sibling)
        for t in range(nt):
            _remote(outs[t], outs[t], send_sems.at[t], recv_sems.at[t], sibling).wait_recv()
        for t in range(nt):
            _remote(ins[t], ins[t], send_sems.at[t], recv_sems.at[t], sibling).wait_send()

    out_shapes = [jax.ShapeDtypeStruct(h.shape, h.dtype) for h in halves]
    return _comm_call(body, "swap_cores", halves, out_shapes, (nt, nt))


def _pack_rows(pieces, cols, dtype):
    tile = SUBLANES * cols
    rows = []
    for p in pieces:
        flat = p.reshape(-1).astype(dtype)
        rows.append(jnp.pad(flat, (0, (-flat.shape[0]) % tile)).reshape(-1, cols))
    total = sum(r.shape[0] for r in rows)
    if total % (2 * SUBLANES):
        rows.append(jnp.zeros((SUBLANES, cols), dtype))
    return jnp.concatenate(rows, axis=0)


def _unpack_rows(buf, shapes):
    out, row = [], 0
    for s in shapes:
        n = math.prod(s)
        nrows = -(-n // (SUBLANES * buf.shape[1])) * SUBLANES
        out.append(buf[row:row + nrows].reshape(-1)[:n].reshape(s))
        row += nrows
    return out


ROW_SHARDED = ('w_out', 'w_glu')


def _chips_to_full(name, chips):
    mats = [c.reshape(-1, c.shape[-1]) for c in chips]
    if name in ROW_SHARDED:
        return jnp.concatenate(mats, axis=0)
    if name != 'w_in':
        return jnp.concatenate(mats, axis=1)
    cs = mats[0].shape[1]
    pieces = []
    for start, size in _PERM_PIECES:
        lo = start
        while lo < start + size:
            q = lo // cs
            hi = min(start + size, (q + 1) * cs)
            pieces.append(mats[q][:, lo - q * cs:hi - q * cs])
            lo = hi
    return jnp.concatenate(pieces, axis=1)


def _full_to_dests(name, full):
    rows, cols = full.shape
    if name in ROW_SHARDED:
        return jnp.transpose(full.reshape(N_CHIPS, 2, rows // (2 * N_CHIPS), cols), (1, 0, 2, 3))
    cs = cols // N_CHIPS
    if name != 'w_in':
        dests = [full[:, q * cs:(q + 1) * cs] for q in range(N_CHIPS)]
    else:
        offsets, off = [], 0
        for _, size in _PERM_PIECES:
            offsets.append(off)
            off += size
        by_start = sorted(zip(_PERM_PIECES, offsets))
        dests = []
        for q in range(N_CHIPS):
            parts = []
            for (start, size), at in by_start:
                lo, hi = max(start, q * cs), min(start + size, (q + 1) * cs)
                if lo < hi:
                    parts.append(full[:, at + lo - start:at + hi - start])
            dests.append(jnp.concatenate(parts, axis=1))
    return jnp.stack([d.reshape(2, rows // 2, cs) for d in dests], axis=1)


def _by_chip(own, related, x, y):
    grid = ((own, related[1]), (related[0], related[2]))
    along_x = [[jnp.where(x == 0, grid[px][dy], grid[1 - px][dy]) for dy in range(2)] for px in range(2)]
    return [jnp.where(y == 0, along_x[px][py], along_x[px][1 - py]) for px in range(2) for py in range(2)]


def _join_halves(mine, theirs, core, axis):
    return jnp.where(core == 0, jnp.concatenate([mine, theirs], axis=axis), jnp.concatenate([theirs, mine], axis=axis))


ROWS_WGRAD = 2048
SSM_RAW = ('ssm_a_re', 'ssm_a_im', 'ssm_log_dt', 'ssm_b_re', 'ssm_b_im', 'ssm_c_re', 'ssm_c_im')


def _all_tables(p):
    raw = tuple(p[k] for k in SSM_RAW)
    (_, bb, cb), vjp = jax.vjp(jax.vmap(_ssm_tables), *raw)
    return (bb.astype(BF16), cb.astype(BF16), jax.vmap(_ssm_powers)(*raw[:3])), vjp


def _layer_fwd(x, p, w_in, rest, li, mod, tables, sends=None, late_sends=None):
    bb, cb, pw = (t[li] for t in tables)
    h = _norm_fwd(x, _row(p['norm_g'][li]), mod)
    if isinstance(rest, dict):
        proj = _matmul(h, w_in, 'nn', tm=512, tn=1280, tk=1024, name="proj", n_outer=True)
    else:
        proj, arrived = _matmul(h, w_in, 'nn', tm=512, tn=1280, tk=1024, name="proj_sending", n_outer=True, sends=rest[0])
        rest = rest[1](arrived)
    full = dict(rest, w_in=w_in)
    o = _attention_fwd(proj, p['attn_sinks'][li])
    y_lin, states, brought = _ssm_fwd(proj, bb, cb, pw, sends)
    y_pool = _pool_fwd(proj, p['w_pool'][li].astype(BF16), _row(p['pool_scale'][li]))
    x_new, brought_late = _merge_fwd(proj, o, y_lin, y_pool, x, _row(p['ssm_d'][li]), full['w_glu'],
                                     _row(p['b_glu'][li]), full['w_br_att'], full['w_br_ssm'], full['w_br_pool'],
                                     full['w_out'], mod, late_sends)
    saved = dict(x=x, h=h, proj=proj, o=o, y_lin=y_lin, y_pool=y_pool, states=states, bb=bb, cb=cb, pw=pw, full=full)
    return x_new, saved, brought + brought_late


def _layer_bwd(dxo, s, p, li, mod, core, upper=None, own_early=(), w_in_sender=None):
    l = dxo.shape[0]
    proj, full = s['proj'], s['full']
    outs, arrived_upper = _merge_bwd(
        proj, s['o'], s['y_lin'], s['y_pool'], dxo, _row(p['ssm_d'][li]), full['w_glu'], _row(p['b_glu'][li]),
        full['w_br_att'], full['w_br_ssm'], full['w_br_pool'], full['w_out'], mod,
        None if upper is None else _ChipSends(upper, _pick_slab))
    (do, dyl, dus_skip, dyp, dproj, ya, ys, ypl, y2, dt, dpa, dps, dpp, mg, dout, st) = outs
    g = {}
    g['w_br_att'] = _matmul(ya, dpa, 'tn', tm=512, tn=1024, tk=ROWS_WGRAD, name="grad_w_br")
    g['w_br_ssm'] = _matmul(ys, dps, 'tn', tm=512, tn=1024, tk=ROWS_WGRAD, name="grad_w_br")
    g['w_br_pool'] = _matmul(ypl, dpp, 'tn', tm=512, tn=1024, tk=ROWS_WGRAD, name="grad_w_br")
    g['w_out'] = _matmul(mg, dout, 'tn', tm=512, tn=1024, tk=ROWS_WGRAD, name="grad_w_out")
    g['w_glu'] = _matmul(y2, dt, 'tn', tm=512, tn=512, tk=ROWS_WGRAD, name="grad_w_glu")
    g['b_glu'] = st[1, 0:SSM_W]
    g['ssm_d'] = st[1, SSM_W:2 * SSM_W]
    dgate = st[0]

    early = _layer_grad_halves(g, own_early)
    dq, dkv, dsink, theirs = _attention_bwd(proj, p['attn_sinks'][li], do, _SiblingSends(early) if early else None)
    g['attn_sinks'] = dsink[:, 0]
    pair = _pair_sums(early, [BF16] * len(early), core, theirs) if early else []
    sends = _ChipSends(pair, _pick_slab) if pair else None
    dus_scan, dbb, dcb, dlam, brought = _ssm_bwd(proj, s['states'], dyl, s['bb'], s['cb'], s['pw'], sends)
    g['ssm_tables'] = (dlam[:, 0:2, :], dbb, dcb)
    dup, dwp, dps_scale = _pool_bwd(proj, p['w_pool'][li].astype(BF16), _row(p['pool_scale'][li]), dyp)
    g['w_pool'] = dwp
    g['pool_scale'] = dps_scale[0]

    for off, piece in ((OFF_Q, dq), (OFF_US, dus_skip + dus_scan), (OFF_UP, dup), (OFF_KV, dkv[HALO:])):
        dproj = lax.dynamic_update_slice(dproj, piece.astype(BF16), (0, off))
    g['w_in'] = _matmul(s['h'], dproj, 'tn', tm=1024, tn=1280, tk=ROWS_WGRAD, name="grad_w_in")
    if w_in_sender is None:
        dh, brought_late = _matmul(dproj, full['w_in'], 'nt', tm=1024, tn=1024, tk=1280, name="grad_h"), None
    else:
        dh, brought_late = _matmul(dproj, full['w_in'], 'nt', tm=1024, tn=1024, tk=1280, name="grad_h_sending",
                                   sends=w_in_sender(g['w_in']))
    dx, nst = _norm_bwd(s['x'], dh, dxo, _row(p['norm_g'][li]), mod)
    g['norm_g'] = nst[2]
    dmod = jnp.concatenate([nst[0], nst[1], dgate])
    del l
    return dx, g, dmod, (pair, brought), arrived_upper, brought_late


BIG_NAMES = tuple(name for name, _ in BIG)


def _weight_sends(shards, names, li):
    return _ChipSends([shards[name][li] for name in names], _pick_weight_half)


def _assemble_layer(shards, names, li, over_ici, place):
    x, y, core, _ = place
    from_sibling = _swap_cores(over_ici)
    full = {}
    for name, a, b in zip(names, over_ici, from_sibling):
        own = shards[name][li]
        own_halves = own.reshape(2, own.shape[0] // 2, own.shape[1])
        halves = [_join_halves(a[j][None], b[j][None], core, 0) for j in range(3)]
        full[name] = _chips_to_full(name, _by_chip(own_halves, halves, x, y))
    return full


def _pair_sums(tensors, wire, core, theirs=None):
    if theirs is None:
        theirs = _SiblingSends(tensors).run("swap_halves")
    return [_add2(lax.dynamic_index_in_dim(g, core, 0, keepdims=False), a, dt, "sum_core_pair")
            for g, a, dt in zip(tensors, theirs, wire)]


def _layer_grad_halves(layer_grads, names):
    return [_full_to_dests(name, layer_grads[name]) for name in names]


def _step(p, m, v, x, c, target):
    depth = p['norm_g'].shape[0]
    d = D_MODEL
    ix, iy, ic = lax.axis_index("x"), lax.axis_index("y"), lax.axis_index("c")
    chip = 2 * ix + iy
    dev = 4 * ix + 2 * iy + ic
    x0 = x[0]

    c_pad = jnp.pad(c, ((0, SUBLANES - 1), (0, 0)))
    c_all = _exchange(c_pad, 'xyc', False, "gather_c")[:, 0, :]
    n_ada = p['w_ada'].shape[-1]
    b_shard = lax.dynamic_slice_in_dim(p['b_ada'], chip * n_ada, n_ada, axis=1)[:, None, :]
    mod_shard = _ada_fwd(c_all, p['w_ada'].astype(BF16), b_shard)
    mod_all = _exchange(mod_shard.reshape(depth * N_DEV, n_ada), 'xy', False, "gather_mod")
    mod_all = jnp.transpose(mod_all.reshape(N_CHIPS, depth, N_DEV, n_ada), (1, 2, 0, 3)).reshape(depth, N_DEV, 3 * d)
    mods = lax.dynamic_index_in_dim(mod_all, dev, axis=1, keepdims=True)

    place = (ix, iy, ic, chip)
    n_big = len(BIG)

    shards = {name: p[name].astype(BF16) for name, _ in BIG}
    tables, tables_vjp = _all_tables(p)
    others = BIG_NAMES[1:]
    w_in = _assemble_layer(shards, BIG_NAMES[:1], 0, _weight_sends(shards, BIG_NAMES[:1], 0).run("send_weight_halves"),
                           place)['w_in']
    rest = (_weight_sends(shards, others, 0), lambda arrived: _assemble_layer(shards, others, 0, arrived, place))
    saved = []
    xs = x0
    for li in range(depth):
        more = li + 1 < depth
        xs, s, arrived = _layer_fwd(xs, p, w_in, rest, li, mods[li], tables,
                                    _weight_sends(shards, BIG_NAMES[:1], li + 1) if more else None,
                                    _weight_sends(shards, others, li + 1) if more else None)
        saved.append(s)
        if more:
            rest = _assemble_layer(shards, BIG_NAMES, li + 1, arrived, place)
            w_in = rest.pop('w_in')
    dx, hst = _loss_head(xs, _row(p['final_g']), target[0])
    loss = lax.psum(hst[1, 0], ("x", "y", "c"))

    grads, dmods, pair, from_chips = [None] * depth, [None] * depth, [None] * depth, [None] * depth
    w_in_halves, w_in_pair, w_in_arrived = {}, {}, {}

    def w_in_sender(li):
        def to_chips(g_w_in):
            w_in_pair[li] = _pair_sums([_full_to_dests('w_in', g_w_in)], [BF16], ic)
            return _ChipSends(w_in_pair[li], _pick_slab)

        def to_sibling(g_w_in):
            w_in_halves[li] = [_full_to_dests('w_in', g_w_in)]
            return _SiblingSends(w_in_halves[li])

        return to_chips if li == 0 else to_sibling

    upper = None
    for li in reversed(range(depth)):
        dx, grads[li], dmods[li], early, arrived_upper, arrived_late = _layer_bwd(
            dx, saved[li], p, li, mods[li], ic, upper, others, w_in_sender(li))
        if upper is not None:
            w_in_arrived[li + 1] = arrived_upper
        if li > 0:
            w_in_pair[li] = _pair_sums(w_in_halves[li], [BF16], ic, arrived_late)
            upper = w_in_pair[li]
        else:
            w_in_arrived[li] = arrived_late
        pair[li] = w_in_pair[li] + early[0]
        from_chips[li] = early[1]
    for li in range(depth):
        from_chips[li] = list(w_in_arrived[li]) + list(from_chips[li])

    dmod_pad = jnp.pad(jnp.stack(dmods), ((0, SUBLANES - depth), (0, 0)))
    dmod_all = _exchange(dmod_pad, 'xyc', False, "gather_dmod")[:, :depth, :]
    dmod_cols = lax.dynamic_slice_in_dim(jnp.transpose(dmod_all, (1, 0, 2)), chip * n_ada, n_ada, axis=2)
    g_w_ada = _ada_bwd(jnp.transpose(c_all), dmod_cols)

    local_small = {k: jnp.stack([grads[li][k] for li in range(depth)])
                   for k in SMALL if k not in ('final_g', 'b_ada') + SSM_RAW}
    table_cotangents = tuple(jnp.stack([grads[li]['ssm_tables'][k] for li in range(depth)]) for k in range(3))
    local_small.update(zip(SSM_RAW, tables_vjp(table_cotangents)))
    local_small['final_g'] = hst[0]
    local_small['b_ada'] = jnp.stack(dmods)
    small_pack = _pack_rows([local_small[k] for k in SMALL], PACK_COLS, F32)

    small_halves = small_pack.reshape(2, 1, small_pack.shape[0] // 2, small_pack.shape[1])
    small_pair = _pair_sums([small_halves], [F32], ic)
    pair[0] = pair[0] + small_pair
    from_chips[0] = from_chips[0] + list(_ChipSends(small_pair, _pick_slab).run("scatter_chips"))
    halves = []
    for li in range(depth):
        for t, (ps, others) in enumerate(zip(pair[li], from_chips[li])):
            small = t == n_big
            own = ps[0] if small else lax.dynamic_index_in_dim(ps, chip, 0, keepdims=False)
            halves.append(_sum_chips(own, others, small, "sum_chips_small" if small else "sum_chips")[None])
    totals = [_join_halves(h[0], o[0], ic, 0) for h, o in zip(halves, _swap_cores(halves))]
    small_sum = totals[n_big]
    per_layer = [totals[:n_big]] + [totals[n_big + 1 + (li - 1) * n_big:n_big + 1 + li * n_big] for li in range(1, depth)]

    small_shapes = [p[k].shape for k in SMALL]
    grad = dict(zip(SMALL, _unpack_rows(small_sum, small_shapes)))
    grad['w_ada'] = g_w_ada
    for t, (name, _) in enumerate(BIG):
        grad[name] = jnp.stack([per_layer[li][t] for li in range(depth)])

    delta, new_m, new_v = {}, {}, {}
    outs = _adamw(_pack_rows([p[k] for k in SMALL], PACK_COLS, F32), small_sum,
                  _pack_rows([m[k] for k in SMALL], PACK_COLS, F32), _pack_rows([v[k] for k in SMALL], PACK_COLS, F32),
                  name="adamw_small")
    for res, o in zip((delta, new_m, new_v), outs):
        res.update(zip(SMALL, _unpack_rows(o, small_shapes)))
    for name in ['w_ada'] + [n for n, _ in BIG]:
        shape = p[name].shape
        two_d = (-1, shape[-1])
        outs = _adamw(p[name].reshape(two_d), grad[name].reshape(two_d), m[name].reshape(two_d), v[name].reshape(two_d),
                      name="adamw_" + name)
        delta[name], new_m[name], new_v[name] = (o.reshape(shape) for o in outs)

    return (loss, dx[None], *[grad[k] for k in WEIGHTS], *[delta[k] for k in WEIGHTS],
            *[new_m[k] for k in WEIGHTS], *[new_v[k] for k in WEIGHTS])


def kernel(x, c, norm_g, w_ada, b_ada, w_in, attn_sinks, ssm_a_re, ssm_a_im, ssm_log_dt, ssm_b_re, ssm_b_im, ssm_c_re, ssm_c_im, ssm_d, w_glu, b_glu, w_pool, pool_scale, w_br_att, w_br_ssm, w_br_pool, w_out, final_g, loss_target, m_norm_g, m_w_ada, m_b_ada, m_w_in, m_attn_sinks, m_ssm_a_re, m_ssm_a_im, m_ssm_log_dt, m_ssm_b_re, m_ssm_b_im, m_ssm_c_re, m_ssm_c_im, m_ssm_d, m_w_glu, m_b_glu, m_w_pool, m_pool_scale, m_w_br_att, m_w_br_ssm, m_w_br_pool, m_w_out, m_final_g, v_norm_g, v_w_ada, v_b_ada, v_w_in, v_attn_sinks, v_ssm_a_re, v_ssm_a_im, v_ssm_log_dt, v_ssm_b_re, v_ssm_b_im, v_ssm_c_re, v_ssm_c_im, v_ssm_d, v_w_glu, v_b_glu, v_w_pool, v_pool_scale, v_w_br_att, v_w_br_ssm, v_w_br_pool, v_w_out, v_final_g):
    p = dict(zip(WEIGHTS, (norm_g, w_ada, b_ada, w_in, attn_sinks, ssm_a_re, ssm_a_im, ssm_log_dt, ssm_b_re, ssm_b_im,
                           ssm_c_re, ssm_c_im, ssm_d, w_glu, b_glu, w_pool, pool_scale, w_br_att, w_br_ssm, w_br_pool,
                           w_out, final_g)))
    m = dict(zip(WEIGHTS, (m_norm_g, m_w_ada, m_b_ada, m_w_in, m_attn_sinks, m_ssm_a_re, m_ssm_a_im, m_ssm_log_dt,
                           m_ssm_b_re, m_ssm_b_im, m_ssm_c_re, m_ssm_c_im, m_ssm_d, m_w_glu, m_b_glu, m_w_pool,
                           m_pool_scale, m_w_br_att, m_w_br_ssm, m_w_br_pool, m_w_out, m_final_g)))
    v = dict(zip(WEIGHTS, (v_norm_g, v_w_ada, v_b_ada, v_w_in, v_attn_sinks, v_ssm_a_re, v_ssm_a_im, v_ssm_log_dt,
                           v_ssm_b_re, v_ssm_b_im, v_ssm_c_re, v_ssm_c_im, v_ssm_d, v_w_glu, v_b_glu, v_w_pool,
                           v_pool_scale, v_w_br_att, v_w_br_ssm, v_w_br_pool, v_w_out, v_final_g)))
    return _step(p, m, v, x, c, loss_target)
```

```python
import functools
import math

import jax
import jax.numpy as jnp
from jax import lax
from jax.experimental import pallas as pl
from jax.experimental.pallas import tpu as pltpu

F32 = jnp.float32
BF16 = jnp.bfloat16

D_MODEL = 1024
CHUNK = 64
N_HEADS = 8
N_KV_HEADS = 2
HEAD_DIM = 64
Q_PER_KV = N_HEADS // N_KV_HEADS
HALO = 128
ATT_W = 512
KV_W = 128
SSM_W = 512
SSM_GROUP = 16
SSM_GROUPS = 32
SSM_STATE = 64
POOL_W = 512
POOL_GW = 128
IN_W = 6400
EPS = 1e-6
NEG_INF = -1e30
ADAM_LR = 0.001
ADAM_B1 = 0.9
ADAM_B2 = 0.999
ADAM_EPS = 1e-08
ADAM_WD = 0.01
ADAM_STEP = 10

OFF_GL, OFF_Q, OFF_ZA, OFF_US, OFF_ZS, OFF_UP, OFF_ZP, OFF_KV = 0, 3072, 3584, 4096, 4608, 5120, 5632, 6144
_PERM_PIECES = ((3328, 3072), (0, 512), (1792, 512), (768, 512), (2304, 512), (1280, 512), (2816, 512), (512, 256))

LANES = 128
SUBLANES = 8
N_SBLK = SSM_GROUPS * SSM_STATE // LANES
VMEM_LIMIT = 48 * 1024 * 1024

N_CHIPS = 4
N_DEV = 8

WEIGHTS = ['norm_g', 'w_ada', 'b_ada', 'w_in', 'attn_sinks', 'ssm_a_re', 'ssm_a_im', 'ssm_log_dt', 'ssm_b_re',
           'ssm_b_im', 'ssm_c_re', 'ssm_c_im', 'ssm_d', 'w_glu', 'b_glu', 'w_pool', 'pool_scale', 'w_br_att',
           'w_br_ssm', 'w_br_pool', 'w_out', 'final_g']
SMALL = ['norm_g', 'b_ada', 'attn_sinks', 'ssm_a_re', 'ssm_a_im', 'ssm_log_dt', 'ssm_b_re', 'ssm_b_im', 'ssm_c_re',
         'ssm_c_im', 'ssm_d', 'b_glu', 'w_pool', 'pool_scale', 'final_g']
BIG = (('w_in', (1024, 1600)), ('w_br_att', (512, 256)), ('w_br_ssm', (512, 256)), ('w_br_pool', (512, 256)),
       ('w_out', (256, 1024)), ('w_glu', (128, 512)))
PACK_COLS = 1024


def _params(sem=None):
    return pltpu.CompilerParams(dimension_semantics=sem, vmem_limit_bytes=VMEM_LIMIT)


def _row(v):
    return v.reshape(1, -1)


def _full(shape):
    nd = len(shape)
    return pl.BlockSpec(shape, lambda *_: (0,) * nd)


def _sigmoid(v):
    return 1.0 / (1.0 + jnp.exp(-v))


def _silu_and_grad(z):
    s = _sigmoid(z)
    return z * s, s * (1.0 + z * (1.0 - s))


_GELU_K = math.sqrt(2.0 / math.pi)


def _gelu_and_grad(v):
    inner = _GELU_K * (v + 0.044715 * v * v * v)
    th = jnp.tanh(inner)
    val = 0.5 * v * (1.0 + th)
    grad = 0.5 * (1.0 + th) + 0.5 * v * (1.0 - th * th) * _GELU_K * (1.0 + 3 * 0.044715 * v * v)
    return val, grad


_NN = (((1,), (0,)), ((), ()))
_NT = (((1,), (1,)), ((), ()))
_TN = (((0,), (0,)), ((), ()))


def _dot(a, b, dims=_NN):
    return lax.dot_general(a.astype(BF16), b.astype(BF16), dims, preferred_element_type=F32)


def _matmul(a, b, mode, *, tm, tn, tk, name, out_dtype=F32, n_outer=False, sends=None):
    if mode == 'nn':
        (m, k), (_, n) = a.shape, b.shape
    elif mode == 'nt':
        (m, k), (n, _) = a.shape, b.shape
    else:
        (k, m), (_, n) = a.shape, b.shape
    tm, tn, tk = min(tm, m), min(tn, n), min(tk, k)
    assert m % tm == 0 and n % tn == 0 and k % tk == 0, (name, a.shape, b.shape)
    nk = k // tk
    dims = {'nn': _NN, 'nt': _NT, 'tn': _TN}[mode]

    def body(a_ref, b_ref, o_ref, acc_ref):
        if nk == 1:
            o_ref[...] = _dot(a_ref[...], b_ref[...], dims).astype(out_dtype)
            return
        kk = pl.program_id(2)

        @pl.when(kk == 0)
        def _():
            acc_ref[...] = jnp.zeros_like(acc_ref)

        acc_ref[...] += _dot(a_ref[...], b_ref[...], dims)

        @pl.when(kk == nk - 1)
        def _():
            o_ref[...] = acc_ref[...].astype(out_dtype)

    def spec(shape, index):
        if n_outer:
            return pl.BlockSpec(shape, lambda j, i, kk: index(i, j, kk))
        return pl.BlockSpec(shape, index)

    a_spec = spec((tk, tm), lambda i, j, kk: (kk, i)) if mode == 'tn' else spec((tm, tk), lambda i, j, kk: (i, kk))
    b_spec = spec((tn, tk), lambda i, j, kk: (j, kk)) if mode == 'nt' else spec((tk, tn), lambda i, j, kk: (kk, j))
    grid = (n // tn, m // tm, nk) if n_outer else (m // tm, n // tn, nk)
    body, x_in, x_out, x_shapes, x_scratch, x_args = _carry_sends(body, 2, 1, 1, grid, sends)
    semantics = ("parallel", "parallel", "arbitrary") if sends is None else ("arbitrary",) * 3
    outs = pl.pallas_call(
        body, name=name, grid=grid, in_specs=[a_spec, b_spec] + x_in,
        out_specs=[spec((tm, tn), lambda i, j, kk: (i, j))] + x_out,
        out_shape=[jax.ShapeDtypeStruct((m, n), out_dtype)] + x_shapes,
        scratch_shapes=[pltpu.VMEM((tm, tn), F32)] + x_scratch, compiler_params=_params(semantics))(a, b, *x_args)
    return outs[0] if sends is None else (outs[0], list(outs[1:]))


ROWS_NORM = 512


def _norm_fwd(x, g, mod):
    l, d = x.shape
    tr = min(ROWS_NORM, l)

    def body(x_ref, g_ref, mod_ref, h_ref):
        xv = x_ref[...]
        r = lax.rsqrt(jnp.mean(xv * xv, axis=-1, keepdims=True) + EPS)
        shift, scale = mod_ref[:, 0:d], mod_ref[:, d:2 * d]
        h_ref[...] = ((xv * r * g_ref[...]) * (1.0 + scale) + shift).astype(BF16)

    return pl.pallas_call(
        body, name="norm_fwd", grid=(l // tr,),
        in_specs=[pl.BlockSpec((tr, d), lambda i: (i, 0)), _full((1, d)), _full((1, 3 * d))],
        out_specs=pl.BlockSpec((tr, d), lambda i: (i, 0)), out_shape=jax.ShapeDtypeStruct((l, d), BF16),
        compiler_params=_params(("parallel",)))(x, g, mod)


def _norm_bwd(x, dh, dxo, g, mod):
    l, d = x.shape
    tr = min(ROWS_NORM, l)

    def body(x_ref, dh_ref, dxo_ref, g_ref, mod_ref, dx_ref, st_ref):
        @pl.when(pl.program_id(0) == 0)
        def _():
            st_ref[...] = jnp.zeros_like(st_ref)

        xv, dhv = x_ref[...], dh_ref[...]
        r = lax.rsqrt(jnp.mean(xv * xv, axis=-1, keepdims=True) + EPS)
        xn = xv * r
        gv = g_ref[...]
        sc1 = 1.0 + mod_ref[:, d:2 * d]
        dxn = dhv * gv * sc1
        dx_ref[...] = dxo_ref[...] + r * (dxn - xn * jnp.mean(dxn * xn, axis=-1, keepdims=True))
        st_ref[0:1, :] += jnp.sum(dhv, axis=0, keepdims=True)
        st_ref[1:2, :] += jnp.sum(dhv * xn * gv, axis=0, keepdims=True)
        st_ref[2:3, :] += jnp.sum(dhv * xn * sc1, axis=0, keepdims=True)

    blk = pl.BlockSpec((tr, d), lambda i: (i, 0))
    return pl.pallas_call(
        body, name="norm_bwd", grid=(l // tr,),
        in_specs=[blk, blk, blk, _full((1, d)), _full((1, 3 * d))],
        out_specs=[blk, _full((SUBLANES, d))],
        out_shape=[jax.ShapeDtypeStruct((l, d), F32), jax.ShapeDtypeStruct((SUBLANES, d), F32)],
        compiler_params=_params(("arbitrary",)))(x, dh, dxo, g, mod)


def _loss_head(x, g, target):
    l, d = x.shape
    tr = min(ROWS_NORM, l)

    def body(x_ref, g_ref, t_ref, dx_ref, st_ref):
        @pl.when(pl.program_id(0) == 0)
        def _():
            st_ref[...] = jnp.zeros_like(st_ref)

        xv = x_ref[...]
        r = lax.rsqrt(jnp.mean(xv * xv, axis=-1, keepdims=True) + EPS)
        xn = xv * r
        gv = g_ref[...]
        err = xn * gv - t_ref[...]
        part = 0.5 * jnp.sum(jnp.mean(err * err, axis=-1, keepdims=True), axis=0, keepdims=True)
        dy = err * (1.0 / d)
        dxn = dy * gv
        dx_ref[...] = r * (dxn - xn * jnp.mean(dxn * xn, axis=-1, keepdims=True))
        st_ref[0:1, :] += jnp.sum(dy * xn, axis=0, keepdims=True)
        st_ref[1:2, :] += jnp.broadcast_to(part, (1, d))

    blk = pl.BlockSpec((tr, d), lambda i: (i, 0))
    return pl.pallas_call(
        body, name="loss_head", grid=(l // tr,), in_specs=[blk, _full((1, d)), blk],
        out_specs=[blk, _full((SUBLANES, d))],
        out_shape=[jax.ShapeDtypeStruct((l, d), F32), jax.ShapeDtypeStruct((SUBLANES, d), F32)],
        compiler_params=_params(("arbitrary",)))(x, g, target)


ROWS_ATT = 128
ROWS_ATT_BWD = 256
_SLOPES = tuple(2.0 ** (-8.0 * (h + 1) / N_HEADS) for h in range(N_HEADS))


def _att_mask(i, t):
    r = lax.broadcasted_iota(jnp.int32, (t, t + HALO), 0)
    j = lax.broadcasted_iota(jnp.int32, (t, t + HALO), 1)
    dist = jnp.abs(r + HALO - j).astype(F32)
    rc, jc = r // CHUNK, j // CHUNK
    allowed = (jc >= rc) & (jc <= rc + 2) & ((j >= HALO) | (i > 0))
    return dist, allowed


def _att_probs(qh, k, dist, allowed, slope, sink):
    s = _dot(qh, k, _NT) * (1.0 / math.sqrt(HEAD_DIM)) - slope * dist
    s = jnp.where(allowed, s, NEG_INF)
    m = jnp.maximum(jnp.max(s, axis=1, keepdims=True), sink)
    e = jnp.exp(s - m)
    es = jnp.exp(sink - m)
    den = jnp.sum(e, axis=1, keepdims=True) + es
    return e / den, es / den


def _att_specs(t):
    q_spec = pl.BlockSpec((t, ATT_W), lambda i: (i, OFF_Q // ATT_W))
    kv_spec = pl.BlockSpec((t, 2 * KV_W), lambda i: (i, OFF_KV // (2 * KV_W)))
    halo_spec = pl.BlockSpec((HALO, 2 * KV_W), lambda i: (jnp.maximum(i * (t // HALO) - 1, 0), OFF_KV // (2 * KV_W)))
    return q_spec, kv_spec, halo_spec


def _attention_fwd(proj, sinks):
    l = proj.shape[0]
    t = min(ROWS_ATT, l)

    def body(sink_ref, q_ref, kv_ref, halo_ref, o_ref):
        dist, allowed = _att_mask(pl.program_id(0), t)
        kv = jnp.concatenate([halo_ref[...], kv_ref[...]], axis=0)
        for h in range(N_HEADS):
            kh = h // Q_PER_KV
            k = kv[:, kh * HEAD_DIM:(kh + 1) * HEAD_DIM]
            v = kv[:, KV_W + kh * HEAD_DIM:KV_W + (kh + 1) * HEAD_DIM]
            p, _ = _att_probs(q_ref[:, h * HEAD_DIM:(h + 1) * HEAD_DIM], k, dist, allowed, _SLOPES[h], sink_ref[h])
            o_ref[:, h * HEAD_DIM:(h + 1) * HEAD_DIM] = _dot(p, v)

    q_spec, kv_spec, halo_spec = _att_specs(t)
    return pl.pallas_call(
        body, name="attention_fwd", grid=(l // t,),
        in_specs=[pl.BlockSpec(memory_space=pltpu.SMEM), q_spec, kv_spec, halo_spec],
        out_specs=pl.BlockSpec((t, ATT_W), lambda i: (i, 0)), out_shape=jax.ShapeDtypeStruct((l, ATT_W), F32),
        compiler_params=_params(("parallel",)))(sinks, proj, proj, proj)


def _attention_bwd(proj, sinks, do, sends=None):
    l = proj.shape[0]
    t = min(ROWS_ATT_BWD, l)

    def body(sink_ref, q_ref, kv_ref, halo_ref, do_ref, dq_ref, dkv_ref, dsink_ref):
        i = pl.program_id(0)

        @pl.when(i == 0)
        def _():
            dkv_ref[...] = jnp.zeros_like(dkv_ref)
            dsink_ref[...] = jnp.zeros_like(dsink_ref)

        dist, allowed = _att_mask(i, t)
        kv = jnp.concatenate([halo_ref[...], kv_ref[...]], axis=0)
        rows = pl.ds(pl.multiple_of(i * t, t), t + HALO)
        for kh in range(N_KV_HEADS):
            k = kv[:, kh * HEAD_DIM:(kh + 1) * HEAD_DIM]
            v = kv[:, KV_W + kh * HEAD_DIM:KV_W + (kh + 1) * HEAD_DIM]
            dk = jnp.zeros((t + HALO, HEAD_DIM), F32)
            dv = jnp.zeros((t + HALO, HEAD_DIM), F32)
            for h in range(kh * Q_PER_KV, (kh + 1) * Q_PER_KV):
                qh = q_ref[:, h * HEAD_DIM:(h + 1) * HEAD_DIM]
                doh = do_ref[:, h * HEAD_DIM:(h + 1) * HEAD_DIM]
                p, ps = _att_probs(qh, k, dist, allowed, _SLOPES[h], sink_ref[h])
                dp = _dot(doh, v, _NT)
                delta = jnp.sum(p * dp, axis=1, keepdims=True)
                ds = p * (dp - delta) * (1.0 / math.sqrt(HEAD_DIM))
                dsink_ref[h:h + 1, :] += jnp.broadcast_to(-jnp.sum(ps * delta, axis=0, keepdims=True), (1, LANES))
                dq_ref[:, h * HEAD_DIM:(h + 1) * HEAD_DIM] = _dot(ds, k).astype(BF16)
                dk = dk + _dot(ds, qh, _TN)
                dv = dv + _dot(p, doh, _TN)
            dkv_ref[rows, kh * HEAD_DIM:(kh + 1) * HEAD_DIM] += dk
            dkv_ref[rows, KV_W + kh * HEAD_DIM:KV_W + (kh + 1) * HEAD_DIM] += dv

    q_spec, kv_spec, halo_spec = _att_specs(t)
    blk = pl.BlockSpec((t, ATT_W), lambda i: (i, 0))
    body, x_in, x_out, x_shapes, x_scratch, x_args = _carry_sends(body, 5, 3, 0, (l // t,), sends)
    outs = pl.pallas_call(
        body, name="attention_bwd" if sends is None else "attention_bwd_sending", grid=(l // t,),
        in_specs=[pl.BlockSpec(memory_space=pltpu.SMEM), q_spec, kv_spec, halo_spec, blk] + x_in,
        out_specs=[blk, _full((HALO + l, 2 * KV_W)), _full((N_HEADS, LANES))] + x_out,
        out_shape=[jax.ShapeDtypeStruct((l, ATT_W), BF16), jax.ShapeDtypeStruct((HALO + l, 2 * KV_W), F32),
                   jax.ShapeDtypeStruct((N_HEADS, LANES), F32)] + x_shapes,
        scratch_shapes=x_scratch, compiler_params=_params(("arbitrary",)))(sinks, proj, proj, proj, do, *x_args)
    return outs[0], outs[1], outs[2], list(outs[3:])


ROWS_SSM = 2048
SSM_BLOCKS = 2
SCAN_UNROLL = 2
SSM_STEPS_PER_U = LANES // (SSM_BLOCKS * 2 * SSM_GROUP)


def _ssm_discretize(a_re, a_im, log_dt, b_re, b_im):
    lam = lax.complex(a_re, a_im)
    dt = jnp.exp(log_dt)[:, None]
    lam_bar = jnp.exp(lam * dt)
    b_bar = ((lam_bar - 1.0) / lam)[..., None] * lax.complex(b_re, b_im)
    return lam, dt, lam_bar, b_bar


def _ssm_block_diag(m):
    e = m.reshape(N_SBLK, 2, SSM_GROUP, SSM_STATE)
    e = e[:, :, :, None, :] * jnp.eye(2, dtype=m.dtype)[None, :, None, :, None]
    e = e.reshape(N_SBLK, 2 * SSM_GROUP, LANES)
    oh = jax.nn.one_hot(jnp.arange(N_SBLK) % 4, 4, dtype=m.dtype)
    return (oh[:, :, None, None] * e[:, None]).reshape(N_SBLK, LANES, LANES)


def _ssm_tables(a_re, a_im, log_dt, b_re, b_im, c_re, c_im):
    _, _, lam_bar, b_bar = _ssm_discretize(a_re, a_im, log_dt, b_re, b_im)
    lam_blk = jnp.stack([jnp.real(lam_bar).reshape(N_SBLK, LANES), jnp.imag(lam_bar).reshape(N_SBLK, LANES)], axis=1)
    bt = jnp.transpose(b_bar, (0, 2, 1))
    bb = jnp.concatenate([_ssm_block_diag(jnp.real(bt)), _ssm_block_diag(jnp.imag(bt))], axis=2)
    cb = jnp.concatenate([jnp.transpose(_ssm_block_diag(c_re), (0, 2, 1)),
                          jnp.transpose(_ssm_block_diag(-c_im), (0, 2, 1))], axis=1)
    return lam_blk, bb, cb


def _ssm_powers(a_re, a_im, log_dt):
    lam = lax.complex(a_re, a_im)
    dt = jnp.exp(log_dt)[:, None]
    k = jnp.arange(1, SUBLANES + 1, dtype=F32)
    pw = jnp.exp((lam * dt)[None] * k[:, None, None]).reshape(SUBLANES, N_SBLK, LANES)
    pw = jnp.transpose(pw, (1, 0, 2))
    rev = pw[:, ::-1]
    return jnp.concatenate([jnp.real(pw), jnp.imag(pw), jnp.real(rev), jnp.imag(rev)], axis=1)


def _scan_consts(pw_ref, b, reverse):
    row = lax.broadcasted_iota(jnp.int32, (SUBLANES, LANES), 0)
    sign = -1.0 if reverse else 1.0

    def power(k):
        return (jnp.broadcast_to(pw_ref[b, k - 1:k, :], (SUBLANES, LANES)),
                sign * jnp.broadcast_to(pw_ref[b, SUBLANES + k - 1:SUBLANES + k, :], (SUBLANES, LANES)))

    steps = []
    for d in (1, 2, 4):
        pr, pi = power(d)
        keep = (row < SUBLANES - d) if reverse else (row >= d)
        steps.append((d, jnp.where(keep, pr, 0.0), jnp.where(keep, pi, 0.0)))
    base = 2 * SUBLANES if reverse else 0
    return steps, pw_ref[b, base:base + SUBLANES, :], sign * pw_ref[b, base + SUBLANES:base + 2 * SUBLANES, :]


def _scan_tile(xr, xi, steps, reverse):
    for d, ar, ai in steps:
        shift = SUBLANES - d if reverse else d
        rr, ri = pltpu.roll(xr, shift, 0), pltpu.roll(xi, shift, 0)
        xr, xi = xr + (ar * rr - ai * ri), xi + (ar * ri + ai * rr)
    return xr, xi


def _bcast_row(v, r):
    return jnp.broadcast_to(v[r:r + 1, :], (SUBLANES, LANES))


def _re_im(b):
    return slice(2 * b * LANES, (2 * b + 1) * LANES), slice((2 * b + 1) * LANES, (2 * b + 2) * LANES)


def _scan_forward(s_scr, pw_ref, n_tiles):
    consts = [_scan_consts(pw_ref, b, False) for b in range(SSM_BLOCKS)]

    def tiles(g, carry):
        carry = list(carry)
        rows = [pl.ds(pl.multiple_of((g * SCAN_UNROLL + u) * SUBLANES, SUBLANES), SUBLANES) for u in range(SCAN_UNROLL)]
        loaded = [[(s_scr[r, _re_im(b)[0]], s_scr[r, _re_im(b)[1]]) for b in range(SSM_BLOCKS)] for r in rows]
        local = [[_scan_tile(xr, xi, consts[b][0], False) for b, (xr, xi) in enumerate(per_row)] for per_row in loaded]
        for r, per_row in zip(rows, local):
            for b, (xr, xi) in enumerate(per_row):
                _, pr, pi = consts[b]
                cr, ci = carry[b]
                sr, si = xr + (pr * cr - pi * ci), xi + (pr * ci + pi * cr)
                s_scr[r, _re_im(b)[0]] = sr
                s_scr[r, _re_im(b)[1]] = si
                carry[b] = (_bcast_row(sr, SUBLANES - 1), _bcast_row(si, SUBLANES - 1))
        return tuple(carry)

    zero = jnp.zeros((SUBLANES, LANES), F32)
    lax.fori_loop(0, n_tiles // SCAN_UNROLL, tiles, ((zero, zero),) * SSM_BLOCKS)


def _row_chunks(l):
    rc = min(ROWS_SSM, l)
    return rc, l // rc


def _carry_sends(body, n_in, n_out, n_scratch, grid, sends):
    if sends is None:
        return body, [], [], [], [], []
    k = len(sends.srcs)

    def carrying(*refs):
        ins, send_ins = refs[:n_in], refs[n_in:n_in + k]
        outs, send_outs = refs[n_in + k:n_in + k + n_out], refs[n_in + k + n_out:n_in + 2 * k + n_out]
        rest = refs[n_in + 2 * k + n_out:]
        scratch, sems = rest[:n_scratch], rest[n_scratch:]
        ids = [pl.program_id(axis) for axis in range(len(grid))]
        first = functools.reduce(jnp.logical_and, [i == 0 for i in ids])
        last = functools.reduce(jnp.logical_and, [i == n - 1 for i, n in zip(ids, grid)])

        @pl.when(first)
        def _():
            sends.start(send_ins, send_outs, *sems)

        body(*ins, *outs, *scratch)

        @pl.when(last)
        def _():
            sends.finish(send_ins, send_outs, *sems)

    in_specs, out_specs, scratch = sends.specs()
    return carrying, in_specs, out_specs, sends.out_shapes, scratch, sends.srcs


def _ssm_fwd(proj, bb, cb, pw, sends=None):
    l = proj.shape[0]
    rc, n_chunks = _row_chunks(l)

    def body(u_ref, bb_ref, cb_ref, pw_ref, y_ref, s_scr):
        j = pl.program_id(0)

        def fill(ci, _):
            rows = pl.ds(pl.multiple_of(ci * rc, rc), rc)
            uv = u_ref[rows, :]
            for b in range(SSM_BLOCKS):
                s_scr[rows, 2 * b * LANES:2 * (b + 1) * LANES] = _dot(uv, bb_ref[b])
            return 0

        lax.fori_loop(0, n_chunks, fill, 0)
        _scan_forward(s_scr, pw_ref, l // SUBLANES)

        @pl.when(j % SSM_STEPS_PER_U == 0)
        def _():
            y_ref[...] = jnp.zeros_like(y_ref)

        def emit(ci, _):
            rows = pl.ds(pl.multiple_of(ci * rc, rc), rc)
            for b in range(SSM_BLOCKS):
                y_ref[rows, :] += _dot(s_scr[rows, 2 * b * LANES:2 * (b + 1) * LANES], cb_ref[b])
            return 0

        lax.fori_loop(0, n_chunks, emit, 0)

    steps = N_SBLK // SSM_BLOCKS
    body, x_in, x_out, x_shapes, x_scratch, x_args = _carry_sends(body, 4, 2, 0, (steps,), sends)
    outs = pl.pallas_call(
        body, name="ssm_fwd" if sends is None else "ssm_fwd_sending", grid=(steps,),
        in_specs=[pl.BlockSpec((l, LANES), lambda j: (0, OFF_US // LANES + j // SSM_STEPS_PER_U)),
                  pl.BlockSpec((SSM_BLOCKS, LANES, 2 * LANES), lambda j: (j, 0, 0)),
                  pl.BlockSpec((SSM_BLOCKS, 2 * LANES, LANES), lambda j: (j, 0, 0)),
                  pl.BlockSpec((SSM_BLOCKS, 4 * SUBLANES, LANES), lambda j: (j, 0, 0))] + x_in,
        out_specs=[pl.BlockSpec((l, LANES), lambda j: (0, j // SSM_STEPS_PER_U)),
                   pl.BlockSpec((l, SSM_BLOCKS * 2 * LANES), lambda j: (0, j))] + x_out,
        out_shape=[jax.ShapeDtypeStruct((l, SSM_W), F32), jax.ShapeDtypeStruct((l, N_SBLK * 2 * LANES), F32)] + x_shapes,
        scratch_shapes=x_scratch, compiler_params=_params(("arbitrary",)))(proj, bb, cb, pw, *x_args)
    return outs[0], outs[1], list(outs[2:])


def _ssm_bwd(proj, states, dy, bb, cb, pw, sends=None):
    l = proj.shape[0]
    rc, n_chunks = _row_chunks(l)
    n_tiles = l // SUBLANES

    def body(u_ref, s_scr, dy_ref, bb_ref, cb_ref, pw_ref, du_ref, dbb_ref, dcb_ref, dlam_ref, a_scr):
        j = pl.program_id(0)
        dcb_ref[...] = jnp.zeros_like(dcb_ref)

        def through_c(ci, _):
            rows = pl.ds(pl.multiple_of(ci * rc, rc), rc)
            dyv = dy_ref[rows, :]
            for b in range(SSM_BLOCKS):
                lanes = slice(2 * b * LANES, 2 * (b + 1) * LANES)
                dcb_ref[b] += _dot(s_scr[rows, lanes], dyv, _TN)
                a_scr[rows, lanes] = _dot(dyv, cb_ref[b], _NT)
            return 0

        lax.fori_loop(0, n_chunks, through_c, 0)

        consts = [_scan_consts(pw_ref, b, True) for b in range(SSM_BLOCKS)]
        row = lax.broadcasted_iota(jnp.int32, (SUBLANES, LANES), 0)
        last = row == SUBLANES - 1

        def tiles(g, carry):
            carry = list(carry)
            rows = [pl.ds(pl.multiple_of((n_tiles - 1 - (g * SCAN_UNROLL + u)) * SUBLANES, SUBLANES), SUBLANES)
                    for u in range(SCAN_UNROLL)]
            loaded = [[(a_scr[r, _re_im(b)[0]], a_scr[r, _re_im(b)[1]]) for b in range(SSM_BLOCKS)] for r in rows]
            states = [[(s_scr[r, _re_im(b)[0]], s_scr[r, _re_im(b)[1]]) for b in range(SSM_BLOCKS)] for r in rows]
            local = [[_scan_tile(xr, xi, consts[b][0], True) for b, (xr, xi) in enumerate(per_row)] for per_row in loaded]
            for r, per_row, state_row in zip(rows, local, states):
                for b, ((xr, xi), (sr, si)) in enumerate(zip(per_row, state_row)):
                    _, pr, pi = consts[b]
                    cr, ci, accr, acci = carry[b]
                    gr, gi = xr + (pr * cr - pi * ci), xi + (pr * ci + pi * cr)
                    a_scr[r, _re_im(b)[0]] = gr
                    a_scr[r, _re_im(b)[1]] = gi
                    ur = jnp.where(last, cr, pltpu.roll(gr, SUBLANES - 1, 0))
                    ui = jnp.where(last, ci, pltpu.roll(gi, SUBLANES - 1, 0))
                    carry[b] = (_bcast_row(gr, 0), _bcast_row(gi, 0),
                                accr + (ur * sr + ui * si), acci + (ui * sr - ur * si))
            return tuple(carry)

        zero = jnp.zeros((SUBLANES, LANES), F32)
        done = lax.fori_loop(0, n_tiles // SCAN_UNROLL, tiles, ((zero, zero, zero, zero),) * SSM_BLOCKS)
        for b in range(SSM_BLOCKS):
            dlr = jnp.broadcast_to(jnp.sum(done[b][2], axis=0, keepdims=True), (SUBLANES, LANES))
            dli = jnp.broadcast_to(jnp.sum(done[b][3], axis=0, keepdims=True), (SUBLANES, LANES))
            dlam_ref[b] = jnp.where(row == 0, dlr, jnp.where(row == 1, dli, 0.0))

        dbb_ref[...] = jnp.zeros_like(dbb_ref)

        @pl.when(j % SSM_STEPS_PER_U == 0)
        def _():
            du_ref[...] = jnp.zeros_like(du_ref)

        def through_b(ci, _):
            rows = pl.ds(pl.multiple_of(ci * rc, rc), rc)
            uv = u_ref[rows, :]
            for b in range(SSM_BLOCKS):
                av = a_scr[rows, 2 * b * LANES:2 * (b + 1) * LANES]
                dbb_ref[b] += _dot(uv, av, _TN)
                du_ref[rows, :] += _dot(av, bb_ref[b], _NT)
            return 0

        lax.fori_loop(0, n_chunks, through_b, 0)

    steps = N_SBLK // SSM_BLOCKS
    body, x_in, x_out, x_shapes, x_scratch, x_args = _carry_sends(body, 6, 4, 1, (steps,), sends)
    outs = pl.pallas_call(
        body, name="ssm_bwd" if sends is None else "ssm_bwd_sending", grid=(steps,),
        in_specs=[pl.BlockSpec((l, LANES), lambda j: (0, OFF_US // LANES + j // SSM_STEPS_PER_U)),
                  pl.BlockSpec((l, SSM_BLOCKS * 2 * LANES), lambda j: (0, j)),
                  pl.BlockSpec((l, LANES), lambda j: (0, j // SSM_STEPS_PER_U)),
                  pl.BlockSpec((SSM_BLOCKS, LANES, 2 * LANES), lambda j: (j, 0, 0)),
                  pl.BlockSpec((SSM_BLOCKS, 2 * LANES, LANES), lambda j: (j, 0, 0)),
                  pl.BlockSpec((SSM_BLOCKS, 4 * SUBLANES, LANES), lambda j: (j, 0, 0))] + x_in,
        out_specs=[pl.BlockSpec((l, LANES), lambda j: (0, j // SSM_STEPS_PER_U)),
                   pl.BlockSpec((SSM_BLOCKS, LANES, 2 * LANES), lambda j: (j, 0, 0)),
                   pl.BlockSpec((SSM_BLOCKS, 2 * LANES, LANES), lambda j: (j, 0, 0)),
                   pl.BlockSpec((SSM_BLOCKS, SUBLANES, LANES), lambda j: (j, 0, 0))] + x_out,
        out_shape=[jax.ShapeDtypeStruct((l, SSM_W), F32), jax.ShapeDtypeStruct((N_SBLK, LANES, 2 * LANES), F32),
                   jax.ShapeDtypeStruct((N_SBLK, 2 * LANES, LANES), F32),
                   jax.ShapeDtypeStruct((N_SBLK, SUBLANES, LANES), F32)] + x_shapes,
        scratch_shapes=[pltpu.VMEM((l, SSM_BLOCKS * 2 * LANES), F32)] + x_scratch,
        compiler_params=_params(("arbitrary",)))(proj, states, dy, bb, cb, pw, *x_args)
    return outs[0], outs[1], outs[2], outs[3], list(outs[4:])


def _pool_windows(v, t, l, ahead):
    def shifted(a, d):
        if ahead:
            return jnp.where(t < l - d, pltpu.roll(a, l - d, 0), 0.0)
        return jnp.where(t >= d, pltpu.roll(a, d, 0), 0.0)

    w2 = v + shifted(v, 1)
    w4 = w2 + shifted(w2, 2)
    w8 = w4 + shifted(w4, 4)
    w16 = w8 + shifted(w8, 8)
    return w2, w4, w8, w16


def _pool_select(g, ws):
    return jnp.where(g == 0, ws[0], jnp.where(g == 1, ws[1], jnp.where(g == 2, ws[2], ws[3])))


def _pool_count(g, t):
    return jnp.minimum(t + 1, jnp.left_shift(2, g)).astype(F32)


def _pool_specs(l):
    return [pl.BlockSpec((l, POOL_GW), lambda g: (0, OFF_UP // POOL_GW + g)),
            pl.BlockSpec((1, POOL_GW, POOL_GW), lambda g: (g, 0, 0)),
            pl.BlockSpec((1, POOL_GW), lambda g: (0, g))]


def _pool_fwd(proj, w_pool, scale):
    l = proj.shape[0]

    def body(u_ref, w_ref, sc_ref, y_ref):
        g = pl.program_id(0)
        t = lax.broadcasted_iota(jnp.int32, (l, 1), 0)
        u = u_ref[...]
        pooled = _pool_select(g, _pool_windows(u, t, l, False)) / _pool_count(g, t) - u
        y_ref[...] = _dot(pooled, w_ref[0]) * sc_ref[...]

    return pl.pallas_call(
        body, name="pool_fwd", grid=(4,), in_specs=_pool_specs(l),
        out_specs=pl.BlockSpec((l, POOL_GW), lambda g: (0, g)), out_shape=jax.ShapeDtypeStruct((l, POOL_W), F32),
        compiler_params=_params(("parallel",)))(proj, w_pool, scale)


def _pool_bwd(proj, w_pool, scale, dy):
    l = proj.shape[0]

    def body(u_ref, w_ref, sc_ref, dy_ref, du_ref, dw_ref, dsc_ref):
        g = pl.program_id(0)
        t = lax.broadcasted_iota(jnp.int32, (l, 1), 0)
        u = u_ref[...]
        cnt = _pool_count(g, t)
        pooled = _pool_select(g, _pool_windows(u, t, l, False)) / cnt - u
        dyv = dy_ref[...]
        dsc_ref[...] = jnp.sum(dyv * _dot(pooled, w_ref[0]), axis=0, keepdims=True)
        dyl = dyv * sc_ref[...]
        dw_ref[0] = _dot(pooled, dyl, _TN)
        dpooled = _dot(dyl, w_ref[0], _NT)
        du_ref[...] = (_pool_select(g, _pool_windows(dpooled / cnt, t, l, True)) - dpooled).astype(BF16)

    return pl.pallas_call(
        body, name="pool_bwd", grid=(4,), in_specs=_pool_specs(l) + [pl.BlockSpec((l, POOL_GW), lambda g: (0, g))],
        out_specs=[pl.BlockSpec((l, POOL_GW), lambda g: (0, g)), pl.BlockSpec((1, POOL_GW, POOL_GW), lambda g: (g, 0, 0)),
                   pl.BlockSpec((1, POOL_GW), lambda g: (0, g))],
        out_shape=[jax.ShapeDtypeStruct((l, POOL_W), BF16), jax.ShapeDtypeStruct((4, POOL_GW, POOL_GW), F32),
                   jax.ShapeDtypeStruct((1, POOL_W), F32)],
        compiler_params=_params(("parallel",)))(proj, w_pool, scale, dy)


ROWS_MERGE = 128
ROWS_MERGE_FWD = 256


def _merge_inputs(tr):
    def col(off, w):
        return pl.BlockSpec((tr, w), lambda i: (i, off // w))

    def act(w):
        return pl.BlockSpec((tr, w), lambda i: (i, 0))

    d = D_MODEL
    return ([col(OFF_GL, 3 * d), col(OFF_ZA, ATT_W), col(OFF_US, SSM_W), col(OFF_ZS, SSM_W), col(OFF_ZP, POOL_W),
             act(ATT_W), act(SSM_W), act(POOL_W)]
            + [_full((1, SSM_W)), _full((SSM_W, SSM_W)), _full((1, SSM_W)), _full((ATT_W, d)), _full((SSM_W, d)),
               _full((POOL_W, d)), _full((d, d)), _full((1, 3 * d))])


def _merge_forward_math(gl_ref, za_ref, us_ref, zs_ref, zp_ref, o_ref, yl_ref, yp_ref, d_ref, wg_ref, bg_ref,
                        wba_ref, wbs_ref, wbp_ref):
    d = D_MODEL
    r = {}
    r['sa'], r['dsa'] = _silu_and_grad(za_ref[...])
    r['ss'], r['dss'] = _silu_and_grad(zs_ref[...])
    r['sp'], r['dsp'] = _silu_and_grad(zp_ref[...])
    r['y_att'] = o_ref[...] * r['sa']
    y1 = yl_ref[...] + d_ref[...] * us_ref[...]
    r['y2'], r['dgelu'] = _gelu_and_grad(y1)
    r['sg'] = _sigmoid(_dot(r['y2'], wg_ref[...]) + bg_ref[...])
    r['y3'] = r['y2'] * r['sg']
    r['y_ssm'] = r['y3'] * r['ss']
    r['y_pool'] = yp_ref[...] * r['sp']
    r['g'] = [_sigmoid(gl_ref[:, b * d:(b + 1) * d]) for b in range(3)]
    r['p'] = [_dot(r['y_att'], wba_ref[...]), _dot(r['y_ssm'], wbs_ref[...]), _dot(r['y_pool'], wbp_ref[...])]
    r['merged'] = r['g'][0] * r['p'][0] + r['g'][1] * r['p'][1] + r['g'][2] * r['p'][2]
    return r


def _merge_fwd(proj, o, y_lin, y_pool, x, ssm_d, w_glu, b_glu, wba, wbs, wbp, w_out, mod, sends=None):
    l, d = x.shape
    tr = min(ROWS_MERGE_FWD, l)

    def body(gl_ref, za_ref, us_ref, zs_ref, zp_ref, o_ref, yl_ref, yp_ref, d_ref, wg_ref, bg_ref, wba_ref, wbs_ref,
             wbp_ref, wo_ref, mod_ref, x_ref, xn_ref):
        r = _merge_forward_math(gl_ref, za_ref, us_ref, zs_ref, zp_ref, o_ref, yl_ref, yp_ref, d_ref, wg_ref, bg_ref,
                                wba_ref, wbs_ref, wbp_ref)
        xn_ref[...] = x_ref[...] + mod_ref[:, 2 * d:3 * d] * _dot(r['merged'], wo_ref[...])

    blk = pl.BlockSpec((tr, d), lambda i: (i, 0))
    in_specs = _merge_inputs(tr) + [blk]
    body, x_in, x_out, x_shapes, x_scratch, x_args = _carry_sends(body, len(in_specs), 1, 0, (l // tr,), sends)
    outs = pl.pallas_call(
        body, name="merge_fwd" if sends is None else "merge_fwd_sending", grid=(l // tr,), in_specs=in_specs + x_in,
        out_specs=[blk] + x_out, out_shape=[jax.ShapeDtypeStruct((l, d), F32)] + x_shapes, scratch_shapes=x_scratch,
        compiler_params=_params(("arbitrary",)))(
            proj, proj, proj, proj, proj, o, y_lin, y_pool, ssm_d, w_glu, b_glu, wba, wbs, wbp, w_out, mod, x, *x_args)
    return outs[0], list(outs[1:])


def _merge_bwd(proj, o, y_lin, y_pool, dxo, ssm_d, w_glu, b_glu, wba, wbs, wbp, w_out, mod, sends=None):
    l, d = dxo.shape
    tr = min(ROWS_MERGE, l)

    def body(gl_ref, za_ref, us_ref, zs_ref, zp_ref, o_ref, yl_ref, yp_ref, d_ref, wg_ref, bg_ref, wba_ref, wbs_ref,
             wbp_ref, wo_ref, mod_ref, dxo_ref,
             do_ref, dyl_ref, dus_ref, dyp_ref, dpj_ref,
             ya_ref, ys_ref, ypl_ref, y2_ref, dt_ref, dpa_ref, dps_ref, dpp_ref, mg_ref, dout_ref, st_ref):
        @pl.when(pl.program_id(0) == 0)
        def _():
            st_ref[...] = jnp.zeros_like(st_ref)

        for off, width in ((OFF_Q, ATT_W), (OFF_US, SSM_W), (OFF_UP, POOL_W), (OFF_KV, 2 * KV_W)):
            dpj_ref[:, off:off + width] = jnp.zeros((tr, width), BF16)

        r = _merge_forward_math(gl_ref, za_ref, us_ref, zs_ref, zp_ref, o_ref, yl_ref, yp_ref, d_ref, wg_ref, bg_ref,
                                wba_ref, wbs_ref, wbp_ref)
        dxov = dxo_ref[...]
        out = _dot(r['merged'], wo_ref[...])
        st_ref[0:1, :] += jnp.sum(dxov * out, axis=0, keepdims=True)
        dout = dxov * mod_ref[:, 2 * d:3 * d]
        dmerged = _dot(dout, wo_ref[...], _NT)
        dys = []
        for b, (w_ref, dp_ref) in enumerate(((wba_ref, dpa_ref), (wbs_ref, dps_ref), (wbp_ref, dpp_ref))):
            gb = r['g'][b]
            dpj_ref[:, OFF_GL + b * d:OFF_GL + (b + 1) * d] = (dmerged * r['p'][b] * gb * (1.0 - gb)).astype(BF16)
            dp = dmerged * gb
            dp_ref[...] = dp.astype(BF16)
            dys.append(_dot(dp, w_ref[...], _NT))
        mg_ref[...] = r['merged'].astype(BF16)
        dout_ref[...] = dout.astype(BF16)
        ya_ref[...] = r['y_att'].astype(BF16)
        ys_ref[...] = r['y_ssm'].astype(BF16)
        ypl_ref[...] = r['y_pool'].astype(BF16)
        y2_ref[...] = r['y2'].astype(BF16)
        do_ref[...] = (dys[0] * r['sa']).astype(BF16)
        dpj_ref[:, OFF_ZA:OFF_ZA + ATT_W] = (dys[0] * o_ref[...] * r['dsa']).astype(BF16)
        dy3 = dys[1] * r['ss']
        dpj_ref[:, OFF_ZS:OFF_ZS + SSM_W] = (dys[1] * r['y3'] * r['dss']).astype(BF16)
        dt = dy3 * r['y2'] * r['sg'] * (1.0 - r['sg'])
        dt_ref[...] = dt.astype(BF16)
        dy1 = (dy3 * r['sg'] + _dot(dt, wg_ref[...], _NT)) * r['dgelu']
        dyl_ref[...] = dy1.astype(BF16)
        dus_ref[...] = dy1 * d_ref[...]
        st_ref[1:2, 0:SSM_W] += jnp.sum(dt, axis=0, keepdims=True)
        st_ref[1:2, SSM_W:2 * SSM_W] += jnp.sum(dy1 * us_ref[...], axis=0, keepdims=True)
        dyp_ref[...] = dys[2] * r['sp']
        dpj_ref[:, OFF_ZP:OFF_ZP + POOL_W] = (dys[2] * yp_ref[...] * r['dsp']).astype(BF16)

    blk = pl.BlockSpec((tr, d), lambda i: (i, 0))
    half = pl.BlockSpec((tr, ATT_W), lambda i: (i, 0))
    wide = pl.BlockSpec((tr, IN_W), lambda i: (i, 0))
    sds = jax.ShapeDtypeStruct
    in_specs = _merge_inputs(tr) + [blk]
    out_specs = [half] * 4 + [wide] + [half] * 5 + [blk] * 5 + [_full((SUBLANES, d))]
    body, x_in, x_out, x_shapes, x_scratch, x_args = _carry_sends(body, len(in_specs), len(out_specs), 0, (l // tr,), sends)
    outs = pl.pallas_call(
        body, name="merge_bwd" if sends is None else "merge_bwd_sending", grid=(l // tr,), in_specs=in_specs + x_in,
        out_specs=out_specs + x_out,
        out_shape=[sds((l, ATT_W), t) for t in (BF16, BF16, F32, F32)] + [sds((l, IN_W), BF16)]
        + [sds((l, ATT_W), BF16)] * 5 + [sds((l, d), BF16)] * 5 + [sds((SUBLANES, d), F32)] + x_shapes,
        scratch_shapes=x_scratch, compiler_params=_params(("arbitrary",)))(
            proj, proj, proj, proj, proj, o, y_lin, y_pool, ssm_d, w_glu, b_glu, wba, wbs, wbp, w_out, mod, dxo, *x_args)
    return list(outs[:len(out_specs)]), list(outs[len(out_specs):])


def _ada_fwd(c_all, w_ada, b_shard):
    depth, d, n = w_ada.shape

    def body(c_ref, w_ref, b_ref, o_ref):
        act, _ = _silu_and_grad(c_ref[...])
        o_ref[0] = _dot(act, w_ref[0]) + b_ref[0]

    return pl.pallas_call(
        body, name="ada_fwd", grid=(depth,),
        in_specs=[_full((N_DEV, d)), pl.BlockSpec((1, d, n), lambda i: (i, 0, 0)), pl.BlockSpec((1, 1, n), lambda i: (i, 0, 0))],
        out_specs=pl.BlockSpec((1, N_DEV, n), lambda i: (i, 0, 0)), out_shape=jax.ShapeDtypeStruct((depth, N_DEV, n), F32),
        compiler_params=_params(("parallel",)))(c_all, w_ada, b_shard)


def _ada_bwd(c_all_t, dmod):
    d = c_all_t.shape[0]
    depth, _, n = dmod.shape

    def body(c_ref, dm_ref, o_ref):
        act, _ = _silu_and_grad(c_ref[...])
        acc = act[:, 0:1] * dm_ref[0, 0:1, :]
        for b in range(1, N_DEV):
            acc = acc + act[:, b:b + 1] * dm_ref[0, b:b + 1, :]
        o_ref[0] = acc

    return pl.pallas_call(
        body, name="ada_bwd", grid=(depth,),
        in_specs=[_full((d, N_DEV)), pl.BlockSpec((1, N_DEV, n), lambda i: (i, 0, 0))],
        out_specs=pl.BlockSpec((1, d, n), lambda i: (i, 0, 0)), out_shape=jax.ShapeDtypeStruct((depth, d, n), F32),
        compiler_params=_params(("parallel",)))(c_all_t, dmod)


ROWS_FLAT = 256


_RELATION_XOR = (0, 2, 1, 3)


def _sum_chips(own, others, chip_order, name):
    r, c = own.shape
    tr = math.gcd(ROWS_FLAT, r)

    def body(own_ref, oth_ref, o_ref):
        terms = [own_ref[...].astype(F32)] + [oth_ref[k].astype(F32) for k in range(3)]
        if chip_order:
            chip = 2 * lax.axis_index("x") + lax.axis_index("y")
            by_chip = []
            for q in range(N_CHIPS):
                rel = jnp.bitwise_xor(chip, q)
                pick = terms[3]
                for k in (2, 1, 0):
                    pick = jnp.where(rel == _RELATION_XOR[k], terms[k], pick)
                by_chip.append(pick)
            terms = by_chip
        o_ref[...] = ((terms[0] + terms[1]) + terms[2]) + terms[3]

    return pl.pallas_call(
        body, name=name, grid=(r // tr,),
        in_specs=[pl.BlockSpec((tr, c), lambda i: (i, 0)), pl.BlockSpec((3, tr, c), lambda i: (0, i, 0))],
        out_specs=pl.BlockSpec((tr, c), lambda i: (i, 0)), out_shape=jax.ShapeDtypeStruct((r, c), F32),
        compiler_params=_params(("parallel",)))(own, others)


def _add2(a, b, out_dtype, name):
    shape = a.shape
    a, b = a.reshape(-1, shape[-1]), b.reshape(-1, shape[-1])
    r, c = a.shape
    tr = math.gcd(ROWS_FLAT, r)

    def body(a_ref, b_ref, o_ref):
        o_ref[...] = (a_ref[...] + b_ref[...]).astype(out_dtype)

    blk = pl.BlockSpec((tr, c), lambda i: (i, 0))
    return pl.pallas_call(
        body, name=name, grid=(r // tr,), in_specs=[blk, blk], out_specs=blk,
        out_shape=jax.ShapeDtypeStruct((r, c), out_dtype), compiler_params=_params(("parallel",)))(a, b).reshape(shape)


def _adamw(w, g, m, v, name):
    r, c = w.shape
    tr = math.gcd(ROWS_FLAT, r)

    def body(w_ref, g_ref, m_ref, v_ref, d_ref, nm_ref, nv_ref):
        gv = g_ref[...]
        mv = ADAM_B1 * m_ref[...] + (1.0 - ADAM_B1) * gv
        vv = ADAM_B2 * v_ref[...] + (1.0 - ADAM_B2) * (gv * gv)
        m_hat = mv / (1.0 - ADAM_B1 ** ADAM_STEP)
        v_hat = vv / (1.0 - ADAM_B2 ** ADAM_STEP)
        d_ref[...] = -ADAM_LR * (m_hat / (jnp.sqrt(v_hat) + ADAM_EPS) + ADAM_WD * w_ref[...])
        nm_ref[...] = mv
        nv_ref[...] = vv

    blk = pl.BlockSpec((tr, c), lambda i: (i, 0))
    return pl.pallas_call(
        body, name=name, grid=(r // tr,), in_specs=[blk] * 4, out_specs=[blk] * 3,
        out_shape=[jax.ShapeDtypeStruct((r, c), F32)] * 3, compiler_params=_params(("parallel",)))(w, g, m, v)


_GROUP_MASKS = {
    'xy': ((1, 0, 0), (0, 1, 0), (1, 1, 0)),
    'c': ((0, 0, 1),),
    'xyc': ((0, 0, 1), (0, 1, 0), (0, 1, 1), (1, 0, 0), (1, 0, 1), (1, 1, 0), (1, 1, 1)),
}


def _group_rank(group, pos):
    x, y, c = pos
    return {'xy': 2 * x + y, 'c': c, 'xyc': 4 * x + 2 * y + c}[group]


def _exchange(src, group, scatter, name):
    masks = _GROUP_MASKS[group]
    n = len(masks) + 1
    shape = src.shape[1:] if scatter else src.shape
    assert (not scatter) or src.shape[0] == n

    def body(src_ref, out_ref, send_sems, recv_sems, local_sem):
        me = (lax.axis_index("x"), lax.axis_index("y"), lax.axis_index("c"))
        my_rank = _group_rank(group, me)

        def mine_for(rank):
            return src_ref.at[rank] if scatter else src_ref

        local = pltpu.make_async_copy(mine_for(my_rank), out_ref.at[my_rank], local_sem)
        local.start()
        sends = []
        for k, mask in enumerate(masks):
            peer = tuple(1 - p if f else p for p, f in zip(me, mask))
            peer_rank = _group_rank(group, peer)
            send = pltpu.make_async_remote_copy(
                src_ref=mine_for(peer_rank), dst_ref=out_ref.at[my_rank], send_sem=send_sems.at[k],
                recv_sem=recv_sems.at[k], device_id=peer, device_id_type=pl.DeviceIdType.MESH)
            send.start()
            sends.append((send, peer, peer_rank))
        for k, (send, peer, peer_rank) in enumerate(sends):
            pltpu.make_async_remote_copy(
                src_ref=mine_for(peer_rank), dst_ref=out_ref.at[peer_rank], send_sem=send_sems.at[k],
                recv_sem=recv_sems.at[k], device_id=peer, device_id_type=pl.DeviceIdType.MESH).wait_recv()
        for send, _, _ in sends:
            send.wait_send()
        local.wait()

    return pl.pallas_call(
        body, name=name, in_specs=[pl.BlockSpec(memory_space=pl.ANY)], out_specs=pl.BlockSpec(memory_space=pl.ANY),
        out_shape=jax.ShapeDtypeStruct((n,) + tuple(shape), src.dtype),
        scratch_shapes=[pltpu.SemaphoreType.DMA((n - 1,)), pltpu.SemaphoreType.DMA((n - 1,)), pltpu.SemaphoreType.DMA(())],
    )(src)


CHUNK_BYTES = 1 << 20
MIN_CHUNK_ROWS = 64


def _row_parts(rows, cols, itemsize):
    n = 1
    while rows % (2 * n) == 0 and rows // (2 * n) >= MIN_CHUNK_ROWS and rows * cols * itemsize > n * CHUNK_BYTES:
        n *= 2
    return n


def _remote(src, dst, send_sem, recv_sem, peer):
    return pltpu.make_async_remote_copy(src_ref=src, dst_ref=dst, send_sem=send_sem, recv_sem=recv_sem,
                                        device_id=peer, device_id_type=pl.DeviceIdType.MESH)


def _start_rows(src, dst, send_sem, recv_sem, peer):
    rows, cols = src.shape
    n = _row_parts(rows, cols, jnp.dtype(src.dtype).itemsize)
    pr = rows // n
    for i in range(n):
        _remote(src.at[pl.ds(i * pr, pr), :], dst.at[pl.ds(i * pr, pr), :], send_sem, recv_sem, peer).start()


def _mesh_place():
    x, y, c = lax.axis_index("x"), lax.axis_index("y"), lax.axis_index("c")
    other_chips = ((1 - x, y), (x, 1 - y), (1 - x, 1 - y))
    return x, y, c, 2 * x + y, (x, y, 1 - c), other_chips


def _comm_call(body, name, ins, out_shapes, sem_counts):
    any_spec = pl.BlockSpec(memory_space=pl.ANY)
    return pl.pallas_call(
        body, name=name, in_specs=[any_spec] * len(ins), out_specs=[any_spec] * len(out_shapes), out_shape=out_shapes,
        scratch_shapes=[pltpu.SemaphoreType.DMA((n,)) for n in sem_counts])(*ins)


class _Sends:
    def run(self, name):
        nt = len(self.srcs)

        def body(*refs):
            ins, outs = refs[:nt], refs[nt:2 * nt]
            self.start(ins, outs, *refs[2 * nt:])
            self.finish(ins, outs, *refs[2 * nt:])

        return _comm_call(body, name, self.srcs, self.out_shapes, (self.n_sems, self.n_sems))

    def specs(self):
        any_spec = pl.BlockSpec(memory_space=pl.ANY)
        return [any_spec] * len(self.srcs), [any_spec] * len(self.srcs), [pltpu.SemaphoreType.DMA((self.n_sems,))] * 2


class _SiblingSends(_Sends):
    def __init__(self, srcs):
        self.srcs = list(srcs)
        self.n_sems = len(self.srcs)
        self.out_shapes = [jax.ShapeDtypeStruct(g.shape[1:], g.dtype) for g in self.srcs]

    def start(self, ins, outs, send_sems, recv_sems):
        _, _, c, _, sibling, _ = _mesh_place()
        for t in range(len(ins)):
            for q in range(ins[t].shape[1]):
                _start_rows(ins[t].at[1 - c, q], outs[t].at[q], send_sems.at[t], recv_sems.at[t], sibling)

    def finish(self, ins, outs, send_sems, recv_sems):
        _, _, _, _, sibling, _ = _mesh_place()
        for t in range(len(ins)):
            arrived = _remote(outs[t], outs[t], send_sems.at[t], recv_sems.at[t], sibling)
            arrived.wait_recv()
            arrived.wait_send()


class _ChipSends(_Sends):
    def __init__(self, srcs, pick):
        self.srcs, self.pick = list(srcs), pick
        nt = len(self.srcs)
        self.n_sems = 3 * nt
        self.out_shapes = []
        for s in self.srcs:
            r2 = s.shape[-2] // 2 if pick is _pick_weight_half else s.shape[-2]
            self.out_shapes.append(jax.ShapeDtypeStruct((3, r2, s.shape[-1]), s.dtype))

    def start(self, ins, outs, send_sems, recv_sems):
        _, _, c, _, _, other_chips = _mesh_place()
        for t in range(len(ins)):
            for j, (px, py) in enumerate(other_chips):
                _remote(self.pick(ins[t], c, 2 * px + py), outs[t].at[j], send_sems.at[3 * t + j], recv_sems.at[3 * t + j],
                        (px, py, c)).start()

    def finish(self, ins, outs, send_sems, recv_sems):
        _, _, c, _, _, other_chips = _mesh_place()
        for t in range(len(ins)):
            for j, (px, py) in enumerate(other_chips):
                landed = _remote(outs[t].at[j], outs[t].at[j], send_sems.at[3 * t + j], recv_sems.at[3 * t + j], (px, py, c))
                landed.wait_recv()
                landed.wait_send()


def _pick_weight_half(src, core, chip):
    r2 = src.shape[0] // 2
    return src.at[pl.ds(pl.multiple_of(core * r2, MIN_CHUNK_ROWS), r2), :]


def _pick_slab(src, core, chip):
    return src.at[chip if src.shape[0] == N_CHIPS else 0]


def _swap_cores(halves):
    nt = len(halves)

    def body(*refs):
        ins, outs = refs[:nt], refs[nt:2 * nt]
        send_sems, recv_sems = refs[2 * nt:]
        _, _, _, _, sibling, _ = _mesh_place()
        for t in range(nt):
            for li in range(ins[t].shape[0]):
                _start_rows(ins[t].at[li], outs[t].at[li], send_sems.at[t], recv_sems.at[t], sibling)
        for t in range(nt):
            _remote(outs[t], outs[t], send_sems.at[t], recv_sems.at[t], sibling).wait_recv()
        for t in range(nt):
            _remote(ins[t], ins[t], send_sems.at[t], recv_sems.at[t], sibling).wait_send()

    out_shapes = [jax.ShapeDtypeStruct(h.shape, h.dtype) for h in halves]
    return _comm_call(body, "swap_cores", halves, out_shapes, (nt, nt))


def _pack_rows(pieces, cols, dtype):
    tile = SUBLANES * cols
    rows = []
    for p in pieces:
        flat = p.reshape(-1).astype(dtype)
        rows.append(jnp.pad(flat, (0, (-flat.shape[0]) % tile)).reshape(-1, cols))
    total = sum(r.shape[0] for r in rows)
    if total % (2 * SUBLANES):
        rows.append(jnp.zeros((SUBLANES, cols), dtype))
    return jnp.concatenate(rows, axis=0)


def _unpack_rows(buf, shapes):
    out, row = [], 0
    for s in shapes:
        n = math.prod(s)
        nrows = -(-n // (SUBLANES * buf.shape[1])) * SUBLANES
        out.append(buf[row:row + nrows].reshape(-1)[:n].reshape(s))
        row += nrows
    return out


ROW_SHARDED = ('w_out', 'w_glu')


def _chips_to_full(name, chips):
    mats = [c.reshape(-1, c.shape[-1]) for c in chips]
    if name in ROW_SHARDED:
        return jnp.concatenate(mats, axis=0)
    if name != 'w_in':
        return jnp.concatenate(mats, axis=1)
    cs = mats[0].shape[1]
    pieces = []
    for start, size in _PERM_PIECES:
        lo = start
        while lo < start + size:
            q = lo // cs
            hi = min(start + size, (q + 1) * cs)
            pieces.append(mats[q][:, lo - q * cs:hi - q * cs])
            lo = hi
    return jnp.concatenate(pieces, axis=1)


def _full_to_dests(name, full):
    rows, cols = full.shape
    if name in ROW_SHARDED:
        return jnp.transpose(full.reshape(N_CHIPS, 2, rows // (2 * N_CHIPS), cols), (1, 0, 2, 3))
    cs = cols // N_CHIPS
    if name != 'w_in':
        dests = [full[:, q * cs:(q + 1) * cs] for q in range(N_CHIPS)]
    else:
        offsets, off = [], 0
        for _, size in _PERM_PIECES:
            offsets.append(off)
            off += size
        by_start = sorted(zip(_PERM_PIECES, offsets))
        dests = []
        for q in range(N_CHIPS):
            parts = []
            for (start, size), at in by_start:
                lo, hi = max(start, q * cs), min(start + size, (q + 1) * cs)
                if lo < hi:
                    parts.append(full[:, at + lo - start:at + hi - start])
            dests.append(jnp.concatenate(parts, axis=1))
    return jnp.stack([d.reshape(2, rows // 2, cs) for d in dests], axis=1)


def _by_chip(own, related, x, y):
    grid = ((own, related[1]), (related[0], related[2]))
    along_x = [[jnp.where(x == 0, grid[px][dy], grid[1 - px][dy]) for dy in range(2)] for px in range(2)]
    return [jnp.where(y == 0, along_x[px][py], along_x[px][1 - py]) for px in range(2) for py in range(2)]


def _join_halves(mine, theirs, core, axis):
    return jnp.where(core == 0, jnp.concatenate([mine, theirs], axis=axis), jnp.concatenate([theirs, mine], axis=axis))


ROWS_WGRAD = 2048
SSM_RAW = ('ssm_a_re', 'ssm_a_im', 'ssm_log_dt', 'ssm_b_re', 'ssm_b_im', 'ssm_c_re', 'ssm_c_im')


def _all_tables(p):
    raw = tuple(p[k] for k in SSM_RAW)
    (_, bb, cb), vjp = jax.vjp(jax.vmap(_ssm_tables), *raw)
    return (bb.astype(BF16), cb.astype(BF16), jax.vmap(_ssm_powers)(*raw[:3])), vjp


def _layer_fwd(x, p, w_in, rest, li, mod, tables, sends=None, late_sends=None):
    bb, cb, pw = (t[li] for t in tables)
    h = _norm_fwd(x, _row(p['norm_g'][li]), mod)
    if isinstance(rest, dict):
        proj = _matmul(h, w_in, 'nn', tm=1024, tn=1280, tk=1024, name="proj", n_outer=True)
    else:
        proj, arrived = _matmul(h, w_in, 'nn', tm=1024, tn=1280, tk=1024, name="proj_sending", n_outer=True, sends=rest[0])
        rest = rest[1](arrived)
    full = dict(rest, w_in=w_in)
    o = _attention_fwd(proj, p['attn_sinks'][li])
    y_lin, states, brought = _ssm_fwd(proj, bb, cb, pw, sends)
    y_pool = _pool_fwd(proj, p['w_pool'][li].astype(BF16), _row(p['pool_scale'][li]))
    x_new, brought_late = _merge_fwd(proj, o, y_lin, y_pool, x, _row(p['ssm_d'][li]), full['w_glu'],
                                     _row(p['b_glu'][li]), full['w_br_att'], full['w_br_ssm'], full['w_br_pool'],
                                     full['w_out'], mod, late_sends)
    saved = dict(x=x, h=h, proj=proj, o=o, y_lin=y_lin, y_pool=y_pool, states=states, bb=bb, cb=cb, pw=pw, full=full)
    return x_new, saved, brought + brought_late


def _layer_bwd(dxo, s, p, li, mod, core, upper=None, own_early=(), w_in_sender=None):
    l = dxo.shape[0]
    proj, full = s['proj'], s['full']
    outs, arrived_upper = _merge_bwd(
        proj, s['o'], s['y_lin'], s['y_pool'], dxo, _row(p['ssm_d'][li]), full['w_glu'], _row(p['b_glu'][li]),
        full['w_br_att'], full['w_br_ssm'], full['w_br_pool'], full['w_out'], mod,
        None if upper is None else _ChipSends(upper, _pick_slab))
    (do, dyl, dus_skip, dyp, dproj, ya, ys, ypl, y2, dt, dpa, dps, dpp, mg, dout, st) = outs
    g = {}
    g['w_br_att'] = _matmul(ya, dpa, 'tn', tm=512, tn=1024, tk=ROWS_WGRAD, name="grad_w_br")
    g['w_br_ssm'] = _matmul(ys, dps, 'tn', tm=512, tn=1024, tk=ROWS_WGRAD, name="grad_w_br")
    g['w_br_pool'] = _matmul(ypl, dpp, 'tn', tm=512, tn=1024, tk=ROWS_WGRAD, name="grad_w_br")
    g['w_out'] = _matmul(mg, dout, 'tn', tm=512, tn=1024, tk=ROWS_WGRAD, name="grad_w_out")
    g['w_glu'] = _matmul(y2, dt, 'tn', tm=512, tn=512, tk=ROWS_WGRAD, name="grad_w_glu")
    g['b_glu'] = st[1, 0:SSM_W]
    g['ssm_d'] = st[1, SSM_W:2 * SSM_W]
    dgate = st[0]

    early = _layer_grad_halves(g, own_early)
    dq, dkv, dsink, theirs = _attention_bwd(proj, p['attn_sinks'][li], do, _SiblingSends(early) if early else None)
    g['attn_sinks'] = dsink[:, 0]
    pair = _pair_sums(early, [BF16] * len(early), core, theirs) if early else []
    sends = _ChipSends(pair, _pick_slab) if pair else None
    dus_scan, dbb, dcb, dlam, brought = _ssm_bwd(proj, s['states'], dyl, s['bb'], s['cb'], s['pw'], sends)
    g['ssm_tables'] = (dlam[:, 0:2, :], dbb, dcb)
    dup, dwp, dps_scale = _pool_bwd(proj, p['w_pool'][li].astype(BF16), _row(p['pool_scale'][li]), dyp)
    g['w_pool'] = dwp
    g['pool_scale'] = dps_scale[0]

    for off, piece in ((OFF_Q, dq), (OFF_US, dus_skip + dus_scan), (OFF_UP, dup), (OFF_KV, dkv[HALO:])):
        dproj = lax.dynamic_update_slice(dproj, piece.astype(BF16), (0, off))
    g['w_in'] = _matmul(s['h'], dproj, 'tn', tm=1024, tn=1280, tk=ROWS_WGRAD, name="grad_w_in")
    if w_in_sender is None:
        dh, brought_late = _matmul(dproj, full['w_in'], 'nt', tm=1024, tn=1024, tk=3200, name="grad_h"), None
    else:
        dh, brought_late = _matmul(dproj, full['w_in'], 'nt', tm=1024, tn=1024, tk=3200, name="grad_h_sending",
                                   sends=w_in_sender(g['w_in']))
    dx, nst = _norm_bwd(s['x'], dh, dxo, _row(p['norm_g'][li]), mod)
    g['norm_g'] = nst[2]
    dmod = jnp.concatenate([nst[0], nst[1], dgate])
    del l
    return dx, g, dmod, (pair, brought), arrived_upper, brought_late


BIG_NAMES = tuple(name for name, _ in BIG)


def _weight_sends(shards, names, li):
    return _ChipSends([shards[name][li] for name in names], _pick_weight_half)


def _assemble_layer(shards, names, li, over_ici, place):
    x, y, core, _ = place
    from_sibling = _swap_cores(over_ici)
    full = {}
    for name, a, b in zip(names, over_ici, from_sibling):
        own = shards[name][li]
        own_halves = own.reshape(2, own.shape[0] // 2, own.shape[1])
        halves = [_join_halves(a[j][None], b[j][None], core, 0) for j in range(3)]
        full[name] = _chips_to_full(name, _by_chip(own_halves, halves, x, y))
    return full


def _pair_sums(tensors, wire, core, theirs=None):
    if theirs is None:
        theirs = _SiblingSends(tensors).run("swap_halves")
    return [_add2(lax.dynamic_index_in_dim(g, core, 0, keepdims=False), a, dt, "sum_core_pair")
            for g, a, dt in zip(tensors, theirs, wire)]


def _layer_grad_halves(layer_grads, names):
    return [_full_to_dests(name, layer_grads[name]) for name in names]


def _step(p, m, v, x, c, target):
    depth = p['norm_g'].shape[0]
    d = D_MODEL
    ix, iy, ic = lax.axis_index("x"), lax.axis_index("y"), lax.axis_index("c")
    chip = 2 * ix + iy
    dev = 4 * ix + 2 * iy + ic
    x0 = x[0]

    c_pad = jnp.pad(c, ((0, SUBLANES - 1), (0, 0)))
    c_all = _exchange(c_pad, 'xyc', False, "gather_c")[:, 0, :]
    n_ada = p['w_ada'].shape[-1]
    b_shard = lax.dynamic_slice_in_dim(p['b_ada'], chip * n_ada, n_ada, axis=1)[:, None, :]
    mod_shard = _ada_fwd(c_all, p['w_ada'].astype(BF16), b_shard)
    mod_all = _exchange(mod_shard.reshape(depth * N_DEV, n_ada), 'xy', False, "gather_mod")
    mod_all = jnp.transpose(mod_all.reshape(N_CHIPS, depth, N_DEV, n_ada), (1, 2, 0, 3)).reshape(depth, N_DEV, 3 * d)
    mods = lax.dynamic_index_in_dim(mod_all, dev, axis=1, keepdims=True)

    place = (ix, iy, ic, chip)
    n_big = len(BIG)

    shards = {name: p[name].astype(BF16) for name, _ in BIG}
    tables, tables_vjp = _all_tables(p)
    others = BIG_NAMES[1:]
    w_in = _assemble_layer(shards, BIG_NAMES[:1], 0, _weight_sends(shards, BIG_NAMES[:1], 0).run("send_weight_halves"),
                           place)['w_in']
    rest = (_weight_sends(shards, others, 0), lambda arrived: _assemble_layer(shards, others, 0, arrived, place))
    saved = []
    xs = x0
    for li in range(depth):
        more = li + 1 < depth
        xs, s, arrived = _layer_fwd(xs, p, w_in, rest, li, mods[li], tables,
                                    _weight_sends(shards, BIG_NAMES[:1], li + 1) if more else None,
                                    _weight_sends(shards, others, li + 1) if more else None)
        saved.append(s)
        if more:
            rest = _assemble_layer(shards, BIG_NAMES, li + 1, arrived, place)
            w_in = rest.pop('w_in')
    dx, hst = _loss_head(xs, _row(p['final_g']), target[0])
    loss = lax.psum(hst[1, 0], ("x", "y", "c"))

    grads, dmods, pair, from_chips = [None] * depth, [None] * depth, [None] * depth, [None] * depth
    w_in_halves, w_in_pair, w_in_arrived = {}, {}, {}

    def w_in_sender(li):
        def to_chips(g_w_in):
            w_in_pair[li] = _pair_sums([_full_to_dests('w_in', g_w_in)], [BF16], ic)
            return _ChipSends(w_in_pair[li], _pick_slab)

        def to_sibling(g_w_in):
            w_in_halves[li] = [_full_to_dests('w_in', g_w_in)]
            return _SiblingSends(w_in_halves[li])

        return to_chips if li == 0 else to_sibling

    upper = None
    for li in reversed(range(depth)):
        dx, grads[li], dmods[li], early, arrived_upper, arrived_late = _layer_bwd(
            dx, saved[li], p, li, mods[li], ic, upper, others, w_in_sender(li))
        if upper is not None:
            w_in_arrived[li + 1] = arrived_upper
        if li > 0:
            w_in_pair[li] = _pair_sums(w_in_halves[li], [BF16], ic, arrived_late)
            upper = w_in_pair[li]
        else:
            w_in_arrived[li] = arrived_late
        pair[li] = w_in_pair[li] + early[0]
        from_chips[li] = early[1]
    for li in range(depth):
        from_chips[li] = list(w_in_arrived[li]) + list(from_chips[li])

    dmod_pad = jnp.pad(jnp.stack(dmods), ((0, SUBLANES - depth), (0, 0)))
    dmod_all = _exchange(dmod_pad, 'xyc', False, "gather_dmod")[:, :depth, :]
    dmod_cols = lax.dynamic_slice_in_dim(jnp.transpose(dmod_all, (1, 0, 2)), chip * n_ada, n_ada, axis=2)
    g_w_ada = _ada_bwd(jnp.transpose(c_all), dmod_cols)

    local_small = {k: jnp.stack([grads[li][k] for li in range(depth)])
                   for k in SMALL if k not in ('final_g', 'b_ada') + SSM_RAW}
    table_cotangents = tuple(jnp.stack([grads[li]['ssm_tables'][k] for li in range(depth)]) for k in range(3))
    local_small.update(zip(SSM_RAW, tables_vjp(table_cotangents)))
    local_small['final_g'] = hst[0]
    local_small['b_ada'] = jnp.stack(dmods)
    small_pack = _pack_rows([local_small[k] for k in SMALL], PACK_COLS, F32)

    small_halves = small_pack.reshape(2, 1, small_pack.shape[0] // 2, small_pack.shape[1])
    small_pair = _pair_sums([small_halves], [F32], ic)
    pair[0] = pair[0] + small_pair
    from_chips[0] = from_chips[0] + list(_ChipSends(small_pair, _pick_slab).run("scatter_chips"))
    halves = []
    for li in range(depth):
        for t, (ps, others) in enumerate(zip(pair[li], from_chips[li])):
            small = t == n_big
            own = ps[0] if small else lax.dynamic_index_in_dim(ps, chip, 0, keepdims=False)
            halves.append(_sum_chips(own, others, small, "sum_chips_small" if small else "sum_chips")[None])
    totals = [_join_halves(h[0], o[0], ic, 0) for h, o in zip(halves, _swap_cores(halves))]
    small_sum = totals[n_big]
    per_layer = [totals[:n_big]] + [totals[n_big + 1 + (li - 1) * n_big:n_big + 1 + li * n_big] for li in range(1, depth)]

    small_shapes = [p[k].shape for k in SMALL]
    grad = dict(zip(SMALL, _unpack_rows(small_sum, small_shapes)))
    grad['w_ada'] = g_w_ada
    for t, (name, _) in enumerate(BIG):
        grad[name] = jnp.stack([per_layer[li][t] for li in range(depth)])

    delta, new_m, new_v = {}, {}, {}
    outs = _adamw(_pack_rows([p[k] for k in SMALL], PACK_COLS, F32), small_sum,
                  _pack_rows([m[k] for k in SMALL], PACK_COLS, F32), _pack_rows([v[k] for k in SMALL], PACK_COLS, F32),
                  name="adamw_small")
    for res, o in zip((delta, new_m, new_v), outs):
        res.update(zip(SMALL, _unpack_rows(o, small_shapes)))
    for name in ['w_ada'] + [n for n, _ in BIG]:
        shape = p[name].shape
        two_d = (-1, shape[-1])
        outs = _adamw(p[name].reshape(two_d), grad[name].reshape(two_d), m[name].reshape(two_d), v[name].reshape(two_d),
                      name="adamw_" + name)
        delta[name], new_m[name], new_v[name] = (o.reshape(shape) for o in outs)

    return (loss, dx[None], *[grad[k] for k in WEIGHTS], *[delta[k] for k in WEIGHTS],
            *[new_m[k] for k in WEIGHTS], *[new_v[k] for k in WEIGHTS])


def kernel(x, c, norm_g, w_ada, b_ada, w_in, attn_sinks, ssm_a_re, ssm_a_im, ssm_log_dt, ssm_b_re, ssm_b_im, ssm_c_re, ssm_c_im, ssm_d, w_glu, b_glu, w_pool, pool_scale, w_br_att, w_br_ssm, w_br_pool, w_out, final_g, loss_target, m_norm_g, m_w_ada, m_b_ada, m_w_in, m_attn_sinks, m_ssm_a_re, m_ssm_a_im, m_ssm_log_dt, m_ssm_b_re, m_ssm_b_im, m_ssm_c_re, m_ssm_c_im, m_ssm_d, m_w_glu, m_b_glu, m_w_pool, m_pool_scale, m_w_br_att, m_w_br_ssm, m_w_br_pool, m_w_out, m_final_g, v_norm_g, v_w_ada, v_b_ada, v_w_in, v_attn_sinks, v_ssm_a_re, v_ssm_a_im, v_ssm_log_dt, v_ssm_b_re, v_ssm_b_im, v_ssm_c_re, v_ssm_c_im, v_ssm_d, v_w_glu, v_b_glu, v_w_pool, v_pool_scale, v_w_br_att, v_w_br_ssm, v_w_br_pool, v_w_out, v_final_g):
    p = dict(zip(WEIGHTS, (norm_g, w_ada, b_ada, w_in, attn_sinks, ssm_a_re, ssm_a_im, ssm_log_dt, ssm_b_re, ssm_b_im,
                           ssm_c_re, ssm_c_im, ssm_d, w_glu, b_glu, w_pool, pool_scale, w_br_att, w_br_ssm, w_br_pool,
                           w_out, final_g)))
    m = dict(zip(WEIGHTS, (m_norm_g, m_w_ada, m_b_ada, m_w_in, m_attn_sinks, m_ssm_a_re, m_ssm_a_im, m_ssm_log_dt,
                           m_ssm_b_re, m_ssm_b_im, m_ssm_c_re, m_ssm_c_im, m_ssm_d, m_w_glu, m_b_glu, m_w_pool,
                           m_pool_scale, m_w_br_att, m_w_br_ssm, m_w_br_pool, m_w_out, m_final_g)))
    v = dict(zip(WEIGHTS, (v_norm_g, v_w_ada, v_b_ada, v_w_in, v_attn_sinks, v_ssm_a_re, v_ssm_a_im, v_ssm_log_dt,
                           v_ssm_b_re, v_ssm_b_im, v_ssm_c_re, v_ssm_c_im, v_ssm_d, v_w_glu, v_b_glu, v_w_pool,
                           v_pool_scale, v_w_br_att, v_w_br_ssm, v_w_br_pool, v_w_out, v_final_g)))
    return _step(p, m, v, x, c, loss_target)
```

```python
import functools
import math

import jax
import jax.numpy as jnp
from jax import lax
from jax.experimental import pallas as pl
from jax.experimental.pallas import tpu as pltpu

F32 = jnp.float32
BF16 = jnp.bfloat16

D_MODEL = 1024
CHUNK = 64
N_HEADS = 8
N_KV_HEADS = 2
HEAD_DIM = 64
Q_PER_KV = N_HEADS // N_KV_HEADS
HALO = 128
ATT_W = 512
KV_W = 128
SSM_W = 512
SSM_GROUP = 16
SSM_GROUPS = 32
SSM_STATE = 64
POOL_W = 512
POOL_GW = 128
IN_W = 6400
EPS = 1e-6
NEG_INF = -1e30
ADAM_LR = 0.001
ADAM_B1 = 0.9
ADAM_B2 = 0.999
ADAM_EPS = 1e-08
ADAM_WD = 0.01
ADAM_STEP = 10

OFF_GL, OFF_Q, OFF_ZA, OFF_US, OFF_ZS, OFF_UP, OFF_ZP, OFF_KV = 0, 3072, 3584, 4096, 4608, 5120, 5632, 6144
_PERM_PIECES = ((3328, 3072), (0, 512), (1792, 512), (768, 512), (2304, 512), (1280, 512), (2816, 512), (512, 256))

LANES = 128
SUBLANES = 8
N_SBLK = SSM_GROUPS * SSM_STATE // LANES
VMEM_LIMIT = 48 * 1024 * 1024

N_CHIPS = 4
N_DEV = 8

WEIGHTS = ['norm_g', 'w_ada', 'b_ada', 'w_in', 'attn_sinks', 'ssm_a_re', 'ssm_a_im', 'ssm_log_dt', 'ssm_b_re',
           'ssm_b_im', 'ssm_c_re', 'ssm_c_im', 'ssm_d', 'w_glu', 'b_glu', 'w_pool', 'pool_scale', 'w_br_att',
           'w_br_ssm', 'w_br_pool', 'w_out', 'final_g']
SMALL = ['norm_g', 'b_ada', 'attn_sinks', 'ssm_a_re', 'ssm_a_im', 'ssm_log_dt', 'ssm_b_re', 'ssm_b_im', 'ssm_c_re',
         'ssm_c_im', 'ssm_d', 'b_glu', 'w_pool', 'pool_scale', 'final_g']
BIG = (('w_in', (1024, 1600)), ('w_br_att', (512, 256)), ('w_br_ssm', (512, 256)), ('w_br_pool', (512, 256)),
       ('w_out', (256, 1024)), ('w_glu', (128, 512)))
PACK_COLS = 1024


def _params(sem=None):
    return pltpu.CompilerParams(dimension_semantics=sem, vmem_limit_bytes=VMEM_LIMIT)


def _row(v):
    return v.reshape(1, -1)


def _full(shape):
    nd = len(shape)
    return pl.BlockSpec(shape, lambda *_: (0,) * nd)


def _sigmoid(v):
    return 1.0 / (1.0 + jnp.exp(-v))


def _silu_and_grad(z):
    s = _sigmoid(z)
    return z * s, s * (1.0 + z * (1.0 - s))


_GELU_K = math.sqrt(2.0 / math.pi)


def _gelu_and_grad(v):
    inner = _GELU_K * (v + 0.044715 * v * v * v)
    th = jnp.tanh(inner)
    val = 0.5 * v * (1.0 + th)
    grad = 0.5 * (1.0 + th) + 0.5 * v * (1.0 - th * th) * _GELU_K * (1.0 + 3 * 0.044715 * v * v)
    return val, grad


_NN = (((1,), (0,)), ((), ()))
_NT = (((1,), (1,)), ((), ()))
_TN = (((0,), (0,)), ((), ()))


def _dot(a, b, dims=_NN):
    return lax.dot_general(a.astype(BF16), b.astype(BF16), dims, preferred_element_type=F32)


def _matmul(a, b, mode, *, tm, tn, tk, name, out_dtype=F32, n_outer=False, sends=None):
    if mode == 'nn':
        (m, k), (_, n) = a.shape, b.shape
    elif mode == 'nt':
        (m, k), (n, _) = a.shape, b.shape
    else:
        (k, m), (_, n) = a.shape, b.shape
    tm, tn, tk = min(tm, m), min(tn, n), min(tk, k)
    assert m % tm == 0 and n % tn == 0 and k % tk == 0, (name, a.shape, b.shape)
    nk = k // tk
    dims = {'nn': _NN, 'nt': _NT, 'tn': _TN}[mode]

    def body(a_ref, b_ref, o_ref, acc_ref):
        if nk == 1:
            o_ref[...] = _dot(a_ref[...], b_ref[...], dims).astype(out_dtype)
            return
        kk = pl.program_id(2)

        @pl.when(kk == 0)
        def _():
            acc_ref[...] = jnp.zeros_like(acc_ref)

        acc_ref[...] += _dot(a_ref[...], b_ref[...], dims)

        @pl.when(kk == nk - 1)
        def _():
            o_ref[...] = acc_ref[...].astype(out_dtype)

    def spec(shape, index):
        if n_outer:
            return pl.BlockSpec(shape, lambda j, i, kk: index(i, j, kk))
        return pl.BlockSpec(shape, index)

    a_spec = spec((tk, tm), lambda i, j, kk: (kk, i)) if mode == 'tn' else spec((tm, tk), lambda i, j, kk: (i, kk))
    b_spec = spec((tn, tk), lambda i, j, kk: (j, kk)) if mode == 'nt' else spec((tk, tn), lambda i, j, kk: (kk, j))
    grid = (n // tn, m // tm, nk) if n_outer else (m // tm, n // tn, nk)
    body, x_in, x_out, x_shapes, x_scratch, x_args = _carry_sends(body, 2, 1, 1, grid, sends)
    semantics = ("parallel", "parallel", "arbitrary") if sends is None else ("arbitrary",) * 3
    outs = pl.pallas_call(
        body, name=name, grid=grid, in_specs=[a_spec, b_spec] + x_in,
        out_specs=[spec((tm, tn), lambda i, j, kk: (i, j))] + x_out,
        out_shape=[jax.ShapeDtypeStruct((m, n), out_dtype)] + x_shapes,
        scratch_shapes=[pltpu.VMEM((tm, tn), F32)] + x_scratch, compiler_params=_params(semantics))(a, b, *x_args)
    return outs[0] if sends is None else (outs[0], list(outs[1:]))


ROWS_NORM = 512


def _norm_fwd(x, g, mod):
    l, d = x.shape
    tr = min(ROWS_NORM, l)

    def body(x_ref, g_ref, mod_ref, h_ref):
        xv = x_ref[...]
        r = lax.rsqrt(jnp.mean(xv * xv, axis=-1, keepdims=True) + EPS)
        shift, scale = mod_ref[:, 0:d], mod_ref[:, d:2 * d]
        h_ref[...] = ((xv * r * g_ref[...]) * (1.0 + scale) + shift).astype(BF16)

    return pl.pallas_call(
        body, name="norm_fwd", grid=(l // tr,),
        in_specs=[pl.BlockSpec((tr, d), lambda i: (i, 0)), _full((1, d)), _full((1, 3 * d))],
        out_specs=pl.BlockSpec((tr, d), lambda i: (i, 0)), out_shape=jax.ShapeDtypeStruct((l, d), BF16),
        compiler_params=_params(("parallel",)))(x, g, mod)


def _norm_bwd(x, dh, dxo, g, mod):
    l, d = x.shape
    tr = min(ROWS_NORM, l)

    def body(x_ref, dh_ref, dxo_ref, g_ref, mod_ref, dx_ref, st_ref):
        @pl.when(pl.program_id(0) == 0)
        def _():
            st_ref[...] = jnp.zeros_like(st_ref)

        xv, dhv = x_ref[...], dh_ref[...]
        r = lax.rsqrt(jnp.mean(xv * xv, axis=-1, keepdims=True) + EPS)
        xn = xv * r
        gv = g_ref[...]
        sc1 = 1.0 + mod_ref[:, d:2 * d]
        dxn = dhv * gv * sc1
        dx_ref[...] = dxo_ref[...] + r * (dxn - xn * jnp.mean(dxn * xn, axis=-1, keepdims=True))
        st_ref[0:1, :] += jnp.sum(dhv, axis=0, keepdims=True)
        st_ref[1:2, :] += jnp.sum(dhv * xn * gv, axis=0, keepdims=True)
        st_ref[2:3, :] += jnp.sum(dhv * xn * sc1, axis=0, keepdims=True)

    blk = pl.BlockSpec((tr, d), lambda i: (i, 0))
    return pl.pallas_call(
        body, name="norm_bwd", grid=(l // tr,),
        in_specs=[blk, blk, blk, _full((1, d)), _full((1, 3 * d))],
        out_specs=[blk, _full((SUBLANES, d))],
        out_shape=[jax.ShapeDtypeStruct((l, d), F32), jax.ShapeDtypeStruct((SUBLANES, d), F32)],
        compiler_params=_params(("arbitrary",)))(x, dh, dxo, g, mod)


def _loss_head(x, g, target):
    l, d = x.shape
    tr = min(ROWS_NORM, l)

    def body(x_ref, g_ref, t_ref, dx_ref, st_ref):
        @pl.when(pl.program_id(0) == 0)
        def _():
            st_ref[...] = jnp.zeros_like(st_ref)

        xv = x_ref[...]
        r = lax.rsqrt(jnp.mean(xv * xv, axis=-1, keepdims=True) + EPS)
        xn = xv * r
        gv = g_ref[...]
        err = xn * gv - t_ref[...]
        part = 0.5 * jnp.sum(jnp.mean(err * err, axis=-1, keepdims=True), axis=0, keepdims=True)
        dy = err * (1.0 / d)
        dxn = dy * gv
        dx_ref[...] = r * (dxn - xn * jnp.mean(dxn * xn, axis=-1, keepdims=True))
        st_ref[0:1, :] += jnp.sum(dy * xn, axis=0, keepdims=True)
        st_ref[1:2, :] += jnp.broadcast_to(part, (1, d))

    blk = pl.BlockSpec((tr, d), lambda i: (i, 0))
    return pl.pallas_call(
        body, name="loss_head", grid=(l // tr,), in_specs=[blk, _full((1, d)), blk],
        out_specs=[blk, _full((SUBLANES, d))],
        out_shape=[jax.ShapeDtypeStruct((l, d), F32), jax.ShapeDtypeStruct((SUBLANES, d), F32)],
        compiler_params=_params(("arbitrary",)))(x, g, target)


ROWS_ATT = 128
ROWS_ATT_BWD = 256
_SLOPES = tuple(2.0 ** (-8.0 * (h + 1) / N_HEADS) for h in range(N_HEADS))


def _att_mask(i, t):
    r = lax.broadcasted_iota(jnp.int32, (t, t + HALO), 0)
    j = lax.broadcasted_iota(jnp.int32, (t, t + HALO), 1)
    dist = jnp.abs(r + HALO - j).astype(F32)
    rc, jc = r // CHUNK, j // CHUNK
    allowed = (jc >= rc) & (jc <= rc + 2) & ((j >= HALO) | (i > 0))
    return dist, allowed


def _att_probs(qh, k, dist, allowed, slope, sink):
    s = _dot(qh, k, _NT) * (1.0 / math.sqrt(HEAD_DIM)) - slope * dist
    s = jnp.where(allowed, s, NEG_INF)
    m = jnp.maximum(jnp.max(s, axis=1, keepdims=True), sink)
    e = jnp.exp(s - m)
    es = jnp.exp(sink - m)
    den = jnp.sum(e, axis=1, keepdims=True) + es
    return e / den, es / den


def _att_specs(t):
    q_spec = pl.BlockSpec((t, ATT_W), lambda i: (i, OFF_Q // ATT_W))
    kv_spec = pl.BlockSpec((t, 2 * KV_W), lambda i: (i, OFF_KV // (2 * KV_W)))
    halo_spec = pl.BlockSpec((HALO, 2 * KV_W), lambda i: (jnp.maximum(i * (t // HALO) - 1, 0), OFF_KV // (2 * KV_W)))
    return q_spec, kv_spec, halo_spec


def _attention_fwd(proj, sinks):
    l = proj.shape[0]
    t = min(ROWS_ATT, l)

    def body(sink_ref, q_ref, kv_ref, halo_ref, o_ref):
        dist, allowed = _att_mask(pl.program_id(0), t)
        kv = jnp.concatenate([halo_ref[...], kv_ref[...]], axis=0)
        for h in range(N_HEADS):
            kh = h // Q_PER_KV
            k = kv[:, kh * HEAD_DIM:(kh + 1) * HEAD_DIM]
            v = kv[:, KV_W + kh * HEAD_DIM:KV_W + (kh + 1) * HEAD_DIM]
            p, _ = _att_probs(q_ref[:, h * HEAD_DIM:(h + 1) * HEAD_DIM], k, dist, allowed, _SLOPES[h], sink_ref[h])
            o_ref[:, h * HEAD_DIM:(h + 1) * HEAD_DIM] = _dot(p, v)

    q_spec, kv_spec, halo_spec = _att_specs(t)
    return pl.pallas_call(
        body, name="attention_fwd", grid=(l // t,),
        in_specs=[pl.BlockSpec(memory_space=pltpu.SMEM), q_spec, kv_spec, halo_spec],
        out_specs=pl.BlockSpec((t, ATT_W), lambda i: (i, 0)), out_shape=jax.ShapeDtypeStruct((l, ATT_W), F32),
        compiler_params=_params(("parallel",)))(sinks, proj, proj, proj)


def _attention_bwd(proj, sinks, do, sends=None):
    l = proj.shape[0]
    t = min(ROWS_ATT_BWD, l)

    def body(sink_ref, q_ref, kv_ref, halo_ref, do_ref, dq_ref, dkv_ref, dsink_ref):
        i = pl.program_id(0)

        @pl.when(i == 0)
        def _():
            dkv_ref[...] = jnp.zeros_like(dkv_ref)
            dsink_ref[...] = jnp.zeros_like(dsink_ref)

        dist, allowed = _att_mask(i, t)
        kv = jnp.concatenate([halo_ref[...], kv_ref[...]], axis=0)
        rows = pl.ds(pl.multiple_of(i * t, t), t + HALO)
        for kh in range(N_KV_HEADS):
            k = kv[:, kh * HEAD_DIM:(kh + 1) * HEAD_DIM]
            v = kv[:, KV_W + kh * HEAD_DIM:KV_W + (kh + 1) * HEAD_DIM]
            dk = jnp.zeros((t + HALO, HEAD_DIM), F32)
            dv = jnp.zeros((t + HALO, HEAD_DIM), F32)
            for h in range(kh * Q_PER_KV, (kh + 1) * Q_PER_KV):
                qh = q_ref[:, h * HEAD_DIM:(h + 1) * HEAD_DIM]
                doh = do_ref[:, h * HEAD_DIM:(h + 1) * HEAD_DIM]
                p, ps = _att_probs(qh, k, dist, allowed, _SLOPES[h], sink_ref[h])
                dp = _dot(doh, v, _NT)
                delta = jnp.sum(p * dp, axis=1, keepdims=True)
                ds = p * (dp - delta) * (1.0 / math.sqrt(HEAD_DIM))
                dsink_ref[h:h + 1, :] += jnp.broadcast_to(-jnp.sum(ps * delta, axis=0, keepdims=True), (1, LANES))
                dq_ref[:, h * HEAD_DIM:(h + 1) * HEAD_DIM] = _dot(ds, k).astype(BF16)
                dk = dk + _dot(ds, qh, _TN)
                dv = dv + _dot(p, doh, _TN)
            dkv_ref[rows, kh * HEAD_DIM:(kh + 1) * HEAD_DIM] += dk
            dkv_ref[rows, KV_W + kh * HEAD_DIM:KV_W + (kh + 1) * HEAD_DIM] += dv

    q_spec, kv_spec, halo_spec = _att_specs(t)
    blk = pl.BlockSpec((t, ATT_W), lambda i: (i, 0))
    body, x_in, x_out, x_shapes, x_scratch, x_args = _carry_sends(body, 5, 3, 0, (l // t,), sends)
    outs = pl.pallas_call(
        body, name="attention_bwd" if sends is None else "attention_bwd_sending", grid=(l // t,),
        in_specs=[pl.BlockSpec(memory_space=pltpu.SMEM), q_spec, kv_spec, halo_spec, blk] + x_in,
        out_specs=[blk, _full((HALO + l, 2 * KV_W)), _full((N_HEADS, LANES))] + x_out,
        out_shape=[jax.ShapeDtypeStruct((l, ATT_W), BF16), jax.ShapeDtypeStruct((HALO + l, 2 * KV_W), F32),
                   jax.ShapeDtypeStruct((N_HEADS, LANES), F32)] + x_shapes,
        scratch_shapes=x_scratch, compiler_params=_params(("arbitrary",)))(sinks, proj, proj, proj, do, *x_args)
    return outs[0], outs[1], outs[2], list(outs[3:])


ROWS_SSM = 2048
SSM_BLOCKS = 2
SCAN_UNROLL = 2
SSM_STEPS_PER_U = LANES // (SSM_BLOCKS * 2 * SSM_GROUP)


def _ssm_discretize(a_re, a_im, log_dt, b_re, b_im):
    lam = lax.complex(a_re, a_im)
    dt = jnp.exp(log_dt)[:, None]
    lam_bar = jnp.exp(lam * dt)
    b_bar = ((lam_bar - 1.0) / lam)[..., None] * lax.complex(b_re, b_im)
    return lam, dt, lam_bar, b_bar


def _ssm_block_diag(m):
    e = m.reshape(N_SBLK, 2, SSM_GROUP, SSM_STATE)
    e = e[:, :, :, None, :] * jnp.eye(2, dtype=m.dtype)[None, :, None, :, None]
    e = e.reshape(N_SBLK, 2 * SSM_GROUP, LANES)
    oh = jax.nn.one_hot(jnp.arange(N_SBLK) % 4, 4, dtype=m.dtype)
    return (oh[:, :, None, None] * e[:, None]).reshape(N_SBLK, LANES, LANES)


def _ssm_tables(a_re, a_im, log_dt, b_re, b_im, c_re, c_im):
    _, _, lam_bar, b_bar = _ssm_discretize(a_re, a_im, log_dt, b_re, b_im)
    lam_blk = jnp.stack([jnp.real(lam_bar).reshape(N_SBLK, LANES), jnp.imag(lam_bar).reshape(N_SBLK, LANES)], axis=1)
    bt = jnp.transpose(b_bar, (0, 2, 1))
    bb = jnp.concatenate([_ssm_block_diag(jnp.real(bt)), _ssm_block_diag(jnp.imag(bt))], axis=2)
    cb = jnp.concatenate([jnp.transpose(_ssm_block_diag(c_re), (0, 2, 1)),
                          jnp.transpose(_ssm_block_diag(-c_im), (0, 2, 1))], axis=1)
    return lam_blk, bb, cb


def _ssm_powers(a_re, a_im, log_dt):
    lam = lax.complex(a_re, a_im)
    dt = jnp.exp(log_dt)[:, None]
    k = jnp.arange(1, SUBLANES + 1, dtype=F32)
    pw = jnp.exp((lam * dt)[None] * k[:, None, None]).reshape(SUBLANES, N_SBLK, LANES)
    pw = jnp.transpose(pw, (1, 0, 2))
    rev = pw[:, ::-1]
    return jnp.concatenate([jnp.real(pw), jnp.imag(pw), jnp.real(rev), jnp.imag(rev)], axis=1)


def _scan_consts(pw_ref, b, reverse):
    row = lax.broadcasted_iota(jnp.int32, (SUBLANES, LANES), 0)
    sign = -1.0 if reverse else 1.0

    def power(k):
        return (jnp.broadcast_to(pw_ref[b, k - 1:k, :], (SUBLANES, LANES)),
                sign * jnp.broadcast_to(pw_ref[b, SUBLANES + k - 1:SUBLANES + k, :], (SUBLANES, LANES)))

    steps = []
    for d in (1, 2, 4):
        pr, pi = power(d)
        keep = (row < SUBLANES - d) if reverse else (row >= d)
        steps.append((d, jnp.where(keep, pr, 0.0), jnp.where(keep, pi, 0.0)))
    base = 2 * SUBLANES if reverse else 0
    return steps, pw_ref[b, base:base + SUBLANES, :], sign * pw_ref[b, base + SUBLANES:base + 2 * SUBLANES, :]


def _scan_tile(xr, xi, steps, reverse):
    for d, ar, ai in steps:
        shift = SUBLANES - d if reverse else d
        rr, ri = pltpu.roll(xr, shift, 0), pltpu.roll(xi, shift, 0)
        xr, xi = xr + (ar * rr - ai * ri), xi + (ar * ri + ai * rr)
    return xr, xi


def _bcast_row(v, r):
    return jnp.broadcast_to(v[r:r + 1, :], (SUBLANES, LANES))


def _re_im(b):
    return slice(2 * b * LANES, (2 * b + 1) * LANES), slice((2 * b + 1) * LANES, (2 * b + 2) * LANES)


def _scan_forward(s_scr, pw_ref, n_tiles):
    consts = [_scan_consts(pw_ref, b, False) for b in range(SSM_BLOCKS)]

    def tiles(g, carry):
        carry = list(carry)
        rows = [pl.ds(pl.multiple_of((g * SCAN_UNROLL + u) * SUBLANES, SUBLANES), SUBLANES) for u in range(SCAN_UNROLL)]
        loaded = [[(s_scr[r, _re_im(b)[0]], s_scr[r, _re_im(b)[1]]) for b in range(SSM_BLOCKS)] for r in rows]
        local = [[_scan_tile(xr, xi, consts[b][0], False) for b, (xr, xi) in enumerate(per_row)] for per_row in loaded]
        for r, per_row in zip(rows, local):
            for b, (xr, xi) in enumerate(per_row):
                _, pr, pi = consts[b]
                cr, ci = carry[b]
                sr, si = xr + (pr * cr - pi * ci), xi + (pr * ci + pi * cr)
                s_scr[r, _re_im(b)[0]] = sr
                s_scr[r, _re_im(b)[1]] = si
                carry[b] = (_bcast_row(sr, SUBLANES - 1), _bcast_row(si, SUBLANES - 1))
        return tuple(carry)

    zero = jnp.zeros((SUBLANES, LANES), F32)
    lax.fori_loop(0, n_tiles // SCAN_UNROLL, tiles, ((zero, zero),) * SSM_BLOCKS)


def _row_chunks(l):
    rc = min(ROWS_SSM, l)
    return rc, l // rc


def _carry_sends(body, n_in, n_out, n_scratch, grid, sends):
    if sends is None:
        return body, [], [], [], [], []
    k, k_out = len(sends.srcs), len(sends.out_shapes)

    def carrying(*refs):
        ins, send_ins = refs[:n_in], refs[n_in:n_in + k]
        outs, send_outs = refs[n_in + k:n_in + k + n_out], refs[n_in + k + n_out:n_in + k + n_out + k_out]
        rest = refs[n_in + k + n_out + k_out:]
        scratch, sems = rest[:n_scratch], rest[n_scratch:]
        ids = [pl.program_id(axis) for axis in range(len(grid))]
        first = functools.reduce(jnp.logical_and, [i == 0 for i in ids])
        last = functools.reduce(jnp.logical_and, [i == n - 1 for i, n in zip(ids, grid)])

        @pl.when(first)
        def _():
            sends.start(send_ins, send_outs, *sems)

        body(*ins, *outs, *scratch)

        @pl.when(last)
        def _():
            sends.finish(send_ins, send_outs, *sems)

    in_specs, out_specs, scratch = sends.specs()
    return carrying, in_specs, out_specs, sends.out_shapes, scratch, sends.srcs


def _ssm_fwd(proj, bb, cb, pw, sends=None):
    l = proj.shape[0]
    rc, n_chunks = _row_chunks(l)

    def body(u_ref, bb_ref, cb_ref, pw_ref, y_ref, s_scr):
        j = pl.program_id(0)

        def fill(ci, _):
            rows = pl.ds(pl.multiple_of(ci * rc, rc), rc)
            uv = u_ref[rows, :]
            for b in range(SSM_BLOCKS):
                s_scr[rows, 2 * b * LANES:2 * (b + 1) * LANES] = _dot(uv, bb_ref[b])
            return 0

        lax.fori_loop(0, n_chunks, fill, 0)
        _scan_forward(s_scr, pw_ref, l // SUBLANES)

        @pl.when(j % SSM_STEPS_PER_U == 0)
        def _():
            y_ref[...] = jnp.zeros_like(y_ref)

        def emit(ci, _):
            rows = pl.ds(pl.multiple_of(ci * rc, rc), rc)
            for b in range(SSM_BLOCKS):
                y_ref[rows, :] += _dot(s_scr[rows, 2 * b * LANES:2 * (b + 1) * LANES], cb_ref[b])
            return 0

        lax.fori_loop(0, n_chunks, emit, 0)

    steps = N_SBLK // SSM_BLOCKS
    body, x_in, x_out, x_shapes, x_scratch, x_args = _carry_sends(body, 4, 2, 0, (steps,), sends)
    outs = pl.pallas_call(
        body, name="ssm_fwd" if sends is None else "ssm_fwd_sending", grid=(steps,),
        in_specs=[pl.BlockSpec((l, LANES), lambda j: (0, OFF_US // LANES + j // SSM_STEPS_PER_U)),
                  pl.BlockSpec((SSM_BLOCKS, LANES, 2 * LANES), lambda j: (j, 0, 0)),
                  pl.BlockSpec((SSM_BLOCKS, 2 * LANES, LANES), lambda j: (j, 0, 0)),
                  pl.BlockSpec((SSM_BLOCKS, 4 * SUBLANES, LANES), lambda j: (j, 0, 0))] + x_in,
        out_specs=[pl.BlockSpec((l, LANES), lambda j: (0, j // SSM_STEPS_PER_U)),
                   pl.BlockSpec((l, SSM_BLOCKS * 2 * LANES), lambda j: (0, j))] + x_out,
        out_shape=[jax.ShapeDtypeStruct((l, SSM_W), F32), jax.ShapeDtypeStruct((l, N_SBLK * 2 * LANES), F32)] + x_shapes,
        scratch_shapes=x_scratch, compiler_params=_params(("arbitrary",)))(proj, bb, cb, pw, *x_args)
    return outs[0], outs[1], list(outs[2:])


def _ssm_bwd(proj, states, dy, bb, cb, pw, sends=None):
    l = proj.shape[0]
    rc, n_chunks = _row_chunks(l)
    n_tiles = l // SUBLANES

    def body(u_ref, s_scr, dy_ref, bb_ref, cb_ref, pw_ref, du_ref, dbb_ref, dcb_ref, dlam_ref, a_scr):
        j = pl.program_id(0)
        dcb_ref[...] = jnp.zeros_like(dcb_ref)

        def through_c(ci, _):
            rows = pl.ds(pl.multiple_of(ci * rc, rc), rc)
            dyv = dy_ref[rows, :]
            for b in range(SSM_BLOCKS):
                lanes = slice(2 * b * LANES, 2 * (b + 1) * LANES)
                dcb_ref[b] += _dot(s_scr[rows, lanes], dyv, _TN)
                a_scr[rows, lanes] = _dot(dyv, cb_ref[b], _NT)
            return 0

        lax.fori_loop(0, n_chunks, through_c, 0)

        consts = [_scan_consts(pw_ref, b, True) for b in range(SSM_BLOCKS)]
        row = lax.broadcasted_iota(jnp.int32, (SUBLANES, LANES), 0)
        last = row == SUBLANES - 1

        def tiles(g, carry):
            carry = list(carry)
            rows = [pl.ds(pl.multiple_of((n_tiles - 1 - (g * SCAN_UNROLL + u)) * SUBLANES, SUBLANES), SUBLANES)
                    for u in range(SCAN_UNROLL)]
            loaded = [[(a_scr[r, _re_im(b)[0]], a_scr[r, _re_im(b)[1]]) for b in range(SSM_BLOCKS)] for r in rows]
            states = [[(s_scr[r, _re_im(b)[0]], s_scr[r, _re_im(b)[1]]) for b in range(SSM_BLOCKS)] for r in rows]
            local = [[_scan_tile(xr, xi, consts[b][0], True) for b, (xr, xi) in enumerate(per_row)] for per_row in loaded]
            for r, per_row, state_row in zip(rows, local, states):
                for b, ((xr, xi), (sr, si)) in enumerate(zip(per_row, state_row)):
                    _, pr, pi = consts[b]
                    cr, ci, accr, acci = carry[b]
                    gr, gi = xr + (pr * cr - pi * ci), xi + (pr * ci + pi * cr)
                    a_scr[r, _re_im(b)[0]] = gr
                    a_scr[r, _re_im(b)[1]] = gi
                    ur = jnp.where(last, cr, pltpu.roll(gr, SUBLANES - 1, 0))
                    ui = jnp.where(last, ci, pltpu.roll(gi, SUBLANES - 1, 0))
                    carry[b] = (_bcast_row(gr, 0), _bcast_row(gi, 0),
                                accr + (ur * sr + ui * si), acci + (ui * sr - ur * si))
            return tuple(carry)

        zero = jnp.zeros((SUBLANES, LANES), F32)
        done = lax.fori_loop(0, n_tiles // SCAN_UNROLL, tiles, ((zero, zero, zero, zero),) * SSM_BLOCKS)
        for b in range(SSM_BLOCKS):
            dlr = jnp.broadcast_to(jnp.sum(done[b][2], axis=0, keepdims=True), (SUBLANES, LANES))
            dli = jnp.broadcast_to(jnp.sum(done[b][3], axis=0, keepdims=True), (SUBLANES, LANES))
            dlam_ref[b] = jnp.where(row == 0, dlr, jnp.where(row == 1, dli, 0.0))

        dbb_ref[...] = jnp.zeros_like(dbb_ref)

        @pl.when(j % SSM_STEPS_PER_U == 0)
        def _():
            du_ref[...] = jnp.zeros_like(du_ref)

        def through_b(ci, _):
            rows = pl.ds(pl.multiple_of(ci * rc, rc), rc)
            uv = u_ref[rows, :]
            for b in range(SSM_BLOCKS):
                av = a_scr[rows, 2 * b * LANES:2 * (b + 1) * LANES]
                dbb_ref[b] += _dot(uv, av, _TN)
                du_ref[rows, :] += _dot(av, bb_ref[b], _NT)
            return 0

        lax.fori_loop(0, n_chunks, through_b, 0)

    steps = N_SBLK // SSM_BLOCKS
    body, x_in, x_out, x_shapes, x_scratch, x_args = _carry_sends(body, 6, 4, 1, (steps,), sends)
    outs = pl.pallas_call(
        body, name="ssm_bwd" if sends is None else "ssm_bwd_sending", grid=(steps,),
        in_specs=[pl.BlockSpec((l, LANES), lambda j: (0, OFF_US // LANES + j // SSM_STEPS_PER_U)),
                  pl.BlockSpec((l, SSM_BLOCKS * 2 * LANES), lambda j: (0, j)),
                  pl.BlockSpec((l, LANES), lambda j: (0, j // SSM_STEPS_PER_U)),
                  pl.BlockSpec((SSM_BLOCKS, LANES, 2 * LANES), lambda j: (j, 0, 0)),
                  pl.BlockSpec((SSM_BLOCKS, 2 * LANES, LANES), lambda j: (j, 0, 0)),
                  pl.BlockSpec((SSM_BLOCKS, 4 * SUBLANES, LANES), lambda j: (j, 0, 0))] + x_in,
        out_specs=[pl.BlockSpec((l, LANES), lambda j: (0, j // SSM_STEPS_PER_U)),
                   pl.BlockSpec((SSM_BLOCKS, LANES, 2 * LANES), lambda j: (j, 0, 0)),
                   pl.BlockSpec((SSM_BLOCKS, 2 * LANES, LANES), lambda j: (j, 0, 0)),
                   pl.BlockSpec((SSM_BLOCKS, SUBLANES, LANES), lambda j: (j, 0, 0))] + x_out,
        out_shape=[jax.ShapeDtypeStruct((l, SSM_W), F32), jax.ShapeDtypeStruct((N_SBLK, LANES, 2 * LANES), F32),
                   jax.ShapeDtypeStruct((N_SBLK, 2 * LANES, LANES), F32),
                   jax.ShapeDtypeStruct((N_SBLK, SUBLANES, LANES), F32)] + x_shapes,
        scratch_shapes=[pltpu.VMEM((l, SSM_BLOCKS * 2 * LANES), F32)] + x_scratch,
        compiler_params=_params(("arbitrary",)))(proj, states, dy, bb, cb, pw, *x_args)
    return outs[0], outs[1], outs[2], outs[3], list(outs[4:])


def _pool_windows(v, t, l, ahead):
    def shifted(a, d):
        if ahead:
            return jnp.where(t < l - d, pltpu.roll(a, l - d, 0), 0.0)
        return jnp.where(t >= d, pltpu.roll(a, d, 0), 0.0)

    w2 = v + shifted(v, 1)
    w4 = w2 + shifted(w2, 2)
    w8 = w4 + shifted(w4, 4)
    w16 = w8 + shifted(w8, 8)
    return w2, w4, w8, w16


def _pool_select(g, ws):
    return jnp.where(g == 0, ws[0], jnp.where(g == 1, ws[1], jnp.where(g == 2, ws[2], ws[3])))


def _pool_count(g, t):
    return jnp.minimum(t + 1, jnp.left_shift(2, g)).astype(F32)


def _pool_specs(l):
    return [pl.BlockSpec((l, POOL_GW), lambda g: (0, OFF_UP // POOL_GW + g)),
            pl.BlockSpec((1, POOL_GW, POOL_GW), lambda g: (g, 0, 0)),
            pl.BlockSpec((1, POOL_GW), lambda g: (0, g))]


def _pool_fwd(proj, w_pool, scale):
    l = proj.shape[0]

    def body(u_ref, w_ref, sc_ref, y_ref):
        g = pl.program_id(0)
        t = lax.broadcasted_iota(jnp.int32, (l, 1), 0)
        u = u_ref[...]
        pooled = _pool_select(g, _pool_windows(u, t, l, False)) / _pool_count(g, t) - u
        y_ref[...] = _dot(pooled, w_ref[0]) * sc_ref[...]

    return pl.pallas_call(
        body, name="pool_fwd", grid=(4,), in_specs=_pool_specs(l),
        out_specs=pl.BlockSpec((l, POOL_GW), lambda g: (0, g)), out_shape=jax.ShapeDtypeStruct((l, POOL_W), F32),
        compiler_params=_params(("parallel",)))(proj, w_pool, scale)


def _pool_bwd(proj, w_pool, scale, dy):
    l = proj.shape[0]

    def body(u_ref, w_ref, sc_ref, dy_ref, du_ref, dw_ref, dsc_ref):
        g = pl.program_id(0)
        t = lax.broadcasted_iota(jnp.int32, (l, 1), 0)
        u = u_ref[...]
        cnt = _pool_count(g, t)
        pooled = _pool_select(g, _pool_windows(u, t, l, False)) / cnt - u
        dyv = dy_ref[...]
        dsc_ref[...] = jnp.sum(dyv * _dot(pooled, w_ref[0]), axis=0, keepdims=True)
        dyl = dyv * sc_ref[...]
        dw_ref[0] = _dot(pooled, dyl, _TN)
        dpooled = _dot(dyl, w_ref[0], _NT)
        du_ref[...] = (_pool_select(g, _pool_windows(dpooled / cnt, t, l, True)) - dpooled).astype(BF16)

    return pl.pallas_call(
        body, name="pool_bwd", grid=(4,), in_specs=_pool_specs(l) + [pl.BlockSpec((l, POOL_GW), lambda g: (0, g))],
        out_specs=[pl.BlockSpec((l, POOL_GW), lambda g: (0, g)), pl.BlockSpec((1, POOL_GW, POOL_GW), lambda g: (g, 0, 0)),
                   pl.BlockSpec((1, POOL_GW), lambda g: (0, g))],
        out_shape=[jax.ShapeDtypeStruct((l, POOL_W), BF16), jax.ShapeDtypeStruct((4, POOL_GW, POOL_GW), F32),
                   jax.ShapeDtypeStruct((1, POOL_W), F32)],
        compiler_params=_params(("parallel",)))(proj, w_pool, scale, dy)


ROWS_MERGE = 128
ROWS_MERGE_FWD = 256


def _merge_inputs(tr):
    def col(off, w):
        return pl.BlockSpec((tr, w), lambda i: (i, off // w))

    def act(w):
        return pl.BlockSpec((tr, w), lambda i: (i, 0))

    d = D_MODEL
    return ([col(OFF_GL, 3 * d), col(OFF_ZA, ATT_W), col(OFF_US, SSM_W), col(OFF_ZS, SSM_W), col(OFF_ZP, POOL_W),
             act(ATT_W), act(SSM_W), act(POOL_W)]
            + [_full((1, SSM_W)), _full((SSM_W, SSM_W)), _full((1, SSM_W)), _full((ATT_W, d)), _full((SSM_W, d)),
               _full((POOL_W, d)), _full((d, d)), _full((1, 3 * d))])


def _merge_forward_math(gl_ref, za_ref, us_ref, zs_ref, zp_ref, o_ref, yl_ref, yp_ref, d_ref, wg_ref, bg_ref,
                        wba_ref, wbs_ref, wbp_ref):
    d = D_MODEL
    r = {}
    r['sa'], r['dsa'] = _silu_and_grad(za_ref[...])
    r['ss'], r['dss'] = _silu_and_grad(zs_ref[...])
    r['sp'], r['dsp'] = _silu_and_grad(zp_ref[...])
    r['y_att'] = o_ref[...] * r['sa']
    y1 = yl_ref[...] + d_ref[...] * us_ref[...]
    r['y2'], r['dgelu'] = _gelu_and_grad(y1)
    r['sg'] = _sigmoid(_dot(r['y2'], wg_ref[...]) + bg_ref[...])
    r['y3'] = r['y2'] * r['sg']
    r['y_ssm'] = r['y3'] * r['ss']
    r['y_pool'] = yp_ref[...] * r['sp']
    r['g'] = [_sigmoid(gl_ref[:, b * d:(b + 1) * d]) for b in range(3)]
    r['p'] = [_dot(r['y_att'], wba_ref[...]), _dot(r['y_ssm'], wbs_ref[...]), _dot(r['y_pool'], wbp_ref[...])]
    r['merged'] = r['g'][0] * r['p'][0] + r['g'][1] * r['p'][1] + r['g'][2] * r['p'][2]
    return r


def _merge_fwd(proj, o, y_lin, y_pool, x, ssm_d, w_glu, b_glu, wba, wbs, wbp, w_out, mod, sends=None):
    l, d = x.shape
    tr = min(ROWS_MERGE_FWD, l)

    def body(gl_ref, za_ref, us_ref, zs_ref, zp_ref, o_ref, yl_ref, yp_ref, d_ref, wg_ref, bg_ref, wba_ref, wbs_ref,
             wbp_ref, wo_ref, mod_ref, x_ref, xn_ref):
        r = _merge_forward_math(gl_ref, za_ref, us_ref, zs_ref, zp_ref, o_ref, yl_ref, yp_ref, d_ref, wg_ref, bg_ref,
                                wba_ref, wbs_ref, wbp_ref)
        xn_ref[...] = x_ref[...] + mod_ref[:, 2 * d:3 * d] * _dot(r['merged'], wo_ref[...])

    blk = pl.BlockSpec((tr, d), lambda i: (i, 0))
    in_specs = _merge_inputs(tr) + [blk]
    body, x_in, x_out, x_shapes, x_scratch, x_args = _carry_sends(body, len(in_specs), 1, 0, (l // tr,), sends)
    outs = pl.pallas_call(
        body, name="merge_fwd" if sends is None else "merge_fwd_sending", grid=(l // tr,), in_specs=in_specs + x_in,
        out_specs=[blk] + x_out, out_shape=[jax.ShapeDtypeStruct((l, d), F32)] + x_shapes, scratch_shapes=x_scratch,
        compiler_params=_params(("arbitrary",)))(
            proj, proj, proj, proj, proj, o, y_lin, y_pool, ssm_d, w_glu, b_glu, wba, wbs, wbp, w_out, mod, x, *x_args)
    return outs[0], list(outs[1:])


def _merge_bwd(proj, o, y_lin, y_pool, dxo, ssm_d, w_glu, b_glu, wba, wbs, wbp, w_out, mod, sends=None):
    l, d = dxo.shape
    tr = min(ROWS_MERGE, l)

    def body(gl_ref, za_ref, us_ref, zs_ref, zp_ref, o_ref, yl_ref, yp_ref, d_ref, wg_ref, bg_ref, wba_ref, wbs_ref,
             wbp_ref, wo_ref, mod_ref, dxo_ref,
             do_ref, dyl_ref, dus_ref, dyp_ref, dpj_ref,
             ya_ref, ys_ref, ypl_ref, y2_ref, dt_ref, dpa_ref, dps_ref, dpp_ref, mg_ref, dout_ref, st_ref):
        @pl.when(pl.program_id(0) == 0)
        def _():
            st_ref[...] = jnp.zeros_like(st_ref)

        for off, width in ((OFF_Q, ATT_W), (OFF_US, SSM_W), (OFF_UP, POOL_W), (OFF_KV, 2 * KV_W)):
            dpj_ref[:, off:off + width] = jnp.zeros((tr, width), BF16)

        r = _merge_forward_math(gl_ref, za_ref, us_ref, zs_ref, zp_ref, o_ref, yl_ref, yp_ref, d_ref, wg_ref, bg_ref,
                                wba_ref, wbs_ref, wbp_ref)
        dxov = dxo_ref[...]
        out = _dot(r['merged'], wo_ref[...])
        st_ref[0:1, :] += jnp.sum(dxov * out, axis=0, keepdims=True)
        dout = dxov * mod_ref[:, 2 * d:3 * d]
        dmerged = _dot(dout, wo_ref[...], _NT)
        dys = []
        for b, (w_ref, dp_ref) in enumerate(((wba_ref, dpa_ref), (wbs_ref, dps_ref), (wbp_ref, dpp_ref))):
            gb = r['g'][b]
            dpj_ref[:, OFF_GL + b * d:OFF_GL + (b + 1) * d] = (dmerged * r['p'][b] * gb * (1.0 - gb)).astype(BF16)
            dp = dmerged * gb
            dp_ref[...] = dp.astype(BF16)
            dys.append(_dot(dp, w_ref[...], _NT))
        mg_ref[...] = r['merged'].astype(BF16)
        dout_ref[...] = dout.astype(BF16)
        ya_ref[...] = r['y_att'].astype(BF16)
        ys_ref[...] = r['y_ssm'].astype(BF16)
        ypl_ref[...] = r['y_pool'].astype(BF16)
        y2_ref[...] = r['y2'].astype(BF16)
        do_ref[...] = (dys[0] * r['sa']).astype(BF16)
        dpj_ref[:, OFF_ZA:OFF_ZA + ATT_W] = (dys[0] * o_ref[...] * r['dsa']).astype(BF16)
        dy3 = dys[1] * r['ss']
        dpj_ref[:, OFF_ZS:OFF_ZS + SSM_W] = (dys[1] * r['y3'] * r['dss']).astype(BF16)
        dt = dy3 * r['y2'] * r['sg'] * (1.0 - r['sg'])
        dt_ref[...] = dt.astype(BF16)
        dy1 = (dy3 * r['sg'] + _dot(dt, wg_ref[...], _NT)) * r['dgelu']
        dyl_ref[...] = dy1.astype(BF16)
        dus_ref[...] = dy1 * d_ref[...]
        st_ref[1:2, 0:SSM_W] += jnp.sum(dt, axis=0, keepdims=True)
        st_ref[1:2, SSM_W:2 * SSM_W] += jnp.sum(dy1 * us_ref[...], axis=0, keepdims=True)
        dyp_ref[...] = dys[2] * r['sp']
        dpj_ref[:, OFF_ZP:OFF_ZP + POOL_W] = (dys[2] * yp_ref[...] * r['dsp']).astype(BF16)

    blk = pl.BlockSpec((tr, d), lambda i: (i, 0))
    half = pl.BlockSpec((tr, ATT_W), lambda i: (i, 0))
    wide = pl.BlockSpec((tr, IN_W), lambda i: (i, 0))
    sds = jax.ShapeDtypeStruct
    in_specs = _merge_inputs(tr) + [blk]
    out_specs = [half] * 4 + [wide] + [half] * 5 + [blk] * 5 + [_full((SUBLANES, d))]
    body, x_in, x_out, x_shapes, x_scratch, x_args = _carry_sends(body, len(in_specs), len(out_specs), 0, (l // tr,), sends)
    outs = pl.pallas_call(
        body, name="merge_bwd" if sends is None else "merge_bwd_sending", grid=(l // tr,), in_specs=in_specs + x_in,
        out_specs=out_specs + x_out,
        out_shape=[sds((l, ATT_W), t) for t in (BF16, BF16, F32, F32)] + [sds((l, IN_W), BF16)]
        + [sds((l, ATT_W), BF16)] * 5 + [sds((l, d), BF16)] * 5 + [sds((SUBLANES, d), F32)] + x_shapes,
        scratch_shapes=x_scratch, compiler_params=_params(("arbitrary",)))(
            proj, proj, proj, proj, proj, o, y_lin, y_pool, ssm_d, w_glu, b_glu, wba, wbs, wbp, w_out, mod, dxo, *x_args)
    return list(outs[:len(out_specs)]), list(outs[len(out_specs):])


def _ada_fwd(c_all, w_ada, b_shard):
    depth, d, n = w_ada.shape

    def body(c_ref, w_ref, b_ref, o_ref):
        act, _ = _silu_and_grad(c_ref[...])
        o_ref[0] = _dot(act, w_ref[0]) + b_ref[0]

    return pl.pallas_call(
        body, name="ada_fwd", grid=(depth,),
        in_specs=[_full((N_DEV, d)), pl.BlockSpec((1, d, n), lambda i: (i, 0, 0)), pl.BlockSpec((1, 1, n), lambda i: (i, 0, 0))],
        out_specs=pl.BlockSpec((1, N_DEV, n), lambda i: (i, 0, 0)), out_shape=jax.ShapeDtypeStruct((depth, N_DEV, n), F32),
        compiler_params=_params(("parallel",)))(c_all, w_ada, b_shard)


def _ada_bwd(c_all_t, dmod):
    d = c_all_t.shape[0]
    depth, _, n = dmod.shape

    def body(c_ref, dm_ref, o_ref):
        act, _ = _silu_and_grad(c_ref[...])
        acc = act[:, 0:1] * dm_ref[0, 0:1, :]
        for b in range(1, N_DEV):
            acc = acc + act[:, b:b + 1] * dm_ref[0, b:b + 1, :]
        o_ref[0] = acc

    return pl.pallas_call(
        body, name="ada_bwd", grid=(depth,),
        in_specs=[_full((d, N_DEV)), pl.BlockSpec((1, N_DEV, n), lambda i: (i, 0, 0))],
        out_specs=pl.BlockSpec((1, d, n), lambda i: (i, 0, 0)), out_shape=jax.ShapeDtypeStruct((depth, d, n), F32),
        compiler_params=_params(("parallel",)))(c_all_t, dmod)


ROWS_FLAT = 256


_RELATION_XOR = (0, 2, 1, 3)


def _sum_chips(own, others, chip_order, name):
    r, c = own.shape
    tr = math.gcd(ROWS_FLAT, r)

    def body(own_ref, oth_ref, o_ref):
        terms = [own_ref[...].astype(F32)] + [oth_ref[k].astype(F32) for k in range(3)]
        if chip_order:
            chip = 2 * lax.axis_index("x") + lax.axis_index("y")
            by_chip = []
            for q in range(N_CHIPS):
                rel = jnp.bitwise_xor(chip, q)
                pick = terms[3]
                for k in (2, 1, 0):
                    pick = jnp.where(rel == _RELATION_XOR[k], terms[k], pick)
                by_chip.append(pick)
            terms = by_chip
        o_ref[...] = ((terms[0] + terms[1]) + terms[2]) + terms[3]

    return pl.pallas_call(
        body, name=name, grid=(r // tr,),
        in_specs=[pl.BlockSpec((tr, c), lambda i: (i, 0)), pl.BlockSpec((3, tr, c), lambda i: (0, i, 0))],
        out_specs=pl.BlockSpec((tr, c), lambda i: (i, 0)), out_shape=jax.ShapeDtypeStruct((r, c), F32),
        compiler_params=_params(("parallel",)))(own, others)


def _add2(a, b, out_dtype, name):
    shape = a.shape
    a, b = a.reshape(-1, shape[-1]), b.reshape(-1, shape[-1])
    r, c = a.shape
    tr = math.gcd(ROWS_FLAT, r)

    def body(a_ref, b_ref, o_ref):
        o_ref[...] = (a_ref[...] + b_ref[...]).astype(out_dtype)

    blk = pl.BlockSpec((tr, c), lambda i: (i, 0))
    return pl.pallas_call(
        body, name=name, grid=(r // tr,), in_specs=[blk, blk], out_specs=blk,
        out_shape=jax.ShapeDtypeStruct((r, c), out_dtype), compiler_params=_params(("parallel",)))(a, b).reshape(shape)


def _adamw(w, g, m, v, name):
    r, c = w.shape
    tr = math.gcd(ROWS_FLAT, r)

    def body(w_ref, g_ref, m_ref, v_ref, d_ref, nm_ref, nv_ref):
        gv = g_ref[...]
        mv = ADAM_B1 * m_ref[...] + (1.0 - ADAM_B1) * gv
        vv = ADAM_B2 * v_ref[...] + (1.0 - ADAM_B2) * (gv * gv)
        m_hat = mv / (1.0 - ADAM_B1 ** ADAM_STEP)
        v_hat = vv / (1.0 - ADAM_B2 ** ADAM_STEP)
        d_ref[...] = -ADAM_LR * (m_hat / (jnp.sqrt(v_hat) + ADAM_EPS) + ADAM_WD * w_ref[...])
        nm_ref[...] = mv
        nv_ref[...] = vv

    blk = pl.BlockSpec((tr, c), lambda i: (i, 0))
    return pl.pallas_call(
        body, name=name, grid=(r // tr,), in_specs=[blk] * 4, out_specs=[blk] * 3,
        out_shape=[jax.ShapeDtypeStruct((r, c), F32)] * 3, compiler_params=_params(("parallel",)))(w, g, m, v)


_GROUP_MASKS = {
    'xy': ((1, 0, 0), (0, 1, 0), (1, 1, 0)),
    'c': ((0, 0, 1),),
    'xyc': ((0, 0, 1), (0, 1, 0), (0, 1, 1), (1, 0, 0), (1, 0, 1), (1, 1, 0), (1, 1, 1)),
}


def _group_rank(group, pos):
    x, y, c = pos
    return {'xy': 2 * x + y, 'c': c, 'xyc': 4 * x + 2 * y + c}[group]


def _exchange(src, group, scatter, name):
    masks = _GROUP_MASKS[group]
    n = len(masks) + 1
    shape = src.shape[1:] if scatter else src.shape
    assert (not scatter) or src.shape[0] == n

    def body(src_ref, out_ref, send_sems, recv_sems, local_sem):
        me = (lax.axis_index("x"), lax.axis_index("y"), lax.axis_index("c"))
        my_rank = _group_rank(group, me)

        def mine_for(rank):
            return src_ref.at[rank] if scatter else src_ref

        local = pltpu.make_async_copy(mine_for(my_rank), out_ref.at[my_rank], local_sem)
        local.start()
        sends = []
        for k, mask in enumerate(masks):
            peer = tuple(1 - p if f else p for p, f in zip(me, mask))
            peer_rank = _group_rank(group, peer)
            send = pltpu.make_async_remote_copy(
                src_ref=mine_for(peer_rank), dst_ref=out_ref.at[my_rank], send_sem=send_sems.at[k],
                recv_sem=recv_sems.at[k], device_id=peer, device_id_type=pl.DeviceIdType.MESH)
            send.start()
            sends.append((send, peer, peer_rank))
        for k, (send, peer, peer_rank) in enumerate(sends):
            pltpu.make_async_remote_copy(
                src_ref=mine_for(peer_rank), dst_ref=out_ref.at[peer_rank], send_sem=send_sems.at[k],
                recv_sem=recv_sems.at[k], device_id=peer, device_id_type=pl.DeviceIdType.MESH).wait_recv()
        for send, _, _ in sends:
            send.wait_send()
        local.wait()

    return pl.pallas_call(
        body, name=name, in_specs=[pl.BlockSpec(memory_space=pl.ANY)], out_specs=pl.BlockSpec(memory_space=pl.ANY),
        out_shape=jax.ShapeDtypeStruct((n,) + tuple(shape), src.dtype),
        scratch_shapes=[pltpu.SemaphoreType.DMA((n - 1,)), pltpu.SemaphoreType.DMA((n - 1,)), pltpu.SemaphoreType.DMA(())],
    )(src)


CHUNK_BYTES = 1 << 20
MIN_CHUNK_ROWS = 64


def _row_parts(rows, cols, itemsize):
    n = 1
    while rows % (2 * n) == 0 and rows // (2 * n) >= MIN_CHUNK_ROWS and rows * cols * itemsize > n * CHUNK_BYTES:
        n *= 2
    return n


def _remote(src, dst, send_sem, recv_sem, peer):
    return pltpu.make_async_remote_copy(src_ref=src, dst_ref=dst, send_sem=send_sem, recv_sem=recv_sem,
                                        device_id=peer, device_id_type=pl.DeviceIdType.MESH)


def _start_rows(src, dst, send_sem, recv_sem, peer):
    rows, cols = src.shape
    n = _row_parts(rows, cols, jnp.dtype(src.dtype).itemsize)
    pr = rows // n
    for i in range(n):
        _remote(src.at[pl.ds(i * pr, pr), :], dst.at[pl.ds(i * pr, pr), :], send_sem, recv_sem, peer).start()


def _mesh_place():
    x, y, c = lax.axis_index("x"), lax.axis_index("y"), lax.axis_index("c")
    other_chips = ((1 - x, y), (x, 1 - y), (1 - x, 1 - y))
    return x, y, c, 2 * x + y, (x, y, 1 - c), other_chips


def _comm_call(body, name, ins, out_shapes, sem_counts):
    any_spec = pl.BlockSpec(memory_space=pl.ANY)
    return pl.pallas_call(
        body, name=name, in_specs=[any_spec] * len(ins), out_specs=[any_spec] * len(out_shapes), out_shape=out_shapes,
        scratch_shapes=[pltpu.SemaphoreType.DMA((n,)) for n in sem_counts])(*ins)


class _Sends:
    sem_arrays = 2

    def run(self, name):
        k_in, k_out = len(self.srcs), len(self.out_shapes)

        def body(*refs):
            ins, outs, sems = refs[:k_in], refs[k_in:k_in + k_out], refs[k_in + k_out:]
            self.start(ins, outs, *sems)
            self.finish(ins, outs, *sems)

        return _comm_call(body, name, self.srcs, self.out_shapes, (self.n_sems,) * self.sem_arrays)

    def specs(self):
        any_spec = pl.BlockSpec(memory_space=pl.ANY)
        return ([any_spec] * len(self.srcs), [any_spec] * len(self.out_shapes),
                [pltpu.SemaphoreType.DMA((self.n_sems,))] * self.sem_arrays)


class _SiblingSends(_Sends):
    def __init__(self, srcs):
        self.srcs = list(srcs)
        self.n_sems = len(self.srcs)
        self.out_shapes = [jax.ShapeDtypeStruct(g.shape[1:], g.dtype) for g in self.srcs]

    def start(self, ins, outs, send_sems, recv_sems):
        _, _, c, _, sibling, _ = _mesh_place()
        for t in range(len(ins)):
            for q in range(ins[t].shape[1]):
                _start_rows(ins[t].at[1 - c, q], outs[t].at[q], send_sems.at[t], recv_sems.at[t], sibling)

    def finish(self, ins, outs, send_sems, recv_sems):
        _, _, _, _, sibling, _ = _mesh_place()
        for t in range(len(ins)):
            arrived = _remote(outs[t], outs[t], send_sems.at[t], recv_sems.at[t], sibling)
            arrived.wait_recv()
            arrived.wait_send()


class _ChipSends(_Sends):
    def __init__(self, srcs, pick, pass_on=False):
        self.srcs, self.pick, self.pass_on = list(srcs), pick, pass_on
        nt = len(self.srcs)
        self.n_sems = 3 * nt
        self.sem_arrays = 4 if pass_on else 2
        landing = []
        for s in self.srcs:
            r2 = s.shape[-2] // 2 if pick is _pick_weight_half else s.shape[-2]
            landing.append(jax.ShapeDtypeStruct((3, r2, s.shape[-1]), s.dtype))
        self.out_shapes = landing + (landing if pass_on else [])

    def start(self, ins, outs, send_sems, recv_sems, *unused):
        _, _, c, _, _, other_chips = _mesh_place()
        for t in range(len(ins)):
            for j, (px, py) in enumerate(other_chips):
                _remote(self.pick(ins[t], c, 2 * px + py), outs[t].at[j], send_sems.at[3 * t + j], recv_sems.at[3 * t + j],
                        (px, py, c)).start()

    def finish(self, ins, outs, send_sems, recv_sems, *sibling_sems):
        _, _, c, _, sibling, other_chips = _mesh_place()
        nt = len(ins)
        for t in range(nt):
            for j, (px, py) in enumerate(other_chips):
                k = 3 * t + j
                landed = _remote(outs[t].at[j], outs[t].at[j], send_sems.at[k], recv_sems.at[k], (px, py, c))
                landed.wait_recv()
                if self.pass_on:
                    _start_rows(outs[t].at[j], outs[nt + t].at[j], sibling_sems[0].at[k], sibling_sems[1].at[k], sibling)
                landed.wait_send()
        if self.pass_on:
            for t in range(nt):
                for j in range(3):
                    k = 3 * t + j
                    passed = _remote(outs[nt + t].at[j], outs[nt + t].at[j], sibling_sems[0].at[k], sibling_sems[1].at[k],
                                     sibling)
                    passed.wait_recv()
                    passed.wait_send()


def _pick_weight_half(src, core, chip):
    r2 = src.shape[0] // 2
    return src.at[pl.ds(pl.multiple_of(core * r2, MIN_CHUNK_ROWS), r2), :]


def _pick_slab(src, core, chip):
    return src.at[chip if src.shape[0] == N_CHIPS else 0]


def _swap_cores(halves):
    nt = len(halves)

    def body(*refs):
        ins, outs = refs[:nt], refs[nt:2 * nt]
        send_sems, recv_sems = refs[2 * nt:]
        _, _, _, _, sibling, _ = _mesh_place()
        for t in range(nt):
            for li in range(ins[t].shape[0]):
                _start_rows(ins[t].at[li], outs[t].at[li], send_sems.at[t], recv_sems.at[t], sibling)
        for t in range(nt):
            _remote(outs[t], outs[t], send_sems.at[t], recv_sems.at[t], sibling).wait_recv()
        for t in range(nt):
            _remote(ins[t], ins[t], send_sems.at[t], recv_sems.at[t], sibling).wait_send()

    out_shapes = [jax.ShapeDtypeStruct(h.shape, h.dtype) for h in halves]
    return _comm_call(body, "swap_cores", halves, out_shapes, (nt, nt))


def _pack_rows(pieces, cols, dtype):
    tile = SUBLANES * cols
    rows = []
    for p in pieces:
        flat = p.reshape(-1).astype(dtype)
        rows.append(jnp.pad(flat, (0, (-flat.shape[0]) % tile)).reshape(-1, cols))
    total = sum(r.shape[0] for r in rows)
    if total % (2 * SUBLANES):
        rows.append(jnp.zeros((SUBLANES, cols), dtype))
    return jnp.concatenate(rows, axis=0)


def _unpack_rows(buf, shapes):
    out, row = [], 0
    for s in shapes:
        n = math.prod(s)
        nrows = -(-n // (SUBLANES * buf.shape[1])) * SUBLANES
        out.append(buf[row:row + nrows].reshape(-1)[:n].reshape(s))
        row += nrows
    return out


ROW_SHARDED = ('w_out', 'w_glu')


def _chips_to_full(name, chips):
    mats = [c.reshape(-1, c.shape[-1]) for c in chips]
    if name in ROW_SHARDED:
        return jnp.concatenate(mats, axis=0)
    if name != 'w_in':
        return jnp.concatenate(mats, axis=1)
    cs = mats[0].shape[1]
    pieces = []
    for start, size in _PERM_PIECES:
        lo = start
        while lo < start + size:
            q = lo // cs
            hi = min(start + size, (q + 1) * cs)
            pieces.append(mats[q][:, lo - q * cs:hi - q * cs])
            lo = hi
    return jnp.concatenate(pieces, axis=1)


def _full_to_dests(name, full):
    rows, cols = full.shape
    if name in ROW_SHARDED:
        return jnp.transpose(full.reshape(N_CHIPS, 2, rows // (2 * N_CHIPS), cols), (1, 0, 2, 3))
    cs = cols // N_CHIPS
    if name != 'w_in':
        dests = [full[:, q * cs:(q + 1) * cs] for q in range(N_CHIPS)]
    else:
        offsets, off = [], 0
        for _, size in _PERM_PIECES:
            offsets.append(off)
            off += size
        by_start = sorted(zip(_PERM_PIECES, offsets))
        dests = []
        for q in range(N_CHIPS):
            parts = []
            for (start, size), at in by_start:
                lo, hi = max(start, q * cs), min(start + size, (q + 1) * cs)
                if lo < hi:
                    parts.append(full[:, at + lo - start:at + hi - start])
            dests.append(jnp.concatenate(parts, axis=1))
    return jnp.stack([d.reshape(2, rows // 2, cs) for d in dests], axis=1)


def _by_chip(own, related, x, y):
    grid = ((own, related[1]), (related[0], related[2]))
    along_x = [[jnp.where(x == 0, grid[px][dy], grid[1 - px][dy]) for dy in range(2)] for px in range(2)]
    return [jnp.where(y == 0, along_x[px][py], along_x[px][1 - py]) for px in range(2) for py in range(2)]


def _join_halves(mine, theirs, core, axis):
    return jnp.where(core == 0, jnp.concatenate([mine, theirs], axis=axis), jnp.concatenate([theirs, mine], axis=axis))


ROWS_WGRAD = 2048
SSM_RAW = ('ssm_a_re', 'ssm_a_im', 'ssm_log_dt', 'ssm_b_re', 'ssm_b_im', 'ssm_c_re', 'ssm_c_im')


def _all_tables(p):
    raw = tuple(p[k] for k in SSM_RAW)
    (_, bb, cb), vjp = jax.vjp(jax.vmap(_ssm_tables), *raw)
    return (bb.astype(BF16), cb.astype(BF16), jax.vmap(_ssm_powers)(*raw[:3])), vjp


def _layer_fwd(x, p, w_in, rest, li, mod, tables, sends=None, late_sends=None):
    bb, cb, pw = (t[li] for t in tables)
    h = _norm_fwd(x, _row(p['norm_g'][li]), mod)
    if isinstance(rest, dict):
        proj = _matmul(h, w_in, 'nn', tm=1024, tn=1280, tk=1024, name="proj", n_outer=True)
    else:
        proj, arrived = _matmul(h, w_in, 'nn', tm=1024, tn=1280, tk=1024, name="proj_sending", n_outer=True, sends=rest[0])
        rest = rest[1](arrived)
    full = dict(rest, w_in=w_in)
    o = _attention_fwd(proj, p['attn_sinks'][li])
    y_lin, states, brought = _ssm_fwd(proj, bb, cb, pw, sends)
    y_pool = _pool_fwd(proj, p['w_pool'][li].astype(BF16), _row(p['pool_scale'][li]))
    x_new, brought_late = _merge_fwd(proj, o, y_lin, y_pool, x, _row(p['ssm_d'][li]), full['w_glu'],
                                     _row(p['b_glu'][li]), full['w_br_att'], full['w_br_ssm'], full['w_br_pool'],
                                     full['w_out'], mod, late_sends)
    saved = dict(x=x, h=h, proj=proj, o=o, y_lin=y_lin, y_pool=y_pool, states=states, bb=bb, cb=cb, pw=pw, full=full)
    a, b = len(brought) // 2, len(brought_late) // 2
    return x_new, saved, brought[:a] + brought_late[:b] + brought[a:] + brought_late[b:]


def _layer_bwd(dxo, s, p, li, mod, core, upper=None, own_early=(), w_in_sender=None):
    l = dxo.shape[0]
    proj, full = s['proj'], s['full']
    outs, arrived_upper = _merge_bwd(
        proj, s['o'], s['y_lin'], s['y_pool'], dxo, _row(p['ssm_d'][li]), full['w_glu'], _row(p['b_glu'][li]),
        full['w_br_att'], full['w_br_ssm'], full['w_br_pool'], full['w_out'], mod,
        None if upper is None else _ChipSends(upper, _pick_slab))
    (do, dyl, dus_skip, dyp, dproj, ya, ys, ypl, y2, dt, dpa, dps, dpp, mg, dout, st) = outs
    g = {}
    g['w_br_att'] = _matmul(ya, dpa, 'tn', tm=512, tn=1024, tk=ROWS_WGRAD, name="grad_w_br")
    g['w_br_ssm'] = _matmul(ys, dps, 'tn', tm=512, tn=1024, tk=ROWS_WGRAD, name="grad_w_br")
    g['w_br_pool'] = _matmul(ypl, dpp, 'tn', tm=512, tn=1024, tk=ROWS_WGRAD, name="grad_w_br")
    g['w_out'] = _matmul(mg, dout, 'tn', tm=512, tn=1024, tk=ROWS_WGRAD, name="grad_w_out")
    g['w_glu'] = _matmul(y2, dt, 'tn', tm=512, tn=512, tk=ROWS_WGRAD, name="grad_w_glu")
    g['b_glu'] = st[1, 0:SSM_W]
    g['ssm_d'] = st[1, SSM_W:2 * SSM_W]
    dgate = st[0]

    early = _layer_grad_halves(g, own_early)
    dq, dkv, dsink, theirs = _attention_bwd(proj, p['attn_sinks'][li], do, _SiblingSends(early) if early else None)
    g['attn_sinks'] = dsink[:, 0]
    pair = _pair_sums(early, [BF16] * len(early), core, theirs) if early else []
    sends = _ChipSends(pair, _pick_slab) if pair else None
    dus_scan, dbb, dcb, dlam, brought = _ssm_bwd(proj, s['states'], dyl, s['bb'], s['cb'], s['pw'], sends)
    g['ssm_tables'] = (dlam[:, 0:2, :], dbb, dcb)
    dup, dwp, dps_scale = _pool_bwd(proj, p['w_pool'][li].astype(BF16), _row(p['pool_scale'][li]), dyp)
    g['w_pool'] = dwp
    g['pool_scale'] = dps_scale[0]

    for off, piece in ((OFF_Q, dq), (OFF_US, dus_skip + dus_scan), (OFF_UP, dup), (OFF_KV, dkv[HALO:])):
        dproj = lax.dynamic_update_slice(dproj, piece.astype(BF16), (0, off))
    g['w_in'] = _matmul(s['h'], dproj, 'tn', tm=1024, tn=1280, tk=ROWS_WGRAD, name="grad_w_in")
    if w_in_sender is None:
        dh, brought_late = _matmul(dproj, full['w_in'], 'nt', tm=1024, tn=1024, tk=3200, name="grad_h"), None
    else:
        dh, brought_late = _matmul(dproj, full['w_in'], 'nt', tm=1024, tn=1024, tk=3200, name="grad_h_sending",
                                   sends=w_in_sender(g['w_in']))
    dx, nst = _norm_bwd(s['x'], dh, dxo, _row(p['norm_g'][li]), mod)
    g['norm_g'] = nst[2]
    dmod = jnp.concatenate([nst[0], nst[1], dgate])
    del l
    return dx, g, dmod, (pair, brought), arrived_upper, brought_late


BIG_NAMES = tuple(name for name, _ in BIG)


def _weight_sends(shards, names, li):
    return _ChipSends([shards[name][li] for name in names], _pick_weight_half, pass_on=True)


def _assemble_layer(shards, names, li, arrived, place):
    x, y, core, _ = place
    over_ici, from_sibling = arrived[:len(names)], arrived[len(names):]
    full = {}
    for name, a, b in zip(names, over_ici, from_sibling):
        own = shards[name][li]
        own_halves = own.reshape(2, own.shape[0] // 2, own.shape[1])
        halves = [_join_halves(a[j][None], b[j][None], core, 0) for j in range(3)]
        full[name] = _chips_to_full(name, _by_chip(own_halves, halves, x, y))
    return full


def _pair_sums(tensors, wire, core, theirs=None):
    if theirs is None:
        theirs = _SiblingSends(tensors).run("swap_halves")
    return [_add2(lax.dynamic_index_in_dim(g, core, 0, keepdims=False), a, dt, "sum_core_pair")
            for g, a, dt in zip(tensors, theirs, wire)]


def _layer_grad_halves(layer_grads, names):
    return [_full_to_dests(name, layer_grads[name]) for name in names]


def _step(p, m, v, x, c, target):
    depth = p['norm_g'].shape[0]
    d = D_MODEL
    ix, iy, ic = lax.axis_index("x"), lax.axis_index("y"), lax.axis_index("c")
    chip = 2 * ix + iy
    dev = 4 * ix + 2 * iy + ic
    x0 = x[0]

    c_pad = jnp.pad(c, ((0, SUBLANES - 1), (0, 0)))
    c_all = _exchange(c_pad, 'xyc', False, "gather_c")[:, 0, :]
    n_ada = p['w_ada'].shape[-1]
    b_shard = lax.dynamic_slice_in_dim(p['b_ada'], chip * n_ada, n_ada, axis=1)[:, None, :]
    mod_shard = _ada_fwd(c_all, p['w_ada'].astype(BF16), b_shard)
    mod_all = _exchange(mod_shard.reshape(depth * N_DEV, n_ada), 'xy', False, "gather_mod")
    mod_all = jnp.transpose(mod_all.reshape(N_CHIPS, depth, N_DEV, n_ada), (1, 2, 0, 3)).reshape(depth, N_DEV, 3 * d)
    mods = lax.dynamic_index_in_dim(mod_all, dev, axis=1, keepdims=True)

    place = (ix, iy, ic, chip)
    n_big = len(BIG)

    shards = {name: p[name].astype(BF16) for name, _ in BIG}
    tables, tables_vjp = _all_tables(p)
    others = BIG_NAMES[1:]
    w_in = _assemble_layer(shards, BIG_NAMES[:1], 0, _weight_sends(shards, BIG_NAMES[:1], 0).run("send_weight_halves"),
                           place)['w_in']
    rest = (_weight_sends(shards, others, 0), lambda arrived: _assemble_layer(shards, others, 0, arrived, place))
    saved = []
    xs = x0
    for li in range(depth):
        more = li + 1 < depth
        xs, s, arrived = _layer_fwd(xs, p, w_in, rest, li, mods[li], tables,
                                    _weight_sends(shards, BIG_NAMES[:1], li + 1) if more else None,
                                    _weight_sends(shards, others, li + 1) if more else None)
        saved.append(s)
        if more:
            rest = _assemble_layer(shards, BIG_NAMES, li + 1, arrived, place)
            w_in = rest.pop('w_in')
    dx, hst = _loss_head(xs, _row(p['final_g']), target[0])
    loss = lax.psum(hst[1, 0], ("x", "y", "c"))

    grads, dmods, pair, from_chips = [None] * depth, [None] * depth, [None] * depth, [None] * depth
    w_in_halves, w_in_pair, w_in_arrived = {}, {}, {}

    def w_in_sender(li):
        def to_chips(g_w_in):
            w_in_pair[li] = _pair_sums([_full_to_dests('w_in', g_w_in)], [BF16], ic)
            return _ChipSends(w_in_pair[li], _pick_slab)

        def to_sibling(g_w_in):
            w_in_halves[li] = [_full_to_dests('w_in', g_w_in)]
            return _SiblingSends(w_in_halves[li])

        return to_chips if li == 0 else to_sibling

    upper = None
    for li in reversed(range(depth)):
        dx, grads[li], dmods[li], early, arrived_upper, arrived_late = _layer_bwd(
            dx, saved[li], p, li, mods[li], ic, upper, others, w_in_sender(li))
        if upper is not None:
            w_in_arrived[li + 1] = arrived_upper
        if li > 0:
            w_in_pair[li] = _pair_sums(w_in_halves[li], [BF16], ic, arrived_late)
            upper = w_in_pair[li]
        else:
            w_in_arrived[li] = arrived_late
        pair[li] = w_in_pair[li] + early[0]
        from_chips[li] = early[1]
    for li in range(depth):
        from_chips[li] = list(w_in_arrived[li]) + list(from_chips[li])

    dmod_pad = jnp.pad(jnp.stack(dmods), ((0, SUBLANES - depth), (0, 0)))
    dmod_all = _exchange(dmod_pad, 'xyc', False, "gather_dmod")[:, :depth, :]
    dmod_cols = lax.dynamic_slice_in_dim(jnp.transpose(dmod_all, (1, 0, 2)), chip * n_ada, n_ada, axis=2)
    g_w_ada = _ada_bwd(jnp.transpose(c_all), dmod_cols)

    local_small = {k: jnp.stack([grads[li][k] for li in range(depth)])
                   for k in SMALL if k not in ('final_g', 'b_ada') + SSM_RAW}
    table_cotangents = tuple(jnp.stack([grads[li]['ssm_tables'][k] for li in range(depth)]) for k in range(3))
    local_small.update(zip(SSM_RAW, tables_vjp(table_cotangents)))
    local_small['final_g'] = hst[0]
    local_small['b_ada'] = jnp.stack(dmods)
    small_pack = _pack_rows([local_small[k] for k in SMALL], PACK_COLS, F32)

    small_halves = small_pack.reshape(2, 1, small_pack.shape[0] // 2, small_pack.shape[1])
    small_pair = _pair_sums([small_halves], [F32], ic)
    pair[0] = pair[0] + small_pair
    from_chips[0] = from_chips[0] + list(_ChipSends(small_pair, _pick_slab).run("scatter_chips"))
    halves = []
    for li in range(depth):
        for t, (ps, others) in enumerate(zip(pair[li], from_chips[li])):
            small = t == n_big
            own = ps[0] if small else lax.dynamic_index_in_dim(ps, chip, 0, keepdims=False)
            halves.append(_sum_chips(own, others, small, "sum_chips_small" if small else "sum_chips")[None])
    totals = [_join_halves(h[0], o[0], ic, 0) for h, o in zip(halves, _swap_cores(halves))]
    small_sum = totals[n_big]
    per_layer = [totals[:n_big]] + [totals[n_big + 1 + (li - 1) * n_big:n_big + 1 + li * n_big] for li in range(1, depth)]

    small_shapes = [p[k].shape for k in SMALL]
    grad = dict(zip(SMALL, _unpack_rows(small_sum, small_shapes)))
    grad['w_ada'] = g_w_ada
    for t, (name, _) in enumerate(BIG):
        grad[name] = jnp.stack([per_layer[li][t] for li in range(depth)])

    delta, new_m, new_v = {}, {}, {}
    outs = _adamw(_pack_rows([p[k] for k in SMALL], PACK_COLS, F32), small_sum,
                  _pack_rows([m[k] for k in SMALL], PACK_COLS, F32), _pack_rows([v[k] for k in SMALL], PACK_COLS, F32),
                  name="adamw_small")
    for res, o in zip((delta, new_m, new_v), outs):
        res.update(zip(SMALL, _unpack_rows(o, small_shapes)))
    for name in ['w_ada'] + [n for n, _ in BIG]:
        shape = p[name].shape
        two_d = (-1, shape[-1])
        outs = _adamw(p[name].reshape(two_d), grad[name].reshape(two_d), m[name].reshape(two_d), v[name].reshape(two_d),
                      name="adamw_" + name)
        delta[name], new_m[name], new_v[name] = (o.reshape(shape) for o in outs)

    return (loss, dx[None], *[grad[k] for k in WEIGHTS], *[delta[k] for k in WEIGHTS],
            *[new_m[k] for k in WEIGHTS], *[new_v[k] for k in WEIGHTS])


def kernel(x, c, norm_g, w_ada, b_ada, w_in, attn_sinks, ssm_a_re, ssm_a_im, ssm_log_dt, ssm_b_re, ssm_b_im, ssm_c_re, ssm_c_im, ssm_d, w_glu, b_glu, w_pool, pool_scale, w_br_att, w_br_ssm, w_br_pool, w_out, final_g, loss_target, m_norm_g, m_w_ada, m_b_ada, m_w_in, m_attn_sinks, m_ssm_a_re, m_ssm_a_im, m_ssm_log_dt, m_ssm_b_re, m_ssm_b_im, m_ssm_c_re, m_ssm_c_im, m_ssm_d, m_w_glu, m_b_glu, m_w_pool, m_pool_scale, m_w_br_att, m_w_br_ssm, m_w_br_pool, m_w_out, m_final_g, v_norm_g, v_w_ada, v_b_ada, v_w_in, v_attn_sinks, v_ssm_a_re, v_ssm_a_im, v_ssm_log_dt, v_ssm_b_re, v_ssm_b_im, v_ssm_c_re, v_ssm_c_im, v_ssm_d, v_w_glu, v_b_glu, v_w_pool, v_pool_scale, v_w_br_att, v_w_br_ssm, v_w_br_pool, v_w_out, v_final_g):
    p = dict(zip(WEIGHTS, (norm_g, w_ada, b_ada, w_in, attn_sinks, ssm_a_re, ssm_a_im, ssm_log_dt, ssm_b_re, ssm_b_im,
                           ssm_c_re, ssm_c_im, ssm_d, w_glu, b_glu, w_pool, pool_scale, w_br_att, w_br_ssm, w_br_pool,
                           w_out, final_g)))
    m = dict(zip(WEIGHTS, (m_norm_g, m_w_ada, m_b_ada, m_w_in, m_attn_sinks, m_ssm_a_re, m_ssm_a_im, m_ssm_log_dt,
                           m_ssm_b_re, m_ssm_b_im, m_ssm_c_re, m_ssm_c_im, m_ssm_d, m_w_glu, m_b_glu, m_w_pool,
                           m_pool_scale, m_w_br_att, m_w_br_ssm, m_w_br_pool, m_w_out, m_final_g)))
    v = dict(zip(WEIGHTS, (v_norm_g, v_w_ada, v_b_ada, v_w_in, v_attn_sinks, v_ssm_a_re, v_ssm_a_im, v_ssm_log_dt,
                           v_ssm_b_re, v_ssm_b_im, v_ssm_c_re, v_ssm_c_im, v_ssm_d, v_w_glu, v_b_glu, v_w_pool,
                           v_pool_scale, v_w_br_att, v_w_br_ssm, v_w_br_pool, v_w_out, v_final_g)))
    return _step(p, m, v, x, c, loss_target)
```

```python
import functools
import math

import jax
import jax.numpy as jnp
from jax import lax
from jax.experimental import pallas as pl
from jax.experimental.pallas import tpu as pltpu

F32 = jnp.float32
BF16 = jnp.bfloat16

D_MODEL = 1024
CHUNK = 64
N_HEADS = 8
N_KV_HEADS = 2
HEAD_DIM = 64
Q_PER_KV = N_HEADS // N_KV_HEADS
HALO = 128
ATT_W = 512
KV_W = 128
SSM_W = 512
SSM_GROUP = 16
SSM_GROUPS = 32
SSM_STATE = 64
POOL_W = 512
POOL_GW = 128
IN_W = 6400
EPS = 1e-6
NEG_INF = -1e30
ADAM_LR = 0.001
ADAM_B1 = 0.9
ADAM_B2 = 0.999
ADAM_EPS = 1e-08
ADAM_WD = 0.01
ADAM_STEP = 10

OFF_GL, OFF_Q, OFF_ZA, OFF_US, OFF_ZS, OFF_UP, OFF_ZP, OFF_KV = 0, 3072, 3584, 4096, 4608, 5120, 5632, 6144
_PERM_PIECES = ((3328, 3072), (0, 512), (1792, 512), (768, 512), (2304, 512), (1280, 512), (2816, 512), (512, 256))

LANES = 128
SUBLANES = 8
N_SBLK = SSM_GROUPS * SSM_STATE // LANES
VMEM_LIMIT = 48 * 1024 * 1024

N_CHIPS = 4
N_DEV = 8

WEIGHTS = ['norm_g', 'w_ada', 'b_ada', 'w_in', 'attn_sinks', 'ssm_a_re', 'ssm_a_im', 'ssm_log_dt', 'ssm_b_re',
           'ssm_b_im', 'ssm_c_re', 'ssm_c_im', 'ssm_d', 'w_glu', 'b_glu', 'w_pool', 'pool_scale', 'w_br_att',
           'w_br_ssm', 'w_br_pool', 'w_out', 'final_g']
SMALL = ['norm_g', 'b_ada', 'attn_sinks', 'ssm_a_re', 'ssm_a_im', 'ssm_log_dt', 'ssm_b_re', 'ssm_b_im', 'ssm_c_re',
         'ssm_c_im', 'ssm_d', 'b_glu', 'w_pool', 'pool_scale', 'final_g']
BIG = (('w_in', (1024, 1600)), ('w_br_att', (512, 256)), ('w_br_ssm', (512, 256)), ('w_br_pool', (512, 256)),
       ('w_out', (256, 1024)), ('w_glu', (128, 512)))
PACK_COLS = 1024


def _params(sem=None):
    return pltpu.CompilerParams(dimension_semantics=sem, vmem_limit_bytes=VMEM_LIMIT)


def _row(v):
    return v.reshape(1, -1)


def _full(shape):
    nd = len(shape)
    return pl.BlockSpec(shape, lambda *_: (0,) * nd)


def _sigmoid(v):
    return 1.0 / (1.0 + jnp.exp(-v))


def _silu_and_grad(z):
    s = _sigmoid(z)
    return z * s, s * (1.0 + z * (1.0 - s))


_GELU_K = math.sqrt(2.0 / math.pi)


def _gelu_and_grad(v):
    inner = _GELU_K * (v + 0.044715 * v * v * v)
    th = jnp.tanh(inner)
    val = 0.5 * v * (1.0 + th)
    grad = 0.5 * (1.0 + th) + 0.5 * v * (1.0 - th * th) * _GELU_K * (1.0 + 3 * 0.044715 * v * v)
    return val, grad


_NN = (((1,), (0,)), ((), ()))
_NT = (((1,), (1,)), ((), ()))
_TN = (((0,), (0,)), ((), ()))


def _dot(a, b, dims=_NN):
    return lax.dot_general(a.astype(BF16), b.astype(BF16), dims, preferred_element_type=F32)


def _matmul(a, b, mode, *, tm, tn, tk, name, out_dtype=F32, n_outer=False, sends=None):
    if mode == 'nn':
        (m, k), (_, n) = a.shape, b.shape
    elif mode == 'nt':
        (m, k), (n, _) = a.shape, b.shape
    else:
        (k, m), (_, n) = a.shape, b.shape
    tm, tn, tk = min(tm, m), min(tn, n), min(tk, k)
    assert m % tm == 0 and n % tn == 0 and k % tk == 0, (name, a.shape, b.shape)
    nk = k // tk
    dims = {'nn': _NN, 'nt': _NT, 'tn': _TN}[mode]

    def body(a_ref, b_ref, o_ref, acc_ref):
        if nk == 1:
            o_ref[...] = _dot(a_ref[...], b_ref[...], dims).astype(out_dtype)
            return
        kk = pl.program_id(2)

        @pl.when(kk == 0)
        def _():
            acc_ref[...] = jnp.zeros_like(acc_ref)

        acc_ref[...] += _dot(a_ref[...], b_ref[...], dims)

        @pl.when(kk == nk - 1)
        def _():
            o_ref[...] = acc_ref[...].astype(out_dtype)

    def spec(shape, index):
        if n_outer:
            return pl.BlockSpec(shape, lambda j, i, kk: index(i, j, kk))
        return pl.BlockSpec(shape, index)

    a_spec = spec((tk, tm), lambda i, j, kk: (kk, i)) if mode == 'tn' else spec((tm, tk), lambda i, j, kk: (i, kk))
    b_spec = spec((tn, tk), lambda i, j, kk: (j, kk)) if mode == 'nt' else spec((tk, tn), lambda i, j, kk: (kk, j))
    grid = (n // tn, m // tm, nk) if n_outer else (m // tm, n // tn, nk)
    body, x_in, x_out, x_shapes, x_scratch, x_args = _carry_sends(body, 2, 1, 1, grid, sends)
    semantics = ("parallel", "parallel", "arbitrary") if sends is None else ("arbitrary",) * 3
    outs = pl.pallas_call(
        body, name=name, grid=grid, in_specs=[a_spec, b_spec] + x_in,
        out_specs=[spec((tm, tn), lambda i, j, kk: (i, j))] + x_out,
        out_shape=[jax.ShapeDtypeStruct((m, n), out_dtype)] + x_shapes,
        scratch_shapes=[pltpu.VMEM((tm, tn), F32)] + x_scratch, compiler_params=_params(semantics))(a, b, *x_args)
    return outs[0] if sends is None else (outs[0], list(outs[1:]))


ROWS_NORM = 512


def _norm_fwd(x, g, mod):
    l, d = x.shape
    tr = min(ROWS_NORM, l)

    def body(x_ref, g_ref, mod_ref, h_ref):
        xv = x_ref[...]
        r = lax.rsqrt(jnp.mean(xv * xv, axis=-1, keepdims=True) + EPS)
        shift, scale = mod_ref[:, 0:d], mod_ref[:, d:2 * d]
        h_ref[...] = ((xv * r * g_ref[...]) * (1.0 + scale) + shift).astype(BF16)

    return pl.pallas_call(
        body, name="norm_fwd", grid=(l // tr,),
        in_specs=[pl.BlockSpec((tr, d), lambda i: (i, 0)), _full((1, d)), _full((1, 3 * d))],
        out_specs=pl.BlockSpec((tr, d), lambda i: (i, 0)), out_shape=jax.ShapeDtypeStruct((l, d), BF16),
        compiler_params=_params(("parallel",)))(x, g, mod)


def _norm_bwd(x, dh, dxo, g, mod):
    l, d = x.shape
    tr = min(ROWS_NORM, l)

    def body(x_ref, dh_ref, dxo_ref, g_ref, mod_ref, dx_ref, st_ref):
        @pl.when(pl.program_id(0) == 0)
        def _():
            st_ref[...] = jnp.zeros_like(st_ref)

        xv, dhv = x_ref[...], dh_ref[...]
        r = lax.rsqrt(jnp.mean(xv * xv, axis=-1, keepdims=True) + EPS)
        xn = xv * r
        gv = g_ref[...]
        sc1 = 1.0 + mod_ref[:, d:2 * d]
        dxn = dhv * gv * sc1
        dx_ref[...] = dxo_ref[...] + r * (dxn - xn * jnp.mean(dxn * xn, axis=-1, keepdims=True))
        st_ref[0:1, :] += jnp.sum(dhv, axis=0, keepdims=True)
        st_ref[1:2, :] += jnp.sum(dhv * xn * gv, axis=0, keepdims=True)
        st_ref[2:3, :] += jnp.sum(dhv * xn * sc1, axis=0, keepdims=True)

    blk = pl.BlockSpec((tr, d), lambda i: (i, 0))
    return pl.pallas_call(
        body, name="norm_bwd", grid=(l // tr,),
        in_specs=[blk, blk, blk, _full((1, d)), _full((1, 3 * d))],
        out_specs=[blk, _full((SUBLANES, d))],
        out_shape=[jax.ShapeDtypeStruct((l, d), F32), jax.ShapeDtypeStruct((SUBLANES, d), F32)],
        compiler_params=_params(("arbitrary",)))(x, dh, dxo, g, mod)


def _loss_head(x, g, target):
    l, d = x.shape
    tr = min(ROWS_NORM, l)

    def body(x_ref, g_ref, t_ref, dx_ref, st_ref):
        @pl.when(pl.program_id(0) == 0)
        def _():
            st_ref[...] = jnp.zeros_like(st_ref)

        xv = x_ref[...]
        r = lax.rsqrt(jnp.mean(xv * xv, axis=-1, keepdims=True) + EPS)
        xn = xv * r
        gv = g_ref[...]
        err = xn * gv - t_ref[...]
        part = 0.5 * jnp.sum(jnp.mean(err * err, axis=-1, keepdims=True), axis=0, keepdims=True)
        dy = err * (1.0 / d)
        dxn = dy * gv
        dx_ref[...] = r * (dxn - xn * jnp.mean(dxn * xn, axis=-1, keepdims=True))
        st_ref[0:1, :] += jnp.sum(dy * xn, axis=0, keepdims=True)
        st_ref[1:2, :] += jnp.broadcast_to(part, (1, d))

    blk = pl.BlockSpec((tr, d), lambda i: (i, 0))
    return pl.pallas_call(
        body, name="loss_head", grid=(l // tr,), in_specs=[blk, _full((1, d)), blk],
        out_specs=[blk, _full((SUBLANES, d))],
        out_shape=[jax.ShapeDtypeStruct((l, d), F32), jax.ShapeDtypeStruct((SUBLANES, d), F32)],
        compiler_params=_params(("arbitrary",)))(x, g, target)


ROWS_ATT = 128
ROWS_ATT_BWD = 256
_SLOPES = tuple(2.0 ** (-8.0 * (h + 1) / N_HEADS) for h in range(N_HEADS))


def _att_mask(i, t):
    r = lax.broadcasted_iota(jnp.int32, (t, t + HALO), 0)
    j = lax.broadcasted_iota(jnp.int32, (t, t + HALO), 1)
    dist = jnp.abs(r + HALO - j).astype(F32)
    rc, jc = r // CHUNK, j // CHUNK
    allowed = (jc >= rc) & (jc <= rc + 2) & ((j >= HALO) | (i > 0))
    return dist, allowed


def _att_probs(qh, k, dist, allowed, slope, sink):
    s = _dot(qh, k, _NT) * (1.0 / math.sqrt(HEAD_DIM)) - slope * dist
    s = jnp.where(allowed, s, NEG_INF)
    m = jnp.maximum(jnp.max(s, axis=1, keepdims=True), sink)
    e = jnp.exp(s - m)
    es = jnp.exp(sink - m)
    den = jnp.sum(e, axis=1, keepdims=True) + es
    return e / den, es / den


def _att_specs(t):
    q_spec = pl.BlockSpec((t, ATT_W), lambda i: (i, OFF_Q // ATT_W))
    kv_spec = pl.BlockSpec((t, 2 * KV_W), lambda i: (i, OFF_KV // (2 * KV_W)))
    halo_spec = pl.BlockSpec((HALO, 2 * KV_W), lambda i: (jnp.maximum(i * (t // HALO) - 1, 0), OFF_KV // (2 * KV_W)))
    return q_spec, kv_spec, halo_spec


def _attention_fwd(proj, sinks):
    l = proj.shape[0]
    t = min(ROWS_ATT, l)

    def body(sink_ref, q_ref, kv_ref, halo_ref, o_ref):
        dist, allowed = _att_mask(pl.program_id(0), t)
        kv = jnp.concatenate([halo_ref[...], kv_ref[...]], axis=0)
        for h in range(N_HEADS):
            kh = h // Q_PER_KV
            k = kv[:, kh * HEAD_DIM:(kh + 1) * HEAD_DIM]
            v = kv[:, KV_W + kh * HEAD_DIM:KV_W + (kh + 1) * HEAD_DIM]
            p, _ = _att_probs(q_ref[:, h * HEAD_DIM:(h + 1) * HEAD_DIM], k, dist, allowed, _SLOPES[h], sink_ref[h])
            o_ref[:, h * HEAD_DIM:(h + 1) * HEAD_DIM] = _dot(p, v)

    q_spec, kv_spec, halo_spec = _att_specs(t)
    return pl.pallas_call(
        body, name="attention_fwd", grid=(l // t,),
        in_specs=[pl.BlockSpec(memory_space=pltpu.SMEM), q_spec, kv_spec, halo_spec],
        out_specs=pl.BlockSpec((t, ATT_W), lambda i: (i, 0)), out_shape=jax.ShapeDtypeStruct((l, ATT_W), F32),
        compiler_params=_params(("parallel",)))(sinks, proj, proj, proj)


def _attention_bwd(proj, sinks, do, sends=None):
    l = proj.shape[0]
    t = min(ROWS_ATT_BWD, l)

    def body(sink_ref, q_ref, kv_ref, halo_ref, do_ref, dq_ref, dkv_ref, dsink_ref):
        i = pl.program_id(0)

        @pl.when(i == 0)
        def _():
            dkv_ref[...] = jnp.zeros_like(dkv_ref)
            dsink_ref[...] = jnp.zeros_like(dsink_ref)

        dist, allowed = _att_mask(i, t)
        kv = jnp.concatenate([halo_ref[...], kv_ref[...]], axis=0)
        rows = pl.ds(pl.multiple_of(i * t, t), t + HALO)
        for kh in range(N_KV_HEADS):
            k = kv[:, kh * HEAD_DIM:(kh + 1) * HEAD_DIM]
            v = kv[:, KV_W + kh * HEAD_DIM:KV_W + (kh + 1) * HEAD_DIM]
            dk = jnp.zeros((t + HALO, HEAD_DIM), F32)
            dv = jnp.zeros((t + HALO, HEAD_DIM), F32)
            for h in range(kh * Q_PER_KV, (kh + 1) * Q_PER_KV):
                qh = q_ref[:, h * HEAD_DIM:(h + 1) * HEAD_DIM]
                doh = do_ref[:, h * HEAD_DIM:(h + 1) * HEAD_DIM]
                p, ps = _att_probs(qh, k, dist, allowed, _SLOPES[h], sink_ref[h])
                dp = _dot(doh, v, _NT)
                delta = jnp.sum(p * dp, axis=1, keepdims=True)
                ds = p * (dp - delta) * (1.0 / math.sqrt(HEAD_DIM))
                dsink_ref[h:h + 1, :] += jnp.broadcast_to(-jnp.sum(ps * delta, axis=0, keepdims=True), (1, LANES))
                dq_ref[:, h * HEAD_DIM:(h + 1) * HEAD_DIM] = _dot(ds, k).astype(BF16)
                dk = dk + _dot(ds, qh, _TN)
                dv = dv + _dot(p, doh, _TN)
            dkv_ref[rows, kh * HEAD_DIM:(kh + 1) * HEAD_DIM] += dk
            dkv_ref[rows, KV_W + kh * HEAD_DIM:KV_W + (kh + 1) * HEAD_DIM] += dv

    q_spec, kv_spec, halo_spec = _att_specs(t)
    blk = pl.BlockSpec((t, ATT_W), lambda i: (i, 0))
    body, x_in, x_out, x_shapes, x_scratch, x_args = _carry_sends(body, 5, 3, 0, (l // t,), sends)
    outs = pl.pallas_call(
        body, name="attention_bwd" if sends is None else "attention_bwd_sending", grid=(l // t,),
        in_specs=[pl.BlockSpec(memory_space=pltpu.SMEM), q_spec, kv_spec, halo_spec, blk] + x_in,
        out_specs=[blk, _full((HALO + l, 2 * KV_W)), _full((N_HEADS, LANES))] + x_out,
        out_shape=[jax.ShapeDtypeStruct((l, ATT_W), BF16), jax.ShapeDtypeStruct((HALO + l, 2 * KV_W), F32),
                   jax.ShapeDtypeStruct((N_HEADS, LANES), F32)] + x_shapes,
        scratch_shapes=x_scratch, compiler_params=_params(("arbitrary",)))(sinks, proj, proj, proj, do, *x_args)
    return outs[0], outs[1], outs[2], list(outs[3:])


ROWS_SSM = 2048
SSM_BLOCKS = 2
SCAN_UNROLL = 2
SSM_STEPS_PER_U = LANES // (SSM_BLOCKS * 2 * SSM_GROUP)


def _ssm_discretize(a_re, a_im, log_dt, b_re, b_im):
    lam = lax.complex(a_re, a_im)
    dt = jnp.exp(log_dt)[:, None]
    lam_bar = jnp.exp(lam * dt)
    b_bar = ((lam_bar - 1.0) / lam)[..., None] * lax.complex(b_re, b_im)
    return lam, dt, lam_bar, b_bar


def _ssm_block_diag(m):
    e = m.reshape(N_SBLK, 2, SSM_GROUP, SSM_STATE)
    e = e[:, :, :, None, :] * jnp.eye(2, dtype=m.dtype)[None, :, None, :, None]
    e = e.reshape(N_SBLK, 2 * SSM_GROUP, LANES)
    oh = jax.nn.one_hot(jnp.arange(N_SBLK) % 4, 4, dtype=m.dtype)
    return (oh[:, :, None, None] * e[:, None]).reshape(N_SBLK, LANES, LANES)


def _ssm_tables(a_re, a_im, log_dt, b_re, b_im, c_re, c_im):
    _, _, lam_bar, b_bar = _ssm_discretize(a_re, a_im, log_dt, b_re, b_im)
    lam_blk = jnp.stack([jnp.real(lam_bar).reshape(N_SBLK, LANES), jnp.imag(lam_bar).reshape(N_SBLK, LANES)], axis=1)
    bt = jnp.transpose(b_bar, (0, 2, 1))
    bb = jnp.concatenate([_ssm_block_diag(jnp.real(bt)), _ssm_block_diag(jnp.imag(bt))], axis=2)
    cb = jnp.concatenate([jnp.transpose(_ssm_block_diag(c_re), (0, 2, 1)),
                          jnp.transpose(_ssm_block_diag(-c_im), (0, 2, 1))], axis=1)
    return lam_blk, bb, cb


def _ssm_powers(a_re, a_im, log_dt):
    lam = lax.complex(a_re, a_im)
    dt = jnp.exp(log_dt)[:, None]
    k = jnp.arange(1, SUBLANES + 1, dtype=F32)
    pw = jnp.exp((lam * dt)[None] * k[:, None, None]).reshape(SUBLANES, N_SBLK, LANES)
    pw = jnp.transpose(pw, (1, 0, 2))
    rev = pw[:, ::-1]
    return jnp.concatenate([jnp.real(pw), jnp.imag(pw), jnp.real(rev), jnp.imag(rev)], axis=1)


def _scan_consts(pw_ref, b, reverse):
    row = lax.broadcasted_iota(jnp.int32, (SUBLANES, LANES), 0)
    sign = -1.0 if reverse else 1.0

    def power(k):
        return (jnp.broadcast_to(pw_ref[b, k - 1:k, :], (SUBLANES, LANES)),
                sign * jnp.broadcast_to(pw_ref[b, SUBLANES + k - 1:SUBLANES + k, :], (SUBLANES, LANES)))

    steps = []
    for d in (1, 2, 4):
        pr, pi = power(d)
        keep = (row < SUBLANES - d) if reverse else (row >= d)
        steps.append((d, jnp.where(keep, pr, 0.0), jnp.where(keep, pi, 0.0)))
    base = 2 * SUBLANES if reverse else 0
    return steps, pw_ref[b, base:base + SUBLANES, :], sign * pw_ref[b, base + SUBLANES:base + 2 * SUBLANES, :]


def _scan_tile(xr, xi, steps, reverse):
    for d, ar, ai in steps:
        shift = SUBLANES - d if reverse else d
        rr, ri = pltpu.roll(xr, shift, 0), pltpu.roll(xi, shift, 0)
        xr, xi = xr + (ar * rr - ai * ri), xi + (ar * ri + ai * rr)
    return xr, xi


def _bcast_row(v, r):
    return jnp.broadcast_to(v[r:r + 1, :], (SUBLANES, LANES))


def _re_im(b):
    return slice(2 * b * LANES, (2 * b + 1) * LANES), slice((2 * b + 1) * LANES, (2 * b + 2) * LANES)


def _scan_forward(s_scr, pw_ref, n_tiles):
    consts = [_scan_consts(pw_ref, b, False) for b in range(SSM_BLOCKS)]

    def tiles(g, carry):
        carry = list(carry)
        rows = [pl.ds(pl.multiple_of((g * SCAN_UNROLL + u) * SUBLANES, SUBLANES), SUBLANES) for u in range(SCAN_UNROLL)]
        loaded = [[(s_scr[r, _re_im(b)[0]], s_scr[r, _re_im(b)[1]]) for b in range(SSM_BLOCKS)] for r in rows]
        local = [[_scan_tile(xr, xi, consts[b][0], False) for b, (xr, xi) in enumerate(per_row)] for per_row in loaded]
        for r, per_row in zip(rows, local):
            for b, (xr, xi) in enumerate(per_row):
                _, pr, pi = consts[b]
                cr, ci = carry[b]
                sr, si = xr + (pr * cr - pi * ci), xi + (pr * ci + pi * cr)
                s_scr[r, _re_im(b)[0]] = sr
                s_scr[r, _re_im(b)[1]] = si
                carry[b] = (_bcast_row(sr, SUBLANES - 1), _bcast_row(si, SUBLANES - 1))
        return tuple(carry)

    zero = jnp.zeros((SUBLANES, LANES), F32)
    lax.fori_loop(0, n_tiles // SCAN_UNROLL, tiles, ((zero, zero),) * SSM_BLOCKS)


def _row_chunks(l):
    rc = min(ROWS_SSM, l)
    return rc, l // rc


def _carry_sends(body, n_in, n_out, n_scratch, grid, sends):
    if sends is None:
        return body, [], [], [], [], []
    k, k_out = len(sends.srcs), len(sends.out_shapes)

    def carrying(*refs):
        ins, send_ins = refs[:n_in], refs[n_in:n_in + k]
        outs, send_outs = refs[n_in + k:n_in + k + n_out], refs[n_in + k + n_out:n_in + k + n_out + k_out]
        rest = refs[n_in + k + n_out + k_out:]
        scratch, sems = rest[:n_scratch], rest[n_scratch:]
        ids = [pl.program_id(axis) for axis in range(len(grid))]
        first = functools.reduce(jnp.logical_and, [i == 0 for i in ids])
        last = functools.reduce(jnp.logical_and, [i == n - 1 for i, n in zip(ids, grid)])

        @pl.when(first)
        def _():
            sends.start(send_ins, send_outs, *sems)

        body(*ins, *outs, *scratch)

        @pl.when(last)
        def _():
            sends.finish(send_ins, send_outs, *sems)

    in_specs, out_specs, scratch = sends.specs()
    return carrying, in_specs, out_specs, sends.out_shapes, scratch, sends.srcs


def _ssm_fwd(proj, bb, cb, pw, sends=None):
    l = proj.shape[0]
    rc, n_chunks = _row_chunks(l)

    def body(u_ref, bb_ref, cb_ref, pw_ref, y_ref, s_scr):
        j = pl.program_id(0)

        def fill(ci, _):
            rows = pl.ds(pl.multiple_of(ci * rc, rc), rc)
            uv = u_ref[rows, :]
            for b in range(SSM_BLOCKS):
                s_scr[rows, 2 * b * LANES:2 * (b + 1) * LANES] = _dot(uv, bb_ref[b])
            return 0

        lax.fori_loop(0, n_chunks, fill, 0)
        _scan_forward(s_scr, pw_ref, l // SUBLANES)

        @pl.when(j % SSM_STEPS_PER_U == 0)
        def _():
            y_ref[...] = jnp.zeros_like(y_ref)

        def emit(ci, _):
            rows = pl.ds(pl.multiple_of(ci * rc, rc), rc)
            for b in range(SSM_BLOCKS):
                y_ref[rows, :] += _dot(s_scr[rows, 2 * b * LANES:2 * (b + 1) * LANES], cb_ref[b])
            return 0

        lax.fori_loop(0, n_chunks, emit, 0)

    steps = N_SBLK // SSM_BLOCKS
    body, x_in, x_out, x_shapes, x_scratch, x_args = _carry_sends(body, 4, 2, 0, (steps,), sends)
    outs = pl.pallas_call(
        body, name="ssm_fwd" if sends is None else "ssm_fwd_sending", grid=(steps,),
        in_specs=[pl.BlockSpec((l, LANES), lambda j: (0, OFF_US // LANES + j // SSM_STEPS_PER_U)),
                  pl.BlockSpec((SSM_BLOCKS, LANES, 2 * LANES), lambda j: (j, 0, 0)),
                  pl.BlockSpec((SSM_BLOCKS, 2 * LANES, LANES), lambda j: (j, 0, 0)),
                  pl.BlockSpec((SSM_BLOCKS, 4 * SUBLANES, LANES), lambda j: (j, 0, 0))] + x_in,
        out_specs=[pl.BlockSpec((l, LANES), lambda j: (0, j // SSM_STEPS_PER_U)),
                   pl.BlockSpec((l, SSM_BLOCKS * 2 * LANES), lambda j: (0, j))] + x_out,
        out_shape=[jax.ShapeDtypeStruct((l, SSM_W), F32), jax.ShapeDtypeStruct((l, N_SBLK * 2 * LANES), F32)] + x_shapes,
        scratch_shapes=x_scratch, compiler_params=_params(("arbitrary",)))(proj, bb, cb, pw, *x_args)
    return outs[0], outs[1], list(outs[2:])


def _ssm_bwd(proj, states, dy, bb, cb, pw, sends=None):
    l = proj.shape[0]
    rc, n_chunks = _row_chunks(l)
    n_tiles = l // SUBLANES

    def body(u_ref, s_scr, dy_ref, bb_ref, cb_ref, pw_ref, du_ref, dbb_ref, dcb_ref, dlam_ref, a_scr):
        j = pl.program_id(0)
        dcb_ref[...] = jnp.zeros_like(dcb_ref)

        def through_c(ci, _):
            rows = pl.ds(pl.multiple_of(ci * rc, rc), rc)
            dyv = dy_ref[rows, :]
            for b in range(SSM_BLOCKS):
                lanes = slice(2 * b * LANES, 2 * (b + 1) * LANES)
                dcb_ref[b] += _dot(s_scr[rows, lanes], dyv, _TN)
                a_scr[rows, lanes] = _dot(dyv, cb_ref[b], _NT)
            return 0

        lax.fori_loop(0, n_chunks, through_c, 0)

        consts = [_scan_consts(pw_ref, b, True) for b in range(SSM_BLOCKS)]
        row = lax.broadcasted_iota(jnp.int32, (SUBLANES, LANES), 0)
        last = row == SUBLANES - 1

        def tiles(g, carry):
            carry = list(carry)
            rows = [pl.ds(pl.multiple_of((n_tiles - 1 - (g * SCAN_UNROLL + u)) * SUBLANES, SUBLANES), SUBLANES)
                    for u in range(SCAN_UNROLL)]
            loaded = [[(a_scr[r, _re_im(b)[0]], a_scr[r, _re_im(b)[1]]) for b in range(SSM_BLOCKS)] for r in rows]
            states = [[(s_scr[r, _re_im(b)[0]], s_scr[r, _re_im(b)[1]]) for b in range(SSM_BLOCKS)] for r in rows]
            local = [[_scan_tile(xr, xi, consts[b][0], True) for b, (xr, xi) in enumerate(per_row)] for per_row in loaded]
            for r, per_row, state_row in zip(rows, local, states):
                for b, ((xr, xi), (sr, si)) in enumerate(zip(per_row, state_row)):
                    _, pr, pi = consts[b]
                    cr, ci, accr, acci = carry[b]
                    gr, gi = xr + (pr * cr - pi * ci), xi + (pr * ci + pi * cr)
                    a_scr[r, _re_im(b)[0]] = gr
                    a_scr[r, _re_im(b)[1]] = gi
                    ur = jnp.where(last, cr, pltpu.roll(gr, SUBLANES - 1, 0))
                    ui = jnp.where(last, ci, pltpu.roll(gi, SUBLANES - 1, 0))
                    carry[b] = (_bcast_row(gr, 0), _bcast_row(gi, 0),
                                accr + (ur * sr + ui * si), acci + (ui * sr - ur * si))
            return tuple(carry)

        zero = jnp.zeros((SUBLANES, LANES), F32)
        done = lax.fori_loop(0, n_tiles // SCAN_UNROLL, tiles, ((zero, zero, zero, zero),) * SSM_BLOCKS)
        for b in range(SSM_BLOCKS):
            dlr = jnp.broadcast_to(jnp.sum(done[b][2], axis=0, keepdims=True), (SUBLANES, LANES))
            dli = jnp.broadcast_to(jnp.sum(done[b][3], axis=0, keepdims=True), (SUBLANES, LANES))
            dlam_ref[b] = jnp.where(row == 0, dlr, jnp.where(row == 1, dli, 0.0))

        dbb_ref[...] = jnp.zeros_like(dbb_ref)

        @pl.when(j % SSM_STEPS_PER_U == 0)
        def _():
            du_ref[...] = jnp.zeros_like(du_ref)

        def through_b(ci, _):
            rows = pl.ds(pl.multiple_of(ci * rc, rc), rc)
            uv = u_ref[rows, :]
            for b in range(SSM_BLOCKS):
                av = a_scr[rows, 2 * b * LANES:2 * (b + 1) * LANES]
                dbb_ref[b] += _dot(uv, av, _TN)
                du_ref[rows, :] += _dot(av, bb_ref[b], _NT)
            return 0

        lax.fori_loop(0, n_chunks, through_b, 0)

    steps = N_SBLK // SSM_BLOCKS
    body, x_in, x_out, x_shapes, x_scratch, x_args = _carry_sends(body, 6, 4, 1, (steps,), sends)
    outs = pl.pallas_call(
        body, name="ssm_bwd" if sends is None else "ssm_bwd_sending", grid=(steps,),
        in_specs=[pl.BlockSpec((l, LANES), lambda j: (0, OFF_US // LANES + j // SSM_STEPS_PER_U)),
                  pl.BlockSpec((l, SSM_BLOCKS * 2 * LANES), lambda j: (0, j)),
                  pl.BlockSpec((l, LANES), lambda j: (0, j // SSM_STEPS_PER_U)),
                  pl.BlockSpec((SSM_BLOCKS, LANES, 2 * LANES), lambda j: (j, 0, 0)),
                  pl.BlockSpec((SSM_BLOCKS, 2 * LANES, LANES), lambda j: (j, 0, 0)),
                  pl.BlockSpec((SSM_BLOCKS, 4 * SUBLANES, LANES), lambda j: (j, 0, 0))] + x_in,
        out_specs=[pl.BlockSpec((l, LANES), lambda j: (0, j // SSM_STEPS_PER_U)),
                   pl.BlockSpec((SSM_BLOCKS, LANES, 2 * LANES), lambda j: (j, 0, 0)),
                   pl.BlockSpec((SSM_BLOCKS, 2 * LANES, LANES), lambda j: (j, 0, 0)),
                   pl.BlockSpec((SSM_BLOCKS, SUBLANES, LANES), lambda j: (j, 0, 0))] + x_out,
        out_shape=[jax.ShapeDtypeStruct((l, SSM_W), F32), jax.ShapeDtypeStruct((N_SBLK, LANES, 2 * LANES), F32),
                   jax.ShapeDtypeStruct((N_SBLK, 2 * LANES, LANES), F32),
                   jax.ShapeDtypeStruct((N_SBLK, SUBLANES, LANES), F32)] + x_shapes,
        scratch_shapes=[pltpu.VMEM((l, SSM_BLOCKS * 2 * LANES), F32)] + x_scratch,
        compiler_params=_params(("arbitrary",)))(proj, states, dy, bb, cb, pw, *x_args)
    return outs[0], outs[1], outs[2], outs[3], list(outs[4:])


def _pool_windows(v, t, l, ahead):
    def shifted(a, d):
        if ahead:
            return jnp.where(t < l - d, pltpu.roll(a, l - d, 0), 0.0)
        return jnp.where(t >= d, pltpu.roll(a, d, 0), 0.0)

    w2 = v + shifted(v, 1)
    w4 = w2 + shifted(w2, 2)
    w8 = w4 + shifted(w4, 4)
    w16 = w8 + shifted(w8, 8)
    return w2, w4, w8, w16


def _pool_select(g, ws):
    return jnp.where(g == 0, ws[0], jnp.where(g == 1, ws[1], jnp.where(g == 2, ws[2], ws[3])))


def _pool_count(g, t):
    return jnp.minimum(t + 1, jnp.left_shift(2, g)).astype(F32)


def _pool_specs(l):
    return [pl.BlockSpec((l, POOL_GW), lambda g: (0, OFF_UP // POOL_GW + g)),
            pl.BlockSpec((1, POOL_GW, POOL_GW), lambda g: (g, 0, 0)),
            pl.BlockSpec((1, POOL_GW), lambda g: (0, g))]


def _pool_fwd(proj, w_pool, scale):
    l = proj.shape[0]

    def body(u_ref, w_ref, sc_ref, y_ref):
        g = pl.program_id(0)
        t = lax.broadcasted_iota(jnp.int32, (l, 1), 0)
        u = u_ref[...]
        pooled = _pool_select(g, _pool_windows(u, t, l, False)) / _pool_count(g, t) - u
        y_ref[...] = _dot(pooled, w_ref[0]) * sc_ref[...]

    return pl.pallas_call(
        body, name="pool_fwd", grid=(4,), in_specs=_pool_specs(l),
        out_specs=pl.BlockSpec((l, POOL_GW), lambda g: (0, g)), out_shape=jax.ShapeDtypeStruct((l, POOL_W), F32),
        compiler_params=_params(("parallel",)))(proj, w_pool, scale)


def _pool_bwd(proj, w_pool, scale, dy):
    l = proj.shape[0]

    def body(u_ref, w_ref, sc_ref, dy_ref, du_ref, dw_ref, dsc_ref):
        g = pl.program_id(0)
        t = lax.broadcasted_iota(jnp.int32, (l, 1), 0)
        u = u_ref[...]
        cnt = _pool_count(g, t)
        pooled = _pool_select(g, _pool_windows(u, t, l, False)) / cnt - u
        dyv = dy_ref[...]
        dsc_ref[...] = jnp.sum(dyv * _dot(pooled, w_ref[0]), axis=0, keepdims=True)
        dyl = dyv * sc_ref[...]
        dw_ref[0] = _dot(pooled, dyl, _TN)
        dpooled = _dot(dyl, w_ref[0], _NT)
        du_ref[...] = (_pool_select(g, _pool_windows(dpooled / cnt, t, l, True)) - dpooled).astype(BF16)

    return pl.pallas_call(
        body, name="pool_bwd", grid=(4,), in_specs=_pool_specs(l) + [pl.BlockSpec((l, POOL_GW), lambda g: (0, g))],
        out_specs=[pl.BlockSpec((l, POOL_GW), lambda g: (0, g)), pl.BlockSpec((1, POOL_GW, POOL_GW), lambda g: (g, 0, 0)),
                   pl.BlockSpec((1, POOL_GW), lambda g: (0, g))],
        out_shape=[jax.ShapeDtypeStruct((l, POOL_W), BF16), jax.ShapeDtypeStruct((4, POOL_GW, POOL_GW), F32),
                   jax.ShapeDtypeStruct((1, POOL_W), F32)],
        compiler_params=_params(("parallel",)))(proj, w_pool, scale, dy)


ROWS_MERGE = 128
ROWS_MERGE_FWD = 256


def _merge_inputs(tr):
    def col(off, w):
        return pl.BlockSpec((tr, w), lambda i: (i, off // w))

    def act(w):
        return pl.BlockSpec((tr, w), lambda i: (i, 0))

    d = D_MODEL
    return ([col(OFF_GL, 3 * d), col(OFF_ZA, ATT_W), col(OFF_US, SSM_W), col(OFF_ZS, SSM_W), col(OFF_ZP, POOL_W),
             act(ATT_W), act(SSM_W), act(POOL_W)]
            + [_full((1, SSM_W)), _full((SSM_W, SSM_W)), _full((1, SSM_W)), _full((ATT_W, d)), _full((SSM_W, d)),
               _full((POOL_W, d)), _full((d, d)), _full((1, 3 * d))])


def _merge_forward_math(gl_ref, za_ref, us_ref, zs_ref, zp_ref, o_ref, yl_ref, yp_ref, d_ref, wg_ref, bg_ref,
                        wba_ref, wbs_ref, wbp_ref):
    d = D_MODEL
    r = {}
    r['sa'], r['dsa'] = _silu_and_grad(za_ref[...])
    r['ss'], r['dss'] = _silu_and_grad(zs_ref[...])
    r['sp'], r['dsp'] = _silu_and_grad(zp_ref[...])
    r['y_att'] = o_ref[...] * r['sa']
    y1 = yl_ref[...] + d_ref[...] * us_ref[...]
    r['y2'], r['dgelu'] = _gelu_and_grad(y1)
    r['sg'] = _sigmoid(_dot(r['y2'], wg_ref[...]) + bg_ref[...])
    r['y3'] = r['y2'] * r['sg']
    r['y_ssm'] = r['y3'] * r['ss']
    r['y_pool'] = yp_ref[...] * r['sp']
    r['g'] = [_sigmoid(gl_ref[:, b * d:(b + 1) * d]) for b in range(3)]
    r['p'] = [_dot(r['y_att'], wba_ref[...]), _dot(r['y_ssm'], wbs_ref[...]), _dot(r['y_pool'], wbp_ref[...])]
    r['merged'] = r['g'][0] * r['p'][0] + r['g'][1] * r['p'][1] + r['g'][2] * r['p'][2]
    return r


def _merge_fwd(proj, o, y_lin, y_pool, x, ssm_d, w_glu, b_glu, wba, wbs, wbp, w_out, mod, sends=None):
    l, d = x.shape
    tr = min(ROWS_MERGE_FWD, l)

    def body(gl_ref, za_ref, us_ref, zs_ref, zp_ref, o_ref, yl_ref, yp_ref, d_ref, wg_ref, bg_ref, wba_ref, wbs_ref,
             wbp_ref, wo_ref, mod_ref, x_ref, xn_ref):
        r = _merge_forward_math(gl_ref, za_ref, us_ref, zs_ref, zp_ref, o_ref, yl_ref, yp_ref, d_ref, wg_ref, bg_ref,
                                wba_ref, wbs_ref, wbp_ref)
        xn_ref[...] = x_ref[...] + mod_ref[:, 2 * d:3 * d] * _dot(r['merged'], wo_ref[...])

    blk = pl.BlockSpec((tr, d), lambda i: (i, 0))
    in_specs = _merge_inputs(tr) + [blk]
    body, x_in, x_out, x_shapes, x_scratch, x_args = _carry_sends(body, len(in_specs), 1, 0, (l // tr,), sends)
    outs = pl.pallas_call(
        body, name="merge_fwd" if sends is None else "merge_fwd_sending", grid=(l // tr,), in_specs=in_specs + x_in,
        out_specs=[blk] + x_out, out_shape=[jax.ShapeDtypeStruct((l, d), F32)] + x_shapes, scratch_shapes=x_scratch,
        compiler_params=_params(("arbitrary",)))(
            proj, proj, proj, proj, proj, o, y_lin, y_pool, ssm_d, w_glu, b_glu, wba, wbs, wbp, w_out, mod, x, *x_args)
    return outs[0], list(outs[1:])


def _merge_bwd(proj, o, y_lin, y_pool, dxo, ssm_d, w_glu, b_glu, wba, wbs, wbp, w_out, mod, sends=None):
    l, d = dxo.shape
    tr = min(ROWS_MERGE, l)

    def body(gl_ref, za_ref, us_ref, zs_ref, zp_ref, o_ref, yl_ref, yp_ref, d_ref, wg_ref, bg_ref, wba_ref, wbs_ref,
             wbp_ref, wo_ref, mod_ref, dxo_ref,
             do_ref, dyl_ref, dus_ref, dyp_ref, dpj_ref,
             ya_ref, ys_ref, ypl_ref, y2_ref, dt_ref, dpa_ref, dps_ref, dpp_ref, mg_ref, dout_ref, st_ref):
        @pl.when(pl.program_id(0) == 0)
        def _():
            st_ref[...] = jnp.zeros_like(st_ref)

        for off, width in ((OFF_Q, ATT_W), (OFF_US, SSM_W), (OFF_UP, POOL_W), (OFF_KV, 2 * KV_W)):
            dpj_ref[:, off:off + width] = jnp.zeros((tr, width), BF16)

        r = _merge_forward_math(gl_ref, za_ref, us_ref, zs_ref, zp_ref, o_ref, yl_ref, yp_ref, d_ref, wg_ref, bg_ref,
                                wba_ref, wbs_ref, wbp_ref)
        dxov = dxo_ref[...]
        out = _dot(r['merged'], wo_ref[...])
        st_ref[0:1, :] += jnp.sum(dxov * out, axis=0, keepdims=True)
        dout = dxov * mod_ref[:, 2 * d:3 * d]
        dmerged = _dot(dout, wo_ref[...], _NT)
        dys = []
        for b, (w_ref, dp_ref) in enumerate(((wba_ref, dpa_ref), (wbs_ref, dps_ref), (wbp_ref, dpp_ref))):
            gb = r['g'][b]
            dpj_ref[:, OFF_GL + b * d:OFF_GL + (b + 1) * d] = (dmerged * r['p'][b] * gb * (1.0 - gb)).astype(BF16)
            dp = dmerged * gb
            dp_ref[...] = dp.astype(BF16)
            dys.append(_dot(dp, w_ref[...], _NT))
        mg_ref[...] = r['merged'].astype(BF16)
        dout_ref[...] = dout.astype(BF16)
        ya_ref[...] = r['y_att'].astype(BF16)
        ys_ref[...] = r['y_ssm'].astype(BF16)
        ypl_ref[...] = r['y_pool'].astype(BF16)
        y2_ref[...] = r['y2'].astype(BF16)
        do_ref[...] = (dys[0] * r['sa']).astype(BF16)
        dpj_ref[:, OFF_ZA:OFF_ZA + ATT_W] = (dys[0] * o_ref[...] * r['dsa']).astype(BF16)
        dy3 = dys[1] * r['ss']
        dpj_ref[:, OFF_ZS:OFF_ZS + SSM_W] = (dys[1] * r['y3'] * r['dss']).astype(BF16)
        dt = dy3 * r['y2'] * r['sg'] * (1.0 - r['sg'])
        dt_ref[...] = dt.astype(BF16)
        dy1 = (dy3 * r['sg'] + _dot(dt, wg_ref[...], _NT)) * r['dgelu']
        dyl_ref[...] = dy1.astype(BF16)
        dus_ref[...] = dy1 * d_ref[...]
        st_ref[1:2, 0:SSM_W] += jnp.sum(dt, axis=0, keepdims=True)
        st_ref[1:2, SSM_W:2 * SSM_W] += jnp.sum(dy1 * us_ref[...], axis=0, keepdims=True)
        dyp_ref[...] = dys[2] * r['sp']
        dpj_ref[:, OFF_ZP:OFF_ZP + POOL_W] = (dys[2] * yp_ref[...] * r['dsp']).astype(BF16)

    blk = pl.BlockSpec((tr, d), lambda i: (i, 0))
    half = pl.BlockSpec((tr, ATT_W), lambda i: (i, 0))
    wide = pl.BlockSpec((tr, IN_W), lambda i: (i, 0))
    sds = jax.ShapeDtypeStruct
    in_specs = _merge_inputs(tr) + [blk]
    out_specs = [half] * 4 + [wide] + [half] * 5 + [blk] * 5 + [_full((SUBLANES, d))]
    body, x_in, x_out, x_shapes, x_scratch, x_args = _carry_sends(body, len(in_specs), len(out_specs), 0, (l // tr,), sends)
    outs = pl.pallas_call(
        body, name="merge_bwd" if sends is None else "merge_bwd_sending", grid=(l // tr,), in_specs=in_specs + x_in,
        out_specs=out_specs + x_out,
        out_shape=[sds((l, ATT_W), t) for t in (BF16, BF16, F32, F32)] + [sds((l, IN_W), BF16)]
        + [sds((l, ATT_W), BF16)] * 5 + [sds((l, d), BF16)] * 5 + [sds((SUBLANES, d), F32)] + x_shapes,
        scratch_shapes=x_scratch, compiler_params=_params(("arbitrary",)))(
            proj, proj, proj, proj, proj, o, y_lin, y_pool, ssm_d, w_glu, b_glu, wba, wbs, wbp, w_out, mod, dxo, *x_args)
    return list(outs[:len(out_specs)]), list(outs[len(out_specs):])


def _ada_fwd(c_all, w_ada, b_shard):
    depth, d, n = w_ada.shape

    def body(c_ref, w_ref, b_ref, o_ref):
        act, _ = _silu_and_grad(c_ref[...])
        o_ref[0] = _dot(act, w_ref[0]) + b_ref[0]

    return pl.pallas_call(
        body, name="ada_fwd", grid=(depth,),
        in_specs=[_full((N_DEV, d)), pl.BlockSpec((1, d, n), lambda i: (i, 0, 0)), pl.BlockSpec((1, 1, n), lambda i: (i, 0, 0))],
        out_specs=pl.BlockSpec((1, N_DEV, n), lambda i: (i, 0, 0)), out_shape=jax.ShapeDtypeStruct((depth, N_DEV, n), F32),
        compiler_params=_params(("parallel",)))(c_all, w_ada, b_shard)


def _ada_bwd(c_all_t, dmod):
    d = c_all_t.shape[0]
    depth, _, n = dmod.shape

    def body(c_ref, dm_ref, o_ref):
        act, _ = _silu_and_grad(c_ref[...])
        acc = act[:, 0:1] * dm_ref[0, 0:1, :]
        for b in range(1, N_DEV):
            acc = acc + act[:, b:b + 1] * dm_ref[0, b:b + 1, :]
        o_ref[0] = acc

    return pl.pallas_call(
        body, name="ada_bwd", grid=(depth,),
        in_specs=[_full((d, N_DEV)), pl.BlockSpec((1, N_DEV, n), lambda i: (i, 0, 0))],
        out_specs=pl.BlockSpec((1, d, n), lambda i: (i, 0, 0)), out_shape=jax.ShapeDtypeStruct((depth, d, n), F32),
        compiler_params=_params(("parallel",)))(c_all_t, dmod)


ROWS_FLAT = 256


_RELATION_XOR = (0, 2, 1, 3)


def _sum_chips(own, others, chip_order, name):
    r, c = own.shape
    tr = math.gcd(ROWS_FLAT, r)

    def body(own_ref, oth_ref, o_ref):
        terms = [own_ref[...].astype(F32)] + [oth_ref[k].astype(F32) for k in range(3)]
        if chip_order:
            chip = 2 * lax.axis_index("x") + lax.axis_index("y")
            by_chip = []
            for q in range(N_CHIPS):
                rel = jnp.bitwise_xor(chip, q)
                pick = terms[3]
                for k in (2, 1, 0):
                    pick = jnp.where(rel == _RELATION_XOR[k], terms[k], pick)
                by_chip.append(pick)
            terms = by_chip
        o_ref[...] = ((terms[0] + terms[1]) + terms[2]) + terms[3]

    return pl.pallas_call(
        body, name=name, grid=(r // tr,),
        in_specs=[pl.BlockSpec((tr, c), lambda i: (i, 0)), pl.BlockSpec((3, tr, c), lambda i: (0, i, 0))],
        out_specs=pl.BlockSpec((tr, c), lambda i: (i, 0)), out_shape=jax.ShapeDtypeStruct((r, c), F32),
        compiler_params=_params(("parallel",)))(own, others)


def _add2(a, b, out_dtype, name):
    shape = a.shape
    a, b = a.reshape(-1, shape[-1]), b.reshape(-1, shape[-1])
    r, c = a.shape
    tr = math.gcd(ROWS_FLAT, r)

    def body(a_ref, b_ref, o_ref):
        o_ref[...] = (a_ref[...] + b_ref[...]).astype(out_dtype)

    blk = pl.BlockSpec((tr, c), lambda i: (i, 0))
    return pl.pallas_call(
        body, name=name, grid=(r // tr,), in_specs=[blk, blk], out_specs=blk,
        out_shape=jax.ShapeDtypeStruct((r, c), out_dtype), compiler_params=_params(("parallel",)))(a, b).reshape(shape)


def _adamw(w, g, m, v, name):
    r, c = w.shape
    tr = math.gcd(ROWS_FLAT, r)

    def body(w_ref, g_ref, m_ref, v_ref, d_ref, nm_ref, nv_ref):
        gv = g_ref[...]
        mv = ADAM_B1 * m_ref[...] + (1.0 - ADAM_B1) * gv
        vv = ADAM_B2 * v_ref[...] + (1.0 - ADAM_B2) * (gv * gv)
        m_hat = mv / (1.0 - ADAM_B1 ** ADAM_STEP)
        v_hat = vv / (1.0 - ADAM_B2 ** ADAM_STEP)
        d_ref[...] = -ADAM_LR * (m_hat / (jnp.sqrt(v_hat) + ADAM_EPS) + ADAM_WD * w_ref[...])
        nm_ref[...] = mv
        nv_ref[...] = vv

    blk = pl.BlockSpec((tr, c), lambda i: (i, 0))
    return pl.pallas_call(
        body, name=name, grid=(r // tr,), in_specs=[blk] * 4, out_specs=[blk] * 3,
        out_shape=[jax.ShapeDtypeStruct((r, c), F32)] * 3, compiler_params=_params(("parallel",)))(w, g, m, v)


_GROUP_MASKS = {
    'xy': ((1, 0, 0), (0, 1, 0), (1, 1, 0)),
    'xyc': ((0, 0, 1), (0, 1, 0), (0, 1, 1), (1, 0, 0), (1, 0, 1), (1, 1, 0), (1, 1, 1)),
}


def _group_rank(group, pos):
    x, y, c = pos
    return {'xy': 2 * x + y, 'xyc': 4 * x + 2 * y + c}[group]


def _gather(src, group, name):
    masks = _GROUP_MASKS[group]
    n = len(masks) + 1

    def body(src_ref, out_ref, send_sems, recv_sems, local_sem):
        me = (lax.axis_index("x"), lax.axis_index("y"), lax.axis_index("c"))
        my_rank = _group_rank(group, me)
        local = pltpu.make_async_copy(src_ref, out_ref.at[my_rank], local_sem)
        local.start()
        sends = []
        for k, mask in enumerate(masks):
            peer = tuple(1 - p if f else p for p, f in zip(me, mask))
            send = _remote(src_ref, out_ref.at[my_rank], send_sems.at[k], recv_sems.at[k], peer)
            send.start()
            sends.append((send, peer))
        for k, (send, peer) in enumerate(sends):
            _remote(src_ref, out_ref.at[_group_rank(group, peer)], send_sems.at[k], recv_sems.at[k], peer).wait_recv()
        for send, _ in sends:
            send.wait_send()
        local.wait()

    return pl.pallas_call(
        body, name=name, in_specs=[pl.BlockSpec(memory_space=pl.ANY)], out_specs=pl.BlockSpec(memory_space=pl.ANY),
        out_shape=jax.ShapeDtypeStruct((n,) + tuple(src.shape), src.dtype),
        scratch_shapes=[pltpu.SemaphoreType.DMA((n - 1,)), pltpu.SemaphoreType.DMA((n - 1,)), pltpu.SemaphoreType.DMA(())],
    )(src)


CHUNK_BYTES = 1 << 20
MIN_CHUNK_ROWS = 64


def _row_parts(rows, cols, itemsize):
    n = 1
    while rows % (2 * n) == 0 and rows // (2 * n) >= MIN_CHUNK_ROWS and rows * cols * itemsize > n * CHUNK_BYTES:
        n *= 2
    return n


def _remote(src, dst, send_sem, recv_sem, peer):
    return pltpu.make_async_remote_copy(src_ref=src, dst_ref=dst, send_sem=send_sem, recv_sem=recv_sem,
                                        device_id=peer, device_id_type=pl.DeviceIdType.MESH)


def _start_rows(src, dst, send_sem, recv_sem, peer):
    rows, cols = src.shape
    n = _row_parts(rows, cols, jnp.dtype(src.dtype).itemsize)
    pr = rows // n
    for i in range(n):
        _remote(src.at[pl.ds(i * pr, pr), :], dst.at[pl.ds(i * pr, pr), :], send_sem, recv_sem, peer).start()


def _mesh_place():
    x, y, c = lax.axis_index("x"), lax.axis_index("y"), lax.axis_index("c")
    other_chips = ((1 - x, y), (x, 1 - y), (1 - x, 1 - y))
    return x, y, c, 2 * x + y, (x, y, 1 - c), other_chips


def _comm_call(body, name, ins, out_shapes, sem_counts):
    any_spec = pl.BlockSpec(memory_space=pl.ANY)
    return pl.pallas_call(
        body, name=name, in_specs=[any_spec] * len(ins), out_specs=[any_spec] * len(out_shapes), out_shape=out_shapes,
        scratch_shapes=[pltpu.SemaphoreType.DMA((n,)) for n in sem_counts])(*ins)


class _Sends:
    sem_arrays = 2

    def run(self, name):
        k_in, k_out = len(self.srcs), len(self.out_shapes)

        def body(*refs):
            ins, outs, sems = refs[:k_in], refs[k_in:k_in + k_out], refs[k_in + k_out:]
            self.start(ins, outs, *sems)
            self.finish(ins, outs, *sems)

        return _comm_call(body, name, self.srcs, self.out_shapes, (self.n_sems,) * self.sem_arrays)

    def specs(self):
        any_spec = pl.BlockSpec(memory_space=pl.ANY)
        return ([any_spec] * len(self.srcs), [any_spec] * len(self.out_shapes),
                [pltpu.SemaphoreType.DMA((self.n_sems,))] * self.sem_arrays)


class _SiblingSends(_Sends):
    def __init__(self, srcs):
        self.srcs = list(srcs)
        self.n_sems = len(self.srcs)
        self.out_shapes = [jax.ShapeDtypeStruct(g.shape[1:], g.dtype) for g in self.srcs]

    def start(self, ins, outs, send_sems, recv_sems):
        _, _, c, _, sibling, _ = _mesh_place()
        for t in range(len(ins)):
            for q in range(ins[t].shape[1]):
                _start_rows(ins[t].at[1 - c, q], outs[t].at[q], send_sems.at[t], recv_sems.at[t], sibling)

    def finish(self, ins, outs, send_sems, recv_sems):
        _, _, _, _, sibling, _ = _mesh_place()
        for t in range(len(ins)):
            arrived = _remote(outs[t], outs[t], send_sems.at[t], recv_sems.at[t], sibling)
            arrived.wait_recv()
            arrived.wait_send()


class _ChipSends(_Sends):
    def __init__(self, srcs, pick, pass_on=False):
        self.srcs, self.pick, self.pass_on = list(srcs), pick, pass_on
        nt = len(self.srcs)
        self.n_sems = 3 * nt
        self.sem_arrays = 4 if pass_on else 2
        landing = []
        for s in self.srcs:
            r2 = s.shape[-2] // 2 if pick is _pick_weight_half else s.shape[-2]
            landing.append(jax.ShapeDtypeStruct((3, r2, s.shape[-1]), s.dtype))
        self.out_shapes = landing + (landing if pass_on else [])

    def start(self, ins, outs, send_sems, recv_sems, *unused):
        _, _, c, _, _, other_chips = _mesh_place()
        for t in range(len(ins)):
            for j, (px, py) in enumerate(other_chips):
                _remote(self.pick(ins[t], c, 2 * px + py), outs[t].at[j], send_sems.at[3 * t + j], recv_sems.at[3 * t + j],
                        (px, py, c)).start()

    def finish(self, ins, outs, send_sems, recv_sems, *sibling_sems):
        _, _, c, _, sibling, other_chips = _mesh_place()
        nt = len(ins)
        for t in range(nt):
            for j, (px, py) in enumerate(other_chips):
                k = 3 * t + j
                landed = _remote(outs[t].at[j], outs[t].at[j], send_sems.at[k], recv_sems.at[k], (px, py, c))
                landed.wait_recv()
                if self.pass_on:
                    _start_rows(outs[t].at[j], outs[nt + t].at[j], sibling_sems[0].at[k], sibling_sems[1].at[k], sibling)
                landed.wait_send()
        if self.pass_on:
            for t in range(nt):
                for j in range(3):
                    k = 3 * t + j
                    passed = _remote(outs[nt + t].at[j], outs[nt + t].at[j], sibling_sems[0].at[k], sibling_sems[1].at[k],
                                     sibling)
                    passed.wait_recv()
                    passed.wait_send()


def _pick_weight_half(src, core, chip):
    r2 = src.shape[0] // 2
    return src.at[pl.ds(pl.multiple_of(core * r2, MIN_CHUNK_ROWS), r2), :]


def _pick_slab(src, core, chip):
    return src.at[chip if src.shape[0] == N_CHIPS else 0]


def _swap_cores(halves):
    nt = len(halves)

    def body(*refs):
        ins, outs = refs[:nt], refs[nt:2 * nt]
        send_sems, recv_sems = refs[2 * nt:]
        _, _, _, _, sibling, _ = _mesh_place()
        for t in range(nt):
            for li in range(ins[t].shape[0]):
                _start_rows(ins[t].at[li], outs[t].at[li], send_sems.at[t], recv_sems.at[t], sibling)
        for t in range(nt):
            _remote(outs[t], outs[t], send_sems.at[t], recv_sems.at[t], sibling).wait_recv()
        for t in range(nt):
            _remote(ins[t], ins[t], send_sems.at[t], recv_sems.at[t], sibling).wait_send()

    out_shapes = [jax.ShapeDtypeStruct(h.shape, h.dtype) for h in halves]
    return _comm_call(body, "swap_cores", halves, out_shapes, (nt, nt))


def _pack_rows(pieces, cols, dtype):
    tile = SUBLANES * cols
    rows = []
    for p in pieces:
        flat = p.reshape(-1).astype(dtype)
        rows.append(jnp.pad(flat, (0, (-flat.shape[0]) % tile)).reshape(-1, cols))
    total = sum(r.shape[0] for r in rows)
    if total % (2 * SUBLANES):
        rows.append(jnp.zeros((SUBLANES, cols), dtype))
    return jnp.concatenate(rows, axis=0)


def _unpack_rows(buf, shapes):
    out, row = [], 0
    for s in shapes:
        n = math.prod(s)
        nrows = -(-n // (SUBLANES * buf.shape[1])) * SUBLANES
        out.append(buf[row:row + nrows].reshape(-1)[:n].reshape(s))
        row += nrows
    return out


ROW_SHARDED = ('w_out', 'w_glu')


def _chips_to_full(name, chips):
    mats = [c.reshape(-1, c.shape[-1]) for c in chips]
    if name in ROW_SHARDED:
        return jnp.concatenate(mats, axis=0)
    if name != 'w_in':
        return jnp.concatenate(mats, axis=1)
    cs = mats[0].shape[1]
    pieces = []
    for start, size in _PERM_PIECES:
        lo = start
        while lo < start + size:
            q = lo // cs
            hi = min(start + size, (q + 1) * cs)
            pieces.append(mats[q][:, lo - q * cs:hi - q * cs])
            lo = hi
    return jnp.concatenate(pieces, axis=1)


def _full_to_dests(name, full):
    rows, cols = full.shape
    if name in ROW_SHARDED:
        return jnp.transpose(full.reshape(N_CHIPS, 2, rows // (2 * N_CHIPS), cols), (1, 0, 2, 3))
    cs = cols // N_CHIPS
    if name != 'w_in':
        dests = [full[:, q * cs:(q + 1) * cs] for q in range(N_CHIPS)]
    else:
        offsets, off = [], 0
        for _, size in _PERM_PIECES:
            offsets.append(off)
            off += size
        by_start = sorted(zip(_PERM_PIECES, offsets))
        dests = []
        for q in range(N_CHIPS):
            parts = []
            for (start, size), at in by_start:
                lo, hi = max(start, q * cs), min(start + size, (q + 1) * cs)
                if lo < hi:
                    parts.append(full[:, at + lo - start:at + hi - start])
            dests.append(jnp.concatenate(parts, axis=1))
    return jnp.stack([d.reshape(2, rows // 2, cs) for d in dests], axis=1)


def _by_chip(own, related, x, y):
    grid = ((own, related[1]), (related[0], related[2]))
    along_x = [[jnp.where(x == 0, grid[px][dy], grid[1 - px][dy]) for dy in range(2)] for px in range(2)]
    return [jnp.where(y == 0, along_x[px][py], along_x[px][1 - py]) for px in range(2) for py in range(2)]


def _join_halves(mine, theirs, core, axis):
    return jnp.where(core == 0, jnp.concatenate([mine, theirs], axis=axis), jnp.concatenate([theirs, mine], axis=axis))


ROWS_WGRAD = 2048
SSM_RAW = ('ssm_a_re', 'ssm_a_im', 'ssm_log_dt', 'ssm_b_re', 'ssm_b_im', 'ssm_c_re', 'ssm_c_im')


def _all_tables(p):
    raw = tuple(p[k] for k in SSM_RAW)
    (_, bb, cb), vjp = jax.vjp(jax.vmap(_ssm_tables), *raw)
    return (bb.astype(BF16), cb.astype(BF16), jax.vmap(_ssm_powers)(*raw[:3])), vjp


def _layer_fwd(x, p, w_in, rest, li, mod, tables, sends=None, late_sends=None):
    bb, cb, pw = (t[li] for t in tables)
    h = _norm_fwd(x, _row(p['norm_g'][li]), mod)
    if isinstance(rest, dict):
        proj = _matmul(h, w_in, 'nn', tm=1024, tn=1280, tk=1024, name="proj", n_outer=True)
    else:
        proj, arrived = _matmul(h, w_in, 'nn', tm=1024, tn=1280, tk=1024, name="proj_sending", n_outer=True, sends=rest[0])
        rest = rest[1](arrived)
    full = dict(rest, w_in=w_in)
    o = _attention_fwd(proj, p['attn_sinks'][li])
    y_lin, states, brought = _ssm_fwd(proj, bb, cb, pw, sends)
    y_pool = _pool_fwd(proj, p['w_pool'][li].astype(BF16), _row(p['pool_scale'][li]))
    x_new, brought_late = _merge_fwd(proj, o, y_lin, y_pool, x, _row(p['ssm_d'][li]), full['w_glu'],
                                     _row(p['b_glu'][li]), full['w_br_att'], full['w_br_ssm'], full['w_br_pool'],
                                     full['w_out'], mod, late_sends)
    saved = dict(x=x, h=h, proj=proj, o=o, y_lin=y_lin, y_pool=y_pool, states=states, bb=bb, cb=cb, pw=pw, full=full)
    a, b = len(brought) // 2, len(brought_late) // 2
    return x_new, saved, brought[:a] + brought_late[:b] + brought[a:] + brought_late[b:]


def _layer_bwd(dxo, s, p, li, mod, core, upper=None, own_early=(), w_in_sender=None):
    l = dxo.shape[0]
    proj, full = s['proj'], s['full']
    outs, arrived_upper = _merge_bwd(
        proj, s['o'], s['y_lin'], s['y_pool'], dxo, _row(p['ssm_d'][li]), full['w_glu'], _row(p['b_glu'][li]),
        full['w_br_att'], full['w_br_ssm'], full['w_br_pool'], full['w_out'], mod,
        None if upper is None else _ChipSends(upper, _pick_slab))
    (do, dyl, dus_skip, dyp, dproj, ya, ys, ypl, y2, dt, dpa, dps, dpp, mg, dout, st) = outs
    g = {}
    g['w_br_att'] = _matmul(ya, dpa, 'tn', tm=512, tn=1024, tk=ROWS_WGRAD, name="grad_w_br")
    g['w_br_ssm'] = _matmul(ys, dps, 'tn', tm=512, tn=1024, tk=ROWS_WGRAD, name="grad_w_br")
    g['w_br_pool'] = _matmul(ypl, dpp, 'tn', tm=512, tn=1024, tk=ROWS_WGRAD, name="grad_w_br")
    g['w_out'] = _matmul(mg, dout, 'tn', tm=512, tn=1024, tk=ROWS_WGRAD, name="grad_w_out")
    g['w_glu'] = _matmul(y2, dt, 'tn', tm=512, tn=512, tk=ROWS_WGRAD, name="grad_w_glu")
    g['b_glu'] = st[1, 0:SSM_W]
    g['ssm_d'] = st[1, SSM_W:2 * SSM_W]
    dgate = st[0]

    early = _layer_grad_halves(g, own_early)
    dq, dkv, dsink, theirs = _attention_bwd(proj, p['attn_sinks'][li], do, _SiblingSends(early) if early else None)
    g['attn_sinks'] = dsink[:, 0]
    pair = _pair_sums(early, [BF16] * len(early), core, theirs) if early else []
    sends = _ChipSends(pair, _pick_slab) if pair else None
    dus_scan, dbb, dcb, dlam, brought = _ssm_bwd(proj, s['states'], dyl, s['bb'], s['cb'], s['pw'], sends)
    g['ssm_tables'] = (dlam[:, 0:2, :], dbb, dcb)
    dup, dwp, dps_scale = _pool_bwd(proj, p['w_pool'][li].astype(BF16), _row(p['pool_scale'][li]), dyp)
    g['w_pool'] = dwp
    g['pool_scale'] = dps_scale[0]

    for off, piece in ((OFF_Q, dq), (OFF_US, dus_skip + dus_scan), (OFF_UP, dup), (OFF_KV, dkv[HALO:])):
        dproj = lax.dynamic_update_slice(dproj, piece.astype(BF16), (0, off))
    g['w_in'] = _matmul(s['h'], dproj, 'tn', tm=1024, tn=1280, tk=ROWS_WGRAD, name="grad_w_in")
    if w_in_sender is None:
        dh, brought_late = _matmul(dproj, full['w_in'], 'nt', tm=1024, tn=1024, tk=3200, name="grad_h"), None
    else:
        dh, brought_late = _matmul(dproj, full['w_in'], 'nt', tm=1024, tn=1024, tk=3200, name="grad_h_sending",
                                   sends=w_in_sender(g['w_in']))
    dx, nst = _norm_bwd(s['x'], dh, dxo, _row(p['norm_g'][li]), mod)
    g['norm_g'] = nst[2]
    dmod = jnp.concatenate([nst[0], nst[1], dgate])
    del l
    return dx, g, dmod, (pair, brought), arrived_upper, brought_late


BIG_NAMES = tuple(name for name, _ in BIG)


def _weight_sends(shards, names, li):
    return _ChipSends([shards[name][li] for name in names], _pick_weight_half, pass_on=True)


def _assemble_layer(shards, names, li, arrived, place):
    x, y, core, _ = place
    over_ici, from_sibling = arrived[:len(names)], arrived[len(names):]
    full = {}
    for name, a, b in zip(names, over_ici, from_sibling):
        own = shards[name][li]
        own_halves = own.reshape(2, own.shape[0] // 2, own.shape[1])
        halves = [_join_halves(a[j][None], b[j][None], core, 0) for j in range(3)]
        full[name] = _chips_to_full(name, _by_chip(own_halves, halves, x, y))
    return full


def _pair_sums(tensors, wire, core, theirs=None):
    if theirs is None:
        theirs = _SiblingSends(tensors).run("swap_halves")
    return [_add2(lax.dynamic_index_in_dim(g, core, 0, keepdims=False), a, dt, "sum_core_pair")
            for g, a, dt in zip(tensors, theirs, wire)]


def _layer_grad_halves(layer_grads, names):
    return [_full_to_dests(name, layer_grads[name]) for name in names]


def _step(p, m, v, x, c, target):
    depth = p['norm_g'].shape[0]
    d = D_MODEL
    ix, iy, ic = lax.axis_index("x"), lax.axis_index("y"), lax.axis_index("c")
    chip = 2 * ix + iy
    dev = 4 * ix + 2 * iy + ic
    x0 = x[0]

    c_pad = jnp.pad(c, ((0, SUBLANES - 1), (0, 0)))
    c_all = _gather(c_pad, 'xyc', "gather_c")[:, 0, :]
    n_ada = p['w_ada'].shape[-1]
    b_shard = lax.dynamic_slice_in_dim(p['b_ada'], chip * n_ada, n_ada, axis=1)[:, None, :]
    mod_shard = _ada_fwd(c_all, p['w_ada'].astype(BF16), b_shard)
    mod_all = _gather(mod_shard.reshape(depth * N_DEV, n_ada), 'xy', "gather_mod")
    mod_all = jnp.transpose(mod_all.reshape(N_CHIPS, depth, N_DEV, n_ada), (1, 2, 0, 3)).reshape(depth, N_DEV, 3 * d)
    mods = lax.dynamic_index_in_dim(mod_all, dev, axis=1, keepdims=True)

    place = (ix, iy, ic, chip)
    n_big = len(BIG)

    shards = {name: p[name].astype(BF16) for name, _ in BIG}
    tables, tables_vjp = _all_tables(p)
    others = BIG_NAMES[1:]
    w_in = _assemble_layer(shards, BIG_NAMES[:1], 0, _weight_sends(shards, BIG_NAMES[:1], 0).run("send_weight_halves"),
                           place)['w_in']
    rest = (_weight_sends(shards, others, 0), lambda arrived: _assemble_layer(shards, others, 0, arrived, place))
    saved = []
    xs = x0
    for li in range(depth):
        more = li + 1 < depth
        xs, s, arrived = _layer_fwd(xs, p, w_in, rest, li, mods[li], tables,
                                    _weight_sends(shards, BIG_NAMES[:1], li + 1) if more else None,
                                    _weight_sends(shards, others, li + 1) if more else None)
        saved.append(s)
        if more:
            rest = _assemble_layer(shards, BIG_NAMES, li + 1, arrived, place)
            w_in = rest.pop('w_in')
    dx, hst = _loss_head(xs, _row(p['final_g']), target[0])
    loss = lax.psum(hst[1, 0], ("x", "y", "c"))

    grads, dmods, pair, from_chips = [None] * depth, [None] * depth, [None] * depth, [None] * depth
    w_in_halves, w_in_pair, w_in_arrived = {}, {}, {}

    def w_in_sender(li):
        def to_chips(g_w_in):
            w_in_pair[li] = _pair_sums([_full_to_dests('w_in', g_w_in)], [BF16], ic)
            return _ChipSends(w_in_pair[li], _pick_slab)

        def to_sibling(g_w_in):
            w_in_halves[li] = [_full_to_dests('w_in', g_w_in)]
            return _SiblingSends(w_in_halves[li])

        return to_chips if li == 0 else to_sibling

    upper = None
    for li in reversed(range(depth)):
        dx, grads[li], dmods[li], early, arrived_upper, arrived_late = _layer_bwd(
            dx, saved[li], p, li, mods[li], ic, upper, others, w_in_sender(li))
        if upper is not None:
            w_in_arrived[li + 1] = arrived_upper
        if li > 0:
            w_in_pair[li] = _pair_sums(w_in_halves[li], [BF16], ic, arrived_late)
            upper = w_in_pair[li]
        else:
            w_in_arrived[li] = arrived_late
        pair[li] = w_in_pair[li] + early[0]
        from_chips[li] = early[1]
    for li in range(depth):
        from_chips[li] = list(w_in_arrived[li]) + list(from_chips[li])

    dmod_pad = jnp.pad(jnp.stack(dmods), ((0, SUBLANES - depth), (0, 0)))
    dmod_all = _gather(dmod_pad, 'xyc', "gather_dmod")[:, :depth, :]
    dmod_cols = lax.dynamic_slice_in_dim(jnp.transpose(dmod_all, (1, 0, 2)), chip * n_ada, n_ada, axis=2)
    g_w_ada = _ada_bwd(jnp.transpose(c_all), dmod_cols)

    local_small = {k: jnp.stack([grads[li][k] for li in range(depth)])
                   for k in SMALL if k not in ('final_g', 'b_ada') + SSM_RAW}
    table_cotangents = tuple(jnp.stack([grads[li]['ssm_tables'][k] for li in range(depth)]) for k in range(3))
    local_small.update(zip(SSM_RAW, tables_vjp(table_cotangents)))
    local_small['final_g'] = hst[0]
    local_small['b_ada'] = jnp.stack(dmods)
    small_pack = _pack_rows([local_small[k] for k in SMALL], PACK_COLS, F32)

    small_halves = small_pack.reshape(2, 1, small_pack.shape[0] // 2, small_pack.shape[1])
    small_pair = _pair_sums([small_halves], [F32], ic)
    pair[0] = pair[0] + small_pair
    from_chips[0] = from_chips[0] + list(_ChipSends(small_pair, _pick_slab).run("scatter_chips"))
    halves = []
    for li in range(depth):
        for t, (ps, others) in enumerate(zip(pair[li], from_chips[li])):
            small = t == n_big
            own = ps[0] if small else lax.dynamic_index_in_dim(ps, chip, 0, keepdims=False)
            halves.append(_sum_chips(own, others, small, "sum_chips_small" if small else "sum_chips")[None])
    totals = [_join_halves(h[0], o[0], ic, 0) for h, o in zip(halves, _swap_cores(halves))]
    small_sum = totals[n_big]
    per_layer = [totals[:n_big]] + [totals[n_big + 1 + (li - 1) * n_big:n_big + 1 + li * n_big] for li in range(1, depth)]

    small_shapes = [p[k].shape for k in SMALL]
    grad = dict(zip(SMALL, _unpack_rows(small_sum, small_shapes)))
    grad['w_ada'] = g_w_ada
    for t, (name, _) in enumerate(BIG):
        grad[name] = jnp.stack([per_layer[li][t] for li in range(depth)])

    delta, new_m, new_v = {}, {}, {}
    outs = _adamw(_pack_rows([p[k] for k in SMALL], PACK_COLS, F32), small_sum,
                  _pack_rows([m[k] for k in SMALL], PACK_COLS, F32), _pack_rows([v[k] for k in SMALL], PACK_COLS, F32),
                  name="adamw_small")
    for res, o in zip((delta, new_m, new_v), outs):
        res.update(zip(SMALL, _unpack_rows(o, small_shapes)))
    for name in ['w_ada'] + [n for n, _ in BIG]:
        shape = p[name].shape
        two_d = (-1, shape[-1])
        outs = _adamw(p[name].reshape(two_d), grad[name].reshape(two_d), m[name].reshape(two_d), v[name].reshape(two_d),
                      name="adamw_" + name)
        delta[name], new_m[name], new_v[name] = (o.reshape(shape) for o in outs)

    return (loss, dx[None], *[grad[k] for k in WEIGHTS], *[delta[k] for k in WEIGHTS],
            *[new_m[k] for k in WEIGHTS], *[new_v[k] for k in WEIGHTS])


def kernel(x, c, norm_g, w_ada, b_ada, w_in, attn_sinks, ssm_a_re, ssm_a_im, ssm_log_dt, ssm_b_re, ssm_b_im, ssm_c_re, ssm_c_im, ssm_d, w_glu, b_glu, w_pool, pool_scale, w_br_att, w_br_ssm, w_br_pool, w_out, final_g, loss_target, m_norm_g, m_w_ada, m_b_ada, m_w_in, m_attn_sinks, m_ssm_a_re, m_ssm_a_im, m_ssm_log_dt, m_ssm_b_re, m_ssm_b_im, m_ssm_c_re, m_ssm_c_im, m_ssm_d, m_w_glu, m_b_glu, m_w_pool, m_pool_scale, m_w_br_att, m_w_br_ssm, m_w_br_pool, m_w_out, m_final_g, v_norm_g, v_w_ada, v_b_ada, v_w_in, v_attn_sinks, v_ssm_a_re, v_ssm_a_im, v_ssm_log_dt, v_ssm_b_re, v_ssm_b_im, v_ssm_c_re, v_ssm_c_im, v_ssm_d, v_w_glu, v_b_glu, v_w_pool, v_pool_scale, v_w_br_att, v_w_br_ssm, v_w_br_pool, v_w_out, v_final_g):
    p = dict(zip(WEIGHTS, (norm_g, w_ada, b_ada, w_in, attn_sinks, ssm_a_re, ssm_a_im, ssm_log_dt, ssm_b_re, ssm_b_im,
                           ssm_c_re, ssm_c_im, ssm_d, w_glu, b_glu, w_pool, pool_scale, w_br_att, w_br_ssm, w_br_pool,
                           w_out, final_g)))
    m = dict(zip(WEIGHTS, (m_norm_g, m_w_ada, m_b_ada, m_w_in, m_attn_sinks, m_ssm_a_re, m_ssm_a_im, m_ssm_log_dt,
                           m_ssm_b_re, m_ssm_b_im, m_ssm_c_re, m_ssm_c_im, m_ssm_d, m_w_glu, m_b_glu, m_w_pool,
                           m_pool_scale, m_w_br_att, m_w_br_ssm, m_w_br_pool, m_w_out, m_final_g)))
    v = dict(zip(WEIGHTS, (v_norm_g, v_w_ada, v_b_ada, v_w_in, v_attn_sinks, v_ssm_a_re, v_ssm_a_im, v_ssm_log_dt,
                           v_ssm_b_re, v_ssm_b_im, v_ssm_c_re, v_ssm_c_im, v_ssm_d, v_w_glu, v_b_glu, v_w_pool,
                           v_pool_scale, v_w_br_att, v_w_br_ssm, v_w_br_pool, v_w_out, v_final_g)))
    return _step(p, m, v, x, c, loss_target)
```

```python
import functools
import math

import jax
import jax.numpy as jnp
from jax import lax
from jax.experimental import pallas as pl
from jax.experimental.pallas import tpu as pltpu

F32 = jnp.float32
BF16 = jnp.bfloat16

D_MODEL = 1024
CHUNK = 64
N_HEADS = 8
N_KV_HEADS = 2
HEAD_DIM = 64
Q_PER_KV = N_HEADS // N_KV_HEADS
HALO = 128
ATT_W = 512
KV_W = 128
SSM_W = 512
SSM_GROUP = 16
SSM_GROUPS = 32
SSM_STATE = 64
POOL_W = 512
POOL_GW = 128
IN_W = 6400
EPS = 1e-6
NEG_INF = -1e30
ADAM_LR = 0.001
ADAM_B1 = 0.9
ADAM_B2 = 0.999
ADAM_EPS = 1e-08
ADAM_WD = 0.01
ADAM_STEP = 10

OFF_GL, OFF_Q, OFF_ZA, OFF_US, OFF_ZS, OFF_UP, OFF_ZP, OFF_KV = 0, 3072, 3584, 4096, 4608, 5120, 5632, 6144
_PERM_PIECES = ((3328, 3072), (0, 512), (1792, 512), (768, 512), (2304, 512), (1280, 512), (2816, 512), (512, 256))

LANES = 128
SUBLANES = 8
N_SBLK = SSM_GROUPS * SSM_STATE // LANES
VMEM_LIMIT = 48 * 1024 * 1024

N_CHIPS = 4
N_DEV = 8

WEIGHTS = ['norm_g', 'w_ada', 'b_ada', 'w_in', 'attn_sinks', 'ssm_a_re', 'ssm_a_im', 'ssm_log_dt', 'ssm_b_re',
           'ssm_b_im', 'ssm_c_re', 'ssm_c_im', 'ssm_d', 'w_glu', 'b_glu', 'w_pool', 'pool_scale', 'w_br_att',
           'w_br_ssm', 'w_br_pool', 'w_out', 'final_g']
SMALL = ['norm_g', 'b_ada', 'attn_sinks', 'ssm_a_re', 'ssm_a_im', 'ssm_log_dt', 'ssm_b_re', 'ssm_b_im', 'ssm_c_re',
         'ssm_c_im', 'ssm_d', 'b_glu', 'w_pool', 'pool_scale', 'final_g']
BIG = (('w_in', (1024, 1600)), ('w_br_att', (512, 256)), ('w_br_ssm', (512, 256)), ('w_br_pool', (512, 256)),
       ('w_out', (256, 1024)), ('w_glu', (128, 512)))
PACK_COLS = 1024


def _params(sem=None):
    return pltpu.CompilerParams(dimension_semantics=sem, vmem_limit_bytes=VMEM_LIMIT)


def _row(v):
    return v.reshape(1, -1)


def _full(shape):
    nd = len(shape)
    return pl.BlockSpec(shape, lambda *_: (0,) * nd)


def _sigmoid(v):
    return 1.0 / (1.0 + jnp.exp(-v))


def _silu_and_grad(z):
    s = _sigmoid(z)
    silu = z * s
    return silu, s + silu * (1.0 - s)


_GELU_K = math.sqrt(2.0 / math.pi)


def _gelu_and_grad(v):
    inner = _GELU_K * (v + 0.044715 * v * v * v)
    th = jnp.tanh(inner)
    val = 0.5 * v * (1.0 + th)
    grad = 0.5 * (1.0 + th) + 0.5 * v * (1.0 - th * th) * _GELU_K * (1.0 + 3 * 0.044715 * v * v)
    return val, grad


_NN = (((1,), (0,)), ((), ()))
_NT = (((1,), (1,)), ((), ()))
_TN = (((0,), (0,)), ((), ()))


def _dot(a, b, dims=_NN):
    return lax.dot_general(a.astype(BF16), b.astype(BF16), dims, preferred_element_type=F32)


def _matmul(a, b, mode, *, tm, tn, tk, name, out_dtype=F32, n_outer=False, sends=None):
    if mode == 'nn':
        (m, k), (_, n) = a.shape, b.shape
    elif mode == 'nt':
        (m, k), (n, _) = a.shape, b.shape
    else:
        (k, m), (_, n) = a.shape, b.shape
    tm, tn, tk = min(tm, m), min(tn, n), min(tk, k)
    assert m % tm == 0 and n % tn == 0 and k % tk == 0, (name, a.shape, b.shape)
    nk = k // tk
    dims = {'nn': _NN, 'nt': _NT, 'tn': _TN}[mode]

    def body(a_ref, b_ref, o_ref, acc_ref):
        if nk == 1:
            o_ref[...] = _dot(a_ref[...], b_ref[...], dims).astype(out_dtype)
            return
        kk = pl.program_id(2)

        @pl.when(kk == 0)
        def _():
            acc_ref[...] = jnp.zeros_like(acc_ref)

        acc_ref[...] += _dot(a_ref[...], b_ref[...], dims)

        @pl.when(kk == nk - 1)
        def _():
            o_ref[...] = acc_ref[...].astype(out_dtype)

    def spec(shape, index):
        if n_outer:
            return pl.BlockSpec(shape, lambda j, i, kk: index(i, j, kk))
        return pl.BlockSpec(shape, index)

    a_spec = spec((tk, tm), lambda i, j, kk: (kk, i)) if mode == 'tn' else spec((tm, tk), lambda i, j, kk: (i, kk))
    b_spec = spec((tn, tk), lambda i, j, kk: (j, kk)) if mode == 'nt' else spec((tk, tn), lambda i, j, kk: (kk, j))
    grid = (n // tn, m // tm, nk) if n_outer else (m // tm, n // tn, nk)
    body, x_in, x_out, x_shapes, x_scratch, x_args = _carry_sends(body, 2, 1, 1, grid, sends)
    semantics = ("parallel", "parallel", "arbitrary") if sends is None else ("arbitrary",) * 3
    outs = pl.pallas_call(
        body, name=name, grid=grid, in_specs=[a_spec, b_spec] + x_in,
        out_specs=[spec((tm, tn), lambda i, j, kk: (i, j))] + x_out,
        out_shape=[jax.ShapeDtypeStruct((m, n), out_dtype)] + x_shapes,
        scratch_shapes=[pltpu.VMEM((tm, tn), F32)] + x_scratch, compiler_params=_params(semantics))(a, b, *x_args)
    return outs[0] if sends is None else (outs[0], list(outs[1:]))


ROWS_NORM = 512


def _norm_fwd(x, g, mod):
    l, d = x.shape
    tr = min(ROWS_NORM, l)

    def body(x_ref, g_ref, mod_ref, h_ref):
        xv = x_ref[...]
        r = lax.rsqrt(jnp.mean(xv * xv, axis=-1, keepdims=True) + EPS)
        shift, scale = mod_ref[:, 0:d], mod_ref[:, d:2 * d]
        h_ref[...] = ((xv * r * g_ref[...]) * (1.0 + scale) + shift).astype(BF16)

    return pl.pallas_call(
        body, name="norm_fwd", grid=(l // tr,),
        in_specs=[pl.BlockSpec((tr, d), lambda i: (i, 0)), _full((1, d)), _full((1, 3 * d))],
        out_specs=pl.BlockSpec((tr, d), lambda i: (i, 0)), out_shape=jax.ShapeDtypeStruct((l, d), BF16),
        compiler_params=_params(("parallel",)))(x, g, mod)


def _norm_bwd(x, dh, dxo, g, mod):
    l, d = x.shape
    tr = min(ROWS_NORM, l)

    def body(x_ref, dh_ref, dxo_ref, g_ref, mod_ref, dx_ref, st_ref):
        @pl.when(pl.program_id(0) == 0)
        def _():
            st_ref[...] = jnp.zeros_like(st_ref)

        xv, dhv = x_ref[...], dh_ref[...]
        r = lax.rsqrt(jnp.mean(xv * xv, axis=-1, keepdims=True) + EPS)
        xn = xv * r
        gv = g_ref[...]
        sc1 = 1.0 + mod_ref[:, d:2 * d]
        dxn = dhv * gv * sc1
        dx_ref[...] = dxo_ref[...] + r * (dxn - xn * jnp.mean(dxn * xn, axis=-1, keepdims=True))
        st_ref[0:1, :] += jnp.sum(dhv, axis=0, keepdims=True)
        st_ref[1:2, :] += jnp.sum(dhv * xn * gv, axis=0, keepdims=True)
        st_ref[2:3, :] += jnp.sum(dhv * xn * sc1, axis=0, keepdims=True)

    blk = pl.BlockSpec((tr, d), lambda i: (i, 0))
    return pl.pallas_call(
        body, name="norm_bwd", grid=(l // tr,),
        in_specs=[blk, blk, blk, _full((1, d)), _full((1, 3 * d))],
        out_specs=[blk, _full((SUBLANES, d))],
        out_shape=[jax.ShapeDtypeStruct((l, d), F32), jax.ShapeDtypeStruct((SUBLANES, d), F32)],
        compiler_params=_params(("arbitrary",)))(x, dh, dxo, g, mod)


def _loss_head(x, g, target):
    l, d = x.shape
    tr = min(ROWS_NORM, l)

    def body(x_ref, g_ref, t_ref, dx_ref, st_ref):
        @pl.when(pl.program_id(0) == 0)
        def _():
            st_ref[...] = jnp.zeros_like(st_ref)

        xv = x_ref[...]
        r = lax.rsqrt(jnp.mean(xv * xv, axis=-1, keepdims=True) + EPS)
        xn = xv * r
        gv = g_ref[...]
        err = xn * gv - t_ref[...]
        part = 0.5 * jnp.sum(jnp.mean(err * err, axis=-1, keepdims=True), axis=0, keepdims=True)
        dy = err * (1.0 / d)
        dxn = dy * gv
        dx_ref[...] = r * (dxn - xn * jnp.mean(dxn * xn, axis=-1, keepdims=True))
        st_ref[0:1, :] += jnp.sum(dy * xn, axis=0, keepdims=True)
        st_ref[1:2, :] += jnp.broadcast_to(part, (1, d))

    blk = pl.BlockSpec((tr, d), lambda i: (i, 0))
    return pl.pallas_call(
        body, name="loss_head", grid=(l // tr,), in_specs=[blk, _full((1, d)), blk],
        out_specs=[blk, _full((SUBLANES, d))],
        out_shape=[jax.ShapeDtypeStruct((l, d), F32), jax.ShapeDtypeStruct((SUBLANES, d), F32)],
        compiler_params=_params(("arbitrary",)))(x, g, target)


ROWS_ATT = 128
ROWS_ATT_BWD = 256
_SLOPES = tuple(2.0 ** (-8.0 * (h + 1) / N_HEADS) for h in range(N_HEADS))


def _att_mask(i, t):
    r = lax.broadcasted_iota(jnp.int32, (t, t + HALO), 0)
    j = lax.broadcasted_iota(jnp.int32, (t, t + HALO), 1)
    dist = jnp.abs(r + HALO - j).astype(F32)
    rc, jc = r // CHUNK, j // CHUNK
    allowed = (jc >= rc) & (jc <= rc + 2) & ((j >= HALO) | (i > 0))
    return dist, allowed


def _att_probs(qh, k, dist, allowed, slope, sink):
    s = _dot(qh, k, _NT) * (1.0 / math.sqrt(HEAD_DIM)) - slope * dist
    s = jnp.where(allowed, s, NEG_INF)
    m = jnp.maximum(jnp.max(s, axis=1, keepdims=True), sink)
    e = jnp.exp(s - m)
    es = jnp.exp(sink - m)
    den = jnp.sum(e, axis=1, keepdims=True) + es
    return e / den, es / den


def _att_specs(t):
    q_spec = pl.BlockSpec((t, ATT_W), lambda i: (i, OFF_Q // ATT_W))
    kv_spec = pl.BlockSpec((t, 2 * KV_W), lambda i: (i, OFF_KV // (2 * KV_W)))
    halo_spec = pl.BlockSpec((HALO, 2 * KV_W), lambda i: (jnp.maximum(i * (t // HALO) - 1, 0), OFF_KV // (2 * KV_W)))
    return q_spec, kv_spec, halo_spec


def _attention_fwd(proj, sinks):
    l = proj.shape[0]
    t = min(ROWS_ATT, l)

    def body(sink_ref, q_ref, kv_ref, halo_ref, o_ref):
        dist, allowed = _att_mask(pl.program_id(0), t)
        kv = jnp.concatenate([halo_ref[...], kv_ref[...]], axis=0)
        for h in range(N_HEADS):
            kh = h // Q_PER_KV
            k = kv[:, kh * HEAD_DIM:(kh + 1) * HEAD_DIM]
            v = kv[:, KV_W + kh * HEAD_DIM:KV_W + (kh + 1) * HEAD_DIM]
            p, _ = _att_probs(q_ref[:, h * HEAD_DIM:(h + 1) * HEAD_DIM], k, dist, allowed, _SLOPES[h], sink_ref[h])
            o_ref[:, h * HEAD_DIM:(h + 1) * HEAD_DIM] = _dot(p, v)

    q_spec, kv_spec, halo_spec = _att_specs(t)
    return pl.pallas_call(
        body, name="attention_fwd", grid=(l // t,),
        in_specs=[pl.BlockSpec(memory_space=pltpu.SMEM), q_spec, kv_spec, halo_spec],
        out_specs=pl.BlockSpec((t, ATT_W), lambda i: (i, 0)), out_shape=jax.ShapeDtypeStruct((l, ATT_W), F32),
        compiler_params=_params(("parallel",)))(sinks, proj, proj, proj)


def _attention_bwd(proj, sinks, do, sends=None):
    l = proj.shape[0]
    t = min(ROWS_ATT_BWD, l)

    def body(sink_ref, q_ref, kv_ref, halo_ref, do_ref, dq_ref, dkv_ref, dsink_ref):
        i = pl.program_id(0)

        @pl.when(i == 0)
        def _():
            dkv_ref[...] = jnp.zeros_like(dkv_ref)
            dsink_ref[...] = jnp.zeros_like(dsink_ref)

        dist, allowed = _att_mask(i, t)
        kv = jnp.concatenate([halo_ref[...], kv_ref[...]], axis=0)
        rows = pl.ds(pl.multiple_of(i * t, t), t + HALO)
        for kh in range(N_KV_HEADS):
            k = kv[:, kh * HEAD_DIM:(kh + 1) * HEAD_DIM]
            v = kv[:, KV_W + kh * HEAD_DIM:KV_W + (kh + 1) * HEAD_DIM]
            dk = jnp.zeros((t + HALO, HEAD_DIM), F32)
            dv = jnp.zeros((t + HALO, HEAD_DIM), F32)
            for h in range(kh * Q_PER_KV, (kh + 1) * Q_PER_KV):
                qh = q_ref[:, h * HEAD_DIM:(h + 1) * HEAD_DIM]
                doh = do_ref[:, h * HEAD_DIM:(h + 1) * HEAD_DIM]
                p, ps = _att_probs(qh, k, dist, allowed, _SLOPES[h], sink_ref[h])
                dp = _dot(doh, v, _NT)
                delta = jnp.sum(p * dp, axis=1, keepdims=True)
                ds = p * (dp - delta) * (1.0 / math.sqrt(HEAD_DIM))
                dsink_ref[h:h + 1, :] += jnp.broadcast_to(-jnp.sum(ps * delta, axis=0, keepdims=True), (1, LANES))
                dq_ref[:, h * HEAD_DIM:(h + 1) * HEAD_DIM] = _dot(ds, k).astype(BF16)
                dk = dk + _dot(ds, qh, _TN)
                dv = dv + _dot(p, doh, _TN)
            dkv_ref[rows, kh * HEAD_DIM:(kh + 1) * HEAD_DIM] += dk
            dkv_ref[rows, KV_W + kh * HEAD_DIM:KV_W + (kh + 1) * HEAD_DIM] += dv

    q_spec, kv_spec, halo_spec = _att_specs(t)
    blk = pl.BlockSpec((t, ATT_W), lambda i: (i, 0))
    body, x_in, x_out, x_shapes, x_scratch, x_args = _carry_sends(body, 5, 3, 0, (l // t,), sends)
    outs = pl.pallas_call(
        body, name="attention_bwd" if sends is None else "attention_bwd_sending", grid=(l // t,),
        in_specs=[pl.BlockSpec(memory_space=pltpu.SMEM), q_spec, kv_spec, halo_spec, blk] + x_in,
        out_specs=[blk, _full((HALO + l, 2 * KV_W)), _full((N_HEADS, LANES))] + x_out,
        out_shape=[jax.ShapeDtypeStruct((l, ATT_W), BF16), jax.ShapeDtypeStruct((HALO + l, 2 * KV_W), F32),
                   jax.ShapeDtypeStruct((N_HEADS, LANES), F32)] + x_shapes,
        scratch_shapes=x_scratch, compiler_params=_params(("arbitrary",)))(sinks, proj, proj, proj, do, *x_args)
    return outs[0], outs[1], outs[2], list(outs[3:])


ROWS_SSM = 2048
SSM_BLOCKS = 2
SCAN_UNROLL = 2
SSM_STEPS_PER_U = LANES // (SSM_BLOCKS * 2 * SSM_GROUP)


def _ssm_discretize(a_re, a_im, log_dt, b_re, b_im):
    lam = lax.complex(a_re, a_im)
    dt = jnp.exp(log_dt)[:, None]
    lam_bar = jnp.exp(lam * dt)
    b_bar = ((lam_bar - 1.0) / lam)[..., None] * lax.complex(b_re, b_im)
    return lam, dt, lam_bar, b_bar


def _ssm_block_diag(m):
    e = m.reshape(N_SBLK, 2, SSM_GROUP, SSM_STATE)
    e = e[:, :, :, None, :] * jnp.eye(2, dtype=m.dtype)[None, :, None, :, None]
    e = e.reshape(N_SBLK, 2 * SSM_GROUP, LANES)
    oh = jax.nn.one_hot(jnp.arange(N_SBLK) % 4, 4, dtype=m.dtype)
    return (oh[:, :, None, None] * e[:, None]).reshape(N_SBLK, LANES, LANES)


def _ssm_tables(a_re, a_im, log_dt, b_re, b_im, c_re, c_im):
    _, _, lam_bar, b_bar = _ssm_discretize(a_re, a_im, log_dt, b_re, b_im)
    lam_blk = jnp.stack([jnp.real(lam_bar).reshape(N_SBLK, LANES), jnp.imag(lam_bar).reshape(N_SBLK, LANES)], axis=1)
    bt = jnp.transpose(b_bar, (0, 2, 1))
    bb = jnp.concatenate([_ssm_block_diag(jnp.real(bt)), _ssm_block_diag(jnp.imag(bt))], axis=2)
    cb = jnp.concatenate([jnp.transpose(_ssm_block_diag(c_re), (0, 2, 1)),
                          jnp.transpose(_ssm_block_diag(-c_im), (0, 2, 1))], axis=1)
    return lam_blk, bb, cb


def _ssm_powers(a_re, a_im, log_dt):
    lam = lax.complex(a_re, a_im)
    dt = jnp.exp(log_dt)[:, None]
    k = jnp.arange(1, SUBLANES + 1, dtype=F32)
    pw = jnp.exp((lam * dt)[None] * k[:, None, None]).reshape(SUBLANES, N_SBLK, LANES)
    pw = jnp.transpose(pw, (1, 0, 2))
    rev = pw[:, ::-1]
    return jnp.concatenate([jnp.real(pw), jnp.imag(pw), jnp.real(rev), jnp.imag(rev)], axis=1)


def _scan_consts(pw_ref, b, reverse):
    row = lax.broadcasted_iota(jnp.int32, (SUBLANES, LANES), 0)
    sign = -1.0 if reverse else 1.0

    def power(k):
        return (jnp.broadcast_to(pw_ref[b, k - 1:k, :], (SUBLANES, LANES)),
                sign * jnp.broadcast_to(pw_ref[b, SUBLANES + k - 1:SUBLANES + k, :], (SUBLANES, LANES)))

    steps = []
    for d in (1, 2, 4):
        pr, pi = power(d)
        keep = (row < SUBLANES - d) if reverse else (row >= d)
        steps.append((d, jnp.where(keep, pr, 0.0), jnp.where(keep, pi, 0.0)))
    base = 2 * SUBLANES if reverse else 0
    return steps, pw_ref[b, base:base + SUBLANES, :], sign * pw_ref[b, base + SUBLANES:base + 2 * SUBLANES, :]


def _scan_tile(xr, xi, steps, reverse):
    for d, ar, ai in steps:
        shift = SUBLANES - d if reverse else d
        rr, ri = pltpu.roll(xr, shift, 0), pltpu.roll(xi, shift, 0)
        xr, xi = xr + (ar * rr - ai * ri), xi + (ar * ri + ai * rr)
    return xr, xi


def _bcast_row(v, r):
    return jnp.broadcast_to(v[r:r + 1, :], (SUBLANES, LANES))


def _re_im(b):
    return slice(2 * b * LANES, (2 * b + 1) * LANES), slice((2 * b + 1) * LANES, (2 * b + 2) * LANES)


def _scan_forward(s_scr, pw_ref, n_tiles):
    consts = [_scan_consts(pw_ref, b, False) for b in range(SSM_BLOCKS)]

    def tiles(g, carry):
        carry = list(carry)
        rows = [pl.ds(pl.multiple_of((g * SCAN_UNROLL + u) * SUBLANES, SUBLANES), SUBLANES) for u in range(SCAN_UNROLL)]
        loaded = [[(s_scr[r, _re_im(b)[0]], s_scr[r, _re_im(b)[1]]) for b in range(SSM_BLOCKS)] for r in rows]
        local = [[_scan_tile(xr, xi, consts[b][0], False) for b, (xr, xi) in enumerate(per_row)] for per_row in loaded]
        for r, per_row in zip(rows, local):
            for b, (xr, xi) in enumerate(per_row):
                _, pr, pi = consts[b]
                cr, ci = carry[b]
                sr, si = xr + (pr * cr - pi * ci), xi + (pr * ci + pi * cr)
                s_scr[r, _re_im(b)[0]] = sr
                s_scr[r, _re_im(b)[1]] = si
                carry[b] = (_bcast_row(sr, SUBLANES - 1), _bcast_row(si, SUBLANES - 1))
        return tuple(carry)

    zero = jnp.zeros((SUBLANES, LANES), F32)
    lax.fori_loop(0, n_tiles // SCAN_UNROLL, tiles, ((zero, zero),) * SSM_BLOCKS)


def _row_chunks(l):
    rc = min(ROWS_SSM, l)
    return rc, l // rc


def _carry_sends(body, n_in, n_out, n_scratch, grid, sends):
    if sends is None:
        return body, [], [], [], [], []
    k, k_out = len(sends.srcs), len(sends.out_shapes)

    def carrying(*refs):
        ins, send_ins = refs[:n_in], refs[n_in:n_in + k]
        outs, send_outs = refs[n_in + k:n_in + k + n_out], refs[n_in + k + n_out:n_in + k + n_out + k_out]
        rest = refs[n_in + k + n_out + k_out:]
        scratch, sems = rest[:n_scratch], rest[n_scratch:]
        ids = [pl.program_id(axis) for axis in range(len(grid))]
        first = functools.reduce(jnp.logical_and, [i == 0 for i in ids])
        last = functools.reduce(jnp.logical_and, [i == n - 1 for i, n in zip(ids, grid)])

        @pl.when(first)
        def _():
            sends.start(send_ins, send_outs, *sems)

        body(*ins, *outs, *scratch)

        @pl.when(last)
        def _():
            sends.finish(send_ins, send_outs, *sems)

    in_specs, out_specs, scratch = sends.specs()
    return carrying, in_specs, out_specs, sends.out_shapes, scratch, sends.srcs


def _ssm_fwd(proj, bb, cb, pw, sends=None):
    l = proj.shape[0]
    rc, n_chunks = _row_chunks(l)

    def body(u_ref, bb_ref, cb_ref, pw_ref, y_ref, s_scr):
        j = pl.program_id(0)

        def fill(ci, _):
            rows = pl.ds(pl.multiple_of(ci * rc, rc), rc)
            uv = u_ref[rows, :]
            for b in range(SSM_BLOCKS):
                s_scr[rows, 2 * b * LANES:2 * (b + 1) * LANES] = _dot(uv, bb_ref[b])
            return 0

        lax.fori_loop(0, n_chunks, fill, 0)
        _scan_forward(s_scr, pw_ref, l // SUBLANES)

        @pl.when(j % SSM_STEPS_PER_U == 0)
        def _():
            y_ref[...] = jnp.zeros_like(y_ref)

        def emit(ci, _):
            rows = pl.ds(pl.multiple_of(ci * rc, rc), rc)
            for b in range(SSM_BLOCKS):
                y_ref[rows, :] += _dot(s_scr[rows, 2 * b * LANES:2 * (b + 1) * LANES], cb_ref[b])
            return 0

        lax.fori_loop(0, n_chunks, emit, 0)

    steps = N_SBLK // SSM_BLOCKS
    body, x_in, x_out, x_shapes, x_scratch, x_args = _carry_sends(body, 4, 2, 0, (steps,), sends)
    outs = pl.pallas_call(
        body, name="ssm_fwd" if sends is None else "ssm_fwd_sending", grid=(steps,),
        in_specs=[pl.BlockSpec((l, LANES), lambda j: (0, OFF_US // LANES + j // SSM_STEPS_PER_U)),
                  pl.BlockSpec((SSM_BLOCKS, LANES, 2 * LANES), lambda j: (j, 0, 0)),
                  pl.BlockSpec((SSM_BLOCKS, 2 * LANES, LANES), lambda j: (j, 0, 0)),
                  pl.BlockSpec((SSM_BLOCKS, 4 * SUBLANES, LANES), lambda j: (j, 0, 0))] + x_in,
        out_specs=[pl.BlockSpec((l, LANES), lambda j: (0, j // SSM_STEPS_PER_U)),
                   pl.BlockSpec((l, SSM_BLOCKS * 2 * LANES), lambda j: (0, j))] + x_out,
        out_shape=[jax.ShapeDtypeStruct((l, SSM_W), F32), jax.ShapeDtypeStruct((l, N_SBLK * 2 * LANES), F32)] + x_shapes,
        scratch_shapes=x_scratch, compiler_params=_params(("arbitrary",)))(proj, bb, cb, pw, *x_args)
    return outs[0], outs[1], list(outs[2:])


def _ssm_bwd(proj, states, dy, bb, cb, pw, sends=None):
    l = proj.shape[0]
    rc, n_chunks = _row_chunks(l)
    n_tiles = l // SUBLANES

    def body(u_ref, s_scr, dy_ref, bb_ref, cb_ref, pw_ref, du_ref, dbb_ref, dcb_ref, dlam_ref, a_scr):
        j = pl.program_id(0)
        dcb_ref[...] = jnp.zeros_like(dcb_ref)

        def through_c(ci, _):
            rows = pl.ds(pl.multiple_of(ci * rc, rc), rc)
            dyv = dy_ref[rows, :]
            for b in range(SSM_BLOCKS):
                lanes = slice(2 * b * LANES, 2 * (b + 1) * LANES)
                dcb_ref[b] += _dot(s_scr[rows, lanes], dyv, _TN)
                a_scr[rows, lanes] = _dot(dyv, cb_ref[b], _NT)
            return 0

        lax.fori_loop(0, n_chunks, through_c, 0)

        consts = [_scan_consts(pw_ref, b, True) for b in range(SSM_BLOCKS)]
        row = lax.broadcasted_iota(jnp.int32, (SUBLANES, LANES), 0)
        last = row == SUBLANES - 1

        def tiles(g, carry):
            carry = list(carry)
            rows = [pl.ds(pl.multiple_of((n_tiles - 1 - (g * SCAN_UNROLL + u)) * SUBLANES, SUBLANES), SUBLANES)
                    for u in range(SCAN_UNROLL)]
            loaded = [[(a_scr[r, _re_im(b)[0]], a_scr[r, _re_im(b)[1]]) for b in range(SSM_BLOCKS)] for r in rows]
            states = [[(s_scr[r, _re_im(b)[0]], s_scr[r, _re_im(b)[1]]) for b in range(SSM_BLOCKS)] for r in rows]
            local = [[_scan_tile(xr, xi, consts[b][0], True) for b, (xr, xi) in enumerate(per_row)] for per_row in loaded]
            for r, per_row, state_row in zip(rows, local, states):
                for b, ((xr, xi), (sr, si)) in enumerate(zip(per_row, state_row)):
                    _, pr, pi = consts[b]
                    cr, ci, accr, acci = carry[b]
                    gr, gi = xr + (pr * cr - pi * ci), xi + (pr * ci + pi * cr)
                    a_scr[r, _re_im(b)[0]] = gr
                    a_scr[r, _re_im(b)[1]] = gi
                    ur = jnp.where(last, cr, pltpu.roll(gr, SUBLANES - 1, 0))
                    ui = jnp.where(last, ci, pltpu.roll(gi, SUBLANES - 1, 0))
                    carry[b] = (_bcast_row(gr, 0), _bcast_row(gi, 0),
                                accr + (ur * sr + ui * si), acci + (ui * sr - ur * si))
            return tuple(carry)

        zero = jnp.zeros((SUBLANES, LANES), F32)
        done = lax.fori_loop(0, n_tiles // SCAN_UNROLL, tiles, ((zero, zero, zero, zero),) * SSM_BLOCKS)
        for b in range(SSM_BLOCKS):
            dlr = jnp.broadcast_to(jnp.sum(done[b][2], axis=0, keepdims=True), (SUBLANES, LANES))
            dli = jnp.broadcast_to(jnp.sum(done[b][3], axis=0, keepdims=True), (SUBLANES, LANES))
            dlam_ref[b] = jnp.where(row == 0, dlr, jnp.where(row == 1, dli, 0.0))

        dbb_ref[...] = jnp.zeros_like(dbb_ref)

        @pl.when(j % SSM_STEPS_PER_U == 0)
        def _():
            du_ref[...] = jnp.zeros_like(du_ref)

        def through_b(ci, _):
            rows = pl.ds(pl.multiple_of(ci * rc, rc), rc)
            uv = u_ref[rows, :]
            for b in range(SSM_BLOCKS):
                av = a_scr[rows, 2 * b * LANES:2 * (b + 1) * LANES]
                dbb_ref[b] += _dot(uv, av, _TN)
                du_ref[rows, :] += _dot(av, bb_ref[b], _NT)
            return 0

        lax.fori_loop(0, n_chunks, through_b, 0)

    steps = N_SBLK // SSM_BLOCKS
    body, x_in, x_out, x_shapes, x_scratch, x_args = _carry_sends(body, 6, 4, 1, (steps,), sends)
    outs = pl.pallas_call(
        body, name="ssm_bwd" if sends is None else "ssm_bwd_sending", grid=(steps,),
        in_specs=[pl.BlockSpec((l, LANES), lambda j: (0, OFF_US // LANES + j // SSM_STEPS_PER_U)),
                  pl.BlockSpec((l, SSM_BLOCKS * 2 * LANES), lambda j: (0, j)),
                  pl.BlockSpec((l, LANES), lambda j: (0, j // SSM_STEPS_PER_U)),
                  pl.BlockSpec((SSM_BLOCKS, LANES, 2 * LANES), lambda j: (j, 0, 0)),
                  pl.BlockSpec((SSM_BLOCKS, 2 * LANES, LANES), lambda j: (j, 0, 0)),
                  pl.BlockSpec((SSM_BLOCKS, 4 * SUBLANES, LANES), lambda j: (j, 0, 0))] + x_in,
        out_specs=[pl.BlockSpec((l, LANES), lambda j: (0, j // SSM_STEPS_PER_U)),
                   pl.BlockSpec((SSM_BLOCKS, LANES, 2 * LANES), lambda j: (j, 0, 0)),
                   pl.BlockSpec((SSM_BLOCKS, 2 * LANES, LANES), lambda j: (j, 0, 0)),
                   pl.BlockSpec((SSM_BLOCKS, SUBLANES, LANES), lambda j: (j, 0, 0))] + x_out,
        out_shape=[jax.ShapeDtypeStruct((l, SSM_W), F32), jax.ShapeDtypeStruct((N_SBLK, LANES, 2 * LANES), F32),
                   jax.ShapeDtypeStruct((N_SBLK, 2 * LANES, LANES), F32),
                   jax.ShapeDtypeStruct((N_SBLK, SUBLANES, LANES), F32)] + x_shapes,
        scratch_shapes=[pltpu.VMEM((l, SSM_BLOCKS * 2 * LANES), F32)] + x_scratch,
        compiler_params=_params(("arbitrary",)))(proj, states, dy, bb, cb, pw, *x_args)
    return outs[0], outs[1], outs[2], outs[3], list(outs[4:])


def _pool_windows(v, t, l, ahead):
    def shifted(a, d):
        if ahead:
            return jnp.where(t < l - d, pltpu.roll(a, l - d, 0), 0.0)
        return jnp.where(t >= d, pltpu.roll(a, d, 0), 0.0)

    w2 = v + shifted(v, 1)
    w4 = w2 + shifted(w2, 2)
    w8 = w4 + shifted(w4, 4)
    w16 = w8 + shifted(w8, 8)
    return w2, w4, w8, w16


def _pool_select(g, ws):
    return jnp.where(g == 0, ws[0], jnp.where(g == 1, ws[1], jnp.where(g == 2, ws[2], ws[3])))


def _pool_count(g, t):
    return jnp.minimum(t + 1, jnp.left_shift(2, g)).astype(F32)


def _pool_specs(l):
    return [pl.BlockSpec((l, POOL_GW), lambda g: (0, OFF_UP // POOL_GW + g)),
            pl.BlockSpec((1, POOL_GW, POOL_GW), lambda g: (g, 0, 0)),
            pl.BlockSpec((1, POOL_GW), lambda g: (0, g))]


def _pool_fwd(proj, w_pool, scale):
    l = proj.shape[0]

    def body(u_ref, w_ref, sc_ref, y_ref):
        g = pl.program_id(0)
        t = lax.broadcasted_iota(jnp.int32, (l, 1), 0)
        u = u_ref[...]
        pooled = _pool_select(g, _pool_windows(u, t, l, False)) / _pool_count(g, t) - u
        y_ref[...] = _dot(pooled, w_ref[0]) * sc_ref[...]

    return pl.pallas_call(
        body, name="pool_fwd", grid=(4,), in_specs=_pool_specs(l),
        out_specs=pl.BlockSpec((l, POOL_GW), lambda g: (0, g)), out_shape=jax.ShapeDtypeStruct((l, POOL_W), F32),
        compiler_params=_params(("parallel",)))(proj, w_pool, scale)


def _pool_bwd(proj, w_pool, scale, dy):
    l = proj.shape[0]

    def body(u_ref, w_ref, sc_ref, dy_ref, du_ref, dw_ref, dsc_ref):
        g = pl.program_id(0)
        t = lax.broadcasted_iota(jnp.int32, (l, 1), 0)
        u = u_ref[...]
        cnt = _pool_count(g, t)
        pooled = _pool_select(g, _pool_windows(u, t, l, False)) / cnt - u
        dyv = dy_ref[...]
        dsc_ref[...] = jnp.sum(dyv * _dot(pooled, w_ref[0]), axis=0, keepdims=True)
        dyl = dyv * sc_ref[...]
        dw_ref[0] = _dot(pooled, dyl, _TN)
        dpooled = _dot(dyl, w_ref[0], _NT)
        du_ref[...] = (_pool_select(g, _pool_windows(dpooled / cnt, t, l, True)) - dpooled).astype(BF16)

    return pl.pallas_call(
        body, name="pool_bwd", grid=(4,), in_specs=_pool_specs(l) + [pl.BlockSpec((l, POOL_GW), lambda g: (0, g))],
        out_specs=[pl.BlockSpec((l, POOL_GW), lambda g: (0, g)), pl.BlockSpec((1, POOL_GW, POOL_GW), lambda g: (g, 0, 0)),
                   pl.BlockSpec((1, POOL_GW), lambda g: (0, g))],
        out_shape=[jax.ShapeDtypeStruct((l, POOL_W), BF16), jax.ShapeDtypeStruct((4, POOL_GW, POOL_GW), F32),
                   jax.ShapeDtypeStruct((1, POOL_W), F32)],
        compiler_params=_params(("parallel",)))(proj, w_pool, scale, dy)


ROWS_MERGE = 128
ROWS_MERGE_FWD = 256


def _merge_inputs(tr):
    def col(off, w):
        return pl.BlockSpec((tr, w), lambda i: (i, off // w))

    def act(w):
        return pl.BlockSpec((tr, w), lambda i: (i, 0))

    d = D_MODEL
    return ([col(OFF_GL, 3 * d), col(OFF_ZA, ATT_W), col(OFF_US, SSM_W), col(OFF_ZS, SSM_W), col(OFF_ZP, POOL_W),
             act(ATT_W), act(SSM_W), act(POOL_W)]
            + [_full((1, SSM_W)), _full((SSM_W, SSM_W)), _full((1, SSM_W)), _full((ATT_W, d)), _full((SSM_W, d)),
               _full((POOL_W, d)), _full((d, d)), _full((1, 3 * d))])


def _merge_forward_math(gl_ref, za_ref, us_ref, zs_ref, zp_ref, o_ref, yl_ref, yp_ref, d_ref, wg_ref, bg_ref,
                        wba_ref, wbs_ref, wbp_ref):
    d = D_MODEL
    r = {}
    r['sa'], r['dsa'] = _silu_and_grad(za_ref[...])
    r['ss'], r['dss'] = _silu_and_grad(zs_ref[...])
    r['sp'], r['dsp'] = _silu_and_grad(zp_ref[...])
    r['y_att'] = o_ref[...] * r['sa']
    y1 = yl_ref[...] + d_ref[...] * us_ref[...]
    r['y2'], r['dgelu'] = _gelu_and_grad(y1)
    r['sg'] = _sigmoid(_dot(r['y2'], wg_ref[...]) + bg_ref[...])
    r['y3'] = r['y2'] * r['sg']
    r['y_ssm'] = r['y3'] * r['ss']
    r['y_pool'] = yp_ref[...] * r['sp']
    r['g'] = [_sigmoid(gl_ref[:, b * d:(b + 1) * d]) for b in range(3)]
    r['p'] = [_dot(r['y_att'], wba_ref[...]), _dot(r['y_ssm'], wbs_ref[...]), _dot(r['y_pool'], wbp_ref[...])]
    r['merged'] = r['g'][0] * r['p'][0] + r['g'][1] * r['p'][1] + r['g'][2] * r['p'][2]
    return r


def _merge_fwd(proj, o, y_lin, y_pool, x, ssm_d, w_glu, b_glu, wba, wbs, wbp, w_out, mod, sends=None):
    l, d = x.shape
    tr = min(ROWS_MERGE_FWD, l)

    def body(gl_ref, za_ref, us_ref, zs_ref, zp_ref, o_ref, yl_ref, yp_ref, d_ref, wg_ref, bg_ref, wba_ref, wbs_ref,
             wbp_ref, wo_ref, mod_ref, x_ref, xn_ref):
        r = _merge_forward_math(gl_ref, za_ref, us_ref, zs_ref, zp_ref, o_ref, yl_ref, yp_ref, d_ref, wg_ref, bg_ref,
                                wba_ref, wbs_ref, wbp_ref)
        xn_ref[...] = x_ref[...] + mod_ref[:, 2 * d:3 * d] * _dot(r['merged'], wo_ref[...])

    blk = pl.BlockSpec((tr, d), lambda i: (i, 0))
    in_specs = _merge_inputs(tr) + [blk]
    body, x_in, x_out, x_shapes, x_scratch, x_args = _carry_sends(body, len(in_specs), 1, 0, (l // tr,), sends)
    outs = pl.pallas_call(
        body, name="merge_fwd" if sends is None else "merge_fwd_sending", grid=(l // tr,), in_specs=in_specs + x_in,
        out_specs=[blk] + x_out, out_shape=[jax.ShapeDtypeStruct((l, d), F32)] + x_shapes, scratch_shapes=x_scratch,
        compiler_params=_params(("arbitrary",)))(
            proj, proj, proj, proj, proj, o, y_lin, y_pool, ssm_d, w_glu, b_glu, wba, wbs, wbp, w_out, mod, x, *x_args)
    return outs[0], list(outs[1:])


def _merge_bwd(proj, o, y_lin, y_pool, dxo, ssm_d, w_glu, b_glu, wba, wbs, wbp, w_out, mod, sends=None):
    l, d = dxo.shape
    tr = min(ROWS_MERGE, l)

    def body(gl_ref, za_ref, us_ref, zs_ref, zp_ref, o_ref, yl_ref, yp_ref, d_ref, wg_ref, bg_ref, wba_ref, wbs_ref,
             wbp_ref, wo_ref, mod_ref, dxo_ref,
             do_ref, dyl_ref, dus_ref, dyp_ref, dpj_ref,
             ya_ref, ys_ref, ypl_ref, y2_ref, dt_ref, dpa_ref, dps_ref, dpp_ref, mg_ref, dout_ref, st_ref):
        @pl.when(pl.program_id(0) == 0)
        def _():
            st_ref[...] = jnp.zeros_like(st_ref)

        for off, width in ((OFF_Q, ATT_W), (OFF_US, SSM_W), (OFF_UP, POOL_W), (OFF_KV, 2 * KV_W)):
            dpj_ref[:, off:off + width] = jnp.zeros((tr, width), BF16)

        r = _merge_forward_math(gl_ref, za_ref, us_ref, zs_ref, zp_ref, o_ref, yl_ref, yp_ref, d_ref, wg_ref, bg_ref,
                                wba_ref, wbs_ref, wbp_ref)
        dxov = dxo_ref[...]
        out = _dot(r['merged'], wo_ref[...])
        st_ref[0:1, :] += jnp.sum(dxov * out, axis=0, keepdims=True)
        dout = dxov * mod_ref[:, 2 * d:3 * d]
        dmerged = _dot(dout, wo_ref[...], _NT)
        dys = []
        for b, (w_ref, dp_ref) in enumerate(((wba_ref, dpa_ref), (wbs_ref, dps_ref), (wbp_ref, dpp_ref))):
            gb = r['g'][b]
            dp = dmerged * gb
            dpj_ref[:, OFF_GL + b * d:OFF_GL + (b + 1) * d] = (dp * r['p'][b] * (1.0 - gb)).astype(BF16)
            dp_ref[...] = dp.astype(BF16)
            dys.append(_dot(dp, w_ref[...], _NT))
        mg_ref[...] = r['merged'].astype(BF16)
        dout_ref[...] = dout.astype(BF16)
        ya_ref[...] = r['y_att'].astype(BF16)
        ys_ref[...] = r['y_ssm'].astype(BF16)
        ypl_ref[...] = r['y_pool'].astype(BF16)
        y2_ref[...] = r['y2'].astype(BF16)
        do_ref[...] = (dys[0] * r['sa']).astype(BF16)
        dpj_ref[:, OFF_ZA:OFF_ZA + ATT_W] = (dys[0] * o_ref[...] * r['dsa']).astype(BF16)
        dy3 = dys[1] * r['ss']
        dpj_ref[:, OFF_ZS:OFF_ZS + SSM_W] = (dys[1] * r['y3'] * r['dss']).astype(BF16)
        dt = dy3 * r['y3'] * (1.0 - r['sg'])
        dt_ref[...] = dt.astype(BF16)
        dy1 = (dy3 * r['sg'] + _dot(dt, wg_ref[...], _NT)) * r['dgelu']
        dyl_ref[...] = dy1.astype(BF16)
        dus_ref[...] = dy1 * d_ref[...]
        st_ref[1:2, 0:SSM_W] += jnp.sum(dt, axis=0, keepdims=True)
        st_ref[1:2, SSM_W:2 * SSM_W] += jnp.sum(dy1 * us_ref[...], axis=0, keepdims=True)
        dyp_ref[...] = dys[2] * r['sp']
        dpj_ref[:, OFF_ZP:OFF_ZP + POOL_W] = (dys[2] * yp_ref[...] * r['dsp']).astype(BF16)

    blk = pl.BlockSpec((tr, d), lambda i: (i, 0))
    half = pl.BlockSpec((tr, ATT_W), lambda i: (i, 0))
    wide = pl.BlockSpec((tr, IN_W), lambda i: (i, 0))
    sds = jax.ShapeDtypeStruct
    in_specs = _merge_inputs(tr) + [blk]
    out_specs = [half] * 4 + [wide] + [half] * 5 + [blk] * 5 + [_full((SUBLANES, d))]
    body, x_in, x_out, x_shapes, x_scratch, x_args = _carry_sends(body, len(in_specs), len(out_specs), 0, (l // tr,), sends)
    outs = pl.pallas_call(
        body, name="merge_bwd" if sends is None else "merge_bwd_sending", grid=(l // tr,), in_specs=in_specs + x_in,
        out_specs=out_specs + x_out,
        out_shape=[sds((l, ATT_W), t) for t in (BF16, BF16, F32, F32)] + [sds((l, IN_W), BF16)]
        + [sds((l, ATT_W), BF16)] * 5 + [sds((l, d), BF16)] * 5 + [sds((SUBLANES, d), F32)] + x_shapes,
        scratch_shapes=x_scratch, compiler_params=_params(("arbitrary",)))(
            proj, proj, proj, proj, proj, o, y_lin, y_pool, ssm_d, w_glu, b_glu, wba, wbs, wbp, w_out, mod, dxo, *x_args)
    return list(outs[:len(out_specs)]), list(outs[len(out_specs):])


def _ada_fwd(c_all, w_ada, b_shard):
    depth, d, n = w_ada.shape

    def body(c_ref, w_ref, b_ref, o_ref):
        act, _ = _silu_and_grad(c_ref[...])
        o_ref[0] = _dot(act, w_ref[0]) + b_ref[0]

    return pl.pallas_call(
        body, name="ada_fwd", grid=(depth,),
        in_specs=[_full((N_DEV, d)), pl.BlockSpec((1, d, n), lambda i: (i, 0, 0)), pl.BlockSpec((1, 1, n), lambda i: (i, 0, 0))],
        out_specs=pl.BlockSpec((1, N_DEV, n), lambda i: (i, 0, 0)), out_shape=jax.ShapeDtypeStruct((depth, N_DEV, n), F32),
        compiler_params=_params(("parallel",)))(c_all, w_ada, b_shard)


def _ada_bwd(c_all_t, dmod):
    d = c_all_t.shape[0]
    depth, _, n = dmod.shape

    def body(c_ref, dm_ref, o_ref):
        act, _ = _silu_and_grad(c_ref[...])
        acc = act[:, 0:1] * dm_ref[0, 0:1, :]
        for b in range(1, N_DEV):
            acc = acc + act[:, b:b + 1] * dm_ref[0, b:b + 1, :]
        o_ref[0] = acc

    return pl.pallas_call(
        body, name="ada_bwd", grid=(depth,),
        in_specs=[_full((d, N_DEV)), pl.BlockSpec((1, N_DEV, n), lambda i: (i, 0, 0))],
        out_specs=pl.BlockSpec((1, d, n), lambda i: (i, 0, 0)), out_shape=jax.ShapeDtypeStruct((depth, d, n), F32),
        compiler_params=_params(("parallel",)))(c_all_t, dmod)


ROWS_FLAT = 256


_RELATION_XOR = (0, 2, 1, 3)


def _sum_chips(own, others, chip_order, name):
    r, c = own.shape
    tr = math.gcd(ROWS_FLAT, r)

    def body(own_ref, oth_ref, o_ref):
        terms = [own_ref[...].astype(F32)] + [oth_ref[k].astype(F32) for k in range(3)]
        if chip_order:
            chip = 2 * lax.axis_index("x") + lax.axis_index("y")
            by_chip = []
            for q in range(N_CHIPS):
                rel = jnp.bitwise_xor(chip, q)
                pick = terms[3]
                for k in (2, 1, 0):
                    pick = jnp.where(rel == _RELATION_XOR[k], terms[k], pick)
                by_chip.append(pick)
            terms = by_chip
        o_ref[...] = ((terms[0] + terms[1]) + terms[2]) + terms[3]

    return pl.pallas_call(
        body, name=name, grid=(r // tr,),
        in_specs=[pl.BlockSpec((tr, c), lambda i: (i, 0)), pl.BlockSpec((3, tr, c), lambda i: (0, i, 0))],
        out_specs=pl.BlockSpec((tr, c), lambda i: (i, 0)), out_shape=jax.ShapeDtypeStruct((r, c), F32),
        compiler_params=_params(("parallel",)))(own, others)


def _add2(a, b, out_dtype, name):
    shape = a.shape
    a, b = a.reshape(-1, shape[-1]), b.reshape(-1, shape[-1])
    r, c = a.shape
    tr = math.gcd(ROWS_FLAT, r)

    def body(a_ref, b_ref, o_ref):
        o_ref[...] = (a_ref[...] + b_ref[...]).astype(out_dtype)

    blk = pl.BlockSpec((tr, c), lambda i: (i, 0))
    return pl.pallas_call(
        body, name=name, grid=(r // tr,), in_specs=[blk, blk], out_specs=blk,
        out_shape=jax.ShapeDtypeStruct((r, c), out_dtype), compiler_params=_params(("parallel",)))(a, b).reshape(shape)


def _adamw(w, g, m, v, name):
    r, c = w.shape
    tr = math.gcd(ROWS_FLAT, r)

    def body(w_ref, g_ref, m_ref, v_ref, d_ref, nm_ref, nv_ref):
        gv = g_ref[...]
        mv = ADAM_B1 * m_ref[...] + (1.0 - ADAM_B1) * gv
        vv = ADAM_B2 * v_ref[...] + (1.0 - ADAM_B2) * (gv * gv)
        m_hat = mv / (1.0 - ADAM_B1 ** ADAM_STEP)
        v_hat = vv / (1.0 - ADAM_B2 ** ADAM_STEP)
        d_ref[...] = -ADAM_LR * (m_hat / (jnp.sqrt(v_hat) + ADAM_EPS) + ADAM_WD * w_ref[...])
        nm_ref[...] = mv
        nv_ref[...] = vv

    blk = pl.BlockSpec((tr, c), lambda i: (i, 0))
    return pl.pallas_call(
        body, name=name, grid=(r // tr,), in_specs=[blk] * 4, out_specs=[blk] * 3,
        out_shape=[jax.ShapeDtypeStruct((r, c), F32)] * 3, compiler_params=_params(("parallel",)))(w, g, m, v)


_GROUP_MASKS = {
    'xy': ((1, 0, 0), (0, 1, 0), (1, 1, 0)),
    'xyc': ((0, 0, 1), (0, 1, 0), (0, 1, 1), (1, 0, 0), (1, 0, 1), (1, 1, 0), (1, 1, 1)),
}


def _group_rank(group, pos):
    x, y, c = pos
    return {'xy': 2 * x + y, 'xyc': 4 * x + 2 * y + c}[group]


def _gather(src, group, name):
    masks = _GROUP_MASKS[group]
    n = len(masks) + 1

    def body(src_ref, out_ref, send_sems, recv_sems, local_sem):
        me = (lax.axis_index("x"), lax.axis_index("y"), lax.axis_index("c"))
        my_rank = _group_rank(group, me)
        local = pltpu.make_async_copy(src_ref, out_ref.at[my_rank], local_sem)
        local.start()
        sends = []
        for k, mask in enumerate(masks):
            peer = tuple(1 - p if f else p for p, f in zip(me, mask))
            send = _remote(src_ref, out_ref.at[my_rank], send_sems.at[k], recv_sems.at[k], peer)
            send.start()
            sends.append((send, peer))
        for k, (send, peer) in enumerate(sends):
            _remote(src_ref, out_ref.at[_group_rank(group, peer)], send_sems.at[k], recv_sems.at[k], peer).wait_recv()
        for send, _ in sends:
            send.wait_send()
        local.wait()

    return pl.pallas_call(
        body, name=name, in_specs=[pl.BlockSpec(memory_space=pl.ANY)], out_specs=pl.BlockSpec(memory_space=pl.ANY),
        out_shape=jax.ShapeDtypeStruct((n,) + tuple(src.shape), src.dtype),
        scratch_shapes=[pltpu.SemaphoreType.DMA((n - 1,)), pltpu.SemaphoreType.DMA((n - 1,)), pltpu.SemaphoreType.DMA(())],
    )(src)


CHUNK_BYTES = 1 << 20
MIN_CHUNK_ROWS = 64


def _row_parts(rows, cols, itemsize):
    n = 1
    while rows % (2 * n) == 0 and rows // (2 * n) >= MIN_CHUNK_ROWS and rows * cols * itemsize > n * CHUNK_BYTES:
        n *= 2
    return n


def _remote(src, dst, send_sem, recv_sem, peer):
    return pltpu.make_async_remote_copy(src_ref=src, dst_ref=dst, send_sem=send_sem, recv_sem=recv_sem,
                                        device_id=peer, device_id_type=pl.DeviceIdType.MESH)


def _start_rows(src, dst, send_sem, recv_sem, peer):
    rows, cols = src.shape
    n = _row_parts(rows, cols, jnp.dtype(src.dtype).itemsize)
    pr = rows // n
    for i in range(n):
        _remote(src.at[pl.ds(i * pr, pr), :], dst.at[pl.ds(i * pr, pr), :], send_sem, recv_sem, peer).start()


def _mesh_place():
    x, y, c = lax.axis_index("x"), lax.axis_index("y"), lax.axis_index("c")
    other_chips = ((1 - x, y), (x, 1 - y), (1 - x, 1 - y))
    return x, y, c, 2 * x + y, (x, y, 1 - c), other_chips


def _comm_call(body, name, ins, out_shapes, sem_counts):
    any_spec = pl.BlockSpec(memory_space=pl.ANY)
    return pl.pallas_call(
        body, name=name, in_specs=[any_spec] * len(ins), out_specs=[any_spec] * len(out_shapes), out_shape=out_shapes,
        scratch_shapes=[pltpu.SemaphoreType.DMA((n,)) for n in sem_counts])(*ins)


class _Sends:
    sem_arrays = 2

    def run(self, name):
        k_in, k_out = len(self.srcs), len(self.out_shapes)

        def body(*refs):
            ins, outs, sems = refs[:k_in], refs[k_in:k_in + k_out], refs[k_in + k_out:]
            self.start(ins, outs, *sems)
            self.finish(ins, outs, *sems)

        return _comm_call(body, name, self.srcs, self.out_shapes, (self.n_sems,) * self.sem_arrays)

    def specs(self):
        any_spec = pl.BlockSpec(memory_space=pl.ANY)
        return ([any_spec] * len(self.srcs), [any_spec] * len(self.out_shapes),
                [pltpu.SemaphoreType.DMA((self.n_sems,))] * self.sem_arrays)


class _SiblingSends(_Sends):
    def __init__(self, srcs):
        self.srcs = list(srcs)
        self.n_sems = len(self.srcs)
        self.out_shapes = [jax.ShapeDtypeStruct(g.shape[1:], g.dtype) for g in self.srcs]

    def start(self, ins, outs, send_sems, recv_sems):
        _, _, c, _, sibling, _ = _mesh_place()
        for t in range(len(ins)):
            for q in range(ins[t].shape[1]):
                _start_rows(ins[t].at[1 - c, q], outs[t].at[q], send_sems.at[t], recv_sems.at[t], sibling)

    def finish(self, ins, outs, send_sems, recv_sems):
        _, _, _, _, sibling, _ = _mesh_place()
        for t in range(len(ins)):
            arrived = _remote(outs[t], outs[t], send_sems.at[t], recv_sems.at[t], sibling)
            arrived.wait_recv()
            arrived.wait_send()


class _ChipSends(_Sends):
    def __init__(self, srcs, pick, pass_on=False):
        self.srcs, self.pick, self.pass_on = list(srcs), pick, pass_on
        nt = len(self.srcs)
        self.n_sems = 3 * nt
        self.sem_arrays = 4 if pass_on else 2
        landing = []
        for s in self.srcs:
            r2 = s.shape[-2] // 2 if pick is _pick_weight_half else s.shape[-2]
            landing.append(jax.ShapeDtypeStruct((3, r2, s.shape[-1]), s.dtype))
        self.out_shapes = landing + (landing if pass_on else [])

    def start(self, ins, outs, send_sems, recv_sems, *unused):
        _, _, c, _, _, other_chips = _mesh_place()
        for t in range(len(ins)):
            for j, (px, py) in enumerate(other_chips):
                _remote(self.pick(ins[t], c, 2 * px + py), outs[t].at[j], send_sems.at[3 * t + j], recv_sems.at[3 * t + j],
                        (px, py, c)).start()

    def finish(self, ins, outs, send_sems, recv_sems, *sibling_sems):
        _, _, c, _, sibling, other_chips = _mesh_place()
        nt = len(ins)
        for t in range(nt):
            for j, (px, py) in enumerate(other_chips):
                k = 3 * t + j
                landed = _remote(outs[t].at[j], outs[t].at[j], send_sems.at[k], recv_sems.at[k], (px, py, c))
                landed.wait_recv()
                if self.pass_on:
                    _start_rows(outs[t].at[j], outs[nt + t].at[j], sibling_sems[0].at[k], sibling_sems[1].at[k], sibling)
                landed.wait_send()
        if self.pass_on:
            for t in range(nt):
                for j in range(3):
                    k = 3 * t + j
                    passed = _remote(outs[nt + t].at[j], outs[nt + t].at[j], sibling_sems[0].at[k], sibling_sems[1].at[k],
                                     sibling)
                    passed.wait_recv()
                    passed.wait_send()


def _pick_weight_half(src, core, chip):
    r2 = src.shape[0] // 2
    return src.at[pl.ds(pl.multiple_of(core * r2, MIN_CHUNK_ROWS), r2), :]


def _pick_slab(src, core, chip):
    return src.at[chip if src.shape[0] == N_CHIPS else 0]


def _swap_cores(halves):
    nt = len(halves)

    def body(*refs):
        ins, outs = refs[:nt], refs[nt:2 * nt]
        send_sems, recv_sems = refs[2 * nt:]
        _, _, _, _, sibling, _ = _mesh_place()
        for t in range(nt):
            for li in range(ins[t].shape[0]):
                _start_rows(ins[t].at[li], outs[t].at[li], send_sems.at[t], recv_sems.at[t], sibling)
        for t in range(nt):
            _remote(outs[t], outs[t], send_sems.at[t], recv_sems.at[t], sibling).wait_recv()
        for t in range(nt):
            _remote(ins[t], ins[t], send_sems.at[t], recv_sems.at[t], sibling).wait_send()

    out_shapes = [jax.ShapeDtypeStruct(h.shape, h.dtype) for h in halves]
    return _comm_call(body, "swap_cores", halves, out_shapes, (nt, nt))


def _pack_rows(pieces, cols, dtype):
    tile = SUBLANES * cols
    rows = []
    for p in pieces:
        flat = p.reshape(-1).astype(dtype)
        rows.append(jnp.pad(flat, (0, (-flat.shape[0]) % tile)).reshape(-1, cols))
    total = sum(r.shape[0] for r in rows)
    if total % (2 * SUBLANES):
        rows.append(jnp.zeros((SUBLANES, cols), dtype))
    return jnp.concatenate(rows, axis=0)


def _unpack_rows(buf, shapes):
    out, row = [], 0
    for s in shapes:
        n = math.prod(s)
        nrows = -(-n // (SUBLANES * buf.shape[1])) * SUBLANES
        out.append(buf[row:row + nrows].reshape(-1)[:n].reshape(s))
        row += nrows
    return out


ROW_SHARDED = ('w_out', 'w_glu')


def _chips_to_full(name, chips):
    mats = [c.reshape(-1, c.shape[-1]) for c in chips]
    if name in ROW_SHARDED:
        return jnp.concatenate(mats, axis=0)
    if name != 'w_in':
        return jnp.concatenate(mats, axis=1)
    cs = mats[0].shape[1]
    pieces = []
    for start, size in _PERM_PIECES:
        lo = start
        while lo < start + size:
            q = lo // cs
            hi = min(start + size, (q + 1) * cs)
            pieces.append(mats[q][:, lo - q * cs:hi - q * cs])
            lo = hi
    return jnp.concatenate(pieces, axis=1)


def _full_to_dests(name, full):
    rows, cols = full.shape
    if name in ROW_SHARDED:
        return jnp.transpose(full.reshape(N_CHIPS, 2, rows // (2 * N_CHIPS), cols), (1, 0, 2, 3))
    cs = cols // N_CHIPS
    if name != 'w_in':
        dests = [full[:, q * cs:(q + 1) * cs] for q in range(N_CHIPS)]
    else:
        offsets, off = [], 0
        for _, size in _PERM_PIECES:
            offsets.append(off)
            off += size
        by_start = sorted(zip(_PERM_PIECES, offsets))
        dests = []
        for q in range(N_CHIPS):
            parts = []
            for (start, size), at in by_start:
                lo, hi = max(start, q * cs), min(start + size, (q + 1) * cs)
                if lo < hi:
                    parts.append(full[:, at + lo - start:at + hi - start])
            dests.append(jnp.concatenate(parts, axis=1))
    return jnp.stack([d.reshape(2, rows // 2, cs) for d in dests], axis=1)


def _by_chip(own, related, x, y):
    grid = ((own, related[1]), (related[0], related[2]))
    along_x = [[jnp.where(x == 0, grid[px][dy], grid[1 - px][dy]) for dy in range(2)] for px in range(2)]
    return [jnp.where(y == 0, along_x[px][py], along_x[px][1 - py]) for px in range(2) for py in range(2)]


def _join_halves(mine, theirs, core, axis):
    return jnp.where(core == 0, jnp.concatenate([mine, theirs], axis=axis), jnp.concatenate([theirs, mine], axis=axis))


ROWS_WGRAD = 2048
SSM_RAW = ('ssm_a_re', 'ssm_a_im', 'ssm_log_dt', 'ssm_b_re', 'ssm_b_im', 'ssm_c_re', 'ssm_c_im')


def _all_tables(p):
    raw = tuple(p[k] for k in SSM_RAW)
    (_, bb, cb), vjp = jax.vjp(jax.vmap(_ssm_tables), *raw)
    return (bb.astype(BF16), cb.astype(BF16), jax.vmap(_ssm_powers)(*raw[:3])), vjp


def _layer_fwd(x, p, w_in, rest, li, mod, tables, sends=None, late_sends=None):
    bb, cb, pw = (t[li] for t in tables)
    h = _norm_fwd(x, _row(p['norm_g'][li]), mod)
    if isinstance(rest, dict):
        proj = _matmul(h, w_in, 'nn', tm=1024, tn=1280, tk=1024, name="proj", n_outer=True)
    else:
        proj, arrived = _matmul(h, w_in, 'nn', tm=1024, tn=1280, tk=1024, name="proj_sending", n_outer=True, sends=rest[0])
        rest = rest[1](arrived)
    full = dict(rest, w_in=w_in)
    o = _attention_fwd(proj, p['attn_sinks'][li])
    y_lin, states, brought = _ssm_fwd(proj, bb, cb, pw, sends)
    y_pool = _pool_fwd(proj, p['w_pool'][li].astype(BF16), _row(p['pool_scale'][li]))
    x_new, brought_late = _merge_fwd(proj, o, y_lin, y_pool, x, _row(p['ssm_d'][li]), full['w_glu'],
                                     _row(p['b_glu'][li]), full['w_br_att'], full['w_br_ssm'], full['w_br_pool'],
                                     full['w_out'], mod, late_sends)
    saved = dict(x=x, h=h, proj=proj, o=o, y_lin=y_lin, y_pool=y_pool, states=states, bb=bb, cb=cb, pw=pw, full=full)
    a, b = len(brought) // 2, len(brought_late) // 2
    return x_new, saved, brought[:a] + brought_late[:b] + brought[a:] + brought_late[b:]


def _layer_bwd(dxo, s, p, li, mod, core, upper=None, own_early=(), w_in_sender=None):
    l = dxo.shape[0]
    proj, full = s['proj'], s['full']
    outs, arrived_upper = _merge_bwd(
        proj, s['o'], s['y_lin'], s['y_pool'], dxo, _row(p['ssm_d'][li]), full['w_glu'], _row(p['b_glu'][li]),
        full['w_br_att'], full['w_br_ssm'], full['w_br_pool'], full['w_out'], mod,
        None if upper is None else _ChipSends(upper, _pick_slab))
    (do, dyl, dus_skip, dyp, dproj, ya, ys, ypl, y2, dt, dpa, dps, dpp, mg, dout, st) = outs
    g = {}
    g['w_br_att'] = _matmul(ya, dpa, 'tn', tm=512, tn=1024, tk=ROWS_WGRAD, name="grad_w_br")
    g['w_br_ssm'] = _matmul(ys, dps, 'tn', tm=512, tn=1024, tk=ROWS_WGRAD, name="grad_w_br")
    g['w_br_pool'] = _matmul(ypl, dpp, 'tn', tm=512, tn=1024, tk=ROWS_WGRAD, name="grad_w_br")
    g['w_out'] = _matmul(mg, dout, 'tn', tm=512, tn=1024, tk=ROWS_WGRAD, name="grad_w_out")
    g['w_glu'] = _matmul(y2, dt, 'tn', tm=512, tn=512, tk=ROWS_WGRAD, name="grad_w_glu")
    g['b_glu'] = st[1, 0:SSM_W]
    g['ssm_d'] = st[1, SSM_W:2 * SSM_W]
    dgate = st[0]

    early = _layer_grad_halves(g, own_early)
    dq, dkv, dsink, theirs = _attention_bwd(proj, p['attn_sinks'][li], do, _SiblingSends(early) if early else None)
    g['attn_sinks'] = dsink[:, 0]
    pair = _pair_sums(early, [BF16] * len(early), core, theirs) if early else []
    sends = _ChipSends(pair, _pick_slab) if pair else None
    dus_scan, dbb, dcb, dlam, brought = _ssm_bwd(proj, s['states'], dyl, s['bb'], s['cb'], s['pw'], sends)
    g['ssm_tables'] = (dlam[:, 0:2, :], dbb, dcb)
    dup, dwp, dps_scale = _pool_bwd(proj, p['w_pool'][li].astype(BF16), _row(p['pool_scale'][li]), dyp)
    g['w_pool'] = dwp
    g['pool_scale'] = dps_scale[0]

    for off, piece in ((OFF_Q, dq), (OFF_US, dus_skip + dus_scan), (OFF_UP, dup), (OFF_KV, dkv[HALO:])):
        dproj = lax.dynamic_update_slice(dproj, piece.astype(BF16), (0, off))
    g['w_in'] = _matmul(s['h'], dproj, 'tn', tm=1024, tn=1280, tk=ROWS_WGRAD, name="grad_w_in")
    if w_in_sender is None:
        dh, brought_late = _matmul(dproj, full['w_in'], 'nt', tm=1024, tn=1024, tk=3200, name="grad_h"), None
    else:
        dh, brought_late = _matmul(dproj, full['w_in'], 'nt', tm=1024, tn=1024, tk=3200, name="grad_h_sending",
                                   sends=w_in_sender(g['w_in']))
    dx, nst = _norm_bwd(s['x'], dh, dxo, _row(p['norm_g'][li]), mod)
    g['norm_g'] = nst[2]
    dmod = jnp.concatenate([nst[0], nst[1], dgate])
    del l
    return dx, g, dmod, (pair, brought), arrived_upper, brought_late


BIG_NAMES = tuple(name for name, _ in BIG)


def _weight_sends(shards, names, li):
    return _ChipSends([shards[name][li] for name in names], _pick_weight_half, pass_on=True)


def _assemble_layer(shards, names, li, arrived, place):
    x, y, core, _ = place
    over_ici, from_sibling = arrived[:len(names)], arrived[len(names):]
    full = {}
    for name, a, b in zip(names, over_ici, from_sibling):
        own = shards[name][li]
        own_halves = own.reshape(2, own.shape[0] // 2, own.shape[1])
        halves = [_join_halves(a[j][None], b[j][None], core, 0) for j in range(3)]
        full[name] = _chips_to_full(name, _by_chip(own_halves, halves, x, y))
    return full


def _pair_sums(tensors, wire, core, theirs=None):
    if theirs is None:
        theirs = _SiblingSends(tensors).run("swap_halves")
    return [_add2(lax.dynamic_index_in_dim(g, core, 0, keepdims=False), a, dt, "sum_core_pair")
            for g, a, dt in zip(tensors, theirs, wire)]


def _layer_grad_halves(layer_grads, names):
    return [_full_to_dests(name, layer_grads[name]) for name in names]


def _step(p, m, v, x, c, target):
    depth = p['norm_g'].shape[0]
    d = D_MODEL
    ix, iy, ic = lax.axis_index("x"), lax.axis_index("y"), lax.axis_index("c")
    chip = 2 * ix + iy
    dev = 4 * ix + 2 * iy + ic
    x0 = x[0]

    c_pad = jnp.pad(c, ((0, SUBLANES - 1), (0, 0)))
    c_all = _gather(c_pad, 'xyc', "gather_c")[:, 0, :]
    n_ada = p['w_ada'].shape[-1]
    b_shard = lax.dynamic_slice_in_dim(p['b_ada'], chip * n_ada, n_ada, axis=1)[:, None, :]
    mod_shard = _ada_fwd(c_all, p['w_ada'].astype(BF16), b_shard)
    mod_all = _gather(mod_shard.reshape(depth * N_DEV, n_ada), 'xy', "gather_mod")
    mod_all = jnp.transpose(mod_all.reshape(N_CHIPS, depth, N_DEV, n_ada), (1, 2, 0, 3)).reshape(depth, N_DEV, 3 * d)
    mods = lax.dynamic_index_in_dim(mod_all, dev, axis=1, keepdims=True)

    place = (ix, iy, ic, chip)
    n_big = len(BIG)

    shards = {name: p[name].astype(BF16) for name, _ in BIG}
    tables, tables_vjp = _all_tables(p)
    others = BIG_NAMES[1:]
    w_in = _assemble_layer(shards, BIG_NAMES[:1], 0, _weight_sends(shards, BIG_NAMES[:1], 0).run("send_weight_halves"),
                           place)['w_in']
    rest = (_weight_sends(shards, others, 0), lambda arrived: _assemble_layer(shards, others, 0, arrived, place))
    saved = []
    xs = x0
    for li in range(depth):
        more = li + 1 < depth
        xs, s, arrived = _layer_fwd(xs, p, w_in, rest, li, mods[li], tables,
                                    _weight_sends(shards, BIG_NAMES[:1], li + 1) if more else None,
                                    _weight_sends(shards, others, li + 1) if more else None)
        saved.append(s)
        if more:
            rest = _assemble_layer(shards, BIG_NAMES, li + 1, arrived, place)
            w_in = rest.pop('w_in')
    dx, hst = _loss_head(xs, _row(p['final_g']), target[0])
    loss = lax.psum(hst[1, 0], ("x", "y", "c"))

    grads, dmods, pair, from_chips = [None] * depth, [None] * depth, [None] * depth, [None] * depth
    w_in_halves, w_in_pair, w_in_arrived = {}, {}, {}

    def w_in_sender(li):
        def to_chips(g_w_in):
            w_in_pair[li] = _pair_sums([_full_to_dests('w_in', g_w_in)], [BF16], ic)
            return _ChipSends(w_in_pair[li], _pick_slab)

        def to_sibling(g_w_in):
            w_in_halves[li] = [_full_to_dests('w_in', g_w_in)]
            return _SiblingSends(w_in_halves[li])

        return to_chips if li == 0 else to_sibling

    upper = None
    for li in reversed(range(depth)):
        dx, grads[li], dmods[li], early, arrived_upper, arrived_late = _layer_bwd(
            dx, saved[li], p, li, mods[li], ic, upper, others, w_in_sender(li))
        if upper is not None:
            w_in_arrived[li + 1] = arrived_upper
        if li > 0:
            w_in_pair[li] = _pair_sums(w_in_halves[li], [BF16], ic, arrived_late)
            upper = w_in_pair[li]
        else:
            w_in_arrived[li] = arrived_late
        pair[li] = w_in_pair[li] + early[0]
        from_chips[li] = early[1]
    for li in range(depth):
        from_chips[li] = list(w_in_arrived[li]) + list(from_chips[li])

    dmod_pad = jnp.pad(jnp.stack(dmods), ((0, SUBLANES - depth), (0, 0)))
    dmod_all = _gather(dmod_pad, 'xyc', "gather_dmod")[:, :depth, :]
    dmod_cols = lax.dynamic_slice_in_dim(jnp.transpose(dmod_all, (1, 0, 2)), chip * n_ada, n_ada, axis=2)
    g_w_ada = _ada_bwd(jnp.transpose(c_all), dmod_cols)

    local_small = {k: jnp.stack([grads[li][k] for li in range(depth)])
                   for k in SMALL if k not in ('final_g', 'b_ada') + SSM_RAW}
    table_cotangents = tuple(jnp.stack([grads[li]['ssm_tables'][k] for li in range(depth)]) for k in range(3))
    local_small.update(zip(SSM_RAW, tables_vjp(table_cotangents)))
    local_small['final_g'] = hst[0]
    local_small['b_ada'] = jnp.stack(dmods)
    small_pack = _pack_rows([local_small[k] for k in SMALL], PACK_COLS, F32)

    small_halves = small_pack.reshape(2, 1, small_pack.shape[0] // 2, small_pack.shape[1])
    small_pair = _pair_sums([small_halves], [F32], ic)
    pair[0] = pair[0] + small_pair
    from_chips[0] = from_chips[0] + list(_ChipSends(small_pair, _pick_slab).run("scatter_chips"))
    halves = []
    for li in range(depth):
        for t, (ps, others) in enumerate(zip(pair[li], from_chips[li])):
            small = t == n_big
            own = ps[0] if small else lax.dynamic_index_in_dim(ps, chip, 0, keepdims=False)
            halves.append(_sum_chips(own, others, small, "sum_chips_small" if small else "sum_chips")[None])
    totals = [_join_halves(h[0], o[0], ic, 0) for h, o in zip(halves, _swap_cores(halves))]
    small_sum = totals[n_big]
    per_layer = [totals[:n_big]] + [totals[n_big + 1 + (li - 1) * n_big:n_big + 1 + li * n_big] for li in range(1, depth)]

    small_shapes = [p[k].shape for k in SMALL]
    grad = dict(zip(SMALL, _unpack_rows(small_sum, small_shapes)))
    grad['w_ada'] = g_w_ada
    for t, (name, _) in enumerate(BIG):
        grad[name] = jnp.stack([per_layer[li][t] for li in range(depth)])

    delta, new_m, new_v = {}, {}, {}
    outs = _adamw(_pack_rows([p[k] for k in SMALL], PACK_COLS, F32), small_sum,
                  _pack_rows([m[k] for k in SMALL], PACK_COLS, F32), _pack_rows([v[k] for k in SMALL], PACK_COLS, F32),
                  name="adamw_small")
    for res, o in zip((delta, new_m, new_v), outs):
        res.update(zip(SMALL, _unpack_rows(o, small_shapes)))
    for name in ['w_ada'] + [n for n, _ in BIG]:
        shape = p[name].shape
        two_d = (-1, shape[-1])
        outs = _adamw(p[name].reshape(two_d), grad[name].reshape(two_d), m[name].reshape(two_d), v[name].reshape(two_d),
                      name="adamw_" + name)
        delta[name], new_m[name], new_v[name] = (o.reshape(shape) for o in outs)

    return (loss, dx[None], *[grad[k] for k in WEIGHTS], *[delta[k] for k in WEIGHTS],
            *[new_m[k] for k in WEIGHTS], *[new_v[k] for k in WEIGHTS])


def kernel(x, c, norm_g, w_ada, b_ada, w_in, attn_sinks, ssm_a_re, ssm_a_im, ssm_log_dt, ssm_b_re, ssm_b_im, ssm_c_re, ssm_c_im, ssm_d, w_glu, b_glu, w_pool, pool_scale, w_br_att, w_br_ssm, w_br_pool, w_out, final_g, loss_target, m_norm_g, m_w_ada, m_b_ada, m_w_in, m_attn_sinks, m_ssm_a_re, m_ssm_a_im, m_ssm_log_dt, m_ssm_b_re, m_ssm_b_im, m_ssm_c_re, m_ssm_c_im, m_ssm_d, m_w_glu, m_b_glu, m_w_pool, m_pool_scale, m_w_br_att, m_w_br_ssm, m_w_br_pool, m_w_out, m_final_g, v_norm_g, v_w_ada, v_b_ada, v_w_in, v_attn_sinks, v_ssm_a_re, v_ssm_a_im, v_ssm_log_dt, v_ssm_b_re, v_ssm_b_im, v_ssm_c_re, v_ssm_c_im, v_ssm_d, v_w_glu, v_b_glu, v_w_pool, v_pool_scale, v_w_br_att, v_w_br_ssm, v_w_br_pool, v_w_out, v_final_g):
    p = dict(zip(WEIGHTS, (norm_g, w_ada, b_ada, w_in, attn_sinks, ssm_a_re, ssm_a_im, ssm_log_dt, ssm_b_re, ssm_b_im,
                           ssm_c_re, ssm_c_im, ssm_d, w_glu, b_glu, w_pool, pool_scale, w_br_att, w_br_ssm, w_br_pool,
                           w_out, final_g)))
    m = dict(zip(WEIGHTS, (m_norm_g, m_w_ada, m_b_ada, m_w_in, m_attn_sinks, m_ssm_a_re, m_ssm_a_im, m_ssm_log_dt,
                           m_ssm_b_re, m_ssm_b_im, m_ssm_c_re, m_ssm_c_im, m_ssm_d, m_w_glu, m_b_glu, m_w_pool,
                           m_pool_scale, m_w_br_att, m_w_br_ssm, m_w_br_pool, m_w_out, m_final_g)))
    v = dict(zip(WEIGHTS, (v_norm_g, v_w_ada, v_b_ada, v_w_in, v_attn_sinks, v_ssm_a_re, v_ssm_a_im, v_ssm_log_dt,
                           v_ssm_b_re, v_ssm_b_im, v_ssm_c_re, v_ssm_c_im, v_ssm_d, v_w_glu, v_b_glu, v_w_pool,
                           v_pool_scale, v_w_br_att, v_w_br_ssm, v_w_br_pool, v_w_out, v_final_g)))
    return _step(p, m, v, x, c, loss_target)
```

```python
import functools
import math

import jax
import jax.numpy as jnp
from jax import lax
from jax.experimental import pallas as pl
from jax.experimental.pallas import tpu as pltpu

F32 = jnp.float32
BF16 = jnp.bfloat16

D_MODEL = 1024
CHUNK = 64
N_HEADS = 8
N_KV_HEADS = 2
HEAD_DIM = 64
Q_PER_KV = N_HEADS // N_KV_HEADS
HALO = 128
ATT_W = 512
KV_W = 128
SSM_W = 512
SSM_GROUP = 16
SSM_GROUPS = 32
SSM_STATE = 64
POOL_W = 512
POOL_GW = 128
IN_W = 6400
EPS = 1e-6
NEG_INF = -1e30
ADAM_LR = 0.001
ADAM_B1 = 0.9
ADAM_B2 = 0.999
ADAM_EPS = 1e-08
ADAM_WD = 0.01
ADAM_STEP = 10

OFF_GL, OFF_Q, OFF_ZA, OFF_US, OFF_ZS, OFF_UP, OFF_ZP, OFF_KV = 0, 3072, 3584, 4096, 4608, 5120, 5632, 6144
_PERM_PIECES = ((3328, 3072), (0, 512), (1792, 512), (768, 512), (2304, 512), (1280, 512), (2816, 512), (512, 256))

LANES = 128
SUBLANES = 8
N_SBLK = SSM_GROUPS * SSM_STATE // LANES
VMEM_LIMIT = 48 * 1024 * 1024

N_CHIPS = 4
N_DEV = 8

WEIGHTS = ['norm_g', 'w_ada', 'b_ada', 'w_in', 'attn_sinks', 'ssm_a_re', 'ssm_a_im', 'ssm_log_dt', 'ssm_b_re',
           'ssm_b_im', 'ssm_c_re', 'ssm_c_im', 'ssm_d', 'w_glu', 'b_glu', 'w_pool', 'pool_scale', 'w_br_att',
           'w_br_ssm', 'w_br_pool', 'w_out', 'final_g']
SMALL = ['norm_g', 'b_ada', 'attn_sinks', 'ssm_a_re', 'ssm_a_im', 'ssm_log_dt', 'ssm_b_re', 'ssm_b_im', 'ssm_c_re',
         'ssm_c_im', 'ssm_d', 'b_glu', 'w_pool', 'pool_scale', 'final_g']
BIG = (('w_in', (1024, 1600)), ('w_br_att', (512, 256)), ('w_br_ssm', (512, 256)), ('w_br_pool', (512, 256)),
       ('w_out', (256, 1024)), ('w_glu', (128, 512)))
PACK_COLS = 1024


def _params(sem=None, vmem=VMEM_LIMIT):
    return pltpu.CompilerParams(dimension_semantics=sem, vmem_limit_bytes=vmem)


def _row(v):
    return v.reshape(1, -1)


def _full(shape):
    nd = len(shape)
    return pl.BlockSpec(shape, lambda *_: (0,) * nd)


def _sigmoid(v):
    return 1.0 / (1.0 + jnp.exp(-v))


def _silu_and_grad(z):
    s = _sigmoid(z)
    silu = z * s
    return silu, s + silu * (1.0 - s)


_GELU_K = math.sqrt(2.0 / math.pi)


def _gelu_and_grad(v):
    inner = _GELU_K * (v + 0.044715 * v * v * v)
    th = jnp.tanh(inner)
    val = 0.5 * v * (1.0 + th)
    grad = 0.5 * (1.0 + th) + 0.5 * v * (1.0 - th * th) * _GELU_K * (1.0 + 3 * 0.044715 * v * v)
    return val, grad


_NN = (((1,), (0,)), ((), ()))
_NT = (((1,), (1,)), ((), ()))
_TN = (((0,), (0,)), ((), ()))


def _dot(a, b, dims=_NN):
    return lax.dot_general(a.astype(BF16), b.astype(BF16), dims, preferred_element_type=F32)


def _matmul(a, b, mode, *, tm, tn, tk, name, out_dtype=F32, n_outer=False, sends=None):
    if mode == 'nn':
        (m, k), (_, n) = a.shape, b.shape
    elif mode == 'nt':
        (m, k), (n, _) = a.shape, b.shape
    else:
        (k, m), (_, n) = a.shape, b.shape
    tm, tn, tk = min(tm, m), min(tn, n), min(tk, k)
    assert m % tm == 0 and n % tn == 0 and k % tk == 0, (name, a.shape, b.shape)
    nk = k // tk
    dims = {'nn': _NN, 'nt': _NT, 'tn': _TN}[mode]

    def body(a_ref, b_ref, o_ref, acc_ref):
        if nk == 1:
            o_ref[...] = _dot(a_ref[...], b_ref[...], dims).astype(out_dtype)
            return
        kk = pl.program_id(2)

        @pl.when(kk == 0)
        def _():
            acc_ref[...] = jnp.zeros_like(acc_ref)

        acc_ref[...] += _dot(a_ref[...], b_ref[...], dims)

        @pl.when(kk == nk - 1)
        def _():
            o_ref[...] = acc_ref[...].astype(out_dtype)

    def spec(shape, index):
        if n_outer:
            return pl.BlockSpec(shape, lambda j, i, kk: index(i, j, kk))
        return pl.BlockSpec(shape, index)

    a_spec = spec((tk, tm), lambda i, j, kk: (kk, i)) if mode == 'tn' else spec((tm, tk), lambda i, j, kk: (i, kk))
    b_spec = spec((tn, tk), lambda i, j, kk: (j, kk)) if mode == 'nt' else spec((tk, tn), lambda i, j, kk: (kk, j))
    grid = (n // tn, m // tm, nk) if n_outer else (m // tm, n // tn, nk)
    body, x_in, x_out, x_shapes, x_scratch, x_args = _carry_sends(body, 2, 1, 1, grid, sends)
    semantics = ("parallel", "parallel", "arbitrary") if sends is None else ("arbitrary",) * 3
    outs = pl.pallas_call(
        body, name=name, grid=grid, in_specs=[a_spec, b_spec] + x_in,
        out_specs=[spec((tm, tn), lambda i, j, kk: (i, j))] + x_out,
        out_shape=[jax.ShapeDtypeStruct((m, n), out_dtype)] + x_shapes,
        scratch_shapes=[pltpu.VMEM((tm, tn), F32)] + x_scratch, compiler_params=_params(semantics))(a, b, *x_args)
    return outs[0] if sends is None else (outs[0], list(outs[1:]))


ROWS_NORM = 512


def _norm_fwd(x, g, mod):
    l, d = x.shape
    tr = min(ROWS_NORM, l)

    def body(x_ref, g_ref, mod_ref, h_ref):
        xv = x_ref[...]
        r = lax.rsqrt(jnp.mean(xv * xv, axis=-1, keepdims=True) + EPS)
        shift, scale = mod_ref[:, 0:d], mod_ref[:, d:2 * d]
        h_ref[...] = ((xv * r * g_ref[...]) * (1.0 + scale) + shift).astype(BF16)

    return pl.pallas_call(
        body, name="norm_fwd", grid=(l // tr,),
        in_specs=[pl.BlockSpec((tr, d), lambda i: (i, 0)), _full((1, d)), _full((1, 3 * d))],
        out_specs=pl.BlockSpec((tr, d), lambda i: (i, 0)), out_shape=jax.ShapeDtypeStruct((l, d), BF16),
        compiler_params=_params(("parallel",)))(x, g, mod)


def _norm_bwd(x, dh, dxo, g, mod):
    l, d = x.shape
    tr = min(ROWS_NORM, l)

    def body(x_ref, dh_ref, dxo_ref, g_ref, mod_ref, dx_ref, st_ref):
        @pl.when(pl.program_id(0) == 0)
        def _():
            st_ref[...] = jnp.zeros_like(st_ref)

        xv, dhv = x_ref[...], dh_ref[...]
        r = lax.rsqrt(jnp.mean(xv * xv, axis=-1, keepdims=True) + EPS)
        xn = xv * r
        gv = g_ref[...]
        sc1 = 1.0 + mod_ref[:, d:2 * d]
        dxn = dhv * gv * sc1
        dx_ref[...] = dxo_ref[...] + r * (dxn - xn * jnp.mean(dxn * xn, axis=-1, keepdims=True))
        st_ref[0:1, :] += jnp.sum(dhv, axis=0, keepdims=True)
        st_ref[1:2, :] += jnp.sum(dhv * xn * gv, axis=0, keepdims=True)
        st_ref[2:3, :] += jnp.sum(dhv * xn * sc1, axis=0, keepdims=True)

    blk = pl.BlockSpec((tr, d), lambda i: (i, 0))
    return pl.pallas_call(
        body, name="norm_bwd", grid=(l // tr,),
        in_specs=[blk, blk, blk, _full((1, d)), _full((1, 3 * d))],
        out_specs=[blk, _full((SUBLANES, d))],
        out_shape=[jax.ShapeDtypeStruct((l, d), F32), jax.ShapeDtypeStruct((SUBLANES, d), F32)],
        compiler_params=_params(("arbitrary",)))(x, dh, dxo, g, mod)


def _loss_head(x, g, target):
    l, d = x.shape
    tr = min(ROWS_NORM, l)

    def body(x_ref, g_ref, t_ref, dx_ref, st_ref):
        @pl.when(pl.program_id(0) == 0)
        def _():
            st_ref[...] = jnp.zeros_like(st_ref)

        xv = x_ref[...]
        r = lax.rsqrt(jnp.mean(xv * xv, axis=-1, keepdims=True) + EPS)
        xn = xv * r
        gv = g_ref[...]
        err = xn * gv - t_ref[...]
        part = 0.5 * jnp.sum(jnp.mean(err * err, axis=-1, keepdims=True), axis=0, keepdims=True)
        dy = err * (1.0 / d)
        dxn = dy * gv
        dx_ref[...] = r * (dxn - xn * jnp.mean(dxn * xn, axis=-1, keepdims=True))
        st_ref[0:1, :] += jnp.sum(dy * xn, axis=0, keepdims=True)
        st_ref[1:2, :] += jnp.broadcast_to(part, (1, d))

    blk = pl.BlockSpec((tr, d), lambda i: (i, 0))
    return pl.pallas_call(
        body, name="loss_head", grid=(l // tr,), in_specs=[blk, _full((1, d)), blk],
        out_specs=[blk, _full((SUBLANES, d))],
        out_shape=[jax.ShapeDtypeStruct((l, d), F32), jax.ShapeDtypeStruct((SUBLANES, d), F32)],
        compiler_params=_params(("arbitrary",)))(x, g, target)


ROWS_ATT = 128
ROWS_ATT_BWD = 256
_SLOPES = tuple(2.0 ** (-8.0 * (h + 1) / N_HEADS) for h in range(N_HEADS))


def _att_mask(i, t):
    r = lax.broadcasted_iota(jnp.int32, (t, t + HALO), 0)
    j = lax.broadcasted_iota(jnp.int32, (t, t + HALO), 1)
    dist = jnp.abs(r + HALO - j).astype(F32)
    rc, jc = r // CHUNK, j // CHUNK
    allowed = (jc >= rc) & (jc <= rc + 2) & ((j >= HALO) | (i > 0))
    return dist, allowed


def _att_probs(qh, k, dist, allowed, slope, sink):
    s = _dot(qh, k, _NT) * (1.0 / math.sqrt(HEAD_DIM)) - slope * dist
    s = jnp.where(allowed, s, NEG_INF)
    m = jnp.maximum(jnp.max(s, axis=1, keepdims=True), sink)
    e = jnp.exp(s - m)
    es = jnp.exp(sink - m)
    den = jnp.sum(e, axis=1, keepdims=True) + es
    return e / den, es / den


def _att_specs(t):
    q_spec = pl.BlockSpec((t, ATT_W), lambda i: (i, OFF_Q // ATT_W))
    kv_spec = pl.BlockSpec((t, 2 * KV_W), lambda i: (i, OFF_KV // (2 * KV_W)))
    halo_spec = pl.BlockSpec((HALO, 2 * KV_W), lambda i: (jnp.maximum(i * (t // HALO) - 1, 0), OFF_KV // (2 * KV_W)))
    return q_spec, kv_spec, halo_spec


def _attention_fwd(proj, sinks):
    l = proj.shape[0]
    t = min(ROWS_ATT, l)

    def body(sink_ref, q_ref, kv_ref, halo_ref, o_ref):
        dist, allowed = _att_mask(pl.program_id(0), t)
        kv = jnp.concatenate([halo_ref[...], kv_ref[...]], axis=0)
        for h in range(N_HEADS):
            kh = h // Q_PER_KV
            k = kv[:, kh * HEAD_DIM:(kh + 1) * HEAD_DIM]
            v = kv[:, KV_W + kh * HEAD_DIM:KV_W + (kh + 1) * HEAD_DIM]
            p, _ = _att_probs(q_ref[:, h * HEAD_DIM:(h + 1) * HEAD_DIM], k, dist, allowed, _SLOPES[h], sink_ref[h])
            o_ref[:, h * HEAD_DIM:(h + 1) * HEAD_DIM] = _dot(p, v)

    q_spec, kv_spec, halo_spec = _att_specs(t)
    return pl.pallas_call(
        body, name="attention_fwd", grid=(l // t,),
        in_specs=[pl.BlockSpec(memory_space=pltpu.SMEM), q_spec, kv_spec, halo_spec],
        out_specs=pl.BlockSpec((t, ATT_W), lambda i: (i, 0)), out_shape=jax.ShapeDtypeStruct((l, ATT_W), F32),
        compiler_params=_params(("parallel",)))(sinks, proj, proj, proj)


def _attention_bwd(proj, sinks, do, sends=None):
    l = proj.shape[0]
    t = min(ROWS_ATT_BWD, l)

    def body(sink_ref, q_ref, kv_ref, halo_ref, do_ref, dq_ref, dkv_ref, dsink_ref):
        i = pl.program_id(0)

        @pl.when(i == 0)
        def _():
            dkv_ref[...] = jnp.zeros_like(dkv_ref)
            dsink_ref[...] = jnp.zeros_like(dsink_ref)

        dist, allowed = _att_mask(i, t)
        kv = jnp.concatenate([halo_ref[...], kv_ref[...]], axis=0)
        rows = pl.ds(pl.multiple_of(i * t, t), t + HALO)
        for kh in range(N_KV_HEADS):
            k = kv[:, kh * HEAD_DIM:(kh + 1) * HEAD_DIM]
            v = kv[:, KV_W + kh * HEAD_DIM:KV_W + (kh + 1) * HEAD_DIM]
            dk = jnp.zeros((t + HALO, HEAD_DIM), F32)
            dv = jnp.zeros((t + HALO, HEAD_DIM), F32)
            for h in range(kh * Q_PER_KV, (kh + 1) * Q_PER_KV):
                qh = q_ref[:, h * HEAD_DIM:(h + 1) * HEAD_DIM]
                doh = do_ref[:, h * HEAD_DIM:(h + 1) * HEAD_DIM]
                p, ps = _att_probs(qh, k, dist, allowed, _SLOPES[h], sink_ref[h])
                dp = _dot(doh, v, _NT)
                delta = jnp.sum(p * dp, axis=1, keepdims=True)
                ds = p * (dp - delta) * (1.0 / math.sqrt(HEAD_DIM))
                dsink_ref[h:h + 1, :] += jnp.broadcast_to(-jnp.sum(ps * delta, axis=0, keepdims=True), (1, LANES))
                dq_ref[:, h * HEAD_DIM:(h + 1) * HEAD_DIM] = _dot(ds, k).astype(BF16)
                dk = dk + _dot(ds, qh, _TN)
                dv = dv + _dot(p, doh, _TN)
            dkv_ref[rows, kh * HEAD_DIM:(kh + 1) * HEAD_DIM] += dk
            dkv_ref[rows, KV_W + kh * HEAD_DIM:KV_W + (kh + 1) * HEAD_DIM] += dv

    q_spec, kv_spec, halo_spec = _att_specs(t)
    blk = pl.BlockSpec((t, ATT_W), lambda i: (i, 0))
    body, x_in, x_out, x_shapes, x_scratch, x_args = _carry_sends(body, 5, 3, 0, (l // t,), sends)
    outs = pl.pallas_call(
        body, name="attention_bwd" if sends is None else "attention_bwd_sending", grid=(l // t,),
        in_specs=[pl.BlockSpec(memory_space=pltpu.SMEM), q_spec, kv_spec, halo_spec, blk] + x_in,
        out_specs=[blk, _full((HALO + l, 2 * KV_W)), _full((N_HEADS, LANES))] + x_out,
        out_shape=[jax.ShapeDtypeStruct((l, ATT_W), BF16), jax.ShapeDtypeStruct((HALO + l, 2 * KV_W), F32),
                   jax.ShapeDtypeStruct((N_HEADS, LANES), F32)] + x_shapes,
        scratch_shapes=x_scratch, compiler_params=_params(("arbitrary",)))(sinks, proj, proj, proj, do, *x_args)
    return outs[0], outs[1], outs[2], list(outs[3:])


ROWS_SSM = 2048
SSM_BLOCKS = 2
SCAN_UNROLL = 2
SSM_STEPS_PER_U = LANES // (SSM_BLOCKS * 2 * SSM_GROUP)


def _ssm_discretize(a_re, a_im, log_dt, b_re, b_im):
    lam = lax.complex(a_re, a_im)
    dt = jnp.exp(log_dt)[:, None]
    lam_bar = jnp.exp(lam * dt)
    b_bar = ((lam_bar - 1.0) / lam)[..., None] * lax.complex(b_re, b_im)
    return lam, dt, lam_bar, b_bar


def _ssm_block_diag(m):
    e = m.reshape(N_SBLK, 2, SSM_GROUP, SSM_STATE)
    e = e[:, :, :, None, :] * jnp.eye(2, dtype=m.dtype)[None, :, None, :, None]
    e = e.reshape(N_SBLK, 2 * SSM_GROUP, LANES)
    oh = jax.nn.one_hot(jnp.arange(N_SBLK) % 4, 4, dtype=m.dtype)
    return (oh[:, :, None, None] * e[:, None]).reshape(N_SBLK, LANES, LANES)


def _ssm_tables(a_re, a_im, log_dt, b_re, b_im, c_re, c_im):
    _, _, lam_bar, b_bar = _ssm_discretize(a_re, a_im, log_dt, b_re, b_im)
    lam_blk = jnp.stack([jnp.real(lam_bar).reshape(N_SBLK, LANES), jnp.imag(lam_bar).reshape(N_SBLK, LANES)], axis=1)
    bt = jnp.transpose(b_bar, (0, 2, 1))
    bb = jnp.concatenate([_ssm_block_diag(jnp.real(bt)), _ssm_block_diag(jnp.imag(bt))], axis=2)
    cb = jnp.concatenate([jnp.transpose(_ssm_block_diag(c_re), (0, 2, 1)),
                          jnp.transpose(_ssm_block_diag(-c_im), (0, 2, 1))], axis=1)
    return lam_blk, bb, cb


def _ssm_powers(a_re, a_im, log_dt):
    lam = lax.complex(a_re, a_im)
    dt = jnp.exp(log_dt)[:, None]
    k = jnp.arange(1, SUBLANES + 1, dtype=F32)
    pw = jnp.exp((lam * dt)[None] * k[:, None, None]).reshape(SUBLANES, N_SBLK, LANES)
    pw = jnp.transpose(pw, (1, 0, 2))
    rev = pw[:, ::-1]
    return jnp.concatenate([jnp.real(pw), jnp.imag(pw), jnp.real(rev), jnp.imag(rev)], axis=1)


def _scan_consts(pw_ref, b, reverse):
    row = lax.broadcasted_iota(jnp.int32, (SUBLANES, LANES), 0)
    sign = -1.0 if reverse else 1.0

    def power(k):
        return (jnp.broadcast_to(pw_ref[b, k - 1:k, :], (SUBLANES, LANES)),
                sign * jnp.broadcast_to(pw_ref[b, SUBLANES + k - 1:SUBLANES + k, :], (SUBLANES, LANES)))

    steps = []
    for d in (1, 2, 4):
        pr, pi = power(d)
        keep = (row < SUBLANES - d) if reverse else (row >= d)
        steps.append((d, jnp.where(keep, pr, 0.0), jnp.where(keep, pi, 0.0)))
    base = 2 * SUBLANES if reverse else 0
    return steps, pw_ref[b, base:base + SUBLANES, :], sign * pw_ref[b, base + SUBLANES:base + 2 * SUBLANES, :]


def _scan_tile(xr, xi, steps, reverse):
    for d, ar, ai in steps:
        shift = SUBLANES - d if reverse else d
        rr, ri = pltpu.roll(xr, shift, 0), pltpu.roll(xi, shift, 0)
        xr, xi = xr + (ar * rr - ai * ri), xi + (ar * ri + ai * rr)
    return xr, xi


def _bcast_row(v, r):
    return jnp.broadcast_to(v[r:r + 1, :], (SUBLANES, LANES))


def _re_im(b):
    return slice(2 * b * LANES, (2 * b + 1) * LANES), slice((2 * b + 1) * LANES, (2 * b + 2) * LANES)


def _scan_forward(s_scr, pw_ref, n_tiles):
    consts = [_scan_consts(pw_ref, b, False) for b in range(SSM_BLOCKS)]

    def tiles(g, carry):
        carry = list(carry)
        rows = [pl.ds(pl.multiple_of((g * SCAN_UNROLL + u) * SUBLANES, SUBLANES), SUBLANES) for u in range(SCAN_UNROLL)]
        loaded = [[(s_scr[r, _re_im(b)[0]], s_scr[r, _re_im(b)[1]]) for b in range(SSM_BLOCKS)] for r in rows]
        local = [[_scan_tile(xr, xi, consts[b][0], False) for b, (xr, xi) in enumerate(per_row)] for per_row in loaded]
        for r, per_row in zip(rows, local):
            for b, (xr, xi) in enumerate(per_row):
                _, pr, pi = consts[b]
                cr, ci = carry[b]
                sr, si = xr + (pr * cr - pi * ci), xi + (pr * ci + pi * cr)
                s_scr[r, _re_im(b)[0]] = sr
                s_scr[r, _re_im(b)[1]] = si
                carry[b] = (_bcast_row(sr, SUBLANES - 1), _bcast_row(si, SUBLANES - 1))
        return tuple(carry)

    zero = jnp.zeros((SUBLANES, LANES), F32)
    lax.fori_loop(0, n_tiles // SCAN_UNROLL, tiles, ((zero, zero),) * SSM_BLOCKS)


def _row_chunks(l):
    rc = min(ROWS_SSM, l)
    return rc, l // rc


def _carry_sends(body, n_in, n_out, n_scratch, grid, sends):
    if sends is None:
        return body, [], [], [], [], []
    k, k_out = len(sends.srcs), len(sends.out_shapes)

    def carrying(*refs):
        ins, send_ins = refs[:n_in], refs[n_in:n_in + k]
        outs, send_outs = refs[n_in + k:n_in + k + n_out], refs[n_in + k + n_out:n_in + k + n_out + k_out]
        rest = refs[n_in + k + n_out + k_out:]
        scratch, sems = rest[:n_scratch], rest[n_scratch:]
        ids = [pl.program_id(axis) for axis in range(len(grid))]
        first = functools.reduce(jnp.logical_and, [i == 0 for i in ids])
        last = functools.reduce(jnp.logical_and, [i == n - 1 for i, n in zip(ids, grid)])

        @pl.when(first)
        def _():
            sends.start(send_ins, send_outs, *sems)

        body(*ins, *outs, *scratch)

        @pl.when(last)
        def _():
            sends.finish(send_ins, send_outs, *sems)

    in_specs, out_specs, scratch = sends.specs()
    return carrying, in_specs, out_specs, sends.out_shapes, scratch, sends.srcs


def _ssm_fwd(proj, bb, cb, pw, sends=None):
    l = proj.shape[0]
    rc, n_chunks = _row_chunks(l)

    def body(u_ref, bb_ref, cb_ref, pw_ref, y_ref, s_scr):
        j = pl.program_id(0)

        def fill(ci, _):
            rows = pl.ds(pl.multiple_of(ci * rc, rc), rc)
            uv = u_ref[rows, :]
            for b in range(SSM_BLOCKS):
                s_scr[rows, 2 * b * LANES:2 * (b + 1) * LANES] = _dot(uv, bb_ref[b])
            return 0

        lax.fori_loop(0, n_chunks, fill, 0)
        _scan_forward(s_scr, pw_ref, l // SUBLANES)

        @pl.when(j % SSM_STEPS_PER_U == 0)
        def _():
            y_ref[...] = jnp.zeros_like(y_ref)

        def emit(ci, _):
            rows = pl.ds(pl.multiple_of(ci * rc, rc), rc)
            for b in range(SSM_BLOCKS):
                y_ref[rows, :] += _dot(s_scr[rows, 2 * b * LANES:2 * (b + 1) * LANES], cb_ref[b])
            return 0

        lax.fori_loop(0, n_chunks, emit, 0)

    steps = N_SBLK // SSM_BLOCKS
    body, x_in, x_out, x_shapes, x_scratch, x_args = _carry_sends(body, 4, 2, 0, (steps,), sends)
    outs = pl.pallas_call(
        body, name="ssm_fwd" if sends is None else "ssm_fwd_sending", grid=(steps,),
        in_specs=[pl.BlockSpec((l, LANES), lambda j: (0, OFF_US // LANES + j // SSM_STEPS_PER_U)),
                  pl.BlockSpec((SSM_BLOCKS, LANES, 2 * LANES), lambda j: (j, 0, 0)),
                  pl.BlockSpec((SSM_BLOCKS, 2 * LANES, LANES), lambda j: (j, 0, 0)),
                  pl.BlockSpec((SSM_BLOCKS, 4 * SUBLANES, LANES), lambda j: (j, 0, 0))] + x_in,
        out_specs=[pl.BlockSpec((l, LANES), lambda j: (0, j // SSM_STEPS_PER_U)),
                   pl.BlockSpec((l, SSM_BLOCKS * 2 * LANES), lambda j: (0, j))] + x_out,
        out_shape=[jax.ShapeDtypeStruct((l, SSM_W), F32), jax.ShapeDtypeStruct((l, N_SBLK * 2 * LANES), F32)] + x_shapes,
        scratch_shapes=x_scratch, compiler_params=_params(("arbitrary",)))(proj, bb, cb, pw, *x_args)
    return outs[0], outs[1], list(outs[2:])


def _ssm_bwd(proj, states, dy, bb, cb, pw, sends=None):
    l = proj.shape[0]
    rc, n_chunks = _row_chunks(l)
    n_tiles = l // SUBLANES

    def body(u_ref, s_scr, dy_ref, bb_ref, cb_ref, pw_ref, du_ref, dbb_ref, dcb_ref, dlam_ref, a_scr):
        j = pl.program_id(0)
        dcb_ref[...] = jnp.zeros_like(dcb_ref)

        def through_c(ci, _):
            rows = pl.ds(pl.multiple_of(ci * rc, rc), rc)
            dyv = dy_ref[rows, :]
            for b in range(SSM_BLOCKS):
                lanes = slice(2 * b * LANES, 2 * (b + 1) * LANES)
                dcb_ref[b] += _dot(s_scr[rows, lanes], dyv, _TN)
                a_scr[rows, lanes] = _dot(dyv, cb_ref[b], _NT)
            return 0

        lax.fori_loop(0, n_chunks, through_c, 0)

        consts = [_scan_consts(pw_ref, b, True) for b in range(SSM_BLOCKS)]
        row = lax.broadcasted_iota(jnp.int32, (SUBLANES, LANES), 0)
        last = row == SUBLANES - 1

        def tiles(g, carry):
            carry = list(carry)
            rows = [pl.ds(pl.multiple_of((n_tiles - 1 - (g * SCAN_UNROLL + u)) * SUBLANES, SUBLANES), SUBLANES)
                    for u in range(SCAN_UNROLL)]
            loaded = [[(a_scr[r, _re_im(b)[0]], a_scr[r, _re_im(b)[1]]) for b in range(SSM_BLOCKS)] for r in rows]
            states = [[(s_scr[r, _re_im(b)[0]], s_scr[r, _re_im(b)[1]]) for b in range(SSM_BLOCKS)] for r in rows]
            local = [[_scan_tile(xr, xi, consts[b][0], True) for b, (xr, xi) in enumerate(per_row)] for per_row in loaded]
            for r, per_row, state_row in zip(rows, local, states):
                for b, ((xr, xi), (sr, si)) in enumerate(zip(per_row, state_row)):
                    _, pr, pi = consts[b]
                    cr, ci, accr, acci = carry[b]
                    gr, gi = xr + (pr * cr - pi * ci), xi + (pr * ci + pi * cr)
                    a_scr[r, _re_im(b)[0]] = gr
                    a_scr[r, _re_im(b)[1]] = gi
                    ur = jnp.where(last, cr, pltpu.roll(gr, SUBLANES - 1, 0))
                    ui = jnp.where(last, ci, pltpu.roll(gi, SUBLANES - 1, 0))
                    carry[b] = (_bcast_row(gr, 0), _bcast_row(gi, 0),
                                accr + (ur * sr + ui * si), acci + (ui * sr - ur * si))
            return tuple(carry)

        zero = jnp.zeros((SUBLANES, LANES), F32)
        done = lax.fori_loop(0, n_tiles // SCAN_UNROLL, tiles, ((zero, zero, zero, zero),) * SSM_BLOCKS)
        for b in range(SSM_BLOCKS):
            dlr = jnp.broadcast_to(jnp.sum(done[b][2], axis=0, keepdims=True), (SUBLANES, LANES))
            dli = jnp.broadcast_to(jnp.sum(done[b][3], axis=0, keepdims=True), (SUBLANES, LANES))
            dlam_ref[b] = jnp.where(row == 0, dlr, jnp.where(row == 1, dli, 0.0))

        dbb_ref[...] = jnp.zeros_like(dbb_ref)

        @pl.when(j % SSM_STEPS_PER_U == 0)
        def _():
            du_ref[...] = jnp.zeros_like(du_ref)

        def through_b(ci, _):
            rows = pl.ds(pl.multiple_of(ci * rc, rc), rc)
            uv = u_ref[rows, :]
            for b in range(SSM_BLOCKS):
                av = a_scr[rows, 2 * b * LANES:2 * (b + 1) * LANES]
                dbb_ref[b] += _dot(uv, av, _TN)
                du_ref[rows, :] += _dot(av, bb_ref[b], _NT)
            return 0

        lax.fori_loop(0, n_chunks, through_b, 0)

    steps = N_SBLK // SSM_BLOCKS
    body, x_in, x_out, x_shapes, x_scratch, x_args = _carry_sends(body, 6, 4, 1, (steps,), sends)
    outs = pl.pallas_call(
        body, name="ssm_bwd" if sends is None else "ssm_bwd_sending", grid=(steps,),
        in_specs=[pl.BlockSpec((l, LANES), lambda j: (0, OFF_US // LANES + j // SSM_STEPS_PER_U)),
                  pl.BlockSpec((l, SSM_BLOCKS * 2 * LANES), lambda j: (0, j)),
                  pl.BlockSpec((l, LANES), lambda j: (0, j // SSM_STEPS_PER_U)),
                  pl.BlockSpec((SSM_BLOCKS, LANES, 2 * LANES), lambda j: (j, 0, 0)),
                  pl.BlockSpec((SSM_BLOCKS, 2 * LANES, LANES), lambda j: (j, 0, 0)),
                  pl.BlockSpec((SSM_BLOCKS, 4 * SUBLANES, LANES), lambda j: (j, 0, 0))] + x_in,
        out_specs=[pl.BlockSpec((l, LANES), lambda j: (0, j // SSM_STEPS_PER_U)),
                   pl.BlockSpec((SSM_BLOCKS, LANES, 2 * LANES), lambda j: (j, 0, 0)),
                   pl.BlockSpec((SSM_BLOCKS, 2 * LANES, LANES), lambda j: (j, 0, 0)),
                   pl.BlockSpec((SSM_BLOCKS, SUBLANES, LANES), lambda j: (j, 0, 0))] + x_out,
        out_shape=[jax.ShapeDtypeStruct((l, SSM_W), F32), jax.ShapeDtypeStruct((N_SBLK, LANES, 2 * LANES), F32),
                   jax.ShapeDtypeStruct((N_SBLK, 2 * LANES, LANES), F32),
                   jax.ShapeDtypeStruct((N_SBLK, SUBLANES, LANES), F32)] + x_shapes,
        scratch_shapes=[pltpu.VMEM((l, SSM_BLOCKS * 2 * LANES), F32)] + x_scratch,
        compiler_params=_params(("arbitrary",)))(proj, states, dy, bb, cb, pw, *x_args)
    return outs[0], outs[1], outs[2], outs[3], list(outs[4:])


def _pool_windows(v, t, l, ahead):
    def shifted(a, d):
        if ahead:
            return jnp.where(t < l - d, pltpu.roll(a, l - d, 0), 0.0)
        return jnp.where(t >= d, pltpu.roll(a, d, 0), 0.0)

    w2 = v + shifted(v, 1)
    w4 = w2 + shifted(w2, 2)
    w8 = w4 + shifted(w4, 4)
    w16 = w8 + shifted(w8, 8)
    return w2, w4, w8, w16


def _pool_select(g, ws):
    return jnp.where(g == 0, ws[0], jnp.where(g == 1, ws[1], jnp.where(g == 2, ws[2], ws[3])))


def _pool_count(g, t):
    return jnp.minimum(t + 1, jnp.left_shift(2, g)).astype(F32)


def _pool_specs(l):
    return [pl.BlockSpec((l, POOL_GW), lambda g: (0, OFF_UP // POOL_GW + g)),
            pl.BlockSpec((1, POOL_GW, POOL_GW), lambda g: (g, 0, 0)),
            pl.BlockSpec((1, POOL_GW), lambda g: (0, g))]


def _pool_fwd(proj, w_pool, scale):
    l = proj.shape[0]

    def body(u_ref, w_ref, sc_ref, y_ref):
        g = pl.program_id(0)
        t = lax.broadcasted_iota(jnp.int32, (l, 1), 0)
        u = u_ref[...]
        pooled = _pool_select(g, _pool_windows(u, t, l, False)) / _pool_count(g, t) - u
        y_ref[...] = _dot(pooled, w_ref[0]) * sc_ref[...]

    return pl.pallas_call(
        body, name="pool_fwd", grid=(4,), in_specs=_pool_specs(l),
        out_specs=pl.BlockSpec((l, POOL_GW), lambda g: (0, g)), out_shape=jax.ShapeDtypeStruct((l, POOL_W), F32),
        compiler_params=_params(("parallel",)))(proj, w_pool, scale)


def _pool_bwd(proj, w_pool, scale, dy):
    l = proj.shape[0]

    def body(u_ref, w_ref, sc_ref, dy_ref, du_ref, dw_ref, dsc_ref):
        g = pl.program_id(0)
        t = lax.broadcasted_iota(jnp.int32, (l, 1), 0)
        u = u_ref[...]
        cnt = _pool_count(g, t)
        pooled = _pool_select(g, _pool_windows(u, t, l, False)) / cnt - u
        dyv = dy_ref[...]
        dsc_ref[...] = jnp.sum(dyv * _dot(pooled, w_ref[0]), axis=0, keepdims=True)
        dyl = dyv * sc_ref[...]
        dw_ref[0] = _dot(pooled, dyl, _TN)
        dpooled = _dot(dyl, w_ref[0], _NT)
        du_ref[...] = (_pool_select(g, _pool_windows(dpooled / cnt, t, l, True)) - dpooled).astype(BF16)

    return pl.pallas_call(
        body, name="pool_bwd", grid=(4,), in_specs=_pool_specs(l) + [pl.BlockSpec((l, POOL_GW), lambda g: (0, g))],
        out_specs=[pl.BlockSpec((l, POOL_GW), lambda g: (0, g)), pl.BlockSpec((1, POOL_GW, POOL_GW), lambda g: (g, 0, 0)),
                   pl.BlockSpec((1, POOL_GW), lambda g: (0, g))],
        out_shape=[jax.ShapeDtypeStruct((l, POOL_W), BF16), jax.ShapeDtypeStruct((4, POOL_GW, POOL_GW), F32),
                   jax.ShapeDtypeStruct((1, POOL_W), F32)],
        compiler_params=_params(("parallel",)))(proj, w_pool, scale, dy)


ROWS_MERGE = 256
VMEM_LIMIT_MERGE_BWD = 60 * 1024 * 1024
ROWS_MERGE_FWD = 256


def _merge_inputs(tr):
    def col(off, w):
        return pl.BlockSpec((tr, w), lambda i: (i, off // w))

    def act(w):
        return pl.BlockSpec((tr, w), lambda i: (i, 0))

    d = D_MODEL
    return ([col(OFF_GL, 3 * d), col(OFF_ZA, ATT_W), col(OFF_US, SSM_W), col(OFF_ZS, SSM_W), col(OFF_ZP, POOL_W),
             act(ATT_W), act(SSM_W), act(POOL_W)]
            + [_full((1, SSM_W)), _full((SSM_W, SSM_W)), _full((1, SSM_W)), _full((ATT_W, d)), _full((SSM_W, d)),
               _full((POOL_W, d)), _full((d, d)), _full((1, 3 * d))])


def _branch_math(za_ref, us_ref, zs_ref, zp_ref, o_ref, yl_ref, yp_ref, d_ref, wg_ref, bg_ref):
    r = {}
    r['sa'], r['dsa'] = _silu_and_grad(za_ref[...])
    r['ss'], r['dss'] = _silu_and_grad(zs_ref[...])
    r['sp'], r['dsp'] = _silu_and_grad(zp_ref[...])
    r['y_att'] = o_ref[...] * r['sa']
    y1 = yl_ref[...] + d_ref[...] * us_ref[...]
    r['y2'], r['dgelu'] = _gelu_and_grad(y1)
    r['sg'] = _sigmoid(_dot(r['y2'], wg_ref[...]) + bg_ref[...])
    r['y3'] = r['y2'] * r['sg']
    r['y_ssm'] = r['y3'] * r['ss']
    r['y_pool'] = yp_ref[...] * r['sp']
    return r


def _merge_forward_math(gl_ref, za_ref, us_ref, zs_ref, zp_ref, o_ref, yl_ref, yp_ref, d_ref, wg_ref, bg_ref,
                        wba_ref, wbs_ref, wbp_ref):
    d = D_MODEL
    r = _branch_math(za_ref, us_ref, zs_ref, zp_ref, o_ref, yl_ref, yp_ref, d_ref, wg_ref, bg_ref)
    r['g'] =[_sigmoid(gl_ref[:, b * d:(b + 1) * d]) for b in range(3)]
    r['p'] = [_dot(r['y_att'], wba_ref[...]), _dot(r['y_ssm'], wbs_ref[...]), _dot(r['y_pool'], wbp_ref[...])]
    r['merged'] = r['g'][0] * r['p'][0] + r['g'][1] * r['p'][1] + r['g'][2] * r['p'][2]
    return r


def _merge_fwd(proj, o, y_lin, y_pool, x, ssm_d, w_glu, b_glu, wba, wbs, wbp, w_out, mod, sends=None):
    l, d = x.shape
    tr = min(ROWS_MERGE_FWD, l)

    def body(gl_ref, za_ref, us_ref, zs_ref, zp_ref, o_ref, yl_ref, yp_ref, d_ref, wg_ref, bg_ref, wba_ref, wbs_ref,
             wbp_ref, wo_ref, mod_ref, x_ref, xn_ref):
        r = _merge_forward_math(gl_ref, za_ref, us_ref, zs_ref, zp_ref, o_ref, yl_ref, yp_ref, d_ref, wg_ref, bg_ref,
                                wba_ref, wbs_ref, wbp_ref)
        xn_ref[...] = x_ref[...] + mod_ref[:, 2 * d:3 * d] * _dot(r['merged'], wo_ref[...])

    blk = pl.BlockSpec((tr, d), lambda i: (i, 0))
    in_specs = _merge_inputs(tr) + [blk]
    body, x_in, x_out, x_shapes, x_scratch, x_args = _carry_sends(body, len(in_specs), 1, 0, (l // tr,), sends)
    outs = pl.pallas_call(
        body, name="merge_fwd" if sends is None else "merge_fwd_sending", grid=(l // tr,), in_specs=in_specs + x_in,
        out_specs=[blk] + x_out, out_shape=[jax.ShapeDtypeStruct((l, d), F32)] + x_shapes, scratch_shapes=x_scratch,
        compiler_params=_params(("arbitrary",)))(
            proj, proj, proj, proj, proj, o, y_lin, y_pool, ssm_d, w_glu, b_glu, wba, wbs, wbp, w_out, mod, x, *x_args)
    return outs[0], list(outs[1:])


def _merge_bwd(proj, o, y_lin, y_pool, dxo, ssm_d, w_glu, b_glu, wba, wbs, wbp, w_out, mod, sends=None):
    l, d = dxo.shape
    tr = min(ROWS_MERGE, l)

    def body(gl_ref, za_ref, us_ref, zs_ref, zp_ref, o_ref, yl_ref, yp_ref, d_ref, wg_ref, bg_ref, wba_ref, wbs_ref,
             wbp_ref, wo_ref, mod_ref, dxo_ref,
             do_ref, dyl_ref, dus_ref, dyp_ref, dpj_ref,
             ya_ref, ys_ref, ypl_ref, y2_ref, dt_ref, dpa_ref, dps_ref, dpp_ref, mg_ref, dout_ref, st_ref):
        @pl.when(pl.program_id(0) == 0)
        def _():
            st_ref[...] = jnp.zeros_like(st_ref)

        for off, width in ((OFF_Q, ATT_W), (OFF_US, SSM_W), (OFF_UP, POOL_W), (OFF_KV, 2 * KV_W)):
            dpj_ref[:, off:off + width] = jnp.zeros((tr, width), BF16)

        r = _branch_math(za_ref, us_ref, zs_ref, zp_ref, o_ref, yl_ref, yp_ref, d_ref, wg_ref, bg_ref)
        dxov = dxo_ref[...]
        dout = dxov * mod_ref[:, 2 * d:3 * d]
        dout_ref[...] = dout.astype(BF16)
        branches = ((r['y_att'], wba_ref, dpa_ref), (r['y_ssm'], wbs_ref, dps_ref), (r['y_pool'], wbp_ref, dpp_ref))
        out = jnp.zeros((tr, d), F32)
        dys = [jnp.zeros((tr, ATT_W), F32) for _ in branches]
        for cols in (slice(0, d // 2), slice(d // 2, d)):
            gates = [_sigmoid(gl_ref[:, b * d + cols.start:b * d + cols.stop]) for b in range(3)]
            projected = [_dot(y, w_ref[:, cols]) for y, w_ref, _ in branches]
            merged = gates[0] * projected[0] + gates[1] * projected[1] + gates[2] * projected[2]
            mg_ref[:, cols] = merged.astype(BF16)
            out = out + _dot(merged, wo_ref[cols, :])
            dmerged = _dot(dout, wo_ref[cols, :], _NT)
            for b, (_, w_ref, dp_ref) in enumerate(branches):
                dp = dmerged * gates[b]
                dpj_ref[:, OFF_GL + b * d + cols.start:OFF_GL + b * d + cols.stop] = (
                    dp * projected[b] * (1.0 - gates[b])).astype(BF16)
                dp_ref[:, cols] = dp.astype(BF16)
                dys[b] = dys[b] + _dot(dp, w_ref[:, cols], _NT)
        st_ref[0:1, :] += jnp.sum(dxov * out, axis=0, keepdims=True)
        ya_ref[...] = r['y_att'].astype(BF16)
        ys_ref[...] = r['y_ssm'].astype(BF16)
        ypl_ref[...] = r['y_pool'].astype(BF16)
        y2_ref[...] = r['y2'].astype(BF16)
        do_ref[...] = (dys[0] * r['sa']).astype(BF16)
        dpj_ref[:, OFF_ZA:OFF_ZA + ATT_W] = (dys[0] * o_ref[...] * r['dsa']).astype(BF16)
        dy3 = dys[1] * r['ss']
        dpj_ref[:, OFF_ZS:OFF_ZS + SSM_W] = (dys[1] * r['y3'] * r['dss']).astype(BF16)
        dt = dy3 * r['y3'] * (1.0 - r['sg'])
        dt_ref[...] = dt.astype(BF16)
        dy1 = (dy3 * r['sg'] + _dot(dt, wg_ref[...], _NT)) * r['dgelu']
        dyl_ref[...] = dy1.astype(BF16)
        dus_ref[...] = dy1 * d_ref[...]
        st_ref[1:2, 0:SSM_W] += jnp.sum(dt, axis=0, keepdims=True)
        st_ref[1:2, SSM_W:2 * SSM_W] += jnp.sum(dy1 * us_ref[...], axis=0, keepdims=True)
        dyp_ref[...] = dys[2] * r['sp']
        dpj_ref[:, OFF_ZP:OFF_ZP + POOL_W] = (dys[2] * yp_ref[...] * r['dsp']).astype(BF16)

    blk = pl.BlockSpec((tr, d), lambda i: (i, 0))
    half = pl.BlockSpec((tr, ATT_W), lambda i: (i, 0))
    wide = pl.BlockSpec((tr, IN_W), lambda i: (i, 0))
    sds = jax.ShapeDtypeStruct
    in_specs = _merge_inputs(tr) + [blk]
    out_specs = [half] * 4 + [wide] + [half] * 5 + [blk] * 5 + [_full((SUBLANES, d))]
    body, x_in, x_out, x_shapes, x_scratch, x_args = _carry_sends(body, len(in_specs), len(out_specs), 0, (l // tr,), sends)
    outs = pl.pallas_call(
        body, name="merge_bwd" if sends is None else "merge_bwd_sending", grid=(l // tr,), in_specs=in_specs + x_in,
        out_specs=out_specs + x_out,
        out_shape=[sds((l, ATT_W), t) for t in (BF16, BF16, F32, F32)] + [sds((l, IN_W), BF16)]
        + [sds((l, ATT_W), BF16)] * 5 + [sds((l, d), BF16)] * 5 + [sds((SUBLANES, d), F32)] + x_shapes,
        scratch_shapes=x_scratch, compiler_params=_params(("arbitrary",), VMEM_LIMIT_MERGE_BWD))(
            proj, proj, proj, proj, proj, o, y_lin, y_pool, ssm_d, w_glu, b_glu, wba, wbs, wbp, w_out, mod, dxo, *x_args)
    return list(outs[:len(out_specs)]), list(outs[len(out_specs):])


def _ada_fwd(c_all, w_ada, b_shard):
    depth, d, n = w_ada.shape

    def body(c_ref, w_ref, b_ref, o_ref):
        act, _ = _silu_and_grad(c_ref[...])
        o_ref[0] = _dot(act, w_ref[0]) + b_ref[0]

    return pl.pallas_call(
        body, name="ada_fwd", grid=(depth,),
        in_specs=[_full((N_DEV, d)), pl.BlockSpec((1, d, n), lambda i: (i, 0, 0)), pl.BlockSpec((1, 1, n), lambda i: (i, 0, 0))],
        out_specs=pl.BlockSpec((1, N_DEV, n), lambda i: (i, 0, 0)), out_shape=jax.ShapeDtypeStruct((depth, N_DEV, n), F32),
        compiler_params=_params(("parallel",)))(c_all, w_ada, b_shard)


def _ada_bwd(c_all_t, dmod):
    d = c_all_t.shape[0]
    depth, _, n = dmod.shape

    def body(c_ref, dm_ref, o_ref):
        act, _ = _silu_and_grad(c_ref[...])
        acc = act[:, 0:1] * dm_ref[0, 0:1, :]
        for b in range(1, N_DEV):
            acc = acc + act[:, b:b + 1] * dm_ref[0, b:b + 1, :]
        o_ref[0] = acc

    return pl.pallas_call(
        body, name="ada_bwd", grid=(depth,),
        in_specs=[_full((d, N_DEV)), pl.BlockSpec((1, N_DEV, n), lambda i: (i, 0, 0))],
        out_specs=pl.BlockSpec((1, d, n), lambda i: (i, 0, 0)), out_shape=jax.ShapeDtypeStruct((depth, d, n), F32),
        compiler_params=_params(("parallel",)))(c_all_t, dmod)


ROWS_FLAT = 256


_RELATION_XOR = (0, 2, 1, 3)


def _sum_chips(own, others, chip_order, name):
    r, c = own.shape
    tr = math.gcd(ROWS_FLAT, r)

    def body(own_ref, oth_ref, o_ref):
        terms = [own_ref[...].astype(F32)] + [oth_ref[k].astype(F32) for k in range(3)]
        if chip_order:
            chip = 2 * lax.axis_index("x") + lax.axis_index("y")
            by_chip = []
            for q in range(N_CHIPS):
                rel = jnp.bitwise_xor(chip, q)
                pick = terms[3]
                for k in (2, 1, 0):
                    pick = jnp.where(rel == _RELATION_XOR[k], terms[k], pick)
                by_chip.append(pick)
            terms = by_chip
        o_ref[...] = ((terms[0] + terms[1]) + terms[2]) + terms[3]

    return pl.pallas_call(
        body, name=name, grid=(r // tr,),
        in_specs=[pl.BlockSpec((tr, c), lambda i: (i, 0)), pl.BlockSpec((3, tr, c), lambda i: (0, i, 0))],
        out_specs=pl.BlockSpec((tr, c), lambda i: (i, 0)), out_shape=jax.ShapeDtypeStruct((r, c), F32),
        compiler_params=_params(("parallel",)))(own, others)


def _add2(a, b, out_dtype, name):
    shape = a.shape
    a, b = a.reshape(-1, shape[-1]), b.reshape(-1, shape[-1])
    r, c = a.shape
    tr = math.gcd(ROWS_FLAT, r)

    def body(a_ref, b_ref, o_ref):
        o_ref[...] = (a_ref[...] + b_ref[...]).astype(out_dtype)

    blk = pl.BlockSpec((tr, c), lambda i: (i, 0))
    return pl.pallas_call(
        body, name=name, grid=(r // tr,), in_specs=[blk, blk], out_specs=blk,
        out_shape=jax.ShapeDtypeStruct((r, c), out_dtype), compiler_params=_params(("parallel",)))(a, b).reshape(shape)


def _adamw(w, g, m, v, name):
    r, c = w.shape
    tr = math.gcd(ROWS_FLAT, r)

    def body(w_ref, g_ref, m_ref, v_ref, d_ref, nm_ref, nv_ref):
        gv = g_ref[...]
        mv = ADAM_B1 * m_ref[...] + (1.0 - ADAM_B1) * gv
        vv = ADAM_B2 * v_ref[...] + (1.0 - ADAM_B2) * (gv * gv)
        m_hat = mv / (1.0 - ADAM_B1 ** ADAM_STEP)
        v_hat = vv / (1.0 - ADAM_B2 ** ADAM_STEP)
        d_ref[...] = -ADAM_LR * (m_hat / (jnp.sqrt(v_hat) + ADAM_EPS) + ADAM_WD * w_ref[...])
        nm_ref[...] = mv
        nv_ref[...] = vv

    blk = pl.BlockSpec((tr, c), lambda i: (i, 0))
    return pl.pallas_call(
        body, name=name, grid=(r // tr,), in_specs=[blk] * 4, out_specs=[blk] * 3,
        out_shape=[jax.ShapeDtypeStruct((r, c), F32)] * 3, compiler_params=_params(("parallel",)))(w, g, m, v)


_GROUP_MASKS = {
    'xy': ((1, 0, 0), (0, 1, 0), (1, 1, 0)),
    'xyc': ((0, 0, 1), (0, 1, 0), (0, 1, 1), (1, 0, 0), (1, 0, 1), (1, 1, 0), (1, 1, 1)),
}


def _group_rank(group, pos):
    x, y, c = pos
    return {'xy': 2 * x + y, 'xyc': 4 * x + 2 * y + c}[group]


def _gather(src, group, name):
    masks = _GROUP_MASKS[group]
    n = len(masks) + 1

    def body(src_ref, out_ref, send_sems, recv_sems, local_sem):
        me = (lax.axis_index("x"), lax.axis_index("y"), lax.axis_index("c"))
        my_rank = _group_rank(group, me)
        local = pltpu.make_async_copy(src_ref, out_ref.at[my_rank], local_sem)
        local.start()
        sends = []
        for k, mask in enumerate(masks):
            peer = tuple(1 - p if f else p for p, f in zip(me, mask))
            send = _remote(src_ref, out_ref.at[my_rank], send_sems.at[k], recv_sems.at[k], peer)
            send.start()
            sends.append((send, peer))
        for k, (send, peer) in enumerate(sends):
            _remote(src_ref, out_ref.at[_group_rank(group, peer)], send_sems.at[k], recv_sems.at[k], peer).wait_recv()
        for send, _ in sends:
            send.wait_send()
        local.wait()

    return pl.pallas_call(
        body, name=name, in_specs=[pl.BlockSpec(memory_space=pl.ANY)], out_specs=pl.BlockSpec(memory_space=pl.ANY),
        out_shape=jax.ShapeDtypeStruct((n,) + tuple(src.shape), src.dtype),
        scratch_shapes=[pltpu.SemaphoreType.DMA((n - 1,)), pltpu.SemaphoreType.DMA((n - 1,)), pltpu.SemaphoreType.DMA(())],
    )(src)


CHUNK_BYTES = 1 << 20
MIN_CHUNK_ROWS = 64


def _row_parts(rows, cols, itemsize):
    n = 1
    while rows % (2 * n) == 0 and rows // (2 * n) >= MIN_CHUNK_ROWS and rows * cols * itemsize > n * CHUNK_BYTES:
        n *= 2
    return n


def _remote(src, dst, send_sem, recv_sem, peer):
    return pltpu.make_async_remote_copy(src_ref=src, dst_ref=dst, send_sem=send_sem, recv_sem=recv_sem,
                                        device_id=peer, device_id_type=pl.DeviceIdType.MESH)


def _start_rows(src, dst, send_sem, recv_sem, peer):
    rows, cols = src.shape
    n = _row_parts(rows, cols, jnp.dtype(src.dtype).itemsize)
    pr = rows // n
    for i in range(n):
        _remote(src.at[pl.ds(i * pr, pr), :], dst.at[pl.ds(i * pr, pr), :], send_sem, recv_sem, peer).start()


def _mesh_place():
    x, y, c = lax.axis_index("x"), lax.axis_index("y"), lax.axis_index("c")
    other_chips = ((1 - x, y), (x, 1 - y), (1 - x, 1 - y))
    return x, y, c, 2 * x + y, (x, y, 1 - c), other_chips


def _comm_call(body, name, ins, out_shapes, sem_counts):
    any_spec = pl.BlockSpec(memory_space=pl.ANY)
    return pl.pallas_call(
        body, name=name, in_specs=[any_spec] * len(ins), out_specs=[any_spec] * len(out_shapes), out_shape=out_shapes,
        scratch_shapes=[pltpu.SemaphoreType.DMA((n,)) for n in sem_counts])(*ins)


class _Sends:
    sem_arrays = 2

    def run(self, name):
        k_in, k_out = len(self.srcs), len(self.out_shapes)

        def body(*refs):
            ins, outs, sems = refs[:k_in], refs[k_in:k_in + k_out], refs[k_in + k_out:]
            self.start(ins, outs, *sems)
            self.finish(ins, outs, *sems)

        return _comm_call(body, name, self.srcs, self.out_shapes, (self.n_sems,) * self.sem_arrays)

    def specs(self):
        any_spec = pl.BlockSpec(memory_space=pl.ANY)
        return ([any_spec] * len(self.srcs), [any_spec] * len(self.out_shapes),
                [pltpu.SemaphoreType.DMA((self.n_sems,))] * self.sem_arrays)


class _SiblingSends(_Sends):
    def __init__(self, srcs):
        self.srcs = list(srcs)
        self.n_sems = len(self.srcs)
        self.out_shapes = [jax.ShapeDtypeStruct(g.shape[1:], g.dtype) for g in self.srcs]

    def start(self, ins, outs, send_sems, recv_sems):
        _, _, c, _, sibling, _ = _mesh_place()
        for t in range(len(ins)):
            for q in range(ins[t].shape[1]):
                _start_rows(ins[t].at[1 - c, q], outs[t].at[q], send_sems.at[t], recv_sems.at[t], sibling)

    def finish(self, ins, outs, send_sems, recv_sems):
        _, _, _, _, sibling, _ = _mesh_place()
        for t in range(len(ins)):
            arrived = _remote(outs[t], outs[t], send_sems.at[t], recv_sems.at[t], sibling)
            arrived.wait_recv()
            arrived.wait_send()


class _ChipSends(_Sends):
    def __init__(self, srcs, pick, pass_on=False):
        self.srcs, self.pick, self.pass_on = list(srcs), pick, pass_on
        nt = len(self.srcs)
        self.n_sems = 3 * nt
        self.sem_arrays = 4 if pass_on else 2
        landing = []
        for s in self.srcs:
            r2 = s.shape[-2] // 2 if pick is _pick_weight_half else s.shape[-2]
            landing.append(jax.ShapeDtypeStruct((3, r2, s.shape[-1]), s.dtype))
        self.out_shapes = landing + (landing if pass_on else [])

    def start(self, ins, outs, send_sems, recv_sems, *unused):
        _, _, c, _, _, other_chips = _mesh_place()
        for t in range(len(ins)):
            for j, (px, py) in enumerate(other_chips):
                _remote(self.pick(ins[t], c, 2 * px + py), outs[t].at[j], send_sems.at[3 * t + j], recv_sems.at[3 * t + j],
                        (px, py, c)).start()

    def finish(self, ins, outs, send_sems, recv_sems, *sibling_sems):
        _, _, c, _, sibling, other_chips = _mesh_place()
        nt = len(ins)
        for t in range(nt):
            for j, (px, py) in enumerate(other_chips):
                k = 3 * t + j
                landed = _remote(outs[t].at[j], outs[t].at[j], send_sems.at[k], recv_sems.at[k], (px, py, c))
                landed.wait_recv()
                if self.pass_on:
                    _start_rows(outs[t].at[j], outs[nt + t].at[j], sibling_sems[0].at[k], sibling_sems[1].at[k], sibling)
                landed.wait_send()
        if self.pass_on:
            for t in range(nt):
                for j in range(3):
                    k = 3 * t + j
                    passed = _remote(outs[nt + t].at[j], outs[nt + t].at[j], sibling_sems[0].at[k], sibling_sems[1].at[k],
                                     sibling)
                    passed.wait_recv()
                    passed.wait_send()


def _pick_weight_half(src, core, chip):
    r2 = src.shape[0] // 2
    return src.at[pl.ds(pl.multiple_of(core * r2, MIN_CHUNK_ROWS), r2), :]


def _pick_slab(src, core, chip):
    return src.at[chip if src.shape[0] == N_CHIPS else 0]


def _swap_cores(halves):
    nt = len(halves)

    def body(*refs):
        ins, outs = refs[:nt], refs[nt:2 * nt]
        send_sems, recv_sems = refs[2 * nt:]
        _, _, _, _, sibling, _ = _mesh_place()
        for t in range(nt):
            for li in range(ins[t].shape[0]):
                _start_rows(ins[t].at[li], outs[t].at[li], send_sems.at[t], recv_sems.at[t], sibling)
        for t in range(nt):
            _remote(outs[t], outs[t], send_sems.at[t], recv_sems.at[t], sibling).wait_recv()
        for t in range(nt):
            _remote(ins[t], ins[t], send_sems.at[t], recv_sems.at[t], sibling).wait_send()

    out_shapes = [jax.ShapeDtypeStruct(h.shape, h.dtype) for h in halves]
    return _comm_call(body, "swap_cores", halves, out_shapes, (nt, nt))


def _pack_rows(pieces, cols, dtype):
    tile = SUBLANES * cols
    rows = []
    for p in pieces:
        flat = p.reshape(-1).astype(dtype)
        rows.append(jnp.pad(flat, (0, (-flat.shape[0]) % tile)).reshape(-1, cols))
    total = sum(r.shape[0] for r in rows)
    if total % (2 * SUBLANES):
        rows.append(jnp.zeros((SUBLANES, cols), dtype))
    return jnp.concatenate(rows, axis=0)


def _unpack_rows(buf, shapes):
    out, row = [], 0
    for s in shapes:
        n = math.prod(s)
        nrows = -(-n // (SUBLANES * buf.shape[1])) * SUBLANES
        out.append(buf[row:row + nrows].reshape(-1)[:n].reshape(s))
        row += nrows
    return out


ROW_SHARDED = ('w_out', 'w_glu')


def _chips_to_full(name, chips):
    mats = [c.reshape(-1, c.shape[-1]) for c in chips]
    if name in ROW_SHARDED:
        return jnp.concatenate(mats, axis=0)
    if name != 'w_in':
        return jnp.concatenate(mats, axis=1)
    cs = mats[0].shape[1]
    pieces = []
    for start, size in _PERM_PIECES:
        lo = start
        while lo < start + size:
            q = lo // cs
            hi = min(start + size, (q + 1) * cs)
            pieces.append(mats[q][:, lo - q * cs:hi - q * cs])
            lo = hi
    return jnp.concatenate(pieces, axis=1)


def _full_to_dests(name, full):
    rows, cols = full.shape
    if name in ROW_SHARDED:
        return jnp.transpose(full.reshape(N_CHIPS, 2, rows // (2 * N_CHIPS), cols), (1, 0, 2, 3))
    cs = cols // N_CHIPS
    if name != 'w_in':
        dests = [full[:, q * cs:(q + 1) * cs] for q in range(N_CHIPS)]
    else:
        offsets, off = [], 0
        for _, size in _PERM_PIECES:
            offsets.append(off)
            off += size
        by_start = sorted(zip(_PERM_PIECES, offsets))
        dests = []
        for q in range(N_CHIPS):
            parts = []
            for (start, size), at in by_start:
                lo, hi = max(start, q * cs), min(start + size, (q + 1) * cs)
                if lo < hi:
                    parts.append(full[:, at + lo - start:at + hi - start])
            dests.append(jnp.concatenate(parts, axis=1))
    return jnp.stack([d.reshape(2, rows // 2, cs) for d in dests], axis=1)


def _by_chip(own, related, x, y):
    grid = ((own, related[1]), (related[0], related[2]))
    along_x = [[jnp.where(x == 0, grid[px][dy], grid[1 - px][dy]) for dy in range(2)] for px in range(2)]
    return [jnp.where(y == 0, along_x[px][py], along_x[px][1 - py]) for px in range(2) for py in range(2)]


def _join_halves(mine, theirs, core, axis):
    return jnp.where(core == 0, jnp.concatenate([mine, theirs], axis=axis), jnp.concatenate([theirs, mine], axis=axis))


ROWS_WGRAD = 2048
SSM_RAW = ('ssm_a_re', 'ssm_a_im', 'ssm_log_dt', 'ssm_b_re', 'ssm_b_im', 'ssm_c_re', 'ssm_c_im')


def _all_tables(p):
    raw = tuple(p[k] for k in SSM_RAW)
    (_, bb, cb), vjp = jax.vjp(jax.vmap(_ssm_tables), *raw)
    return (bb.astype(BF16), cb.astype(BF16), jax.vmap(_ssm_powers)(*raw[:3])), vjp


def _layer_fwd(x, p, w_in, rest, li, mod, tables, sends=None, late_sends=None):
    bb, cb, pw = (t[li] for t in tables)
    h = _norm_fwd(x, _row(p['norm_g'][li]), mod)
    if isinstance(rest, dict):
        proj = _matmul(h, w_in, 'nn', tm=1024, tn=1280, tk=1024, name="proj", n_outer=True)
    else:
        proj, arrived = _matmul(h, w_in, 'nn', tm=1024, tn=1280, tk=1024, name="proj_sending", n_outer=True, sends=rest[0])
        rest = rest[1](arrived)
    full = dict(rest, w_in=w_in)
    o = _attention_fwd(proj, p['attn_sinks'][li])
    y_lin, states, brought = _ssm_fwd(proj, bb, cb, pw, sends)
    y_pool = _pool_fwd(proj, p['w_pool'][li].astype(BF16), _row(p['pool_scale'][li]))
    x_new, brought_late = _merge_fwd(proj, o, y_lin, y_pool, x, _row(p['ssm_d'][li]), full['w_glu'],
                                     _row(p['b_glu'][li]), full['w_br_att'], full['w_br_ssm'], full['w_br_pool'],
                                     full['w_out'], mod, late_sends)
    saved = dict(x=x, h=h, proj=proj, o=o, y_lin=y_lin, y_pool=y_pool, states=states, bb=bb, cb=cb, pw=pw, full=full)
    a, b = len(brought) // 2, len(brought_late) // 2
    return x_new, saved, brought[:a] + brought_late[:b] + brought[a:] + brought_late[b:]


def _layer_bwd(dxo, s, p, li, mod, core, upper=None, own_early=(), w_in_sender=None):
    l = dxo.shape[0]
    proj, full = s['proj'], s['full']
    outs, arrived_upper = _merge_bwd(
        proj, s['o'], s['y_lin'], s['y_pool'], dxo, _row(p['ssm_d'][li]), full['w_glu'], _row(p['b_glu'][li]),
        full['w_br_att'], full['w_br_ssm'], full['w_br_pool'], full['w_out'], mod,
        None if upper is None else _ChipSends(upper, _pick_slab))
    (do, dyl, dus_skip, dyp, dproj, ya, ys, ypl, y2, dt, dpa, dps, dpp, mg, dout, st) = outs
    g = {}
    g['w_br_att'] = _matmul(ya, dpa, 'tn', tm=512, tn=1024, tk=ROWS_WGRAD, name="grad_w_br")
    g['w_br_ssm'] = _matmul(ys, dps, 'tn', tm=512, tn=1024, tk=ROWS_WGRAD, name="grad_w_br")
    g['w_br_pool'] = _matmul(ypl, dpp, 'tn', tm=512, tn=1024, tk=ROWS_WGRAD, name="grad_w_br")
    g['w_out'] = _matmul(mg, dout, 'tn', tm=512, tn=1024, tk=ROWS_WGRAD, name="grad_w_out")
    g['w_glu'] = _matmul(y2, dt, 'tn', tm=512, tn=512, tk=ROWS_WGRAD, name="grad_w_glu")
    g['b_glu'] = st[1, 0:SSM_W]
    g['ssm_d'] = st[1, SSM_W:2 * SSM_W]
    dgate = st[0]

    early = _layer_grad_halves(g, own_early)
    dq, dkv, dsink, theirs = _attention_bwd(proj, p['attn_sinks'][li], do, _SiblingSends(early) if early else None)
    g['attn_sinks'] = dsink[:, 0]
    pair = _pair_sums(early, [BF16] * len(early), core, theirs) if early else []
    sends = _ChipSends(pair, _pick_slab) if pair else None
    dus_scan, dbb, dcb, dlam, brought = _ssm_bwd(proj, s['states'], dyl, s['bb'], s['cb'], s['pw'], sends)
    g['ssm_tables'] = (dlam[:, 0:2, :], dbb, dcb)
    dup, dwp, dps_scale = _pool_bwd(proj, p['w_pool'][li].astype(BF16), _row(p['pool_scale'][li]), dyp)
    g['w_pool'] = dwp
    g['pool_scale'] = dps_scale[0]

    for off, piece in ((OFF_Q, dq), (OFF_US, dus_skip + dus_scan), (OFF_UP, dup), (OFF_KV, dkv[HALO:])):
        dproj = lax.dynamic_update_slice(dproj, piece.astype(BF16), (0, off))
    g['w_in'] = _matmul(s['h'], dproj, 'tn', tm=1024, tn=1280, tk=ROWS_WGRAD, name="grad_w_in")
    if w_in_sender is None:
        dh, brought_late = _matmul(dproj, full['w_in'], 'nt', tm=1024, tn=1024, tk=3200, name="grad_h"), None
    else:
        dh, brought_late = _matmul(dproj, full['w_in'], 'nt', tm=1024, tn=1024, tk=3200, name="grad_h_sending",
                                   sends=w_in_sender(g['w_in']))
    dx, nst = _norm_bwd(s['x'], dh, dxo, _row(p['norm_g'][li]), mod)
    g['norm_g'] = nst[2]
    dmod = jnp.concatenate([nst[0], nst[1], dgate])
    del l
    return dx, g, dmod, (pair, brought), arrived_upper, brought_late


BIG_NAMES = tuple(name for name, _ in BIG)


def _weight_sends(shards, names, li):
    return _ChipSends([shards[name][li] for name in names], _pick_weight_half, pass_on=True)


def _assemble_layer(shards, names, li, arrived, place):
    x, y, core, _ = place
    over_ici, from_sibling = arrived[:len(names)], arrived[len(names):]
    full = {}
    for name, a, b in zip(names, over_ici, from_sibling):
        own = shards[name][li]
        own_halves = own.reshape(2, own.shape[0] // 2, own.shape[1])
        halves = [_join_halves(a[j][None], b[j][None], core, 0) for j in range(3)]
        full[name] = _chips_to_full(name, _by_chip(own_halves, halves, x, y))
    return full


def _pair_sums(tensors, wire, core, theirs=None):
    if theirs is None:
        theirs = _SiblingSends(tensors).run("swap_halves")
    return [_add2(lax.dynamic_index_in_dim(g, core, 0, keepdims=False), a, dt, "sum_core_pair")
            for g, a, dt in zip(tensors, theirs, wire)]


def _layer_grad_halves(layer_grads, names):
    return [_full_to_dests(name, layer_grads[name]) for name in names]


def _step(p, m, v, x, c, target):
    depth = p['norm_g'].shape[0]
    d = D_MODEL
    ix, iy, ic = lax.axis_index("x"), lax.axis_index("y"), lax.axis_index("c")
    chip = 2 * ix + iy
    dev = 4 * ix + 2 * iy + ic
    x0 = x[0]

    c_pad = jnp.pad(c, ((0, SUBLANES - 1), (0, 0)))
    c_all = _gather(c_pad, 'xyc', "gather_c")[:, 0, :]
    n_ada = p['w_ada'].shape[-1]
    b_shard = lax.dynamic_slice_in_dim(p['b_ada'], chip * n_ada, n_ada, axis=1)[:, None, :]
    mod_shard = _ada_fwd(c_all, p['w_ada'].astype(BF16), b_shard)
    mod_all = _gather(mod_shard.reshape(depth * N_DEV, n_ada), 'xy', "gather_mod")
    mod_all = jnp.transpose(mod_all.reshape(N_CHIPS, depth, N_DEV, n_ada), (1, 2, 0, 3)).reshape(depth, N_DEV, 3 * d)
    mods = lax.dynamic_index_in_dim(mod_all, dev, axis=1, keepdims=True)

    place = (ix, iy, ic, chip)
    n_big = len(BIG)

    shards = {name: p[name].astype(BF16) for name, _ in BIG}
    tables, tables_vjp = _all_tables(p)
    others = BIG_NAMES[1:]
    w_in = _assemble_layer(shards, BIG_NAMES[:1], 0, _weight_sends(shards, BIG_NAMES[:1], 0).run("send_weight_halves"),
                           place)['w_in']
    rest = (_weight_sends(shards, others, 0), lambda arrived: _assemble_layer(shards, others, 0, arrived, place))
    saved = []
    xs = x0
    for li in range(depth):
        more = li + 1 < depth
        xs, s, arrived = _layer_fwd(xs, p, w_in, rest, li, mods[li], tables,
                                    _weight_sends(shards, BIG_NAMES[:1], li + 1) if more else None,
                                    _weight_sends(shards, others, li + 1) if more else None)
        saved.append(s)
        if more:
            rest = _assemble_layer(shards, BIG_NAMES, li + 1, arrived, place)
            w_in = rest.pop('w_in')
    dx, hst = _loss_head(xs, _row(p['final_g']), target[0])
    loss = lax.psum(hst[1, 0], ("x", "y", "c"))

    grads, dmods, pair, from_chips = [None] * depth, [None] * depth, [None] * depth, [None] * depth
    w_in_halves, w_in_pair, w_in_arrived = {}, {}, {}

    def w_in_sender(li):
        def to_chips(g_w_in):
            w_in_pair[li] = _pair_sums([_full_to_dests('w_in', g_w_in)], [BF16], ic)
            return _ChipSends(w_in_pair[li], _pick_slab)

        def to_sibling(g_w_in):
            w_in_halves[li] = [_full_to_dests('w_in', g_w_in)]
            return _SiblingSends(w_in_halves[li])

        return to_chips if li == 0 else to_sibling

    upper = None
    for li in reversed(range(depth)):
        dx, grads[li], dmods[li], early, arrived_upper, arrived_late = _layer_bwd(
            dx, saved[li], p, li, mods[li], ic, upper, others, w_in_sender(li))
        if upper is not None:
            w_in_arrived[li + 1] = arrived_upper
        if li > 0:
            w_in_pair[li] = _pair_sums(w_in_halves[li], [BF16], ic, arrived_late)
            upper = w_in_pair[li]
        else:
            w_in_arrived[li] = arrived_late
        pair[li] = w_in_pair[li] + early[0]
        from_chips[li] = early[1]
    for li in range(depth):
        from_chips[li] = list(w_in_arrived[li]) + list(from_chips[li])

    dmod_pad = jnp.pad(jnp.stack(dmods), ((0, SUBLANES - depth), (0, 0)))
    dmod_all = _gather(dmod_pad, 'xyc', "gather_dmod")[:, :depth, :]
    dmod_cols = lax.dynamic_slice_in_dim(jnp.transpose(dmod_all, (1, 0, 2)), chip * n_ada, n_ada, axis=2)
    g_w_ada = _ada_bwd(jnp.transpose(c_all), dmod_cols)

    local_small = {k: jnp.stack([grads[li][k] for li in range(depth)])
                   for k in SMALL if k not in ('final_g', 'b_ada') + SSM_RAW}
    table_cotangents = tuple(jnp.stack([grads[li]['ssm_tables'][k] for li in range(depth)]) for k in range(3))
    local_small.update(zip(SSM_RAW, tables_vjp(table_cotangents)))
    local_small['final_g'] = hst[0]
    local_small['b_ada'] = jnp.stack(dmods)
    small_pack = _pack_rows([local_small[k] for k in SMALL], PACK_COLS, F32)

    small_halves = small_pack.reshape(2, 1, small_pack.shape[0] // 2, small_pack.shape[1])
    small_pair = _pair_sums([small_halves], [F32], ic)
    pair[0] = pair[0] + small_pair
    from_chips[0] = from_chips[0] + list(_ChipSends(small_pair, _pick_slab).run("scatter_chips"))
    halves = []
    for li in range(depth):
        for t, (ps, others) in enumerate(zip(pair[li], from_chips[li])):
            small = t == n_big
            own = ps[0] if small else lax.dynamic_index_in_dim(ps, chip, 0, keepdims=False)
            halves.append(_sum_chips(own, others, small, "sum_chips_small" if small else "sum_chips")[None])
    totals = [_join_halves(h[0], o[0], ic, 0) for h, o in zip(halves, _swap_cores(halves))]
    small_sum = totals[n_big]
    per_layer = [totals[:n_big]] + [totals[n_big + 1 + (li - 1) * n_big:n_big + 1 + li * n_big] for li in range(1, depth)]

    small_shapes = [p[k].shape for k in SMALL]
    grad = dict(zip(SMALL, _unpack_rows(small_sum, small_shapes)))
    grad['w_ada'] = g_w_ada
    for t, (name, _) in enumerate(BIG):
        grad[name] = jnp.stack([per_layer[li][t] for li in range(depth)])

    delta, new_m, new_v = {}, {}, {}
    outs = _adamw(_pack_rows([p[k] for k in SMALL], PACK_COLS, F32), small_sum,
                  _pack_rows([m[k] for k in SMALL], PACK_COLS, F32), _pack_rows([v[k] for k in SMALL], PACK_COLS, F32),
                  name="adamw_small")
    for res, o in zip((delta, new_m, new_v), outs):
        res.update(zip(SMALL, _unpack_rows(o, small_shapes)))
    for name in ['w_ada'] + [n for n, _ in BIG]:
        shape = p[name].shape
        two_d = (-1, shape[-1])
        outs = _adamw(p[name].reshape(two_d), grad[name].reshape(two_d), m[name].reshape(two_d), v[name].reshape(two_d),
                      name="adamw_" + name)
        delta[name], new_m[name], new_v[name] = (o.reshape(shape) for o in outs)

    return (loss, dx[None], *[grad[k] for k in WEIGHTS], *[delta[k] for k in WEIGHTS],
            *[new_m[k] for k in WEIGHTS], *[new_v[k] for k in WEIGHTS])


def kernel(x, c, norm_g, w_ada, b_ada, w_in, attn_sinks, ssm_a_re, ssm_a_im, ssm_log_dt, ssm_b_re, ssm_b_im, ssm_c_re, ssm_c_im, ssm_d, w_glu, b_glu, w_pool, pool_scale, w_br_att, w_br_ssm, w_br_pool, w_out, final_g, loss_target, m_norm_g, m_w_ada, m_b_ada, m_w_in, m_attn_sinks, m_ssm_a_re, m_ssm_a_im, m_ssm_log_dt, m_ssm_b_re, m_ssm_b_im, m_ssm_c_re, m_ssm_c_im, m_ssm_d, m_w_glu, m_b_glu, m_w_pool, m_pool_scale, m_w_br_att, m_w_br_ssm, m_w_br_pool, m_w_out, m_final_g, v_norm_g, v_w_ada, v_b_ada, v_w_in, v_attn_sinks, v_ssm_a_re, v_ssm_a_im, v_ssm_log_dt, v_ssm_b_re, v_ssm_b_im, v_ssm_c_re, v_ssm_c_im, v_ssm_d, v_w_glu, v_b_glu, v_w_pool, v_pool_scale, v_w_br_att, v_w_br_ssm, v_w_br_pool, v_w_out, v_final_g):
    p = dict(zip(WEIGHTS, (norm_g, w_ada, b_ada, w_in, attn_sinks, ssm_a_re, ssm_a_im, ssm_log_dt, ssm_b_re, ssm_b_im,
                           ssm_c_re, ssm_c_im, ssm_d, w_glu, b_glu, w_pool, pool_scale, w_br_att, w_br_ssm, w_br_pool,
                           w_out, final_g)))
    m = dict(zip(WEIGHTS, (m_norm_g, m_w_ada, m_b_ada, m_w_in, m_attn_sinks, m_ssm_a_re, m_ssm_a_im, m_ssm_log_dt,
                           m_ssm_b_re, m_ssm_b_im, m_ssm_c_re, m_ssm_c_im, m_ssm_d, m_w_glu, m_b_glu, m_w_pool,
                           m_pool_scale, m_w_br_att, m_w_br_ssm, m_w_br_pool, m_w_out, m_final_g)))
    v = dict(zip(WEIGHTS, (v_norm_g, v_w_ada, v_b_ada, v_w_in, v_attn_sinks, v_ssm_a_re, v_ssm_a_im, v_ssm_log_dt,
                           v_ssm_b_re, v_ssm_b_im, v_ssm_c_re, v_ssm_c_im, v_ssm_d, v_w_glu, v_b_glu, v_w_pool,
                           v_pool_scale, v_w_br_att, v_w_br_ssm, v_w_br_pool, v_w_out, v_final_g)))
    return _step(p, m, v, x, c, loss_target)
```

```python
import functools
import math

import jax
import jax.numpy as jnp
from jax import lax
from jax.experimental import pallas as pl
from jax.experimental.pallas import tpu as pltpu

F32 = jnp.float32
BF16 = jnp.bfloat16

D_MODEL = 1024
CHUNK = 64
N_HEADS = 8
N_KV_HEADS = 2
HEAD_DIM = 64
Q_PER_KV = N_HEADS // N_KV_HEADS
HALO = 128
ATT_W = 512
KV_W = 128
SSM_W = 512
SSM_GROUP = 16
SSM_GROUPS = 32
SSM_STATE = 64
POOL_W = 512
POOL_GW = 128
IN_W = 6400
EPS = 1e-6
NEG_INF = -1e30
ADAM_LR = 0.001
ADAM_B1 = 0.9
ADAM_B2 = 0.999
ADAM_EPS = 1e-08
ADAM_WD = 0.01
ADAM_STEP = 10

OFF_GL, OFF_Q, OFF_ZA, OFF_US, OFF_ZS, OFF_UP, OFF_ZP, OFF_KV = 0, 3072, 3584, 4096, 4608, 5120, 5632, 6144
_PERM_PIECES = ((3328, 3072), (0, 512), (1792, 512), (768, 512), (2304, 512), (1280, 512), (2816, 512), (512, 256))

LANES = 128
SUBLANES = 8
N_SBLK = SSM_GROUPS * SSM_STATE // LANES
VMEM_LIMIT = 48 * 1024 * 1024

N_CHIPS = 4
N_DEV = 8

WEIGHTS = ['norm_g', 'w_ada', 'b_ada', 'w_in', 'attn_sinks', 'ssm_a_re', 'ssm_a_im', 'ssm_log_dt', 'ssm_b_re',
           'ssm_b_im', 'ssm_c_re', 'ssm_c_im', 'ssm_d', 'w_glu', 'b_glu', 'w_pool', 'pool_scale', 'w_br_att',
           'w_br_ssm', 'w_br_pool', 'w_out', 'final_g']
SMALL = ['norm_g', 'b_ada', 'attn_sinks', 'ssm_a_re', 'ssm_a_im', 'ssm_log_dt', 'ssm_b_re', 'ssm_b_im', 'ssm_c_re',
         'ssm_c_im', 'ssm_d', 'b_glu', 'w_pool', 'pool_scale', 'final_g']
BIG = (('w_in', (1024, 1600)), ('w_br_att', (512, 256)), ('w_br_ssm', (512, 256)), ('w_br_pool', (512, 256)),
       ('w_out', (256, 1024)), ('w_glu', (128, 512)))
PACK_COLS = 1024


def _params(sem=None, vmem=VMEM_LIMIT):
    return pltpu.CompilerParams(dimension_semantics=sem, vmem_limit_bytes=vmem)


def _row(v):
    return v.reshape(1, -1)


def _full(shape):
    nd = len(shape)
    return pl.BlockSpec(shape, lambda *_: (0,) * nd)


def _sigmoid(v):
    return 1.0 / (1.0 + jnp.exp(-v))


def _silu_and_grad(z):
    s = _sigmoid(z)
    silu = z * s
    return silu, s + silu * (1.0 - s)


_GELU_K = math.sqrt(2.0 / math.pi)


def _gelu_and_grad(v):
    inner = _GELU_K * (v + 0.044715 * v * v * v)
    th = jnp.tanh(inner)
    val = 0.5 * v * (1.0 + th)
    grad = 0.5 * (1.0 + th) + 0.5 * v * (1.0 - th * th) * _GELU_K * (1.0 + 3 * 0.044715 * v * v)
    return val, grad


_NN = (((1,), (0,)), ((), ()))
_NT = (((1,), (1,)), ((), ()))
_TN = (((0,), (0,)), ((), ()))


def _dot(a, b, dims=_NN):
    return lax.dot_general(a.astype(BF16), b.astype(BF16), dims, preferred_element_type=F32)


def _matmul(a, b, mode, *, tm, tn, tk, name, out_dtype=F32, n_outer=False, sends=None):
    if mode == 'nn':
        (m, k), (_, n) = a.shape, b.shape
    elif mode == 'nt':
        (m, k), (n, _) = a.shape, b.shape
    else:
        (k, m), (_, n) = a.shape, b.shape
    tm, tn, tk = min(tm, m), min(tn, n), min(tk, k)
    assert m % tm == 0 and n % tn == 0 and k % tk == 0, (name, a.shape, b.shape)
    nk = k // tk
    dims = {'nn': _NN, 'nt': _NT, 'tn': _TN}[mode]

    def body(a_ref, b_ref, o_ref, acc_ref):
        if nk == 1:
            o_ref[...] = _dot(a_ref[...], b_ref[...], dims).astype(out_dtype)
            return
        kk = pl.program_id(2)

        @pl.when(kk == 0)
        def _():
            acc_ref[...] = jnp.zeros_like(acc_ref)

        acc_ref[...] += _dot(a_ref[...], b_ref[...], dims)

        @pl.when(kk == nk - 1)
        def _():
            o_ref[...] = acc_ref[...].astype(out_dtype)

    def spec(shape, index):
        if n_outer:
            return pl.BlockSpec(shape, lambda j, i, kk: index(i, j, kk))
        return pl.BlockSpec(shape, index)

    a_spec = spec((tk, tm), lambda i, j, kk: (kk, i)) if mode == 'tn' else spec((tm, tk), lambda i, j, kk: (i, kk))
    b_spec = spec((tn, tk), lambda i, j, kk: (j, kk)) if mode == 'nt' else spec((tk, tn), lambda i, j, kk: (kk, j))
    grid = (n // tn, m // tm, nk) if n_outer else (m // tm, n // tn, nk)
    body, x_in, x_out, x_shapes, x_scratch, x_args = _carry_sends(body, 2, 1, 1, grid, sends)
    semantics = ("parallel", "parallel", "arbitrary") if sends is None else ("arbitrary",) * 3
    outs = pl.pallas_call(
        body, name=name, grid=grid, in_specs=[a_spec, b_spec] + x_in,
        out_specs=[spec((tm, tn), lambda i, j, kk: (i, j))] + x_out,
        out_shape=[jax.ShapeDtypeStruct((m, n), out_dtype)] + x_shapes,
        scratch_shapes=[pltpu.VMEM((tm, tn), F32)] + x_scratch, compiler_params=_params(semantics))(a, b, *x_args)
    return outs[0] if sends is None else (outs[0], list(outs[1:]))


ROWS_NORM = 1024


def _norm_fwd(x, g, mod):
    l, d = x.shape
    tr = min(ROWS_NORM, l)

    def body(x_ref, g_ref, mod_ref, h_ref):
        xv = x_ref[...]
        r = lax.rsqrt(jnp.mean(xv * xv, axis=-1, keepdims=True) + EPS)
        shift, scale = mod_ref[:, 0:d], mod_ref[:, d:2 * d]
        h_ref[...] = ((xv * r * g_ref[...]) * (1.0 + scale) + shift).astype(BF16)

    return pl.pallas_call(
        body, name="norm_fwd", grid=(l // tr,),
        in_specs=[pl.BlockSpec((tr, d), lambda i: (i, 0)), _full((1, d)), _full((1, 3 * d))],
        out_specs=pl.BlockSpec((tr, d), lambda i: (i, 0)), out_shape=jax.ShapeDtypeStruct((l, d), BF16),
        compiler_params=_params(("parallel",)))(x, g, mod)


def _norm_bwd(x, dh, dxo, g, mod):
    l, d = x.shape
    tr = min(ROWS_NORM, l)

    def body(x_ref, dh_ref, dxo_ref, g_ref, mod_ref, dx_ref, st_ref):
        @pl.when(pl.program_id(0) == 0)
        def _():
            st_ref[...] = jnp.zeros_like(st_ref)

        xv, dhv = x_ref[...], dh_ref[...]
        r = lax.rsqrt(jnp.mean(xv * xv, axis=-1, keepdims=True) + EPS)
        xn = xv * r
        gv = g_ref[...]
        sc1 = 1.0 + mod_ref[:, d:2 * d]
        dxn = dhv * gv * sc1
        dx_ref[...] = dxo_ref[...] + r * (dxn - xn * jnp.mean(dxn * xn, axis=-1, keepdims=True))
        st_ref[0:1, :] += jnp.sum(dhv, axis=0, keepdims=True)
        st_ref[1:2, :] += jnp.sum(dhv * xn * gv, axis=0, keepdims=True)
        st_ref[2:3, :] += jnp.sum(dhv * xn * sc1, axis=0, keepdims=True)

    blk = pl.BlockSpec((tr, d), lambda i: (i, 0))
    return pl.pallas_call(
        body, name="norm_bwd", grid=(l // tr,),
        in_specs=[blk, blk, blk, _full((1, d)), _full((1, 3 * d))],
        out_specs=[blk, _full((SUBLANES, d))],
        out_shape=[jax.ShapeDtypeStruct((l, d), F32), jax.ShapeDtypeStruct((SUBLANES, d), F32)],
        compiler_params=_params(("arbitrary",)))(x, dh, dxo, g, mod)


def _loss_head(x, g, target):
    l, d = x.shape
    tr = min(ROWS_NORM, l)

    def body(x_ref, g_ref, t_ref, dx_ref, st_ref):
        @pl.when(pl.program_id(0) == 0)
        def _():
            st_ref[...] = jnp.zeros_like(st_ref)

        xv = x_ref[...]
        r = lax.rsqrt(jnp.mean(xv * xv, axis=-1, keepdims=True) + EPS)
        xn = xv * r
        gv = g_ref[...]
        err = xn * gv - t_ref[...]
        part = 0.5 * jnp.sum(jnp.mean(err * err, axis=-1, keepdims=True), axis=0, keepdims=True)
        dy = err * (1.0 / d)
        dxn = dy * gv
        dx_ref[...] = r * (dxn - xn * jnp.mean(dxn * xn, axis=-1, keepdims=True))
        st_ref[0:1, :] += jnp.sum(dy * xn, axis=0, keepdims=True)
        st_ref[1:2, :] += jnp.broadcast_to(part, (1, d))

    blk = pl.BlockSpec((tr, d), lambda i: (i, 0))
    return pl.pallas_call(
        body, name="loss_head", grid=(l // tr,), in_specs=[blk, _full((1, d)), blk],
        out_specs=[blk, _full((SUBLANES, d))],
        out_shape=[jax.ShapeDtypeStruct((l, d), F32), jax.ShapeDtypeStruct((SUBLANES, d), F32)],
        compiler_params=_params(("arbitrary",)))(x, g, target)


ROWS_ATT = 128
ROWS_ATT_BWD = 256
_SLOPES = tuple(2.0 ** (-8.0 * (h + 1) / N_HEADS) for h in range(N_HEADS))


def _att_mask(i, t):
    r = lax.broadcasted_iota(jnp.int32, (t, t + HALO), 0)
    j = lax.broadcasted_iota(jnp.int32, (t, t + HALO), 1)
    dist = jnp.abs(r + HALO - j).astype(F32)
    rc, jc = r // CHUNK, j // CHUNK
    allowed = (jc >= rc) & (jc <= rc + 2) & ((j >= HALO) | (i > 0))
    return dist, allowed


def _att_probs(qh, k, dist, allowed, slope, sink):
    s = _dot(qh, k, _NT) * (1.0 / math.sqrt(HEAD_DIM)) - slope * dist
    s = jnp.where(allowed, s, NEG_INF)
    m = jnp.maximum(jnp.max(s, axis=1, keepdims=True), sink)
    e = jnp.exp(s - m)
    es = jnp.exp(sink - m)
    den = jnp.sum(e, axis=1, keepdims=True) + es
    return e / den, es / den


def _att_specs(t):
    q_spec = pl.BlockSpec((t, ATT_W), lambda i: (i, OFF_Q // ATT_W))
    kv_spec = pl.BlockSpec((t, 2 * KV_W), lambda i: (i, OFF_KV // (2 * KV_W)))
    halo_spec = pl.BlockSpec((HALO, 2 * KV_W), lambda i: (jnp.maximum(i * (t // HALO) - 1, 0), OFF_KV // (2 * KV_W)))
    return q_spec, kv_spec, halo_spec


def _attention_fwd(proj, sinks):
    l = proj.shape[0]
    t = min(ROWS_ATT, l)

    def body(sink_ref, q_ref, kv_ref, halo_ref, o_ref):
        dist, allowed = _att_mask(pl.program_id(0), t)
        kv = jnp.concatenate([halo_ref[...], kv_ref[...]], axis=0)
        for h in range(N_HEADS):
            kh = h // Q_PER_KV
            k = kv[:, kh * HEAD_DIM:(kh + 1) * HEAD_DIM]
            v = kv[:, KV_W + kh * HEAD_DIM:KV_W + (kh + 1) * HEAD_DIM]
            p, _ = _att_probs(q_ref[:, h * HEAD_DIM:(h + 1) * HEAD_DIM], k, dist, allowed, _SLOPES[h], sink_ref[h])
            o_ref[:, h * HEAD_DIM:(h + 1) * HEAD_DIM] = _dot(p, v)

    q_spec, kv_spec, halo_spec = _att_specs(t)
    return pl.pallas_call(
        body, name="attention_fwd", grid=(l // t,),
        in_specs=[pl.BlockSpec(memory_space=pltpu.SMEM), q_spec, kv_spec, halo_spec],
        out_specs=pl.BlockSpec((t, ATT_W), lambda i: (i, 0)), out_shape=jax.ShapeDtypeStruct((l, ATT_W), F32),
        compiler_params=_params(("parallel",)))(sinks, proj, proj, proj)


def _attention_bwd(proj, sinks, do, sends=None):
    l = proj.shape[0]
    t = min(ROWS_ATT_BWD, l)

    def body(sink_ref, q_ref, kv_ref, halo_ref, do_ref, dq_ref, dkv_ref, dsink_ref):
        i = pl.program_id(0)

        @pl.when(i == 0)
        def _():
            dkv_ref[...] = jnp.zeros_like(dkv_ref)
            dsink_ref[...] = jnp.zeros_like(dsink_ref)

        dist, allowed = _att_mask(i, t)
        kv = jnp.concatenate([halo_ref[...], kv_ref[...]], axis=0)
        rows = pl.ds(pl.multiple_of(i * t, t), t + HALO)
        for kh in range(N_KV_HEADS):
            k = kv[:, kh * HEAD_DIM:(kh + 1) * HEAD_DIM]
            v = kv[:, KV_W + kh * HEAD_DIM:KV_W + (kh + 1) * HEAD_DIM]
            dk = jnp.zeros((t + HALO, HEAD_DIM), F32)
            dv = jnp.zeros((t + HALO, HEAD_DIM), F32)
            for h in range(kh * Q_PER_KV, (kh + 1) * Q_PER_KV):
                qh = q_ref[:, h * HEAD_DIM:(h + 1) * HEAD_DIM]
                doh = do_ref[:, h * HEAD_DIM:(h + 1) * HEAD_DIM]
                p, ps = _att_probs(qh, k, dist, allowed, _SLOPES[h], sink_ref[h])
                dp = _dot(doh, v, _NT)
                delta = jnp.sum(p * dp, axis=1, keepdims=True)
                ds = p * (dp - delta) * (1.0 / math.sqrt(HEAD_DIM))
                dsink_ref[h:h + 1, :] += jnp.broadcast_to(-jnp.sum(ps * delta, axis=0, keepdims=True), (1, LANES))
                dq_ref[:, h * HEAD_DIM:(h + 1) * HEAD_DIM] = _dot(ds, k).astype(BF16)
                dk = dk + _dot(ds, qh, _TN)
                dv = dv + _dot(p, doh, _TN)
            dkv_ref[rows, kh * HEAD_DIM:(kh + 1) * HEAD_DIM] += dk
            dkv_ref[rows, KV_W + kh * HEAD_DIM:KV_W + (kh + 1) * HEAD_DIM] += dv

    q_spec, kv_spec, halo_spec = _att_specs(t)
    blk = pl.BlockSpec((t, ATT_W), lambda i: (i, 0))
    body, x_in, x_out, x_shapes, x_scratch, x_args = _carry_sends(body, 5, 3, 0, (l // t,), sends)
    outs = pl.pallas_call(
        body, name="attention_bwd" if sends is None else "attention_bwd_sending", grid=(l // t,),
        in_specs=[pl.BlockSpec(memory_space=pltpu.SMEM), q_spec, kv_spec, halo_spec, blk] + x_in,
        out_specs=[blk, _full((HALO + l, 2 * KV_W)), _full((N_HEADS, LANES))] + x_out,
        out_shape=[jax.ShapeDtypeStruct((l, ATT_W), BF16), jax.ShapeDtypeStruct((HALO + l, 2 * KV_W), F32),
                   jax.ShapeDtypeStruct((N_HEADS, LANES), F32)] + x_shapes,
        scratch_shapes=x_scratch, compiler_params=_params(("arbitrary",)))(sinks, proj, proj, proj, do, *x_args)
    return outs[0], outs[1], outs[2], list(outs[3:])


ROWS_SSM = 2048
SSM_BLOCKS = 2
SCAN_UNROLL = 2
SSM_STEPS_PER_U = LANES // (SSM_BLOCKS * 2 * SSM_GROUP)


def _ssm_discretize(a_re, a_im, log_dt, b_re, b_im):
    lam = lax.complex(a_re, a_im)
    dt = jnp.exp(log_dt)[:, None]
    lam_bar = jnp.exp(lam * dt)
    b_bar = ((lam_bar - 1.0) / lam)[..., None] * lax.complex(b_re, b_im)
    return lam, dt, lam_bar, b_bar


def _ssm_block_diag(m):
    e = m.reshape(N_SBLK, 2, SSM_GROUP, SSM_STATE)
    e = e[:, :, :, None, :] * jnp.eye(2, dtype=m.dtype)[None, :, None, :, None]
    e = e.reshape(N_SBLK, 2 * SSM_GROUP, LANES)
    oh = jax.nn.one_hot(jnp.arange(N_SBLK) % 4, 4, dtype=m.dtype)
    return (oh[:, :, None, None] * e[:, None]).reshape(N_SBLK, LANES, LANES)


def _ssm_tables(a_re, a_im, log_dt, b_re, b_im, c_re, c_im):
    _, _, lam_bar, b_bar = _ssm_discretize(a_re, a_im, log_dt, b_re, b_im)
    lam_blk = jnp.stack([jnp.real(lam_bar).reshape(N_SBLK, LANES), jnp.imag(lam_bar).reshape(N_SBLK, LANES)], axis=1)
    bt = jnp.transpose(b_bar, (0, 2, 1))
    bb = jnp.concatenate([_ssm_block_diag(jnp.real(bt)), _ssm_block_diag(jnp.imag(bt))], axis=2)
    cb = jnp.concatenate([jnp.transpose(_ssm_block_diag(c_re), (0, 2, 1)),
                          jnp.transpose(_ssm_block_diag(-c_im), (0, 2, 1))], axis=1)
    return lam_blk, bb, cb


def _ssm_powers(a_re, a_im, log_dt):
    lam = lax.complex(a_re, a_im)
    dt = jnp.exp(log_dt)[:, None]
    k = jnp.arange(1, SUBLANES + 1, dtype=F32)
    pw = jnp.exp((lam * dt)[None] * k[:, None, None]).reshape(SUBLANES, N_SBLK, LANES)
    pw = jnp.transpose(pw, (1, 0, 2))
    rev = pw[:, ::-1]
    return jnp.concatenate([jnp.real(pw), jnp.imag(pw), jnp.real(rev), jnp.imag(rev)], axis=1)


def _scan_consts(pw_ref, b, reverse):
    row = lax.broadcasted_iota(jnp.int32, (SUBLANES, LANES), 0)
    sign = -1.0 if reverse else 1.0

    def power(k):
        return (jnp.broadcast_to(pw_ref[b, k - 1:k, :], (SUBLANES, LANES)),
                sign * jnp.broadcast_to(pw_ref[b, SUBLANES + k - 1:SUBLANES + k, :], (SUBLANES, LANES)))

    steps = []
    for d in (1, 2, 4):
        pr, pi = power(d)
        keep = (row < SUBLANES - d) if reverse else (row >= d)
        steps.append((d, jnp.where(keep, pr, 0.0), jnp.where(keep, pi, 0.0)))
    base = 2 * SUBLANES if reverse else 0
    return steps, pw_ref[b, base:base + SUBLANES, :], sign * pw_ref[b, base + SUBLANES:base + 2 * SUBLANES, :]


def _scan_tile(xr, xi, steps, reverse):
    for d, ar, ai in steps:
        shift = SUBLANES - d if reverse else d
        rr, ri = pltpu.roll(xr, shift, 0), pltpu.roll(xi, shift, 0)
        xr, xi = xr + (ar * rr - ai * ri), xi + (ar * ri + ai * rr)
    return xr, xi


def _bcast_row(v, r):
    return jnp.broadcast_to(v[r:r + 1, :], (SUBLANES, LANES))


def _re_im(b):
    return slice(2 * b * LANES, (2 * b + 1) * LANES), slice((2 * b + 1) * LANES, (2 * b + 2) * LANES)


def _scan_forward(s_scr, pw_ref, n_tiles):
    consts = [_scan_consts(pw_ref, b, False) for b in range(SSM_BLOCKS)]

    def tiles(g, carry):
        carry = list(carry)
        rows = [pl.ds(pl.multiple_of((g * SCAN_UNROLL + u) * SUBLANES, SUBLANES), SUBLANES) for u in range(SCAN_UNROLL)]
        loaded = [[(s_scr[r, _re_im(b)[0]], s_scr[r, _re_im(b)[1]]) for b in range(SSM_BLOCKS)] for r in rows]
        local = [[_scan_tile(xr, xi, consts[b][0], False) for b, (xr, xi) in enumerate(per_row)] for per_row in loaded]
        for r, per_row in zip(rows, local):
            for b, (xr, xi) in enumerate(per_row):
                _, pr, pi = consts[b]
                cr, ci = carry[b]
                sr, si = xr + (pr * cr - pi * ci), xi + (pr * ci + pi * cr)
                s_scr[r, _re_im(b)[0]] = sr
                s_scr[r, _re_im(b)[1]] = si
                carry[b] = (_bcast_row(sr, SUBLANES - 1), _bcast_row(si, SUBLANES - 1))
        return tuple(carry)

    zero = jnp.zeros((SUBLANES, LANES), F32)
    lax.fori_loop(0, n_tiles // SCAN_UNROLL, tiles, ((zero, zero),) * SSM_BLOCKS)


def _row_chunks(l):
    rc = min(ROWS_SSM, l)
    return rc, l // rc


def _carry_sends(body, n_in, n_out, n_scratch, grid, sends):
    if sends is None:
        return body, [], [], [], [], []
    k, k_out = len(sends.srcs), len(sends.out_shapes)

    def carrying(*refs):
        ins, send_ins = refs[:n_in], refs[n_in:n_in + k]
        outs, send_outs = refs[n_in + k:n_in + k + n_out], refs[n_in + k + n_out:n_in + k + n_out + k_out]
        rest = refs[n_in + k + n_out + k_out:]
        scratch, sems = rest[:n_scratch], rest[n_scratch:]
        ids = [pl.program_id(axis) for axis in range(len(grid))]
        first = functools.reduce(jnp.logical_and, [i == 0 for i in ids])
        last = functools.reduce(jnp.logical_and, [i == n - 1 for i, n in zip(ids, grid)])

        @pl.when(first)
        def _():
            sends.start(send_ins, send_outs, *sems)

        body(*ins, *outs, *scratch)

        @pl.when(last)
        def _():
            sends.finish(send_ins, send_outs, *sems)

    in_specs, out_specs, scratch = sends.specs()
    return carrying, in_specs, out_specs, sends.out_shapes, scratch, sends.srcs


def _ssm_fwd(proj, bb, cb, pw, sends=None):
    l = proj.shape[0]
    rc, n_chunks = _row_chunks(l)

    def body(u_ref, bb_ref, cb_ref, pw_ref, y_ref, s_scr):
        j = pl.program_id(0)

        def fill(ci, _):
            rows = pl.ds(pl.multiple_of(ci * rc, rc), rc)
            uv = u_ref[rows, :]
            for b in range(SSM_BLOCKS):
                s_scr[rows, 2 * b * LANES:2 * (b + 1) * LANES] = _dot(uv, bb_ref[b])
            return 0

        lax.fori_loop(0, n_chunks, fill, 0)
        _scan_forward(s_scr, pw_ref, l // SUBLANES)

        @pl.when(j % SSM_STEPS_PER_U == 0)
        def _():
            y_ref[...] = jnp.zeros_like(y_ref)

        def emit(ci, _):
            rows = pl.ds(pl.multiple_of(ci * rc, rc), rc)
            for b in range(SSM_BLOCKS):
                y_ref[rows, :] += _dot(s_scr[rows, 2 * b * LANES:2 * (b + 1) * LANES], cb_ref[b])
            return 0

        lax.fori_loop(0, n_chunks, emit, 0)

    steps = N_SBLK // SSM_BLOCKS
    body, x_in, x_out, x_shapes, x_scratch, x_args = _carry_sends(body, 4, 2, 0, (steps,), sends)
    outs = pl.pallas_call(
        body, name="ssm_fwd" if sends is None else "ssm_fwd_sending", grid=(steps,),
        in_specs=[pl.BlockSpec((l, LANES), lambda j: (0, OFF_US // LANES + j // SSM_STEPS_PER_U)),
                  pl.BlockSpec((SSM_BLOCKS, LANES, 2 * LANES), lambda j: (j, 0, 0)),
                  pl.BlockSpec((SSM_BLOCKS, 2 * LANES, LANES), lambda j: (j, 0, 0)),
                  pl.BlockSpec((SSM_BLOCKS, 4 * SUBLANES, LANES), lambda j: (j, 0, 0))] + x_in,
        out_specs=[pl.BlockSpec((l, LANES), lambda j: (0, j // SSM_STEPS_PER_U)),
                   pl.BlockSpec((l, SSM_BLOCKS * 2 * LANES), lambda j: (0, j))] + x_out,
        out_shape=[jax.ShapeDtypeStruct((l, SSM_W), F32), jax.ShapeDtypeStruct((l, N_SBLK * 2 * LANES), F32)] + x_shapes,
        scratch_shapes=x_scratch, compiler_params=_params(("arbitrary",)))(proj, bb, cb, pw, *x_args)
    return outs[0], outs[1], list(outs[2:])


def _ssm_bwd(proj, states, dy, bb, cb, pw, sends=None):
    l = proj.shape[0]
    rc, n_chunks = _row_chunks(l)
    n_tiles = l // SUBLANES

    def body(u_ref, s_scr, dy_ref, bb_ref, cb_ref, pw_ref, du_ref, dbb_ref, dcb_ref, dlam_ref, a_scr):
        j = pl.program_id(0)
        dcb_ref[...] = jnp.zeros_like(dcb_ref)

        def through_c(ci, _):
            rows = pl.ds(pl.multiple_of(ci * rc, rc), rc)
            dyv = dy_ref[rows, :]
            for b in range(SSM_BLOCKS):
                lanes = slice(2 * b * LANES, 2 * (b + 1) * LANES)
                dcb_ref[b] += _dot(s_scr[rows, lanes], dyv, _TN)
                a_scr[rows, lanes] = _dot(dyv, cb_ref[b], _NT)
            return 0

        lax.fori_loop(0, n_chunks, through_c, 0)

        consts = [_scan_consts(pw_ref, b, True) for b in range(SSM_BLOCKS)]
        row = lax.broadcasted_iota(jnp.int32, (SUBLANES, LANES), 0)
        last = row == SUBLANES - 1

        def tiles(g, carry):
            carry = list(carry)
            rows = [pl.ds(pl.multiple_of((n_tiles - 1 - (g * SCAN_UNROLL + u)) * SUBLANES, SUBLANES), SUBLANES)
                    for u in range(SCAN_UNROLL)]
            loaded = [[(a_scr[r, _re_im(b)[0]], a_scr[r, _re_im(b)[1]]) for b in range(SSM_BLOCKS)] for r in rows]
            states = [[(s_scr[r, _re_im(b)[0]], s_scr[r, _re_im(b)[1]]) for b in range(SSM_BLOCKS)] for r in rows]
            local = [[_scan_tile(xr, xi, consts[b][0], True) for b, (xr, xi) in enumerate(per_row)] for per_row in loaded]
            for r, per_row, state_row in zip(rows, local, states):
                for b, ((xr, xi), (sr, si)) in enumerate(zip(per_row, state_row)):
                    _, pr, pi = consts[b]
                    cr, ci, accr, acci = carry[b]
                    gr, gi = xr + (pr * cr - pi * ci), xi + (pr * ci + pi * cr)
                    a_scr[r, _re_im(b)[0]] = gr
                    a_scr[r, _re_im(b)[1]] = gi
                    ur = jnp.where(last, cr, pltpu.roll(gr, SUBLANES - 1, 0))
                    ui = jnp.where(last, ci, pltpu.roll(gi, SUBLANES - 1, 0))
                    carry[b] = (_bcast_row(gr, 0), _bcast_row(gi, 0),
                                accr + (ur * sr + ui * si), acci + (ui * sr - ur * si))
            return tuple(carry)

        zero = jnp.zeros((SUBLANES, LANES), F32)
        done = lax.fori_loop(0, n_tiles // SCAN_UNROLL, tiles, ((zero, zero, zero, zero),) * SSM_BLOCKS)
        for b in range(SSM_BLOCKS):
            dlr = jnp.broadcast_to(jnp.sum(done[b][2], axis=0, keepdims=True), (SUBLANES, LANES))
            dli = jnp.broadcast_to(jnp.sum(done[b][3], axis=0, keepdims=True), (SUBLANES, LANES))
            dlam_ref[b] = jnp.where(row == 0, dlr, jnp.where(row == 1, dli, 0.0))

        dbb_ref[...] = jnp.zeros_like(dbb_ref)

        @pl.when(j % SSM_STEPS_PER_U == 0)
        def _():
            du_ref[...] = jnp.zeros_like(du_ref)

        def through_b(ci, _):
            rows = pl.ds(pl.multiple_of(ci * rc, rc), rc)
            uv = u_ref[rows, :]
            for b in range(SSM_BLOCKS):
                av = a_scr[rows, 2 * b * LANES:2 * (b + 1) * LANES]
                dbb_ref[b] += _dot(uv, av, _TN)
                du_ref[rows, :] += _dot(av, bb_ref[b], _NT)
            return 0

        lax.fori_loop(0, n_chunks, through_b, 0)

    steps = N_SBLK // SSM_BLOCKS
    body, x_in, x_out, x_shapes, x_scratch, x_args = _carry_sends(body, 6, 4, 1, (steps,), sends)
    outs = pl.pallas_call(
        body, name="ssm_bwd" if sends is None else "ssm_bwd_sending", grid=(steps,),
        in_specs=[pl.BlockSpec((l, LANES), lambda j: (0, OFF_US // LANES + j // SSM_STEPS_PER_U)),
                  pl.BlockSpec((l, SSM_BLOCKS * 2 * LANES), lambda j: (0, j)),
                  pl.BlockSpec((l, LANES), lambda j: (0, j // SSM_STEPS_PER_U)),
                  pl.BlockSpec((SSM_BLOCKS, LANES, 2 * LANES), lambda j: (j, 0, 0)),
                  pl.BlockSpec((SSM_BLOCKS, 2 * LANES, LANES), lambda j: (j, 0, 0)),
                  pl.BlockSpec((SSM_BLOCKS, 4 * SUBLANES, LANES), lambda j: (j, 0, 0))] + x_in,
        out_specs=[pl.BlockSpec((l, LANES), lambda j: (0, j // SSM_STEPS_PER_U)),
                   pl.BlockSpec((SSM_BLOCKS, LANES, 2 * LANES), lambda j: (j, 0, 0)),
                   pl.BlockSpec((SSM_BLOCKS, 2 * LANES, LANES), lambda j: (j, 0, 0)),
                   pl.BlockSpec((SSM_BLOCKS, SUBLANES, LANES), lambda j: (j, 0, 0))] + x_out,
        out_shape=[jax.ShapeDtypeStruct((l, SSM_W), F32), jax.ShapeDtypeStruct((N_SBLK, LANES, 2 * LANES), F32),
                   jax.ShapeDtypeStruct((N_SBLK, 2 * LANES, LANES), F32),
                   jax.ShapeDtypeStruct((N_SBLK, SUBLANES, LANES), F32)] + x_shapes,
        scratch_shapes=[pltpu.VMEM((l, SSM_BLOCKS * 2 * LANES), F32)] + x_scratch,
        compiler_params=_params(("arbitrary",)))(proj, states, dy, bb, cb, pw, *x_args)
    return outs[0], outs[1], outs[2], outs[3], list(outs[4:])


def _pool_windows(v, t, l, ahead):
    def shifted(a, d):
        if ahead:
            return jnp.where(t < l - d, pltpu.roll(a, l - d, 0), 0.0)
        return jnp.where(t >= d, pltpu.roll(a, d, 0), 0.0)

    w2 = v + shifted(v, 1)
    w4 = w2 + shifted(w2, 2)
    w8 = w4 + shifted(w4, 4)
    w16 = w8 + shifted(w8, 8)
    return w2, w4, w8, w16


def _pool_select(g, ws):
    return jnp.where(g == 0, ws[0], jnp.where(g == 1, ws[1], jnp.where(g == 2, ws[2], ws[3])))


def _pool_count(g, t):
    return jnp.minimum(t + 1, jnp.left_shift(2, g)).astype(F32)


def _pool_specs(l):
    return [pl.BlockSpec((l, POOL_GW), lambda g: (0, OFF_UP // POOL_GW + g)),
            pl.BlockSpec((1, POOL_GW, POOL_GW), lambda g: (g, 0, 0)),
            pl.BlockSpec((1, POOL_GW), lambda g: (0, g))]


def _pool_fwd(proj, w_pool, scale):
    l = proj.shape[0]

    def body(u_ref, w_ref, sc_ref, y_ref):
        g = pl.program_id(0)
        t = lax.broadcasted_iota(jnp.int32, (l, 1), 0)
        u = u_ref[...]
        pooled = _pool_select(g, _pool_windows(u, t, l, False)) / _pool_count(g, t) - u
        y_ref[...] = _dot(pooled, w_ref[0]) * sc_ref[...]

    return pl.pallas_call(
        body, name="pool_fwd", grid=(4,), in_specs=_pool_specs(l),
        out_specs=pl.BlockSpec((l, POOL_GW), lambda g: (0, g)), out_shape=jax.ShapeDtypeStruct((l, POOL_W), F32),
        compiler_params=_params(("parallel",)))(proj, w_pool, scale)


def _pool_bwd(proj, w_pool, scale, dy):
    l = proj.shape[0]

    def body(u_ref, w_ref, sc_ref, dy_ref, du_ref, dw_ref, dsc_ref):
        g = pl.program_id(0)
        t = lax.broadcasted_iota(jnp.int32, (l, 1), 0)
        u = u_ref[...]
        cnt = _pool_count(g, t)
        pooled = _pool_select(g, _pool_windows(u, t, l, False)) / cnt - u
        dyv = dy_ref[...]
        dsc_ref[...] = jnp.sum(dyv * _dot(pooled, w_ref[0]), axis=0, keepdims=True)
        dyl = dyv * sc_ref[...]
        dw_ref[0] = _dot(pooled, dyl, _TN)
        dpooled = _dot(dyl, w_ref[0], _NT)
        du_ref[...] = (_pool_select(g, _pool_windows(dpooled / cnt, t, l, True)) - dpooled).astype(BF16)

    return pl.pallas_call(
        body, name="pool_bwd", grid=(4,), in_specs=_pool_specs(l) + [pl.BlockSpec((l, POOL_GW), lambda g: (0, g))],
        out_specs=[pl.BlockSpec((l, POOL_GW), lambda g: (0, g)), pl.BlockSpec((1, POOL_GW, POOL_GW), lambda g: (g, 0, 0)),
                   pl.BlockSpec((1, POOL_GW), lambda g: (0, g))],
        out_shape=[jax.ShapeDtypeStruct((l, POOL_W), BF16), jax.ShapeDtypeStruct((4, POOL_GW, POOL_GW), F32),
                   jax.ShapeDtypeStruct((1, POOL_W), F32)],
        compiler_params=_params(("parallel",)))(proj, w_pool, scale, dy)


ROWS_MERGE = 256
VMEM_LIMIT_MERGE = 60 * 1024 * 1024
ROWS_MERGE_FWD = 512


def _merge_inputs(tr):
    def col(off, w):
        return pl.BlockSpec((tr, w), lambda i: (i, off // w))

    def act(w):
        return pl.BlockSpec((tr, w), lambda i: (i, 0))

    d = D_MODEL
    return ([col(OFF_GL, 3 * d), col(OFF_ZA, ATT_W), col(OFF_US, SSM_W), col(OFF_ZS, SSM_W), col(OFF_ZP, POOL_W),
             act(ATT_W), act(SSM_W), act(POOL_W)]
            + [_full((1, SSM_W)), _full((SSM_W, SSM_W)), _full((1, SSM_W)), _full((ATT_W, d)), _full((SSM_W, d)),
               _full((POOL_W, d)), _full((d, d)), _full((1, 3 * d))])


def _branch_math(za_ref, us_ref, zs_ref, zp_ref, o_ref, yl_ref, yp_ref, d_ref, wg_ref, bg_ref):
    r = {}
    r['sa'], r['dsa'] = _silu_and_grad(za_ref[...])
    r['ss'], r['dss'] = _silu_and_grad(zs_ref[...])
    r['sp'], r['dsp'] = _silu_and_grad(zp_ref[...])
    r['y_att'] = o_ref[...] * r['sa']
    y1 = yl_ref[...] + d_ref[...] * us_ref[...]
    r['y2'], r['dgelu'] = _gelu_and_grad(y1)
    r['sg'] = _sigmoid(_dot(r['y2'], wg_ref[...]) + bg_ref[...])
    r['y3'] = r['y2'] * r['sg']
    r['y_ssm'] = r['y3'] * r['ss']
    r['y_pool'] = yp_ref[...] * r['sp']
    return r


def _merge_fwd(proj, o, y_lin, y_pool, x, ssm_d, w_glu, b_glu, wba, wbs, wbp, w_out, mod, sends=None):
    l, d = x.shape
    tr = min(ROWS_MERGE_FWD, l)

    def body(gl_ref, za_ref, us_ref, zs_ref, zp_ref, o_ref, yl_ref, yp_ref, d_ref, wg_ref, bg_ref, wba_ref, wbs_ref,
             wbp_ref, wo_ref, mod_ref, x_ref, xn_ref):
        r = _branch_math(za_ref, us_ref, zs_ref, zp_ref, o_ref, yl_ref, yp_ref, d_ref, wg_ref, bg_ref)
        branches = ((r['y_att'], wba_ref), (r['y_ssm'], wbs_ref), (r['y_pool'], wbp_ref))
        out = jnp.zeros((tr, d), F32)
        for cols in (slice(0, d // 2), slice(d // 2, d)):
            merged = None
            for b, (y, w_ref) in enumerate(branches):
                term = _sigmoid(gl_ref[:, b * d + cols.start:b * d + cols.stop]) * _dot(y, w_ref[:, cols])
                merged = term if merged is None else merged + term
            out = out + _dot(merged, wo_ref[cols, :])
        xn_ref[...] = x_ref[...] + mod_ref[:, 2 * d:3 * d] * out

    blk = pl.BlockSpec((tr, d), lambda i: (i, 0))
    in_specs = _merge_inputs(tr) + [blk]
    body, x_in, x_out, x_shapes, x_scratch, x_args = _carry_sends(body, len(in_specs), 1, 0, (l // tr,), sends)
    outs = pl.pallas_call(
        body, name="merge_fwd" if sends is None else "merge_fwd_sending", grid=(l // tr,), in_specs=in_specs + x_in,
        out_specs=[blk] + x_out, out_shape=[jax.ShapeDtypeStruct((l, d), F32)] + x_shapes, scratch_shapes=x_scratch,
        compiler_params=_params(("arbitrary",), VMEM_LIMIT_MERGE))(
            proj, proj, proj, proj, proj, o, y_lin, y_pool, ssm_d, w_glu, b_glu, wba, wbs, wbp, w_out, mod, x, *x_args)
    return outs[0], list(outs[1:])


def _merge_bwd(proj, o, y_lin, y_pool, dxo, ssm_d, w_glu, b_glu, wba, wbs, wbp, w_out, mod, sends=None):
    l, d = dxo.shape
    tr = min(ROWS_MERGE, l)

    def body(gl_ref, za_ref, us_ref, zs_ref, zp_ref, o_ref, yl_ref, yp_ref, d_ref, wg_ref, bg_ref, wba_ref, wbs_ref,
             wbp_ref, wo_ref, mod_ref, dxo_ref,
             do_ref, dyl_ref, dus_ref, dyp_ref, dpj_ref,
             ya_ref, ys_ref, ypl_ref, y2_ref, dt_ref, dpa_ref, dps_ref, dpp_ref, mg_ref, dout_ref, st_ref):
        @pl.when(pl.program_id(0) == 0)
        def _():
            st_ref[...] = jnp.zeros_like(st_ref)

        for off, width in ((OFF_Q, ATT_W), (OFF_US, SSM_W), (OFF_UP, POOL_W), (OFF_KV, 2 * KV_W)):
            dpj_ref[:, off:off + width] = jnp.zeros((tr, width), BF16)

        r = _branch_math(za_ref, us_ref, zs_ref, zp_ref, o_ref, yl_ref, yp_ref, d_ref, wg_ref, bg_ref)
        dxov = dxo_ref[...]
        dout = dxov * mod_ref[:, 2 * d:3 * d]
        dout_ref[...] = dout.astype(BF16)
        branches = ((r['y_att'], wba_ref, dpa_ref), (r['y_ssm'], wbs_ref, dps_ref), (r['y_pool'], wbp_ref, dpp_ref))
        out = jnp.zeros((tr, d), F32)
        dys = [jnp.zeros((tr, ATT_W), F32) for _ in branches]
        for cols in (slice(0, d // 2), slice(d // 2, d)):
            gates = [_sigmoid(gl_ref[:, b * d + cols.start:b * d + cols.stop]) for b in range(3)]
            projected = [_dot(y, w_ref[:, cols]) for y, w_ref, _ in branches]
            merged = gates[0] * projected[0] + gates[1] * projected[1] + gates[2] * projected[2]
            mg_ref[:, cols] = merged.astype(BF16)
            out = out + _dot(merged, wo_ref[cols, :])
            dmerged = _dot(dout, wo_ref[cols, :], _NT)
            for b, (_, w_ref, dp_ref) in enumerate(branches):
                dp = dmerged * gates[b]
                dpj_ref[:, OFF_GL + b * d + cols.start:OFF_GL + b * d + cols.stop] = (
                    dp * projected[b] * (1.0 - gates[b])).astype(BF16)
                dp_ref[:, cols] = dp.astype(BF16)
                dys[b] = dys[b] + _dot(dp, w_ref[:, cols], _NT)
        st_ref[0:1, :] += jnp.sum(dxov * out, axis=0, keepdims=True)
        ya_ref[...] = r['y_att'].astype(BF16)
        ys_ref[...] = r['y_ssm'].astype(BF16)
        ypl_ref[...] = r['y_pool'].astype(BF16)
        y2_ref[...] = r['y2'].astype(BF16)
        do_ref[...] = (dys[0] * r['sa']).astype(BF16)
        dpj_ref[:, OFF_ZA:OFF_ZA + ATT_W] = (dys[0] * o_ref[...] * r['dsa']).astype(BF16)
        dy3 = dys[1] * r['ss']
        dpj_ref[:, OFF_ZS:OFF_ZS + SSM_W] = (dys[1] * r['y3'] * r['dss']).astype(BF16)
        dt = dy3 * r['y3'] * (1.0 - r['sg'])
        dt_ref[...] = dt.astype(BF16)
        dy1 = (dy3 * r['sg'] + _dot(dt, wg_ref[...], _NT)) * r['dgelu']
        dyl_ref[...] = dy1.astype(BF16)
        dus_ref[...] = dy1 * d_ref[...]
        st_ref[1:2, 0:SSM_W] += jnp.sum(dt, axis=0, keepdims=True)
        st_ref[1:2, SSM_W:2 * SSM_W] += jnp.sum(dy1 * us_ref[...], axis=0, keepdims=True)
        dyp_ref[...] = dys[2] * r['sp']
        dpj_ref[:, OFF_ZP:OFF_ZP + POOL_W] = (dys[2] * yp_ref[...] * r['dsp']).astype(BF16)

    blk = pl.BlockSpec((tr, d), lambda i: (i, 0))
    half = pl.BlockSpec((tr, ATT_W), lambda i: (i, 0))
    wide = pl.BlockSpec((tr, IN_W), lambda i: (i, 0))
    sds = jax.ShapeDtypeStruct
    in_specs = _merge_inputs(tr) + [blk]
    out_specs = [half] * 4 + [wide] + [half] * 5 + [blk] * 5 + [_full((SUBLANES, d))]
    body, x_in, x_out, x_shapes, x_scratch, x_args = _carry_sends(body, len(in_specs), len(out_specs), 0, (l // tr,), sends)
    outs = pl.pallas_call(
        body, name="merge_bwd" if sends is None else "merge_bwd_sending", grid=(l // tr,), in_specs=in_specs + x_in,
        out_specs=out_specs + x_out,
        out_shape=[sds((l, ATT_W), t) for t in (BF16, BF16, F32, F32)] + [sds((l, IN_W), BF16)]
        + [sds((l, ATT_W), BF16)] * 5 + [sds((l, d), BF16)] * 5 + [sds((SUBLANES, d), F32)] + x_shapes,
        scratch_shapes=x_scratch, compiler_params=_params(("arbitrary",), VMEM_LIMIT_MERGE))(
            proj, proj, proj, proj, proj, o, y_lin, y_pool, ssm_d, w_glu, b_glu, wba, wbs, wbp, w_out, mod, dxo, *x_args)
    return list(outs[:len(out_specs)]), list(outs[len(out_specs):])


def _ada_fwd(c_all, w_ada, b_shard):
    depth, d, n = w_ada.shape

    def body(c_ref, w_ref, b_ref, o_ref):
        act, _ = _silu_and_grad(c_ref[...])
        o_ref[0] = _dot(act, w_ref[0]) + b_ref[0]

    return pl.pallas_call(
        body, name="ada_fwd", grid=(depth,),
        in_specs=[_full((N_DEV, d)), pl.BlockSpec((1, d, n), lambda i: (i, 0, 0)), pl.BlockSpec((1, 1, n), lambda i: (i, 0, 0))],
        out_specs=pl.BlockSpec((1, N_DEV, n), lambda i: (i, 0, 0)), out_shape=jax.ShapeDtypeStruct((depth, N_DEV, n), F32),
        compiler_params=_params(("parallel",)))(c_all, w_ada, b_shard)


def _ada_bwd(c_all_t, dmod):
    d = c_all_t.shape[0]
    depth, _, n = dmod.shape

    def body(c_ref, dm_ref, o_ref):
        act, _ = _silu_and_grad(c_ref[...])
        acc = act[:, 0:1] * dm_ref[0, 0:1, :]
        for b in range(1, N_DEV):
            acc = acc + act[:, b:b + 1] * dm_ref[0, b:b + 1, :]
        o_ref[0] = acc

    return pl.pallas_call(
        body, name="ada_bwd", grid=(depth,),
        in_specs=[_full((d, N_DEV)), pl.BlockSpec((1, N_DEV, n), lambda i: (i, 0, 0))],
        out_specs=pl.BlockSpec((1, d, n), lambda i: (i, 0, 0)), out_shape=jax.ShapeDtypeStruct((depth, d, n), F32),
        compiler_params=_params(("parallel",)))(c_all_t, dmod)


ROWS_FLAT = 256


_RELATION_XOR = (0, 2, 1, 3)


def _sum_chips(own, others, chip_order, name):
    r, c = own.shape
    tr = math.gcd(ROWS_FLAT, r)

    def body(own_ref, oth_ref, o_ref):
        terms = [own_ref[...].astype(F32)] + [oth_ref[k].astype(F32) for k in range(3)]
        if chip_order:
            chip = 2 * lax.axis_index("x") + lax.axis_index("y")
            by_chip = []
            for q in range(N_CHIPS):
                rel = jnp.bitwise_xor(chip, q)
                pick = terms[3]
                for k in (2, 1, 0):
                    pick = jnp.where(rel == _RELATION_XOR[k], terms[k], pick)
                by_chip.append(pick)
            terms = by_chip
        o_ref[...] = ((terms[0] + terms[1]) + terms[2]) + terms[3]

    return pl.pallas_call(
        body, name=name, grid=(r // tr,),
        in_specs=[pl.BlockSpec((tr, c), lambda i: (i, 0)), pl.BlockSpec((3, tr, c), lambda i: (0, i, 0))],
        out_specs=pl.BlockSpec((tr, c), lambda i: (i, 0)), out_shape=jax.ShapeDtypeStruct((r, c), F32),
        compiler_params=_params(("parallel",)))(own, others)


def _add2(a, b, out_dtype, name):
    shape = a.shape
    a, b = a.reshape(-1, shape[-1]), b.reshape(-1, shape[-1])
    r, c = a.shape
    tr = math.gcd(ROWS_FLAT, r)

    def body(a_ref, b_ref, o_ref):
        o_ref[...] = (a_ref[...] + b_ref[...]).astype(out_dtype)

    blk = pl.BlockSpec((tr, c), lambda i: (i, 0))
    return pl.pallas_call(
        body, name=name, grid=(r // tr,), in_specs=[blk, blk], out_specs=blk,
        out_shape=jax.ShapeDtypeStruct((r, c), out_dtype), compiler_params=_params(("parallel",)))(a, b).reshape(shape)


def _adamw(w, g, m, v, name):
    r, c = w.shape
    tr = math.gcd(ROWS_FLAT, r)

    def body(w_ref, g_ref, m_ref, v_ref, d_ref, nm_ref, nv_ref):
        gv = g_ref[...]
        mv = ADAM_B1 * m_ref[...] + (1.0 - ADAM_B1) * gv
        vv = ADAM_B2 * v_ref[...] + (1.0 - ADAM_B2) * (gv * gv)
        m_hat = mv / (1.0 - ADAM_B1 ** ADAM_STEP)
        v_hat = vv / (1.0 - ADAM_B2 ** ADAM_STEP)
        d_ref[...] = -ADAM_LR * (m_hat / (jnp.sqrt(v_hat) + ADAM_EPS) + ADAM_WD * w_ref[...])
        nm_ref[...] = mv
        nv_ref[...] = vv

    blk = pl.BlockSpec((tr, c), lambda i: (i, 0))
    return pl.pallas_call(
        body, name=name, grid=(r // tr,), in_specs=[blk] * 4, out_specs=[blk] * 3,
        out_shape=[jax.ShapeDtypeStruct((r, c), F32)] * 3, compiler_params=_params(("parallel",)))(w, g, m, v)


_GROUP_MASKS = {
    'xy': ((1, 0, 0), (0, 1, 0), (1, 1, 0)),
    'xyc': ((0, 0, 1), (0, 1, 0), (0, 1, 1), (1, 0, 0), (1, 0, 1), (1, 1, 0), (1, 1, 1)),
}


def _group_rank(group, pos):
    x, y, c = pos
    return {'xy': 2 * x + y, 'xyc': 4 * x + 2 * y + c}[group]


def _gather(src, group, name):
    masks = _GROUP_MASKS[group]
    n = len(masks) + 1

    def body(src_ref, out_ref, send_sems, recv_sems, local_sem):
        me = (lax.axis_index("x"), lax.axis_index("y"), lax.axis_index("c"))
        my_rank = _group_rank(group, me)
        local = pltpu.make_async_copy(src_ref, out_ref.at[my_rank], local_sem)
        local.start()
        sends = []
        for k, mask in enumerate(masks):
            peer = tuple(1 - p if f else p for p, f in zip(me, mask))
            send = _remote(src_ref, out_ref.at[my_rank], send_sems.at[k], recv_sems.at[k], peer)
            send.start()
            sends.append((send, peer))
        for k, (send, peer) in enumerate(sends):
            _remote(src_ref, out_ref.at[_group_rank(group, peer)], send_sems.at[k], recv_sems.at[k], peer).wait_recv()
        for send, _ in sends:
            send.wait_send()
        local.wait()

    return pl.pallas_call(
        body, name=name, in_specs=[pl.BlockSpec(memory_space=pl.ANY)], out_specs=pl.BlockSpec(memory_space=pl.ANY),
        out_shape=jax.ShapeDtypeStruct((n,) + tuple(src.shape), src.dtype),
        scratch_shapes=[pltpu.SemaphoreType.DMA((n - 1,)), pltpu.SemaphoreType.DMA((n - 1,)), pltpu.SemaphoreType.DMA(())],
    )(src)


CHUNK_BYTES = 1 << 20
MIN_CHUNK_ROWS = 64


def _row_parts(rows, cols, itemsize):
    n = 1
    while rows % (2 * n) == 0 and rows // (2 * n) >= MIN_CHUNK_ROWS and rows * cols * itemsize > n * CHUNK_BYTES:
        n *= 2
    return n


def _remote(src, dst, send_sem, recv_sem, peer):
    return pltpu.make_async_remote_copy(src_ref=src, dst_ref=dst, send_sem=send_sem, recv_sem=recv_sem,
                                        device_id=peer, device_id_type=pl.DeviceIdType.MESH)


def _start_rows(src, dst, send_sem, recv_sem, peer):
    rows, cols = src.shape
    n = _row_parts(rows, cols, jnp.dtype(src.dtype).itemsize)
    pr = rows // n
    for i in range(n):
        _remote(src.at[pl.ds(i * pr, pr), :], dst.at[pl.ds(i * pr, pr), :], send_sem, recv_sem, peer).start()


def _mesh_place():
    x, y, c = lax.axis_index("x"), lax.axis_index("y"), lax.axis_index("c")
    other_chips = ((1 - x, y), (x, 1 - y), (1 - x, 1 - y))
    return x, y, c, 2 * x + y, (x, y, 1 - c), other_chips


def _comm_call(body, name, ins, out_shapes, sem_counts):
    any_spec = pl.BlockSpec(memory_space=pl.ANY)
    return pl.pallas_call(
        body, name=name, in_specs=[any_spec] * len(ins), out_specs=[any_spec] * len(out_shapes), out_shape=out_shapes,
        scratch_shapes=[pltpu.SemaphoreType.DMA((n,)) for n in sem_counts])(*ins)


class _Sends:
    sem_arrays = 2

    def run(self, name):
        k_in, k_out = len(self.srcs), len(self.out_shapes)

        def body(*refs):
            ins, outs, sems = refs[:k_in], refs[k_in:k_in + k_out], refs[k_in + k_out:]
            self.start(ins, outs, *sems)
            self.finish(ins, outs, *sems)

        return _comm_call(body, name, self.srcs, self.out_shapes, (self.n_sems,) * self.sem_arrays)

    def specs(self):
        any_spec = pl.BlockSpec(memory_space=pl.ANY)
        return ([any_spec] * len(self.srcs), [any_spec] * len(self.out_shapes),
                [pltpu.SemaphoreType.DMA((self.n_sems,))] * self.sem_arrays)


class _SiblingSends(_Sends):
    def __init__(self, srcs):
        self.srcs = list(srcs)
        self.n_sems = len(self.srcs)
        self.out_shapes = [jax.ShapeDtypeStruct(g.shape[1:], g.dtype) for g in self.srcs]

    def start(self, ins, outs, send_sems, recv_sems):
        _, _, c, _, sibling, _ = _mesh_place()
        for t in range(len(ins)):
            for q in range(ins[t].shape[1]):
                _start_rows(ins[t].at[1 - c, q], outs[t].at[q], send_sems.at[t], recv_sems.at[t], sibling)

    def finish(self, ins, outs, send_sems, recv_sems):
        _, _, _, _, sibling, _ = _mesh_place()
        for t in range(len(ins)):
            arrived = _remote(outs[t], outs[t], send_sems.at[t], recv_sems.at[t], sibling)
            arrived.wait_recv()
            arrived.wait_send()


class _ChipSends(_Sends):
    def __init__(self, srcs, pick, pass_on=False):
        self.srcs, self.pick, self.pass_on = list(srcs), pick, pass_on
        nt = len(self.srcs)
        self.n_sems = 3 * nt
        self.sem_arrays = 4 if pass_on else 2
        landing = []
        for s in self.srcs:
            r2 = s.shape[-2] // 2 if pick is _pick_weight_half else s.shape[-2]
            landing.append(jax.ShapeDtypeStruct((3, r2, s.shape[-1]), s.dtype))
        self.out_shapes = landing + (landing if pass_on else [])

    def start(self, ins, outs, send_sems, recv_sems, *unused):
        _, _, c, _, _, other_chips = _mesh_place()
        for t in range(len(ins)):
            for j, (px, py) in enumerate(other_chips):
                _remote(self.pick(ins[t], c, 2 * px + py), outs[t].at[j], send_sems.at[3 * t + j], recv_sems.at[3 * t + j],
                        (px, py, c)).start()

    def finish(self, ins, outs, send_sems, recv_sems, *sibling_sems):
        _, _, c, _, sibling, other_chips = _mesh_place()
        nt = len(ins)
        for t in range(nt):
            for j, (px, py) in enumerate(other_chips):
                k = 3 * t + j
                landed = _remote(outs[t].at[j], outs[t].at[j], send_sems.at[k], recv_sems.at[k], (px, py, c))
                landed.wait_recv()
                if self.pass_on:
                    _start_rows(outs[t].at[j], outs[nt + t].at[j], sibling_sems[0].at[k], sibling_sems[1].at[k], sibling)
                landed.wait_send()
        if self.pass_on:
            for t in range(nt):
                for j in range(3):
                    k = 3 * t + j
                    passed = _remote(outs[nt + t].at[j], outs[nt + t].at[j], sibling_sems[0].at[k], sibling_sems[1].at[k],
                                     sibling)
                    passed.wait_recv()
                    passed.wait_send()


def _pick_weight_half(src, core, chip):
    r2 = src.shape[0] // 2
    return src.at[pl.ds(pl.multiple_of(core * r2, MIN_CHUNK_ROWS), r2), :]


def _pick_slab(src, core, chip):
    return src.at[chip if src.shape[0] == N_CHIPS else 0]


def _swap_cores(halves):
    nt = len(halves)

    def body(*refs):
        ins, outs = refs[:nt], refs[nt:2 * nt]
        send_sems, recv_sems = refs[2 * nt:]
        _, _, _, _, sibling, _ = _mesh_place()
        for t in range(nt):
            for li in range(ins[t].shape[0]):
                _start_rows(ins[t].at[li], outs[t].at[li], send_sems.at[t], recv_sems.at[t], sibling)
        for t in range(nt):
            _remote(outs[t], outs[t], send_sems.at[t], recv_sems.at[t], sibling).wait_recv()
        for t in range(nt):
            _remote(ins[t], ins[t], send_sems.at[t], recv_sems.at[t], sibling).wait_send()

    out_shapes = [jax.ShapeDtypeStruct(h.shape, h.dtype) for h in halves]
    return _comm_call(body, "swap_cores", halves, out_shapes, (nt, nt))


def _pack_rows(pieces, cols, dtype):
    tile = SUBLANES * cols
    rows = []
    for p in pieces:
        flat = p.reshape(-1).astype(dtype)
        rows.append(jnp.pad(flat, (0, (-flat.shape[0]) % tile)).reshape(-1, cols))
    total = sum(r.shape[0] for r in rows)
    if total % (2 * SUBLANES):
        rows.append(jnp.zeros((SUBLANES, cols), dtype))
    return jnp.concatenate(rows, axis=0)


def _unpack_rows(buf, shapes):
    out, row = [], 0
    for s in shapes:
        n = math.prod(s)
        nrows = -(-n // (SUBLANES * buf.shape[1])) * SUBLANES
        out.append(buf[row:row + nrows].reshape(-1)[:n].reshape(s))
        row += nrows
    return out


ROW_SHARDED = ('w_out', 'w_glu')


def _chips_to_full(name, chips):
    mats = [c.reshape(-1, c.shape[-1]) for c in chips]
    if name in ROW_SHARDED:
        return jnp.concatenate(mats, axis=0)
    if name != 'w_in':
        return jnp.concatenate(mats, axis=1)
    cs = mats[0].shape[1]
    pieces = []
    for start, size in _PERM_PIECES:
        lo = start
        while lo < start + size:
            q = lo // cs
            hi = min(start + size, (q + 1) * cs)
            pieces.append(mats[q][:, lo - q * cs:hi - q * cs])
            lo = hi
    return jnp.concatenate(pieces, axis=1)


def _full_to_dests(name, full):
    rows, cols = full.shape
    if name in ROW_SHARDED:
        return jnp.transpose(full.reshape(N_CHIPS, 2, rows // (2 * N_CHIPS), cols), (1, 0, 2, 3))
    cs = cols // N_CHIPS
    if name != 'w_in':
        dests = [full[:, q * cs:(q + 1) * cs] for q in range(N_CHIPS)]
    else:
        offsets, off = [], 0
        for _, size in _PERM_PIECES:
            offsets.append(off)
            off += size
        by_start = sorted(zip(_PERM_PIECES, offsets))
        dests = []
        for q in range(N_CHIPS):
            parts = []
            for (start, size), at in by_start:
                lo, hi = max(start, q * cs), min(start + size, (q + 1) * cs)
                if lo < hi:
                    parts.append(full[:, at + lo - start:at + hi - start])
            dests.append(jnp.concatenate(parts, axis=1))
    return jnp.stack([d.reshape(2, rows // 2, cs) for d in dests], axis=1)


def _by_chip(own, related, x, y):
    grid = ((own, related[1]), (related[0], related[2]))
    along_x = [[jnp.where(x == 0, grid[px][dy], grid[1 - px][dy]) for dy in range(2)] for px in range(2)]
    return [jnp.where(y == 0, along_x[px][py], along_x[px][1 - py]) for px in range(2) for py in range(2)]


def _join_halves(mine, theirs, core, axis):
    return jnp.where(core == 0, jnp.concatenate([mine, theirs], axis=axis), jnp.concatenate([theirs, mine], axis=axis))


ROWS_WGRAD = 2048
SSM_RAW = ('ssm_a_re', 'ssm_a_im', 'ssm_log_dt', 'ssm_b_re', 'ssm_b_im', 'ssm_c_re', 'ssm_c_im')


def _all_tables(p):
    raw = tuple(p[k] for k in SSM_RAW)
    (_, bb, cb), vjp = jax.vjp(jax.vmap(_ssm_tables), *raw)
    return (bb.astype(BF16), cb.astype(BF16), jax.vmap(_ssm_powers)(*raw[:3])), vjp


def _layer_fwd(x, p, w_in, rest, li, mod, tables, sends=None, late_sends=None):
    bb, cb, pw = (t[li] for t in tables)
    h = _norm_fwd(x, _row(p['norm_g'][li]), mod)
    if isinstance(rest, dict):
        proj = _matmul(h, w_in, 'nn', tm=1024, tn=1280, tk=1024, name="proj", n_outer=True)
    else:
        proj, arrived = _matmul(h, w_in, 'nn', tm=1024, tn=1280, tk=1024, name="proj_sending", n_outer=True, sends=rest[0])
        rest = rest[1](arrived)
    full = dict(rest, w_in=w_in)
    o = _attention_fwd(proj, p['attn_sinks'][li])
    y_lin, states, brought = _ssm_fwd(proj, bb, cb, pw, sends)
    y_pool = _pool_fwd(proj, p['w_pool'][li].astype(BF16), _row(p['pool_scale'][li]))
    x_new, brought_late = _merge_fwd(proj, o, y_lin, y_pool, x, _row(p['ssm_d'][li]), full['w_glu'],
                                     _row(p['b_glu'][li]), full['w_br_att'], full['w_br_ssm'], full['w_br_pool'],
                                     full['w_out'], mod, late_sends)
    saved = dict(x=x, h=h, proj=proj, o=o, y_lin=y_lin, y_pool=y_pool, states=states, bb=bb, cb=cb, pw=pw, full=full)
    a, b = len(brought) // 2, len(brought_late) // 2
    return x_new, saved, brought[:a] + brought_late[:b] + brought[a:] + brought_late[b:]


def _layer_bwd(dxo, s, p, li, mod, core, upper=None, own_early=(), w_in_sender=None):
    l = dxo.shape[0]
    proj, full = s['proj'], s['full']
    outs, arrived_upper = _merge_bwd(
        proj, s['o'], s['y_lin'], s['y_pool'], dxo, _row(p['ssm_d'][li]), full['w_glu'], _row(p['b_glu'][li]),
        full['w_br_att'], full['w_br_ssm'], full['w_br_pool'], full['w_out'], mod,
        None if upper is None else _ChipSends(upper, _pick_slab))
    (do, dyl, dus_skip, dyp, dproj, ya, ys, ypl, y2, dt, dpa, dps, dpp, mg, dout, st) = outs
    g = {}
    g['w_br_att'] = _matmul(ya, dpa, 'tn', tm=512, tn=1024, tk=ROWS_WGRAD, name="grad_w_br")
    g['w_br_ssm'] = _matmul(ys, dps, 'tn', tm=512, tn=1024, tk=ROWS_WGRAD, name="grad_w_br")
    g['w_br_pool'] = _matmul(ypl, dpp, 'tn', tm=512, tn=1024, tk=ROWS_WGRAD, name="grad_w_br")
    g['w_out'] = _matmul(mg, dout, 'tn', tm=512, tn=1024, tk=ROWS_WGRAD, name="grad_w_out")
    g['w_glu'] = _matmul(y2, dt, 'tn', tm=512, tn=512, tk=ROWS_WGRAD, name="grad_w_glu")
    g['b_glu'] = st[1, 0:SSM_W]
    g['ssm_d'] = st[1, SSM_W:2 * SSM_W]
    dgate = st[0]

    early = _layer_grad_halves(g, own_early)
    dq, dkv, dsink, theirs = _attention_bwd(proj, p['attn_sinks'][li], do, _SiblingSends(early) if early else None)
    g['attn_sinks'] = dsink[:, 0]
    pair = _pair_sums(early, [BF16] * len(early), core, theirs) if early else []
    sends = _ChipSends(pair, _pick_slab) if pair else None
    dus_scan, dbb, dcb, dlam, brought = _ssm_bwd(proj, s['states'], dyl, s['bb'], s['cb'], s['pw'], sends)
    g['ssm_tables'] = (dlam[:, 0:2, :], dbb, dcb)
    dup, dwp, dps_scale = _pool_bwd(proj, p['w_pool'][li].astype(BF16), _row(p['pool_scale'][li]), dyp)
    g['w_pool'] = dwp
    g['pool_scale'] = dps_scale[0]

    for off, piece in ((OFF_Q, dq), (OFF_US, dus_skip + dus_scan), (OFF_UP, dup), (OFF_KV, dkv[HALO:])):
        dproj = lax.dynamic_update_slice(dproj, piece.astype(BF16), (0, off))
    g['w_in'] = _matmul(s['h'], dproj, 'tn', tm=1024, tn=1280, tk=ROWS_WGRAD, name="grad_w_in")
    if w_in_sender is None:
        dh, brought_late = _matmul(dproj, full['w_in'], 'nt', tm=1024, tn=1024, tk=3200, name="grad_h"), None
    else:
        dh, brought_late = _matmul(dproj, full['w_in'], 'nt', tm=1024, tn=1024, tk=3200, name="grad_h_sending",
                                   sends=w_in_sender(g['w_in']))
    dx, nst = _norm_bwd(s['x'], dh, dxo, _row(p['norm_g'][li]), mod)
    g['norm_g'] = nst[2]
    dmod = jnp.concatenate([nst[0], nst[1], dgate])
    del l
    return dx, g, dmod, (pair, brought), arrived_upper, brought_late


BIG_NAMES = tuple(name for name, _ in BIG)


def _weight_sends(shards, names, li):
    return _ChipSends([shards[name][li] for name in names], _pick_weight_half, pass_on=True)


def _assemble_layer(shards, names, li, arrived, place):
    x, y, core, _ = place
    over_ici, from_sibling = arrived[:len(names)], arrived[len(names):]
    full = {}
    for name, a, b in zip(names, over_ici, from_sibling):
        own = shards[name][li]
        own_halves = own.reshape(2, own.shape[0] // 2, own.shape[1])
        halves = [_join_halves(a[j][None], b[j][None], core, 0) for j in range(3)]
        full[name] = _chips_to_full(name, _by_chip(own_halves, halves, x, y))
    return full


def _pair_sums(tensors, wire, core, theirs=None):
    if theirs is None:
        theirs = _SiblingSends(tensors).run("swap_halves")
    return [_add2(lax.dynamic_index_in_dim(g, core, 0, keepdims=False), a, dt, "sum_core_pair")
            for g, a, dt in zip(tensors, theirs, wire)]


def _layer_grad_halves(layer_grads, names):
    return [_full_to_dests(name, layer_grads[name]) for name in names]


def _step(p, m, v, x, c, target):
    depth = p['norm_g'].shape[0]
    d = D_MODEL
    ix, iy, ic = lax.axis_index("x"), lax.axis_index("y"), lax.axis_index("c")
    chip = 2 * ix + iy
    dev = 4 * ix + 2 * iy + ic
    x0 = x[0]

    c_pad = jnp.pad(c, ((0, SUBLANES - 1), (0, 0)))
    c_all = _gather(c_pad, 'xyc', "gather_c")[:, 0, :]
    n_ada = p['w_ada'].shape[-1]
    b_shard = lax.dynamic_slice_in_dim(p['b_ada'], chip * n_ada, n_ada, axis=1)[:, None, :]
    mod_shard = _ada_fwd(c_all, p['w_ada'].astype(BF16), b_shard)
    mod_all = _gather(mod_shard.reshape(depth * N_DEV, n_ada), 'xy', "gather_mod")
    mod_all = jnp.transpose(mod_all.reshape(N_CHIPS, depth, N_DEV, n_ada), (1, 2, 0, 3)).reshape(depth, N_DEV, 3 * d)
    mods = lax.dynamic_index_in_dim(mod_all, dev, axis=1, keepdims=True)

    place = (ix, iy, ic, chip)
    n_big = len(BIG)

    shards = {name: p[name].astype(BF16) for name, _ in BIG}
    tables, tables_vjp = _all_tables(p)
    others = BIG_NAMES[1:]
    w_in = _assemble_layer(shards, BIG_NAMES[:1], 0, _weight_sends(shards, BIG_NAMES[:1], 0).run("send_weight_halves"),
                           place)['w_in']
    rest = (_weight_sends(shards, others, 0), lambda arrived: _assemble_layer(shards, others, 0, arrived, place))
    saved = []
    xs = x0
    for li in range(depth):
        more = li + 1 < depth
        xs, s, arrived = _layer_fwd(xs, p, w_in, rest, li, mods[li], tables,
                                    _weight_sends(shards, BIG_NAMES[:1], li + 1) if more else None,
                                    _weight_sends(shards, others, li + 1) if more else None)
        saved.append(s)
        if more:
            rest = _assemble_layer(shards, BIG_NAMES, li + 1, arrived, place)
            w_in = rest.pop('w_in')
    dx, hst = _loss_head(xs, _row(p['final_g']), target[0])
    loss = lax.psum(hst[1, 0], ("x", "y", "c"))

    grads, dmods, pair, from_chips = [None] * depth, [None] * depth, [None] * depth, [None] * depth
    w_in_halves, w_in_pair, w_in_arrived = {}, {}, {}

    def w_in_sender(li):
        def to_chips(g_w_in):
            w_in_pair[li] = _pair_sums([_full_to_dests('w_in', g_w_in)], [BF16], ic)
            return _ChipSends(w_in_pair[li], _pick_slab)

        def to_sibling(g_w_in):
            w_in_halves[li] = [_full_to_dests('w_in', g_w_in)]
            return _SiblingSends(w_in_halves[li])

        return to_chips if li == 0 else to_sibling

    upper = None
    for li in reversed(range(depth)):
        dx, grads[li], dmods[li], early, arrived_upper, arrived_late = _layer_bwd(
            dx, saved[li], p, li, mods[li], ic, upper, others, w_in_sender(li))
        if upper is not None:
            w_in_arrived[li + 1] = arrived_upper
        if li > 0:
            w_in_pair[li] = _pair_sums(w_in_halves[li], [BF16], ic, arrived_late)
            upper = w_in_pair[li]
        else:
            w_in_arrived[li] = arrived_late
        pair[li] = w_in_pair[li] + early[0]
        from_chips[li] = early[1]
    for li in range(depth):
        from_chips[li] = list(w_in_arrived[li]) + list(from_chips[li])

    dmod_pad = jnp.pad(jnp.stack(dmods), ((0, SUBLANES - depth), (0, 0)))
    dmod_all = _gather(dmod_pad, 'xyc', "gather_dmod")[:, :depth, :]
    dmod_cols = lax.dynamic_slice_in_dim(jnp.transpose(dmod_all, (1, 0, 2)), chip * n_ada, n_ada, axis=2)
    g_w_ada = _ada_bwd(jnp.transpose(c_all), dmod_cols)

    local_small = {k: jnp.stack([grads[li][k] for li in range(depth)])
                   for k in SMALL if k not in ('final_g', 'b_ada') + SSM_RAW}
    table_cotangents = tuple(jnp.stack([grads[li]['ssm_tables'][k] for li in range(depth)]) for k in range(3))
    local_small.update(zip(SSM_RAW, tables_vjp(table_cotangents)))
    local_small['final_g'] = hst[0]
    local_small['b_ada'] = jnp.stack(dmods)
    small_pack = _pack_rows([local_small[k] for k in SMALL], PACK_COLS, F32)

    small_halves = small_pack.reshape(2, 1, small_pack.shape[0] // 2, small_pack.shape[1])
    small_pair = _pair_sums([small_halves], [F32], ic)
    pair[0] = pair[0] + small_pair
    from_chips[0] = from_chips[0] + list(_ChipSends(small_pair, _pick_slab).run("scatter_chips"))
    halves = []
    for li in range(depth):
        for t, (ps, others) in enumerate(zip(pair[li], from_chips[li])):
            small = t == n_big
            own = ps[0] if small else lax.dynamic_index_in_dim(ps, chip, 0, keepdims=False)
            halves.append(_sum_chips(own, others, small, "sum_chips_small" if small else "sum_chips")[None])
    totals = [_join_halves(h[0], o[0], ic, 0) for h, o in zip(halves, _swap_cores(halves))]
    small_sum = totals[n_big]
    per_layer = [totals[:n_big]] + [totals[n_big + 1 + (li - 1) * n_big:n_big + 1 + li * n_big] for li in range(1, depth)]

    small_shapes = [p[k].shape for k in SMALL]
    grad = dict(zip(SMALL, _unpack_rows(small_sum, small_shapes)))
    grad['w_ada'] = g_w_ada
    for t, (name, _) in enumerate(BIG):
        grad[name] = jnp.stack([per_layer[li][t] for li in range(depth)])

    delta, new_m, new_v = {}, {}, {}
    outs = _adamw(_pack_rows([p[k] for k in SMALL], PACK_COLS, F32), small_sum,
                  _pack_rows([m[k] for k in SMALL], PACK_COLS, F32), _pack_rows([v[k] for k in SMALL], PACK_COLS, F32),
                  name="adamw_small")
    for res, o in zip((delta, new_m, new_v), outs):
        res.update(zip(SMALL, _unpack_rows(o, small_shapes)))
    for name in ['w_ada'] + [n for n, _ in BIG]:
        shape = p[name].shape
        two_d = (-1, shape[-1])
        outs = _adamw(p[name].reshape(two_d), grad[name].reshape(two_d), m[name].reshape(two_d), v[name].reshape(two_d),
                      name="adamw_" + name)
        delta[name], new_m[name], new_v[name] = (o.reshape(shape) for o in outs)

    return (loss, dx[None], *[grad[k] for k in WEIGHTS], *[delta[k] for k in WEIGHTS],
            *[new_m[k] for k in WEIGHTS], *[new_v[k] for k in WEIGHTS])


def kernel(x, c, norm_g, w_ada, b_ada, w_in, attn_sinks, ssm_a_re, ssm_a_im, ssm_log_dt, ssm_b_re, ssm_b_im, ssm_c_re, ssm_c_im, ssm_d, w_glu, b_glu, w_pool, pool_scale, w_br_att, w_br_ssm, w_br_pool, w_out, final_g, loss_target, m_norm_g, m_w_ada, m_b_ada, m_w_in, m_attn_sinks, m_ssm_a_re, m_ssm_a_im, m_ssm_log_dt, m_ssm_b_re, m_ssm_b_im, m_ssm_c_re, m_ssm_c_im, m_ssm_d, m_w_glu, m_b_glu, m_w_pool, m_pool_scale, m_w_br_att, m_w_br_ssm, m_w_br_pool, m_w_out, m_final_g, v_norm_g, v_w_ada, v_b_ada, v_w_in, v_attn_sinks, v_ssm_a_re, v_ssm_a_im, v_ssm_log_dt, v_ssm_b_re, v_ssm_b_im, v_ssm_c_re, v_ssm_c_im, v_ssm_d, v_w_glu, v_b_glu, v_w_pool, v_pool_scale, v_w_br_att, v_w_br_ssm, v_w_br_pool, v_w_out, v_final_g):
    p = dict(zip(WEIGHTS, (norm_g, w_ada, b_ada, w_in, attn_sinks, ssm_a_re, ssm_a_im, ssm_log_dt, ssm_b_re, ssm_b_im,
                           ssm_c_re, ssm_c_im, ssm_d, w_glu, b_glu, w_pool, pool_scale, w_br_att, w_br_ssm, w_br_pool,
                           w_out, final_g)))
    m = dict(zip(WEIGHTS, (m_norm_g, m_w_ada, m_b_ada, m_w_in, m_attn_sinks, m_ssm_a_re, m_ssm_a_im, m_ssm_log_dt,
                           m_ssm_b_re, m_ssm_b_im, m_ssm_c_re, m_ssm_c_im, m_ssm_d, m_w_glu, m_b_glu, m_w_pool,
                           m_pool_scale, m_w_br_att, m_w_br_ssm, m_w_br_pool, m_w_out, m_final_g)))
    v = dict(zip(WEIGHTS, (v_norm_g, v_w_ada, v_b_ada, v_w_in, v_attn_sinks, v_ssm_a_re, v_ssm_a_im, v_ssm_log_dt,
                           v_ssm_b_re, v_ssm_b_im, v_ssm_c_re, v_ssm_c_im, v_ssm_d, v_w_glu, v_b_glu, v_w_pool,
                           v_pool_scale, v_w_br_att, v_w_br_ssm, v_w_br_pool, v_w_out, v_final_g)))
    return _step(p, m, v, x, c, loss_target)
```
